```python
import math
import jax, jax.numpy as jnp
from jax import lax
import numpy as np

D_MODEL = 2048
BATCH = 16
SEQ = 2048
DEPTH = 2

MIX_WIDTH = D_MODEL
A_HEAD_DIM = 64
A_WIDTH = 3 * D_MODEL // 8
A_Q_HEADS = A_WIDTH // A_HEAD_DIM
A_KV_HEADS = A_Q_HEADS // 3
A_GROUP = A_Q_HEADS // A_KV_HEADS
A_KV_WIDTH = A_KV_HEADS * A_HEAD_DIM
WINDOW = 128
ATTN_BLOCK = 128
ROPE_THETA = 500000.0
ROT_DIM = A_HEAD_DIM // 4
B_WIDTH = D_MODEL // 4
B_CONV_WIDTH = 31
C_HEAD_DIM = 128
C_HEADS = (MIX_WIDTH - A_WIDTH - B_WIDTH) // C_HEAD_DIM
C_WIDTH = C_HEADS * C_HEAD_DIM
C_CONV_WIDTH = 4
CHUNK = 64

EPS = 1e-6
MAX_POS_OFFSET = 4096

COL_SPLITS = (A_WIDTH, A_KV_WIDTH, A_KV_WIDTH, A_WIDTH,
              2 * B_WIDTH, B_WIDTH,
              3 * C_WIDTH, C_HEADS, C_HEADS, C_WIDTH)
IN_COLS = sum(COL_SPLITS)

kernel_name = "hybrid_swa_conformer_gdn_parallel"


def _split_cols(p):
    idx = []
    s = 0
    for n in COL_SPLITS[:-1]:
        s += n
        idx.append(s)
    return jnp.split(p, idx, axis=-1)


def rms_norm(x, w):
    x32 = x.astype(jnp.float32)
    y = x32 * lax.rsqrt(jnp.mean(x32 * x32, axis=-1, keepdims=True) + EPS)
    return (y * w.astype(jnp.float32)).astype(x.dtype)


def layer_norm(x, w, b):
    x32 = x.astype(jnp.float32)
    mu = jnp.mean(x32, axis=-1, keepdims=True)
    var = jnp.mean(jnp.square(x32 - mu), axis=-1, keepdims=True)
    y = (x32 - mu) * lax.rsqrt(var + EPS)
    return (y * w.astype(jnp.float32) + b.astype(jnp.float32)).astype(x.dtype)


def l2_norm(x):
    return x * lax.rsqrt(jnp.sum(x * x, axis=-1, keepdims=True) + EPS)


def causal_depthwise_conv(x, w):
    K, C = w.shape
    return lax.conv_general_dilated(
        x, w[:, None, :].astype(x.dtype), window_strides=(1,), padding=[(K - 1, 0)],
        dimension_numbers=('NWC', 'WIO', 'NWC'), feature_group_count=C)


def rope_tables(positions):
    inv_freq = ROPE_THETA ** (-jnp.arange(0, ROT_DIM, 2, dtype=jnp.float32) / ROT_DIM)
    ang = positions.astype(jnp.float32)[..., None] * inv_freq
    return jnp.cos(ang)[:, :, None, :], jnp.sin(ang)[:, :, None, :]


def apply_partial_rope(x, cos, sin):
    half = ROT_DIM // 2
    x1 = x[..., :half].astype(jnp.float32)
    x2 = x[..., half:ROT_DIM].astype(jnp.float32)
    rot = jnp.concatenate([x1 * cos - x2 * sin, x2 * cos + x1 * sin], axis=-1)
    return jnp.concatenate([rot.astype(x.dtype), x[..., ROT_DIM:]], axis=-1)


def sliding_window_sink_attention(q, k, v, sinks):
    bsz, T = q.shape[0], q.shape[1]
    nb = T // ATTN_BLOCK
    qb = q.reshape(bsz, nb, ATTN_BLOCK, A_KV_HEADS, A_GROUP, A_HEAD_DIM)
    kb = k.reshape(bsz, nb, ATTN_BLOCK, A_KV_HEADS, A_HEAD_DIM)
    vb = v.reshape(bsz, nb, ATTN_BLOCK, A_KV_HEADS, A_HEAD_DIM)
    prev = lambda t: jnp.concatenate([jnp.zeros_like(t[:, :1]), t[:, :-1]], axis=1)
    kk = jnp.concatenate([prev(kb), kb], axis=2)
    vv = jnp.concatenate([prev(vb), vb], axis=2)
    s = jnp.einsum('bnqhgd,bnkhd->bnhgqk', qb, kk).astype(jnp.float32) * (A_HEAD_DIM ** -0.5)
    qi = jnp.arange(ATTN_BLOCK)[:, None]
    kj = jnp.arange(2 * ATTN_BLOCK)[None, :]
    dist = qi + ATTN_BLOCK - kj
    band = (dist >= 0) & (dist < WINDOW)
    not_pad = (jnp.arange(nb)[:, None, None] > 0) | (kj[None] >= ATTN_BLOCK)
    valid = band[None] & not_pad
    s = jnp.where(valid[None, :, None, None], s, -jnp.inf)
    sink = sinks.reshape(A_KV_HEADS, A_GROUP).astype(jnp.float32)[None, None, :, :, None, None]
    sink = jnp.broadcast_to(sink, s.shape[:-1] + (1,))
    p = jax.nn.softmax(jnp.concatenate([s, sink], axis=-1), axis=-1)[..., :-1]
    o = jnp.einsum('bnhgqk,bnkhd->bnqhgd', p.astype(v.dtype), vv)
    return o.reshape(bsz, T, A_Q_HEADS * A_HEAD_DIM)


def conformer_conv_module(u, conv_w, conv_b, ln_w, ln_b, pw_w, pw_b):
    a, gt = jnp.split(u, 2, axis=-1)
    h = a * jax.nn.sigmoid(gt)
    h = causal_depthwise_conv(h, conv_w) + conv_b.astype(h.dtype)
    h = jax.nn.silu(layer_norm(h, ln_w, ln_b))
    return h @ pw_w + pw_b


def gated_delta_rule_chunked(q, k, v, g, beta):
    bsz, T, H, Dk = q.shape
    Dv = v.shape[-1]
    n = T // CHUNK
    q = q * (Dk ** -0.5)

    def to_chunks(t):
        return t.reshape(bsz, n, CHUNK, H, t.shape[-1]).transpose(0, 1, 3, 2, 4)

    qc, kc, vc = to_chunks(q), to_chunks(k), to_chunks(v)
    bc = beta.reshape(bsz, n, CHUNK, H).transpose(0, 1, 3, 2)
    gc = jnp.cumsum(g.reshape(bsz, n, CHUNK, H).transpose(0, 1, 3, 2), axis=-1)
    idx = jnp.arange(CHUNK)
    incl = idx[:, None] >= idx[None, :]
    strict = idx[:, None] > idx[None, :]
    decay = jnp.exp(jnp.where(incl, gc[..., :, None] - gc[..., None, :], -jnp.inf))
    kb = kc * bc[..., None]
    lower = jnp.where(strict, jnp.einsum('bnhid,bnhjd->bnhij', kb, kc) * decay, 0.0)
    eye = jnp.eye(CHUNK, dtype=lower.dtype)
    rhs = jnp.concatenate([vc * bc[..., None], kb * jnp.exp(gc)[..., None]], axis=-1)
    sol = lax.linalg.triangular_solve(lower + eye, rhs, left_side=True, lower=True,
                                      unit_diagonal=True)
    u, w = sol[..., :Dv], sol[..., Dv:]
    intra = jnp.where(incl, jnp.einsum('bnhid,bnhjd->bnhij', qc, kc) * decay, 0.0)

    def step(S, xs):
        q_i, k_i, u_i, w_i, g_i, a_i = xs
        v_new = u_i - jnp.einsum('bhck,bhkv->bhcv', w_i, S)
        o_i = (jnp.einsum('bhck,bhkv->bhcv', q_i * jnp.exp(g_i)[..., None], S)
               + jnp.einsum('bhij,bhjv->bhiv', a_i, v_new))
        g_last = g_i[..., -1:]
        S = (S * jnp.exp(g_last)[..., None]
             + jnp.einsum('bhck,bhcv->bhkv', k_i * jnp.exp(g_last - g_i)[..., None], v_new))
        return S, o_i

    S0 = jnp.zeros((bsz, H, Dk, Dv), q.dtype)
    xs = tuple(jnp.moveaxis(t, 1, 0) for t in (qc, kc, u, w, gc, intra))
    _, o = lax.scan(step, S0, xs)
    return o.transpose(1, 0, 3, 2, 4).reshape(bsz, T, H, Dv)


def gated_deltanet(qkv_raw, b_raw, a_raw, conv_w, a_log, dt_bias):
    bsz, T = qkv_raw.shape[0], qkv_raw.shape[1]
    qkv = jax.nn.silu(causal_depthwise_conv(qkv_raw, conv_w)).astype(jnp.float32)
    q, k, v = jnp.split(qkv, 3, axis=-1)
    q = l2_norm(q.reshape(bsz, T, C_HEADS, C_HEAD_DIM))
    k = l2_norm(k.reshape(bsz, T, C_HEADS, C_HEAD_DIM))
    v = v.reshape(bsz, T, C_HEADS, C_HEAD_DIM)
    beta = jax.nn.sigmoid(b_raw.astype(jnp.float32))
    g = -jnp.exp(a_log.astype(jnp.float32)) * jax.nn.softplus(
        a_raw.astype(jnp.float32) + dt_bias.astype(jnp.float32))
    return gated_delta_rule_chunked(q, k, v, g, beta)


def _fwd_setup_inputs(seed: int = 0) -> dict:
    key = jax.random.key(seed)
    ks = jax.random.split(key, 24)
    nrm = jax.random.normal
    x = nrm(ks[0], (BATCH, SEQ, D_MODEL), jnp.float32)
    offset = jax.random.randint(ks[1], (BATCH, 1), 0, MAX_POS_OFFSET, dtype=jnp.int32)
    positions = (offset + jnp.arange(SEQ, dtype=jnp.int32)[None, :]).astype(jnp.int32)
    norm_w = 1.0 + 0.02 * nrm(ks[2], (DEPTH, D_MODEL))
    w_in = nrm(ks[3], (DEPTH, D_MODEL, IN_COLS)) * D_MODEL ** -0.5
    q_norm_w = 1.0 + 0.02 * nrm(ks[4], (DEPTH, A_HEAD_DIM))
    k_norm_w = 1.0 + 0.02 * nrm(ks[5], (DEPTH, A_HEAD_DIM))
    sinks = nrm(ks[6], (DEPTH, A_Q_HEADS))
    b_conv_w = nrm(ks[7], (DEPTH, B_CONV_WIDTH, B_WIDTH)) * B_CONV_WIDTH ** -0.5
    b_conv_b = 0.02 * nrm(ks[8], (DEPTH, B_WIDTH))
    b_ln_w = 1.0 + 0.02 * nrm(ks[9], (DEPTH, B_WIDTH))
    b_ln_b = 0.02 * nrm(ks[10], (DEPTH, B_WIDTH))
    b_pw_w = nrm(ks[11], (DEPTH, B_WIDTH, B_WIDTH)) * B_WIDTH ** -0.5
    b_pw_b = 0.02 * nrm(ks[12], (DEPTH, B_WIDTH))
    c_conv_w = nrm(ks[13], (DEPTH, C_CONV_WIDTH, 3 * C_WIDTH)) * C_CONV_WIDTH ** -0.5
    c_a_log = jnp.log(jax.random.uniform(ks[14], (DEPTH, C_HEADS), minval=1.0, maxval=16.0))
    dt = jnp.exp(jax.random.uniform(ks[15], (DEPTH, C_HEADS),
                                    minval=math.log(1e-3), maxval=math.log(1e-1)))
    c_dt_bias = dt + jnp.log(-jnp.expm1(-dt))
    c_onorm_w = 1.0 + 0.02 * nrm(ks[16], (DEPTH, C_HEAD_DIM))
    w_out = nrm(ks[17], (DEPTH, MIX_WIDTH, D_MODEL)) * MIX_WIDTH ** -0.5
    return {"x": x, "positions": positions, "norm_w": norm_w, "w_in": w_in,
            "q_norm_w": q_norm_w, "k_norm_w": k_norm_w, "sinks": sinks,
            "b_conv_w": b_conv_w, "b_conv_b": b_conv_b, "b_ln_w": b_ln_w,
            "b_ln_b": b_ln_b, "b_pw_w": b_pw_w, "b_pw_b": b_pw_b,
            "c_conv_w": c_conv_w, "c_a_log": c_a_log, "c_dt_bias": c_dt_bias,
            "c_onorm_w": c_onorm_w, "w_out": w_out}


def _fwd_reference(x, positions, norm_w, w_in, q_norm_w, k_norm_w, sinks, b_conv_w, b_conv_b,
              b_ln_w, b_ln_b, b_pw_w, b_pw_b, c_conv_w, c_a_log, c_dt_bias, c_onorm_w,
              w_out):
    bsz, T = x.shape[0], x.shape[1]
    cos, sin = rope_tables(positions)
    for l in range(DEPTH):
        h = rms_norm(x, norm_w[l])
        p = h @ w_in[l]
        qa, ka, va, za, ub, zb, qkv_c, b_c, a_c, zc = _split_cols(p)
        qa = apply_partial_rope(rms_norm(qa.reshape(bsz, T, A_Q_HEADS, A_HEAD_DIM), q_norm_w[l]), cos, sin)
        ka = apply_partial_rope(rms_norm(ka.reshape(bsz, T, A_KV_HEADS, A_HEAD_DIM), k_norm_w[l]), cos, sin)
        va = va.reshape(bsz, T, A_KV_HEADS, A_HEAD_DIM)
        oa = sliding_window_sink_attention(qa, ka, va, sinks[l]) * jax.nn.silu(za)
        ob = conformer_conv_module(ub, b_conv_w[l], b_conv_b[l], b_ln_w[l], b_ln_b[l],
                                   b_pw_w[l], b_pw_b[l]) * jax.nn.silu(zb)
        oc = gated_deltanet(qkv_c, b_c, a_c, c_conv_w[l], c_a_log[l], c_dt_bias[l]).astype(x.dtype)
        oc = rms_norm(oc, c_onorm_w[l]) * jax.nn.silu(zc.reshape(bsz, T, C_HEADS, C_HEAD_DIM))
        oc = oc.reshape(bsz, T, C_WIDTH)
        y = jnp.concatenate([oa, ob, oc], axis=-1)
        x = x + y @ w_out[l]
    return x


import jax as _jax
import jax.numpy as _jnp

TWIN_FORMAT = 'train_step'
FWD_PARAMS = ['x', 'positions', 'norm_w', 'w_in', 'q_norm_w', 'k_norm_w', 'sinks', 'b_conv_w', 'b_conv_b', 'b_ln_w', 'b_ln_b', 'b_pw_w', 'b_pw_b', 'c_conv_w', 'c_a_log', 'c_dt_bias', 'c_onorm_w', 'w_out']
TWIN_WEIGHTS = ['norm_w', 'w_in', 'q_norm_w', 'k_norm_w', 'sinks', 'b_conv_w', 'b_conv_b', 'b_ln_w', 'b_ln_b', 'b_pw_w', 'b_pw_b', 'c_conv_w', 'c_a_log', 'c_dt_bias', 'c_onorm_w', 'w_out']
TWIN_DIFF_INPUT = 'x'
TWIN_INPUTS = ['x', 'positions', 'norm_w', 'w_in', 'q_norm_w', 'k_norm_w', 'sinks', 'b_conv_w', 'b_conv_b', 'b_ln_w', 'b_ln_b', 'b_pw_w', 'b_pw_b', 'c_conv_w', 'c_a_log', 'c_dt_bias', 'c_onorm_w', 'w_out', 'loss_target', 'm_norm_w', 'm_w_in', 'm_q_norm_w', 'm_k_norm_w', 'm_sinks', 'm_b_conv_w', 'm_b_conv_b', 'm_b_ln_w', 'm_b_ln_b', 'm_b_pw_w', 'm_b_pw_b', 'm_c_conv_w', 'm_c_a_log', 'm_c_dt_bias', 'm_c_onorm_w', 'm_w_out', 'v_norm_w', 'v_w_in', 'v_q_norm_w', 'v_k_norm_w', 'v_sinks', 'v_b_conv_w', 'v_b_conv_b', 'v_b_ln_w', 'v_b_ln_b', 'v_b_pw_w', 'v_b_pw_b', 'v_c_conv_w', 'v_c_a_log', 'v_c_dt_bias', 'v_c_onorm_w', 'v_w_out']
TWIN_OUTPUTS = ['loss', 'grad_x', 'grad_norm_w', 'grad_w_in', 'grad_q_norm_w', 'grad_k_norm_w', 'grad_sinks', 'grad_b_conv_w', 'grad_b_conv_b', 'grad_b_ln_w', 'grad_b_ln_b', 'grad_b_pw_w', 'grad_b_pw_b', 'grad_c_conv_w', 'grad_c_a_log', 'grad_c_dt_bias', 'grad_c_onorm_w', 'grad_w_out', 'delta_norm_w', 'delta_w_in', 'delta_q_norm_w', 'delta_k_norm_w', 'delta_sinks', 'delta_b_conv_w', 'delta_b_conv_b', 'delta_b_ln_w', 'delta_b_ln_b', 'delta_b_pw_w', 'delta_b_pw_b', 'delta_c_conv_w', 'delta_c_a_log', 'delta_c_dt_bias', 'delta_c_onorm_w', 'delta_w_out', 'new_m_norm_w', 'new_m_w_in', 'new_m_q_norm_w', 'new_m_k_norm_w', 'new_m_sinks', 'new_m_b_conv_w', 'new_m_b_conv_b', 'new_m_b_ln_w', 'new_m_b_ln_b', 'new_m_b_pw_w', 'new_m_b_pw_b', 'new_m_c_conv_w', 'new_m_c_a_log', 'new_m_c_dt_bias', 'new_m_c_onorm_w', 'new_m_w_out', 'new_v_norm_w', 'new_v_w_in', 'new_v_q_norm_w', 'new_v_k_norm_w', 'new_v_sinks', 'new_v_b_conv_w', 'new_v_b_conv_b', 'new_v_b_ln_w', 'new_v_b_ln_b', 'new_v_b_pw_w', 'new_v_b_pw_b', 'new_v_c_conv_w', 'new_v_c_a_log', 'new_v_c_dt_bias', 'new_v_c_onorm_w', 'new_v_w_out']
TWIN_LEAF_KINDS = {'loss': 'loss', 'grad_x': 'grad_x', 'grad_norm_w': 'grad_w', 'grad_w_in': 'grad_w', 'grad_q_norm_w': 'grad_w', 'grad_k_norm_w': 'grad_w', 'grad_sinks': 'grad_w', 'grad_b_conv_w': 'grad_w', 'grad_b_conv_b': 'grad_w', 'grad_b_ln_w': 'grad_w', 'grad_b_ln_b': 'grad_w', 'grad_b_pw_w': 'grad_w', 'grad_b_pw_b': 'grad_w', 'grad_c_conv_w': 'grad_w', 'grad_c_a_log': 'grad_w', 'grad_c_dt_bias': 'grad_w', 'grad_c_onorm_w': 'grad_w', 'grad_w_out': 'grad_w', 'delta_norm_w': 'delta_w', 'delta_w_in': 'delta_w', 'delta_q_norm_w': 'delta_w', 'delta_k_norm_w': 'delta_w', 'delta_sinks': 'delta_w', 'delta_b_conv_w': 'delta_w', 'delta_b_conv_b': 'delta_w', 'delta_b_ln_w': 'delta_w', 'delta_b_ln_b': 'delta_w', 'delta_b_pw_w': 'delta_w', 'delta_b_pw_b': 'delta_w', 'delta_c_conv_w': 'delta_w', 'delta_c_a_log': 'delta_w', 'delta_c_dt_bias': 'delta_w', 'delta_c_onorm_w': 'delta_w', 'delta_w_out': 'delta_w', 'new_m_norm_w': 'new_m', 'new_m_w_in': 'new_m', 'new_m_q_norm_w': 'new_m', 'new_m_k_norm_w': 'new_m', 'new_m_sinks': 'new_m', 'new_m_b_conv_w': 'new_m', 'new_m_b_conv_b': 'new_m', 'new_m_b_ln_w': 'new_m', 'new_m_b_ln_b': 'new_m', 'new_m_b_pw_w': 'new_m', 'new_m_b_pw_b': 'new_m', 'new_m_c_conv_w': 'new_m', 'new_m_c_a_log': 'new_m', 'new_m_c_dt_bias': 'new_m', 'new_m_c_onorm_w': 'new_m', 'new_m_w_out': 'new_m', 'new_v_norm_w': 'new_v', 'new_v_w_in': 'new_v', 'new_v_q_norm_w': 'new_v', 'new_v_k_norm_w': 'new_v', 'new_v_sinks': 'new_v', 'new_v_b_conv_w': 'new_v', 'new_v_b_conv_b': 'new_v', 'new_v_b_ln_w': 'new_v', 'new_v_b_ln_b': 'new_v', 'new_v_b_pw_w': 'new_v', 'new_v_b_pw_b': 'new_v', 'new_v_c_conv_w': 'new_v', 'new_v_c_a_log': 'new_v', 'new_v_c_dt_bias': 'new_v', 'new_v_c_onorm_w': 'new_v', 'new_v_w_out': 'new_v'}


def _forward(args):
    return _fwd_reference(*[args[k] for k in FWD_PARAMS])


def _output_shape():
    out = _jax.eval_shape(lambda: _forward(_fwd_setup_inputs(0)))
    return out.shape, out.dtype

N_MICROBATCH = 1
ADAM_LR = 0.001
ADAM_B1 = 0.9
ADAM_B2 = 0.999
ADAM_EPS = 1e-08
ADAM_WD = 0.01
ADAM_STEP = 10
PER_EXAMPLE_BATCH_AXIS = {'x': 0, 'positions': 0, 'loss_target': 0}
SHARED_INPUTS = []
_WEIGHT_DTYPES = {'norm_w': _jnp.float32, 'w_in': _jnp.float32, 'q_norm_w': _jnp.float32, 'k_norm_w': _jnp.float32, 'sinks': _jnp.float32, 'b_conv_w': _jnp.float32, 'b_conv_b': _jnp.float32, 'b_ln_w': _jnp.float32, 'b_ln_b': _jnp.float32, 'b_pw_w': _jnp.float32, 'b_pw_b': _jnp.float32, 'c_conv_w': _jnp.float32, 'c_a_log': _jnp.float32, 'c_dt_bias': _jnp.float32, 'c_onorm_w': _jnp.float32, 'w_out': _jnp.float32}
MOMENT_SCALE = {'norm_w': 3.048991e+00, 'w_in': 1.265716e-01, 'q_norm_w': 7.840152e-01, 'k_norm_w': 7.797080e-01, 'sinks': 1.535446e-01, 'b_conv_w': 9.348740e-02, 'b_conv_b': 1.197927e+00, 'b_ln_w': 2.496238e+00, 'b_ln_b': 1.704958e+00, 'b_pw_w': 2.358072e-01, 'b_pw_b': 1.310024e+00, 'c_conv_w': 2.326997e-01, 'c_a_log': 1.100640e+01, 'c_dt_bias': 1.054497e+01, 'c_onorm_w': 3.527851e+01, 'w_out': 2.411973e-01}


def _to_microbatches(a, axis):
    t = _jnp.moveaxis(a, axis, 0)
    t = t.reshape((N_MICROBATCH, t.shape[0] // N_MICROBATCH) + t.shape[1:])
    return _jnp.moveaxis(t, 1, axis + 1)


def setup_inputs(seed: int = 0) -> dict:
    inp = _fwd_setup_inputs(seed)
    key = _jax.random.fold_in(_jax.random.key(seed), 7919)
    shape, _ = _output_shape()
    out = dict(inp)
    out["loss_target"] = _jax.random.normal(_jax.random.fold_in(key, 0), shape, _jnp.float32)
    for i, name in enumerate(TWIN_WEIGHTS):
        w = inp[name].astype(_jnp.float32)
        if MOMENT_SCALE is None:
            s = _jnp.sqrt(_jnp.mean(_jnp.square(w)) + 1e-30)
        else:
            s = MOMENT_SCALE[name]
        km, kv = _jax.random.split(_jax.random.fold_in(key, i + 1))
        out[name] = w
        out["m_" + name] = s * _jax.random.normal(km, w.shape, _jnp.float32)
        out["v_" + name] = (s * s) * _jax.random.uniform(kv, w.shape, _jnp.float32, 0.5, 1.5)
    if N_MICROBATCH > 1:
        for name, axis in PER_EXAMPLE_BATCH_AXIS.items():
            out[name] = _to_microbatches(out[name], axis)
    return {'x': out['x'], 'positions': out['positions'], 'norm_w': out['norm_w'], 'w_in': out['w_in'], 'q_norm_w': out['q_norm_w'], 'k_norm_w': out['k_norm_w'], 'sinks': out['sinks'], 'b_conv_w': out['b_conv_w'], 'b_conv_b': out['b_conv_b'], 'b_ln_w': out['b_ln_w'], 'b_ln_b': out['b_ln_b'], 'b_pw_w': out['b_pw_w'], 'b_pw_b': out['b_pw_b'], 'c_conv_w': out['c_conv_w'], 'c_a_log': out['c_a_log'], 'c_dt_bias': out['c_dt_bias'], 'c_onorm_w': out['c_onorm_w'], 'w_out': out['w_out'], 'loss_target': out['loss_target'], 'm_norm_w': out['m_norm_w'], 'm_w_in': out['m_w_in'], 'm_q_norm_w': out['m_q_norm_w'], 'm_k_norm_w': out['m_k_norm_w'], 'm_sinks': out['m_sinks'], 'm_b_conv_w': out['m_b_conv_w'], 'm_b_conv_b': out['m_b_conv_b'], 'm_b_ln_w': out['m_b_ln_w'], 'm_b_ln_b': out['m_b_ln_b'], 'm_b_pw_w': out['m_b_pw_w'], 'm_b_pw_b': out['m_b_pw_b'], 'm_c_conv_w': out['m_c_conv_w'], 'm_c_a_log': out['m_c_a_log'], 'm_c_dt_bias': out['m_c_dt_bias'], 'm_c_onorm_w': out['m_c_onorm_w'], 'm_w_out': out['m_w_out'], 'v_norm_w': out['v_norm_w'], 'v_w_in': out['v_w_in'], 'v_q_norm_w': out['v_q_norm_w'], 'v_k_norm_w': out['v_k_norm_w'], 'v_sinks': out['v_sinks'], 'v_b_conv_w': out['v_b_conv_w'], 'v_b_conv_b': out['v_b_conv_b'], 'v_b_ln_w': out['v_b_ln_w'], 'v_b_ln_b': out['v_b_ln_b'], 'v_b_pw_w': out['v_b_pw_w'], 'v_b_pw_b': out['v_b_pw_b'], 'v_c_conv_w': out['v_c_conv_w'], 'v_c_a_log': out['v_c_a_log'], 'v_c_dt_bias': out['v_c_dt_bias'], 'v_c_onorm_w': out['v_c_onorm_w'], 'v_w_out': out['v_w_out']}


def _loss(weights, diff, rest, loss_target):
    with _jax.named_scope("forward"):
        args = {**rest, TWIN_DIFF_INPUT: diff, **{k: w.astype(_WEIGHT_DTYPES[k]) for k, w in weights.items()}}
        y = _forward(args)
    with _jax.named_scope("loss_head"):
        err = _jnp.square(y.astype(_jnp.float32) - loss_target)
        return 0.5 * _jnp.sum(_jnp.mean(err, axis=-1)) if err.ndim else 0.5 * err


def _adamw(w, g, m, v):
    m = ADAM_B1 * m + (1.0 - ADAM_B1) * g
    v = ADAM_B2 * v + (1.0 - ADAM_B2) * _jnp.square(g)
    m_hat = m / (1.0 - ADAM_B1 ** ADAM_STEP)
    v_hat = v / (1.0 - ADAM_B2 ** ADAM_STEP)
    delta = -ADAM_LR * (m_hat / (_jnp.sqrt(v_hat) + ADAM_EPS) + ADAM_WD * w)
    return delta, m, v


def reference(x, positions, norm_w, w_in, q_norm_w, k_norm_w, sinks, b_conv_w, b_conv_b, b_ln_w, b_ln_b, b_pw_w, b_pw_b, c_conv_w, c_a_log, c_dt_bias, c_onorm_w, w_out, loss_target, m_norm_w, m_w_in, m_q_norm_w, m_k_norm_w, m_sinks, m_b_conv_w, m_b_conv_b, m_b_ln_w, m_b_ln_b, m_b_pw_w, m_b_pw_b, m_c_conv_w, m_c_a_log, m_c_dt_bias, m_c_onorm_w, m_w_out, v_norm_w, v_w_in, v_q_norm_w, v_k_norm_w, v_sinks, v_b_conv_w, v_b_conv_b, v_b_ln_w, v_b_ln_b, v_b_pw_w, v_b_pw_b, v_c_conv_w, v_c_a_log, v_c_dt_bias, v_c_onorm_w, v_w_out):
    given = dict(x=x, positions=positions, norm_w=norm_w, w_in=w_in, q_norm_w=q_norm_w, k_norm_w=k_norm_w, sinks=sinks, b_conv_w=b_conv_w, b_conv_b=b_conv_b, b_ln_w=b_ln_w, b_ln_b=b_ln_b, b_pw_w=b_pw_w, b_pw_b=b_pw_b, c_conv_w=c_conv_w, c_a_log=c_a_log, c_dt_bias=c_dt_bias, c_onorm_w=c_onorm_w, w_out=w_out, loss_target=loss_target, m_norm_w=m_norm_w, m_w_in=m_w_in, m_q_norm_w=m_q_norm_w, m_k_norm_w=m_k_norm_w, m_sinks=m_sinks, m_b_conv_w=m_b_conv_w, m_b_conv_b=m_b_conv_b, m_b_ln_w=m_b_ln_w, m_b_ln_b=m_b_ln_b, m_b_pw_w=m_b_pw_w, m_b_pw_b=m_b_pw_b, m_c_conv_w=m_c_conv_w, m_c_a_log=m_c_a_log, m_c_dt_bias=m_c_dt_bias, m_c_onorm_w=m_c_onorm_w, m_w_out=m_w_out, v_norm_w=v_norm_w, v_w_in=v_w_in, v_q_norm_w=v_q_norm_w, v_k_norm_w=v_k_norm_w, v_sinks=v_sinks, v_b_conv_w=v_b_conv_w, v_b_conv_b=v_b_conv_b, v_b_ln_w=v_b_ln_w, v_b_ln_b=v_b_ln_b, v_b_pw_w=v_b_pw_w, v_b_pw_b=v_b_pw_b, v_c_conv_w=v_c_conv_w, v_c_a_log=v_c_a_log, v_c_dt_bias=v_c_dt_bias, v_c_onorm_w=v_c_onorm_w, v_w_out=v_w_out)
    weights = {n: given[n] for n in TWIN_WEIGHTS}
    shared = {n: given[n] for n in SHARED_INPUTS}
    per_example = {n: given[n] for n in ['x', 'positions']}
    grad_fn = _jax.value_and_grad(_loss, argnums=(0, 1))

    def one_microbatch(ex, loss_target):
        ex = dict(ex)
        diff = ex.pop(TWIN_DIFF_INPUT)
        return grad_fn(weights, diff, {**shared, **ex}, loss_target)

    if N_MICROBATCH == 1:
        loss, (grad_w, grad_x) = one_microbatch(per_example, given["loss_target"])
    else:
        def body(carry, xs):
            loss_sum, grad_sum = carry
            l_k, (gw_k, gx_k) = one_microbatch(xs[0], xs[1])
            with _jax.named_scope("update"):
                return (loss_sum + l_k, _jax.tree.map(_jnp.add, grad_sum, gw_k)), gx_k

        init = (_jnp.zeros((), _jnp.float32), _jax.tree.map(_jnp.zeros_like, weights))
        (loss, grad_w), grad_x = _jax.lax.scan(body, init, (per_example, given["loss_target"]))
    with _jax.named_scope("update"):
        delta_w, new_m, new_v = {}, {}, {}
        for n in TWIN_WEIGHTS:
            delta_w[n], new_m[n], new_v[n] = _adamw(weights[n], grad_w[n], given["m_" + n], given["v_" + n])
    return (loss, grad_x, *[grad_w[n] for n in TWIN_WEIGHTS], *[delta_w[n] for n in TWIN_WEIGHTS],
            *[new_m[n] for n in TWIN_WEIGHTS], *[new_v[n] for n in TWIN_WEIGHTS])
```

```python
import functools
import math

import jax
import jax.numpy as jnp
from jax import lax
from jax.experimental import pallas as pl
from jax.experimental.pallas import tpu as pltpu

F32 = jnp.float32
BF16 = jnp.bfloat16
HI = lax.Precision.HIGHEST
MESH = pl.DeviceIdType.MESH
S = jax.ShapeDtypeStruct
BS = pl.BlockSpec

N_DEV = 8
DEPTH = 2
D_MODEL = 2048
A_HEADS, A_KV, A_DH, A_W, A_KVW = 12, 4, 64, 768, 256
ROT = 16
THETA = 500000.0
ABLK = 128
B_W, B_K = 512, 31
C_HEADS, C_DH, C_W, C_K, CHUNK = 6, 128, 768, 4, 64
EPS = 1e-6
IN_COLS = 6668
P_Q, P_ZA, P_K, P_V, P_UB, P_ZB, P_QKV, P_ZC, P_BA, P_W = 0, 768, 1536, 1792, 2048, 3072, 3584, 5888, 6656, 6912
Y_A, Y_C, Y_B = 0, 768, 1536
LANES = 128
SUB = 8

ADAM_LR, ADAM_B1, ADAM_B2, ADAM_EPS, ADAM_WD, ADAM_STEP = 0.001, 0.9, 0.999, 1e-08, 0.01, 10


def _cp(*sem, vmem=None):
    kw = {}
    if sem:
        kw["dimension_semantics"] = sem
    if vmem:
        kw["vmem_limit_bytes"] = vmem
    return pltpu.CompilerParams(**kw)


def _pack_cols(w):
    z = jnp.zeros(w.shape[:-1] + (P_W - IN_COLS,), w.dtype)
    return jnp.concatenate([w[..., 0:768], w[..., 1280:2048], w[..., 768:1024], w[..., 1024:1280],
                            w[..., 2048:3072], w[..., 3072:3584], w[..., 3584:5888], w[..., 5900:6668],
                            w[..., 5888:5900], z], axis=-1)


def _unpack_cols(g):
    return jnp.concatenate([g[..., 0:768], g[..., 1536:1792], g[..., 1792:2048], g[..., 768:1536],
                            g[..., 2048:3072], g[..., 3072:3584], g[..., 3584:5888],
                            g[..., 6656:6668], g[..., 5888:6656]], axis=-1)


def _sigmoid(x):
    return 1.0 / (1.0 + jnp.exp(-x))


def _dsilu(x, sg):
    return sg * (1.0 + x * (1.0 - sg))


def _fold8(x):
    r, c = x.shape
    return x.reshape(r // SUB, SUB, c).sum(axis=0)


def _dot(a, b, prec=None):
    return jnp.dot(a, b, preferred_element_type=F32, precision=prec)


def _dot_nt(a, b, prec=None):
    return lax.dot_general(a, b, (((1,), (1,)), ((), ())), preferred_element_type=F32, precision=prec)


def _dot_tn(a, b, prec=None):
    return lax.dot_general(a, b, (((0,), (0,)), ((), ())), preferred_element_type=F32, precision=prec)


def _lane(shape):
    return lax.broadcasted_iota(jnp.int32, shape, 1)


def _subl(shape):
    return lax.broadcasted_iota(jnp.int32, shape, 0)


def _col(x, j):
    return jnp.sum(jnp.where(_lane(x.shape) == j, x, 0.0), axis=-1, keepdims=True)


def _inproj(x, nw, w, tm=512, tn=768):
    n, d = x.shape
    pw = w.shape[1]

    def body(x_ref, nw_ref, w_ref, p_ref, h_ref):
        @pl.when(pl.program_id(1) == 0)
        def _():
            xv = x_ref[...]
            r = lax.rsqrt(jnp.mean(xv * xv, axis=-1, keepdims=True) + EPS)
            h_ref[...] = (xv * r * nw_ref[...]).astype(BF16)

        p_ref[...] = _dot(h_ref[...], w_ref[...])

    return pl.pallas_call(
        body, name="inproj", grid=(n // tm, pw // tn),
        in_specs=[BS((tm, d), lambda i, j: (i, 0)), BS((1, d), lambda i, j: (0, 0)), BS((d, tn), lambda i, j: (0, j))],
        out_specs=[BS((tm, tn), lambda i, j: (i, j)), BS((tm, d), lambda i, j: (i, 0))],
        out_shape=[S((n, pw), F32), S((n, d), BF16)],
        compiler_params=_cp("arbitrary", "arbitrary"),
    )(x, nw, w)


def _outproj(x, y, w, tm=512, tn=1024):
    n, d = x.shape
    k = y.shape[1]

    def body(x_ref, y_ref, w_ref, o_ref):
        o_ref[...] = x_ref[...] + _dot(y_ref[...], w_ref[...])

    return pl.pallas_call(
        body, name="outproj", grid=(n // tm, d // tn),
        in_specs=[BS((tm, tn), lambda i, j: (i, j)), BS((tm, k), lambda i, j: (i, 0)), BS((k, tn), lambda i, j: (0, j))],
        out_specs=BS((tm, tn), lambda i, j: (i, j)),
        out_shape=S((n, d), F32),
        compiler_params=_cp("arbitrary", "arbitrary"),
    )(x, y, w)


def _outproj_loss(x, y, w, tgt, tm=512, tn=1024):
    n, d = x.shape
    k = y.shape[1]

    def body(x_ref, y_ref, w_ref, t_ref, g_ref, l_ref):
        @pl.when((pl.program_id(0) == 0) & (pl.program_id(1) == 0))
        def _():
            l_ref[...] = jnp.zeros_like(l_ref)

        diff = x_ref[...] + _dot(y_ref[...], w_ref[...]) - t_ref[...]
        g_ref[...] = diff * (1.0 / d)
        f = _fold8(diff * diff)
        acc = f[:, 0:LANES]
        for c in range(1, tn // LANES):
            acc = acc + f[:, c * LANES:(c + 1) * LANES]
        l_ref[...] += acc

    return pl.pallas_call(
        body, name="outproj_loss", grid=(n // tm, d // tn),
        in_specs=[BS((tm, tn), lambda i, j: (i, j)), BS((tm, k), lambda i, j: (i, 0)), BS((k, tn), lambda i, j: (0, j)),
                  BS((tm, tn), lambda i, j: (i, j))],
        out_specs=[BS((tm, tn), lambda i, j: (i, j)), BS((SUB, LANES), lambda i, j: (0, 0))],
        out_shape=[S((n, d), F32), S((SUB, LANES), F32)],
        compiler_params=_cp("arbitrary", "arbitrary"),
    )(x, y, w, tgt)


def _matmul(a, b, mode, out_dtype, tm, tn, tk, name):
    if mode == "nn":
        (m, kk), nn = a.shape, b.shape[1]
        a_spec, b_spec = BS((tm, tk), lambda i, j, k: (i, k)), BS((tk, tn), lambda i, j, k: (k, j))
        dot = _dot
    elif mode == "nt":
        (m, kk), nn = a.shape, b.shape[0]
        a_spec, b_spec = BS((tm, tk), lambda i, j, k: (i, k)), BS((tn, tk), lambda i, j, k: (j, k))
        dot = _dot_nt
    else:
        (kk, m), nn = a.shape, b.shape[1]
        a_spec, b_spec = BS((tk, tm), lambda i, j, k: (k, i)), BS((tk, tn), lambda i, j, k: (k, j))
        dot = _dot_tn
    nk = kk // tk

    def body(a_ref, b_ref, o_ref, acc_ref):
        kid = pl.program_id(2)

        @pl.when(kid == 0)
        def _():
            acc_ref[...] = jnp.zeros_like(acc_ref)

        acc_ref[...] += dot(a_ref[...].astype(BF16), b_ref[...].astype(BF16))

        @pl.when(kid == nk - 1)
        def _():
            o_ref[...] = acc_ref[...].astype(out_dtype)

    return pl.pallas_call(
        body, name=name, grid=(m // tm, nn // tn, nk),
        in_specs=[a_spec, b_spec], out_specs=BS((tm, tn), lambda i, j, k: (i, j)),
        out_shape=S((m, nn), out_dtype), scratch_shapes=[pltpu.VMEM((tm, tn), F32)],
        compiler_params=_cp("arbitrary", "arbitrary", "arbitrary"),
    )(a, b)


def _inproj_bwd_dx(dp, w, x, nw, dres, tm=512, tk=768):
    n, d = x.shape
    nk = dp.shape[1] // tk

    def body(dp_ref, w_ref, x_ref, nw_ref, dr_ref, dx_ref, dnw_ref, acc_ref):
        kid = pl.program_id(1)

        @pl.when((pl.program_id(0) == 0) & (kid == 0))
        def _():
            dnw_ref[...] = jnp.zeros_like(dnw_ref)

        @pl.when(kid == 0)
        def _():
            acc_ref[...] = jnp.zeros_like(acc_ref)

        acc_ref[...] += _dot_nt(dp_ref[...], w_ref[...])

        @pl.when(kid == nk - 1)
        def _():
            dh = acc_ref[...]
            xv = x_ref[...]
            r = lax.rsqrt(jnp.mean(xv * xv, axis=-1, keepdims=True) + EPS)
            dnw_ref[...] += _fold8(dh * xv * r)
            g = dh * nw_ref[...]
            mm = jnp.mean(g * xv, axis=-1, keepdims=True)
            dx_ref[...] = dr_ref[...] + r * g - xv * (r * r * r * mm)

    return pl.pallas_call(
        body, name="inproj_bwd_dx", grid=(n // tm, nk),
        in_specs=[BS((tm, tk), lambda i, k: (i, k)), BS((d, tk), lambda i, k: (0, k)), BS((tm, d), lambda i, k: (i, 0)),
                  BS((1, d), lambda i, k: (0, 0)), BS((tm, d), lambda i, k: (i, 0))],
        out_specs=[BS((tm, d), lambda i, k: (i, 0)), BS((SUB, d), lambda i, k: (0, 0))],
        out_shape=[S((n, d), F32), S((SUB, d), F32)],
        scratch_shapes=[pltpu.VMEM((tm, d), F32)],
        compiler_params=_cp("arbitrary", "arbitrary"),
    )(dp, w, x, nw, dres)


def _rope_tables(pos):
    half = ROT // 2
    inv = THETA ** (-jnp.arange(0, ROT, 2, dtype=F32) / ROT)
    ang = pos.astype(F32)[:, None] * inv
    cos, sin = jnp.cos(ang), jnp.sin(ang)
    n = pos.shape[0]
    one = jnp.ones((n, A_DH - ROT), F32)
    zero = jnp.zeros((n, A_DH - ROT), F32)
    zh = jnp.zeros((n, half), F32)
    c = jnp.concatenate([cos, cos, one], axis=1)
    s1 = jnp.concatenate([-sin, zh, zero], axis=1)
    s2 = jnp.concatenate([zh, sin, zero], axis=1)
    return tuple(jnp.concatenate([t, t], axis=1) for t in (c, s1, s2))


def _half_stat(t):
    lo = _lane(t.shape) < A_DH
    s_lo = jnp.sum(jnp.where(lo, t, 0.0), axis=-1, keepdims=True)
    s_hi = jnp.sum(jnp.where(lo, 0.0, t), axis=-1, keepdims=True)
    return jnp.where(lo, s_lo, s_hi)


def _normrope(x, w, c, s1, s2):
    r = lax.rsqrt(_half_stat(x * x) * (1.0 / A_DH) + EPS)
    xn = x * r * w
    return xn * c + pltpu.roll(xn, LANES - ROT // 2, 1) * s1 + pltpu.roll(xn, ROT // 2, 1) * s2, r


def _normrope_bwd(dy, x, r, w, c, s1, s2):
    dxn = dy * c + pltpu.roll(dy * s1, ROT // 2, 1) + pltpu.roll(dy * s2, LANES - ROT // 2, 1)
    g = dxn * w
    mm = _half_stat(g * x) * (1.0 / A_DH)
    return r * g - x * (r * r * r * mm), dxn * x * r


def _attn_mask(first):
    qi = _subl((ABLK, 2 * ABLK))
    kj = _lane((ABLK, 2 * ABLK))
    dist = qi + ABLK - kj
    return (dist >= 0) & (dist < ABLK) & (jnp.logical_not(first) | (kj >= ABLK))


def _attn_fwd(p, tabs, qw, kw, sinks, nseq):
    n = p.shape[0]
    nb = n // nseq // ABLK
    cur = lambda b, i: (b * nb + i, 0)
    prv = lambda b, i: (b * nb + jnp.maximum(i - 1, 0), 0)
    colblk = lambda f, w, off: (lambda b, i: (f(b, i)[0], off // w))

    def body(q_ref, za_ref, kc_ref, vc_ref, kp_ref, vp_ref, c_ref, s1_ref, s2_ref, cp_ref, s1p_ref, s2p_ref,
             qw_ref, kw_ref, sink_ref, y_ref, o_ref, lse_ref):
        first = pl.program_id(1) == 0
        tc = (c_ref[...], s1_ref[...], s2_ref[...])
        tp = (cp_ref[...], s1p_ref[...], s2p_ref[...])
        q, kc, kp = q_ref[...], kc_ref[...], kp_ref[...]
        qn = [_normrope(q[:, LANES * b:LANES * (b + 1)], qw_ref[...], *tc)[0].astype(BF16) for b in range(A_W // LANES)]
        k2, v2 = [], []
        for b in range(A_KVW // LANES):
            sl = slice(LANES * b, LANES * (b + 1))
            k2.append(jnp.concatenate([_normrope(kp[:, sl], kw_ref[...], *tp)[0],
                                       _normrope(kc[:, sl], kw_ref[...], *tc)[0]], axis=0).astype(BF16))
            v2.append(jnp.concatenate([vp_ref[:, sl], vc_ref[:, sl]], axis=0).astype(BF16))
        valid = _attn_mask(first)
        outs = []
        lse = jnp.zeros((ABLK, LANES), F32)
        for g in range(A_KV):
            hs = slice(A_DH * (g % 2), A_DH * (g % 2 + 1))
            kh, vh = k2[g // 2][:, hs], v2[g // 2][:, hs]
            for j in range(3 * g, 3 * g + 3):
                qh = qn[j // 2][:, A_DH * (j % 2):A_DH * (j % 2 + 1)]
                s = jnp.where(valid, _dot_nt(qh, kh) * (A_DH ** -0.5), -jnp.inf)
                sk = sink_ref[j]
                m = jnp.maximum(jnp.max(s, axis=-1, keepdims=True), sk)
                e = jnp.exp(s - m)
                den = jnp.sum(e, axis=-1, keepdims=True) + jnp.exp(sk - m)
                outs.append(_dot((e / den).astype(BF16), vh))
                lse = jnp.where(_lane(lse.shape) == j, m + jnp.log(den), lse)
        o = jnp.concatenate(outs, axis=1)
        za = za_ref[...]
        o_ref[...] = o
        lse_ref[...] = lse
        y_ref[...] = (o * (za * _sigmoid(za))).astype(BF16)

    tab_specs = [BS((ABLK, LANES), cur)] * 3 + [BS((ABLK, LANES), prv)] * 3
    return pl.pallas_call(
        body, name="attn_fwd", grid=(nseq, nb),
        in_specs=[BS((ABLK, A_W), colblk(cur, A_W, P_Q)), BS((ABLK, A_W), colblk(cur, A_W, P_ZA)),
                  BS((ABLK, A_KVW), colblk(cur, A_KVW, P_K)), BS((ABLK, A_KVW), colblk(cur, A_KVW, P_V)),
                  BS((ABLK, A_KVW), colblk(prv, A_KVW, P_K)), BS((ABLK, A_KVW), colblk(prv, A_KVW, P_V))]
        + tab_specs + [BS((1, LANES), lambda b, i: (0, 0))] * 2 + [BS(memory_space=pltpu.SMEM)],
        out_specs=[BS((ABLK, A_W), colblk(cur, A_W, Y_A)), BS((ABLK, A_W), cur), BS((ABLK, LANES), cur)],
        out_shape=[S((n, D_MODEL), BF16), S((n, A_W), F32), S((n, LANES), F32)],
        compiler_params=_cp("arbitrary", "arbitrary"),
    )(p, p, p, p, p, p, *tabs, *tabs, qw, kw, sinks)


def _attn_bwd(p, dy, o, lse, tabs, qw, kw, sinks, nseq):
    n = p.shape[0]
    nb = n // nseq // ABLK
    cur = lambda b, i: (b * nb + jnp.minimum(i, nb - 1), 0)
    prv = lambda b, i: (b * nb + jnp.maximum(i - 1, 0), 0)
    colblk = lambda f, w, off: (lambda b, i: (f(b, i)[0], off // w))

    def body(q_ref, za_ref, kc_ref, vc_ref, kp_ref, vp_ref, dy_ref, o_ref, lse_ref,
             c_ref, s1_ref, s2_ref, cp_ref, s1p_ref, s2p_ref, qw_ref, kw_ref, sink_ref,
             dqza_ref, dkv_ref, dqw_ref, dkw_ref, dsk_ref, tk_ref, tv_ref, ck_ref, cv_ref):
        i = pl.program_id(1)
        first = i == 0
        tc = (c_ref[...], s1_ref[...], s2_ref[...])
        tp = (cp_ref[...], s1p_ref[...], s2p_ref[...])
        nkb = A_KVW // LANES

        @pl.when((pl.program_id(0) == 0) & first)
        def _():
            dqw_ref[...] = jnp.zeros_like(dqw_ref)
            dkw_ref[...] = jnp.zeros_like(dkw_ref)
            dsk_ref[...] = jnp.zeros_like(dsk_ref)

        @pl.when(i < nb)
        def _():
            q, kc, kp = q_ref[...], kc_ref[...], kp_ref[...]
            qn, rq = [], []
            for b in range(A_W // LANES):
                a, r = _normrope(q[:, LANES * b:LANES * (b + 1)], qw_ref[...], *tc)
                qn.append(a.astype(BF16))
                rq.append(r)
            k2, v2 = [], []
            for b in range(nkb):
                sl = slice(LANES * b, LANES * (b + 1))
                k2.append(jnp.concatenate([_normrope(kp[:, sl], kw_ref[...], *tp)[0],
                                           _normrope(kc[:, sl], kw_ref[...], *tc)[0]], axis=0).astype(BF16))
                v2.append(jnp.concatenate([vp_ref[:, sl], vc_ref[:, sl]], axis=0).astype(BF16))
            valid = _attn_mask(first)
            za, dy, o, lse = za_ref[...], dy_ref[...], o_ref[...], lse_ref[...]
            sg = _sigmoid(za)
            do = dy * za * sg
            dqza_ref[:, A_W:2 * A_W] = (dy * o * _dsilu(za, sg)).astype(BF16)
            dqs, dks, dvs = [], [], []
            dsk = jnp.zeros((ABLK, LANES), F32)
            for g in range(A_KV):
                hs = slice(A_DH * (g % 2), A_DH * (g % 2 + 1))
                kh, vh = k2[g // 2][:, hs], v2[g // 2][:, hs]
                dkg = jnp.zeros((2 * ABLK, A_DH), F32)
                dvg = jnp.zeros((2 * ABLK, A_DH), F32)
                for j in range(3 * g, 3 * g + 3):
                    js = slice(A_DH * j, A_DH * (j + 1))
                    qh = qn[j // 2][:, A_DH * (j % 2):A_DH * (j % 2 + 1)]
                    lj = _col(lse, j)
                    s = jnp.where(valid, _dot_nt(qh, kh) * (A_DH ** -0.5), -jnp.inf)
                    pr = jnp.exp(s - lj)
                    doh = do[:, js]
                    delta = jnp.sum(doh * o[:, js], axis=-1, keepdims=True)
                    dsk = dsk + jnp.where(_lane(dsk.shape) == j, -jnp.exp(sink_ref[j] - lj) * delta, 0.0)
                    dohb = doh.astype(BF16)
                    ds = (pr * (_dot_nt(dohb, vh) - delta) * (A_DH ** -0.5)).astype(BF16)
                    dqs.append(_dot(ds, kh))
                    dkg = dkg + _dot_tn(ds, qh)
                    dvg = dvg + _dot_tn(pr.astype(BF16), dohb)
                dks.append(dkg)
                dvs.append(dvg)
            dsk_ref[...] += _fold8(dsk)
            dqn = jnp.concatenate(dqs, axis=1)
            dqw = jnp.zeros((SUB, LANES), F32)
            dqo = []
            for b in range(A_W // LANES):
                sl = slice(LANES * b, LANES * (b + 1))
                dx, wt = _normrope_bwd(dqn[:, sl], q[:, sl], rq[b], qw_ref[...], *tc)
                dqo.append(dx)
                dqw = dqw + _fold8(wt)
            dqw_ref[...] += dqw
            dqza_ref[:, 0:A_W] = jnp.concatenate(dqo, axis=1).astype(BF16)
            tk_ref[...] = jnp.concatenate(dks, axis=1)
            tv_ref[...] = jnp.concatenate(dvs, axis=1)

        @pl.when(i == nb)
        def _():
            tk_ref[...] = jnp.zeros_like(tk_ref)
            tv_ref[...] = jnp.zeros_like(tv_ref)

        @pl.when(i > 0)
        def _():
            kp = kp_ref[...]
            dkn = ck_ref[...] + tk_ref[0:ABLK, :]
            dkw = jnp.zeros((SUB, LANES), F32)
            dko = []
            for b in range(nkb):
                sl = slice(LANES * b, LANES * (b + 1))
                r = _normrope(kp[:, sl], kw_ref[...], *tp)[1]
                dx, wt = _normrope_bwd(dkn[:, sl], kp[:, sl], r, kw_ref[...], *tp)
                dko.append(dx)
                dkw = dkw + _fold8(wt)
            dkw_ref[...] += dkw
            dkv_ref[:, 0:A_KVW] = jnp.concatenate(dko, axis=1).astype(BF16)
            dkv_ref[:, A_KVW:2 * A_KVW] = (cv_ref[...] + tv_ref[0:ABLK, :]).astype(BF16)

        ck_ref[...] = tk_ref[ABLK:2 * ABLK, :]
        cv_ref[...] = tv_ref[ABLK:2 * ABLK, :]

    tab_specs = [BS((ABLK, LANES), cur)] * 3 + [BS((ABLK, LANES), prv)] * 3
    acc = BS((SUB, LANES), lambda b, i: (0, 0))
    return pl.pallas_call(
        body, name="attn_bwd", grid=(nseq, nb + 1),
        in_specs=[BS((ABLK, A_W), colblk(cur, A_W, P_Q)), BS((ABLK, A_W), colblk(cur, A_W, P_ZA)),
                  BS((ABLK, A_KVW), colblk(cur, A_KVW, P_K)), BS((ABLK, A_KVW), colblk(cur, A_KVW, P_V)),
                  BS((ABLK, A_KVW), colblk(prv, A_KVW, P_K)), BS((ABLK, A_KVW), colblk(prv, A_KVW, P_V)),
                  BS((ABLK, A_W), colblk(cur, A_W, Y_A)), BS((ABLK, A_W), cur), BS((ABLK, LANES), cur)]
        + tab_specs + [BS((1, LANES), lambda b, i: (0, 0))] * 2 + [BS(memory_space=pltpu.SMEM)],
        out_specs=[BS((ABLK, 2 * A_W), cur), BS((ABLK, 2 * A_KVW), prv), acc, acc, acc],
        out_shape=[S((n, P_W), BF16), S((n, 2 * A_KVW), BF16)] + [S((SUB, LANES), F32)] * 3,
        scratch_shapes=[pltpu.VMEM((2 * ABLK, A_KVW), F32)] * 2 + [pltpu.VMEM((ABLK, A_KVW), F32)] * 2,
        compiler_params=_cp("arbitrary", "arbitrary"),
    )(p, p, p, p, p, p, dy, o, lse, *tabs, *tabs, qw, kw, sinks)


def _put_cols(dst, src, col_off, tm=512):
    n, w = src.shape

    def body(s_ref, d_in_ref, d_ref):
        d_ref[...] = s_ref[...]

    return pl.pallas_call(
        body, name="put_cols", grid=(n // tm,),
        in_specs=[BS((tm, w), lambda i: (i, 0)), BS(memory_space=pl.ANY)],
        out_specs=BS((tm, w), lambda i: (i, col_off // w)),
        out_shape=S(dst.shape, dst.dtype), input_output_aliases={1: 0},
        compiler_params=_cp("arbitrary"),
    )(src, dst)


HALO_B = 32


def _layernorm(hc, lnw, lnb):
    mu = jnp.mean(hc, axis=-1, keepdims=True)
    xc = hc - mu
    rstd = lax.rsqrt(jnp.mean(xc * xc, axis=-1, keepdims=True) + EPS)
    xhat = xc * rstd
    return xhat, rstd, xhat * lnw + lnb


def _conf_fwd(p, y, cw, cb, lnw, lnb, pw, pwb, nseq, tm=256):
    n = p.shape[0]
    t = n // nseq
    nt = t // tm
    row = lambda b, i: b * nt + i
    halo = lambda b, i: jnp.maximum((b * t + i * tm) // HALO_B - 1, 0)
    vec = BS((1, B_W), lambda b, i: (0, 0))

    def body(ub_ref, uh_ref, zb_ref, cw_ref, cb_ref, lnw_ref, lnb_ref, pw_ref, pwb_ref, y_in_ref, y_ref, hc_ref, buf_ref):
        ub, uh = ub_ref[...], uh_ref[...]
        hh = uh[:, :B_W] * _sigmoid(uh[:, B_W:])
        buf_ref[0:HALO_B, :] = jnp.where(pl.program_id(1) > 0, hh, 0.0)
        buf_ref[HALO_B:, :] = ub[:, :B_W] * _sigmoid(ub[:, B_W:])
        hc = jnp.zeros((tm, B_W), F32) + cb_ref[...]
        for k in range(B_K):
            hc = hc + cw_ref[k:k + 1, :] * buf_ref[pl.ds(HALO_B - B_K + 1 + k, tm), :]
        hc_ref[...] = hc
        ln = _layernorm(hc, lnw_ref[...], lnb_ref[...])[2]
        sw = ln * _sigmoid(ln)
        ob = _dot(sw.astype(BF16), pw_ref[...]) + pwb_ref[...]
        zb = zb_ref[...]
        y_ref[...] = (ob * (zb * _sigmoid(zb))).astype(BF16)

    return pl.pallas_call(
        body, name="conf_fwd", grid=(nseq, nt),
        in_specs=[BS((tm, 2 * B_W), lambda b, i: (row(b, i), P_UB // (2 * B_W))),
                  BS((HALO_B, 2 * B_W), lambda b, i: (halo(b, i), P_UB // (2 * B_W))),
                  BS((tm, B_W), lambda b, i: (row(b, i), P_ZB // B_W)),
                  BS((HALO_B, B_W), lambda b, i: (0, 0)), vec, vec, vec, BS((B_W, B_W), lambda b, i: (0, 0)), vec,
                  BS(memory_space=pl.ANY)],
        out_specs=[BS((tm, B_W), lambda b, i: (row(b, i), Y_B // B_W)), BS((tm, B_W), lambda b, i: (row(b, i), 0))],
        out_shape=[S(y.shape, y.dtype), S((n, B_W), F32)], input_output_aliases={9: 0},
        scratch_shapes=[pltpu.VMEM((HALO_B + tm, B_W), F32)],
        compiler_params=_cp("arbitrary", "arbitrary"),
    )(p, p, p, cw, cb, lnw, lnb, pw, pwb, y)


def _conf_bwd1(p, dy, dp, hc, lnw, lnb, pw, pwb, tm=256):
    n = p.shape[0]
    vec = BS((1, B_W), lambda i: (0, 0))
    acc = BS((SUB, B_W), lambda i: (0, 0))

    def body(dy_ref, zb_ref, hc_ref, lnw_ref, lnb_ref, pw_ref, pwb_ref, dp_in_ref,
             dzb_ref, dhc_ref, dpw_ref, dpwb_ref, dlnw_ref, dlnb_ref, dcb_ref):
        @pl.when(pl.program_id(0) == 0)
        def _():
            for r in (dpw_ref, dpwb_ref, dlnw_ref, dlnb_ref, dcb_ref):
                r[...] = jnp.zeros_like(r)

        xhat, rstd, ln = _layernorm(hc_ref[...], lnw_ref[...], lnb_ref[...])
        sgl = _sigmoid(ln)
        sw = (ln * sgl).astype(BF16)
        ob = _dot(sw, pw_ref[...]) + pwb_ref[...]
        dy, zb = dy_ref[...], zb_ref[...]
        sgz = _sigmoid(zb)
        dzb_ref[...] = (dy * ob * _dsilu(zb, sgz)).astype(BF16)
        dob = dy * zb * sgz
        dobb = dob.astype(BF16)
        dpwb_ref[...] += _fold8(dob)
        dpw_ref[...] += _dot_tn(sw, dobb)
        dln = _dot_nt(dobb, pw_ref[...]) * _dsilu(ln, sgl)
        dlnw_ref[...] += _fold8(dln * xhat)
        dlnb_ref[...] += _fold8(dln)
        dxh = dln * lnw_ref[...]
        dhc = rstd * (dxh - jnp.mean(dxh, axis=-1, keepdims=True) - xhat * jnp.mean(dxh * xhat, axis=-1, keepdims=True))
        dcb_ref[...] += _fold8(dhc)
        dhc_ref[...] = dhc

    return pl.pallas_call(
        body, name="conf_bwd1", grid=(n // tm,),
        in_specs=[BS((tm, B_W), lambda i: (i, Y_B // B_W)), BS((tm, B_W), lambda i: (i, P_ZB // B_W)),
                  BS((tm, B_W), lambda i: (i, 0)), vec, vec, BS((B_W, B_W), lambda i: (0, 0)), vec,
                  BS(memory_space=pl.ANY)],
        out_specs=[BS((tm, B_W), lambda i: (i, P_ZB // B_W)), BS((tm, B_W), lambda i: (i, 0)),
                   BS((B_W, B_W), lambda i: (0, 0)), acc, acc, acc, acc],
        out_shape=[S(dp.shape, dp.dtype), S((n, B_W), F32), S((B_W, B_W), F32)] + [S((SUB, B_W), F32)] * 4,
        input_output_aliases={7: 0},
        compiler_params=_cp("arbitrary"),
    )(dy, p, hc, lnw, lnb, pw, pwb, dp)


def _conf_bwd2(p, dhc, dp, cw, nseq, tm=256):
    n = p.shape[0]
    t = n // nseq
    nt = t // tm
    row = lambda b, i: b * nt + i
    prev = lambda b, i: jnp.maximum((b * t + i * tm) // HALO_B - 1, 0)
    nxt = lambda b, i: jnp.minimum((b * t + (i + 1) * tm) // HALO_B, n // HALO_B - 1)

    def body(ub_ref, uh_ref, dh_ref, dn_ref, cw_ref, dp_in_ref, dub_ref, dcw_ref, buf_ref, dbuf_ref):
        i = pl.program_id(1)

        @pl.when((pl.program_id(0) == 0) & (i == 0))
        def _():
            dcw_ref[...] = jnp.zeros_like(dcw_ref)

        ub, uh = ub_ref[...], uh_ref[...]
        a, sg = ub[:, :B_W], _sigmoid(ub[:, B_W:])
        buf_ref[0:HALO_B, :] = jnp.where(i > 0, uh[:, :B_W] * _sigmoid(uh[:, B_W:]), 0.0)
        buf_ref[HALO_B:, :] = a * sg
        dhc = dh_ref[...]
        dbuf_ref[0:tm, :] = dhc
        dbuf_ref[tm:, :] = jnp.where(i < nt - 1, dn_ref[...], 0.0)
        dhg = jnp.zeros((tm, B_W), F32)
        for k in range(B_K):
            dhg = dhg + cw_ref[k:k + 1, :] * dbuf_ref[pl.ds(B_K - 1 - k, tm), :]
            dcw_ref[SUB * k:SUB * (k + 1), :] += _fold8(dhc * buf_ref[pl.ds(HALO_B - B_K + 1 + k, tm), :])
        dub_ref[...] = jnp.concatenate([dhg * sg, dhg * a * sg * (1.0 - sg)], axis=1).astype(BF16)

    return pl.pallas_call(
        body, name="conf_bwd2", grid=(nseq, nt),
        in_specs=[BS((tm, 2 * B_W), lambda b, i: (row(b, i), P_UB // (2 * B_W))),
                  BS((HALO_B, 2 * B_W), lambda b, i: (prev(b, i), P_UB // (2 * B_W))),
                  BS((tm, B_W), lambda b, i: (row(b, i), 0)), BS((HALO_B, B_W), lambda b, i: (nxt(b, i), 0)),
                  BS((HALO_B, B_W), lambda b, i: (0, 0)), BS(memory_space=pl.ANY)],
        out_specs=[BS((tm, 2 * B_W), lambda b, i: (row(b, i), P_UB // (2 * B_W))),
                   BS((SUB * B_K, B_W), lambda b, i: (0, 0))],
        out_shape=[S(dp.shape, dp.dtype), S((SUB * B_K, B_W), F32)], input_output_aliases={5: 0},
        scratch_shapes=[pltpu.VMEM((HALO_B + tm, B_W), F32)] * 2,
        compiler_params=_cp("arbitrary", "arbitrary"),
    )(p, p, dhc, dhc, cw, dp)


HALO_C = 8
QS = C_DH ** -0.5
NCB = 3 * C_HEADS
CB0 = P_QKV // LANES
ZC0 = P_ZC // LANES
GB, GG = 0, C_HEADS


def _softplus(z):
    return jnp.maximum(z, 0.0) + jnp.log(1.0 + jnp.exp(-jnp.abs(z)))


def _gdn_gates_fwd(p, alog_l, dtb_l, tm=256):
    n = p.shape[0]

    def body(ba_ref, al_ref, db_ref, o_ref):
        blk = ba_ref[...]
        lane = _lane(blk.shape)
        g = jnp.where((lane >= GG) & (lane < GG + C_HEADS), -jnp.exp(al_ref[...]) * _softplus(blk + db_ref[...]), 0.0)
        tri = (_subl((CHUNK, CHUNK)) >= _lane((CHUNK, CHUNK))).astype(F32)
        gc = jnp.concatenate([_dot(tri, g[CHUNK * c:CHUNK * (c + 1)], HI) for c in range(tm // CHUNK)], axis=0)
        o_ref[...] = jnp.where(lane < GG, _sigmoid(blk), gc)

    return pl.pallas_call(
        body, name="gdn_gates_fwd", grid=(n // tm,),
        in_specs=[BS((tm, LANES), lambda i: (i, P_BA // LANES)), BS((1, LANES), lambda i: (0, 0)), BS((1, LANES), lambda i: (0, 0))],
        out_specs=BS((tm, LANES), lambda i: (i, 0)), out_shape=S((n, LANES), F32),
        compiler_params=_cp("arbitrary"),
    )(p, alog_l, dtb_l)


def _gdn_pre_fwd(p, ccw, nseq, tm=256):
    n = p.shape[0]
    t = n // nseq
    nt = t // tm
    row = lambda b, i: b * nt + i
    halo = lambda b, i: jnp.maximum((b * t + i * tm) // HALO_C - 1, 0)

    def body(x_ref, xh_ref, w_ref, xc_ref, o_ref, buf_ref):
        part = pl.program_id(2) // C_HEADS
        buf_ref[0:HALO_C, :] = jnp.where(pl.program_id(1) > 0, xh_ref[...], 0.0)
        buf_ref[HALO_C:, :] = x_ref[...]
        xc = jnp.zeros((tm, LANES), F32)
        for k in range(C_K):
            xc = xc + w_ref[k:k + 1, :] * buf_ref[pl.ds(HALO_C - C_K + 1 + k, tm), :]
        xc_ref[...] = xc
        act = xc * _sigmoid(xc)
        rn = lax.rsqrt(jnp.sum(act * act, axis=-1, keepdims=True) + EPS)
        o_ref[...] = act * jnp.where(part == 2, 1.0, rn * jnp.where(part == 0, QS, 1.0))

    return pl.pallas_call(
        body, name="gdn_pre_fwd", grid=(nseq, nt, NCB),
        in_specs=[BS((tm, LANES), lambda b, i, c: (row(b, i), CB0 + c)), BS((HALO_C, LANES), lambda b, i, c: (halo(b, i), CB0 + c)),
                  BS((SUB, LANES), lambda b, i, c: (0, c))],
        out_specs=[BS((tm, LANES), lambda b, i, c: (row(b, i), c))] * 2,
        out_shape=[S((n, 3 * C_W), F32)] * 2,
        scratch_shapes=[pltpu.VMEM((HALO_C + tm, LANES), F32)],
        compiler_params=_cp("arbitrary", "arbitrary", "arbitrary"),
    )(p, p, ccw)


def _chunk_common(q, k, gt, h):
    beta = _col(gt, GB + h)
    gc = _col(gt, GG + h)
    gtt = gt.T
    gcr = jnp.sum(jnp.where(_subl(gtt.shape) == GG + h, gtt, 0.0), axis=0, keepdims=True)
    ii, jj = _subl((CHUNK, CHUNK)), _lane((CHUNK, CHUNK))
    incl, strict = ii >= jj, ii > jj
    dec = jnp.exp(jnp.where(incl, gc - gcr, -jnp.inf))
    kb = k * beta
    kbf = k.astype(BF16)
    a = jnp.where(strict, _dot_nt(kb.astype(BF16), kbf) * dec, 0.0)
    mq = jnp.where(incl, _dot_nt(q.astype(BF16), kbf) * dec, 0.0)
    glast = jnp.sum(jnp.where(_subl(gc.shape) == CHUNK - 1, gc, 0.0), axis=0, keepdims=True)
    return beta, gc, incl, strict, dec, kb, a, mq, glast


def _unit_lower_inverse(a):
    eye = (_subl(a.shape) == _lane(a.shape)).astype(F32)
    m = -a
    inv = eye + m
    for _ in range(5):
        m = _dot(m, m, HI)
        inv = inv + _dot(inv, m, HI)
    return inv


def _gdn_chunk_fwd(qkv, gates, p, y, onw, nseq):
    n = qkv.shape[0]
    t = n // nseq
    nch = t // CHUNK

    def body(q_ref, k_ref, v_ref, g_ref, zc_ref, onw_ref, y_in_ref, y_ref, o_ref, u_ref, w_ref, t_ref, ss_ref, s_scr):
        h = pl.program_id(1)
        s_scr[...] = jnp.zeros_like(s_scr)

        def step(c, carry):
            rows = pl.ds(pl.multiple_of(c * CHUNK, CHUNK), CHUNK)
            q, k, v, gt = q_ref[rows, :], k_ref[rows, :], v_ref[rows, :], g_ref[rows, :]
            beta, gc, incl, strict, dec, kb, a, mq, glast = _chunk_common(q, k, gt, h)
            tm_ = _unit_lower_inverse(a)
            egc = jnp.exp(gc)
            sol = _dot(tm_, jnp.concatenate([v * beta, kb * egc], axis=1), HI)
            u, w = sol[:, :C_DH], sol[:, C_DH:]
            sv = s_scr[...]
            ss_ref[c] = sv
            sb = sv.astype(BF16)
            vn = u - _dot(w.astype(BF16), sb)
            vnb = vn.astype(BF16)
            o = _dot((q * egc).astype(BF16), sb) + _dot(mq.astype(BF16), vnb)
            s_scr[...] = sv * jnp.exp(glast) + _dot_tn((k * jnp.exp(glast - gc)).astype(BF16), vnb)
            o_ref[rows, :] = o
            u_ref[rows, :] = u
            w_ref[rows, :] = w
            t_ref[rows, :] = jnp.concatenate([tm_, jnp.zeros_like(tm_)], axis=1)
            zc = zc_ref[rows, :]
            r = lax.rsqrt(jnp.mean(o * o, axis=-1, keepdims=True) + EPS)
            y_ref[rows, :] = (o * r * onw_ref[...] * (zc * _sigmoid(zc))).astype(BF16)
            return carry

        lax.fori_loop(0, nch, step, 0)

    hb = lambda off: BS((t, LANES), lambda b, h: (b, off + h))
    return pl.pallas_call(
        body, name="gdn_chunk_fwd", grid=(nseq, C_HEADS),
        in_specs=[hb(0), hb(C_HEADS), hb(2 * C_HEADS), BS((t, LANES), lambda b, h: (b, 0)), hb(ZC0),
                  BS((1, LANES), lambda b, h: (0, 0)), BS(memory_space=pl.ANY)],
        out_specs=[hb(Y_C // LANES), hb(0), hb(0), hb(0), hb(0),
                   BS((None, None, nch, C_DH, C_DH), lambda b, h: (b, h, 0, 0, 0))],
        out_shape=[S(y.shape, y.dtype)] + [S((n, C_W), F32)] * 4 + [S((nseq, C_HEADS, nch, C_DH, C_DH), F32)],
        input_output_aliases={6: 0},
        scratch_shapes=[pltpu.VMEM((C_DH, C_DH), F32)],
        compiler_params=_cp("arbitrary", "arbitrary"),
    )(qkv, qkv, qkv, gates, p, onw, y)


def _gdn_chunk_bwd(qkv, gates, p, dy, dp, onw, o, u, w, tinv, ss, nseq):
    n = qkv.shape[0]
    t = n // nseq
    nch = t // CHUNK

    def body(q_ref, k_ref, v_ref, g_ref, zc_ref, onw_ref, o_ref, dy_ref, u_ref, w_ref, t_ref, ss_ref, dp_in_ref,
             dzc_ref, dqkv_ref, dg_ref, donw_ref, ds_scr):
        h = pl.program_id(1)
        ds_scr[...] = jnp.zeros_like(ds_scr)

        @pl.when((pl.program_id(0) == 0) & (h == 0))
        def _():
            donw_ref[...] = jnp.zeros_like(donw_ref)

        def rsum(x):
            return jnp.sum(x, axis=-1, keepdims=True)

        def step(ci, carry):
            c = nch - 1 - ci
            rows = pl.ds(pl.multiple_of(c * CHUNK, CHUNK), CHUNK)
            q, k, v, gt = q_ref[rows, :], k_ref[rows, :], v_ref[rows, :], g_ref[rows, :]
            zc, o, dy, u, w = zc_ref[rows, :], o_ref[rows, :], dy_ref[rows, :], u_ref[rows, :], w_ref[rows, :]
            tm_ = t_ref[rows, :][:, 0:CHUNK]
            sv, dsv = ss_ref[c], ds_scr[...]
            sb, dsb = sv.astype(BF16), dsv.astype(BF16)
            sg = _sigmoid(zc)
            r = lax.rsqrt(jnp.mean(o * o, axis=-1, keepdims=True) + EPS)
            on = o * r
            ow = onw_ref[...]
            dzc_ref[rows, :] = (dy * on * ow * _dsilu(zc, sg)).astype(BF16)
            t1 = dy * zc * sg
            donw_ref[...] += _fold8(t1 * on)
            don = t1 * ow
            do = r * (don - on * jnp.mean(don * on, axis=-1, keepdims=True))
            dob = do.astype(BF16)
            beta, gc, incl, strict, dec, kb, a, mq, glast = _chunk_common(q, k, gt, h)
            egc = jnp.exp(gc)
            gl = jnp.exp(glast)
            ekd = jnp.exp(glast - gc)
            wb = w.astype(BF16)
            vnb = (u - _dot(wb, sb)).astype(BF16)
            qg = q * egc
            dvn = _dot_tn(mq.astype(BF16), dob) + _dot((k * ekd).astype(BF16), dsb)
            dvnb = dvn.astype(BF16)
            dqg = _dot_nt(dob, sb)
            dmq = jnp.where(incl, _dot_nt(dob, vnb), 0.0)
            dkd = _dot_nt(vnb, dsb)
            dgl = jnp.sum(rsum(dsv * sv), axis=0, keepdims=True)
            dw = -_dot_nt(dvnb, sb)
            ds_scr[...] = gl * dsv + _dot_tn(qg.astype(BF16), dob) - _dot_tn(wb, dvnb)
            db = _dot_tn(tm_, jnp.concatenate([dvn, dw], axis=1), HI)
            dbv, dbk = db[:, :C_DH], db[:, C_DH:]
            da = -jnp.where(strict, _dot_nt(dbv, u, HI) + _dot_nt(dbk, w, HI), 0.0)
            e = da * a + dmq * mq
            dgc = rsum(e) - rsum(e.T)
            dgb, dhb, kbf = (da * dec).astype(BF16), (dmq * dec).astype(BF16), k.astype(BF16)
            dkb = _dot(dgb, kbf)
            tk = rsum(dbk * k)
            rk = rsum(dkd * k) * ekd
            dq = _dot(dhb, kbf) + egc * dqg
            dk = _dot_tn(dgb, kb.astype(BF16)) + _dot_tn(dhb, q.astype(BF16)) + beta * (egc * dbk + dkb) + ekd * dkd
            dbeta = rsum(dbv * v) + tk * egc + rsum(dkb * k)
            dgc = dgc + tk * beta * egc + egc * rsum(dqg * q) - rk
            dglast = jnp.sum(rk, axis=0, keepdims=True) + dgl * gl
            dgc = dgc + jnp.where(_subl(dgc.shape) == CHUNK - 1, dglast, 0.0)
            dqkv_ref[0, rows, :] = dq
            dqkv_ref[1, rows, :] = dk
            dqkv_ref[2, rows, :] = beta * dbv
            lane = _lane((CHUNK, LANES))
            dg_ref[rows, :] = jnp.where(lane == 0, dbeta, jnp.where(lane == 1, dgc, 0.0))
            return carry

        lax.fori_loop(0, nch, step, 0)

    hb = lambda off: BS((t, LANES), lambda b, h: (b, off + h))
    return pl.pallas_call(
        body, name="gdn_chunk_bwd", grid=(nseq, C_HEADS),
        in_specs=[hb(0), hb(C_HEADS), hb(2 * C_HEADS), BS((t, LANES), lambda b, h: (b, 0)), hb(ZC0),
                  BS((1, LANES), lambda b, h: (0, 0)), hb(0), hb(Y_C // LANES), hb(0), hb(0), hb(0),
                  BS((None, None, nch, C_DH, C_DH), lambda b, h: (b, h, 0, 0, 0)), BS(memory_space=pl.ANY)],
        out_specs=[hb(ZC0), BS((3, t, LANES), lambda b, h: (0, b, h)), BS((None, t, LANES), lambda b, h: (h, b, 0)),
                   BS((SUB, LANES), lambda b, h: (0, 0))],
        out_shape=[S(dp.shape, dp.dtype), S((3, n, C_W), F32), S((C_HEADS, n, LANES), F32), S((SUB, LANES), F32)],
        input_output_aliases={12: 0},
        scratch_shapes=[pltpu.VMEM((C_DH, C_DH), F32)],
        compiler_params=_cp("arbitrary", "arbitrary"),
    )(qkv, qkv, qkv, gates, p, onw, o, dy, u, w, tinv, ss, dp)


def _gdn_gates_bwd(dgate, p, alog_l, dtb_l, dp, tm=256):
    n = p.shape[0]
    acc = BS((SUB, LANES), lambda i: (0, 0))

    def body(dg_ref, ba_ref, al_ref, db_ref, dp_in_ref, dba_ref, dal_ref, ddb_ref):
        @pl.when(pl.program_id(0) == 0)
        def _():
            dal_ref[...] = jnp.zeros_like(dal_ref)
            ddb_ref[...] = jnp.zeros_like(ddb_ref)

        blk = ba_ref[...]
        lane = _lane(blk.shape)
        dbeta = jnp.zeros_like(blk)
        dgc = jnp.zeros_like(blk)
        for h in range(C_HEADS):
            dbeta = dbeta + jnp.where(lane == GB + h, _col(dg_ref[h], 0), 0.0)
            dgc = dgc + jnp.where(lane == GG + h, _col(dg_ref[h], 1), 0.0)
        tri = (_subl((CHUNK, CHUNK)) <= _lane((CHUNK, CHUNK))).astype(F32)
        dg = jnp.concatenate([_dot(tri, dgc[CHUNK * c:CHUNK * (c + 1)], HI) for c in range(tm // CHUNK)], axis=0)
        beta = _sigmoid(blk)
        z = blk + db_ref[...]
        ea = jnp.exp(al_ref[...])
        isg = (lane >= GG) & (lane < GG + C_HEADS)
        dz = jnp.where(isg, -dg * ea * _sigmoid(z), 0.0)
        dal_ref[...] += _fold8(jnp.where(isg, -dg * ea * _softplus(z), 0.0))
        ddb_ref[...] += _fold8(dz)
        out = jnp.where(lane < GG, dbeta * beta * (1.0 - beta), dz)
        dba_ref[...] = jnp.concatenate([out, jnp.zeros_like(out)], axis=1).astype(BF16)

    return pl.pallas_call(
        body, name="gdn_gates_bwd", grid=(n // tm,),
        in_specs=[BS((C_HEADS, tm, LANES), lambda i: (0, i, 0)), BS((tm, LANES), lambda i: (i, P_BA // LANES)),
                  BS((1, LANES), lambda i: (0, 0)), BS((1, LANES), lambda i: (0, 0)), BS(memory_space=pl.ANY)],
        out_specs=[BS((tm, 2 * LANES), lambda i: (i, P_BA // (2 * LANES))), acc, acc],
        out_shape=[S(dp.shape, dp.dtype), S((SUB, LANES), F32), S((SUB, LANES), F32)],
        input_output_aliases={4: 0},
        compiler_params=_cp("arbitrary"),
    )(dgate, p, alog_l, dtb_l, dp)


def _gdn_pre_bwd1(dqkv, xc, tm=256):
    n = xc.shape[0]

    def body(d_ref, xc_ref, o_ref):
        part = pl.program_id(1) // C_HEADS
        xc, d = xc_ref[...], d_ref[...]
        sg = _sigmoid(xc)
        act = xc * sg
        rn = lax.rsqrt(jnp.sum(act * act, axis=-1, keepdims=True) + EPS)
        cs = jnp.where(part == 0, QS, 1.0)
        dn = cs * rn * d - act * (cs * rn * rn * rn * jnp.sum(d * act, axis=-1, keepdims=True))
        o_ref[...] = jnp.where(part == 2, d, dn) * _dsilu(xc, sg)

    return pl.pallas_call(
        body, name="gdn_pre_bwd1", grid=(n // tm, NCB),
        in_specs=[BS((None, tm, LANES), lambda i, c: (c // C_HEADS, i, c % C_HEADS)), BS((tm, LANES), lambda i, c: (i, c))],
        out_specs=BS((tm, LANES), lambda i, c: (i, c)), out_shape=S((n, 3 * C_W), F32),
        compiler_params=_cp("arbitrary", "arbitrary"),
    )(dqkv, xc)


def _gdn_pre_bwd2(p, dxc, dp, ccw, nseq, tm=256):
    n = p.shape[0]
    t = n // nseq
    nt = t // tm
    row = lambda b, i: b * nt + i
    prev = lambda b, i: jnp.maximum((b * t + i * tm) // HALO_C - 1, 0)
    nxt = lambda b, i: jnp.minimum((b * t + (i + 1) * tm) // HALO_C, n // HALO_C - 1)

    def body(x_ref, xh_ref, d_ref, dn_ref, w_ref, dp_in_ref, dx_ref, dw_ref, buf_ref, dbuf_ref):
        i = pl.program_id(2)

        @pl.when((pl.program_id(1) == 0) & (i == 0))
        def _():
            dw_ref[...] = jnp.zeros_like(dw_ref)

        buf_ref[0:HALO_C, :] = jnp.where(i > 0, xh_ref[...], 0.0)
        buf_ref[HALO_C:, :] = x_ref[...]
        d = d_ref[...]
        dbuf_ref[0:tm, :] = d
        dbuf_ref[tm:, :] = jnp.where(i < nt - 1, dn_ref[...], 0.0)
        dx = jnp.zeros((tm, LANES), F32)
        for k in range(C_K):
            dx = dx + w_ref[k:k + 1, :] * dbuf_ref[pl.ds(C_K - 1 - k, tm), :]
            dw_ref[SUB * k:SUB * (k + 1), :] += _fold8(d * buf_ref[pl.ds(HALO_C - C_K + 1 + k, tm), :])
        dx_ref[...] = dx.astype(BF16)

    return pl.pallas_call(
        body, name="gdn_pre_bwd2", grid=(NCB, nseq, nt),
        in_specs=[BS((tm, LANES), lambda c, b, i: (row(b, i), CB0 + c)), BS((HALO_C, LANES), lambda c, b, i: (prev(b, i), CB0 + c)),
                  BS((tm, LANES), lambda c, b, i: (row(b, i), c)), BS((HALO_C, LANES), lambda c, b, i: (nxt(b, i), c)),
                  BS((SUB, LANES), lambda c, b, i: (0, c)), BS(memory_space=pl.ANY)],
        out_specs=[BS((tm, LANES), lambda c, b, i: (row(b, i), CB0 + c)), BS((SUB * C_K, LANES), lambda c, b, i: (0, c))],
        out_shape=[S(dp.shape, dp.dtype), S((SUB * C_K, 3 * C_W), F32)], input_output_aliases={5: 0},
        scratch_shapes=[pltpu.VMEM((HALO_C + tm, LANES), F32)] * 2,
        compiler_params=_cp("arbitrary", "arbitrary", "arbitrary"),
    )(p, p, dxc, dxc, ccw, dp)


ANY = BS(memory_space=pl.ANY)


def _my_pos():
    return lax.axis_index("x"), lax.axis_index("y"), lax.axis_index("c")


def _dev_index(dev):
    return 4 * dev[0] + 2 * dev[1] + dev[2]


def _all_gather(shards):
    nk = len(shards)

    def body(*refs):
        ins, outs = refs[:nk], refs[nk:2 * nk]
        send, recv, loc = refs[2 * nk:]
        x, y, c = _my_pos()
        me, sib = (x, y, c), (x, y, 1 - c)
        chips = [(1 - x, y), (x, 1 - y), (1 - x, 1 - y)]

        def rows(t, dev):
            r = ins[t].shape[0]
            return outs[t].at[pl.ds(pl.multiple_of(_dev_index(dev) * r, SUB), r), :]

        def copy(t, k, block, to, src=None):
            return pltpu.make_async_remote_copy(
                src_ref=rows(t, block) if src is None else src, dst_ref=rows(t, block),
                send_sem=send.at[t, k], recv_sem=recv.at[t, k], device_id=to, device_id_type=MESH)

        mine = [pltpu.make_async_copy(ins[t], rows(t, me), loc.at[t]) for t in range(nk)]
        for cp in mine:
            cp.start()
        first = []
        for t in range(nk):
            first.append(copy(t, 0, me, sib, src=ins[t]))
            first += [copy(t, 1 + j, me, (*chip, c), src=ins[t]) for j, chip in enumerate(chips)]
        for cp in first:
            cp.start()
        passed = []
        for j, chip in enumerate(chips):
            for t in range(nk):
                copy(t, 1 + j, (*chip, c), me).wait_recv()
                cp = copy(t, 4 + j, (*chip, c), sib)
                cp.start()
                passed.append(cp)
        for t in range(nk):
            copy(t, 0, sib, me).wait_recv()
            for j, chip in enumerate(chips):
                copy(t, 4 + j, (*chip, 1 - c), me).wait_recv()
        for cp in first + passed:
            cp.wait_send()
        for cp in mine:
            cp.wait()

    return pl.pallas_call(
        body, name="all_gather", in_specs=[ANY] * nk, out_specs=[ANY] * nk,
        out_shape=[S((N_DEV * a.shape[0], a.shape[1]), a.dtype) for a in shards],
        scratch_shapes=[pltpu.SemaphoreType.DMA((nk, 7)), pltpu.SemaphoreType.DMA((nk, 7)), pltpu.SemaphoreType.DMA((nk,))],
    )(*shards)


def _scatter_blocks(parts):
    nk = len(parts)

    def body(*refs):
        ins, outs = refs[:nk], refs[nk:2 * nk]
        send, recv, loc = refs[2 * nk:]
        x, y, c = _my_pos()
        me = _dev_index((x, y, c))
        peers = [((1 - x) if k & 4 else x, (1 - y) if k & 2 else y, (1 - c) if k & 1 else c) for k in range(1, N_DEV)]

        def block(t, dev):
            r = ins[t].shape[0] // N_DEV
            return ins[t].at[pl.ds(pl.multiple_of(_dev_index(dev) * r, SUB), r), :]

        mine = [pltpu.make_async_copy(block(t, (x, y, c)), outs[t].at[me], loc.at[t]) for t in range(nk)]
        for cp in mine:
            cp.start()
        sent = []
        for t in range(nk):
            for k, peer in enumerate(peers):
                cp = pltpu.make_async_remote_copy(src_ref=block(t, peer), dst_ref=outs[t].at[me], send_sem=send.at[t, k],
                                                  recv_sem=recv.at[t, k], device_id=peer, device_id_type=MESH)
                cp.start()
                sent.append(cp)
        for t in range(nk):
            for k, peer in enumerate(peers):
                pltpu.make_async_remote_copy(src_ref=block(t, peer), dst_ref=outs[t].at[_dev_index(peer)], send_sem=send.at[t, k],
                                             recv_sem=recv.at[t, k], device_id=peer, device_id_type=MESH).wait_recv()
        for cp in sent:
            cp.wait_send()
        for cp in mine:
            cp.wait()

    return pl.pallas_call(
        body, name="scatter_blocks", in_specs=[ANY] * nk, out_specs=[ANY] * nk,
        out_shape=[S((N_DEV, a.shape[0] // N_DEV, a.shape[1]), a.dtype) for a in parts],
        scratch_shapes=[pltpu.SemaphoreType.DMA((nk, 7)), pltpu.SemaphoreType.DMA((nk, 7)), pltpu.SemaphoreType.DMA((nk,))],
    )(*parts)


BLOCK_BYTES = 4 << 20


def _row_tile(rows, row_bytes, align):
    best = align
    for tr in range(align, rows + 1, align):
        if rows % tr == 0 and tr * row_bytes <= BLOCK_BYTES:
            best = tr
    return best


def _sum8(a):
    _, r, w = a.shape
    tr = _row_tile(r, N_DEV * w * a.dtype.itemsize, 32 // a.dtype.itemsize)

    def body(a_ref, o_ref):
        acc = a_ref[0].astype(F32)
        for d in range(1, N_DEV):
            acc = acc + a_ref[d].astype(F32)
        o_ref[...] = acc

    return pl.pallas_call(
        body, name="sum8", grid=(r // tr,), in_specs=[BS((N_DEV, tr, w), lambda i: (0, i, 0))],
        out_specs=BS((tr, w), lambda i: (i, 0)), out_shape=S((r, w), F32), compiler_params=_cp("arbitrary"),
    )(a)


def _adamw(w, g, m, v):
    r, c = w.shape
    tr = _row_tile(r, c * 4 * 2, SUB)

    def body(w_ref, g_ref, m_ref, v_ref, d_ref, mo_ref, vo_ref):
        gv = g_ref[...]
        m2 = ADAM_B1 * m_ref[...] + (1.0 - ADAM_B1) * gv
        v2 = ADAM_B2 * v_ref[...] + (1.0 - ADAM_B2) * (gv * gv)
        m_hat = m2 / (1.0 - ADAM_B1 ** ADAM_STEP)
        v_hat = v2 / (1.0 - ADAM_B2 ** ADAM_STEP)
        d_ref[...] = -ADAM_LR * (m_hat / (jnp.sqrt(v_hat) + ADAM_EPS) + ADAM_WD * w_ref[...])
        mo_ref[...] = m2
        vo_ref[...] = v2

    blk = BS((tr, c), lambda i: (i, 0))
    return pl.pallas_call(
        body, name="adamw", grid=(r // tr,), in_specs=[blk] * 4, out_specs=[blk] * 3,
        out_shape=[S((r, c), F32)] * 3, compiler_params=_cp("arbitrary"),
    )(w, g, m, v)


def _blob(arrays):
    flat = jnp.concatenate([a.reshape(-1) for a in arrays])
    rows = -(-flat.shape[0] // (SUB * LANES)) * SUB
    return jnp.pad(flat, (0, rows * LANES - flat.shape[0])).reshape(rows, LANES)


def _unblob(blob, shapes, lead=()):
    flat = blob.reshape(lead + (-1,))
    out, off = [], 0
    for s in shapes:
        size = math.prod(s)
        out.append(flat[..., off:off + size].reshape(lead + tuple(s)))
        off += size
    return out


def _lanes6(a):
    return jnp.zeros((1, LANES), F32).at[0, GG:GG + C_HEADS].set(a)


def _y_rows(w):
    return jnp.concatenate([w[0:A_W], w[A_W + B_W:], w[A_W:A_W + B_W]], axis=0)


def _y_rows_back(g):
    return jnp.concatenate([g[0:A_W], g[A_W + C_W:], g[A_W:A_W + C_W]], axis=0)


SMALL = ("norm_w", "q_norm_w", "k_norm_w", "sinks", "b_conv_b", "b_ln_w", "b_ln_b", "b_pw_b", "c_a_log", "c_dt_bias",
         "c_onorm_w", "b_conv_w", "c_conv_w")
ORDER = ("norm_w", "w_in", "q_norm_w", "k_norm_w", "sinks", "b_conv_w", "b_conv_b", "b_ln_w", "b_ln_b", "b_pw_w", "b_pw_b",
         "c_conv_w", "c_a_log", "c_dt_bias", "c_onorm_w", "w_out")


def kernel(x, positions, norm_w, w_in, q_norm_w, k_norm_w, sinks, b_conv_w, b_conv_b, b_ln_w, b_ln_b, b_pw_w, b_pw_b, c_conv_w, c_a_log, c_dt_bias, c_onorm_w, w_out, loss_target, m_norm_w, m_w_in, m_q_norm_w, m_k_norm_w, m_sinks, m_b_conv_w, m_b_conv_b, m_b_ln_w, m_b_ln_b, m_b_pw_w, m_b_pw_b, m_c_conv_w, m_c_a_log, m_c_dt_bias, m_c_onorm_w, m_w_out, v_norm_w, v_w_in, v_q_norm_w, v_k_norm_w, v_sinks, v_b_conv_w, v_b_conv_b, v_b_ln_w, v_b_ln_b, v_b_pw_w, v_b_pw_b, v_c_conv_w, v_c_a_log, v_c_dt_bias, v_c_onorm_w, v_w_out):
    W = dict(norm_w=norm_w, w_in=w_in, q_norm_w=q_norm_w, k_norm_w=k_norm_w, sinks=sinks, b_conv_w=b_conv_w, b_conv_b=b_conv_b,
             b_ln_w=b_ln_w, b_ln_b=b_ln_b, b_pw_w=b_pw_w, b_pw_b=b_pw_b, c_conv_w=c_conv_w, c_a_log=c_a_log,
             c_dt_bias=c_dt_bias, c_onorm_w=c_onorm_w, w_out=w_out)
    M = dict(norm_w=m_norm_w, w_in=m_w_in, q_norm_w=m_q_norm_w, k_norm_w=m_k_norm_w, sinks=m_sinks, b_conv_w=m_b_conv_w,
             b_conv_b=m_b_conv_b, b_ln_w=m_b_ln_w, b_ln_b=m_b_ln_b, b_pw_w=m_b_pw_w, b_pw_b=m_b_pw_b, c_conv_w=m_c_conv_w,
             c_a_log=m_c_a_log, c_dt_bias=m_c_dt_bias, c_onorm_w=m_c_onorm_w, w_out=m_w_out)
    V = dict(norm_w=v_norm_w, w_in=v_w_in, q_norm_w=v_q_norm_w, k_norm_w=v_k_norm_w, sinks=v_sinks, b_conv_w=v_b_conv_w,
             b_conv_b=v_b_conv_b, b_ln_w=v_b_ln_w, b_ln_b=v_b_ln_b, b_pw_w=v_b_pw_w, b_pw_b=v_b_pw_b, c_conv_w=v_c_conv_w,
             c_a_log=v_c_a_log, c_dt_bias=v_c_dt_bias, c_onorm_w=v_c_onorm_w, w_out=v_w_out)
    nseq, t, d = x.shape
    n = nseq * t
    tr = min(256, t)
    tmm = min(512, n)
    me = _dev_index(_my_pos())
    xs = [x.reshape(n, d)]
    tgt = loss_target.reshape(n, d)
    tabs = _rope_tables(positions.reshape(n))

    win_p = _pack_cols(w_in).astype(BF16)
    wout_b = w_out.astype(BF16)
    sharded_small = (b_pw_w, b_conv_w, c_conv_w)
    g_win0, g_win1, g_wout0, g_wout1, g_small = _all_gather(
        [win_p[0], win_p[1], wout_b[0], wout_b[1], _blob(sharded_small)])
    win = [g_win0, g_win1]
    wout = [_y_rows(g_wout0), _y_rows(g_wout1)]
    pw_all, cw_all, ccw_all = _unblob(g_small, [a.shape for a in sharded_small], lead=(N_DEV,))
    pw_all = pw_all.transpose(1, 0, 2, 3).reshape(DEPTH, B_W, B_W).astype(BF16)
    cw_all = cw_all.transpose(1, 2, 0, 3).reshape(DEPTH, B_K, B_W)
    ccw_all = ccw_all.transpose(1, 2, 0, 3).reshape(DEPTH, C_K, 3 * C_W)

    def layer_params(l):
        return dict(
            nw=norm_w[l][None], qw=jnp.tile(q_norm_w[l], 2)[None], kw=jnp.tile(k_norm_w[l], 2)[None], sinks=sinks[l],
            cw=jnp.pad(cw_all[l], ((0, HALO_B - B_K), (0, 0))), cb=b_conv_b[l][None], lnw=b_ln_w[l][None], lnb=b_ln_b[l][None],
            pw=pw_all[l], pwb=b_pw_b[l][None], ccw=jnp.pad(ccw_all[l], ((0, SUB - C_K), (0, 0))),
            alog=_lanes6(c_a_log[l]), dtb=_lanes6(c_dt_bias[l]), onw=c_onorm_w[l][None])

    saved = []
    for l in range(DEPTH):
        q = layer_params(l)
        p, h = _inproj(xs[l], q["nw"], win[l], tm=tmm)
        y, o_a, lse = _attn_fwd(p, tabs, q["qw"], q["kw"], q["sinks"], nseq)
        gates = _gdn_gates_fwd(p, q["alog"], q["dtb"], tm=tr)
        xc, qkv = _gdn_pre_fwd(p, q["ccw"], nseq, tm=tr)
        y, o_c, u, w, tinv, ss = _gdn_chunk_fwd(qkv, gates, p, y, q["onw"], nseq)
        y, hc = _conf_fwd(p, y, q["cw"], q["cb"], q["lnw"], q["lnb"], q["pw"], q["pwb"], nseq, tm=tr)
        saved.append(dict(q=q, p=p, h=h, y=y, o_a=o_a, lse=lse, gates=gates, xc=xc, qkv=qkv, o_c=o_c, u=u, w=w, tinv=tinv,
                          ss=ss, hc=hc))
        if l + 1 < DEPTH:
            xs.append(_outproj(xs[l], y, wout[l], tm=tmm))
        else:
            dxn, lsum = _outproj_loss(xs[l], y, wout[l], tgt, tm=tmm)
    loss = lax.psum(jnp.sum(lsum) * (0.5 / d), ("x", "y", "c"))

    dwin, dwout, dpw, smalls = [None] * DEPTH, [None] * DEPTH, [None] * DEPTH, [None] * DEPTH
    for l in reversed(range(DEPTH)):
        s = saved[l]
        q, p = s["q"], s["p"]
        dy = _matmul(dxn, wout[l], "nt", F32, tmm, 512, d, "outproj_bwd_dy")
        dwout[l] = _y_rows_back(_matmul(s["y"], dxn, "tn", BF16, 1024, 1024, tmm, "outproj_bwd_dw"))
        dp, dkv, dqw, dkw, dsk = _attn_bwd(p, dy, s["o_a"], s["lse"], tabs, q["qw"], q["kw"], q["sinks"], nseq)
        dp = _put_cols(dp, dkv, P_K, tm=tmm)
        dp, dqkv, dgate, donw = _gdn_chunk_bwd(s["qkv"], s["gates"], p, dy, dp, q["onw"], s["o_c"], s["u"], s["w"],
                                               s["tinv"], s["ss"], nseq)
        dp, dal, ddb = _gdn_gates_bwd(dgate, p, q["alog"], q["dtb"], dp, tm=tr)
        dxc = _gdn_pre_bwd1(dqkv, s["xc"], tm=tr)
        dp, dccw = _gdn_pre_bwd2(p, dxc, dp, q["ccw"], nseq, tm=tr)
        dp, dhc, dpw[l], dpwb, dlnw, dlnb, dcb = _conf_bwd1(p, dy, dp, s["hc"], q["lnw"], q["lnb"], q["pw"], q["pwb"], tm=tr)
        dp, dcw = _conf_bwd2(p, dhc, dp, q["cw"], nseq, tm=tr)
        dwin[l] = _matmul(s["h"], dp, "tn", BF16, 1024, 768, tmm, "inproj_bwd_dw")
        dxn, dnw = _inproj_bwd_dx(dp, win[l], xs[l], q["nw"], dxn, tm=tmm)
        halves = lambda a: a.sum(0)[:A_DH] + a.sum(0)[A_DH:]
        smalls[l] = dict(
            norm_w=dnw.sum(0), q_norm_w=halves(dqw), k_norm_w=halves(dkw), sinks=dsk.sum(0)[:A_HEADS], b_conv_b=dcb.sum(0),
            b_ln_w=dlnw.sum(0), b_ln_b=dlnb.sum(0), b_pw_b=dpwb.sum(0), c_a_log=dal.sum(0)[GG:GG + C_HEADS],
            c_dt_bias=ddb.sum(0)[GG:GG + C_HEADS], c_onorm_w=donw.sum(0),
            b_conv_w=dcw.reshape(B_K, SUB, B_W).sum(1), c_conv_w=dccw.reshape(C_K, SUB, 3 * C_W).sum(1))
    grad_x = dxn.reshape(nseq, t, d)

    r_win0, r_win1, r_wout0, r_wout1, r_pw0, r_pw1 = _scatter_blocks([dwin[0], dwin[1], dwout[0], dwout[1], dpw[0], dpw[1]])
    G = {}
    G["w_in"] = jnp.stack([_unpack_cols(_sum8(r_win0)), _unpack_cols(_sum8(r_win1))])
    G["w_out"] = jnp.stack([_sum8(r_wout0), _sum8(r_wout1)])
    G["b_pw_w"] = jnp.stack([_sum8(r_pw0), _sum8(r_pw1)])
    part = _blob([jnp.stack([smalls[l][k] for l in range(DEPTH)]) for k in SMALL])
    (tot,) = _all_gather([part])
    tot = _sum8(tot.reshape(N_DEV, part.shape[0], LANES))
    full_shapes = [(DEPTH,) + smalls[0][k].shape for k in SMALL]
    for k, g in zip(SMALL, _unblob(tot, full_shapes)):
        G[k] = g
    G["b_conv_w"] = lax.dynamic_slice_in_dim(G["b_conv_w"], me * (B_W // N_DEV), B_W // N_DEV, axis=2)
    G["c_conv_w"] = lax.dynamic_slice_in_dim(G["c_conv_w"], me * (3 * C_W // N_DEV), 3 * C_W // N_DEV, axis=2)

    delta, new_m, new_v = {}, {}, {}
    for k in ("w_in", "w_out", "b_pw_w"):
        shp = W[k].shape
        two = lambda a: a.reshape(shp[0] * shp[1], shp[2])
        dl, mo, vo = _adamw(two(W[k]), two(G[k]), two(M[k]), two(V[k]))
        delta[k], new_m[k], new_v[k] = dl.reshape(shp), mo.reshape(shp), vo.reshape(shp)
    dl, mo, vo = _adamw(*[_blob([src[k] for k in SMALL]) for src in (W, G, M, V)])
    shapes = [W[k].shape for k in SMALL]
    for k, a, b, c in zip(SMALL, _unblob(dl, shapes), _unblob(mo, shapes), _unblob(vo, shapes)):
        delta[k], new_m[k], new_v[k] = a, b, c
    return (loss, grad_x, *[G[k] for k in ORDER], *[delta[k] for k in ORDER], *[new_m[k] for k in ORDER],
            *[new_v[k] for k in ORDER])
```

```python
import functools
import math

import jax
import jax.numpy as jnp
from jax import lax
from jax.experimental import pallas as pl
from jax.experimental.pallas import tpu as pltpu

F32 = jnp.float32
BF16 = jnp.bfloat16
HI = lax.Precision.HIGHEST
MESH = pl.DeviceIdType.MESH
S = jax.ShapeDtypeStruct
BS = pl.BlockSpec

N_DEV = 8
DEPTH = 2
D_MODEL = 2048
A_HEADS, A_KV, A_DH, A_W, A_KVW = 12, 4, 64, 768, 256
ROT = 16
THETA = 500000.0
ABLK = 128
B_W, B_K = 512, 31
C_HEADS, C_DH, C_W, C_K, CHUNK = 6, 128, 768, 4, 64
EPS = 1e-6
IN_COLS = 6668
P_Q, P_ZA, P_ZC, P_QKV, P_K, P_V, P_UB, P_ZB, P_BA, P_W = 0, 768, 1536, 2304, 4608, 4864, 5120, 6144, 6656, 6912
Y_A, Y_C, Y_B = 0, 768, 1536
LANES = 128
SUB = 8

ADAM_LR, ADAM_B1, ADAM_B2, ADAM_EPS, ADAM_WD, ADAM_STEP = 0.001, 0.9, 0.999, 1e-08, 0.01, 10


def _cp(*sem, vmem=None):
    kw = {}
    if sem:
        kw["dimension_semantics"] = sem
    if vmem:
        kw["vmem_limit_bytes"] = vmem
    return pltpu.CompilerParams(**kw)


def _pack_cols(w):
    z = jnp.zeros(w.shape[:-1] + (P_W - IN_COLS,), w.dtype)
    return jnp.concatenate([w[..., 0:768], w[..., 1280:2048], w[..., 5900:6668], w[..., 3584:5888],
                            w[..., 768:1024], w[..., 1024:1280], w[..., 2048:3072], w[..., 3072:3584],
                            w[..., 5888:5900], z], axis=-1)


def _unpack_cols(g):
    return jnp.concatenate([g[..., P_Q:P_Q + 768], g[..., P_K:P_K + 256], g[..., P_V:P_V + 256], g[..., P_ZA:P_ZA + 768],
                            g[..., P_UB:P_UB + 1024], g[..., P_ZB:P_ZB + 512], g[..., P_QKV:P_QKV + 2304],
                            g[..., P_BA:P_BA + 12], g[..., P_ZC:P_ZC + 768]], axis=-1)


def _sigmoid(x):
    return 1.0 / (1.0 + jnp.exp(-x))


def _dsilu(x, sg):
    return sg * (1.0 + x * (1.0 - sg))


def _fold8(x):
    r, c = x.shape
    return x.reshape(r // SUB, SUB, c).sum(axis=0)


def _dot(a, b, prec=None):
    return jnp.dot(a, b, preferred_element_type=F32, precision=prec)


def _dot_nt(a, b, prec=None):
    return lax.dot_general(a, b, (((1,), (1,)), ((), ())), preferred_element_type=F32, precision=prec)


def _dot_tn(a, b, prec=None):
    return lax.dot_general(a, b, (((0,), (0,)), ((), ())), preferred_element_type=F32, precision=prec)


def _lane(shape):
    return lax.broadcasted_iota(jnp.int32, shape, 1)


def _subl(shape):
    return lax.broadcasted_iota(jnp.int32, shape, 0)


def _col(x, j):
    return jnp.sum(jnp.where(_lane(x.shape) == j, x, 0.0), axis=-1, keepdims=True)


def _inproj(x, nw, w, tm=512, tn=768):
    n, d = x.shape
    pw = w.shape[1]

    def body(x_ref, nw_ref, w_ref, p_ref, h_ref):
        @pl.when(pl.program_id(1) == 0)
        def _():
            xv = x_ref[...]
            r = lax.rsqrt(jnp.mean(xv * xv, axis=-1, keepdims=True) + EPS)
            h_ref[...] = (xv * r * nw_ref[...]).astype(BF16)

        p_ref[...] = _dot(h_ref[...], w_ref[...])

    return pl.pallas_call(
        body, name="inproj", grid=(n // tm, pw // tn),
        in_specs=[BS((tm, d), lambda i, j: (i, 0)), BS((1, d), lambda i, j: (0, 0)), BS((d, tn), lambda i, j: (0, j))],
        out_specs=[BS((tm, tn), lambda i, j: (i, j)), BS((tm, d), lambda i, j: (i, 0))],
        out_shape=[S((n, pw), F32), S((n, d), BF16)],
        compiler_params=_cp("arbitrary", "arbitrary"),
    )(x, nw, w)


def _outproj(x, y, w, tm=512, tn=1024):
    n, d = x.shape
    k = y.shape[1]

    def body(x_ref, y_ref, w_ref, o_ref):
        o_ref[...] = x_ref[...] + _dot(y_ref[...], w_ref[...])

    return pl.pallas_call(
        body, name="outproj", grid=(n // tm, d // tn),
        in_specs=[BS((tm, tn), lambda i, j: (i, j)), BS((tm, k), lambda i, j: (i, 0)), BS((k, tn), lambda i, j: (0, j))],
        out_specs=BS((tm, tn), lambda i, j: (i, j)),
        out_shape=S((n, d), F32),
        compiler_params=_cp("arbitrary", "arbitrary"),
    )(x, y, w)


def _outproj_loss(x, y, w, tgt, tm=512, tn=1024):
    n, d = x.shape
    k = y.shape[1]

    def body(x_ref, y_ref, w_ref, t_ref, g_ref, l_ref):
        @pl.when((pl.program_id(0) == 0) & (pl.program_id(1) == 0))
        def _():
            l_ref[...] = jnp.zeros_like(l_ref)

        diff = x_ref[...] + _dot(y_ref[...], w_ref[...]) - t_ref[...]
        g_ref[...] = diff * (1.0 / d)
        f = _fold8(diff * diff)
        acc = f[:, 0:LANES]
        for c in range(1, tn // LANES):
            acc = acc + f[:, c * LANES:(c + 1) * LANES]
        l_ref[...] += acc

    return pl.pallas_call(
        body, name="outproj_loss", grid=(n // tm, d // tn),
        in_specs=[BS((tm, tn), lambda i, j: (i, j)), BS((tm, k), lambda i, j: (i, 0)), BS((k, tn), lambda i, j: (0, j)),
                  BS((tm, tn), lambda i, j: (i, j))],
        out_specs=[BS((tm, tn), lambda i, j: (i, j)), BS((SUB, LANES), lambda i, j: (0, 0))],
        out_shape=[S((n, d), F32), S((SUB, LANES), F32)],
        compiler_params=_cp("arbitrary", "arbitrary"),
    )(x, y, w, tgt)


def _matmul(a, b, mode, out_dtype, tm, tn, tk, name):
    if mode == "nn":
        (m, kk), nn = a.shape, b.shape[1]
        a_spec, b_spec = BS((tm, tk), lambda i, j, k: (i, k)), BS((tk, tn), lambda i, j, k: (k, j))
        dot = _dot
    elif mode == "nt":
        (m, kk), nn = a.shape, b.shape[0]
        a_spec, b_spec = BS((tm, tk), lambda i, j, k: (i, k)), BS((tn, tk), lambda i, j, k: (j, k))
        dot = _dot_nt
    else:
        (kk, m), nn = a.shape, b.shape[1]
        a_spec, b_spec = BS((tk, tm), lambda i, j, k: (k, i)), BS((tk, tn), lambda i, j, k: (k, j))
        dot = _dot_tn
    nk = kk // tk

    def body(a_ref, b_ref, o_ref, acc_ref):
        kid = pl.program_id(2)

        @pl.when(kid == 0)
        def _():
            acc_ref[...] = jnp.zeros_like(acc_ref)

        acc_ref[...] += dot(a_ref[...].astype(BF16), b_ref[...].astype(BF16))

        @pl.when(kid == nk - 1)
        def _():
            o_ref[...] = acc_ref[...].astype(out_dtype)

    return pl.pallas_call(
        body, name=name, grid=(m // tm, nn // tn, nk),
        in_specs=[a_spec, b_spec], out_specs=BS((tm, tn), lambda i, j, k: (i, j)),
        out_shape=S((m, nn), out_dtype), scratch_shapes=[pltpu.VMEM((tm, tn), F32)],
        compiler_params=_cp("arbitrary", "arbitrary", "arbitrary"),
    )(a, b)


def _inproj_bwd_dx(dp, w, x, nw, dres, tm=512, tk=768):
    n, d = x.shape
    nk = dp.shape[1] // tk

    def body(dp_ref, w_ref, x_ref, nw_ref, dr_ref, dx_ref, dnw_ref, acc_ref):
        kid = pl.program_id(1)

        @pl.when((pl.program_id(0) == 0) & (kid == 0))
        def _():
            dnw_ref[...] = jnp.zeros_like(dnw_ref)

        @pl.when(kid == 0)
        def _():
            acc_ref[...] = jnp.zeros_like(acc_ref)

        acc_ref[...] += _dot_nt(dp_ref[...], w_ref[...])

        @pl.when(kid == nk - 1)
        def _():
            dh = acc_ref[...]
            xv = x_ref[...]
            r = lax.rsqrt(jnp.mean(xv * xv, axis=-1, keepdims=True) + EPS)
            dnw_ref[...] += _fold8(dh * xv * r)
            g = dh * nw_ref[...]
            mm = jnp.mean(g * xv, axis=-1, keepdims=True)
            dx_ref[...] = dr_ref[...] + r * g - xv * (r * r * r * mm)

    return pl.pallas_call(
        body, name="inproj_bwd_dx", grid=(n // tm, nk),
        in_specs=[BS((tm, tk), lambda i, k: (i, k)), BS((d, tk), lambda i, k: (0, k)), BS((tm, d), lambda i, k: (i, 0)),
                  BS((1, d), lambda i, k: (0, 0)), BS((tm, d), lambda i, k: (i, 0))],
        out_specs=[BS((tm, d), lambda i, k: (i, 0)), BS((SUB, d), lambda i, k: (0, 0))],
        out_shape=[S((n, d), F32), S((SUB, d), F32)],
        scratch_shapes=[pltpu.VMEM((tm, d), F32)],
        compiler_params=_cp("arbitrary", "arbitrary"),
    )(dp, w, x, nw, dres)


def _rope_tables(pos):
    half = ROT // 2
    inv = THETA ** (-jnp.arange(0, ROT, 2, dtype=F32) / ROT)
    ang = pos.astype(F32)[:, None] * inv
    cos, sin = jnp.cos(ang), jnp.sin(ang)
    n = pos.shape[0]
    one = jnp.ones((n, A_DH - ROT), F32)
    zero = jnp.zeros((n, A_DH - ROT), F32)
    zh = jnp.zeros((n, half), F32)
    c = jnp.concatenate([cos, cos, one], axis=1)
    s1 = jnp.concatenate([-sin, zh, zero], axis=1)
    s2 = jnp.concatenate([zh, sin, zero], axis=1)
    return tuple(jnp.concatenate([t, t], axis=1) for t in (c, s1, s2))


def _half_stat(t):
    lo = _lane(t.shape) < A_DH
    s_lo = jnp.sum(jnp.where(lo, t, 0.0), axis=-1, keepdims=True)
    s_hi = jnp.sum(jnp.where(lo, 0.0, t), axis=-1, keepdims=True)
    return jnp.where(lo, s_lo, s_hi)


def _normrope(x, w, c, s1, s2):
    r = lax.rsqrt(_half_stat(x * x) * (1.0 / A_DH) + EPS)
    xn = x * r * w
    return xn * c + pltpu.roll(xn, LANES - ROT // 2, 1) * s1 + pltpu.roll(xn, ROT // 2, 1) * s2, r


def _normrope_bwd(dy, x, r, w, c, s1, s2):
    dxn = dy * c + pltpu.roll(dy * s1, ROT // 2, 1) + pltpu.roll(dy * s2, LANES - ROT // 2, 1)
    g = dxn * w
    mm = _half_stat(g * x) * (1.0 / A_DH)
    return r * g - x * (r * r * r * mm), dxn * x * r


def _attn_mask(first):
    qi = _subl((ABLK, 2 * ABLK))
    kj = _lane((ABLK, 2 * ABLK))
    dist = qi + ABLK - kj
    return (dist >= 0) & (dist < ABLK) & (jnp.logical_not(first) | (kj >= ABLK))


def _attn_fwd(p, tabs, qw, kw, sinks, nseq):
    n = p.shape[0]
    nb = n // nseq // ABLK
    cur = lambda b, i: (b * nb + i, 0)
    prv = lambda b, i: (b * nb + jnp.maximum(i - 1, 0), 0)
    colblk = lambda f, w, off: (lambda b, i: (f(b, i)[0], off // w))

    def body(q_ref, za_ref, kc_ref, vc_ref, kp_ref, vp_ref, c_ref, s1_ref, s2_ref, cp_ref, s1p_ref, s2p_ref,
             qw_ref, kw_ref, sink_ref, y_ref, o_ref, lse_ref):
        first = pl.program_id(1) == 0
        tc = (c_ref[...], s1_ref[...], s2_ref[...])
        tp = (cp_ref[...], s1p_ref[...], s2p_ref[...])
        q, kc, kp = q_ref[...], kc_ref[...], kp_ref[...]
        qn = [_normrope(q[:, LANES * b:LANES * (b + 1)], qw_ref[...], *tc)[0].astype(BF16) for b in range(A_W // LANES)]
        k2, v2 = [], []
        for b in range(A_KVW // LANES):
            sl = slice(LANES * b, LANES * (b + 1))
            k2.append(jnp.concatenate([_normrope(kp[:, sl], kw_ref[...], *tp)[0],
                                       _normrope(kc[:, sl], kw_ref[...], *tc)[0]], axis=0).astype(BF16))
            v2.append(jnp.concatenate([vp_ref[:, sl], vc_ref[:, sl]], axis=0).astype(BF16))
        valid = _attn_mask(first)
        outs = []
        lse = jnp.zeros((ABLK, LANES), F32)
        for g in range(A_KV):
            hs = slice(A_DH * (g % 2), A_DH * (g % 2 + 1))
            kh, vh = k2[g // 2][:, hs], v2[g // 2][:, hs]
            for j in range(3 * g, 3 * g + 3):
                qh = qn[j // 2][:, A_DH * (j % 2):A_DH * (j % 2 + 1)]
                s = jnp.where(valid, _dot_nt(qh, kh) * (A_DH ** -0.5), -jnp.inf)
                sk = sink_ref[j]
                m = jnp.maximum(jnp.max(s, axis=-1, keepdims=True), sk)
                e = jnp.exp(s - m)
                den = jnp.sum(e, axis=-1, keepdims=True) + jnp.exp(sk - m)
                outs.append(_dot((e / den).astype(BF16), vh))
                lse = jnp.where(_lane(lse.shape) == j, m + jnp.log(den), lse)
        o = jnp.concatenate(outs, axis=1)
        za = za_ref[...]
        o_ref[...] = o
        lse_ref[...] = lse
        y_ref[...] = (o * (za * _sigmoid(za))).astype(BF16)

    tab_specs = [BS((ABLK, LANES), cur)] * 3 + [BS((ABLK, LANES), prv)] * 3
    return pl.pallas_call(
        body, name="attn_fwd", grid=(nseq, nb),
        in_specs=[BS((ABLK, A_W), colblk(cur, A_W, P_Q)), BS((ABLK, A_W), colblk(cur, A_W, P_ZA)),
                  BS((ABLK, A_KVW), colblk(cur, A_KVW, P_K)), BS((ABLK, A_KVW), colblk(cur, A_KVW, P_V)),
                  BS((ABLK, A_KVW), colblk(prv, A_KVW, P_K)), BS((ABLK, A_KVW), colblk(prv, A_KVW, P_V))]
        + tab_specs + [BS((1, LANES), lambda b, i: (0, 0))] * 2 + [BS(memory_space=pltpu.SMEM)],
        out_specs=[BS((ABLK, A_W), colblk(cur, A_W, Y_A)), BS((ABLK, A_W), cur), BS((ABLK, LANES), cur)],
        out_shape=[S((n, D_MODEL), BF16), S((n, A_W), F32), S((n, LANES), F32)],
        compiler_params=_cp("arbitrary", "arbitrary"),
    )(p, p, p, p, p, p, *tabs, *tabs, qw, kw, sinks)


def _attn_bwd(p, dy, o, lse, tabs, qw, kw, sinks, nseq):
    n = p.shape[0]
    nb = n // nseq // ABLK
    cur = lambda b, i: (b * nb + jnp.minimum(i, nb - 1), 0)
    prv = lambda b, i: (b * nb + jnp.maximum(i - 1, 0), 0)
    colblk = lambda f, w, off: (lambda b, i: (f(b, i)[0], off // w))

    def body(q_ref, za_ref, kc_ref, vc_ref, kp_ref, vp_ref, dy_ref, o_ref, lse_ref,
             c_ref, s1_ref, s2_ref, cp_ref, s1p_ref, s2p_ref, qw_ref, kw_ref, sink_ref,
             dqza_ref, dkv_ref, dqw_ref, dkw_ref, dsk_ref, tk_ref, tv_ref, ck_ref, cv_ref):
        i = pl.program_id(1)
        first = i == 0
        tc = (c_ref[...], s1_ref[...], s2_ref[...])
        tp = (cp_ref[...], s1p_ref[...], s2p_ref[...])
        nkb = A_KVW // LANES

        @pl.when((pl.program_id(0) == 0) & first)
        def _():
            dqw_ref[...] = jnp.zeros_like(dqw_ref)
            dkw_ref[...] = jnp.zeros_like(dkw_ref)
            dsk_ref[...] = jnp.zeros_like(dsk_ref)

        @pl.when(i < nb)
        def _():
            q, kc, kp = q_ref[...], kc_ref[...], kp_ref[...]
            qn, rq = [], []
            for b in range(A_W // LANES):
                a, r = _normrope(q[:, LANES * b:LANES * (b + 1)], qw_ref[...], *tc)
                qn.append(a.astype(BF16))
                rq.append(r)
            k2, v2 = [], []
            for b in range(nkb):
                sl = slice(LANES * b, LANES * (b + 1))
                k2.append(jnp.concatenate([_normrope(kp[:, sl], kw_ref[...], *tp)[0],
                                           _normrope(kc[:, sl], kw_ref[...], *tc)[0]], axis=0).astype(BF16))
                v2.append(jnp.concatenate([vp_ref[:, sl], vc_ref[:, sl]], axis=0).astype(BF16))
            valid = _attn_mask(first)
            za, dy, o, lse = za_ref[...], dy_ref[...], o_ref[...], lse_ref[...]
            sg = _sigmoid(za)
            do = dy * za * sg
            dqza_ref[:, A_W:2 * A_W] = (dy * o * _dsilu(za, sg)).astype(BF16)
            dqs, dks, dvs = [], [], []
            dsk = jnp.zeros((ABLK, LANES), F32)
            for g in range(A_KV):
                hs = slice(A_DH * (g % 2), A_DH * (g % 2 + 1))
                kh, vh = k2[g // 2][:, hs], v2[g // 2][:, hs]
                dkg = jnp.zeros((2 * ABLK, A_DH), F32)
                dvg = jnp.zeros((2 * ABLK, A_DH), F32)
                for j in range(3 * g, 3 * g + 3):
                    js = slice(A_DH * j, A_DH * (j + 1))
                    qh = qn[j // 2][:, A_DH * (j % 2):A_DH * (j % 2 + 1)]
                    lj = _col(lse, j)
                    s = jnp.where(valid, _dot_nt(qh, kh) * (A_DH ** -0.5), -jnp.inf)
                    pr = jnp.exp(s - lj)
                    doh = do[:, js]
                    delta = jnp.sum(doh * o[:, js], axis=-1, keepdims=True)
                    dsk = dsk + jnp.where(_lane(dsk.shape) == j, -jnp.exp(sink_ref[j] - lj) * delta, 0.0)
                    dohb = doh.astype(BF16)
                    ds = (pr * (_dot_nt(dohb, vh) - delta) * (A_DH ** -0.5)).astype(BF16)
                    dqs.append(_dot(ds, kh))
                    dkg = dkg + _dot_tn(ds, qh)
                    dvg = dvg + _dot_tn(pr.astype(BF16), dohb)
                dks.append(dkg)
                dvs.append(dvg)
            dsk_ref[...] += _fold8(dsk)
            dqn = jnp.concatenate(dqs, axis=1)
            dqw = jnp.zeros((SUB, LANES), F32)
            dqo = []
            for b in range(A_W // LANES):
                sl = slice(LANES * b, LANES * (b + 1))
                dx, wt = _normrope_bwd(dqn[:, sl], q[:, sl], rq[b], qw_ref[...], *tc)
                dqo.append(dx)
                dqw = dqw + _fold8(wt)
            dqw_ref[...] += dqw
            dqza_ref[:, 0:A_W] = jnp.concatenate(dqo, axis=1).astype(BF16)
            tk_ref[...] = jnp.concatenate(dks, axis=1)
            tv_ref[...] = jnp.concatenate(dvs, axis=1)

        @pl.when(i == nb)
        def _():
            tk_ref[...] = jnp.zeros_like(tk_ref)
            tv_ref[...] = jnp.zeros_like(tv_ref)

        @pl.when(i > 0)
        def _():
            kp = kp_ref[...]
            dkn = ck_ref[...] + tk_ref[0:ABLK, :]
            dkw = jnp.zeros((SUB, LANES), F32)
            dko = []
            for b in range(nkb):
                sl = slice(LANES * b, LANES * (b + 1))
                r = _normrope(kp[:, sl], kw_ref[...], *tp)[1]
                dx, wt = _normrope_bwd(dkn[:, sl], kp[:, sl], r, kw_ref[...], *tp)
                dko.append(dx)
                dkw = dkw + _fold8(wt)
            dkw_ref[...] += dkw
            dkv_ref[:, 0:A_KVW] = jnp.concatenate(dko, axis=1).astype(BF16)
            dkv_ref[:, A_KVW:2 * A_KVW] = (cv_ref[...] + tv_ref[0:ABLK, :]).astype(BF16)

        ck_ref[...] = tk_ref[ABLK:2 * ABLK, :]
        cv_ref[...] = tv_ref[ABLK:2 * ABLK, :]

    tab_specs = [BS((ABLK, LANES), cur)] * 3 + [BS((ABLK, LANES), prv)] * 3
    acc = BS((SUB, LANES), lambda b, i: (0, 0))
    return pl.pallas_call(
        body, name="attn_bwd", grid=(nseq, nb + 1),
        in_specs=[BS((ABLK, A_W), colblk(cur, A_W, P_Q)), BS((ABLK, A_W), colblk(cur, A_W, P_ZA)),
                  BS((ABLK, A_KVW), colblk(cur, A_KVW, P_K)), BS((ABLK, A_KVW), colblk(cur, A_KVW, P_V)),
                  BS((ABLK, A_KVW), colblk(prv, A_KVW, P_K)), BS((ABLK, A_KVW), colblk(prv, A_KVW, P_V)),
                  BS((ABLK, A_W), colblk(cur, A_W, Y_A)), BS((ABLK, A_W), cur), BS((ABLK, LANES), cur)]
        + tab_specs + [BS((1, LANES), lambda b, i: (0, 0))] * 2 + [BS(memory_space=pltpu.SMEM)],
        out_specs=[BS((ABLK, 2 * A_W), cur), BS((ABLK, 2 * A_KVW), prv), acc, acc, acc],
        out_shape=[S((n, P_W), BF16), S((n, 2 * A_KVW), BF16)] + [S((SUB, LANES), F32)] * 3,
        scratch_shapes=[pltpu.VMEM((2 * ABLK, A_KVW), F32)] * 2 + [pltpu.VMEM((ABLK, A_KVW), F32)] * 2,
        compiler_params=_cp("arbitrary", "arbitrary"),
    )(p, p, p, p, p, p, dy, o, lse, *tabs, *tabs, qw, kw, sinks)


def _put_cols(dst, src, col_off, tm=512):
    n, w = src.shape

    def body(s_ref, d_in_ref, d_ref):
        d_ref[...] = s_ref[...]

    return pl.pallas_call(
        body, name="put_cols", grid=(n // tm,),
        in_specs=[BS((tm, w), lambda i: (i, 0)), BS(memory_space=pl.ANY)],
        out_specs=BS((tm, w), lambda i: (i, col_off // w)),
        out_shape=S(dst.shape, dst.dtype), input_output_aliases={1: 0},
        compiler_params=_cp("arbitrary"),
    )(src, dst)


HALO_B = 32


def _layernorm(hc, lnw, lnb):
    mu = jnp.mean(hc, axis=-1, keepdims=True)
    xc = hc - mu
    rstd = lax.rsqrt(jnp.mean(xc * xc, axis=-1, keepdims=True) + EPS)
    xhat = xc * rstd
    return xhat, rstd, xhat * lnw + lnb


def _conf_fwd(p, y, cw, cb, lnw, lnb, pw, pwb, nseq, tm=256):
    n = p.shape[0]
    t = n // nseq
    nt = t // tm
    row = lambda b, i: b * nt + i
    halo = lambda b, i: jnp.maximum((b * t + i * tm) // HALO_B - 1, 0)
    vec = BS((1, B_W), lambda b, i: (0, 0))

    def body(ub_ref, uh_ref, zb_ref, cw_ref, cb_ref, lnw_ref, lnb_ref, pw_ref, pwb_ref, y_in_ref, y_ref, hc_ref, buf_ref):
        ub, uh = ub_ref[...], uh_ref[...]
        hh = uh[:, :B_W] * _sigmoid(uh[:, B_W:])
        buf_ref[0:HALO_B, :] = jnp.where(pl.program_id(1) > 0, hh, 0.0)
        buf_ref[HALO_B:, :] = ub[:, :B_W] * _sigmoid(ub[:, B_W:])
        hc = jnp.zeros((tm, B_W), F32) + cb_ref[...]
        for k in range(B_K):
            hc = hc + cw_ref[k:k + 1, :] * buf_ref[pl.ds(HALO_B - B_K + 1 + k, tm), :]
        hc_ref[...] = hc
        ln = _layernorm(hc, lnw_ref[...], lnb_ref[...])[2]
        sw = ln * _sigmoid(ln)
        ob = _dot(sw.astype(BF16), pw_ref[...]) + pwb_ref[...]
        zb = zb_ref[...]
        y_ref[...] = (ob * (zb * _sigmoid(zb))).astype(BF16)

    return pl.pallas_call(
        body, name="conf_fwd", grid=(nseq, nt),
        in_specs=[BS((tm, 2 * B_W), lambda b, i: (row(b, i), P_UB // (2 * B_W))),
                  BS((HALO_B, 2 * B_W), lambda b, i: (halo(b, i), P_UB // (2 * B_W))),
                  BS((tm, B_W), lambda b, i: (row(b, i), P_ZB // B_W)),
                  BS((HALO_B, B_W), lambda b, i: (0, 0)), vec, vec, vec, BS((B_W, B_W), lambda b, i: (0, 0)), vec,
                  BS(memory_space=pl.ANY)],
        out_specs=[BS((tm, B_W), lambda b, i: (row(b, i), Y_B // B_W)), BS((tm, B_W), lambda b, i: (row(b, i), 0))],
        out_shape=[S(y.shape, y.dtype), S((n, B_W), F32)], input_output_aliases={9: 0},
        scratch_shapes=[pltpu.VMEM((HALO_B + tm, B_W), F32)],
        compiler_params=_cp("arbitrary", "arbitrary"),
    )(p, p, p, cw, cb, lnw, lnb, pw, pwb, y)


def _conf_bwd1(p, dy, dp, hc, lnw, lnb, pw, pwb, tm=256):
    n = p.shape[0]
    vec = BS((1, B_W), lambda i: (0, 0))
    acc = BS((SUB, B_W), lambda i: (0, 0))

    def body(dy_ref, zb_ref, hc_ref, lnw_ref, lnb_ref, pw_ref, pwb_ref, dp_in_ref,
             dzb_ref, dhc_ref, dpw_ref, dpwb_ref, dlnw_ref, dlnb_ref, dcb_ref):
        @pl.when(pl.program_id(0) == 0)
        def _():
            for r in (dpw_ref, dpwb_ref, dlnw_ref, dlnb_ref, dcb_ref):
                r[...] = jnp.zeros_like(r)

        xhat, rstd, ln = _layernorm(hc_ref[...], lnw_ref[...], lnb_ref[...])
        sgl = _sigmoid(ln)
        sw = (ln * sgl).astype(BF16)
        ob = _dot(sw, pw_ref[...]) + pwb_ref[...]
        dy, zb = dy_ref[...], zb_ref[...]
        sgz = _sigmoid(zb)
        dzb_ref[...] = (dy * ob * _dsilu(zb, sgz)).astype(BF16)
        dob = dy * zb * sgz
        dobb = dob.astype(BF16)
        dpwb_ref[...] += _fold8(dob)
        dpw_ref[...] += _dot_tn(sw, dobb)
        dln = _dot_nt(dobb, pw_ref[...]) * _dsilu(ln, sgl)
        dlnw_ref[...] += _fold8(dln * xhat)
        dlnb_ref[...] += _fold8(dln)
        dxh = dln * lnw_ref[...]
        dhc = rstd * (dxh - jnp.mean(dxh, axis=-1, keepdims=True) - xhat * jnp.mean(dxh * xhat, axis=-1, keepdims=True))
        dcb_ref[...] += _fold8(dhc)
        dhc_ref[...] = dhc

    return pl.pallas_call(
        body, name="conf_bwd1", grid=(n // tm,),
        in_specs=[BS((tm, B_W), lambda i: (i, Y_B // B_W)), BS((tm, B_W), lambda i: (i, P_ZB // B_W)),
                  BS((tm, B_W), lambda i: (i, 0)), vec, vec, BS((B_W, B_W), lambda i: (0, 0)), vec,
                  BS(memory_space=pl.ANY)],
        out_specs=[BS((tm, B_W), lambda i: (i, P_ZB // B_W)), BS((tm, B_W), lambda i: (i, 0)),
                   BS((B_W, B_W), lambda i: (0, 0)), acc, acc, acc, acc],
        out_shape=[S(dp.shape, dp.dtype), S((n, B_W), F32), S((B_W, B_W), F32)] + [S((SUB, B_W), F32)] * 4,
        input_output_aliases={7: 0},
        compiler_params=_cp("arbitrary"),
    )(dy, p, hc, lnw, lnb, pw, pwb, dp)


def _conf_bwd2(p, dhc, dp, cw, nseq, tm=256):
    n = p.shape[0]
    t = n // nseq
    nt = t // tm
    row = lambda b, i: b * nt + i
    prev = lambda b, i: jnp.maximum((b * t + i * tm) // HALO_B - 1, 0)
    nxt = lambda b, i: jnp.minimum((b * t + (i + 1) * tm) // HALO_B, n // HALO_B - 1)

    def body(ub_ref, uh_ref, dh_ref, dn_ref, cw_ref, dp_in_ref, dub_ref, dcw_ref, buf_ref, dbuf_ref):
        i = pl.program_id(1)

        @pl.when((pl.program_id(0) == 0) & (i == 0))
        def _():
            dcw_ref[...] = jnp.zeros_like(dcw_ref)

        ub, uh = ub_ref[...], uh_ref[...]
        a, sg = ub[:, :B_W], _sigmoid(ub[:, B_W:])
        buf_ref[0:HALO_B, :] = jnp.where(i > 0, uh[:, :B_W] * _sigmoid(uh[:, B_W:]), 0.0)
        buf_ref[HALO_B:, :] = a * sg
        dhc = dh_ref[...]
        dbuf_ref[0:tm, :] = dhc
        dbuf_ref[tm:, :] = jnp.where(i < nt - 1, dn_ref[...], 0.0)
        dhg = jnp.zeros((tm, B_W), F32)
        for k in range(B_K):
            dhg = dhg + cw_ref[k:k + 1, :] * dbuf_ref[pl.ds(B_K - 1 - k, tm), :]
            dcw_ref[SUB * k:SUB * (k + 1), :] += _fold8(dhc * buf_ref[pl.ds(HALO_B - B_K + 1 + k, tm), :])
        dub_ref[...] = jnp.concatenate([dhg * sg, dhg * a * sg * (1.0 - sg)], axis=1).astype(BF16)

    return pl.pallas_call(
        body, name="conf_bwd2", grid=(nseq, nt),
        in_specs=[BS((tm, 2 * B_W), lambda b, i: (row(b, i), P_UB // (2 * B_W))),
                  BS((HALO_B, 2 * B_W), lambda b, i: (prev(b, i), P_UB // (2 * B_W))),
                  BS((tm, B_W), lambda b, i: (row(b, i), 0)), BS((HALO_B, B_W), lambda b, i: (nxt(b, i), 0)),
                  BS((HALO_B, B_W), lambda b, i: (0, 0)), BS(memory_space=pl.ANY)],
        out_specs=[BS((tm, 2 * B_W), lambda b, i: (row(b, i), P_UB // (2 * B_W))),
                   BS((SUB * B_K, B_W), lambda b, i: (0, 0))],
        out_shape=[S(dp.shape, dp.dtype), S((SUB * B_K, B_W), F32)], input_output_aliases={5: 0},
        scratch_shapes=[pltpu.VMEM((HALO_B + tm, B_W), F32)] * 2,
        compiler_params=_cp("arbitrary", "arbitrary"),
    )(p, p, dhc, dhc, cw, dp)


HALO_C = 8
QS = C_DH ** -0.5
NCB = 3 * C_HEADS
CB0 = P_QKV // LANES
ZC0 = P_ZC // LANES
GB, GG = 0, C_HEADS


def _softplus(z):
    return jnp.maximum(z, 0.0) + jnp.log(1.0 + jnp.exp(-jnp.abs(z)))


def _gdn_gates_fwd(p, alog_l, dtb_l, tm=256):
    n = p.shape[0]

    def body(ba_ref, al_ref, db_ref, o_ref):
        blk = ba_ref[...]
        lane = _lane(blk.shape)
        g = jnp.where((lane >= GG) & (lane < GG + C_HEADS), -jnp.exp(al_ref[...]) * _softplus(blk + db_ref[...]), 0.0)
        tri = (_subl((CHUNK, CHUNK)) >= _lane((CHUNK, CHUNK))).astype(F32)
        gc = jnp.concatenate([_dot(tri, g[CHUNK * c:CHUNK * (c + 1)], HI) for c in range(tm // CHUNK)], axis=0)
        o_ref[...] = jnp.where(lane < GG, _sigmoid(blk), gc)

    return pl.pallas_call(
        body, name="gdn_gates_fwd", grid=(n // tm,),
        in_specs=[BS((tm, LANES), lambda i: (i, P_BA // LANES)), BS((1, LANES), lambda i: (0, 0)), BS((1, LANES), lambda i: (0, 0))],
        out_specs=BS((tm, LANES), lambda i: (i, 0)), out_shape=S((n, LANES), F32),
        compiler_params=_cp("arbitrary"),
    )(p, alog_l, dtb_l)


def _gdn_pre_fwd(p, ccw, nseq, tm=256):
    n = p.shape[0]
    t = n // nseq
    nt = t // tm
    row = lambda b, i: b * nt + i
    halo = lambda b, i: jnp.maximum((b * t + i * tm) // HALO_C - 1, 0)

    def body(x_ref, xh_ref, w_ref, xc_ref, o_ref, buf_ref):
        buf_ref[0:HALO_C, :] = jnp.where(pl.program_id(1) > 0, xh_ref[...], 0.0)
        buf_ref[HALO_C:, :] = x_ref[...]
        for c in range(NCB):
            cs = slice(LANES * c, LANES * (c + 1))
            xc = jnp.zeros((tm, LANES), F32)
            for k in range(C_K):
                xc = xc + w_ref[k:k + 1, cs] * buf_ref[pl.ds(HALO_C - C_K + 1 + k, tm), cs]
            xc_ref[:, cs] = xc
            act = xc * _sigmoid(xc)
            if c < 2 * C_HEADS:
                act = act * (lax.rsqrt(jnp.sum(act * act, axis=-1, keepdims=True) + EPS) * (QS if c < C_HEADS else 1.0))
            o_ref[:, cs] = act

    wide = 3 * C_W
    return pl.pallas_call(
        body, name="gdn_pre_fwd", grid=(nseq, nt),
        in_specs=[BS((tm, wide), lambda b, i: (row(b, i), P_QKV // wide)), BS((HALO_C, wide), lambda b, i: (halo(b, i), P_QKV // wide)),
                  BS((SUB, wide), lambda b, i: (0, 0))],
        out_specs=[BS((tm, wide), lambda b, i: (row(b, i), 0))] * 2,
        out_shape=[S((n, wide), F32)] * 2,
        scratch_shapes=[pltpu.VMEM((HALO_C + tm, wide), F32)],
        compiler_params=_cp("arbitrary", "arbitrary"),
    )(p, p, ccw)


def _chunk_common(q, k, gt, gtt, h):
    beta = _col(gt, GB + h)
    gc = _col(gt, GG + h)
    gcr = gtt[GG + h:GG + h + 1, :]
    ii, jj = _subl((CHUNK, CHUNK)), _lane((CHUNK, CHUNK))
    incl, strict = ii >= jj, ii > jj
    dec = jnp.exp(jnp.where(incl, gc - gcr, -jnp.inf))
    kb = k * beta
    kbf = k.astype(BF16)
    a = jnp.where(strict, _dot_nt(kb.astype(BF16), kbf) * dec, 0.0)
    mq = jnp.where(incl, _dot_nt(q.astype(BF16), kbf) * dec, 0.0)
    glast = jnp.sum(jnp.where(_subl(gc.shape) == CHUNK - 1, gc, 0.0), axis=0, keepdims=True)
    return beta, gc, incl, strict, dec, kb, a, mq, glast


def _unit_lower_inverses(mats):
    eye = (_subl(mats[0].shape) == _lane(mats[0].shape)).astype(F32)
    ms = [-a for a in mats]
    invs = [eye + m for m in ms]
    for _ in range(5):
        ms = [_dot(m, m, HI) for m in ms]
        invs = [inv + _dot(inv, m, HI) for inv, m in zip(invs, ms)]
    return invs


def _gdn_chunk_fwd(qkv, gates, p, y, onw, nseq, tt=512):
    n = qkv.shape[0]
    t = n // nseq
    tt = min(tt, t)
    nt = t // tt
    nch = tt // CHUNK

    def body(q_ref, k_ref, v_ref, g_ref, zc_ref, onw_ref, y_in_ref, y_ref, o_ref, u_ref, w_ref, t_ref, ss_ref, s_scr):
        @pl.when(pl.program_id(1) == 0)
        def _():
            s_scr[...] = jnp.zeros_like(s_scr)

        def step(c, carry):
            rows = pl.ds(pl.multiple_of(c * CHUNK, CHUNK), CHUNK)
            gt = g_ref[rows, :]
            gtt = gt.T
            heads = range(C_HEADS)
            hs = [slice(C_DH * h, C_DH * (h + 1)) for h in heads]
            q, k, v = ([r[rows, hs[h]] for h in heads] for r in (q_ref, k_ref, v_ref))
            cm = [_chunk_common(q[h], k[h], gt, gtt, h) for h in heads]
            beta, gc, kb, mq, glast = ([m[i] for m in cm] for i in (0, 1, 5, 7, 8))
            tinv = _unit_lower_inverses([m[6] for m in cm])
            egc = [jnp.exp(g) for g in gc]
            sol = [_dot(tinv[h], jnp.concatenate([v[h] * beta[h], kb[h] * egc[h]], axis=1), HI) for h in heads]
            sv = [s_scr[h] for h in heads]
            sb = [s.astype(BF16) for s in sv]
            vnb = [(sol[h][:, :C_DH] - _dot(sol[h][:, C_DH:].astype(BF16), sb[h])).astype(BF16) for h in heads]
            o = [_dot((q[h] * egc[h]).astype(BF16), sb[h]) + _dot(mq[h].astype(BF16), vnb[h]) for h in heads]
            for h in heads:
                ss_ref[h, c] = sv[h]
                s_scr[h] = sv[h] * jnp.exp(glast[h]) + _dot_tn((k[h] * jnp.exp(glast[h] - gc[h])).astype(BF16), vnb[h])
            for h in heads:
                o_ref[rows, hs[h]] = o[h]
                u_ref[rows, hs[h]] = sol[h][:, :C_DH]
                w_ref[rows, hs[h]] = sol[h][:, C_DH:]
                t_ref[rows, hs[h]] = jnp.concatenate([tinv[h], jnp.zeros_like(tinv[h])], axis=1)
                zc = zc_ref[rows, hs[h]]
                r = lax.rsqrt(jnp.mean(o[h] * o[h], axis=-1, keepdims=True) + EPS)
                y_ref[rows, hs[h]] = (o[h] * r * onw_ref[...] * (zc * _sigmoid(zc))).astype(BF16)
            return carry

        lax.fori_loop(0, nch, step, 0)

    row = lambda b, i: b * nt + i
    wb = lambda col: BS((tt, C_W), lambda b, i: (row(b, i), col))
    return pl.pallas_call(
        body, name="gdn_chunk_fwd", grid=(nseq, nt),
        in_specs=[wb(0), wb(1), wb(2), BS((tt, LANES), lambda b, i: (row(b, i), 0)), wb(P_ZC // C_W),
                  BS((1, LANES), lambda b, i: (0, 0)), BS(memory_space=pl.ANY)],
        out_specs=[wb(Y_C // C_W), wb(0), wb(0), wb(0), wb(0),
                   BS((None, C_HEADS, nch, C_DH, C_DH), lambda b, i: (b, 0, i, 0, 0))],
        out_shape=[S(y.shape, y.dtype)] + [S((n, C_W), F32)] * 4 + [S((nseq, C_HEADS, t // CHUNK, C_DH, C_DH), F32)],
        input_output_aliases={6: 0},
        scratch_shapes=[pltpu.VMEM((C_HEADS, C_DH, C_DH), F32)],
        compiler_params=_cp("arbitrary", "arbitrary"),
    )(qkv, qkv, qkv, gates, p, onw, y)


def _gdn_chunk_bwd(qkv, gates, p, dy, dp, onw, o, u, w, tinv, ss, nseq, tt=256):
    n = qkv.shape[0]
    t = n // nseq
    tt = min(tt, t)
    nt = t // tt
    nch = tt // CHUNK

    def body(q_ref, k_ref, v_ref, g_ref, zc_ref, onw_ref, o_ref, dy_ref, u_ref, w_ref, t_ref, ss_ref, dp_in_ref,
             dzc_ref, dqkv_ref, dg_ref, donw_ref, ds_scr):
        @pl.when(pl.program_id(1) == 0)
        def _():
            ds_scr[...] = jnp.zeros_like(ds_scr)

        @pl.when((pl.program_id(0) == 0) & (pl.program_id(1) == 0))
        def _():
            donw_ref[...] = jnp.zeros_like(donw_ref)

        def rsum(x):
            return jnp.sum(x, axis=-1, keepdims=True)

        def step(ci, carry):
            c = nch - 1 - ci
            rows = pl.ds(pl.multiple_of(c * CHUNK, CHUNK), CHUNK)
            gt = g_ref[rows, :]
            gtt = gt.T
            live = [head(c, rows, gt, gtt, h) for h in range(C_HEADS)]
            while live:
                live = [g for g in live if next(g, False)]
            return carry

        def head(c, rows, gt, gtt, h):
            hs = slice(C_DH * h, C_DH * (h + 1))
            q, k, v = q_ref[rows, hs], k_ref[rows, hs], v_ref[rows, hs]
            zc, o, dy, u, w = zc_ref[rows, hs], o_ref[rows, hs], dy_ref[rows, hs], u_ref[rows, hs], w_ref[rows, hs]
            tm_ = t_ref[rows, hs][:, 0:CHUNK]
            sv, dsv = ss_ref[h, c], ds_scr[h]
            sb, dsb = sv.astype(BF16), dsv.astype(BF16)
            sg = _sigmoid(zc)
            r = lax.rsqrt(jnp.mean(o * o, axis=-1, keepdims=True) + EPS)
            on = o * r
            ow = onw_ref[...]
            dzc_ref[rows, hs] = (dy * on * ow * _dsilu(zc, sg)).astype(BF16)
            t1 = dy * zc * sg
            donw_ref[...] += _fold8(t1 * on)
            don = t1 * ow
            do = r * (don - on * jnp.mean(don * on, axis=-1, keepdims=True))
            dob = do.astype(BF16)
            yield True
            beta, gc, incl, strict, dec, kb, a, mq, glast = _chunk_common(q, k, gt, gtt, h)
            egc = jnp.exp(gc)
            gl = jnp.exp(glast)
            ekd = jnp.exp(glast - gc)
            wb = w.astype(BF16)
            vnb = (u - _dot(wb, sb)).astype(BF16)
            qg = q * egc
            yield True
            dvn = _dot_tn(mq.astype(BF16), dob) + _dot((k * ekd).astype(BF16), dsb)
            dvnb = dvn.astype(BF16)
            dqg = _dot_nt(dob, sb)
            yield True
            dmq = jnp.where(incl, _dot_nt(dob, vnb), 0.0)
            dkd = _dot_nt(vnb, dsb)
            dgl = jnp.sum(rsum(dsv * sv), axis=0, keepdims=True)
            dw = -_dot_nt(dvnb, sb)
            yield True
            ds_scr[h] = gl * dsv + _dot_tn(qg.astype(BF16), dob) - _dot_tn(wb, dvnb)
            db = _dot_tn(tm_, jnp.concatenate([dvn, dw], axis=1), HI)
            dbv, dbk = db[:, :C_DH], db[:, C_DH:]
            yield True
            da = -jnp.where(strict, _dot_nt(dbv, u, HI) + _dot_nt(dbk, w, HI), 0.0)
            yield True
            e = da * a + dmq * mq
            dgc = rsum(e) - rsum(e.T)
            dgb, dhb, kbf = (da * dec).astype(BF16), (dmq * dec).astype(BF16), k.astype(BF16)
            dkb = _dot(dgb, kbf)
            tk = rsum(dbk * k)
            rk = rsum(dkd * k) * ekd
            dq = _dot(dhb, kbf) + egc * dqg
            dk = _dot_tn(dgb, kb.astype(BF16)) + _dot_tn(dhb, q.astype(BF16)) + beta * (egc * dbk + dkb) + ekd * dkd
            dbeta = rsum(dbv * v) + tk * egc + rsum(dkb * k)
            dgc = dgc + tk * beta * egc + egc * rsum(dqg * q) - rk
            dglast = jnp.sum(rk, axis=0, keepdims=True) + dgl * gl
            dgc = dgc + jnp.where(_subl(dgc.shape) == CHUNK - 1, dglast, 0.0)
            dqkv_ref[0, rows, hs] = dq
            dqkv_ref[1, rows, hs] = dk
            dqkv_ref[2, rows, hs] = beta * dbv
            lane = _lane((CHUNK, LANES))
            dg_ref[h, rows, :] = jnp.where(lane == 0, dbeta, jnp.where(lane == 1, dgc, 0.0))

        lax.fori_loop(0, nch, step, 0)

    row = lambda b, i: b * nt + nt - 1 - i
    wb = lambda col: BS((tt, C_W), lambda b, i: (row(b, i), col))
    return pl.pallas_call(
        body, name="gdn_chunk_bwd", grid=(nseq, nt),
        in_specs=[wb(0), wb(1), wb(2), BS((tt, LANES), lambda b, i: (row(b, i), 0)), wb(P_ZC // C_W),
                  BS((1, LANES), lambda b, i: (0, 0)), wb(0), wb(Y_C // C_W), wb(0), wb(0), wb(0),
                  BS((None, C_HEADS, nch, C_DH, C_DH), lambda b, i: (b, 0, nt - 1 - i, 0, 0)), BS(memory_space=pl.ANY)],
        out_specs=[wb(P_ZC // C_W), BS((3, tt, C_W), lambda b, i: (0, row(b, i), 0)),
                   BS((C_HEADS, tt, LANES), lambda b, i: (0, row(b, i), 0)), BS((SUB, LANES), lambda b, i: (0, 0))],
        out_shape=[S(dp.shape, dp.dtype), S((3, n, C_W), F32), S((C_HEADS, n, LANES), F32), S((SUB, LANES), F32)],
        input_output_aliases={12: 0},
        scratch_shapes=[pltpu.VMEM((C_HEADS, C_DH, C_DH), F32)],
        compiler_params=_cp("arbitrary", "arbitrary"),
    )(qkv, qkv, qkv, gates, p, onw, o, dy, u, w, tinv, ss, dp)


def _gdn_gates_bwd(dgate, p, alog_l, dtb_l, dp, tm=256):
    n = p.shape[0]
    acc = BS((SUB, LANES), lambda i: (0, 0))

    def body(dg_ref, ba_ref, al_ref, db_ref, dp_in_ref, dba_ref, dal_ref, ddb_ref):
        @pl.when(pl.program_id(0) == 0)
        def _():
            dal_ref[...] = jnp.zeros_like(dal_ref)
            ddb_ref[...] = jnp.zeros_like(ddb_ref)

        blk = ba_ref[...]
        lane = _lane(blk.shape)
        dbeta = jnp.zeros_like(blk)
        dgc = jnp.zeros_like(blk)
        for h in range(C_HEADS):
            dbeta = dbeta + jnp.where(lane == GB + h, _col(dg_ref[h], 0), 0.0)
            dgc = dgc + jnp.where(lane == GG + h, _col(dg_ref[h], 1), 0.0)
        tri = (_subl((CHUNK, CHUNK)) <= _lane((CHUNK, CHUNK))).astype(F32)
        dg = jnp.concatenate([_dot(tri, dgc[CHUNK * c:CHUNK * (c + 1)], HI) for c in range(tm // CHUNK)], axis=0)
        beta = _sigmoid(blk)
        z = blk + db_ref[...]
        ea = jnp.exp(al_ref[...])
        isg = (lane >= GG) & (lane < GG + C_HEADS)
        dz = jnp.where(isg, -dg * ea * _sigmoid(z), 0.0)
        dal_ref[...] += _fold8(jnp.where(isg, -dg * ea * _softplus(z), 0.0))
        ddb_ref[...] += _fold8(dz)
        out = jnp.where(lane < GG, dbeta * beta * (1.0 - beta), dz)
        dba_ref[...] = jnp.concatenate([out, jnp.zeros_like(out)], axis=1).astype(BF16)

    return pl.pallas_call(
        body, name="gdn_gates_bwd", grid=(n // tm,),
        in_specs=[BS((C_HEADS, tm, LANES), lambda i: (0, i, 0)), BS((tm, LANES), lambda i: (i, P_BA // LANES)),
                  BS((1, LANES), lambda i: (0, 0)), BS((1, LANES), lambda i: (0, 0)), BS(memory_space=pl.ANY)],
        out_specs=[BS((tm, 2 * LANES), lambda i: (i, P_BA // (2 * LANES))), acc, acc],
        out_shape=[S(dp.shape, dp.dtype), S((SUB, LANES), F32), S((SUB, LANES), F32)],
        input_output_aliases={4: 0},
        compiler_params=_cp("arbitrary"),
    )(dgate, p, alog_l, dtb_l, dp)


def _gdn_pre_bwd(p, dqkv, xc, dp, ccw, nseq, tm=256):
    n = p.shape[0]
    t = n // nseq
    nt = t // tm
    wide = 3 * C_W
    row = lambda b, i: b * nt + i
    prev = lambda b, i: jnp.maximum((b * t + i * tm) // HALO_C - 1, 0)
    nxt = lambda b, i: jnp.minimum((b * t + (i + 1) * tm) // HALO_C, n // HALO_C - 1)

    def d_conv_out(d, xc, part):
        sg = _sigmoid(xc)
        act = xc * sg
        if part < 2:
            cs = QS if part == 0 else 1.0
            rn = lax.rsqrt(jnp.sum(act * act, axis=-1, keepdims=True) + EPS)
            d = cs * rn * d - act * (cs * rn * rn * rn * jnp.sum(d * act, axis=-1, keepdims=True))
        return d * _dsilu(xc, sg)

    def body(x_ref, xh_ref, d_ref, dn_ref, xc_ref, xn_ref, w_ref, dp_in_ref, dx_ref, dw_ref, buf_ref, dbuf_ref):
        i = pl.program_id(1)

        @pl.when((pl.program_id(0) == 0) & (i == 0))
        def _():
            dw_ref[...] = jnp.zeros_like(dw_ref)

        buf_ref[0:HALO_C, :] = jnp.where(i > 0, xh_ref[...], 0.0)
        buf_ref[HALO_C:, :] = x_ref[...]
        for c in range(NCB):
            cs = slice(LANES * c, LANES * (c + 1))
            part, hd = divmod(c, C_HEADS)
            hs = slice(LANES * hd, LANES * (hd + 1))
            d = d_conv_out(d_ref[part, :, hs], xc_ref[:, cs], part)
            dbuf_ref[0:tm, cs] = d
            dbuf_ref[tm:, cs] = jnp.where(i < nt - 1, d_conv_out(dn_ref[part, :, hs], xn_ref[:, cs], part), 0.0)
            dx = jnp.zeros((tm, LANES), F32)
            for k in range(C_K):
                dx = dx + w_ref[k:k + 1, cs] * dbuf_ref[pl.ds(C_K - 1 - k, tm), cs]
                dw_ref[SUB * k:SUB * (k + 1), cs] += _fold8(d * buf_ref[pl.ds(HALO_C - C_K + 1 + k, tm), cs])
            dx_ref[:, cs] = dx.astype(BF16)

    return pl.pallas_call(
        body, name="gdn_pre_bwd", grid=(nseq, nt),
        in_specs=[BS((tm, wide), lambda b, i: (row(b, i), P_QKV // wide)), BS((HALO_C, wide), lambda b, i: (prev(b, i), P_QKV // wide)),
                  BS((3, tm, C_W), lambda b, i: (0, row(b, i), 0)), BS((3, HALO_C, C_W), lambda b, i: (0, nxt(b, i), 0)),
                  BS((tm, wide), lambda b, i: (row(b, i), 0)), BS((HALO_C, wide), lambda b, i: (nxt(b, i), 0)),
                  BS((SUB, wide), lambda b, i: (0, 0)), BS(memory_space=pl.ANY)],
        out_specs=[BS((tm, wide), lambda b, i: (row(b, i), P_QKV // wide)), BS((SUB * C_K, wide), lambda b, i: (0, 0))],
        out_shape=[S(dp.shape, dp.dtype), S((SUB * C_K, wide), F32)], input_output_aliases={7: 0},
        scratch_shapes=[pltpu.VMEM((HALO_C + tm, wide), F32)] * 2,
        compiler_params=_cp("arbitrary", "arbitrary"),
    )(p, p, dqkv, dqkv, xc, xc, ccw, dp)


ANY = BS(memory_space=pl.ANY)


def _my_pos():
    return lax.axis_index("x"), lax.axis_index("y"), lax.axis_index("c")


def _dev_index(dev):
    return 4 * dev[0] + 2 * dev[1] + dev[2]


def _all_gather(shards):
    nk = len(shards)

    def body(*refs):
        ins, outs = refs[:nk], refs[nk:2 * nk]
        send, recv, loc = refs[2 * nk:]
        x, y, c = _my_pos()
        me, sib = (x, y, c), (x, y, 1 - c)
        chips = [(1 - x, y), (x, 1 - y), (1 - x, 1 - y)]

        def rows(t, dev):
            r = ins[t].shape[0]
            return outs[t].at[pl.ds(pl.multiple_of(_dev_index(dev) * r, SUB), r), :]

        def copy(t, k, block, to, src=None):
            return pltpu.make_async_remote_copy(
                src_ref=rows(t, block) if src is None else src, dst_ref=rows(t, block),
                send_sem=send.at[t, k], recv_sem=recv.at[t, k], device_id=to, device_id_type=MESH)

        mine = [pltpu.make_async_copy(ins[t], rows(t, me), loc.at[t]) for t in range(nk)]
        for cp in mine:
            cp.start()
        first = []
        for t in range(nk):
            first.append(copy(t, 0, me, sib, src=ins[t]))
            first += [copy(t, 1 + j, me, (*chip, c), src=ins[t]) for j, chip in enumerate(chips)]
        for cp in first:
            cp.start()
        passed = []
        for j, chip in enumerate(chips):
            for t in range(nk):
                copy(t, 1 + j, (*chip, c), me).wait_recv()
                cp = copy(t, 4 + j, (*chip, c), sib)
                cp.start()
                passed.append(cp)
        for t in range(nk):
            copy(t, 0, sib, me).wait_recv()
            for j, chip in enumerate(chips):
                copy(t, 4 + j, (*chip, 1 - c), me).wait_recv()
        for cp in first + passed:
            cp.wait_send()
        for cp in mine:
            cp.wait()

    return pl.pallas_call(
        body, name="all_gather", in_specs=[ANY] * nk, out_specs=[ANY] * nk,
        out_shape=[S((N_DEV * a.shape[0], a.shape[1]), a.dtype) for a in shards],
        scratch_shapes=[pltpu.SemaphoreType.DMA((nk, 7)), pltpu.SemaphoreType.DMA((nk, 7)), pltpu.SemaphoreType.DMA((nk,))],
    )(*shards)


def _scatter_blocks(parts):
    nk = len(parts)

    def body(*refs):
        ins, outs = refs[:nk], refs[nk:2 * nk]
        send, recv, loc = refs[2 * nk:]
        x, y, c = _my_pos()
        me = _dev_index((x, y, c))
        peers = [((1 - x) if k & 4 else x, (1 - y) if k & 2 else y, (1 - c) if k & 1 else c) for k in range(1, N_DEV)]

        def block(t, dev):
            r = ins[t].shape[0] // N_DEV
            return ins[t].at[pl.ds(pl.multiple_of(_dev_index(dev) * r, SUB), r), :]

        mine = [pltpu.make_async_copy(block(t, (x, y, c)), outs[t].at[me], loc.at[t]) for t in range(nk)]
        for cp in mine:
            cp.start()
        sent = []
        for t in range(nk):
            for k, peer in enumerate(peers):
                cp = pltpu.make_async_remote_copy(src_ref=block(t, peer), dst_ref=outs[t].at[me], send_sem=send.at[t, k],
                                                  recv_sem=recv.at[t, k], device_id=peer, device_id_type=MESH)
                cp.start()
                sent.append(cp)
        for t in range(nk):
            for k, peer in enumerate(peers):
                pltpu.make_async_remote_copy(src_ref=block(t, peer), dst_ref=outs[t].at[_dev_index(peer)], send_sem=send.at[t, k],
                                             recv_sem=recv.at[t, k], device_id=peer, device_id_type=MESH).wait_recv()
        for cp in sent:
            cp.wait_send()
        for cp in mine:
            cp.wait()

    return pl.pallas_call(
        body, name="scatter_blocks", in_specs=[ANY] * nk, out_specs=[ANY] * nk,
        out_shape=[S((N_DEV, a.shape[0] // N_DEV, a.shape[1]), a.dtype) for a in parts],
        scratch_shapes=[pltpu.SemaphoreType.DMA((nk, 7)), pltpu.SemaphoreType.DMA((nk, 7)), pltpu.SemaphoreType.DMA((nk,))],
    )(*parts)


BLOCK_BYTES = 4 << 20


def _row_tile(rows, row_bytes, align):
    best = align
    for tr in range(align, rows + 1, align):
        if rows % tr == 0 and tr * row_bytes <= BLOCK_BYTES:
            best = tr
    return best


def _sum8(a):
    _, r, w = a.shape
    tr = _row_tile(r, N_DEV * w * a.dtype.itemsize, 32 // a.dtype.itemsize)

    def body(a_ref, o_ref):
        acc = a_ref[0].astype(F32)
        for d in range(1, N_DEV):
            acc = acc + a_ref[d].astype(F32)
        o_ref[...] = acc

    return pl.pallas_call(
        body, name="sum8", grid=(r // tr,), in_specs=[BS((N_DEV, tr, w), lambda i: (0, i, 0))],
        out_specs=BS((tr, w), lambda i: (i, 0)), out_shape=S((r, w), F32), compiler_params=_cp("arbitrary"),
    )(a)


def _adamw(w, g, m, v):
    r, c = w.shape
    tr = _row_tile(r, c * 4 * 2, SUB)

    def body(w_ref, g_ref, m_ref, v_ref, d_ref, mo_ref, vo_ref):
        gv = g_ref[...]
        m2 = ADAM_B1 * m_ref[...] + (1.0 - ADAM_B1) * gv
        v2 = ADAM_B2 * v_ref[...] + (1.0 - ADAM_B2) * (gv * gv)
        m_hat = m2 / (1.0 - ADAM_B1 ** ADAM_STEP)
        v_hat = v2 / (1.0 - ADAM_B2 ** ADAM_STEP)
        d_ref[...] = -ADAM_LR * (m_hat / (jnp.sqrt(v_hat) + ADAM_EPS) + ADAM_WD * w_ref[...])
        mo_ref[...] = m2
        vo_ref[...] = v2

    blk = BS((tr, c), lambda i: (i, 0))
    return pl.pallas_call(
        body, name="adamw", grid=(r // tr,), in_specs=[blk] * 4, out_specs=[blk] * 3,
        out_shape=[S((r, c), F32)] * 3, compiler_params=_cp("arbitrary"),
    )(w, g, m, v)


def _blob(arrays):
    flat = jnp.concatenate([a.reshape(-1) for a in arrays])
    rows = -(-flat.shape[0] // (SUB * LANES)) * SUB
    return jnp.pad(flat, (0, rows * LANES - flat.shape[0])).reshape(rows, LANES)


def _unblob(blob, shapes, lead=()):
    flat = blob.reshape(lead + (-1,))
    out, off = [], 0
    for s in shapes:
        size = math.prod(s)
        out.append(flat[..., off:off + size].reshape(lead + tuple(s)))
        off += size
    return out


def _lanes6(a):
    return jnp.zeros((1, LANES), F32).at[0, GG:GG + C_HEADS].set(a)


def _y_rows(w):
    return jnp.concatenate([w[0:A_W], w[A_W + B_W:], w[A_W:A_W + B_W]], axis=0)


def _y_rows_back(g):
    return jnp.concatenate([g[0:A_W], g[A_W + C_W:], g[A_W:A_W + C_W]], axis=0)


SMALL = ("norm_w", "q_norm_w", "k_norm_w", "sinks", "b_conv_b", "b_ln_w", "b_ln_b", "b_pw_b", "c_a_log", "c_dt_bias",
         "c_onorm_w", "b_conv_w", "c_conv_w")
ORDER = ("norm_w", "w_in", "q_norm_w", "k_norm_w", "sinks", "b_conv_w", "b_conv_b", "b_ln_w", "b_ln_b", "b_pw_w", "b_pw_b",
         "c_conv_w", "c_a_log", "c_dt_bias", "c_onorm_w", "w_out")


def kernel(x, positions, norm_w, w_in, q_norm_w, k_norm_w, sinks, b_conv_w, b_conv_b, b_ln_w, b_ln_b, b_pw_w, b_pw_b, c_conv_w, c_a_log, c_dt_bias, c_onorm_w, w_out, loss_target, m_norm_w, m_w_in, m_q_norm_w, m_k_norm_w, m_sinks, m_b_conv_w, m_b_conv_b, m_b_ln_w, m_b_ln_b, m_b_pw_w, m_b_pw_b, m_c_conv_w, m_c_a_log, m_c_dt_bias, m_c_onorm_w, m_w_out, v_norm_w, v_w_in, v_q_norm_w, v_k_norm_w, v_sinks, v_b_conv_w, v_b_conv_b, v_b_ln_w, v_b_ln_b, v_b_pw_w, v_b_pw_b, v_c_conv_w, v_c_a_log, v_c_dt_bias, v_c_onorm_w, v_w_out):
    W = dict(norm_w=norm_w, w_in=w_in, q_norm_w=q_norm_w, k_norm_w=k_norm_w, sinks=sinks, b_conv_w=b_conv_w, b_conv_b=b_conv_b,
             b_ln_w=b_ln_w, b_ln_b=b_ln_b, b_pw_w=b_pw_w, b_pw_b=b_pw_b, c_conv_w=c_conv_w, c_a_log=c_a_log,
             c_dt_bias=c_dt_bias, c_onorm_w=c_onorm_w, w_out=w_out)
    M = dict(norm_w=m_norm_w, w_in=m_w_in, q_norm_w=m_q_norm_w, k_norm_w=m_k_norm_w, sinks=m_sinks, b_conv_w=m_b_conv_w,
             b_conv_b=m_b_conv_b, b_ln_w=m_b_ln_w, b_ln_b=m_b_ln_b, b_pw_w=m_b_pw_w, b_pw_b=m_b_pw_b, c_conv_w=m_c_conv_w,
             c_a_log=m_c_a_log, c_dt_bias=m_c_dt_bias, c_onorm_w=m_c_onorm_w, w_out=m_w_out)
    V = dict(norm_w=v_norm_w, w_in=v_w_in, q_norm_w=v_q_norm_w, k_norm_w=v_k_norm_w, sinks=v_sinks, b_conv_w=v_b_conv_w,
             b_conv_b=v_b_conv_b, b_ln_w=v_b_ln_w, b_ln_b=v_b_ln_b, b_pw_w=v_b_pw_w, b_pw_b=v_b_pw_b, c_conv_w=v_c_conv_w,
             c_a_log=v_c_a_log, c_dt_bias=v_c_dt_bias, c_onorm_w=v_c_onorm_w, w_out=v_w_out)
    nseq, t, d = x.shape
    n = nseq * t
    tr = min(256, t)
    tmm = min(512, n)
    me = _dev_index(_my_pos())
    xs = [x.reshape(n, d)]
    tgt = loss_target.reshape(n, d)
    tabs = _rope_tables(positions.reshape(n))

    win_p = _pack_cols(w_in).astype(BF16)
    wout_b = w_out.astype(BF16)
    sharded_small = (b_pw_w, b_conv_w, c_conv_w)
    g_win0, g_win1, g_wout0, g_wout1, g_small = _all_gather(
        [win_p[0], win_p[1], wout_b[0], wout_b[1], _blob(sharded_small)])
    win = [g_win0, g_win1]
    wout = [_y_rows(g_wout0), _y_rows(g_wout1)]
    pw_all, cw_all, ccw_all = _unblob(g_small, [a.shape for a in sharded_small], lead=(N_DEV,))
    pw_all = pw_all.transpose(1, 0, 2, 3).reshape(DEPTH, B_W, B_W).astype(BF16)
    cw_all = cw_all.transpose(1, 2, 0, 3).reshape(DEPTH, B_K, B_W)
    ccw_all = ccw_all.transpose(1, 2, 0, 3).reshape(DEPTH, C_K, 3 * C_W)

    def layer_params(l):
        return dict(
            nw=norm_w[l][None], qw=jnp.tile(q_norm_w[l], 2)[None], kw=jnp.tile(k_norm_w[l], 2)[None], sinks=sinks[l],
            cw=jnp.pad(cw_all[l], ((0, HALO_B - B_K), (0, 0))), cb=b_conv_b[l][None], lnw=b_ln_w[l][None], lnb=b_ln_b[l][None],
            pw=pw_all[l], pwb=b_pw_b[l][None], ccw=jnp.pad(ccw_all[l], ((0, SUB - C_K), (0, 0))),
            alog=_lanes6(c_a_log[l]), dtb=_lanes6(c_dt_bias[l]), onw=c_onorm_w[l][None])

    saved = []
    for l in range(DEPTH):
        q = layer_params(l)
        p, h = _inproj(xs[l], q["nw"], win[l], tm=tmm)
        y, o_a, lse = _attn_fwd(p, tabs, q["qw"], q["kw"], q["sinks"], nseq)
        gates = _gdn_gates_fwd(p, q["alog"], q["dtb"], tm=tr)
        xc, qkv = _gdn_pre_fwd(p, q["ccw"], nseq, tm=tr)
        y, o_c, u, w, tinv, ss = _gdn_chunk_fwd(qkv, gates, p, y, q["onw"], nseq)
        y, hc = _conf_fwd(p, y, q["cw"], q["cb"], q["lnw"], q["lnb"], q["pw"], q["pwb"], nseq, tm=tr)
        saved.append(dict(q=q, p=p, h=h, y=y, o_a=o_a, lse=lse, gates=gates, xc=xc, qkv=qkv, o_c=o_c, u=u, w=w, tinv=tinv,
                          ss=ss, hc=hc))
        if l + 1 < DEPTH:
            xs.append(_outproj(xs[l], y, wout[l], tm=tmm))
        else:
            dxn, lsum = _outproj_loss(xs[l], y, wout[l], tgt, tm=tmm)
    loss = lax.psum(jnp.sum(lsum) * (0.5 / d), ("x", "y", "c"))

    dwin, dwout, dpw, smalls = [None] * DEPTH, [None] * DEPTH, [None] * DEPTH, [None] * DEPTH
    for l in reversed(range(DEPTH)):
        s = saved[l]
        q, p = s["q"], s["p"]
        dy = _matmul(dxn, wout[l], "nt", F32, tmm, 512, d, "outproj_bwd_dy")
        dwout[l] = _y_rows_back(_matmul(s["y"], dxn, "tn", BF16, 1024, 1024, tmm, "outproj_bwd_dw"))
        dp, dkv, dqw, dkw, dsk = _attn_bwd(p, dy, s["o_a"], s["lse"], tabs, q["qw"], q["kw"], q["sinks"], nseq)
        dp = _put_cols(dp, dkv, P_K, tm=tmm)
        dp, dqkv, dgate, donw = _gdn_chunk_bwd(s["qkv"], s["gates"], p, dy, dp, q["onw"], s["o_c"], s["u"], s["w"],
                                               s["tinv"], s["ss"], nseq)
        dp, dal, ddb = _gdn_gates_bwd(dgate, p, q["alog"], q["dtb"], dp, tm=tr)
        dp, dccw = _gdn_pre_bwd(p, dqkv, s["xc"], dp, q["ccw"], nseq, tm=tr)
        dp, dhc, dpw[l], dpwb, dlnw, dlnb, dcb = _conf_bwd1(p, dy, dp, s["hc"], q["lnw"], q["lnb"], q["pw"], q["pwb"], tm=tr)
        dp, dcw = _conf_bwd2(p, dhc, dp, q["cw"], nseq, tm=tr)
        dwin[l] = _matmul(s["h"], dp, "tn", BF16, 1024, 768, tmm, "inproj_bwd_dw")
        dxn, dnw = _inproj_bwd_dx(dp, win[l], xs[l], q["nw"], dxn, tm=tmm)
        halves = lambda a: a.sum(0)[:A_DH] + a.sum(0)[A_DH:]
        smalls[l] = dict(
            norm_w=dnw.sum(0), q_norm_w=halves(dqw), k_norm_w=halves(dkw), sinks=dsk.sum(0)[:A_HEADS], b_conv_b=dcb.sum(0),
            b_ln_w=dlnw.sum(0), b_ln_b=dlnb.sum(0), b_pw_b=dpwb.sum(0), c_a_log=dal.sum(0)[GG:GG + C_HEADS],
            c_dt_bias=ddb.sum(0)[GG:GG + C_HEADS], c_onorm_w=donw.sum(0),
            b_conv_w=dcw.reshape(B_K, SUB, B_W).sum(1), c_conv_w=dccw.reshape(C_K, SUB, 3 * C_W).sum(1))
    grad_x = dxn.reshape(nseq, t, d)

    r_win0, r_win1, r_wout0, r_wout1, r_pw0, r_pw1 = _scatter_blocks([dwin[0], dwin[1], dwout[0], dwout[1], dpw[0], dpw[1]])
    G = {}
    G["w_in"] = jnp.stack([_unpack_cols(_sum8(r_win0)), _unpack_cols(_sum8(r_win1))])
    G["w_out"] = jnp.stack([_sum8(r_wout0), _sum8(r_wout1)])
    G["b_pw_w"] = jnp.stack([_sum8(r_pw0), _sum8(r_pw1)])
    part = _blob([jnp.stack([smalls[l][k] for l in range(DEPTH)]) for k in SMALL])
    (tot,) = _all_gather([part])
    tot = _sum8(tot.reshape(N_DEV, part.shape[0], LANES))
    full_shapes = [(DEPTH,) + smalls[0][k].shape for k in SMALL]
    for k, g in zip(SMALL, _unblob(tot, full_shapes)):
        G[k] = g
    G["b_conv_w"] = lax.dynamic_slice_in_dim(G["b_conv_w"], me * (B_W // N_DEV), B_W // N_DEV, axis=2)
    G["c_conv_w"] = lax.dynamic_slice_in_dim(G["c_conv_w"], me * (3 * C_W // N_DEV), 3 * C_W // N_DEV, axis=2)

    delta, new_m, new_v = {}, {}, {}
    for k in ("w_in", "w_out", "b_pw_w"):
        shp = W[k].shape
        two = lambda a: a.reshape(shp[0] * shp[1], shp[2])
        dl, mo, vo = _adamw(two(W[k]), two(G[k]), two(M[k]), two(V[k]))
        delta[k], new_m[k], new_v[k] = dl.reshape(shp), mo.reshape(shp), vo.reshape(shp)
    dl, mo, vo = _adamw(*[_blob([src[k] for k in SMALL]) for src in (W, G, M, V)])
    shapes = [W[k].shape for k in SMALL]
    for k, a, b, c in zip(SMALL, _unblob(dl, shapes), _unblob(mo, shapes), _unblob(vo, shapes)):
        delta[k], new_m[k], new_v[k] = a, b, c
    return (loss, grad_x, *[G[k] for k in ORDER], *[delta[k] for k in ORDER], *[new_m[k] for k in ORDER],
            *[new_v[k] for k in ORDER])
```

```python
import functools
import math

import jax
import jax.numpy as jnp
from jax import lax
from jax.experimental import pallas as pl
from jax.experimental.pallas import tpu as pltpu

F32 = jnp.float32
BF16 = jnp.bfloat16
HI = lax.Precision.HIGHEST
MESH = pl.DeviceIdType.MESH
S = jax.ShapeDtypeStruct
BS = pl.BlockSpec

N_DEV = 8
DEPTH = 2
D_MODEL = 2048
A_HEADS, A_KV, A_DH, A_W, A_KVW = 12, 4, 64, 768, 256
ROT = 16
THETA = 500000.0
ABLK = 128
B_W, B_K = 512, 31
C_HEADS, C_DH, C_W, C_K, CHUNK = 6, 128, 768, 4, 64
EPS = 1e-6
IN_COLS = 6668
P_Q, P_ZA, P_ZC, P_QKV, P_K, P_V, P_UB, P_ZB, P_BA, P_W = 0, 768, 1536, 2304, 4608, 4864, 5120, 6144, 6656, 6912
Y_A, Y_C, Y_B = 0, 768, 1536
LANES = 128
SUB = 8

ADAM_LR, ADAM_B1, ADAM_B2, ADAM_EPS, ADAM_WD, ADAM_STEP = 0.001, 0.9, 0.999, 1e-08, 0.01, 10


def _cp(*sem, vmem=None):
    kw = {}
    if sem:
        kw["dimension_semantics"] = sem
    if vmem:
        kw["vmem_limit_bytes"] = vmem
    return pltpu.CompilerParams(**kw)


def _pack_cols(w):
    z = jnp.zeros(w.shape[:-1] + (P_W - IN_COLS,), w.dtype)
    return jnp.concatenate([w[..., 0:768], w[..., 1280:2048], w[..., 5900:6668], w[..., 3584:5888],
                            w[..., 768:1024], w[..., 1024:1280], w[..., 2048:3072], w[..., 3072:3584],
                            w[..., 5888:5900], z], axis=-1)


def _unpack_cols(g):
    return jnp.concatenate([g[..., P_Q:P_Q + 768], g[..., P_K:P_K + 256], g[..., P_V:P_V + 256], g[..., P_ZA:P_ZA + 768],
                            g[..., P_UB:P_UB + 1024], g[..., P_ZB:P_ZB + 512], g[..., P_QKV:P_QKV + 2304],
                            g[..., P_BA:P_BA + 12], g[..., P_ZC:P_ZC + 768]], axis=-1)


def _sigmoid(x):
    return 1.0 / (1.0 + jnp.exp(-x))


def _dsilu(x, sg):
    return sg * (1.0 + x * (1.0 - sg))


def _fold8(x):
    r, c = x.shape
    return x.reshape(r // SUB, SUB, c).sum(axis=0)


def _dot(a, b, prec=None):
    return jnp.dot(a, b, preferred_element_type=F32, precision=prec)


def _dot_nt(a, b, prec=None):
    return lax.dot_general(a, b, (((1,), (1,)), ((), ())), preferred_element_type=F32, precision=prec)


def _dot_tn(a, b, prec=None):
    return lax.dot_general(a, b, (((0,), (0,)), ((), ())), preferred_element_type=F32, precision=prec)


def _lane(shape):
    return lax.broadcasted_iota(jnp.int32, shape, 1)


def _subl(shape):
    return lax.broadcasted_iota(jnp.int32, shape, 0)


def _col(x, j):
    return jnp.sum(jnp.where(_lane(x.shape) == j, x, 0.0), axis=-1, keepdims=True)


def _inproj(x, nw, w, tm=512, tn=768):
    n, d = x.shape
    pw = w.shape[1]

    def body(x_ref, nw_ref, w_ref, p_ref, h_ref):
        @pl.when(pl.program_id(1) == 0)
        def _():
            xv = x_ref[...]
            r = lax.rsqrt(jnp.mean(xv * xv, axis=-1, keepdims=True) + EPS)
            h_ref[...] = (xv * r * nw_ref[...]).astype(BF16)

        p_ref[...] = _dot(h_ref[...], w_ref[...])

    return pl.pallas_call(
        body, name="inproj", grid=(n // tm, pw // tn),
        in_specs=[BS((tm, d), lambda i, j: (i, 0)), BS((1, d), lambda i, j: (0, 0)), BS((d, tn), lambda i, j: (0, j))],
        out_specs=[BS((tm, tn), lambda i, j: (i, j)), BS((tm, d), lambda i, j: (i, 0))],
        out_shape=[S((n, pw), F32), S((n, d), BF16)],
        compiler_params=_cp("arbitrary", "arbitrary"),
    )(x, nw, w)


def _outproj(x, y, w, tm=512, tn=1024):
    n, d = x.shape
    k = y.shape[1]

    def body(x_ref, y_ref, w_ref, o_ref):
        o_ref[...] = x_ref[...] + _dot(y_ref[...], w_ref[...])

    return pl.pallas_call(
        body, name="outproj", grid=(n // tm, d // tn),
        in_specs=[BS((tm, tn), lambda i, j: (i, j)), BS((tm, k), lambda i, j: (i, 0)), BS((k, tn), lambda i, j: (0, j))],
        out_specs=BS((tm, tn), lambda i, j: (i, j)),
        out_shape=S((n, d), F32),
        compiler_params=_cp("arbitrary", "arbitrary"),
    )(x, y, w)


def _outproj_loss(x, y, w, tgt, tm=512, tn=1024):
    n, d = x.shape
    k = y.shape[1]

    def body(x_ref, y_ref, w_ref, t_ref, g_ref, l_ref):
        @pl.when((pl.program_id(0) == 0) & (pl.program_id(1) == 0))
        def _():
            l_ref[...] = jnp.zeros_like(l_ref)

        diff = x_ref[...] + _dot(y_ref[...], w_ref[...]) - t_ref[...]
        g_ref[...] = diff * (1.0 / d)
        f = _fold8(diff * diff)
        acc = f[:, 0:LANES]
        for c in range(1, tn // LANES):
            acc = acc + f[:, c * LANES:(c + 1) * LANES]
        l_ref[...] += acc

    return pl.pallas_call(
        body, name="outproj_loss", grid=(n // tm, d // tn),
        in_specs=[BS((tm, tn), lambda i, j: (i, j)), BS((tm, k), lambda i, j: (i, 0)), BS((k, tn), lambda i, j: (0, j)),
                  BS((tm, tn), lambda i, j: (i, j))],
        out_specs=[BS((tm, tn), lambda i, j: (i, j)), BS((SUB, LANES), lambda i, j: (0, 0))],
        out_shape=[S((n, d), F32), S((SUB, LANES), F32)],
        compiler_params=_cp("arbitrary", "arbitrary"),
    )(x, y, w, tgt)


def _matmul(a, b, mode, out_dtype, tm, tn, tk, name):
    if mode == "nn":
        (m, kk), nn = a.shape, b.shape[1]
        a_spec, b_spec = BS((tm, tk), lambda i, j, k: (i, k)), BS((tk, tn), lambda i, j, k: (k, j))
        dot = _dot
    elif mode == "nt":
        (m, kk), nn = a.shape, b.shape[0]
        a_spec, b_spec = BS((tm, tk), lambda i, j, k: (i, k)), BS((tn, tk), lambda i, j, k: (j, k))
        dot = _dot_nt
    else:
        (kk, m), nn = a.shape, b.shape[1]
        a_spec, b_spec = BS((tk, tm), lambda i, j, k: (k, i)), BS((tk, tn), lambda i, j, k: (k, j))
        dot = _dot_tn
    nk = kk // tk

    def body(a_ref, b_ref, o_ref, acc_ref):
        kid = pl.program_id(2)

        @pl.when(kid == 0)
        def _():
            acc_ref[...] = jnp.zeros_like(acc_ref)

        acc_ref[...] += dot(a_ref[...].astype(BF16), b_ref[...].astype(BF16))

        @pl.when(kid == nk - 1)
        def _():
            o_ref[...] = acc_ref[...].astype(out_dtype)

    return pl.pallas_call(
        body, name=name, grid=(m // tm, nn // tn, nk),
        in_specs=[a_spec, b_spec], out_specs=BS((tm, tn), lambda i, j, k: (i, j)),
        out_shape=S((m, nn), out_dtype), scratch_shapes=[pltpu.VMEM((tm, tn), F32)],
        compiler_params=_cp("arbitrary", "arbitrary", "arbitrary"),
    )(a, b)


def _inproj_bwd_dx(dp, w, x, nw, dres, tm=512, tk=768):
    n, d = x.shape
    nk = dp.shape[1] // tk

    def body(dp_ref, w_ref, x_ref, nw_ref, dr_ref, dx_ref, dnw_ref, acc_ref):
        kid = pl.program_id(1)

        @pl.when((pl.program_id(0) == 0) & (kid == 0))
        def _():
            dnw_ref[...] = jnp.zeros_like(dnw_ref)

        @pl.when(kid == 0)
        def _():
            acc_ref[...] = jnp.zeros_like(acc_ref)

        acc_ref[...] += _dot_nt(dp_ref[...], w_ref[...])

        @pl.when(kid == nk - 1)
        def _():
            dh = acc_ref[...]
            xv = x_ref[...]
            r = lax.rsqrt(jnp.mean(xv * xv, axis=-1, keepdims=True) + EPS)
            dnw_ref[...] += _fold8(dh * xv * r)
            g = dh * nw_ref[...]
            mm = jnp.mean(g * xv, axis=-1, keepdims=True)
            dx_ref[...] = dr_ref[...] + r * g - xv * (r * r * r * mm)

    return pl.pallas_call(
        body, name="inproj_bwd_dx", grid=(n // tm, nk),
        in_specs=[BS((tm, tk), lambda i, k: (i, k)), BS((d, tk), lambda i, k: (0, k)), BS((tm, d), lambda i, k: (i, 0)),
                  BS((1, d), lambda i, k: (0, 0)), BS((tm, d), lambda i, k: (i, 0))],
        out_specs=[BS((tm, d), lambda i, k: (i, 0)), BS((SUB, d), lambda i, k: (0, 0))],
        out_shape=[S((n, d), F32), S((SUB, d), F32)],
        scratch_shapes=[pltpu.VMEM((tm, d), F32)],
        compiler_params=_cp("arbitrary", "arbitrary"),
    )(dp, w, x, nw, dres)


def _rope_tables(pos):
    half = ROT // 2
    inv = THETA ** (-jnp.arange(0, ROT, 2, dtype=F32) / ROT)
    ang = pos.astype(F32)[:, None] * inv
    cos, sin = jnp.cos(ang), jnp.sin(ang)
    n = pos.shape[0]
    one = jnp.ones((n, A_DH - ROT), F32)
    zero = jnp.zeros((n, A_DH - ROT), F32)
    zh = jnp.zeros((n, half), F32)
    c = jnp.concatenate([cos, cos, one], axis=1)
    s1 = jnp.concatenate([-sin, zh, zero], axis=1)
    s2 = jnp.concatenate([zh, sin, zero], axis=1)
    return tuple(jnp.concatenate([t, t], axis=1) for t in (c, s1, s2))


def _half_stat(t):
    lo = _lane(t.shape) < A_DH
    s_lo = jnp.sum(jnp.where(lo, t, 0.0), axis=-1, keepdims=True)
    s_hi = jnp.sum(jnp.where(lo, 0.0, t), axis=-1, keepdims=True)
    return jnp.where(lo, s_lo, s_hi)


def _normrope(x, w, c, s1, s2):
    r = lax.rsqrt(_half_stat(x * x) * (1.0 / A_DH) + EPS)
    xn = x * r * w
    return xn * c + pltpu.roll(xn, LANES - ROT // 2, 1) * s1 + pltpu.roll(xn, ROT // 2, 1) * s2, r


def _normrope_bwd(dy, x, r, w, c, s1, s2):
    dxn = dy * c + pltpu.roll(dy * s1, ROT // 2, 1) + pltpu.roll(dy * s2, LANES - ROT // 2, 1)
    g = dxn * w
    mm = _half_stat(g * x) * (1.0 / A_DH)
    return r * g - x * (r * r * r * mm), dxn * x * r


def _attn_mask(first):
    qi = _subl((ABLK, 2 * ABLK))
    kj = _lane((ABLK, 2 * ABLK))
    dist = qi + ABLK - kj
    return (dist >= 0) & (dist < ABLK) & (jnp.logical_not(first) | (kj >= ABLK))


def _attn_fwd(p, tabs, qw, kw, sinks, nseq):
    n = p.shape[0]
    nb = n // nseq // ABLK
    cur = lambda b, i: (b * nb + i, 0)
    prv = lambda b, i: (b * nb + jnp.maximum(i - 1, 0), 0)
    colblk = lambda f, w, off: (lambda b, i: (f(b, i)[0], off // w))

    def body(q_ref, za_ref, kc_ref, vc_ref, kp_ref, vp_ref, c_ref, s1_ref, s2_ref, cp_ref, s1p_ref, s2p_ref,
             qw_ref, kw_ref, sink_ref, y_ref, o_ref, lse_ref):
        first = pl.program_id(1) == 0
        tc = (c_ref[...], s1_ref[...], s2_ref[...])
        tp = (cp_ref[...], s1p_ref[...], s2p_ref[...])
        q, kc, kp = q_ref[...], kc_ref[...], kp_ref[...]
        qn = [_normrope(q[:, LANES * b:LANES * (b + 1)], qw_ref[...], *tc)[0].astype(BF16) for b in range(A_W // LANES)]
        k2, v2 = [], []
        for b in range(A_KVW // LANES):
            sl = slice(LANES * b, LANES * (b + 1))
            k2.append(jnp.concatenate([_normrope(kp[:, sl], kw_ref[...], *tp)[0],
                                       _normrope(kc[:, sl], kw_ref[...], *tc)[0]], axis=0).astype(BF16))
            v2.append(jnp.concatenate([vp_ref[:, sl], vc_ref[:, sl]], axis=0).astype(BF16))
        valid = _attn_mask(first)
        outs = []
        lse = jnp.zeros((ABLK, LANES), F32)
        for g in range(A_KV):
            hs = slice(A_DH * (g % 2), A_DH * (g % 2 + 1))
            kh, vh = k2[g // 2][:, hs], v2[g // 2][:, hs]
            for j in range(3 * g, 3 * g + 3):
                qh = qn[j // 2][:, A_DH * (j % 2):A_DH * (j % 2 + 1)]
                s = jnp.where(valid, _dot_nt(qh, kh) * (A_DH ** -0.5), -jnp.inf)
                sk = sink_ref[j]
                m = jnp.maximum(jnp.max(s, axis=-1, keepdims=True), sk)
                e = jnp.exp(s - m)
                den = jnp.sum(e, axis=-1, keepdims=True) + jnp.exp(sk - m)
                outs.append(_dot((e / den).astype(BF16), vh))
                lse = jnp.where(_lane(lse.shape) == j, m + jnp.log(den), lse)
        o = jnp.concatenate(outs, axis=1)
        za = za_ref[...]
        o_ref[...] = o
        lse_ref[...] = lse
        y_ref[...] = (o * (za * _sigmoid(za))).astype(BF16)

    tab_specs = [BS((ABLK, LANES), cur)] * 3 + [BS((ABLK, LANES), prv)] * 3
    return pl.pallas_call(
        body, name="attn_fwd", grid=(nseq, nb),
        in_specs=[BS((ABLK, A_W), colblk(cur, A_W, P_Q)), BS((ABLK, A_W), colblk(cur, A_W, P_ZA)),
                  BS((ABLK, A_KVW), colblk(cur, A_KVW, P_K)), BS((ABLK, A_KVW), colblk(cur, A_KVW, P_V)),
                  BS((ABLK, A_KVW), colblk(prv, A_KVW, P_K)), BS((ABLK, A_KVW), colblk(prv, A_KVW, P_V))]
        + tab_specs + [BS((1, LANES), lambda b, i: (0, 0))] * 2 + [BS(memory_space=pltpu.SMEM)],
        out_specs=[BS((ABLK, A_W), colblk(cur, A_W, Y_A)), BS((ABLK, A_W), cur), BS((ABLK, LANES), cur)],
        out_shape=[S((n, D_MODEL), BF16), S((n, A_W), F32), S((n, LANES), F32)],
        compiler_params=_cp("arbitrary", "arbitrary"),
    )(p, p, p, p, p, p, *tabs, *tabs, qw, kw, sinks)


def _attn_bwd(p, dy, o, lse, tabs, qw, kw, sinks, nseq):
    n = p.shape[0]
    nb = n // nseq // ABLK
    cur = lambda b, i: (b * nb + jnp.minimum(i, nb - 1), 0)
    prv = lambda b, i: (b * nb + jnp.maximum(i - 1, 0), 0)
    colblk = lambda f, w, off: (lambda b, i: (f(b, i)[0], off // w))

    def body(q_ref, za_ref, kc_ref, vc_ref, kp_ref, vp_ref, dy_ref, o_ref, lse_ref,
             c_ref, s1_ref, s2_ref, cp_ref, s1p_ref, s2p_ref, qw_ref, kw_ref, sink_ref,
             dqza_ref, dkv_ref, dqw_ref, dkw_ref, dsk_ref, tk_ref, tv_ref, ck_ref, cv_ref):
        i = pl.program_id(1)
        first = i == 0
        tc = (c_ref[...], s1_ref[...], s2_ref[...])
        tp = (cp_ref[...], s1p_ref[...], s2p_ref[...])
        nkb = A_KVW // LANES

        @pl.when((pl.program_id(0) == 0) & first)
        def _():
            dqw_ref[...] = jnp.zeros_like(dqw_ref)
            dkw_ref[...] = jnp.zeros_like(dkw_ref)
            dsk_ref[...] = jnp.zeros_like(dsk_ref)

        @pl.when(i < nb)
        def _():
            q, kc, kp = q_ref[...], kc_ref[...], kp_ref[...]
            qn, rq = [], []
            for b in range(A_W // LANES):
                a, r = _normrope(q[:, LANES * b:LANES * (b + 1)], qw_ref[...], *tc)
                qn.append(a.astype(BF16))
                rq.append(r)
            k2, v2 = [], []
            for b in range(nkb):
                sl = slice(LANES * b, LANES * (b + 1))
                k2.append(jnp.concatenate([_normrope(kp[:, sl], kw_ref[...], *tp)[0],
                                           _normrope(kc[:, sl], kw_ref[...], *tc)[0]], axis=0).astype(BF16))
                v2.append(jnp.concatenate([vp_ref[:, sl], vc_ref[:, sl]], axis=0).astype(BF16))
            valid = _attn_mask(first)
            za, dy, o, lse = za_ref[...], dy_ref[...], o_ref[...], lse_ref[...]
            sg = _sigmoid(za)
            do = dy * za * sg
            dqza_ref[:, A_W:2 * A_W] = (dy * o * _dsilu(za, sg)).astype(BF16)
            dqs, dks, dvs = [], [], []
            dsk = jnp.zeros((ABLK, LANES), F32)
            for g in range(A_KV):
                hs = slice(A_DH * (g % 2), A_DH * (g % 2 + 1))
                kh, vh = k2[g // 2][:, hs], v2[g // 2][:, hs]
                dkg = jnp.zeros((2 * ABLK, A_DH), F32)
                dvg = jnp.zeros((2 * ABLK, A_DH), F32)
                for j in range(3 * g, 3 * g + 3):
                    js = slice(A_DH * j, A_DH * (j + 1))
                    qh = qn[j // 2][:, A_DH * (j % 2):A_DH * (j % 2 + 1)]
                    lj = _col(lse, j)
                    s = jnp.where(valid, _dot_nt(qh, kh) * (A_DH ** -0.5), -jnp.inf)
                    pr = jnp.exp(s - lj)
                    doh = do[:, js]
                    delta = jnp.sum(doh * o[:, js], axis=-1, keepdims=True)
                    dsk = dsk + jnp.where(_lane(dsk.shape) == j, -jnp.exp(sink_ref[j] - lj) * delta, 0.0)
                    dohb = doh.astype(BF16)
                    ds = (pr * (_dot_nt(dohb, vh) - delta) * (A_DH ** -0.5)).astype(BF16)
                    dqs.append(_dot(ds, kh))
                    dkg = dkg + _dot_tn(ds, qh)
                    dvg = dvg + _dot_tn(pr.astype(BF16), dohb)
                dks.append(dkg)
                dvs.append(dvg)
            dsk_ref[...] += _fold8(dsk)
            dqn = jnp.concatenate(dqs, axis=1)
            dqw = jnp.zeros((SUB, LANES), F32)
            dqo = []
            for b in range(A_W // LANES):
                sl = slice(LANES * b, LANES * (b + 1))
                dx, wt = _normrope_bwd(dqn[:, sl], q[:, sl], rq[b], qw_ref[...], *tc)
                dqo.append(dx)
                dqw = dqw + _fold8(wt)
            dqw_ref[...] += dqw
            dqza_ref[:, 0:A_W] = jnp.concatenate(dqo, axis=1).astype(BF16)
            tk_ref[...] = jnp.concatenate(dks, axis=1)
            tv_ref[...] = jnp.concatenate(dvs, axis=1)

        @pl.when(i == nb)
        def _():
            tk_ref[...] = jnp.zeros_like(tk_ref)
            tv_ref[...] = jnp.zeros_like(tv_ref)

        @pl.when(i > 0)
        def _():
            kp = kp_ref[...]
            dkn = ck_ref[...] + tk_ref[0:ABLK, :]
            dkw = jnp.zeros((SUB, LANES), F32)
            dko = []
            for b in range(nkb):
                sl = slice(LANES * b, LANES * (b + 1))
                r = _normrope(kp[:, sl], kw_ref[...], *tp)[1]
                dx, wt = _normrope_bwd(dkn[:, sl], kp[:, sl], r, kw_ref[...], *tp)
                dko.append(dx)
                dkw = dkw + _fold8(wt)
            dkw_ref[...] += dkw
            dkv_ref[:, 0:A_KVW] = jnp.concatenate(dko, axis=1).astype(BF16)
            dkv_ref[:, A_KVW:2 * A_KVW] = (cv_ref[...] + tv_ref[0:ABLK, :]).astype(BF16)

        ck_ref[...] = tk_ref[ABLK:2 * ABLK, :]
        cv_ref[...] = tv_ref[ABLK:2 * ABLK, :]

    tab_specs = [BS((ABLK, LANES), cur)] * 3 + [BS((ABLK, LANES), prv)] * 3
    acc = BS((SUB, LANES), lambda b, i: (0, 0))
    return pl.pallas_call(
        body, name="attn_bwd", grid=(nseq, nb + 1),
        in_specs=[BS((ABLK, A_W), colblk(cur, A_W, P_Q)), BS((ABLK, A_W), colblk(cur, A_W, P_ZA)),
                  BS((ABLK, A_KVW), colblk(cur, A_KVW, P_K)), BS((ABLK, A_KVW), colblk(cur, A_KVW, P_V)),
                  BS((ABLK, A_KVW), colblk(prv, A_KVW, P_K)), BS((ABLK, A_KVW), colblk(prv, A_KVW, P_V)),
                  BS((ABLK, A_W), colblk(cur, A_W, Y_A)), BS((ABLK, A_W), cur), BS((ABLK, LANES), cur)]
        + tab_specs + [BS((1, LANES), lambda b, i: (0, 0))] * 2 + [BS(memory_space=pltpu.SMEM)],
        out_specs=[BS((ABLK, 2 * A_W), cur), BS((ABLK, 2 * A_KVW), prv), acc, acc, acc],
        out_shape=[S((n, P_W), BF16), S((n, 2 * A_KVW), BF16)] + [S((SUB, LANES), F32)] * 3,
        scratch_shapes=[pltpu.VMEM((2 * ABLK, A_KVW), F32)] * 2 + [pltpu.VMEM((ABLK, A_KVW), F32)] * 2,
        compiler_params=_cp("arbitrary", "arbitrary"),
    )(p, p, p, p, p, p, dy, o, lse, *tabs, *tabs, qw, kw, sinks)


def _put_cols(dst, src, col_off, tm=512):
    n, w = src.shape

    def body(s_ref, d_in_ref, d_ref):
        d_ref[...] = s_ref[...]

    return pl.pallas_call(
        body, name="put_cols", grid=(n // tm,),
        in_specs=[BS((tm, w), lambda i: (i, 0)), BS(memory_space=pl.ANY)],
        out_specs=BS((tm, w), lambda i: (i, col_off // w)),
        out_shape=S(dst.shape, dst.dtype), input_output_aliases={1: 0},
        compiler_params=_cp("arbitrary"),
    )(src, dst)


HALO_B = 32


def _layernorm(hc, lnw, lnb):
    mu = jnp.mean(hc, axis=-1, keepdims=True)
    xc = hc - mu
    rstd = lax.rsqrt(jnp.mean(xc * xc, axis=-1, keepdims=True) + EPS)
    xhat = xc * rstd
    return xhat, rstd, xhat * lnw + lnb


def _conf_fwd(p, y, cw, cb, lnw, lnb, pw, pwb, nseq, tm=256):
    n = p.shape[0]
    t = n // nseq
    nt = t // tm
    row = lambda b, i: b * nt + i
    halo = lambda b, i: jnp.maximum((b * t + i * tm) // HALO_B - 1, 0)
    vec = BS((1, B_W), lambda b, i: (0, 0))

    def body(ub_ref, uh_ref, zb_ref, cw_ref, cb_ref, lnw_ref, lnb_ref, pw_ref, pwb_ref, y_in_ref, y_ref, hc_ref, buf_ref):
        ub, uh = ub_ref[...], uh_ref[...]
        hh = uh[:, :B_W] * _sigmoid(uh[:, B_W:])
        buf_ref[0:HALO_B, :] = jnp.where(pl.program_id(1) > 0, hh, 0.0)
        buf_ref[HALO_B:, :] = ub[:, :B_W] * _sigmoid(ub[:, B_W:])
        hc = jnp.zeros((tm, B_W), F32) + cb_ref[...]
        for k in range(B_K):
            hc = hc + cw_ref[k:k + 1, :] * buf_ref[pl.ds(HALO_B - B_K + 1 + k, tm), :]
        hc_ref[...] = hc
        ln = _layernorm(hc, lnw_ref[...], lnb_ref[...])[2]
        sw = ln * _sigmoid(ln)
        ob = _dot(sw.astype(BF16), pw_ref[...]) + pwb_ref[...]
        zb = zb_ref[...]
        y_ref[...] = (ob * (zb * _sigmoid(zb))).astype(BF16)

    return pl.pallas_call(
        body, name="conf_fwd", grid=(nseq, nt),
        in_specs=[BS((tm, 2 * B_W), lambda b, i: (row(b, i), P_UB // (2 * B_W))),
                  BS((HALO_B, 2 * B_W), lambda b, i: (halo(b, i), P_UB // (2 * B_W))),
                  BS((tm, B_W), lambda b, i: (row(b, i), P_ZB // B_W)),
                  BS((HALO_B, B_W), lambda b, i: (0, 0)), vec, vec, vec, BS((B_W, B_W), lambda b, i: (0, 0)), vec,
                  BS(memory_space=pl.ANY)],
        out_specs=[BS((tm, B_W), lambda b, i: (row(b, i), Y_B // B_W)), BS((tm, B_W), lambda b, i: (row(b, i), 0))],
        out_shape=[S(y.shape, y.dtype), S((n, B_W), F32)], input_output_aliases={9: 0},
        scratch_shapes=[pltpu.VMEM((HALO_B + tm, B_W), F32)],
        compiler_params=_cp("arbitrary", "arbitrary"),
    )(p, p, p, cw, cb, lnw, lnb, pw, pwb, y)


def _conf_bwd1(p, dy, dp, hc, lnw, lnb, pw, pwb, tm=256):
    n = p.shape[0]
    vec = BS((1, B_W), lambda i: (0, 0))
    acc = BS((SUB, B_W), lambda i: (0, 0))

    def body(dy_ref, zb_ref, hc_ref, lnw_ref, lnb_ref, pw_ref, pwb_ref, dp_in_ref,
             dzb_ref, dhc_ref, dpw_ref, dpwb_ref, dlnw_ref, dlnb_ref, dcb_ref):
        @pl.when(pl.program_id(0) == 0)
        def _():
            for r in (dpw_ref, dpwb_ref, dlnw_ref, dlnb_ref, dcb_ref):
                r[...] = jnp.zeros_like(r)

        xhat, rstd, ln = _layernorm(hc_ref[...], lnw_ref[...], lnb_ref[...])
        sgl = _sigmoid(ln)
        sw = (ln * sgl).astype(BF16)
        ob = _dot(sw, pw_ref[...]) + pwb_ref[...]
        dy, zb = dy_ref[...], zb_ref[...]
        sgz = _sigmoid(zb)
        dzb_ref[...] = (dy * ob * _dsilu(zb, sgz)).astype(BF16)
        dob = dy * zb * sgz
        dobb = dob.astype(BF16)
        dpwb_ref[...] += _fold8(dob)
        dpw_ref[...] += _dot_tn(sw, dobb)
        dln = _dot_nt(dobb, pw_ref[...]) * _dsilu(ln, sgl)
        dlnw_ref[...] += _fold8(dln * xhat)
        dlnb_ref[...] += _fold8(dln)
        dxh = dln * lnw_ref[...]
        dhc = rstd * (dxh - jnp.mean(dxh, axis=-1, keepdims=True) - xhat * jnp.mean(dxh * xhat, axis=-1, keepdims=True))
        dcb_ref[...] += _fold8(dhc)
        dhc_ref[...] = dhc

    return pl.pallas_call(
        body, name="conf_bwd1", grid=(n // tm,),
        in_specs=[BS((tm, B_W), lambda i: (i, Y_B // B_W)), BS((tm, B_W), lambda i: (i, P_ZB // B_W)),
                  BS((tm, B_W), lambda i: (i, 0)), vec, vec, BS((B_W, B_W), lambda i: (0, 0)), vec,
                  BS(memory_space=pl.ANY)],
        out_specs=[BS((tm, B_W), lambda i: (i, P_ZB // B_W)), BS((tm, B_W), lambda i: (i, 0)),
                   BS((B_W, B_W), lambda i: (0, 0)), acc, acc, acc, acc],
        out_shape=[S(dp.shape, dp.dtype), S((n, B_W), F32), S((B_W, B_W), F32)] + [S((SUB, B_W), F32)] * 4,
        input_output_aliases={7: 0},
        compiler_params=_cp("arbitrary"),
    )(dy, p, hc, lnw, lnb, pw, pwb, dp)


def _conf_bwd2(p, dhc, dp, cw, nseq, tm=256):
    n = p.shape[0]
    t = n // nseq
    nt = t // tm
    row = lambda b, i: b * nt + i
    prev = lambda b, i: jnp.maximum((b * t + i * tm) // HALO_B - 1, 0)
    nxt = lambda b, i: jnp.minimum((b * t + (i + 1) * tm) // HALO_B, n // HALO_B - 1)

    def body(ub_ref, uh_ref, dh_ref, dn_ref, cw_ref, dp_in_ref, dub_ref, dcw_ref, buf_ref, dbuf_ref):
        i = pl.program_id(1)

        @pl.when((pl.program_id(0) == 0) & (i == 0))
        def _():
            dcw_ref[...] = jnp.zeros_like(dcw_ref)

        ub, uh = ub_ref[...], uh_ref[...]
        a, sg = ub[:, :B_W], _sigmoid(ub[:, B_W:])
        buf_ref[0:HALO_B, :] = jnp.where(i > 0, uh[:, :B_W] * _sigmoid(uh[:, B_W:]), 0.0)
        buf_ref[HALO_B:, :] = a * sg
        dhc = dh_ref[...]
        dbuf_ref[0:tm, :] = dhc
        dbuf_ref[tm:, :] = jnp.where(i < nt - 1, dn_ref[...], 0.0)
        dhg = jnp.zeros((tm, B_W), F32)
        for k in range(B_K):
            dhg = dhg + cw_ref[k:k + 1, :] * dbuf_ref[pl.ds(B_K - 1 - k, tm), :]
            dcw_ref[SUB * k:SUB * (k + 1), :] += _fold8(dhc * buf_ref[pl.ds(HALO_B - B_K + 1 + k, tm), :])
        dub_ref[...] = jnp.concatenate([dhg * sg, dhg * a * sg * (1.0 - sg)], axis=1).astype(BF16)

    return pl.pallas_call(
        body, name="conf_bwd2", grid=(nseq, nt),
        in_specs=[BS((tm, 2 * B_W), lambda b, i: (row(b, i), P_UB // (2 * B_W))),
                  BS((HALO_B, 2 * B_W), lambda b, i: (prev(b, i), P_UB // (2 * B_W))),
                  BS((tm, B_W), lambda b, i: (row(b, i), 0)), BS((HALO_B, B_W), lambda b, i: (nxt(b, i), 0)),
                  BS((HALO_B, B_W), lambda b, i: (0, 0)), BS(memory_space=pl.ANY)],
        out_specs=[BS((tm, 2 * B_W), lambda b, i: (row(b, i), P_UB // (2 * B_W))),
                   BS((SUB * B_K, B_W), lambda b, i: (0, 0))],
        out_shape=[S(dp.shape, dp.dtype), S((SUB * B_K, B_W), F32)], input_output_aliases={5: 0},
        scratch_shapes=[pltpu.VMEM((HALO_B + tm, B_W), F32)] * 2,
        compiler_params=_cp("arbitrary", "arbitrary"),
    )(p, p, dhc, dhc, cw, dp)


HALO_C = 8
QS = C_DH ** -0.5
NCB = 3 * C_HEADS
CB0 = P_QKV // LANES
ZC0 = P_ZC // LANES
GB, GG = 0, C_HEADS


def _softplus(z):
    return jnp.maximum(z, 0.0) + jnp.log(1.0 + jnp.exp(-jnp.abs(z)))


def _gdn_gates_fwd(p, alog_l, dtb_l, tm=256):
    n = p.shape[0]

    def body(ba_ref, al_ref, db_ref, o_ref):
        blk = ba_ref[...]
        lane = _lane(blk.shape)
        g = jnp.where((lane >= GG) & (lane < GG + C_HEADS), -jnp.exp(al_ref[...]) * _softplus(blk + db_ref[...]), 0.0)
        tri = (_subl((CHUNK, CHUNK)) >= _lane((CHUNK, CHUNK))).astype(F32)
        gc = jnp.concatenate([_dot(tri, g[CHUNK * c:CHUNK * (c + 1)], HI) for c in range(tm // CHUNK)], axis=0)
        o_ref[...] = jnp.where(lane < GG, _sigmoid(blk), gc)

    return pl.pallas_call(
        body, name="gdn_gates_fwd", grid=(n // tm,),
        in_specs=[BS((tm, LANES), lambda i: (i, P_BA // LANES)), BS((1, LANES), lambda i: (0, 0)), BS((1, LANES), lambda i: (0, 0))],
        out_specs=BS((tm, LANES), lambda i: (i, 0)), out_shape=S((n, LANES), F32),
        compiler_params=_cp("arbitrary"),
    )(p, alog_l, dtb_l)


def _gdn_pre_fwd(p, ccw, nseq, tm=256):
    n = p.shape[0]
    t = n // nseq
    nt = t // tm
    row = lambda b, i: b * nt + i
    halo = lambda b, i: jnp.maximum((b * t + i * tm) // HALO_C - 1, 0)

    def body(x_ref, xh_ref, w_ref, xc_ref, o_ref, buf_ref):
        buf_ref[0:HALO_C, :] = jnp.where(pl.program_id(1) > 0, xh_ref[...], 0.0)
        buf_ref[HALO_C:, :] = x_ref[...]
        for c in range(NCB):
            cs = slice(LANES * c, LANES * (c + 1))
            xc = jnp.zeros((tm, LANES), F32)
            for k in range(C_K):
                xc = xc + w_ref[k:k + 1, cs] * buf_ref[pl.ds(HALO_C - C_K + 1 + k, tm), cs]
            xc_ref[:, cs] = xc
            act = xc * _sigmoid(xc)
            if c < 2 * C_HEADS:
                act = act * (lax.rsqrt(jnp.sum(act * act, axis=-1, keepdims=True) + EPS) * (QS if c < C_HEADS else 1.0))
            o_ref[:, cs] = act

    wide = 3 * C_W
    return pl.pallas_call(
        body, name="gdn_pre_fwd", grid=(nseq, nt),
        in_specs=[BS((tm, wide), lambda b, i: (row(b, i), P_QKV // wide)), BS((HALO_C, wide), lambda b, i: (halo(b, i), P_QKV // wide)),
                  BS((SUB, wide), lambda b, i: (0, 0))],
        out_specs=[BS((tm, wide), lambda b, i: (row(b, i), 0))] * 2,
        out_shape=[S((n, wide), F32)] * 2,
        scratch_shapes=[pltpu.VMEM((HALO_C + tm, wide), F32)],
        compiler_params=_cp("arbitrary", "arbitrary"),
    )(p, p, ccw)


def _chunk_common(q, k, gt, gtt, h):
    beta = _col(gt, GB + h)
    gc = _col(gt, GG + h)
    gcr = gtt[GG + h:GG + h + 1, :]
    ii, jj = _subl((CHUNK, CHUNK)), _lane((CHUNK, CHUNK))
    incl, strict = ii >= jj, ii > jj
    dec = jnp.exp(jnp.where(incl, gc - gcr, -jnp.inf))
    kb = k * beta
    kbf = k.astype(BF16)
    a = jnp.where(strict, _dot_nt(kb.astype(BF16), kbf) * dec, 0.0)
    mq = jnp.where(incl, _dot_nt(q.astype(BF16), kbf) * dec, 0.0)
    glast = jnp.sum(jnp.where(_subl(gc.shape) == CHUNK - 1, gc, 0.0), axis=0, keepdims=True)
    return beta, gc, incl, strict, dec, kb, a, mq, glast


def _unit_lower_inverses(mats):
    eye = (_subl(mats[0].shape) == _lane(mats[0].shape)).astype(F32)
    ms = [-a for a in mats]
    invs = [eye + m for m in ms]
    for _ in range(5):
        ms = [_dot(m, m, HI) for m in ms]
        invs = [inv + _dot(inv, m, HI) for inv, m in zip(invs, ms)]
    return invs


def _gdn_chunk_fwd(qkv, gates, p, y, onw, nseq, tt=512):
    n = qkv.shape[0]
    t = n // nseq
    tt = min(tt, t)
    nt = t // tt
    nch = tt // CHUNK

    def body(q_ref, k_ref, v_ref, g_ref, zc_ref, onw_ref, y_in_ref, y_ref, o_ref, u_ref, w_ref, t_ref, ss_ref, s_scr):
        @pl.when(pl.program_id(1) == 0)
        def _():
            s_scr[...] = jnp.zeros_like(s_scr)

        def step(c, carry):
            rows = pl.ds(pl.multiple_of(c * CHUNK, CHUNK), CHUNK)
            gt = g_ref[rows, :]
            gtt = gt.T
            heads = range(C_HEADS)
            hs = [slice(C_DH * h, C_DH * (h + 1)) for h in heads]
            q, k, v = ([r[rows, hs[h]] for h in heads] for r in (q_ref, k_ref, v_ref))
            cm = [_chunk_common(q[h], k[h], gt, gtt, h) for h in heads]
            beta, gc, kb, mq, glast = ([m[i] for m in cm] for i in (0, 1, 5, 7, 8))
            tinv = _unit_lower_inverses([m[6] for m in cm])
            egc = [jnp.exp(g) for g in gc]
            sol = [_dot(tinv[h], jnp.concatenate([v[h] * beta[h], kb[h] * egc[h]], axis=1), HI) for h in heads]
            sv = [s_scr[h] for h in heads]
            sb = [s.astype(BF16) for s in sv]
            vnb = [(sol[h][:, :C_DH] - _dot(sol[h][:, C_DH:].astype(BF16), sb[h])).astype(BF16) for h in heads]
            o = [_dot((q[h] * egc[h]).astype(BF16), sb[h]) + _dot(mq[h].astype(BF16), vnb[h]) for h in heads]
            for h in heads:
                ss_ref[h, c] = sv[h]
                s_scr[h] = sv[h] * jnp.exp(glast[h]) + _dot_tn((k[h] * jnp.exp(glast[h] - gc[h])).astype(BF16), vnb[h])
            for h in heads:
                o_ref[rows, hs[h]] = o[h]
                u_ref[rows, hs[h]] = sol[h][:, :C_DH]
                w_ref[rows, hs[h]] = sol[h][:, C_DH:]
                t_ref[rows, hs[h]] = jnp.concatenate([tinv[h], jnp.zeros_like(tinv[h])], axis=1)
                zc = zc_ref[rows, hs[h]]
                r = lax.rsqrt(jnp.mean(o[h] * o[h], axis=-1, keepdims=True) + EPS)
                y_ref[rows, hs[h]] = (o[h] * r * onw_ref[...] * (zc * _sigmoid(zc))).astype(BF16)
            return carry

        lax.fori_loop(0, nch, step, 0)

    row = lambda b, i: b * nt + i
    wb = lambda col: BS((tt, C_W), lambda b, i: (row(b, i), col))
    return pl.pallas_call(
        body, name="gdn_chunk_fwd", grid=(nseq, nt),
        in_specs=[wb(0), wb(1), wb(2), BS((tt, LANES), lambda b, i: (row(b, i), 0)), wb(P_ZC // C_W),
                  BS((1, LANES), lambda b, i: (0, 0)), BS(memory_space=pl.ANY)],
        out_specs=[wb(Y_C // C_W), wb(0), wb(0), wb(0), wb(0),
                   BS((None, C_HEADS, nch, C_DH, C_DH), lambda b, i: (b, 0, i, 0, 0))],
        out_shape=[S(y.shape, y.dtype)] + [S((n, C_W), F32)] * 4 + [S((nseq, C_HEADS, t // CHUNK, C_DH, C_DH), F32)],
        input_output_aliases={6: 0},
        scratch_shapes=[pltpu.VMEM((C_HEADS, C_DH, C_DH), F32)],
        compiler_params=_cp("arbitrary", "arbitrary"),
    )(qkv, qkv, qkv, gates, p, onw, y)


def _gdn_chunk_bwd(qkv, gates, p, dy, dp, onw, o, u, w, tinv, ss, nseq, tt=256):
    n = qkv.shape[0]
    t = n // nseq
    tt = min(tt, t)
    nt = t // tt
    nch = tt // CHUNK

    def body(q_ref, k_ref, v_ref, g_ref, zc_ref, onw_ref, o_ref, dy_ref, u_ref, w_ref, t_ref, ss_ref, dp_in_ref,
             dzc_ref, dqkv_ref, dg_ref, donw_ref, ds_scr):
        @pl.when(pl.program_id(1) == 0)
        def _():
            ds_scr[...] = jnp.zeros_like(ds_scr)

        @pl.when((pl.program_id(0) == 0) & (pl.program_id(1) == 0))
        def _():
            donw_ref[...] = jnp.zeros_like(donw_ref)

        def rsum(x):
            return jnp.sum(x, axis=-1, keepdims=True)

        def step(ci, carry):
            c = nch - 1 - ci
            rows = pl.ds(pl.multiple_of(c * CHUNK, CHUNK), CHUNK)
            gt = g_ref[rows, :]
            gtt = gt.T
            live = [head(c, rows, gt, gtt, h) for h in range(C_HEADS)]
            while live:
                live = [g for g in live if next(g, False)]
            return carry

        def head(c, rows, gt, gtt, h):
            hs = slice(C_DH * h, C_DH * (h + 1))
            q, k, v = q_ref[rows, hs], k_ref[rows, hs], v_ref[rows, hs]
            zc, o, dy, u, w = zc_ref[rows, hs], o_ref[rows, hs], dy_ref[rows, hs], u_ref[rows, hs], w_ref[rows, hs]
            tm_ = t_ref[rows, hs][:, 0:CHUNK]
            sv, dsv = ss_ref[h, c], ds_scr[h]
            sb, dsb = sv.astype(BF16), dsv.astype(BF16)
            sg = _sigmoid(zc)
            r = lax.rsqrt(jnp.mean(o * o, axis=-1, keepdims=True) + EPS)
            on = o * r
            ow = onw_ref[...]
            dzc_ref[rows, hs] = (dy * on * ow * _dsilu(zc, sg)).astype(BF16)
            t1 = dy * zc * sg
            donw_ref[...] += _fold8(t1 * on)
            don = t1 * ow
            do = r * (don - on * jnp.mean(don * on, axis=-1, keepdims=True))
            dob = do.astype(BF16)
            yield True
            beta, gc, incl, strict, dec, kb, a, mq, glast = _chunk_common(q, k, gt, gtt, h)
            egc = jnp.exp(gc)
            gl = jnp.exp(glast)
            ekd = jnp.exp(glast - gc)
            wb = w.astype(BF16)
            vnb = (u - _dot(wb, sb)).astype(BF16)
            qg = q * egc
            yield True
            dvn = _dot_tn(mq.astype(BF16), dob) + _dot((k * ekd).astype(BF16), dsb)
            dvnb = dvn.astype(BF16)
            dqg = _dot_nt(dob, sb)
            yield True
            dmq = jnp.where(incl, _dot_nt(dob, vnb), 0.0)
            dkd = _dot_nt(vnb, dsb)
            dgl = jnp.sum(rsum(dsv * sv), axis=0, keepdims=True)
            dw = -_dot_nt(dvnb, sb)
            yield True
            ds_scr[h] = gl * dsv + _dot_tn(qg.astype(BF16), dob) - _dot_tn(wb, dvnb)
            db = _dot_tn(tm_, jnp.concatenate([dvn, dw], axis=1), HI)
            dbv, dbk = db[:, :C_DH], db[:, C_DH:]
            yield True
            da = -jnp.where(strict, _dot_nt(dbv, u, HI) + _dot_nt(dbk, w, HI), 0.0)
            yield True
            e = da * a + dmq * mq
            dgc = rsum(e) - rsum(e.T)
            dgb, dhb, kbf = (da * dec).astype(BF16), (dmq * dec).astype(BF16), k.astype(BF16)
            dkb = _dot(dgb, kbf)
            tk = rsum(dbk * k)
            rk = rsum(dkd * k) * ekd
            dq = _dot(dhb, kbf) + egc * dqg
            dk = _dot_tn(dgb, kb.astype(BF16)) + _dot_tn(dhb, q.astype(BF16)) + beta * (egc * dbk + dkb) + ekd * dkd
            dbeta = rsum(dbv * v) + tk * egc + rsum(dkb * k)
            dgc = dgc + tk * beta * egc + egc * rsum(dqg * q) - rk
            dglast = jnp.sum(rk, axis=0, keepdims=True) + dgl * gl
            dgc = dgc + jnp.where(_subl(dgc.shape) == CHUNK - 1, dglast, 0.0)
            dqkv_ref[0, rows, hs] = dq
            dqkv_ref[1, rows, hs] = dk
            dqkv_ref[2, rows, hs] = beta * dbv
            lane = _lane((CHUNK, LANES))
            dg_ref[h, rows, :] = jnp.where(lane == 0, dbeta, jnp.where(lane == 1, dgc, 0.0))

        lax.fori_loop(0, nch, step, 0)

    row = lambda b, i: b * nt + nt - 1 - i
    wb = lambda col: BS((tt, C_W), lambda b, i: (row(b, i), col))
    return pl.pallas_call(
        body, name="gdn_chunk_bwd", grid=(nseq, nt),
        in_specs=[wb(0), wb(1), wb(2), BS((tt, LANES), lambda b, i: (row(b, i), 0)), wb(P_ZC // C_W),
                  BS((1, LANES), lambda b, i: (0, 0)), wb(0), wb(Y_C // C_W), wb(0), wb(0), wb(0),
                  BS((None, C_HEADS, nch, C_DH, C_DH), lambda b, i: (b, 0, nt - 1 - i, 0, 0)), BS(memory_space=pl.ANY)],
        out_specs=[wb(P_ZC // C_W), BS((3, tt, C_W), lambda b, i: (0, row(b, i), 0)),
                   BS((C_HEADS, tt, LANES), lambda b, i: (0, row(b, i), 0)), BS((SUB, LANES), lambda b, i: (0, 0))],
        out_shape=[S(dp.shape, dp.dtype), S((3, n, C_W), F32), S((C_HEADS, n, LANES), F32), S((SUB, LANES), F32)],
        input_output_aliases={12: 0},
        scratch_shapes=[pltpu.VMEM((C_HEADS, C_DH, C_DH), F32)],
        compiler_params=_cp("arbitrary", "arbitrary"),
    )(qkv, qkv, qkv, gates, p, onw, o, dy, u, w, tinv, ss, dp)


def _gdn_gates_bwd(dgate, p, alog_l, dtb_l, dp, tm=256):
    n = p.shape[0]
    acc = BS((SUB, LANES), lambda i: (0, 0))

    def body(dg_ref, ba_ref, al_ref, db_ref, dp_in_ref, dba_ref, dal_ref, ddb_ref):
        @pl.when(pl.program_id(0) == 0)
        def _():
            dal_ref[...] = jnp.zeros_like(dal_ref)
            ddb_ref[...] = jnp.zeros_like(ddb_ref)

        blk = ba_ref[...]
        lane = _lane(blk.shape)
        dbeta = jnp.zeros_like(blk)
        dgc = jnp.zeros_like(blk)
        for h in range(C_HEADS):
            dbeta = dbeta + jnp.where(lane == GB + h, _col(dg_ref[h], 0), 0.0)
            dgc = dgc + jnp.where(lane == GG + h, _col(dg_ref[h], 1), 0.0)
        tri = (_subl((CHUNK, CHUNK)) <= _lane((CHUNK, CHUNK))).astype(F32)
        dg = jnp.concatenate([_dot(tri, dgc[CHUNK * c:CHUNK * (c + 1)], HI) for c in range(tm // CHUNK)], axis=0)
        beta = _sigmoid(blk)
        z = blk + db_ref[...]
        ea = jnp.exp(al_ref[...])
        isg = (lane >= GG) & (lane < GG + C_HEADS)
        dz = jnp.where(isg, -dg * ea * _sigmoid(z), 0.0)
        dal_ref[...] += _fold8(jnp.where(isg, -dg * ea * _softplus(z), 0.0))
        ddb_ref[...] += _fold8(dz)
        out = jnp.where(lane < GG, dbeta * beta * (1.0 - beta), dz)
        dba_ref[...] = jnp.concatenate([out, jnp.zeros_like(out)], axis=1).astype(BF16)

    return pl.pallas_call(
        body, name="gdn_gates_bwd", grid=(n // tm,),
        in_specs=[BS((C_HEADS, tm, LANES), lambda i: (0, i, 0)), BS((tm, LANES), lambda i: (i, P_BA // LANES)),
                  BS((1, LANES), lambda i: (0, 0)), BS((1, LANES), lambda i: (0, 0)), BS(memory_space=pl.ANY)],
        out_specs=[BS((tm, 2 * LANES), lambda i: (i, P_BA // (2 * LANES))), acc, acc],
        out_shape=[S(dp.shape, dp.dtype), S((SUB, LANES), F32), S((SUB, LANES), F32)],
        input_output_aliases={4: 0},
        compiler_params=_cp("arbitrary"),
    )(dgate, p, alog_l, dtb_l, dp)


def _gdn_pre_bwd(p, dqkv, xc, dp, ccw, nseq, tm=256):
    n = p.shape[0]
    t = n // nseq
    nt = t // tm
    wide = 3 * C_W
    row = lambda b, i: b * nt + i
    prev = lambda b, i: jnp.maximum((b * t + i * tm) // HALO_C - 1, 0)
    nxt = lambda b, i: jnp.minimum((b * t + (i + 1) * tm) // HALO_C, n // HALO_C - 1)

    def d_conv_out(d, xc, part):
        sg = _sigmoid(xc)
        act = xc * sg
        if part < 2:
            cs = QS if part == 0 else 1.0
            rn = lax.rsqrt(jnp.sum(act * act, axis=-1, keepdims=True) + EPS)
            d = cs * rn * d - act * (cs * rn * rn * rn * jnp.sum(d * act, axis=-1, keepdims=True))
        return d * _dsilu(xc, sg)

    def body(x_ref, xh_ref, d_ref, dn_ref, xc_ref, xn_ref, w_ref, dp_in_ref, dx_ref, dw_ref, buf_ref, dbuf_ref):
        i = pl.program_id(1)

        @pl.when((pl.program_id(0) == 0) & (i == 0))
        def _():
            dw_ref[...] = jnp.zeros_like(dw_ref)

        buf_ref[0:HALO_C, :] = jnp.where(i > 0, xh_ref[...], 0.0)
        buf_ref[HALO_C:, :] = x_ref[...]
        for c in range(NCB):
            cs = slice(LANES * c, LANES * (c + 1))
            part, hd = divmod(c, C_HEADS)
            hs = slice(LANES * hd, LANES * (hd + 1))
            d = d_conv_out(d_ref[part, :, hs], xc_ref[:, cs], part)
            dbuf_ref[0:tm, cs] = d
            dbuf_ref[tm:, cs] = jnp.where(i < nt - 1, d_conv_out(dn_ref[part, :, hs], xn_ref[:, cs], part), 0.0)
            dx = jnp.zeros((tm, LANES), F32)
            for k in range(C_K):
                dx = dx + w_ref[k:k + 1, cs] * dbuf_ref[pl.ds(C_K - 1 - k, tm), cs]
                dw_ref[SUB * k:SUB * (k + 1), cs] += _fold8(d * buf_ref[pl.ds(HALO_C - C_K + 1 + k, tm), cs])
            dx_ref[:, cs] = dx.astype(BF16)

    return pl.pallas_call(
        body, name="gdn_pre_bwd", grid=(nseq, nt),
        in_specs=[BS((tm, wide), lambda b, i: (row(b, i), P_QKV // wide)), BS((HALO_C, wide), lambda b, i: (prev(b, i), P_QKV // wide)),
                  BS((3, tm, C_W), lambda b, i: (0, row(b, i), 0)), BS((3, HALO_C, C_W), lambda b, i: (0, nxt(b, i), 0)),
                  BS((tm, wide), lambda b, i: (row(b, i), 0)), BS((HALO_C, wide), lambda b, i: (nxt(b, i), 0)),
                  BS((SUB, wide), lambda b, i: (0, 0)), BS(memory_space=pl.ANY)],
        out_specs=[BS((tm, wide), lambda b, i: (row(b, i), P_QKV // wide)), BS((SUB * C_K, wide), lambda b, i: (0, 0))],
        out_shape=[S(dp.shape, dp.dtype), S((SUB * C_K, wide), F32)], input_output_aliases={7: 0},
        scratch_shapes=[pltpu.VMEM((HALO_C + tm, wide), F32)] * 2,
        compiler_params=_cp("arbitrary", "arbitrary"),
    )(p, p, dqkv, dqkv, xc, xc, ccw, dp)


ANY = BS(memory_space=pl.ANY)


def _my_pos():
    return lax.axis_index("x"), lax.axis_index("y"), lax.axis_index("c")


def _dev_index(dev):
    return 4 * dev[0] + 2 * dev[1] + dev[2]


def _all_gather(shards):
    nk = len(shards)

    def body(*refs):
        ins, outs = refs[:nk], refs[nk:2 * nk]
        send, recv, loc = refs[2 * nk:]
        x, y, c = _my_pos()
        me, sib = (x, y, c), (x, y, 1 - c)
        chips = [(1 - x, y), (x, 1 - y), (1 - x, 1 - y)]

        def rows(t, dev):
            r = ins[t].shape[0]
            return outs[t].at[pl.ds(pl.multiple_of(_dev_index(dev) * r, SUB), r), :]

        def copy(t, k, block, to, src=None):
            return pltpu.make_async_remote_copy(
                src_ref=rows(t, block) if src is None else src, dst_ref=rows(t, block),
                send_sem=send.at[t, k], recv_sem=recv.at[t, k], device_id=to, device_id_type=MESH)

        mine = [pltpu.make_async_copy(ins[t], rows(t, me), loc.at[t]) for t in range(nk)]
        for cp in mine:
            cp.start()
        first = []
        for t in range(nk):
            first.append(copy(t, 0, me, sib, src=ins[t]))
            first += [copy(t, 1 + j, me, (*chip, c), src=ins[t]) for j, chip in enumerate(chips)]
        for cp in first:
            cp.start()
        passed = []
        for j, chip in enumerate(chips):
            for t in range(nk):
                copy(t, 1 + j, (*chip, c), me).wait_recv()
                cp = copy(t, 4 + j, (*chip, c), sib)
                cp.start()
                passed.append(cp)
        for t in range(nk):
            copy(t, 0, sib, me).wait_recv()
            for j, chip in enumerate(chips):
                copy(t, 4 + j, (*chip, 1 - c), me).wait_recv()
        for cp in first + passed:
            cp.wait_send()
        for cp in mine:
            cp.wait()

    return pl.pallas_call(
        body, name="all_gather", in_specs=[ANY] * nk, out_specs=[ANY] * nk,
        out_shape=[S((N_DEV * a.shape[0], a.shape[1]), a.dtype) for a in shards],
        scratch_shapes=[pltpu.SemaphoreType.DMA((nk, 7)), pltpu.SemaphoreType.DMA((nk, 7)), pltpu.SemaphoreType.DMA((nk,))],
    )(*shards)


def _scatter_blocks(parts):
    nk = len(parts)

    def body(*refs):
        ins, outs = refs[:nk], refs[nk:2 * nk]
        send, recv, loc = refs[2 * nk:]
        x, y, c = _my_pos()
        me = _dev_index((x, y, c))
        peers = [((1 - x) if k & 4 else x, (1 - y) if k & 2 else y, (1 - c) if k & 1 else c) for k in range(1, N_DEV)]

        def block(t, dev):
            r = ins[t].shape[0] // N_DEV
            return ins[t].at[pl.ds(pl.multiple_of(_dev_index(dev) * r, SUB), r), :]

        mine = [pltpu.make_async_copy(block(t, (x, y, c)), outs[t].at[me], loc.at[t]) for t in range(nk)]
        for cp in mine:
            cp.start()
        sent = []
        for t in range(nk):
            for k, peer in enumerate(peers):
                cp = pltpu.make_async_remote_copy(src_ref=block(t, peer), dst_ref=outs[t].at[me], send_sem=send.at[t, k],
                                                  recv_sem=recv.at[t, k], device_id=peer, device_id_type=MESH)
                cp.start()
                sent.append(cp)
        for t in range(nk):
            for k, peer in enumerate(peers):
                pltpu.make_async_remote_copy(src_ref=block(t, peer), dst_ref=outs[t].at[_dev_index(peer)], send_sem=send.at[t, k],
                                             recv_sem=recv.at[t, k], device_id=peer, device_id_type=MESH).wait_recv()
        for cp in sent:
            cp.wait_send()
        for cp in mine:
            cp.wait()

    return pl.pallas_call(
        body, name="scatter_blocks", in_specs=[ANY] * nk, out_specs=[ANY] * nk,
        out_shape=[S((N_DEV, a.shape[0] // N_DEV, a.shape[1]), a.dtype) for a in parts],
        scratch_shapes=[pltpu.SemaphoreType.DMA((nk, 7)), pltpu.SemaphoreType.DMA((nk, 7)), pltpu.SemaphoreType.DMA((nk,))],
    )(*parts)


SEM = BS(memory_space=pltpu.SEMAPHORE)
HBM = BS(memory_space=pltpu.HBM)
EFFECT = pltpu.SideEffectType.DATAFLOW_SIDE_EFFECTING


def _peers(x, y, c):
    return [((1 - x) if k & 4 else x, (1 - y) if k & 2 else y, (1 - c) if k & 1 else c) for k in range(1, N_DEV)]


def _exchange_copy(kind, src, land, send, recv, t, k, peer, me, arriving):
    frm = peer if arriving else me
    if kind == "gather":
        r = src.shape[0]
        s_ref = src
        d_ref = land.at[pl.ds(pl.multiple_of(_dev_index(frm) * r, SUB), r), :]
    else:
        r = src.shape[0] // N_DEV
        s_ref = src.at[pl.ds(pl.multiple_of(_dev_index(peer) * r, SUB), r), :]
        d_ref = land.at[_dev_index(frm)]
    sem = t * (N_DEV - 1) + k
    return pltpu.make_async_remote_copy(src_ref=s_ref, dst_ref=d_ref, send_sem=send.at[sem], recv_sem=recv.at[sem],
                                        device_id=peer, device_id_type=MESH)


def _exchange_start(kind, srcs, after, name):
    nk = len(srcs)
    if kind == "gather":
        lands = [lax.empty((N_DEV * a.shape[0], a.shape[1]), a.dtype) for a in srcs]
    else:
        lands = [lax.empty((N_DEV, a.shape[0] // N_DEV, a.shape[1]), a.dtype) for a in srcs]

    def body(*refs):
        src, land = refs[:nk], refs[nk:2 * nk]
        send, recv = refs[2 * nk + 1], refs[2 * nk + 2]
        token, loc = refs[-2], refs[-1]
        x, y, c = _my_pos()
        me = (x, y, c)
        for t in range(nk):
            for k, peer in enumerate(_peers(x, y, c)):
                _exchange_copy(kind, src[t], land[t], send, recv, t, k, peer, me, False).start()
        for t in range(nk):
            if kind == "gather":
                r = src[t].shape[0]
                cp = pltpu.make_async_copy(src[t], land[t].at[pl.ds(pl.multiple_of(_dev_index(me) * r, SUB), r), :], loc)
            else:
                r = src[t].shape[0] // N_DEV
                cp = pltpu.make_async_copy(src[t].at[pl.ds(pl.multiple_of(_dev_index(me) * r, SUB), r), :],
                                           land[t].at[_dev_index(me)], loc)
            cp.start()
            cp.wait()
        token[...] = jnp.zeros_like(token)

    hbm = lambda a: pltpu.HBM(a.shape, a.dtype)
    out = pl.pallas_call(
        body, name=name,
        out_shape=(pltpu.SemaphoreType.DMA((nk * (N_DEV - 1),)), pltpu.SemaphoreType.DMA((nk * (N_DEV - 1),)),
                   *[hbm(a) for a in srcs], *[hbm(a) for a in lands], S((SUB, LANES), F32)),
        in_specs=[HBM] * (2 * nk) + [ANY],
        out_specs=(SEM, SEM, *[HBM] * (2 * nk), BS(memory_space=pltpu.VMEM)),
        input_output_aliases={i: 2 + i for i in range(2 * nk)},
        scratch_shapes=[pltpu.SemaphoreType.DMA],
        compiler_params=pltpu.CompilerParams(has_side_effects=EFFECT),
    )(*[pltpu.with_memory_space_constraint(a, pltpu.HBM) for a in (*srcs, *lands)], after)
    return dict(kind=kind, nk=nk, send=out[0], recv=out[1], srcs=out[2:2 + nk], lands=out[2 + nk:2 + 2 * nk], token=out[-1])


def _exchange_wait(ex, after, name):
    kind, nk = ex["kind"], ex["nk"]

    def body(*refs):
        src, land = refs[:nk], refs[nk:2 * nk]
        send, recv = refs[2 * nk], refs[2 * nk + 1]
        x, y, c = _my_pos()
        me = (x, y, c)
        for t in range(nk):
            for k, peer in enumerate(_peers(x, y, c)):
                _exchange_copy(kind, src[t], land[t], send, recv, t, k, peer, me, False).wait_send()
                _exchange_copy(kind, src[t], land[t], send, recv, t, k, peer, me, True).wait_recv()

    hbm = lambda a: pltpu.HBM(a.shape, a.dtype)
    out = pl.pallas_call(
        body, name=name,
        out_shape=(*[hbm(a) for a in ex["srcs"]], *[hbm(a) for a in ex["lands"]]),
        in_specs=[HBM] * (2 * nk) + [SEM, SEM, ANY], out_specs=tuple([HBM] * (2 * nk)),
        input_output_aliases={i: i for i in range(2 * nk)},
        compiler_params=pltpu.CompilerParams(has_side_effects=EFFECT),
    )(*ex["srcs"], *ex["lands"], ex["send"], ex["recv"], after)
    return list(out[nk:])


BLOCK_BYTES = 4 << 20


def _row_tile(rows, row_bytes, align):
    best = align
    for tr in range(align, rows + 1, align):
        if rows % tr == 0 and tr * row_bytes <= BLOCK_BYTES:
            best = tr
    return best


def _sum8(a):
    _, r, w = a.shape
    tr = _row_tile(r, N_DEV * w * a.dtype.itemsize, 32 // a.dtype.itemsize)

    def body(a_ref, o_ref):
        acc = a_ref[0].astype(F32)
        for d in range(1, N_DEV):
            acc = acc + a_ref[d].astype(F32)
        o_ref[...] = acc

    return pl.pallas_call(
        body, name="sum8", grid=(r // tr,), in_specs=[BS((N_DEV, tr, w), lambda i: (0, i, 0))],
        out_specs=BS((tr, w), lambda i: (i, 0)), out_shape=S((r, w), F32), compiler_params=_cp("arbitrary"),
    )(a)


def _adamw(w, g, m, v):
    r, c = w.shape
    tr = _row_tile(r, c * 4 * 2, SUB)

    def body(w_ref, g_ref, m_ref, v_ref, d_ref, mo_ref, vo_ref):
        gv = g_ref[...]
        m2 = ADAM_B1 * m_ref[...] + (1.0 - ADAM_B1) * gv
        v2 = ADAM_B2 * v_ref[...] + (1.0 - ADAM_B2) * (gv * gv)
        m_hat = m2 / (1.0 - ADAM_B1 ** ADAM_STEP)
        v_hat = v2 / (1.0 - ADAM_B2 ** ADAM_STEP)
        d_ref[...] = -ADAM_LR * (m_hat / (jnp.sqrt(v_hat) + ADAM_EPS) + ADAM_WD * w_ref[...])
        mo_ref[...] = m2
        vo_ref[...] = v2

    blk = BS((tr, c), lambda i: (i, 0))
    return pl.pallas_call(
        body, name="adamw", grid=(r // tr,), in_specs=[blk] * 4, out_specs=[blk] * 3,
        out_shape=[S((r, c), F32)] * 3, compiler_params=_cp("arbitrary"),
    )(w, g, m, v)


def _blob(arrays):
    flat = jnp.concatenate([a.reshape(-1) for a in arrays])
    rows = -(-flat.shape[0] // (SUB * LANES)) * SUB
    return jnp.pad(flat, (0, rows * LANES - flat.shape[0])).reshape(rows, LANES)


def _unblob(blob, shapes, lead=()):
    flat = blob.reshape(lead + (-1,))
    out, off = [], 0
    for s in shapes:
        size = math.prod(s)
        out.append(flat[..., off:off + size].reshape(lead + tuple(s)))
        off += size
    return out


def _lanes6(a):
    return jnp.zeros((1, LANES), F32).at[0, GG:GG + C_HEADS].set(a)


def _y_rows(w):
    return jnp.concatenate([w[0:A_W], w[A_W + B_W:], w[A_W:A_W + B_W]], axis=0)


def _y_rows_back(g):
    return jnp.concatenate([g[0:A_W], g[A_W + C_W:], g[A_W:A_W + C_W]], axis=0)


SMALL = ("norm_w", "q_norm_w", "k_norm_w", "sinks", "b_conv_b", "b_ln_w", "b_ln_b", "b_pw_b", "c_a_log", "c_dt_bias",
         "c_onorm_w", "b_conv_w", "c_conv_w")
ORDER = ("norm_w", "w_in", "q_norm_w", "k_norm_w", "sinks", "b_conv_w", "b_conv_b", "b_ln_w", "b_ln_b", "b_pw_w", "b_pw_b",
         "c_conv_w", "c_a_log", "c_dt_bias", "c_onorm_w", "w_out")


def kernel(x, positions, norm_w, w_in, q_norm_w, k_norm_w, sinks, b_conv_w, b_conv_b, b_ln_w, b_ln_b, b_pw_w, b_pw_b, c_conv_w, c_a_log, c_dt_bias, c_onorm_w, w_out, loss_target, m_norm_w, m_w_in, m_q_norm_w, m_k_norm_w, m_sinks, m_b_conv_w, m_b_conv_b, m_b_ln_w, m_b_ln_b, m_b_pw_w, m_b_pw_b, m_c_conv_w, m_c_a_log, m_c_dt_bias, m_c_onorm_w, m_w_out, v_norm_w, v_w_in, v_q_norm_w, v_k_norm_w, v_sinks, v_b_conv_w, v_b_conv_b, v_b_ln_w, v_b_ln_b, v_b_pw_w, v_b_pw_b, v_c_conv_w, v_c_a_log, v_c_dt_bias, v_c_onorm_w, v_w_out):
    W = dict(norm_w=norm_w, w_in=w_in, q_norm_w=q_norm_w, k_norm_w=k_norm_w, sinks=sinks, b_conv_w=b_conv_w, b_conv_b=b_conv_b,
             b_ln_w=b_ln_w, b_ln_b=b_ln_b, b_pw_w=b_pw_w, b_pw_b=b_pw_b, c_conv_w=c_conv_w, c_a_log=c_a_log,
             c_dt_bias=c_dt_bias, c_onorm_w=c_onorm_w, w_out=w_out)
    M = dict(norm_w=m_norm_w, w_in=m_w_in, q_norm_w=m_q_norm_w, k_norm_w=m_k_norm_w, sinks=m_sinks, b_conv_w=m_b_conv_w,
             b_conv_b=m_b_conv_b, b_ln_w=m_b_ln_w, b_ln_b=m_b_ln_b, b_pw_w=m_b_pw_w, b_pw_b=m_b_pw_b, c_conv_w=m_c_conv_w,
             c_a_log=m_c_a_log, c_dt_bias=m_c_dt_bias, c_onorm_w=m_c_onorm_w, w_out=m_w_out)
    V = dict(norm_w=v_norm_w, w_in=v_w_in, q_norm_w=v_q_norm_w, k_norm_w=v_k_norm_w, sinks=v_sinks, b_conv_w=v_b_conv_w,
             b_conv_b=v_b_conv_b, b_ln_w=v_b_ln_w, b_ln_b=v_b_ln_b, b_pw_w=v_b_pw_w, b_pw_b=v_b_pw_b, c_conv_w=v_c_conv_w,
             c_a_log=v_c_a_log, c_dt_bias=v_c_dt_bias, c_onorm_w=v_c_onorm_w, w_out=v_w_out)
    nseq, t, d = x.shape
    n = nseq * t
    tr = min(256, t)
    tmm = min(512, n)
    me = _dev_index(_my_pos())
    xs = [x.reshape(n, d)]
    tgt = loss_target.reshape(n, d)
    tabs = _rope_tables(positions.reshape(n))

    win_p = _pack_cols(w_in).astype(BF16)
    wout_b = w_out.astype(BF16)
    sharded_small = (b_pw_w, b_conv_w, c_conv_w)
    g_win0, g_small = _all_gather([win_p[0], _blob(sharded_small)])
    win = [g_win0]
    later = _exchange_start("gather", [win_p[1], wout_b[0], wout_b[1]], g_small, "gather_start")
    pw_all, cw_all, ccw_all = _unblob(g_small, [a.shape for a in sharded_small], lead=(N_DEV,))
    pw_all = pw_all.transpose(1, 0, 2, 3).reshape(DEPTH, B_W, B_W).astype(BF16)
    cw_all = cw_all.transpose(1, 2, 0, 3).reshape(DEPTH, B_K, B_W)
    ccw_all = ccw_all.transpose(1, 2, 0, 3).reshape(DEPTH, C_K, 3 * C_W)

    def layer_params(l):
        return dict(
            nw=norm_w[l][None], qw=jnp.tile(q_norm_w[l], 2)[None], kw=jnp.tile(k_norm_w[l], 2)[None], sinks=sinks[l],
            cw=jnp.pad(cw_all[l], ((0, HALO_B - B_K), (0, 0))), cb=b_conv_b[l][None], lnw=b_ln_w[l][None], lnb=b_ln_b[l][None],
            pw=pw_all[l], pwb=b_pw_b[l][None], ccw=jnp.pad(ccw_all[l], ((0, SUB - C_K), (0, 0))),
            alog=_lanes6(c_a_log[l]), dtb=_lanes6(c_dt_bias[l]), onw=c_onorm_w[l][None])

    saved = []
    for l in range(DEPTH):
        q = layer_params(l)
        nw = q["nw"] + later["token"][0:1, 0:1] if l == 0 else q["nw"]
        p, h = _inproj(xs[l], nw, win[l], tm=tmm)
        y, o_a, lse = _attn_fwd(p, tabs, q["qw"], q["kw"], q["sinks"], nseq)
        gates = _gdn_gates_fwd(p, q["alog"], q["dtb"], tm=tr)
        xc, qkv = _gdn_pre_fwd(p, q["ccw"], nseq, tm=tr)
        y, o_c, u, w, tinv, ss = _gdn_chunk_fwd(qkv, gates, p, y, q["onw"], nseq)
        y, hc = _conf_fwd(p, y, q["cw"], q["cb"], q["lnw"], q["lnb"], q["pw"], q["pwb"], nseq, tm=tr)
        saved.append(dict(q=q, p=p, h=h, y=y, o_a=o_a, lse=lse, gates=gates, xc=xc, qkv=qkv, o_c=o_c, u=u, w=w, tinv=tinv,
                          ss=ss, hc=hc))
        if l == 0:
            g_win1, g_wout0, g_wout1 = _exchange_wait(later, y, "gather_wait")
            win.append(g_win1)
            wout = [_y_rows(g_wout0), _y_rows(g_wout1)]
        if l + 1 < DEPTH:
            xs.append(_outproj(xs[l], y, wout[l], tm=tmm))
        else:
            dxn, lsum = _outproj_loss(xs[l], y, wout[l], tgt, tm=tmm)
    loss = lax.psum(jnp.sum(lsum) * (0.5 / d), ("x", "y", "c"))

    sent, smalls = [None] * DEPTH, [None] * DEPTH
    for l in reversed(range(DEPTH)):
        s = saved[l]
        q, p = s["q"], s["p"]
        dy = _matmul(dxn, wout[l], "nt", F32, tmm, 512, d, "outproj_bwd_dy")
        dwout = _y_rows_back(_matmul(s["y"], dxn, "tn", BF16, 1024, 1024, tmm, "outproj_bwd_dw"))
        dp, dkv, dqw, dkw, dsk = _attn_bwd(p, dy, s["o_a"], s["lse"], tabs, q["qw"], q["kw"], q["sinks"], nseq)
        dp = _put_cols(dp, dkv, P_K, tm=tmm)
        dp, dqkv, dgate, donw = _gdn_chunk_bwd(s["qkv"], s["gates"], p, dy, dp, q["onw"], s["o_c"], s["u"], s["w"],
                                               s["tinv"], s["ss"], nseq)
        dp, dal, ddb = _gdn_gates_bwd(dgate, p, q["alog"], q["dtb"], dp, tm=tr)
        dp, dccw = _gdn_pre_bwd(p, dqkv, s["xc"], dp, q["ccw"], nseq, tm=tr)
        dp, dhc, dpw, dpwb, dlnw, dlnb, dcb = _conf_bwd1(p, dy, dp, s["hc"], q["lnw"], q["lnb"], q["pw"], q["pwb"], tm=tr)
        dp, dcw = _conf_bwd2(p, dhc, dp, q["cw"], nseq, tm=tr)
        dwin = _matmul(s["h"], dp, "tn", BF16, 1024, 768, tmm, "inproj_bwd_dw")
        sent[l] = _exchange_start("scatter", [dwin, dwout, dpw], dpwb, "scatter_start_%d" % l)
        dxn, dnw = _inproj_bwd_dx(dp, win[l], xs[l], q["nw"] + sent[l]["token"][0:1, 0:1], dxn, tm=tmm)
        halves = lambda a: a.sum(0)[:A_DH] + a.sum(0)[A_DH:]
        smalls[l] = dict(
            norm_w=dnw.sum(0), q_norm_w=halves(dqw), k_norm_w=halves(dkw), sinks=dsk.sum(0)[:A_HEADS], b_conv_b=dcb.sum(0),
            b_ln_w=dlnw.sum(0), b_ln_b=dlnb.sum(0), b_pw_b=dpwb.sum(0), c_a_log=dal.sum(0)[GG:GG + C_HEADS],
            c_dt_bias=ddb.sum(0)[GG:GG + C_HEADS], c_onorm_w=donw.sum(0),
            b_conv_w=dcw.reshape(B_K, SUB, B_W).sum(1), c_conv_w=dccw.reshape(C_K, SUB, 3 * C_W).sum(1))
    grad_x = dxn.reshape(nseq, t, d)

    r_win1, r_wout1, r_pw1 = _exchange_wait(sent[1], dxn, "scatter_wait_1")
    r_win0, r_wout0, r_pw0 = _exchange_wait(sent[0], r_pw1, "scatter_wait_0")
    G = {}
    G["w_in"] = jnp.stack([_unpack_cols(_sum8(r_win0)), _unpack_cols(_sum8(r_win1))])
    G["w_out"] = jnp.stack([_sum8(r_wout0), _sum8(r_wout1)])
    G["b_pw_w"] = jnp.stack([_sum8(r_pw0), _sum8(r_pw1)])
    part = _blob([jnp.stack([smalls[l][k] for l in range(DEPTH)]) for k in SMALL])
    (tot,) = _all_gather([part])
    tot = _sum8(tot.reshape(N_DEV, part.shape[0], LANES))
    full_shapes = [(DEPTH,) + smalls[0][k].shape for k in SMALL]
    for k, g in zip(SMALL, _unblob(tot, full_shapes)):
        G[k] = g
    G["b_conv_w"] = lax.dynamic_slice_in_dim(G["b_conv_w"], me * (B_W // N_DEV), B_W // N_DEV, axis=2)
    G["c_conv_w"] = lax.dynamic_slice_in_dim(G["c_conv_w"], me * (3 * C_W // N_DEV), 3 * C_W // N_DEV, axis=2)

    delta, new_m, new_v = {}, {}, {}
    for k in ("w_in", "w_out", "b_pw_w"):
        shp = W[k].shape
        two = lambda a: a.reshape(shp[0] * shp[1], shp[2])
        dl, mo, vo = _adamw(two(W[k]), two(G[k]), two(M[k]), two(V[k]))
        delta[k], new_m[k], new_v[k] = dl.reshape(shp), mo.reshape(shp), vo.reshape(shp)
    dl, mo, vo = _adamw(*[_blob([src[k] for k in SMALL]) for src in (W, G, M, V)])
    shapes = [W[k].shape for k in SMALL]
    for k, a, b, c in zip(SMALL, _unblob(dl, shapes), _unblob(mo, shapes), _unblob(vo, shapes)):
        delta[k], new_m[k], new_v[k] = a, b, c
    return (loss, grad_x, *[G[k] for k in ORDER], *[delta[k] for k in ORDER], *[new_m[k] for k in ORDER],
            *[new_v[k] for k in ORDER])
```

```python
import functools
import math

import jax
import jax.numpy as jnp
from jax import lax
from jax.experimental import pallas as pl
from jax.experimental.pallas import tpu as pltpu

F32 = jnp.float32
BF16 = jnp.bfloat16
HI = lax.Precision.HIGHEST
MESH = pl.DeviceIdType.MESH
S = jax.ShapeDtypeStruct
BS = pl.BlockSpec

N_DEV = 8
DEPTH = 2
D_MODEL = 2048
A_HEADS, A_KV, A_DH, A_W, A_KVW = 12, 4, 64, 768, 256
ROT = 16
THETA = 500000.0
ABLK = 128
B_W, B_K = 512, 31
C_HEADS, C_DH, C_W, C_K, CHUNK = 6, 128, 768, 4, 64
EPS = 1e-6
IN_COLS = 6668
P_Q, P_ZA, P_ZC, P_QKV, P_K, P_V, P_UB, P_ZB, P_BA, P_W = 0, 768, 1536, 2304, 4608, 4864, 5120, 6144, 6656, 6912
Y_A, Y_C, Y_B = 0, 768, 1536
LANES = 128
SUB = 8

ADAM_LR, ADAM_B1, ADAM_B2, ADAM_EPS, ADAM_WD, ADAM_STEP = 0.001, 0.9, 0.999, 1e-08, 0.01, 10


def _cp(*sem, vmem=None):
    kw = {}
    if sem:
        kw["dimension_semantics"] = sem
    if vmem:
        kw["vmem_limit_bytes"] = vmem
    return pltpu.CompilerParams(**kw)


def _pack_cols(w):
    z = jnp.zeros(w.shape[:-1] + (P_W - IN_COLS,), w.dtype)
    return jnp.concatenate([w[..., 0:768], w[..., 1280:2048], w[..., 5900:6668], w[..., 3584:5888],
                            w[..., 768:1024], w[..., 1024:1280], w[..., 2048:3072], w[..., 3072:3584],
                            w[..., 5888:5900], z], axis=-1)


def _unpack_cols(g):
    return jnp.concatenate([g[..., P_Q:P_Q + 768], g[..., P_K:P_K + 256], g[..., P_V:P_V + 256], g[..., P_ZA:P_ZA + 768],
                            g[..., P_UB:P_UB + 1024], g[..., P_ZB:P_ZB + 512], g[..., P_QKV:P_QKV + 2304],
                            g[..., P_BA:P_BA + 12], g[..., P_ZC:P_ZC + 768]], axis=-1)


def _sigmoid(x):
    return 1.0 / (1.0 + jnp.exp(-x))


def _dsilu(x, sg):
    return sg * (1.0 + x * (1.0 - sg))


def _fold8(x):
    r, c = x.shape
    return x.reshape(r // SUB, SUB, c).sum(axis=0)


def _dot(a, b, prec=None):
    return jnp.dot(a, b, preferred_element_type=F32, precision=prec)


def _dot_nt(a, b, prec=None):
    return lax.dot_general(a, b, (((1,), (1,)), ((), ())), preferred_element_type=F32, precision=prec)


def _dot_tn(a, b, prec=None):
    return lax.dot_general(a, b, (((0,), (0,)), ((), ())), preferred_element_type=F32, precision=prec)


def _lane(shape):
    return lax.broadcasted_iota(jnp.int32, shape, 1)


def _subl(shape):
    return lax.broadcasted_iota(jnp.int32, shape, 0)


def _col(x, j):
    return jnp.sum(jnp.where(_lane(x.shape) == j, x, 0.0), axis=-1, keepdims=True)


def _inproj(x, nw, w, tm=512, tn=768):
    n, d = x.shape
    pw = w.shape[1]

    def body(x_ref, nw_ref, w_ref, p_ref, h_ref):
        @pl.when(pl.program_id(1) == 0)
        def _():
            xv = x_ref[...]
            r = lax.rsqrt(jnp.mean(xv * xv, axis=-1, keepdims=True) + EPS)
            h_ref[...] = (xv * r * nw_ref[...]).astype(BF16)

        p_ref[...] = _dot(h_ref[...], w_ref[...])

    return pl.pallas_call(
        body, name="inproj", grid=(n // tm, pw // tn),
        in_specs=[BS((tm, d), lambda i, j: (i, 0)), BS((1, d), lambda i, j: (0, 0)), BS((d, tn), lambda i, j: (0, j))],
        out_specs=[BS((tm, tn), lambda i, j: (i, j)), BS((tm, d), lambda i, j: (i, 0))],
        out_shape=[S((n, pw), F32), S((n, d), BF16)],
        compiler_params=_cp("arbitrary", "arbitrary"),
    )(x, nw, w)


def _outproj(x, y, w, tm=512, tn=1024):
    n, d = x.shape
    k = y.shape[1]

    def body(x_ref, y_ref, w_ref, o_ref):
        o_ref[...] = x_ref[...] + _dot(y_ref[...], w_ref[...])

    return pl.pallas_call(
        body, name="outproj", grid=(n // tm, d // tn),
        in_specs=[BS((tm, tn), lambda i, j: (i, j)), BS((tm, k), lambda i, j: (i, 0)), BS((k, tn), lambda i, j: (0, j))],
        out_specs=BS((tm, tn), lambda i, j: (i, j)),
        out_shape=S((n, d), F32),
        compiler_params=_cp("arbitrary", "arbitrary"),
    )(x, y, w)


def _outproj_loss(x, y, w, tgt, tm=512, tn=1024):
    n, d = x.shape
    k = y.shape[1]

    def body(x_ref, y_ref, w_ref, t_ref, g_ref, l_ref):
        @pl.when((pl.program_id(0) == 0) & (pl.program_id(1) == 0))
        def _():
            l_ref[...] = jnp.zeros_like(l_ref)

        diff = x_ref[...] + _dot(y_ref[...], w_ref[...]) - t_ref[...]
        g_ref[...] = diff * (1.0 / d)
        f = _fold8(diff * diff)
        acc = f[:, 0:LANES]
        for c in range(1, tn // LANES):
            acc = acc + f[:, c * LANES:(c + 1) * LANES]
        l_ref[...] += acc

    return pl.pallas_call(
        body, name="outproj_loss", grid=(n // tm, d // tn),
        in_specs=[BS((tm, tn), lambda i, j: (i, j)), BS((tm, k), lambda i, j: (i, 0)), BS((k, tn), lambda i, j: (0, j)),
                  BS((tm, tn), lambda i, j: (i, j))],
        out_specs=[BS((tm, tn), lambda i, j: (i, j)), BS((SUB, LANES), lambda i, j: (0, 0))],
        out_shape=[S((n, d), F32), S((SUB, LANES), F32)],
        compiler_params=_cp("arbitrary", "arbitrary"),
    )(x, y, w, tgt)


def _matmul(a, b, mode, out_dtype, tm, tn, tk, name):
    if mode == "nn":
        (m, kk), nn = a.shape, b.shape[1]
        a_spec, b_spec = BS((tm, tk), lambda i, j, k: (i, k)), BS((tk, tn), lambda i, j, k: (k, j))
        dot = _dot
    elif mode == "nt":
        (m, kk), nn = a.shape, b.shape[0]
        a_spec, b_spec = BS((tm, tk), lambda i, j, k: (i, k)), BS((tn, tk), lambda i, j, k: (j, k))
        dot = _dot_nt
    else:
        (kk, m), nn = a.shape, b.shape[1]
        a_spec, b_spec = BS((tk, tm), lambda i, j, k: (k, i)), BS((tk, tn), lambda i, j, k: (k, j))
        dot = _dot_tn
    nk = kk // tk

    def body(a_ref, b_ref, o_ref, acc_ref):
        kid = pl.program_id(2)

        @pl.when(kid == 0)
        def _():
            acc_ref[...] = jnp.zeros_like(acc_ref)

        acc_ref[...] += dot(a_ref[...].astype(BF16), b_ref[...].astype(BF16))

        @pl.when(kid == nk - 1)
        def _():
            o_ref[...] = acc_ref[...].astype(out_dtype)

    return pl.pallas_call(
        body, name=name, grid=(m // tm, nn // tn, nk),
        in_specs=[a_spec, b_spec], out_specs=BS((tm, tn), lambda i, j, k: (i, j)),
        out_shape=S((m, nn), out_dtype), scratch_shapes=[pltpu.VMEM((tm, tn), F32)],
        compiler_params=_cp("arbitrary", "arbitrary", "arbitrary"),
    )(a, b)


SLAB = 16


def _inproj_bwd_dx(dp, w, x, nw, dres, tm=512, tk=1152):
    n, d = x.shape
    nk = dp.shape[1] // tk

    def body(dp_ref, w_ref, x_ref, nw_ref, dr_ref, dx_ref, dnw_ref, acc_ref):
        kid = pl.program_id(1)

        @pl.when((pl.program_id(0) == 0) & (kid == 0))
        def _():
            dnw_ref[...] = jnp.zeros_like(dnw_ref)

        @pl.when(kid == 0)
        def _():
            acc_ref[...] = jnp.zeros_like(acc_ref)

        acc_ref[...] += _dot_nt(dp_ref[...], w_ref[...])

        @pl.when(kid == nk - 1)
        def _():
            def slab(i, carry):
                rows = pl.ds(pl.multiple_of(i * SLAB, SLAB), SLAB)
                dh, xv = acc_ref[rows, :], x_ref[rows, :]
                r = lax.rsqrt(jnp.mean(xv * xv, axis=-1, keepdims=True) + EPS)
                dnw_ref[...] += _fold8(dh * xv * r)
                g = dh * nw_ref[...]
                mm = jnp.mean(g * xv, axis=-1, keepdims=True)
                dx_ref[rows, :] = dr_ref[rows, :] + r * g - xv * (r * r * r * mm)
                return carry

            lax.fori_loop(0, tm // SLAB, slab, 0)

    return pl.pallas_call(
        body, name="inproj_bwd_dx", grid=(n // tm, nk),
        in_specs=[BS((tm, tk), lambda i, k: (i, k)), BS((d, tk), lambda i, k: (0, k)), BS((tm, d), lambda i, k: (i, 0)),
                  BS((1, d), lambda i, k: (0, 0)), BS((tm, d), lambda i, k: (i, 0))],
        out_specs=[BS((tm, d), lambda i, k: (i, 0)), BS((SUB, d), lambda i, k: (0, 0))],
        out_shape=[S((n, d), F32), S((SUB, d), F32)],
        scratch_shapes=[pltpu.VMEM((tm, d), F32)],
        compiler_params=_cp("arbitrary", "arbitrary"),
    )(dp, w, x, nw, dres)


def _rope_tables(pos):
    half = ROT // 2
    inv = THETA ** (-jnp.arange(0, ROT, 2, dtype=F32) / ROT)
    ang = pos.astype(F32)[:, None] * inv
    cos, sin = jnp.cos(ang), jnp.sin(ang)
    n = pos.shape[0]
    one = jnp.ones((n, A_DH - ROT), F32)
    zero = jnp.zeros((n, A_DH - ROT), F32)
    zh = jnp.zeros((n, half), F32)
    c = jnp.concatenate([cos, cos, one], axis=1)
    s1 = jnp.concatenate([-sin, zh, zero], axis=1)
    s2 = jnp.concatenate([zh, sin, zero], axis=1)
    return tuple(jnp.concatenate([t, t], axis=1) for t in (c, s1, s2))


def _half_stat(t):
    lo = _lane(t.shape) < A_DH
    s_lo = jnp.sum(jnp.where(lo, t, 0.0), axis=-1, keepdims=True)
    s_hi = jnp.sum(jnp.where(lo, 0.0, t), axis=-1, keepdims=True)
    return jnp.where(lo, s_lo, s_hi)


def _normrope(x, w, c, s1, s2):
    r = lax.rsqrt(_half_stat(x * x) * (1.0 / A_DH) + EPS)
    xn = x * r * w
    return xn * c + pltpu.roll(xn, LANES - ROT // 2, 1) * s1 + pltpu.roll(xn, ROT // 2, 1) * s2, r


def _normrope_bwd(dy, x, r, w, c, s1, s2):
    dxn = dy * c + pltpu.roll(dy * s1, ROT // 2, 1) + pltpu.roll(dy * s2, LANES - ROT // 2, 1)
    g = dxn * w
    mm = _half_stat(g * x) * (1.0 / A_DH)
    return r * g - x * (r * r * r * mm), dxn * x * r


def _attn_mask(first):
    qi = _subl((ABLK, 2 * ABLK))
    kj = _lane((ABLK, 2 * ABLK))
    dist = qi + ABLK - kj
    return (dist >= 0) & (dist < ABLK) & (jnp.logical_not(first) | (kj >= ABLK))


def _attn_fwd(p, tabs, qw, kw, sinks, nseq):
    n = p.shape[0]
    nb = n // nseq // ABLK
    cur = lambda b, i: (b * nb + i, 0)
    prv = lambda b, i: (b * nb + jnp.maximum(i - 1, 0), 0)
    colblk = lambda f, w, off: (lambda b, i: (f(b, i)[0], off // w))

    def body(q_ref, za_ref, kc_ref, vc_ref, kp_ref, vp_ref, c_ref, s1_ref, s2_ref, cp_ref, s1p_ref, s2p_ref,
             qw_ref, kw_ref, sink_ref, y_ref, o_ref, lse_ref):
        first = pl.program_id(1) == 0
        tc = (c_ref[...], s1_ref[...], s2_ref[...])
        tp = (cp_ref[...], s1p_ref[...], s2p_ref[...])
        q, kc, kp = q_ref[...], kc_ref[...], kp_ref[...]
        qn = [_normrope(q[:, LANES * b:LANES * (b + 1)], qw_ref[...], *tc)[0].astype(BF16) for b in range(A_W // LANES)]
        k2, v2 = [], []
        for b in range(A_KVW // LANES):
            sl = slice(LANES * b, LANES * (b + 1))
            k2.append(jnp.concatenate([_normrope(kp[:, sl], kw_ref[...], *tp)[0],
                                       _normrope(kc[:, sl], kw_ref[...], *tc)[0]], axis=0).astype(BF16))
            v2.append(jnp.concatenate([vp_ref[:, sl], vc_ref[:, sl]], axis=0).astype(BF16))
        valid = _attn_mask(first)
        outs = []
        lse = jnp.zeros((ABLK, LANES), F32)
        for g in range(A_KV):
            hs = slice(A_DH * (g % 2), A_DH * (g % 2 + 1))
            kh, vh = k2[g // 2][:, hs], v2[g // 2][:, hs]
            for j in range(3 * g, 3 * g + 3):
                qh = qn[j // 2][:, A_DH * (j % 2):A_DH * (j % 2 + 1)]
                s = jnp.where(valid, _dot_nt(qh, kh) * (A_DH ** -0.5), -jnp.inf)
                sk = sink_ref[j]
                m = jnp.maximum(jnp.max(s, axis=-1, keepdims=True), sk)
                e = jnp.exp(s - m)
                den = jnp.sum(e, axis=-1, keepdims=True) + jnp.exp(sk - m)
                outs.append(_dot((e / den).astype(BF16), vh))
                lse = jnp.where(_lane(lse.shape) == j, m + jnp.log(den), lse)
        o = jnp.concatenate(outs, axis=1)
        za = za_ref[...]
        o_ref[...] = o
        lse_ref[...] = lse
        y_ref[...] = (o * (za * _sigmoid(za))).astype(BF16)

    tab_specs = [BS((ABLK, LANES), cur)] * 3 + [BS((ABLK, LANES), prv)] * 3
    return pl.pallas_call(
        body, name="attn_fwd", grid=(nseq, nb),
        in_specs=[BS((ABLK, A_W), colblk(cur, A_W, P_Q)), BS((ABLK, A_W), colblk(cur, A_W, P_ZA)),
                  BS((ABLK, A_KVW), colblk(cur, A_KVW, P_K)), BS((ABLK, A_KVW), colblk(cur, A_KVW, P_V)),
                  BS((ABLK, A_KVW), colblk(prv, A_KVW, P_K)), BS((ABLK, A_KVW), colblk(prv, A_KVW, P_V))]
        + tab_specs + [BS((1, LANES), lambda b, i: (0, 0))] * 2 + [BS(memory_space=pltpu.SMEM)],
        out_specs=[BS((ABLK, A_W), colblk(cur, A_W, Y_A)), BS((ABLK, A_W), cur), BS((ABLK, LANES), cur)],
        out_shape=[S((n, D_MODEL), BF16), S((n, A_W), F32), S((n, LANES), F32)],
        compiler_params=_cp("arbitrary", "arbitrary"),
    )(p, p, p, p, p, p, *tabs, *tabs, qw, kw, sinks)


def _attn_bwd(p, dy, o, lse, tabs, qw, kw, sinks, nseq):
    n = p.shape[0]
    nb = n // nseq // ABLK
    cur = lambda b, i: (b * nb + jnp.minimum(i, nb - 1), 0)
    prv = lambda b, i: (b * nb + jnp.maximum(i - 1, 0), 0)
    colblk = lambda f, w, off: (lambda b, i: (f(b, i)[0], off // w))

    def body(q_ref, za_ref, kc_ref, vc_ref, kp_ref, vp_ref, dy_ref, o_ref, lse_ref,
             c_ref, s1_ref, s2_ref, cp_ref, s1p_ref, s2p_ref, qw_ref, kw_ref, sink_ref,
             dqza_ref, dkv_ref, dqw_ref, dkw_ref, dsk_ref, tk_ref, tv_ref, ck_ref, cv_ref):
        i = pl.program_id(1)
        first = i == 0
        tc = (c_ref[...], s1_ref[...], s2_ref[...])
        tp = (cp_ref[...], s1p_ref[...], s2p_ref[...])
        nkb = A_KVW // LANES

        @pl.when((pl.program_id(0) == 0) & first)
        def _():
            dqw_ref[...] = jnp.zeros_like(dqw_ref)
            dkw_ref[...] = jnp.zeros_like(dkw_ref)
            dsk_ref[...] = jnp.zeros_like(dsk_ref)

        @pl.when(i < nb)
        def _():
            q, kc, kp = q_ref[...], kc_ref[...], kp_ref[...]
            qn, rq = [], []
            for b in range(A_W // LANES):
                a, r = _normrope(q[:, LANES * b:LANES * (b + 1)], qw_ref[...], *tc)
                qn.append(a.astype(BF16))
                rq.append(r)
            k2, v2 = [], []
            for b in range(nkb):
                sl = slice(LANES * b, LANES * (b + 1))
                k2.append(jnp.concatenate([_normrope(kp[:, sl], kw_ref[...], *tp)[0],
                                           _normrope(kc[:, sl], kw_ref[...], *tc)[0]], axis=0).astype(BF16))
                v2.append(jnp.concatenate([vp_ref[:, sl], vc_ref[:, sl]], axis=0).astype(BF16))
            valid = _attn_mask(first)
            za, dy, o, lse = za_ref[...], dy_ref[...], o_ref[...], lse_ref[...]
            sg = _sigmoid(za)
            do = dy * za * sg
            dqza_ref[:, A_W:2 * A_W] = (dy * o * _dsilu(za, sg)).astype(BF16)
            dqs, dks, dvs = [], [], []
            dsk = jnp.zeros((ABLK, LANES), F32)
            for g in range(A_KV):
                hs = slice(A_DH * (g % 2), A_DH * (g % 2 + 1))
                kh, vh = k2[g // 2][:, hs], v2[g // 2][:, hs]
                dkg = jnp.zeros((2 * ABLK, A_DH), F32)
                dvg = jnp.zeros((2 * ABLK, A_DH), F32)
                for j in range(3 * g, 3 * g + 3):
                    js = slice(A_DH * j, A_DH * (j + 1))
                    qh = qn[j // 2][:, A_DH * (j % 2):A_DH * (j % 2 + 1)]
                    lj = _col(lse, j)
                    s = jnp.where(valid, _dot_nt(qh, kh) * (A_DH ** -0.5), -jnp.inf)
                    pr = jnp.exp(s - lj)
                    doh = do[:, js]
                    delta = jnp.sum(doh * o[:, js], axis=-1, keepdims=True)
                    dsk = dsk + jnp.where(_lane(dsk.shape) == j, -jnp.exp(sink_ref[j] - lj) * delta, 0.0)
                    dohb = doh.astype(BF16)
                    ds = (pr * (_dot_nt(dohb, vh) - delta) * (A_DH ** -0.5)).astype(BF16)
                    dqs.append(_dot(ds, kh))
                    dkg = dkg + _dot_tn(ds, qh)
                    dvg = dvg + _dot_tn(pr.astype(BF16), dohb)
                dks.append(dkg)
                dvs.append(dvg)
            dsk_ref[...] += _fold8(dsk)
            dqn = jnp.concatenate(dqs, axis=1)
            dqw = jnp.zeros((SUB, LANES), F32)
            dqo = []
            for b in range(A_W // LANES):
                sl = slice(LANES * b, LANES * (b + 1))
                dx, wt = _normrope_bwd(dqn[:, sl], q[:, sl], rq[b], qw_ref[...], *tc)
                dqo.append(dx)
                dqw = dqw + _fold8(wt)
            dqw_ref[...] += dqw
            dqza_ref[:, 0:A_W] = jnp.concatenate(dqo, axis=1).astype(BF16)
            tk_ref[...] = jnp.concatenate(dks, axis=1)
            tv_ref[...] = jnp.concatenate(dvs, axis=1)

        @pl.when(i == nb)
        def _():
            tk_ref[...] = jnp.zeros_like(tk_ref)
            tv_ref[...] = jnp.zeros_like(tv_ref)

        @pl.when(i > 0)
        def _():
            kp = kp_ref[...]
            dkn = ck_ref[...] + tk_ref[0:ABLK, :]
            dkw = jnp.zeros((SUB, LANES), F32)
            dko = []
            for b in range(nkb):
                sl = slice(LANES * b, LANES * (b + 1))
                r = _normrope(kp[:, sl], kw_ref[...], *tp)[1]
                dx, wt = _normrope_bwd(dkn[:, sl], kp[:, sl], r, kw_ref[...], *tp)
                dko.append(dx)
                dkw = dkw + _fold8(wt)
            dkw_ref[...] += dkw
            dkv_ref[:, 0:A_KVW] = jnp.concatenate(dko, axis=1).astype(BF16)
            dkv_ref[:, A_KVW:2 * A_KVW] = (cv_ref[...] + tv_ref[0:ABLK, :]).astype(BF16)

        ck_ref[...] = tk_ref[ABLK:2 * ABLK, :]
        cv_ref[...] = tv_ref[ABLK:2 * ABLK, :]

    tab_specs = [BS((ABLK, LANES), cur)] * 3 + [BS((ABLK, LANES), prv)] * 3
    acc = BS((SUB, LANES), lambda b, i: (0, 0))
    return pl.pallas_call(
        body, name="attn_bwd", grid=(nseq, nb + 1),
        in_specs=[BS((ABLK, A_W), colblk(cur, A_W, P_Q)), BS((ABLK, A_W), colblk(cur, A_W, P_ZA)),
                  BS((ABLK, A_KVW), colblk(cur, A_KVW, P_K)), BS((ABLK, A_KVW), colblk(cur, A_KVW, P_V)),
                  BS((ABLK, A_KVW), colblk(prv, A_KVW, P_K)), BS((ABLK, A_KVW), colblk(prv, A_KVW, P_V)),
                  BS((ABLK, A_W), colblk(cur, A_W, Y_A)), BS((ABLK, A_W), cur), BS((ABLK, LANES), cur)]
        + tab_specs + [BS((1, LANES), lambda b, i: (0, 0))] * 2 + [BS(memory_space=pltpu.SMEM)],
        out_specs=[BS((ABLK, 2 * A_W), cur), BS((ABLK, 2 * A_KVW), prv), acc, acc, acc],
        out_shape=[S((n, P_W), BF16), S((n, 2 * A_KVW), BF16)] + [S((SUB, LANES), F32)] * 3,
        scratch_shapes=[pltpu.VMEM((2 * ABLK, A_KVW), F32)] * 2 + [pltpu.VMEM((ABLK, A_KVW), F32)] * 2,
        compiler_params=_cp("arbitrary", "arbitrary"),
    )(p, p, p, p, p, p, dy, o, lse, *tabs, *tabs, qw, kw, sinks)


def _put_cols(dst, src, col_off, tm=512):
    n, w = src.shape

    def body(s_ref, d_in_ref, d_ref):
        d_ref[...] = s_ref[...]

    return pl.pallas_call(
        body, name="put_cols", grid=(n // tm,),
        in_specs=[BS((tm, w), lambda i: (i, 0)), BS(memory_space=pl.ANY)],
        out_specs=BS((tm, w), lambda i: (i, col_off // w)),
        out_shape=S(dst.shape, dst.dtype), input_output_aliases={1: 0},
        compiler_params=_cp("arbitrary"),
    )(src, dst)


HALO_B = 32


def _layernorm(hc, lnw, lnb):
    mu = jnp.mean(hc, axis=-1, keepdims=True)
    xc = hc - mu
    rstd = lax.rsqrt(jnp.mean(xc * xc, axis=-1, keepdims=True) + EPS)
    xhat = xc * rstd
    return xhat, rstd, xhat * lnw + lnb


def _conf_fwd(p, y, cw, cb, lnw, lnb, pw, pwb, nseq, tm=256):
    n = p.shape[0]
    t = n // nseq
    nt = t // tm
    row = lambda b, i: b * nt + i
    halo = lambda b, i: jnp.maximum((b * t + i * tm) // HALO_B - 1, 0)
    vec = BS((1, B_W), lambda b, i: (0, 0))

    def body(ub_ref, uh_ref, zb_ref, cw_ref, cb_ref, lnw_ref, lnb_ref, pw_ref, pwb_ref, y_in_ref, y_ref, hc_ref, buf_ref):
        ub, uh = ub_ref[...], uh_ref[...]
        hh = uh[:, :B_W] * _sigmoid(uh[:, B_W:])
        buf_ref[0:HALO_B, :] = jnp.where(pl.program_id(1) > 0, hh, 0.0)
        buf_ref[HALO_B:, :] = ub[:, :B_W] * _sigmoid(ub[:, B_W:])
        hc = jnp.zeros((tm, B_W), F32) + cb_ref[...]
        for k in range(B_K):
            hc = hc + cw_ref[k:k + 1, :] * buf_ref[pl.ds(HALO_B - B_K + 1 + k, tm), :]
        hc_ref[...] = hc
        ln = _layernorm(hc, lnw_ref[...], lnb_ref[...])[2]
        sw = ln * _sigmoid(ln)
        ob = _dot(sw.astype(BF16), pw_ref[...]) + pwb_ref[...]
        zb = zb_ref[...]
        y_ref[...] = (ob * (zb * _sigmoid(zb))).astype(BF16)

    return pl.pallas_call(
        body, name="conf_fwd", grid=(nseq, nt),
        in_specs=[BS((tm, 2 * B_W), lambda b, i: (row(b, i), P_UB // (2 * B_W))),
                  BS((HALO_B, 2 * B_W), lambda b, i: (halo(b, i), P_UB // (2 * B_W))),
                  BS((tm, B_W), lambda b, i: (row(b, i), P_ZB // B_W)),
                  BS((HALO_B, B_W), lambda b, i: (0, 0)), vec, vec, vec, BS((B_W, B_W), lambda b, i: (0, 0)), vec,
                  BS(memory_space=pl.ANY)],
        out_specs=[BS((tm, B_W), lambda b, i: (row(b, i), Y_B // B_W)), BS((tm, B_W), lambda b, i: (row(b, i), 0))],
        out_shape=[S(y.shape, y.dtype), S((n, B_W), F32)], input_output_aliases={9: 0},
        scratch_shapes=[pltpu.VMEM((HALO_B + tm, B_W), F32)],
        compiler_params=_cp("arbitrary", "arbitrary"),
    )(p, p, p, cw, cb, lnw, lnb, pw, pwb, y)


def _conf_bwd1(p, dy, dp, hc, lnw, lnb, pw, pwb, tm=256):
    n = p.shape[0]
    vec = BS((1, B_W), lambda i: (0, 0))
    acc = BS((SUB, B_W), lambda i: (0, 0))

    def body(dy_ref, zb_ref, hc_ref, lnw_ref, lnb_ref, pw_ref, pwb_ref, dp_in_ref,
             dzb_ref, dhc_ref, dpw_ref, dpwb_ref, dlnw_ref, dlnb_ref, dcb_ref):
        @pl.when(pl.program_id(0) == 0)
        def _():
            for r in (dpw_ref, dpwb_ref, dlnw_ref, dlnb_ref, dcb_ref):
                r[...] = jnp.zeros_like(r)

        xhat, rstd, ln = _layernorm(hc_ref[...], lnw_ref[...], lnb_ref[...])
        sgl = _sigmoid(ln)
        sw = (ln * sgl).astype(BF16)
        ob = _dot(sw, pw_ref[...]) + pwb_ref[...]
        dy, zb = dy_ref[...], zb_ref[...]
        sgz = _sigmoid(zb)
        dzb_ref[...] = (dy * ob * _dsilu(zb, sgz)).astype(BF16)
        dob = dy * zb * sgz
        dobb = dob.astype(BF16)
        dpwb_ref[...] += _fold8(dob)
        dpw_ref[...] += _dot_tn(sw, dobb)
        dln = _dot_nt(dobb, pw_ref[...]) * _dsilu(ln, sgl)
        dlnw_ref[...] += _fold8(dln * xhat)
        dlnb_ref[...] += _fold8(dln)
        dxh = dln * lnw_ref[...]
        dhc = rstd * (dxh - jnp.mean(dxh, axis=-1, keepdims=True) - xhat * jnp.mean(dxh * xhat, axis=-1, keepdims=True))
        dcb_ref[...] += _fold8(dhc)
        dhc_ref[...] = dhc

    return pl.pallas_call(
        body, name="conf_bwd1", grid=(n // tm,),
        in_specs=[BS((tm, B_W), lambda i: (i, Y_B // B_W)), BS((tm, B_W), lambda i: (i, P_ZB // B_W)),
                  BS((tm, B_W), lambda i: (i, 0)), vec, vec, BS((B_W, B_W), lambda i: (0, 0)), vec,
                  BS(memory_space=pl.ANY)],
        out_specs=[BS((tm, B_W), lambda i: (i, P_ZB // B_W)), BS((tm, B_W), lambda i: (i, 0)),
                   BS((B_W, B_W), lambda i: (0, 0)), acc, acc, acc, acc],
        out_shape=[S(dp.shape, dp.dtype), S((n, B_W), F32), S((B_W, B_W), F32)] + [S((SUB, B_W), F32)] * 4,
        input_output_aliases={7: 0},
        compiler_params=_cp("arbitrary"),
    )(dy, p, hc, lnw, lnb, pw, pwb, dp)


def _conf_bwd2(p, dhc, dp, cw, nseq, tm=256):
    n = p.shape[0]
    t = n // nseq
    nt = t // tm
    row = lambda b, i: b * nt + i
    prev = lambda b, i: jnp.maximum((b * t + i * tm) // HALO_B - 1, 0)
    nxt = lambda b, i: jnp.minimum((b * t + (i + 1) * tm) // HALO_B, n // HALO_B - 1)

    def body(ub_ref, uh_ref, dh_ref, dn_ref, cw_ref, dp_in_ref, dub_ref, dcw_ref, buf_ref, dbuf_ref):
        i = pl.program_id(1)

        @pl.when((pl.program_id(0) == 0) & (i == 0))
        def _():
            dcw_ref[...] = jnp.zeros_like(dcw_ref)

        ub, uh = ub_ref[...], uh_ref[...]
        a, sg = ub[:, :B_W], _sigmoid(ub[:, B_W:])
        buf_ref[0:HALO_B, :] = jnp.where(i > 0, uh[:, :B_W] * _sigmoid(uh[:, B_W:]), 0.0)
        buf_ref[HALO_B:, :] = a * sg
        dhc = dh_ref[...]
        dbuf_ref[0:tm, :] = dhc
        dbuf_ref[tm:, :] = jnp.where(i < nt - 1, dn_ref[...], 0.0)
        dhg = jnp.zeros((tm, B_W), F32)
        for k in range(B_K):
            dhg = dhg + cw_ref[k:k + 1, :] * dbuf_ref[pl.ds(B_K - 1 - k, tm), :]
            dcw_ref[SUB * k:SUB * (k + 1), :] += _fold8(dhc * buf_ref[pl.ds(HALO_B - B_K + 1 + k, tm), :])
        dub_ref[...] = jnp.concatenate([dhg * sg, dhg * a * sg * (1.0 - sg)], axis=1).astype(BF16)

    return pl.pallas_call(
        body, name="conf_bwd2", grid=(nseq, nt),
        in_specs=[BS((tm, 2 * B_W), lambda b, i: (row(b, i), P_UB // (2 * B_W))),
                  BS((HALO_B, 2 * B_W), lambda b, i: (prev(b, i), P_UB // (2 * B_W))),
                  BS((tm, B_W), lambda b, i: (row(b, i), 0)), BS((HALO_B, B_W), lambda b, i: (nxt(b, i), 0)),
                  BS((HALO_B, B_W), lambda b, i: (0, 0)), BS(memory_space=pl.ANY)],
        out_specs=[BS((tm, 2 * B_W), lambda b, i: (row(b, i), P_UB // (2 * B_W))),
                   BS((SUB * B_K, B_W), lambda b, i: (0, 0))],
        out_shape=[S(dp.shape, dp.dtype), S((SUB * B_K, B_W), F32)], input_output_aliases={5: 0},
        scratch_shapes=[pltpu.VMEM((HALO_B + tm, B_W), F32)] * 2,
        compiler_params=_cp("arbitrary", "arbitrary"),
    )(p, p, dhc, dhc, cw, dp)


HALO_C = 8
QS = C_DH ** -0.5
NCB = 3 * C_HEADS
CB0 = P_QKV // LANES
ZC0 = P_ZC // LANES
GB, GG = 0, C_HEADS


def _softplus(z):
    return jnp.maximum(z, 0.0) + jnp.log(1.0 + jnp.exp(-jnp.abs(z)))


def _gdn_gates_fwd(p, alog_l, dtb_l, tm=256):
    n = p.shape[0]

    def body(ba_ref, al_ref, db_ref, o_ref):
        blk = ba_ref[...]
        lane = _lane(blk.shape)
        g = jnp.where((lane >= GG) & (lane < GG + C_HEADS), -jnp.exp(al_ref[...]) * _softplus(blk + db_ref[...]), 0.0)
        tri = (_subl((CHUNK, CHUNK)) >= _lane((CHUNK, CHUNK))).astype(F32)
        gc = jnp.concatenate([_dot(tri, g[CHUNK * c:CHUNK * (c + 1)], HI) for c in range(tm // CHUNK)], axis=0)
        o_ref[...] = jnp.where(lane < GG, _sigmoid(blk), gc)

    return pl.pallas_call(
        body, name="gdn_gates_fwd", grid=(n // tm,),
        in_specs=[BS((tm, LANES), lambda i: (i, P_BA // LANES)), BS((1, LANES), lambda i: (0, 0)), BS((1, LANES), lambda i: (0, 0))],
        out_specs=BS((tm, LANES), lambda i: (i, 0)), out_shape=S((n, LANES), F32),
        compiler_params=_cp("arbitrary"),
    )(p, alog_l, dtb_l)


def _gdn_pre_fwd(p, ccw, nseq, tm=256):
    n = p.shape[0]
    t = n // nseq
    nt = t // tm
    row = lambda b, i: b * nt + i
    halo = lambda b, i: jnp.maximum((b * t + i * tm) // HALO_C - 1, 0)

    def body(x_ref, xh_ref, w_ref, xc_ref, o_ref, buf_ref):
        buf_ref[0:HALO_C, :] = jnp.where(pl.program_id(1) > 0, xh_ref[...], 0.0)
        buf_ref[HALO_C:, :] = x_ref[...]
        for c in range(NCB):
            cs = slice(LANES * c, LANES * (c + 1))
            xc = jnp.zeros((tm, LANES), F32)
            for k in range(C_K):
                xc = xc + w_ref[k:k + 1, cs] * buf_ref[pl.ds(HALO_C - C_K + 1 + k, tm), cs]
            xc_ref[:, cs] = xc
            act = xc * _sigmoid(xc)
            if c < 2 * C_HEADS:
                act = act * (lax.rsqrt(jnp.sum(act * act, axis=-1, keepdims=True) + EPS) * (QS if c < C_HEADS else 1.0))
            o_ref[:, cs] = act

    wide = 3 * C_W
    return pl.pallas_call(
        body, name="gdn_pre_fwd", grid=(nseq, nt),
        in_specs=[BS((tm, wide), lambda b, i: (row(b, i), P_QKV // wide)), BS((HALO_C, wide), lambda b, i: (halo(b, i), P_QKV // wide)),
                  BS((SUB, wide), lambda b, i: (0, 0))],
        out_specs=[BS((tm, wide), lambda b, i: (row(b, i), 0))] * 2,
        out_shape=[S((n, wide), F32)] * 2,
        scratch_shapes=[pltpu.VMEM((HALO_C + tm, wide), F32)],
        compiler_params=_cp("arbitrary", "arbitrary"),
    )(p, p, ccw)


def _chunk_common(q, k, gt, gtt, h):
    beta = _col(gt, GB + h)
    gc = _col(gt, GG + h)
    gcr = gtt[GG + h:GG + h + 1, :]
    ii, jj = _subl((CHUNK, CHUNK)), _lane((CHUNK, CHUNK))
    incl, strict = ii >= jj, ii > jj
    dec = jnp.exp(jnp.where(incl, gc - gcr, -jnp.inf))
    kb = k * beta
    kbf = k.astype(BF16)
    a = jnp.where(strict, _dot_nt(kb.astype(BF16), kbf) * dec, 0.0)
    mq = jnp.where(incl, _dot_nt(q.astype(BF16), kbf) * dec, 0.0)
    glast = jnp.sum(jnp.where(_subl(gc.shape) == CHUNK - 1, gc, 0.0), axis=0, keepdims=True)
    return beta, gc, incl, strict, dec, kb, a, mq, glast


def _unit_lower_inverses(mats):
    eye = (_subl(mats[0].shape) == _lane(mats[0].shape)).astype(F32)
    ms = [-a for a in mats]
    invs = [eye + m for m in ms]
    for _ in range(5):
        ms = [_dot(m, m, HI) for m in ms]
        invs = [inv + _dot(inv, m, HI) for inv, m in zip(invs, ms)]
    return invs


def _gdn_chunk_fwd(qkv, gates, p, y, onw, nseq, tt=512):
    n = qkv.shape[0]
    t = n // nseq
    tt = min(tt, t)
    nt = t // tt
    nch = tt // CHUNK

    def body(q_ref, k_ref, v_ref, g_ref, zc_ref, onw_ref, y_in_ref, y_ref, o_ref, u_ref, w_ref, t_ref, ss_ref, s_scr):
        @pl.when(pl.program_id(1) == 0)
        def _():
            s_scr[...] = jnp.zeros_like(s_scr)

        def step(c, carry):
            rows = pl.ds(pl.multiple_of(c * CHUNK, CHUNK), CHUNK)
            gt = g_ref[rows, :]
            gtt = gt.T
            heads = range(C_HEADS)
            hs = [slice(C_DH * h, C_DH * (h + 1)) for h in heads]
            q, k, v = ([r[rows, hs[h]] for h in heads] for r in (q_ref, k_ref, v_ref))
            cm = [_chunk_common(q[h], k[h], gt, gtt, h) for h in heads]
            beta, gc, kb, mq, glast = ([m[i] for m in cm] for i in (0, 1, 5, 7, 8))
            tinv = _unit_lower_inverses([m[6] for m in cm])
            egc = [jnp.exp(g) for g in gc]
            sol = [_dot(tinv[h], jnp.concatenate([v[h] * beta[h], kb[h] * egc[h]], axis=1), HI) for h in heads]
            sv = [s_scr[h] for h in heads]
            sb = [s.astype(BF16) for s in sv]
            vnb = [(sol[h][:, :C_DH] - _dot(sol[h][:, C_DH:].astype(BF16), sb[h])).astype(BF16) for h in heads]
            o = [_dot((q[h] * egc[h]).astype(BF16), sb[h]) + _dot(mq[h].astype(BF16), vnb[h]) for h in heads]
            for h in heads:
                ss_ref[h, c] = sv[h]
                s_scr[h] = sv[h] * jnp.exp(glast[h]) + _dot_tn((k[h] * jnp.exp(glast[h] - gc[h])).astype(BF16), vnb[h])
            for h in heads:
                o_ref[rows, hs[h]] = o[h]
                u_ref[rows, hs[h]] = sol[h][:, :C_DH]
                w_ref[rows, hs[h]] = sol[h][:, C_DH:]
                t_ref[rows, hs[h]] = jnp.concatenate([tinv[h], jnp.zeros_like(tinv[h])], axis=1)
                zc = zc_ref[rows, hs[h]]
                r = lax.rsqrt(jnp.mean(o[h] * o[h], axis=-1, keepdims=True) + EPS)
                y_ref[rows, hs[h]] = (o[h] * r * onw_ref[...] * (zc * _sigmoid(zc))).astype(BF16)
            return carry

        lax.fori_loop(0, nch, step, 0)

    row = lambda b, i: b * nt + i
    wb = lambda col: BS((tt, C_W), lambda b, i: (row(b, i), col))
    return pl.pallas_call(
        body, name="gdn_chunk_fwd", grid=(nseq, nt),
        in_specs=[wb(0), wb(1), wb(2), BS((tt, LANES), lambda b, i: (row(b, i), 0)), wb(P_ZC // C_W),
                  BS((1, LANES), lambda b, i: (0, 0)), BS(memory_space=pl.ANY)],
        out_specs=[wb(Y_C // C_W), wb(0), wb(0), wb(0), wb(0),
                   BS((None, C_HEADS, nch, C_DH, C_DH), lambda b, i: (b, 0, i, 0, 0))],
        out_shape=[S(y.shape, y.dtype)] + [S((n, C_W), F32)] * 4 + [S((nseq, C_HEADS, t // CHUNK, C_DH, C_DH), F32)],
        input_output_aliases={6: 0},
        scratch_shapes=[pltpu.VMEM((C_HEADS, C_DH, C_DH), F32)],
        compiler_params=_cp("arbitrary", "arbitrary"),
    )(qkv, qkv, qkv, gates, p, onw, y)


def _gdn_chunk_bwd(qkv, gates, p, dy, dp, onw, o, u, w, tinv, ss, nseq, tt=256):
    n = qkv.shape[0]
    t = n // nseq
    tt = min(tt, t)
    nt = t // tt
    nch = tt // CHUNK

    def body(q_ref, k_ref, v_ref, g_ref, zc_ref, onw_ref, o_ref, dy_ref, u_ref, w_ref, t_ref, ss_ref, dp_in_ref,
             dzc_ref, dqkv_ref, dg_ref, donw_ref, ds_scr):
        @pl.when(pl.program_id(1) == 0)
        def _():
            ds_scr[...] = jnp.zeros_like(ds_scr)

        @pl.when((pl.program_id(0) == 0) & (pl.program_id(1) == 0))
        def _():
            donw_ref[...] = jnp.zeros_like(donw_ref)

        def rsum(x):
            return jnp.sum(x, axis=-1, keepdims=True)

        def step(ci, carry):
            c = nch - 1 - ci
            rows = pl.ds(pl.multiple_of(c * CHUNK, CHUNK), CHUNK)
            gt = g_ref[rows, :]
            gtt = gt.T
            live = [head(c, rows, gt, gtt, h) for h in range(C_HEADS)]
            while live:
                live = [g for g in live if next(g, False)]
            return carry

        def head(c, rows, gt, gtt, h):
            hs = slice(C_DH * h, C_DH * (h + 1))
            q, k, v = q_ref[rows, hs], k_ref[rows, hs], v_ref[rows, hs]
            zc, o, dy, u, w = zc_ref[rows, hs], o_ref[rows, hs], dy_ref[rows, hs], u_ref[rows, hs], w_ref[rows, hs]
            tm_ = t_ref[rows, hs][:, 0:CHUNK]
            sv, dsv = ss_ref[h, c], ds_scr[h]
            sb, dsb = sv.astype(BF16), dsv.astype(BF16)
            sg = _sigmoid(zc)
            r = lax.rsqrt(jnp.mean(o * o, axis=-1, keepdims=True) + EPS)
            on = o * r
            ow = onw_ref[...]
            dzc_ref[rows, hs] = (dy * on * ow * _dsilu(zc, sg)).astype(BF16)
            t1 = dy * zc * sg
            donw_ref[...] += _fold8(t1 * on)
            don = t1 * ow
            do = r * (don - on * jnp.mean(don * on, axis=-1, keepdims=True))
            dob = do.astype(BF16)
            yield True
            beta, gc, incl, strict, dec, kb, a, mq, glast = _chunk_common(q, k, gt, gtt, h)
            egc = jnp.exp(gc)
            gl = jnp.exp(glast)
            ekd = jnp.exp(glast - gc)
            wb = w.astype(BF16)
            vnb = (u - _dot(wb, sb)).astype(BF16)
            qg = q * egc
            yield True
            dvn = _dot_tn(mq.astype(BF16), dob) + _dot((k * ekd).astype(BF16), dsb)
            dvnb = dvn.astype(BF16)
            dqg = _dot_nt(dob, sb)
            yield True
            dmq = jnp.where(incl, _dot_nt(dob, vnb), 0.0)
            dkd = _dot_nt(vnb, dsb)
            dgl = jnp.sum(rsum(dsv * sv), axis=0, keepdims=True)
            dw = -_dot_nt(dvnb, sb)
            yield True
            ds_scr[h] = gl * dsv + _dot_tn(qg.astype(BF16), dob) - _dot_tn(wb, dvnb)
            db = _dot_tn(tm_, jnp.concatenate([dvn, dw], axis=1), HI)
            dbv, dbk = db[:, :C_DH], db[:, C_DH:]
            yield True
            da = -jnp.where(strict, _dot_nt(dbv, u, HI) + _dot_nt(dbk, w, HI), 0.0)
            yield True
            e = da * a + dmq * mq
            dgc = rsum(e) - rsum(e.T)
            dgb, dhb, kbf = (da * dec).astype(BF16), (dmq * dec).astype(BF16), k.astype(BF16)
            dkb = _dot(dgb, kbf)
            tk = rsum(dbk * k)
            rk = rsum(dkd * k) * ekd
            dq = _dot(dhb, kbf) + egc * dqg
            dk = _dot_tn(dgb, kb.astype(BF16)) + _dot_tn(dhb, q.astype(BF16)) + beta * (egc * dbk + dkb) + ekd * dkd
            dbeta = rsum(dbv * v) + tk * egc + rsum(dkb * k)
            dgc = dgc + tk * beta * egc + egc * rsum(dqg * q) - rk
            dglast = jnp.sum(rk, axis=0, keepdims=True) + dgl * gl
            dgc = dgc + jnp.where(_subl(dgc.shape) == CHUNK - 1, dglast, 0.0)
            dqkv_ref[0, rows, hs] = dq
            dqkv_ref[1, rows, hs] = dk
            dqkv_ref[2, rows, hs] = beta * dbv
            lane = _lane((CHUNK, LANES))
            dg_ref[h, rows, :] = jnp.where(lane == 0, dbeta, jnp.where(lane == 1, dgc, 0.0))

        lax.fori_loop(0, nch, step, 0)

    row = lambda b, i: b * nt + nt - 1 - i
    wb = lambda col: BS((tt, C_W), lambda b, i: (row(b, i), col))
    return pl.pallas_call(
        body, name="gdn_chunk_bwd", grid=(nseq, nt),
        in_specs=[wb(0), wb(1), wb(2), BS((tt, LANES), lambda b, i: (row(b, i), 0)), wb(P_ZC // C_W),
                  BS((1, LANES), lambda b, i: (0, 0)), wb(0), wb(Y_C // C_W), wb(0), wb(0), wb(0),
                  BS((None, C_HEADS, nch, C_DH, C_DH), lambda b, i: (b, 0, nt - 1 - i, 0, 0)), BS(memory_space=pl.ANY)],
        out_specs=[wb(P_ZC // C_W), BS((3, tt, C_W), lambda b, i: (0, row(b, i), 0)),
                   BS((C_HEADS, tt, LANES), lambda b, i: (0, row(b, i), 0)), BS((SUB, LANES), lambda b, i: (0, 0))],
        out_shape=[S(dp.shape, dp.dtype), S((3, n, C_W), F32), S((C_HEADS, n, LANES), F32), S((SUB, LANES), F32)],
        input_output_aliases={12: 0},
        scratch_shapes=[pltpu.VMEM((C_HEADS, C_DH, C_DH), F32)],
        compiler_params=_cp("arbitrary", "arbitrary"),
    )(qkv, qkv, qkv, gates, p, onw, o, dy, u, w, tinv, ss, dp)


def _gdn_gates_bwd(dgate, p, alog_l, dtb_l, dp, tm=256):
    n = p.shape[0]
    acc = BS((SUB, LANES), lambda i: (0, 0))

    def body(dg_ref, ba_ref, al_ref, db_ref, dp_in_ref, dba_ref, dal_ref, ddb_ref):
        @pl.when(pl.program_id(0) == 0)
        def _():
            dal_ref[...] = jnp.zeros_like(dal_ref)
            ddb_ref[...] = jnp.zeros_like(ddb_ref)

        blk = ba_ref[...]
        lane = _lane(blk.shape)
        dbeta = jnp.zeros_like(blk)
        dgc = jnp.zeros_like(blk)
        for h in range(C_HEADS):
            dbeta = dbeta + jnp.where(lane == GB + h, _col(dg_ref[h], 0), 0.0)
            dgc = dgc + jnp.where(lane == GG + h, _col(dg_ref[h], 1), 0.0)
        tri = (_subl((CHUNK, CHUNK)) <= _lane((CHUNK, CHUNK))).astype(F32)
        dg = jnp.concatenate([_dot(tri, dgc[CHUNK * c:CHUNK * (c + 1)], HI) for c in range(tm // CHUNK)], axis=0)
        beta = _sigmoid(blk)
        z = blk + db_ref[...]
        ea = jnp.exp(al_ref[...])
        isg = (lane >= GG) & (lane < GG + C_HEADS)
        dz = jnp.where(isg, -dg * ea * _sigmoid(z), 0.0)
        dal_ref[...] += _fold8(jnp.where(isg, -dg * ea * _softplus(z), 0.0))
        ddb_ref[...] += _fold8(dz)
        out = jnp.where(lane < GG, dbeta * beta * (1.0 - beta), dz)
        dba_ref[...] = jnp.concatenate([out, jnp.zeros_like(out)], axis=1).astype(BF16)

    return pl.pallas_call(
        body, name="gdn_gates_bwd", grid=(n // tm,),
        in_specs=[BS((C_HEADS, tm, LANES), lambda i: (0, i, 0)), BS((tm, LANES), lambda i: (i, P_BA // LANES)),
                  BS((1, LANES), lambda i: (0, 0)), BS((1, LANES), lambda i: (0, 0)), BS(memory_space=pl.ANY)],
        out_specs=[BS((tm, 2 * LANES), lambda i: (i, P_BA // (2 * LANES))), acc, acc],
        out_shape=[S(dp.shape, dp.dtype), S((SUB, LANES), F32), S((SUB, LANES), F32)],
        input_output_aliases={4: 0},
        compiler_params=_cp("arbitrary"),
    )(dgate, p, alog_l, dtb_l, dp)


def _gdn_pre_bwd(p, dqkv, xc, dp, ccw, nseq, tm=256):
    n = p.shape[0]
    t = n // nseq
    nt = t // tm
    wide = 3 * C_W
    row = lambda b, i: b * nt + i
    prev = lambda b, i: jnp.maximum((b * t + i * tm) // HALO_C - 1, 0)
    nxt = lambda b, i: jnp.minimum((b * t + (i + 1) * tm) // HALO_C, n // HALO_C - 1)

    def d_conv_out(d, xc, part):
        sg = _sigmoid(xc)
        act = xc * sg
        if part < 2:
            cs = QS if part == 0 else 1.0
            rn = lax.rsqrt(jnp.sum(act * act, axis=-1, keepdims=True) + EPS)
            d = cs * rn * d - act * (cs * rn * rn * rn * jnp.sum(d * act, axis=-1, keepdims=True))
        return d * _dsilu(xc, sg)

    def body(x_ref, xh_ref, d_ref, dn_ref, xc_ref, xn_ref, w_ref, dp_in_ref, dx_ref, dw_ref, buf_ref, dbuf_ref):
        i = pl.program_id(1)

        @pl.when((pl.program_id(0) == 0) & (i == 0))
        def _():
            dw_ref[...] = jnp.zeros_like(dw_ref)

        buf_ref[0:HALO_C, :] = jnp.where(i > 0, xh_ref[...], 0.0)
        buf_ref[HALO_C:, :] = x_ref[...]
        for c in range(NCB):
            cs = slice(LANES * c, LANES * (c + 1))
            part, hd = divmod(c, C_HEADS)
            hs = slice(LANES * hd, LANES * (hd + 1))
            d = d_conv_out(d_ref[part, :, hs], xc_ref[:, cs], part)
            dbuf_ref[0:tm, cs] = d
            dbuf_ref[tm:, cs] = jnp.where(i < nt - 1, d_conv_out(dn_ref[part, :, hs], xn_ref[:, cs], part), 0.0)
            dx = jnp.zeros((tm, LANES), F32)
            for k in range(C_K):
                dx = dx + w_ref[k:k + 1, cs] * dbuf_ref[pl.ds(C_K - 1 - k, tm), cs]
                dw_ref[SUB * k:SUB * (k + 1), cs] += _fold8(d * buf_ref[pl.ds(HALO_C - C_K + 1 + k, tm), cs])
            dx_ref[:, cs] = dx.astype(BF16)

    return pl.pallas_call(
        body, name="gdn_pre_bwd", grid=(nseq, nt),
        in_specs=[BS((tm, wide), lambda b, i: (row(b, i), P_QKV // wide)), BS((HALO_C, wide), lambda b, i: (prev(b, i), P_QKV // wide)),
                  BS((3, tm, C_W), lambda b, i: (0, row(b, i), 0)), BS((3, HALO_C, C_W), lambda b, i: (0, nxt(b, i), 0)),
                  BS((tm, wide), lambda b, i: (row(b, i), 0)), BS((HALO_C, wide), lambda b, i: (nxt(b, i), 0)),
                  BS((SUB, wide), lambda b, i: (0, 0)), BS(memory_space=pl.ANY)],
        out_specs=[BS((tm, wide), lambda b, i: (row(b, i), P_QKV // wide)), BS((SUB * C_K, wide), lambda b, i: (0, 0))],
        out_shape=[S(dp.shape, dp.dtype), S((SUB * C_K, wide), F32)], input_output_aliases={7: 0},
        scratch_shapes=[pltpu.VMEM((HALO_C + tm, wide), F32)] * 2,
        compiler_params=_cp("arbitrary", "arbitrary"),
    )(p, p, dqkv, dqkv, xc, xc, ccw, dp)


ANY = BS(memory_space=pl.ANY)


def _my_pos():
    return lax.axis_index("x"), lax.axis_index("y"), lax.axis_index("c")


def _dev_index(dev):
    return 4 * dev[0] + 2 * dev[1] + dev[2]


def _all_gather(shards):
    nk = len(shards)

    def body(*refs):
        ins, outs = refs[:nk], refs[nk:2 * nk]
        send, recv, loc = refs[2 * nk:]
        x, y, c = _my_pos()
        me, sib = (x, y, c), (x, y, 1 - c)
        chips = [(1 - x, y), (x, 1 - y), (1 - x, 1 - y)]

        def rows(t, dev):
            r = ins[t].shape[0]
            return outs[t].at[pl.ds(pl.multiple_of(_dev_index(dev) * r, SUB), r), :]

        def copy(t, k, block, to, src=None):
            return pltpu.make_async_remote_copy(
                src_ref=rows(t, block) if src is None else src, dst_ref=rows(t, block),
                send_sem=send.at[t, k], recv_sem=recv.at[t, k], device_id=to, device_id_type=MESH)

        mine = [pltpu.make_async_copy(ins[t], rows(t, me), loc.at[t]) for t in range(nk)]
        for cp in mine:
            cp.start()
        first = []
        for t in range(nk):
            first.append(copy(t, 0, me, sib, src=ins[t]))
            first += [copy(t, 1 + j, me, (*chip, c), src=ins[t]) for j, chip in enumerate(chips)]
        for cp in first:
            cp.start()
        passed = []
        for j, chip in enumerate(chips):
            for t in range(nk):
                copy(t, 1 + j, (*chip, c), me).wait_recv()
                cp = copy(t, 4 + j, (*chip, c), sib)
                cp.start()
                passed.append(cp)
        for t in range(nk):
            copy(t, 0, sib, me).wait_recv()
            for j, chip in enumerate(chips):
                copy(t, 4 + j, (*chip, 1 - c), me).wait_recv()
        for cp in first + passed:
            cp.wait_send()
        for cp in mine:
            cp.wait()

    return pl.pallas_call(
        body, name="all_gather", in_specs=[ANY] * nk, out_specs=[ANY] * nk,
        out_shape=[S((N_DEV * a.shape[0], a.shape[1]), a.dtype) for a in shards],
        scratch_shapes=[pltpu.SemaphoreType.DMA((nk, 7)), pltpu.SemaphoreType.DMA((nk, 7)), pltpu.SemaphoreType.DMA((nk,))],
    )(*shards)


def _scatter_blocks(parts):
    nk = len(parts)

    def body(*refs):
        ins, outs = refs[:nk], refs[nk:2 * nk]
        send, recv, loc = refs[2 * nk:]
        x, y, c = _my_pos()
        me = _dev_index((x, y, c))
        peers = [((1 - x) if k & 4 else x, (1 - y) if k & 2 else y, (1 - c) if k & 1 else c) for k in range(1, N_DEV)]

        def block(t, dev):
            r = ins[t].shape[0] // N_DEV
            return ins[t].at[pl.ds(pl.multiple_of(_dev_index(dev) * r, SUB), r), :]

        mine = [pltpu.make_async_copy(block(t, (x, y, c)), outs[t].at[me], loc.at[t]) for t in range(nk)]
        for cp in mine:
            cp.start()
        sent = []
        for t in range(nk):
            for k, peer in enumerate(peers):
                cp = pltpu.make_async_remote_copy(src_ref=block(t, peer), dst_ref=outs[t].at[me], send_sem=send.at[t, k],
                                                  recv_sem=recv.at[t, k], device_id=peer, device_id_type=MESH)
                cp.start()
                sent.append(cp)
        for t in range(nk):
            for k, peer in enumerate(peers):
                pltpu.make_async_remote_copy(src_ref=block(t, peer), dst_ref=outs[t].at[_dev_index(peer)], send_sem=send.at[t, k],
                                             recv_sem=recv.at[t, k], device_id=peer, device_id_type=MESH).wait_recv()
        for cp in sent:
            cp.wait_send()
        for cp in mine:
            cp.wait()

    return pl.pallas_call(
        body, name="scatter_blocks", in_specs=[ANY] * nk, out_specs=[ANY] * nk,
        out_shape=[S((N_DEV, a.shape[0] // N_DEV, a.shape[1]), a.dtype) for a in parts],
        scratch_shapes=[pltpu.SemaphoreType.DMA((nk, 7)), pltpu.SemaphoreType.DMA((nk, 7)), pltpu.SemaphoreType.DMA((nk,))],
    )(*parts)


SEM = BS(memory_space=pltpu.SEMAPHORE)
HBM = BS(memory_space=pltpu.HBM)
EFFECT = pltpu.SideEffectType.DATAFLOW_SIDE_EFFECTING


def _peers(x, y, c):
    return [((1 - x) if k & 4 else x, (1 - y) if k & 2 else y, (1 - c) if k & 1 else c) for k in range(1, N_DEV)]


def _exchange_copy(kind, src, land, send, recv, t, k, peer, me, arriving):
    frm = peer if arriving else me
    if kind == "gather":
        r = src.shape[0]
        s_ref = src
        d_ref = land.at[pl.ds(pl.multiple_of(_dev_index(frm) * r, SUB), r), :]
    else:
        r = src.shape[0] // N_DEV
        s_ref = src.at[pl.ds(pl.multiple_of(_dev_index(peer) * r, SUB), r), :]
        d_ref = land.at[_dev_index(frm)]
    sem = t * (N_DEV - 1) + k
    return pltpu.make_async_remote_copy(src_ref=s_ref, dst_ref=d_ref, send_sem=send.at[sem], recv_sem=recv.at[sem],
                                        device_id=peer, device_id_type=MESH)


def _exchange_start(kind, srcs, after, name):
    nk = len(srcs)
    if kind == "gather":
        lands = [lax.empty((N_DEV * a.shape[0], a.shape[1]), a.dtype) for a in srcs]
    else:
        lands = [lax.empty((N_DEV, a.shape[0] // N_DEV, a.shape[1]), a.dtype) for a in srcs]

    def body(*refs):
        src, land = refs[:nk], refs[nk:2 * nk]
        send, recv = refs[2 * nk + 1], refs[2 * nk + 2]
        token, loc = refs[-2], refs[-1]
        x, y, c = _my_pos()
        me = (x, y, c)
        for t in range(nk):
            if kind == "gather":
                r = src[t].shape[0]
                cp = pltpu.make_async_copy(src[t], land[t].at[pl.ds(pl.multiple_of(_dev_index(me) * r, SUB), r), :], loc)
            else:
                r = src[t].shape[0] // N_DEV
                cp = pltpu.make_async_copy(src[t].at[pl.ds(pl.multiple_of(_dev_index(me) * r, SUB), r), :],
                                           land[t].at[_dev_index(me)], loc)
            cp.start()
            cp.wait()
        for t in range(nk):
            for k, peer in enumerate(_peers(x, y, c)):
                _exchange_copy(kind, src[t], land[t], send, recv, t, k, peer, me, False).start()
        token[...] = jnp.zeros_like(token)

    hbm = lambda a: pltpu.HBM(a.shape, a.dtype)
    out = pl.pallas_call(
        body, name=name,
        out_shape=(pltpu.SemaphoreType.DMA((nk * (N_DEV - 1),)), pltpu.SemaphoreType.DMA((nk * (N_DEV - 1),)),
                   *[hbm(a) for a in srcs], *[hbm(a) for a in lands], S((SUB, LANES), F32)),
        in_specs=[HBM] * (2 * nk) + [ANY],
        out_specs=(SEM, SEM, *[HBM] * (2 * nk), BS(memory_space=pltpu.VMEM)),
        input_output_aliases={i: 2 + i for i in range(2 * nk)},
        scratch_shapes=[pltpu.SemaphoreType.DMA],
        compiler_params=pltpu.CompilerParams(has_side_effects=EFFECT),
    )(*[pltpu.with_memory_space_constraint(a, pltpu.HBM) for a in (*srcs, *lands)], after)
    return dict(kind=kind, nk=nk, send=out[0], recv=out[1], srcs=out[2:2 + nk], lands=out[2 + nk:2 + 2 * nk], token=out[-1])


def _exchange_wait(ex, after, name):
    kind, nk = ex["kind"], ex["nk"]

    def body(*refs):
        src, land = refs[:nk], refs[nk:2 * nk]
        send, recv = refs[2 * nk], refs[2 * nk + 1]
        x, y, c = _my_pos()
        me = (x, y, c)
        for t in range(nk):
            for k, peer in enumerate(_peers(x, y, c)):
                _exchange_copy(kind, src[t], land[t], send, recv, t, k, peer, me, False).wait_send()
                _exchange_copy(kind, src[t], land[t], send, recv, t, k, peer, me, True).wait_recv()

    hbm = lambda a: pltpu.HBM(a.shape, a.dtype)
    out = pl.pallas_call(
        body, name=name,
        out_shape=(*[hbm(a) for a in ex["srcs"]], *[hbm(a) for a in ex["lands"]]),
        in_specs=[HBM] * (2 * nk) + [SEM, SEM, ANY], out_specs=tuple([HBM] * (2 * nk)),
        input_output_aliases={i: i for i in range(2 * nk)},
        compiler_params=pltpu.CompilerParams(has_side_effects=EFFECT),
    )(*ex["srcs"], *ex["lands"], ex["send"], ex["recv"], after)
    return list(out[nk:])


BLOCK_BYTES = 4 << 20


def _row_tile(rows, row_bytes, align):
    best = align
    for tr in range(align, rows + 1, align):
        if rows % tr == 0 and tr * row_bytes <= BLOCK_BYTES:
            best = tr
    return best


def _sum8(a):
    _, r, w = a.shape
    tr = _row_tile(r, N_DEV * w * a.dtype.itemsize, 32 // a.dtype.itemsize)

    def body(a_ref, o_ref):
        acc = a_ref[0].astype(F32)
        for d in range(1, N_DEV):
            acc = acc + a_ref[d].astype(F32)
        o_ref[...] = acc

    return pl.pallas_call(
        body, name="sum8", grid=(r // tr,), in_specs=[BS((N_DEV, tr, w), lambda i: (0, i, 0))],
        out_specs=BS((tr, w), lambda i: (i, 0)), out_shape=S((r, w), F32), compiler_params=_cp("arbitrary"),
    )(a)


def _adamw(w, g, m, v):
    r, c = w.shape
    tr = _row_tile(r, c * 4 * 2, SUB)

    def body(w_ref, g_ref, m_ref, v_ref, d_ref, mo_ref, vo_ref):
        gv = g_ref[...]
        m2 = ADAM_B1 * m_ref[...] + (1.0 - ADAM_B1) * gv
        v2 = ADAM_B2 * v_ref[...] + (1.0 - ADAM_B2) * (gv * gv)
        m_hat = m2 / (1.0 - ADAM_B1 ** ADAM_STEP)
        v_hat = v2 / (1.0 - ADAM_B2 ** ADAM_STEP)
        d_ref[...] = -ADAM_LR * (m_hat / (jnp.sqrt(v_hat) + ADAM_EPS) + ADAM_WD * w_ref[...])
        mo_ref[...] = m2
        vo_ref[...] = v2

    blk = BS((tr, c), lambda i: (i, 0))
    return pl.pallas_call(
        body, name="adamw", grid=(r // tr,), in_specs=[blk] * 4, out_specs=[blk] * 3,
        out_shape=[S((r, c), F32)] * 3, compiler_params=_cp("arbitrary"),
    )(w, g, m, v)


def _blob(arrays):
    flat = jnp.concatenate([a.reshape(-1) for a in arrays])
    rows = -(-flat.shape[0] // (SUB * LANES)) * SUB
    return jnp.pad(flat, (0, rows * LANES - flat.shape[0])).reshape(rows, LANES)


def _unblob(blob, shapes, lead=()):
    flat = blob.reshape(lead + (-1,))
    out, off = [], 0
    for s in shapes:
        size = math.prod(s)
        out.append(flat[..., off:off + size].reshape(lead + tuple(s)))
        off += size
    return out


def _lanes6(a):
    return jnp.zeros((1, LANES), F32).at[0, GG:GG + C_HEADS].set(a)


def _y_rows(w):
    return jnp.concatenate([w[0:A_W], w[A_W + B_W:], w[A_W:A_W + B_W]], axis=0)


def _y_rows_back(g):
    return jnp.concatenate([g[0:A_W], g[A_W + C_W:], g[A_W:A_W + C_W]], axis=0)


SMALL = ("norm_w", "q_norm_w", "k_norm_w", "sinks", "b_conv_b", "b_ln_w", "b_ln_b", "b_pw_b", "c_a_log", "c_dt_bias",
         "c_onorm_w", "b_conv_w", "c_conv_w")
ORDER = ("norm_w", "w_in", "q_norm_w", "k_norm_w", "sinks", "b_conv_w", "b_conv_b", "b_ln_w", "b_ln_b", "b_pw_w", "b_pw_b",
         "c_conv_w", "c_a_log", "c_dt_bias", "c_onorm_w", "w_out")


def kernel(x, positions, norm_w, w_in, q_norm_w, k_norm_w, sinks, b_conv_w, b_conv_b, b_ln_w, b_ln_b, b_pw_w, b_pw_b, c_conv_w, c_a_log, c_dt_bias, c_onorm_w, w_out, loss_target, m_norm_w, m_w_in, m_q_norm_w, m_k_norm_w, m_sinks, m_b_conv_w, m_b_conv_b, m_b_ln_w, m_b_ln_b, m_b_pw_w, m_b_pw_b, m_c_conv_w, m_c_a_log, m_c_dt_bias, m_c_onorm_w, m_w_out, v_norm_w, v_w_in, v_q_norm_w, v_k_norm_w, v_sinks, v_b_conv_w, v_b_conv_b, v_b_ln_w, v_b_ln_b, v_b_pw_w, v_b_pw_b, v_c_conv_w, v_c_a_log, v_c_dt_bias, v_c_onorm_w, v_w_out):
    W = dict(norm_w=norm_w, w_in=w_in, q_norm_w=q_norm_w, k_norm_w=k_norm_w, sinks=sinks, b_conv_w=b_conv_w, b_conv_b=b_conv_b,
             b_ln_w=b_ln_w, b_ln_b=b_ln_b, b_pw_w=b_pw_w, b_pw_b=b_pw_b, c_conv_w=c_conv_w, c_a_log=c_a_log,
             c_dt_bias=c_dt_bias, c_onorm_w=c_onorm_w, w_out=w_out)
    M = dict(norm_w=m_norm_w, w_in=m_w_in, q_norm_w=m_q_norm_w, k_norm_w=m_k_norm_w, sinks=m_sinks, b_conv_w=m_b_conv_w,
             b_conv_b=m_b_conv_b, b_ln_w=m_b_ln_w, b_ln_b=m_b_ln_b, b_pw_w=m_b_pw_w, b_pw_b=m_b_pw_b, c_conv_w=m_c_conv_w,
             c_a_log=m_c_a_log, c_dt_bias=m_c_dt_bias, c_onorm_w=m_c_onorm_w, w_out=m_w_out)
    V = dict(norm_w=v_norm_w, w_in=v_w_in, q_norm_w=v_q_norm_w, k_norm_w=v_k_norm_w, sinks=v_sinks, b_conv_w=v_b_conv_w,
             b_conv_b=v_b_conv_b, b_ln_w=v_b_ln_w, b_ln_b=v_b_ln_b, b_pw_w=v_b_pw_w, b_pw_b=v_b_pw_b, c_conv_w=v_c_conv_w,
             c_a_log=v_c_a_log, c_dt_bias=v_c_dt_bias, c_onorm_w=v_c_onorm_w, w_out=v_w_out)
    nseq, t, d = x.shape
    n = nseq * t
    tr = min(256, t)
    tmm = min(512, n)
    tmw = min(1024, n)
    tkk = min(2048, n)
    me = _dev_index(_my_pos())
    xs = [x.reshape(n, d)]
    tgt = loss_target.reshape(n, d)
    tabs = _rope_tables(positions.reshape(n))

    win_p = _pack_cols(w_in).astype(BF16)
    wout_b = w_out.astype(BF16)
    sharded_small = (b_pw_w, b_conv_w, c_conv_w)
    g_win0, g_small = _all_gather([win_p[0], _blob(sharded_small)])
    win = [g_win0]
    later = _exchange_start("gather", [win_p[1], wout_b[0], wout_b[1]], g_small, "gather_start")
    pw_all, cw_all, ccw_all = _unblob(g_small, [a.shape for a in sharded_small], lead=(N_DEV,))
    pw_all = pw_all.transpose(1, 0, 2, 3).reshape(DEPTH, B_W, B_W).astype(BF16)
    cw_all = cw_all.transpose(1, 2, 0, 3).reshape(DEPTH, B_K, B_W)
    ccw_all = ccw_all.transpose(1, 2, 0, 3).reshape(DEPTH, C_K, 3 * C_W)

    def layer_params(l):
        return dict(
            nw=norm_w[l][None], qw=jnp.tile(q_norm_w[l], 2)[None], kw=jnp.tile(k_norm_w[l], 2)[None], sinks=sinks[l],
            cw=jnp.pad(cw_all[l], ((0, HALO_B - B_K), (0, 0))), cb=b_conv_b[l][None], lnw=b_ln_w[l][None], lnb=b_ln_b[l][None],
            pw=pw_all[l], pwb=b_pw_b[l][None], ccw=jnp.pad(ccw_all[l], ((0, SUB - C_K), (0, 0))),
            alog=_lanes6(c_a_log[l]), dtb=_lanes6(c_dt_bias[l]), onw=c_onorm_w[l][None])

    saved = []
    for l in range(DEPTH):
        q = layer_params(l)
        nw = q["nw"] + later["token"][0:1, 0:1] if l == 0 else q["nw"]
        p, h = _inproj(xs[l], nw, win[l], tm=tmw)
        y, o_a, lse = _attn_fwd(p, tabs, q["qw"], q["kw"], q["sinks"], nseq)
        gates = _gdn_gates_fwd(p, q["alog"], q["dtb"], tm=tr)
        xc, qkv = _gdn_pre_fwd(p, q["ccw"], nseq, tm=tr)
        y, o_c, u, w, tinv, ss = _gdn_chunk_fwd(qkv, gates, p, y, q["onw"], nseq)
        y, hc = _conf_fwd(p, y, q["cw"], q["cb"], q["lnw"], q["lnb"], q["pw"], q["pwb"], nseq, tm=tr)
        saved.append(dict(q=q, p=p, h=h, y=y, o_a=o_a, lse=lse, gates=gates, xc=xc, qkv=qkv, o_c=o_c, u=u, w=w, tinv=tinv,
                          ss=ss, hc=hc))
        if l == 0:
            g_win1, g_wout0, g_wout1 = _exchange_wait(later, y, "gather_wait")
            win.append(g_win1)
            wout = [_y_rows(g_wout0), _y_rows(g_wout1)]
        if l + 1 < DEPTH:
            xs.append(_outproj(xs[l], y, wout[l], tm=tmw, tn=512))
        else:
            dxn, lsum = _outproj_loss(xs[l], y, wout[l], tgt, tm=tmw, tn=512)
    loss = lax.psum(jnp.sum(lsum) * (0.5 / d), ("x", "y", "c"))

    sent, smalls = [None] * DEPTH, [None] * DEPTH
    for l in reversed(range(DEPTH)):
        s = saved[l]
        q, p = s["q"], s["p"]
        dy = _matmul(dxn, wout[l], "nt", F32, tmw, 512, d, "outproj_bwd_dy")
        dwout = _y_rows_back(_matmul(s["y"], dxn, "tn", BF16, 1024, 1024, tkk, "outproj_bwd_dw"))
        dp, dkv, dqw, dkw, dsk = _attn_bwd(p, dy, s["o_a"], s["lse"], tabs, q["qw"], q["kw"], q["sinks"], nseq)
        dp = _put_cols(dp, dkv, P_K, tm=tmm)
        dp, dqkv, dgate, donw = _gdn_chunk_bwd(s["qkv"], s["gates"], p, dy, dp, q["onw"], s["o_c"], s["u"], s["w"],
                                               s["tinv"], s["ss"], nseq)
        dp, dal, ddb = _gdn_gates_bwd(dgate, p, q["alog"], q["dtb"], dp, tm=tr)
        dp, dccw = _gdn_pre_bwd(p, dqkv, s["xc"], dp, q["ccw"], nseq, tm=tr)
        dp, dhc, dpw, dpwb, dlnw, dlnb, dcb = _conf_bwd1(p, dy, dp, s["hc"], q["lnw"], q["lnb"], q["pw"], q["pwb"], tm=tr)
        dp, dcw = _conf_bwd2(p, dhc, dp, q["cw"], nseq, tm=tr)
        dwin = _matmul(s["h"], dp, "tn", BF16, 1024, 768, tkk, "inproj_bwd_dw")
        sent[l] = _exchange_start("scatter", [dwin, dwout, dpw], dpwb, "scatter_start_%d" % l)
        dxn, dnw = _inproj_bwd_dx(dp, win[l], xs[l], q["nw"] + sent[l]["token"][0:1, 0:1], dxn, tm=tmm)
        halves = lambda a: a.sum(0)[:A_DH] + a.sum(0)[A_DH:]
        smalls[l] = dict(
            norm_w=dnw.sum(0), q_norm_w=halves(dqw), k_norm_w=halves(dkw), sinks=dsk.sum(0)[:A_HEADS], b_conv_b=dcb.sum(0),
            b_ln_w=dlnw.sum(0), b_ln_b=dlnb.sum(0), b_pw_b=dpwb.sum(0), c_a_log=dal.sum(0)[GG:GG + C_HEADS],
            c_dt_bias=ddb.sum(0)[GG:GG + C_HEADS], c_onorm_w=donw.sum(0),
            b_conv_w=dcw.reshape(B_K, SUB, B_W).sum(1), c_conv_w=dccw.reshape(C_K, SUB, 3 * C_W).sum(1))
    grad_x = dxn.reshape(nseq, t, d)

    r_win1, r_wout1, r_pw1 = _exchange_wait(sent[1], dxn, "scatter_wait_1")
    r_win0, r_wout0, r_pw0 = _exchange_wait(sent[0], r_pw1, "scatter_wait_0")
    G = {}
    G["w_in"] = jnp.stack([_unpack_cols(_sum8(r_win0)), _unpack_cols(_sum8(r_win1))])
    G["w_out"] = jnp.stack([_sum8(r_wout0), _sum8(r_wout1)])
    G["b_pw_w"] = jnp.stack([_sum8(r_pw0), _sum8(r_pw1)])
    part = _blob([jnp.stack([smalls[l][k] for l in range(DEPTH)]) for k in SMALL])
    (tot,) = _all_gather([part])
    tot = _sum8(tot.reshape(N_DEV, part.shape[0], LANES))
    full_shapes = [(DEPTH,) + smalls[0][k].shape for k in SMALL]
    for k, g in zip(SMALL, _unblob(tot, full_shapes)):
        G[k] = g
    G["b_conv_w"] = lax.dynamic_slice_in_dim(G["b_conv_w"], me * (B_W // N_DEV), B_W // N_DEV, axis=2)
    G["c_conv_w"] = lax.dynamic_slice_in_dim(G["c_conv_w"], me * (3 * C_W // N_DEV), 3 * C_W // N_DEV, axis=2)

    delta, new_m, new_v = {}, {}, {}
    for k in ("w_in", "w_out", "b_pw_w"):
        shp = W[k].shape
        two = lambda a: a.reshape(shp[0] * shp[1], shp[2])
        dl, mo, vo = _adamw(two(W[k]), two(G[k]), two(M[k]), two(V[k]))
        delta[k], new_m[k], new_v[k] = dl.reshape(shp), mo.reshape(shp), vo.reshape(shp)
    dl, mo, vo = _adamw(*[_blob([src[k] for k in SMALL]) for src in (W, G, M, V)])
    shapes = [W[k].shape for k in SMALL]
    for k, a, b, c in zip(SMALL, _unblob(dl, shapes), _unblob(mo, shapes), _unblob(vo, shapes)):
        delta[k], new_m[k], new_v[k] = a, b, c
    return (loss, grad_x, *[G[k] for k in ORDER], *[delta[k] for k in ORDER], *[new_m[k] for k in ORDER],
            *[new_v[k] for k in ORDER])
```

```python
import functools
import math

import jax
import jax.numpy as jnp
from jax import lax
from jax.experimental import pallas as pl
from jax.experimental.pallas import tpu as pltpu

F32 = jnp.float32
BF16 = jnp.bfloat16
HI = lax.Precision.HIGHEST
MESH = pl.DeviceIdType.MESH
S = jax.ShapeDtypeStruct
BS = pl.BlockSpec

N_DEV = 8
DEPTH = 2
D_MODEL = 2048
A_HEADS, A_KV, A_DH, A_W, A_KVW = 12, 4, 64, 768, 256
ROT = 16
THETA = 500000.0
ABLK = 128
B_W, B_K = 512, 31
C_HEADS, C_DH, C_W, C_K, CHUNK = 6, 128, 768, 4, 64
EPS = 1e-6
IN_COLS = 6668
P_Q, P_ZA, P_ZC, P_QKV, P_K, P_V, P_UB, P_ZB, P_BA, P_W = 0, 768, 1536, 2304, 4608, 4864, 5120, 6144, 6656, 6912
Y_A, Y_C, Y_B = 0, 768, 1536
LANES = 128
SUB = 8

ADAM_LR, ADAM_B1, ADAM_B2, ADAM_EPS, ADAM_WD, ADAM_STEP = 0.001, 0.9, 0.999, 1e-08, 0.01, 10


def _cp(*sem, vmem=None):
    kw = {}
    if sem:
        kw["dimension_semantics"] = sem
    if vmem:
        kw["vmem_limit_bytes"] = vmem
    return pltpu.CompilerParams(**kw)


def _pack_cols(w):
    z = jnp.zeros(w.shape[:-1] + (P_W - IN_COLS,), w.dtype)
    return jnp.concatenate([w[..., 0:768], w[..., 1280:2048], w[..., 5900:6668], w[..., 3584:5888],
                            w[..., 768:1024], w[..., 1024:1280], w[..., 2048:3072], w[..., 3072:3584],
                            w[..., 5888:5900], z], axis=-1)


def _unpack_cols(g):
    return jnp.concatenate([g[..., P_Q:P_Q + 768], g[..., P_K:P_K + 256], g[..., P_V:P_V + 256], g[..., P_ZA:P_ZA + 768],
                            g[..., P_UB:P_UB + 1024], g[..., P_ZB:P_ZB + 512], g[..., P_QKV:P_QKV + 2304],
                            g[..., P_BA:P_BA + 12], g[..., P_ZC:P_ZC + 768]], axis=-1)


def _sigmoid(x):
    return 1.0 / (1.0 + jnp.exp(-x))


def _dsilu(x, sg):
    return sg * (1.0 + x * (1.0 - sg))


def _fold8(x):
    r, c = x.shape
    return x.reshape(r // SUB, SUB, c).sum(axis=0)


def _dot(a, b, prec=None):
    return jnp.dot(a, b, preferred_element_type=F32, precision=prec)


def _dot_nt(a, b, prec=None):
    return lax.dot_general(a, b, (((1,), (1,)), ((), ())), preferred_element_type=F32, precision=prec)


def _dot_tn(a, b, prec=None):
    return lax.dot_general(a, b, (((0,), (0,)), ((), ())), preferred_element_type=F32, precision=prec)


def _lane(shape):
    return lax.broadcasted_iota(jnp.int32, shape, 1)


def _subl(shape):
    return lax.broadcasted_iota(jnp.int32, shape, 0)


def _col(x, j):
    return jnp.sum(jnp.where(_lane(x.shape) == j, x, 0.0), axis=-1, keepdims=True)


def _inproj(x, nw, w, tm=512, tn=768):
    n, d = x.shape
    pw = w.shape[1]

    def body(x_ref, nw_ref, w_ref, p_ref, h_ref):
        @pl.when(pl.program_id(1) == 0)
        def _():
            xv = x_ref[...]
            r = lax.rsqrt(jnp.mean(xv * xv, axis=-1, keepdims=True) + EPS)
            h_ref[...] = (xv * r * nw_ref[...]).astype(BF16)

        p_ref[...] = _dot(h_ref[...], w_ref[...])

    return pl.pallas_call(
        body, name="inproj", grid=(n // tm, pw // tn),
        in_specs=[BS((tm, d), lambda i, j: (i, 0)), BS((1, d), lambda i, j: (0, 0)), BS((d, tn), lambda i, j: (0, j))],
        out_specs=[BS((tm, tn), lambda i, j: (i, j)), BS((tm, d), lambda i, j: (i, 0))],
        out_shape=[S((n, pw), F32), S((n, d), BF16)],
        compiler_params=_cp("arbitrary", "arbitrary"),
    )(x, nw, w)


def _outproj(x, y, w, tm=512, tn=1024):
    n, d = x.shape
    k = y.shape[1]

    def body(x_ref, y_ref, w_ref, o_ref):
        o_ref[...] = x_ref[...] + _dot(y_ref[...], w_ref[...])

    return pl.pallas_call(
        body, name="outproj", grid=(n // tm, d // tn),
        in_specs=[BS((tm, tn), lambda i, j: (i, j)), BS((tm, k), lambda i, j: (i, 0)), BS((k, tn), lambda i, j: (0, j))],
        out_specs=BS((tm, tn), lambda i, j: (i, j)),
        out_shape=S((n, d), F32),
        compiler_params=_cp("arbitrary", "arbitrary"),
    )(x, y, w)


def _outproj_loss(x, y, w, tgt, tm=512, tn=1024):
    n, d = x.shape
    k = y.shape[1]

    def body(x_ref, y_ref, w_ref, t_ref, g_ref, l_ref):
        @pl.when((pl.program_id(0) == 0) & (pl.program_id(1) == 0))
        def _():
            l_ref[...] = jnp.zeros_like(l_ref)

        diff = x_ref[...] + _dot(y_ref[...], w_ref[...]) - t_ref[...]
        g_ref[...] = diff * (1.0 / d)
        f = _fold8(diff * diff)
        acc = f[:, 0:LANES]
        for c in range(1, tn // LANES):
            acc = acc + f[:, c * LANES:(c + 1) * LANES]
        l_ref[...] += acc

    return pl.pallas_call(
        body, name="outproj_loss", grid=(n // tm, d // tn),
        in_specs=[BS((tm, tn), lambda i, j: (i, j)), BS((tm, k), lambda i, j: (i, 0)), BS((k, tn), lambda i, j: (0, j)),
                  BS((tm, tn), lambda i, j: (i, j))],
        out_specs=[BS((tm, tn), lambda i, j: (i, j)), BS((SUB, LANES), lambda i, j: (0, 0))],
        out_shape=[S((n, d), F32), S((SUB, LANES), F32)],
        compiler_params=_cp("arbitrary", "arbitrary"),
    )(x, y, w, tgt)


def _matmul(a, b, mode, out_dtype, tm, tn, tk, name):
    if mode == "nn":
        (m, kk), nn = a.shape, b.shape[1]
        a_spec, b_spec = BS((tm, tk), lambda i, j, k: (i, k)), BS((tk, tn), lambda i, j, k: (k, j))
        dot = _dot
    elif mode == "nt":
        (m, kk), nn = a.shape, b.shape[0]
        a_spec, b_spec = BS((tm, tk), lambda i, j, k: (i, k)), BS((tn, tk), lambda i, j, k: (j, k))
        dot = _dot_nt
    else:
        (kk, m), nn = a.shape, b.shape[1]
        a_spec, b_spec = BS((tk, tm), lambda i, j, k: (k, i)), BS((tk, tn), lambda i, j, k: (k, j))
        dot = _dot_tn
    nk = kk // tk

    def body(a_ref, b_ref, o_ref, acc_ref):
        kid = pl.program_id(2)

        @pl.when(kid == 0)
        def _():
            acc_ref[...] = jnp.zeros_like(acc_ref)

        acc_ref[...] += dot(a_ref[...].astype(BF16), b_ref[...].astype(BF16))

        @pl.when(kid == nk - 1)
        def _():
            o_ref[...] = acc_ref[...].astype(out_dtype)

    return pl.pallas_call(
        body, name=name, grid=(m // tm, nn // tn, nk),
        in_specs=[a_spec, b_spec], out_specs=BS((tm, tn), lambda i, j, k: (i, j)),
        out_shape=S((m, nn), out_dtype), scratch_shapes=[pltpu.VMEM((tm, tn), F32)],
        compiler_params=_cp("arbitrary", "arbitrary", "arbitrary"),
    )(a, b)


SLAB = 16


def _inproj_bwd_dx(dp, w, x, nw, dres, tm=512, tk=1152):
    n, d = x.shape
    nk = dp.shape[1] // tk

    def body(dp_ref, w_ref, x_ref, nw_ref, dr_ref, dx_ref, dnw_ref, acc_ref):
        kid = pl.program_id(1)

        @pl.when((pl.program_id(0) == 0) & (kid == 0))
        def _():
            dnw_ref[...] = jnp.zeros_like(dnw_ref)

        @pl.when(kid == 0)
        def _():
            acc_ref[...] = jnp.zeros_like(acc_ref)

        acc_ref[...] += _dot_nt(dp_ref[...], w_ref[...])

        @pl.when(kid == nk - 1)
        def _():
            def slab(i, carry):
                rows = pl.ds(pl.multiple_of(i * SLAB, SLAB), SLAB)
                dh, xv = acc_ref[rows, :], x_ref[rows, :]
                r = lax.rsqrt(jnp.mean(xv * xv, axis=-1, keepdims=True) + EPS)
                dnw_ref[...] += _fold8(dh * xv * r)
                g = dh * nw_ref[...]
                mm = jnp.mean(g * xv, axis=-1, keepdims=True)
                dx_ref[rows, :] = dr_ref[rows, :] + r * g - xv * (r * r * r * mm)
                return carry

            lax.fori_loop(0, tm // SLAB, slab, 0)

    return pl.pallas_call(
        body, name="inproj_bwd_dx", grid=(n // tm, nk),
        in_specs=[BS((tm, tk), lambda i, k: (i, k)), BS((d, tk), lambda i, k: (0, k)), BS((tm, d), lambda i, k: (i, 0)),
                  BS((1, d), lambda i, k: (0, 0)), BS((tm, d), lambda i, k: (i, 0))],
        out_specs=[BS((tm, d), lambda i, k: (i, 0)), BS((SUB, d), lambda i, k: (0, 0))],
        out_shape=[S((n, d), F32), S((SUB, d), F32)],
        scratch_shapes=[pltpu.VMEM((tm, d), F32)],
        compiler_params=_cp("arbitrary", "arbitrary"),
    )(dp, w, x, nw, dres)


def _rope_tables(pos):
    half = ROT // 2
    inv = THETA ** (-jnp.arange(0, ROT, 2, dtype=F32) / ROT)
    ang = pos.astype(F32)[:, None] * inv
    cos, sin = jnp.cos(ang), jnp.sin(ang)
    n = pos.shape[0]
    one = jnp.ones((n, A_DH - ROT), F32)
    zero = jnp.zeros((n, A_DH - ROT), F32)
    zh = jnp.zeros((n, half), F32)
    c = jnp.concatenate([cos, cos, one], axis=1)
    s1 = jnp.concatenate([-sin, zh, zero], axis=1)
    s2 = jnp.concatenate([zh, sin, zero], axis=1)
    return tuple(jnp.concatenate([t, t], axis=1) for t in (c, s1, s2))


def _half_stat(t):
    lo = _lane(t.shape) < A_DH
    s_lo = jnp.sum(jnp.where(lo, t, 0.0), axis=-1, keepdims=True)
    s_hi = jnp.sum(jnp.where(lo, 0.0, t), axis=-1, keepdims=True)
    return jnp.where(lo, s_lo, s_hi)


def _normrope(x, w, c, s1, s2):
    r = lax.rsqrt(_half_stat(x * x) * (1.0 / A_DH) + EPS)
    xn = x * r * w
    return xn * c + pltpu.roll(xn, LANES - ROT // 2, 1) * s1 + pltpu.roll(xn, ROT // 2, 1) * s2, r


def _normrope_bwd(dy, x, r, w, c, s1, s2):
    dxn = dy * c + pltpu.roll(dy * s1, ROT // 2, 1) + pltpu.roll(dy * s2, LANES - ROT // 2, 1)
    g = dxn * w
    mm = _half_stat(g * x) * (1.0 / A_DH)
    return r * g - x * (r * r * r * mm), dxn * x * r


def _attn_mask(first):
    qi = _subl((ABLK, 2 * ABLK))
    kj = _lane((ABLK, 2 * ABLK))
    dist = qi + ABLK - kj
    return (dist >= 0) & (dist < ABLK) & (jnp.logical_not(first) | (kj >= ABLK))


def _attn_fwd(p, tabs, qw, kw, sinks, nseq):
    n = p.shape[0]
    nb = n // nseq // ABLK
    cur = lambda b, i: (b * nb + i, 0)
    prv = lambda b, i: (b * nb + jnp.maximum(i - 1, 0), 0)
    colblk = lambda f, w, off: (lambda b, i: (f(b, i)[0], off // w))

    def body(q_ref, za_ref, kc_ref, vc_ref, kp_ref, vp_ref, c_ref, s1_ref, s2_ref, cp_ref, s1p_ref, s2p_ref,
             qw_ref, kw_ref, sink_ref, y_ref, o_ref, lse_ref):
        first = pl.program_id(1) == 0
        tc = (c_ref[...], s1_ref[...], s2_ref[...])
        tp = (cp_ref[...], s1p_ref[...], s2p_ref[...])
        q, kc, kp = q_ref[...], kc_ref[...], kp_ref[...]
        qn = [_normrope(q[:, LANES * b:LANES * (b + 1)], qw_ref[...], *tc)[0].astype(BF16) for b in range(A_W // LANES)]
        k2, v2 = [], []
        for b in range(A_KVW // LANES):
            sl = slice(LANES * b, LANES * (b + 1))
            k2.append(jnp.concatenate([_normrope(kp[:, sl], kw_ref[...], *tp)[0],
                                       _normrope(kc[:, sl], kw_ref[...], *tc)[0]], axis=0).astype(BF16))
            v2.append(jnp.concatenate([vp_ref[:, sl], vc_ref[:, sl]], axis=0).astype(BF16))
        valid = _attn_mask(first)
        heads = range(A_HEADS)
        half = lambda a, i: a[:, A_DH * (i % 2):A_DH * (i % 2 + 1)]
        kh = [half(k2[g // 2], g) for g in range(A_KV)]
        vh = [half(v2[g // 2], g) for g in range(A_KV)]
        s = [jnp.where(valid, _dot_nt(half(qn[j // 2], j), kh[j // 3]) * (A_DH ** -0.5), -jnp.inf) for j in heads]
        m = [jnp.maximum(jnp.max(s[j], axis=-1, keepdims=True), sink_ref[j]) for j in heads]
        e = [jnp.exp(s[j] - m[j]) for j in heads]
        den = [jnp.sum(e[j], axis=-1, keepdims=True) + jnp.exp(sink_ref[j] - m[j]) for j in heads]
        outs = [_dot((e[j] / den[j]).astype(BF16), vh[j // 3]) for j in heads]
        lse = jnp.zeros((ABLK, LANES), F32)
        for j in heads:
            lse = jnp.where(_lane(lse.shape) == j, m[j] + jnp.log(den[j]), lse)
        o = jnp.concatenate(outs, axis=1)
        za = za_ref[...]
        o_ref[...] = o
        lse_ref[...] = lse
        y_ref[...] = (o * (za * _sigmoid(za))).astype(BF16)

    tab_specs = [BS((ABLK, LANES), cur)] * 3 + [BS((ABLK, LANES), prv)] * 3
    return pl.pallas_call(
        body, name="attn_fwd", grid=(nseq, nb),
        in_specs=[BS((ABLK, A_W), colblk(cur, A_W, P_Q)), BS((ABLK, A_W), colblk(cur, A_W, P_ZA)),
                  BS((ABLK, A_KVW), colblk(cur, A_KVW, P_K)), BS((ABLK, A_KVW), colblk(cur, A_KVW, P_V)),
                  BS((ABLK, A_KVW), colblk(prv, A_KVW, P_K)), BS((ABLK, A_KVW), colblk(prv, A_KVW, P_V))]
        + tab_specs + [BS((1, LANES), lambda b, i: (0, 0))] * 2 + [BS(memory_space=pltpu.SMEM)],
        out_specs=[BS((ABLK, A_W), colblk(cur, A_W, Y_A)), BS((ABLK, A_W), cur), BS((ABLK, LANES), cur)],
        out_shape=[S((n, D_MODEL), BF16), S((n, A_W), F32), S((n, LANES), F32)],
        compiler_params=_cp("arbitrary", "arbitrary"),
    )(p, p, p, p, p, p, *tabs, *tabs, qw, kw, sinks)


def _attn_bwd(p, dy, o, lse, tabs, qw, kw, sinks, nseq):
    n = p.shape[0]
    nb = n // nseq // ABLK
    cur = lambda b, i: (b * nb + jnp.minimum(i, nb - 1), 0)
    prv = lambda b, i: (b * nb + jnp.maximum(i - 1, 0), 0)
    colblk = lambda f, w, off: (lambda b, i: (f(b, i)[0], off // w))

    def body(q_ref, za_ref, kc_ref, vc_ref, kp_ref, vp_ref, dy_ref, o_ref, lse_ref,
             c_ref, s1_ref, s2_ref, cp_ref, s1p_ref, s2p_ref, qw_ref, kw_ref, sink_ref,
             dqza_ref, dkv_ref, dqw_ref, dkw_ref, dsk_ref, tk_ref, tv_ref, ck_ref, cv_ref):
        i = pl.program_id(1)
        first = i == 0
        tc = (c_ref[...], s1_ref[...], s2_ref[...])
        tp = (cp_ref[...], s1p_ref[...], s2p_ref[...])
        nkb = A_KVW // LANES

        @pl.when((pl.program_id(0) == 0) & first)
        def _():
            dqw_ref[...] = jnp.zeros_like(dqw_ref)
            dkw_ref[...] = jnp.zeros_like(dkw_ref)
            dsk_ref[...] = jnp.zeros_like(dsk_ref)

        @pl.when(i < nb)
        def _():
            q, kc, kp = q_ref[...], kc_ref[...], kp_ref[...]
            qn, rq = [], []
            for b in range(A_W // LANES):
                a, r = _normrope(q[:, LANES * b:LANES * (b + 1)], qw_ref[...], *tc)
                qn.append(a.astype(BF16))
                rq.append(r)
            k2, v2 = [], []
            for b in range(nkb):
                sl = slice(LANES * b, LANES * (b + 1))
                k2.append(jnp.concatenate([_normrope(kp[:, sl], kw_ref[...], *tp)[0],
                                           _normrope(kc[:, sl], kw_ref[...], *tc)[0]], axis=0).astype(BF16))
                v2.append(jnp.concatenate([vp_ref[:, sl], vc_ref[:, sl]], axis=0).astype(BF16))
            valid = _attn_mask(first)
            za, dy, o, lse = za_ref[...], dy_ref[...], o_ref[...], lse_ref[...]
            sg = _sigmoid(za)
            do = dy * za * sg
            dqza_ref[:, A_W:2 * A_W] = (dy * o * _dsilu(za, sg)).astype(BF16)
            heads = range(A_HEADS)
            half = lambda a, i: a[:, A_DH * (i % 2):A_DH * (i % 2 + 1)]
            kh = [half(k2[g // 2], g) for g in range(A_KV)]
            vh = [half(v2[g // 2], g) for g in range(A_KV)]
            qh = [half(qn[j // 2], j) for j in heads]
            lj = [_col(lse, j) for j in heads]
            pr = [jnp.exp(jnp.where(valid, _dot_nt(qh[j], kh[j // 3]) * (A_DH ** -0.5), -jnp.inf) - lj[j]) for j in heads]
            doh = [do[:, A_DH * j:A_DH * (j + 1)] for j in heads]
            delta = [jnp.sum(doh[j] * o[:, A_DH * j:A_DH * (j + 1)], axis=-1, keepdims=True) for j in heads]
            dohb = [a.astype(BF16) for a in doh]
            ds = [(pr[j] * (_dot_nt(dohb[j], vh[j // 3]) - delta[j]) * (A_DH ** -0.5)).astype(BF16) for j in heads]
            dqs = [_dot(ds[j], kh[j // 3]) for j in heads]
            dkh = [_dot_tn(ds[j], qh[j]) for j in heads]
            dvh = [_dot_tn(pr[j].astype(BF16), dohb[j]) for j in heads]
            dks = [dkh[3 * g] + dkh[3 * g + 1] + dkh[3 * g + 2] for g in range(A_KV)]
            dvs = [dvh[3 * g] + dvh[3 * g + 1] + dvh[3 * g + 2] for g in range(A_KV)]
            dsk = jnp.zeros((ABLK, LANES), F32)
            for j in heads:
                dsk = dsk + jnp.where(_lane(dsk.shape) == j, -jnp.exp(sink_ref[j] - lj[j]) * delta[j], 0.0)
            dsk_ref[...] += _fold8(dsk)
            dqn = jnp.concatenate(dqs, axis=1)
            dqw = jnp.zeros((SUB, LANES), F32)
            dqo = []
            for b in range(A_W // LANES):
                sl = slice(LANES * b, LANES * (b + 1))
                dx, wt = _normrope_bwd(dqn[:, sl], q[:, sl], rq[b], qw_ref[...], *tc)
                dqo.append(dx)
                dqw = dqw + _fold8(wt)
            dqw_ref[...] += dqw
            dqza_ref[:, 0:A_W] = jnp.concatenate(dqo, axis=1).astype(BF16)
            tk_ref[...] = jnp.concatenate(dks, axis=1)
            tv_ref[...] = jnp.concatenate(dvs, axis=1)

        @pl.when(i == nb)
        def _():
            tk_ref[...] = jnp.zeros_like(tk_ref)
            tv_ref[...] = jnp.zeros_like(tv_ref)

        @pl.when(i > 0)
        def _():
            kp = kp_ref[...]
            dkn = ck_ref[...] + tk_ref[0:ABLK, :]
            dkw = jnp.zeros((SUB, LANES), F32)
            dko = []
            for b in range(nkb):
                sl = slice(LANES * b, LANES * (b + 1))
                r = _normrope(kp[:, sl], kw_ref[...], *tp)[1]
                dx, wt = _normrope_bwd(dkn[:, sl], kp[:, sl], r, kw_ref[...], *tp)
                dko.append(dx)
                dkw = dkw + _fold8(wt)
            dkw_ref[...] += dkw
            dkv_ref[:, 0:A_KVW] = jnp.concatenate(dko, axis=1).astype(BF16)
            dkv_ref[:, A_KVW:2 * A_KVW] = (cv_ref[...] + tv_ref[0:ABLK, :]).astype(BF16)

        ck_ref[...] = tk_ref[ABLK:2 * ABLK, :]
        cv_ref[...] = tv_ref[ABLK:2 * ABLK, :]

    tab_specs = [BS((ABLK, LANES), cur)] * 3 + [BS((ABLK, LANES), prv)] * 3
    acc = BS((SUB, LANES), lambda b, i: (0, 0))
    return pl.pallas_call(
        body, name="attn_bwd", grid=(nseq, nb + 1),
        in_specs=[BS((ABLK, A_W), colblk(cur, A_W, P_Q)), BS((ABLK, A_W), colblk(cur, A_W, P_ZA)),
                  BS((ABLK, A_KVW), colblk(cur, A_KVW, P_K)), BS((ABLK, A_KVW), colblk(cur, A_KVW, P_V)),
                  BS((ABLK, A_KVW), colblk(prv, A_KVW, P_K)), BS((ABLK, A_KVW), colblk(prv, A_KVW, P_V)),
                  BS((ABLK, A_W), colblk(cur, A_W, Y_A)), BS((ABLK, A_W), cur), BS((ABLK, LANES), cur)]
        + tab_specs + [BS((1, LANES), lambda b, i: (0, 0))] * 2 + [BS(memory_space=pltpu.SMEM)],
        out_specs=[BS((ABLK, 2 * A_W), cur), BS((ABLK, 2 * A_KVW), prv), acc, acc, acc],
        out_shape=[S((n, P_W), BF16), S((n, 2 * A_KVW), BF16)] + [S((SUB, LANES), F32)] * 3,
        scratch_shapes=[pltpu.VMEM((2 * ABLK, A_KVW), F32)] * 2 + [pltpu.VMEM((ABLK, A_KVW), F32)] * 2,
        compiler_params=_cp("arbitrary", "arbitrary"),
    )(p, p, p, p, p, p, dy, o, lse, *tabs, *tabs, qw, kw, sinks)


def _put_cols(dst, src, col_off, tm=512):
    n, w = src.shape

    def body(s_ref, d_in_ref, d_ref):
        d_ref[...] = s_ref[...]

    return pl.pallas_call(
        body, name="put_cols", grid=(n // tm,),
        in_specs=[BS((tm, w), lambda i: (i, 0)), BS(memory_space=pl.ANY)],
        out_specs=BS((tm, w), lambda i: (i, col_off // w)),
        out_shape=S(dst.shape, dst.dtype), input_output_aliases={1: 0},
        compiler_params=_cp("arbitrary"),
    )(src, dst)


HALO_B = 32


def _layernorm(hc, lnw, lnb):
    mu = jnp.mean(hc, axis=-1, keepdims=True)
    xc = hc - mu
    rstd = lax.rsqrt(jnp.mean(xc * xc, axis=-1, keepdims=True) + EPS)
    xhat = xc * rstd
    return xhat, rstd, xhat * lnw + lnb


def _shifted_copies(buf_ref, sh_ref):
    rows = sh_ref.shape[1]
    for b in range(1, SUB):
        sh_ref[b - 1] = buf_ref[pl.ds(b, rows), :]


def _rows_from(buf_ref, sh_ref, off, rows):
    a, b = divmod(off, SUB)
    if b == 0:
        return buf_ref[pl.ds(SUB * a, rows), :]
    return sh_ref[b - 1, pl.ds(SUB * a, rows), :]


def _conf_fwd(p, y, cw, cb, lnw, lnb, pw, pwb, nseq, tm=256):
    n = p.shape[0]
    t = n // nseq
    nt = t // tm
    row = lambda b, i: b * nt + i
    halo = lambda b, i: jnp.maximum((b * t + i * tm) // HALO_B - 1, 0)
    vec = BS((1, B_W), lambda b, i: (0, 0))

    def body(ub_ref, uh_ref, zb_ref, cw_ref, cb_ref, lnw_ref, lnb_ref, pw_ref, pwb_ref, y_in_ref, y_ref, hc_ref, buf_ref, sh_ref):
        ub, uh = ub_ref[...], uh_ref[...]
        hh = uh[:, :B_W] * _sigmoid(uh[:, B_W:])
        buf_ref[0:HALO_B, :] = jnp.where(pl.program_id(1) > 0, hh, 0.0)
        buf_ref[HALO_B:, :] = ub[:, :B_W] * _sigmoid(ub[:, B_W:])
        _shifted_copies(buf_ref, sh_ref)
        hc = jnp.zeros((tm, B_W), F32) + cb_ref[...]
        for k in range(B_K):
            hc = hc + cw_ref[k:k + 1, :] * _rows_from(buf_ref, sh_ref, HALO_B - B_K + 1 + k, tm)
        hc_ref[...] = hc
        ln = _layernorm(hc, lnw_ref[...], lnb_ref[...])[2]
        sw = ln * _sigmoid(ln)
        ob = _dot(sw.astype(BF16), pw_ref[...]) + pwb_ref[...]
        zb = zb_ref[...]
        y_ref[...] = (ob * (zb * _sigmoid(zb))).astype(BF16)

    return pl.pallas_call(
        body, name="conf_fwd", grid=(nseq, nt),
        in_specs=[BS((tm, 2 * B_W), lambda b, i: (row(b, i), P_UB // (2 * B_W))),
                  BS((HALO_B, 2 * B_W), lambda b, i: (halo(b, i), P_UB // (2 * B_W))),
                  BS((tm, B_W), lambda b, i: (row(b, i), P_ZB // B_W)),
                  BS((HALO_B, B_W), lambda b, i: (0, 0)), vec, vec, vec, BS((B_W, B_W), lambda b, i: (0, 0)), vec,
                  BS(memory_space=pl.ANY)],
        out_specs=[BS((tm, B_W), lambda b, i: (row(b, i), Y_B // B_W)), BS((tm, B_W), lambda b, i: (row(b, i), 0))],
        out_shape=[S(y.shape, y.dtype), S((n, B_W), F32)], input_output_aliases={9: 0},
        scratch_shapes=[pltpu.VMEM((HALO_B + tm, B_W), F32), pltpu.VMEM((SUB - 1, HALO_B + tm - SUB, B_W), F32)],
        compiler_params=_cp("arbitrary", "arbitrary"),
    )(p, p, p, cw, cb, lnw, lnb, pw, pwb, y)


def _conf_bwd1(p, dy, dp, hc, lnw, lnb, pw, pwb, tm=256):
    n = p.shape[0]
    vec = BS((1, B_W), lambda i: (0, 0))
    acc = BS((SUB, B_W), lambda i: (0, 0))

    def body(dy_ref, zb_ref, hc_ref, lnw_ref, lnb_ref, pw_ref, pwb_ref, dp_in_ref,
             dzb_ref, dhc_ref, dpw_ref, dpwb_ref, dlnw_ref, dlnb_ref, dcb_ref):
        @pl.when(pl.program_id(0) == 0)
        def _():
            for r in (dpw_ref, dpwb_ref, dlnw_ref, dlnb_ref, dcb_ref):
                r[...] = jnp.zeros_like(r)

        xhat, rstd, ln = _layernorm(hc_ref[...], lnw_ref[...], lnb_ref[...])
        sgl = _sigmoid(ln)
        sw = (ln * sgl).astype(BF16)
        ob = _dot(sw, pw_ref[...]) + pwb_ref[...]
        dy, zb = dy_ref[...], zb_ref[...]
        sgz = _sigmoid(zb)
        dzb_ref[...] = (dy * ob * _dsilu(zb, sgz)).astype(BF16)
        dob = dy * zb * sgz
        dobb = dob.astype(BF16)
        dpwb_ref[...] += _fold8(dob)
        dpw_ref[...] += _dot_tn(sw, dobb)
        dln = _dot_nt(dobb, pw_ref[...]) * _dsilu(ln, sgl)
        dlnw_ref[...] += _fold8(dln * xhat)
        dlnb_ref[...] += _fold8(dln)
        dxh = dln * lnw_ref[...]
        dhc = rstd * (dxh - jnp.mean(dxh, axis=-1, keepdims=True) - xhat * jnp.mean(dxh * xhat, axis=-1, keepdims=True))
        dcb_ref[...] += _fold8(dhc)
        dhc_ref[...] = dhc

    return pl.pallas_call(
        body, name="conf_bwd1", grid=(n // tm,),
        in_specs=[BS((tm, B_W), lambda i: (i, Y_B // B_W)), BS((tm, B_W), lambda i: (i, P_ZB // B_W)),
                  BS((tm, B_W), lambda i: (i, 0)), vec, vec, BS((B_W, B_W), lambda i: (0, 0)), vec,
                  BS(memory_space=pl.ANY)],
        out_specs=[BS((tm, B_W), lambda i: (i, P_ZB // B_W)), BS((tm, B_W), lambda i: (i, 0)),
                   BS((B_W, B_W), lambda i: (0, 0)), acc, acc, acc, acc],
        out_shape=[S(dp.shape, dp.dtype), S((n, B_W), F32), S((B_W, B_W), F32)] + [S((SUB, B_W), F32)] * 4,
        input_output_aliases={7: 0},
        compiler_params=_cp("arbitrary"),
    )(dy, p, hc, lnw, lnb, pw, pwb, dp)


def _conf_bwd2(p, dhc, dp, cw, nseq, tm=256):
    n = p.shape[0]
    t = n // nseq
    nt = t // tm
    row = lambda b, i: b * nt + i
    prev = lambda b, i: jnp.maximum((b * t + i * tm) // HALO_B - 1, 0)
    nxt = lambda b, i: jnp.minimum((b * t + (i + 1) * tm) // HALO_B, n // HALO_B - 1)

    def body(ub_ref, uh_ref, dh_ref, dn_ref, cw_ref, dp_in_ref, dub_ref, dcw_ref, buf_ref, dbuf_ref, sh_ref, dsh_ref):
        i = pl.program_id(1)

        @pl.when((pl.program_id(0) == 0) & (i == 0))
        def _():
            dcw_ref[...] = jnp.zeros_like(dcw_ref)

        ub, uh = ub_ref[...], uh_ref[...]
        a, sg = ub[:, :B_W], _sigmoid(ub[:, B_W:])
        buf_ref[0:HALO_B, :] = jnp.where(i > 0, uh[:, :B_W] * _sigmoid(uh[:, B_W:]), 0.0)
        buf_ref[HALO_B:, :] = a * sg
        dhc = dh_ref[...]
        dbuf_ref[0:tm, :] = dhc
        dbuf_ref[tm:, :] = jnp.where(i < nt - 1, dn_ref[...], 0.0)
        _shifted_copies(buf_ref, sh_ref)
        _shifted_copies(dbuf_ref, dsh_ref)
        dhg = jnp.zeros((tm, B_W), F32)
        for k in range(B_K):
            dhg = dhg + cw_ref[k:k + 1, :] * _rows_from(dbuf_ref, dsh_ref, B_K - 1 - k, tm)
            dcw_ref[SUB * k:SUB * (k + 1), :] += _fold8(dhc * _rows_from(buf_ref, sh_ref, HALO_B - B_K + 1 + k, tm))
        dub_ref[...] = jnp.concatenate([dhg * sg, dhg * a * sg * (1.0 - sg)], axis=1).astype(BF16)

    return pl.pallas_call(
        body, name="conf_bwd2", grid=(nseq, nt),
        in_specs=[BS((tm, 2 * B_W), lambda b, i: (row(b, i), P_UB // (2 * B_W))),
                  BS((HALO_B, 2 * B_W), lambda b, i: (prev(b, i), P_UB // (2 * B_W))),
                  BS((tm, B_W), lambda b, i: (row(b, i), 0)), BS((HALO_B, B_W), lambda b, i: (nxt(b, i), 0)),
                  BS((HALO_B, B_W), lambda b, i: (0, 0)), BS(memory_space=pl.ANY)],
        out_specs=[BS((tm, 2 * B_W), lambda b, i: (row(b, i), P_UB // (2 * B_W))),
                   BS((SUB * B_K, B_W), lambda b, i: (0, 0))],
        out_shape=[S(dp.shape, dp.dtype), S((SUB * B_K, B_W), F32)], input_output_aliases={5: 0},
        scratch_shapes=[pltpu.VMEM((HALO_B + tm, B_W), F32)] * 2 + [pltpu.VMEM((SUB - 1, HALO_B + tm - SUB, B_W), F32)] * 2,
        compiler_params=_cp("arbitrary", "arbitrary"),
    )(p, p, dhc, dhc, cw, dp)


HALO_C = 8
QS = C_DH ** -0.5
NCB = 3 * C_HEADS
CB0 = P_QKV // LANES
ZC0 = P_ZC // LANES
GB, GG = 0, C_HEADS


def _softplus(z):
    return jnp.maximum(z, 0.0) + jnp.log(1.0 + jnp.exp(-jnp.abs(z)))


def _gdn_gates_fwd(p, alog_l, dtb_l, tm=256):
    n = p.shape[0]

    def body(ba_ref, al_ref, db_ref, o_ref):
        blk = ba_ref[...]
        lane = _lane(blk.shape)
        g = jnp.where((lane >= GG) & (lane < GG + C_HEADS), -jnp.exp(al_ref[...]) * _softplus(blk + db_ref[...]), 0.0)
        tri = (_subl((CHUNK, CHUNK)) >= _lane((CHUNK, CHUNK))).astype(F32)
        gc = jnp.concatenate([_dot(tri, g[CHUNK * c:CHUNK * (c + 1)], HI) for c in range(tm // CHUNK)], axis=0)
        o_ref[...] = jnp.where(lane < GG, _sigmoid(blk), gc)

    return pl.pallas_call(
        body, name="gdn_gates_fwd", grid=(n // tm,),
        in_specs=[BS((tm, LANES), lambda i: (i, P_BA // LANES)), BS((1, LANES), lambda i: (0, 0)), BS((1, LANES), lambda i: (0, 0))],
        out_specs=BS((tm, LANES), lambda i: (i, 0)), out_shape=S((n, LANES), F32),
        compiler_params=_cp("arbitrary"),
    )(p, alog_l, dtb_l)


def _gdn_pre_fwd(p, ccw, nseq, tm=256):
    n = p.shape[0]
    t = n // nseq
    nt = t // tm
    row = lambda b, i: b * nt + i
    halo = lambda b, i: jnp.maximum((b * t + i * tm) // HALO_C - 1, 0)

    def body(x_ref, xh_ref, w_ref, xc_ref, o_ref, buf_ref):
        buf_ref[0:HALO_C, :] = jnp.where(pl.program_id(1) > 0, xh_ref[...], 0.0)
        buf_ref[HALO_C:, :] = x_ref[...]
        for c in range(NCB):
            cs = slice(LANES * c, LANES * (c + 1))
            xc = jnp.zeros((tm, LANES), F32)
            for k in range(C_K):
                xc = xc + w_ref[k:k + 1, cs] * buf_ref[pl.ds(HALO_C - C_K + 1 + k, tm), cs]
            xc_ref[:, cs] = xc
            act = xc * _sigmoid(xc)
            if c < 2 * C_HEADS:
                act = act * (lax.rsqrt(jnp.sum(act * act, axis=-1, keepdims=True) + EPS) * (QS if c < C_HEADS else 1.0))
            o_ref[:, cs] = act

    wide = 3 * C_W
    return pl.pallas_call(
        body, name="gdn_pre_fwd", grid=(nseq, nt),
        in_specs=[BS((tm, wide), lambda b, i: (row(b, i), P_QKV // wide)), BS((HALO_C, wide), lambda b, i: (halo(b, i), P_QKV // wide)),
                  BS((SUB, wide), lambda b, i: (0, 0))],
        out_specs=[BS((tm, wide), lambda b, i: (row(b, i), 0))] * 2,
        out_shape=[S((n, wide), F32)] * 2,
        scratch_shapes=[pltpu.VMEM((HALO_C + tm, wide), F32)],
        compiler_params=_cp("arbitrary", "arbitrary"),
    )(p, p, ccw)


def _chunk_common(q, k, gt, gtt, h):
    beta = _col(gt, GB + h)
    gc = _col(gt, GG + h)
    gcr = gtt[GG + h:GG + h + 1, :]
    ii, jj = _subl((CHUNK, CHUNK)), _lane((CHUNK, CHUNK))
    incl, strict = ii >= jj, ii > jj
    dec = jnp.exp(jnp.where(incl, gc - gcr, -jnp.inf))
    kb = k * beta
    kbf = k.astype(BF16)
    a = jnp.where(strict, _dot_nt(kb.astype(BF16), kbf) * dec, 0.0)
    mq = jnp.where(incl, _dot_nt(q.astype(BF16), kbf) * dec, 0.0)
    glast = jnp.sum(jnp.where(_subl(gc.shape) == CHUNK - 1, gc, 0.0), axis=0, keepdims=True)
    return beta, gc, incl, strict, dec, kb, a, mq, glast


def _unit_lower_inverses(mats):
    eye = (_subl(mats[0].shape) == _lane(mats[0].shape)).astype(F32)
    ms = [-a for a in mats]
    invs = [eye + m for m in ms]
    for _ in range(5):
        ms = [_dot(m, m, HI) for m in ms]
        invs = [inv + _dot(inv, m, HI) for inv, m in zip(invs, ms)]
    return invs


def _gdn_chunk_fwd(qkv, gates, p, y, onw, nseq, tt=512):
    n = qkv.shape[0]
    t = n // nseq
    tt = min(tt, t)
    nt = t // tt
    nch = tt // CHUNK

    def body(q_ref, k_ref, v_ref, g_ref, zc_ref, onw_ref, y_in_ref, y_ref, o_ref, u_ref, w_ref, t_ref, ss_ref, s_scr):
        @pl.when(pl.program_id(1) == 0)
        def _():
            s_scr[...] = jnp.zeros_like(s_scr)

        def step(c, carry):
            rows = pl.ds(pl.multiple_of(c * CHUNK, CHUNK), CHUNK)
            gt = g_ref[rows, :]
            gtt = gt.T
            heads = range(C_HEADS)
            hs = [slice(C_DH * h, C_DH * (h + 1)) for h in heads]
            q, k, v = ([r[rows, hs[h]] for h in heads] for r in (q_ref, k_ref, v_ref))
            cm = [_chunk_common(q[h], k[h], gt, gtt, h) for h in heads]
            beta, gc, kb, mq, glast = ([m[i] for m in cm] for i in (0, 1, 5, 7, 8))
            tinv = _unit_lower_inverses([m[6] for m in cm])
            egc = [jnp.exp(g) for g in gc]
            sol = [_dot(tinv[h], jnp.concatenate([v[h] * beta[h], kb[h] * egc[h]], axis=1), HI) for h in heads]
            sv = [s_scr[h] for h in heads]
            sb = [s.astype(BF16) for s in sv]
            vnb = [(sol[h][:, :C_DH] - _dot(sol[h][:, C_DH:].astype(BF16), sb[h])).astype(BF16) for h in heads]
            o = [_dot((q[h] * egc[h]).astype(BF16), sb[h]) + _dot(mq[h].astype(BF16), vnb[h]) for h in heads]
            for h in heads:
                ss_ref[h, c] = sv[h]
                s_scr[h] = sv[h] * jnp.exp(glast[h]) + _dot_tn((k[h] * jnp.exp(glast[h] - gc[h])).astype(BF16), vnb[h])
            for h in heads:
                o_ref[rows, hs[h]] = o[h]
                u_ref[rows, hs[h]] = sol[h][:, :C_DH]
                w_ref[rows, hs[h]] = sol[h][:, C_DH:]
                t_ref[rows, hs[h]] = jnp.concatenate([tinv[h], jnp.zeros_like(tinv[h])], axis=1)
                zc = zc_ref[rows, hs[h]]
                r = lax.rsqrt(jnp.mean(o[h] * o[h], axis=-1, keepdims=True) + EPS)
                y_ref[rows, hs[h]] = (o[h] * r * onw_ref[...] * (zc * _sigmoid(zc))).astype(BF16)
            return carry

        lax.fori_loop(0, nch, step, 0)

    row = lambda b, i: b * nt + i
    wb = lambda col: BS((tt, C_W), lambda b, i: (row(b, i), col))
    return pl.pallas_call(
        body, name="gdn_chunk_fwd", grid=(nseq, nt),
        in_specs=[wb(0), wb(1), wb(2), BS((tt, LANES), lambda b, i: (row(b, i), 0)), wb(P_ZC // C_W),
                  BS((1, LANES), lambda b, i: (0, 0)), BS(memory_space=pl.ANY)],
        out_specs=[wb(Y_C // C_W), wb(0), wb(0), wb(0), wb(0),
                   BS((None, C_HEADS, nch, C_DH, C_DH), lambda b, i: (b, 0, i, 0, 0))],
        out_shape=[S(y.shape, y.dtype)] + [S((n, C_W), F32)] * 4 + [S((nseq, C_HEADS, t // CHUNK, C_DH, C_DH), F32)],
        input_output_aliases={6: 0},
        scratch_shapes=[pltpu.VMEM((C_HEADS, C_DH, C_DH), F32)],
        compiler_params=_cp("arbitrary", "arbitrary"),
    )(qkv, qkv, qkv, gates, p, onw, y)


def _gdn_chunk_bwd(qkv, gates, p, dy, dp, onw, o, u, w, tinv, ss, nseq, tt=256):
    n = qkv.shape[0]
    t = n // nseq
    tt = min(tt, t)
    nt = t // tt
    nch = tt // CHUNK

    def body(q_ref, k_ref, v_ref, g_ref, zc_ref, onw_ref, o_ref, dy_ref, u_ref, w_ref, t_ref, ss_ref, dp_in_ref,
             dzc_ref, dqkv_ref, dg_ref, donw_ref, ds_scr):
        @pl.when(pl.program_id(1) == 0)
        def _():
            ds_scr[...] = jnp.zeros_like(ds_scr)

        @pl.when((pl.program_id(0) == 0) & (pl.program_id(1) == 0))
        def _():
            donw_ref[...] = jnp.zeros_like(donw_ref)

        def rsum(x):
            return jnp.sum(x, axis=-1, keepdims=True)

        def step(ci, carry):
            c = nch - 1 - ci
            rows = pl.ds(pl.multiple_of(c * CHUNK, CHUNK), CHUNK)
            gt = g_ref[rows, :]
            gtt = gt.T
            live = [head(c, rows, gt, gtt, h) for h in range(C_HEADS)]
            while live:
                live = [g for g in live if next(g, False)]
            return carry

        def head(c, rows, gt, gtt, h):
            hs = slice(C_DH * h, C_DH * (h + 1))
            q, k, v = q_ref[rows, hs], k_ref[rows, hs], v_ref[rows, hs]
            zc, o, dy, u, w = zc_ref[rows, hs], o_ref[rows, hs], dy_ref[rows, hs], u_ref[rows, hs], w_ref[rows, hs]
            tm_ = t_ref[rows, hs][:, 0:CHUNK]
            sv, dsv = ss_ref[h, c], ds_scr[h]
            sb, dsb = sv.astype(BF16), dsv.astype(BF16)
            sg = _sigmoid(zc)
            r = lax.rsqrt(jnp.mean(o * o, axis=-1, keepdims=True) + EPS)
            on = o * r
            ow = onw_ref[...]
            dzc_ref[rows, hs] = (dy * on * ow * _dsilu(zc, sg)).astype(BF16)
            t1 = dy * zc * sg
            donw_ref[...] += _fold8(t1 * on)
            don = t1 * ow
            do = r * (don - on * jnp.mean(don * on, axis=-1, keepdims=True))
            dob = do.astype(BF16)
            yield True
            beta, gc, incl, strict, dec, kb, a, mq, glast = _chunk_common(q, k, gt, gtt, h)
            egc = jnp.exp(gc)
            gl = jnp.exp(glast)
            ekd = jnp.exp(glast - gc)
            wb = w.astype(BF16)
            vnb = (u - _dot(wb, sb)).astype(BF16)
            qg = q * egc
            yield True
            dvn = _dot_tn(mq.astype(BF16), dob) + _dot((k * ekd).astype(BF16), dsb)
            dvnb = dvn.astype(BF16)
            dqg = _dot_nt(dob, sb)
            yield True
            dmq = jnp.where(incl, _dot_nt(dob, vnb), 0.0)
            dkd = _dot_nt(vnb, dsb)
            dgl = jnp.sum(rsum(dsv * sv), axis=0, keepdims=True)
            dw = -_dot_nt(dvnb, sb)
            yield True
            ds_scr[h] = gl * dsv + _dot_tn(qg.astype(BF16), dob) - _dot_tn(wb, dvnb)
            db = _dot_tn(tm_, jnp.concatenate([dvn, dw], axis=1), HI)
            dbv, dbk = db[:, :C_DH], db[:, C_DH:]
            yield True
            da = -jnp.where(strict, _dot_nt(dbv, u, HI) + _dot_nt(dbk, w, HI), 0.0)
            yield True
            e = da * a + dmq * mq
            dgc = rsum(e) - rsum(e.T)
            dgb, dhb, kbf = (da * dec).astype(BF16), (dmq * dec).astype(BF16), k.astype(BF16)
            dkb = _dot(dgb, kbf)
            tk = rsum(dbk * k)
            rk = rsum(dkd * k) * ekd
            dq = _dot(dhb, kbf) + egc * dqg
            dk = _dot_tn(dgb, kb.astype(BF16)) + _dot_tn(dhb, q.astype(BF16)) + beta * (egc * dbk + dkb) + ekd * dkd
            dbeta = rsum(dbv * v) + tk * egc + rsum(dkb * k)
            dgc = dgc + tk * beta * egc + egc * rsum(dqg * q) - rk
            dglast = jnp.sum(rk, axis=0, keepdims=True) + dgl * gl
            dgc = dgc + jnp.where(_subl(dgc.shape) == CHUNK - 1, dglast, 0.0)
            dqkv_ref[0, rows, hs] = dq
            dqkv_ref[1, rows, hs] = dk
            dqkv_ref[2, rows, hs] = beta * dbv
            lane = _lane((CHUNK, LANES))
            dg_ref[h, rows, :] = jnp.where(lane == 0, dbeta, jnp.where(lane == 1, dgc, 0.0))

        lax.fori_loop(0, nch, step, 0)

    row = lambda b, i: b * nt + nt - 1 - i
    wb = lambda col: BS((tt, C_W), lambda b, i: (row(b, i), col))
    return pl.pallas_call(
        body, name="gdn_chunk_bwd", grid=(nseq, nt),
        in_specs=[wb(0), wb(1), wb(2), BS((tt, LANES), lambda b, i: (row(b, i), 0)), wb(P_ZC // C_W),
                  BS((1, LANES), lambda b, i: (0, 0)), wb(0), wb(Y_C // C_W), wb(0), wb(0), wb(0),
                  BS((None, C_HEADS, nch, C_DH, C_DH), lambda b, i: (b, 0, nt - 1 - i, 0, 0)), BS(memory_space=pl.ANY)],
        out_specs=[wb(P_ZC // C_W), BS((3, tt, C_W), lambda b, i: (0, row(b, i), 0)),
                   BS((C_HEADS, tt, LANES), lambda b, i: (0, row(b, i), 0)), BS((SUB, LANES), lambda b, i: (0, 0))],
        out_shape=[S(dp.shape, dp.dtype), S((3, n, C_W), F32), S((C_HEADS, n, LANES), F32), S((SUB, LANES), F32)],
        input_output_aliases={12: 0},
        scratch_shapes=[pltpu.VMEM((C_HEADS, C_DH, C_DH), F32)],
        compiler_params=_cp("arbitrary", "arbitrary"),
    )(qkv, qkv, qkv, gates, p, onw, o, dy, u, w, tinv, ss, dp)


def _gdn_gates_bwd(dgate, p, alog_l, dtb_l, dp, tm=256):
    n = p.shape[0]
    acc = BS((SUB, LANES), lambda i: (0, 0))

    def body(dg_ref, ba_ref, al_ref, db_ref, dp_in_ref, dba_ref, dal_ref, ddb_ref):
        @pl.when(pl.program_id(0) == 0)
        def _():
            dal_ref[...] = jnp.zeros_like(dal_ref)
            ddb_ref[...] = jnp.zeros_like(ddb_ref)

        blk = ba_ref[...]
        lane = _lane(blk.shape)
        dbeta = jnp.zeros_like(blk)
        dgc = jnp.zeros_like(blk)
        for h in range(C_HEADS):
            dbeta = dbeta + jnp.where(lane == GB + h, _col(dg_ref[h], 0), 0.0)
            dgc = dgc + jnp.where(lane == GG + h, _col(dg_ref[h], 1), 0.0)
        tri = (_subl((CHUNK, CHUNK)) <= _lane((CHUNK, CHUNK))).astype(F32)
        dg = jnp.concatenate([_dot(tri, dgc[CHUNK * c:CHUNK * (c + 1)], HI) for c in range(tm // CHUNK)], axis=0)
        beta = _sigmoid(blk)
        z = blk + db_ref[...]
        ea = jnp.exp(al_ref[...])
        isg = (lane >= GG) & (lane < GG + C_HEADS)
        dz = jnp.where(isg, -dg * ea * _sigmoid(z), 0.0)
        dal_ref[...] += _fold8(jnp.where(isg, -dg * ea * _softplus(z), 0.0))
        ddb_ref[...] += _fold8(dz)
        out = jnp.where(lane < GG, dbeta * beta * (1.0 - beta), dz)
        dba_ref[...] = jnp.concatenate([out, jnp.zeros_like(out)], axis=1).astype(BF16)

    return pl.pallas_call(
        body, name="gdn_gates_bwd", grid=(n // tm,),
        in_specs=[BS((C_HEADS, tm, LANES), lambda i: (0, i, 0)), BS((tm, LANES), lambda i: (i, P_BA // LANES)),
                  BS((1, LANES), lambda i: (0, 0)), BS((1, LANES), lambda i: (0, 0)), BS(memory_space=pl.ANY)],
        out_specs=[BS((tm, 2 * LANES), lambda i: (i, P_BA // (2 * LANES))), acc, acc],
        out_shape=[S(dp.shape, dp.dtype), S((SUB, LANES), F32), S((SUB, LANES), F32)],
        input_output_aliases={4: 0},
        compiler_params=_cp("arbitrary"),
    )(dgate, p, alog_l, dtb_l, dp)


def _gdn_pre_bwd(p, dqkv, xc, dp, ccw, nseq, tm=256):
    n = p.shape[0]
    t = n // nseq
    nt = t // tm
    wide = 3 * C_W
    row = lambda b, i: b * nt + i
    prev = lambda b, i: jnp.maximum((b * t + i * tm) // HALO_C - 1, 0)
    nxt = lambda b, i: jnp.minimum((b * t + (i + 1) * tm) // HALO_C, n // HALO_C - 1)

    def d_conv_out(d, xc, part):
        sg = _sigmoid(xc)
        act = xc * sg
        if part < 2:
            cs = QS if part == 0 else 1.0
            rn = lax.rsqrt(jnp.sum(act * act, axis=-1, keepdims=True) + EPS)
            d = cs * rn * d - act * (cs * rn * rn * rn * jnp.sum(d * act, axis=-1, keepdims=True))
        return d * _dsilu(xc, sg)

    def body(x_ref, xh_ref, d_ref, dn_ref, xc_ref, xn_ref, w_ref, dp_in_ref, dx_ref, dw_ref, buf_ref, dbuf_ref):
        i = pl.program_id(1)

        @pl.when((pl.program_id(0) == 0) & (i == 0))
        def _():
            dw_ref[...] = jnp.zeros_like(dw_ref)

        buf_ref[0:HALO_C, :] = jnp.where(i > 0, xh_ref[...], 0.0)
        buf_ref[HALO_C:, :] = x_ref[...]
        for c in range(NCB):
            cs = slice(LANES * c, LANES * (c + 1))
            part, hd = divmod(c, C_HEADS)
            hs = slice(LANES * hd, LANES * (hd + 1))
            d = d_conv_out(d_ref[part, :, hs], xc_ref[:, cs], part)
            dbuf_ref[0:tm, cs] = d
            dbuf_ref[tm:, cs] = jnp.where(i < nt - 1, d_conv_out(dn_ref[part, :, hs], xn_ref[:, cs], part), 0.0)
            dx = jnp.zeros((tm, LANES), F32)
            for k in range(C_K):
                dx = dx + w_ref[k:k + 1, cs] * dbuf_ref[pl.ds(C_K - 1 - k, tm), cs]
                dw_ref[SUB * k:SUB * (k + 1), cs] += _fold8(d * buf_ref[pl.ds(HALO_C - C_K + 1 + k, tm), cs])
            dx_ref[:, cs] = dx.astype(BF16)

    return pl.pallas_call(
        body, name="gdn_pre_bwd", grid=(nseq, nt),
        in_specs=[BS((tm, wide), lambda b, i: (row(b, i), P_QKV // wide)), BS((HALO_C, wide), lambda b, i: (prev(b, i), P_QKV // wide)),
                  BS((3, tm, C_W), lambda b, i: (0, row(b, i), 0)), BS((3, HALO_C, C_W), lambda b, i: (0, nxt(b, i), 0)),
                  BS((tm, wide), lambda b, i: (row(b, i), 0)), BS((HALO_C, wide), lambda b, i: (nxt(b, i), 0)),
                  BS((SUB, wide), lambda b, i: (0, 0)), BS(memory_space=pl.ANY)],
        out_specs=[BS((tm, wide), lambda b, i: (row(b, i), P_QKV // wide)), BS((SUB * C_K, wide), lambda b, i: (0, 0))],
        out_shape=[S(dp.shape, dp.dtype), S((SUB * C_K, wide), F32)], input_output_aliases={7: 0},
        scratch_shapes=[pltpu.VMEM((HALO_C + tm, wide), F32)] * 2,
        compiler_params=_cp("arbitrary", "arbitrary"),
    )(p, p, dqkv, dqkv, xc, xc, ccw, dp)


ANY = BS(memory_space=pl.ANY)


def _my_pos():
    return lax.axis_index("x"), lax.axis_index("y"), lax.axis_index("c")


def _dev_index(dev):
    return 4 * dev[0] + 2 * dev[1] + dev[2]


def _all_gather(shards):
    nk = len(shards)

    def body(*refs):
        ins, outs = refs[:nk], refs[nk:2 * nk]
        send, recv, loc = refs[2 * nk:]
        x, y, c = _my_pos()
        me, sib = (x, y, c), (x, y, 1 - c)
        chips = [(1 - x, y), (x, 1 - y), (1 - x, 1 - y)]

        def rows(t, dev):
            r = ins[t].shape[0]
            return outs[t].at[pl.ds(pl.multiple_of(_dev_index(dev) * r, SUB), r), :]

        def copy(t, k, block, to, src=None):
            return pltpu.make_async_remote_copy(
                src_ref=rows(t, block) if src is None else src, dst_ref=rows(t, block),
                send_sem=send.at[t, k], recv_sem=recv.at[t, k], device_id=to, device_id_type=MESH)

        mine = [pltpu.make_async_copy(ins[t], rows(t, me), loc.at[t]) for t in range(nk)]
        for cp in mine:
            cp.start()
        first = []
        for t in range(nk):
            first.append(copy(t, 0, me, sib, src=ins[t]))
            first += [copy(t, 1 + j, me, (*chip, c), src=ins[t]) for j, chip in enumerate(chips)]
        for cp in first:
            cp.start()
        passed = []
        for j, chip in enumerate(chips):
            for t in range(nk):
                copy(t, 1 + j, (*chip, c), me).wait_recv()
                cp = copy(t, 4 + j, (*chip, c), sib)
                cp.start()
                passed.append(cp)
        for t in range(nk):
            copy(t, 0, sib, me).wait_recv()
            for j, chip in enumerate(chips):
                copy(t, 4 + j, (*chip, 1 - c), me).wait_recv()
        for cp in first + passed:
            cp.wait_send()
        for cp in mine:
            cp.wait()

    return pl.pallas_call(
        body, name="all_gather", in_specs=[ANY] * nk, out_specs=[ANY] * nk,
        out_shape=[S((N_DEV * a.shape[0], a.shape[1]), a.dtype) for a in shards],
        scratch_shapes=[pltpu.SemaphoreType.DMA((nk, 7)), pltpu.SemaphoreType.DMA((nk, 7)), pltpu.SemaphoreType.DMA((nk,))],
    )(*shards)


def _scatter_blocks(parts):
    nk = len(parts)

    def body(*refs):
        ins, outs = refs[:nk], refs[nk:2 * nk]
        send, recv, loc = refs[2 * nk:]
        x, y, c = _my_pos()
        me = _dev_index((x, y, c))
        peers = [((1 - x) if k & 4 else x, (1 - y) if k & 2 else y, (1 - c) if k & 1 else c) for k in range(1, N_DEV)]

        def block(t, dev):
            r = ins[t].shape[0] // N_DEV
            return ins[t].at[pl.ds(pl.multiple_of(_dev_index(dev) * r, SUB), r), :]

        mine = [pltpu.make_async_copy(block(t, (x, y, c)), outs[t].at[me], loc.at[t]) for t in range(nk)]
        for cp in mine:
            cp.start()
        sent = []
        for t in range(nk):
            for k, peer in enumerate(peers):
                cp = pltpu.make_async_remote_copy(src_ref=block(t, peer), dst_ref=outs[t].at[me], send_sem=send.at[t, k],
                                                  recv_sem=recv.at[t, k], device_id=peer, device_id_type=MESH)
                cp.start()
                sent.append(cp)
        for t in range(nk):
            for k, peer in enumerate(peers):
                pltpu.make_async_remote_copy(src_ref=block(t, peer), dst_ref=outs[t].at[_dev_index(peer)], send_sem=send.at[t, k],
                                             recv_sem=recv.at[t, k], device_id=peer, device_id_type=MESH).wait_recv()
        for cp in sent:
            cp.wait_send()
        for cp in mine:
            cp.wait()

    return pl.pallas_call(
        body, name="scatter_blocks", in_specs=[ANY] * nk, out_specs=[ANY] * nk,
        out_shape=[S((N_DEV, a.shape[0] // N_DEV, a.shape[1]), a.dtype) for a in parts],
        scratch_shapes=[pltpu.SemaphoreType.DMA((nk, 7)), pltpu.SemaphoreType.DMA((nk, 7)), pltpu.SemaphoreType.DMA((nk,))],
    )(*parts)


SEM = BS(memory_space=pltpu.SEMAPHORE)
HBM = BS(memory_space=pltpu.HBM)
EFFECT = pltpu.SideEffectType.DATAFLOW_SIDE_EFFECTING


def _peers(x, y, c):
    return [((1 - x) if k & 4 else x, (1 - y) if k & 2 else y, (1 - c) if k & 1 else c) for k in range(1, N_DEV)]


def _exchange_copy(kind, src, land, send, recv, t, k, peer, me, arriving):
    frm = peer if arriving else me
    if kind == "gather":
        r = src.shape[0]
        s_ref = src
        d_ref = land.at[pl.ds(pl.multiple_of(_dev_index(frm) * r, SUB), r), :]
    else:
        r = src.shape[0] // N_DEV
        s_ref = src.at[pl.ds(pl.multiple_of(_dev_index(peer) * r, SUB), r), :]
        d_ref = land.at[_dev_index(frm)]
    sem = t * (N_DEV - 1) + k
    return pltpu.make_async_remote_copy(src_ref=s_ref, dst_ref=d_ref, send_sem=send.at[sem], recv_sem=recv.at[sem],
                                        device_id=peer, device_id_type=MESH)


def _own_copy(kind, src, land, own, t, me):
    if kind == "gather":
        r = src.shape[0]
        return pltpu.make_async_copy(src, land.at[pl.ds(pl.multiple_of(_dev_index(me) * r, SUB), r), :], own.at[t])
    r = src.shape[0] // N_DEV
    return pltpu.make_async_copy(src.at[pl.ds(pl.multiple_of(_dev_index(me) * r, SUB), r), :], land.at[_dev_index(me)], own.at[t])


def _exchange_start(kind, srcs, after, name):
    nk = len(srcs)
    if kind == "gather":
        lands = [lax.empty((N_DEV * a.shape[0], a.shape[1]), a.dtype) for a in srcs]
    else:
        lands = [lax.empty((N_DEV, a.shape[0] // N_DEV, a.shape[1]), a.dtype) for a in srcs]

    def body(*refs):
        src, land = refs[:nk], refs[nk:2 * nk]
        send, recv, own = refs[2 * nk + 1], refs[2 * nk + 2], refs[2 * nk + 3]
        token = refs[-1]
        x, y, c = _my_pos()
        me = (x, y, c)
        for t in range(nk):
            _own_copy(kind, src[t], land[t], own, t, me).start()
            for k, peer in enumerate(_peers(x, y, c)):
                _exchange_copy(kind, src[t], land[t], send, recv, t, k, peer, me, False).start()
        token[...] = jnp.zeros_like(token)

    hbm = lambda a: pltpu.HBM(a.shape, a.dtype)
    out = pl.pallas_call(
        body, name=name,
        out_shape=(pltpu.SemaphoreType.DMA((nk * (N_DEV - 1),)), pltpu.SemaphoreType.DMA((nk * (N_DEV - 1),)),
                   pltpu.SemaphoreType.DMA((nk,)), *[hbm(a) for a in srcs], *[hbm(a) for a in lands], S((SUB, LANES), F32)),
        in_specs=[HBM] * (2 * nk) + [ANY],
        out_specs=(SEM, SEM, SEM, *[HBM] * (2 * nk), BS(memory_space=pltpu.VMEM)),
        input_output_aliases={i: 3 + i for i in range(2 * nk)},
        compiler_params=pltpu.CompilerParams(has_side_effects=EFFECT),
    )(*[pltpu.with_memory_space_constraint(a, pltpu.HBM) for a in (*srcs, *lands)], after)
    return dict(kind=kind, nk=nk, send=out[0], recv=out[1], own=out[2], srcs=out[3:3 + nk], lands=out[3 + nk:3 + 2 * nk],
                token=out[-1])


def _exchange_wait(ex, after, name):
    kind, nk = ex["kind"], ex["nk"]

    def body(*refs):
        src, land = refs[:nk], refs[nk:2 * nk]
        send, recv, own = refs[2 * nk], refs[2 * nk + 1], refs[2 * nk + 2]
        x, y, c = _my_pos()
        me = (x, y, c)
        for t in range(nk):
            _own_copy(kind, src[t], land[t], own, t, me).wait()
            for k, peer in enumerate(_peers(x, y, c)):
                _exchange_copy(kind, src[t], land[t], send, recv, t, k, peer, me, False).wait_send()
                _exchange_copy(kind, src[t], land[t], send, recv, t, k, peer, me, True).wait_recv()

    hbm = lambda a: pltpu.HBM(a.shape, a.dtype)
    out = pl.pallas_call(
        body, name=name,
        out_shape=(*[hbm(a) for a in ex["srcs"]], *[hbm(a) for a in ex["lands"]]),
        in_specs=[HBM] * (2 * nk) + [SEM, SEM, SEM, ANY], out_specs=tuple([HBM] * (2 * nk)),
        input_output_aliases={i: i for i in range(2 * nk)},
        compiler_params=pltpu.CompilerParams(has_side_effects=EFFECT),
    )(*ex["srcs"], *ex["lands"], ex["send"], ex["recv"], ex["own"], after)
    return list(out[nk:])


BLOCK_BYTES = 4 << 20


def _row_tile(rows, row_bytes, align):
    best = align
    for tr in range(align, rows + 1, align):
        if rows % tr == 0 and tr * row_bytes <= BLOCK_BYTES:
            best = tr
    return best


def _sum8(a):
    _, r, w = a.shape
    tr = _row_tile(r, N_DEV * w * a.dtype.itemsize, 32 // a.dtype.itemsize)

    def body(a_ref, o_ref):
        acc = a_ref[0].astype(F32)
        for d in range(1, N_DEV):
            acc = acc + a_ref[d].astype(F32)
        o_ref[...] = acc

    return pl.pallas_call(
        body, name="sum8", grid=(r // tr,), in_specs=[BS((N_DEV, tr, w), lambda i: (0, i, 0))],
        out_specs=BS((tr, w), lambda i: (i, 0)), out_shape=S((r, w), F32), compiler_params=_cp("arbitrary"),
    )(a)


def _adamw(w, g, m, v):
    r, c = w.shape
    tr = _row_tile(r, c * 4 * 2, SUB)

    def body(w_ref, g_ref, m_ref, v_ref, d_ref, mo_ref, vo_ref):
        gv = g_ref[...]
        m2 = ADAM_B1 * m_ref[...] + (1.0 - ADAM_B1) * gv
        v2 = ADAM_B2 * v_ref[...] + (1.0 - ADAM_B2) * (gv * gv)
        m_hat = m2 / (1.0 - ADAM_B1 ** ADAM_STEP)
        v_hat = v2 / (1.0 - ADAM_B2 ** ADAM_STEP)
        d_ref[...] = -ADAM_LR * (m_hat / (jnp.sqrt(v_hat) + ADAM_EPS) + ADAM_WD * w_ref[...])
        mo_ref[...] = m2
        vo_ref[...] = v2

    blk = BS((tr, c), lambda i: (i, 0))
    return pl.pallas_call(
        body, name="adamw", grid=(r // tr,), in_specs=[blk] * 4, out_specs=[blk] * 3,
        out_shape=[S((r, c), F32)] * 3, compiler_params=_cp("arbitrary"),
    )(w, g, m, v)


def _blob(arrays):
    flat = jnp.concatenate([a.reshape(-1) for a in arrays])
    rows = -(-flat.shape[0] // (SUB * LANES)) * SUB
    return jnp.pad(flat, (0, rows * LANES - flat.shape[0])).reshape(rows, LANES)


def _unblob(blob, shapes, lead=()):
    flat = blob.reshape(lead + (-1,))
    out, off = [], 0
    for s in shapes:
        size = math.prod(s)
        out.append(flat[..., off:off + size].reshape(lead + tuple(s)))
        off += size
    return out


def _lanes6(a):
    return jnp.zeros((1, LANES), F32).at[0, GG:GG + C_HEADS].set(a)


def _y_rows(w):
    return jnp.concatenate([w[0:A_W], w[A_W + B_W:], w[A_W:A_W + B_W]], axis=0)


def _y_rows_back(g):
    return jnp.concatenate([g[0:A_W], g[A_W + C_W:], g[A_W:A_W + C_W]], axis=0)


SMALL = ("norm_w", "q_norm_w", "k_norm_w", "sinks", "b_conv_b", "b_ln_w", "b_ln_b", "b_pw_b", "c_a_log", "c_dt_bias",
         "c_onorm_w", "b_conv_w", "c_conv_w")
ORDER = ("norm_w", "w_in", "q_norm_w", "k_norm_w", "sinks", "b_conv_w", "b_conv_b", "b_ln_w", "b_ln_b", "b_pw_w", "b_pw_b",
         "c_conv_w", "c_a_log", "c_dt_bias", "c_onorm_w", "w_out")


def kernel(x, positions, norm_w, w_in, q_norm_w, k_norm_w, sinks, b_conv_w, b_conv_b, b_ln_w, b_ln_b, b_pw_w, b_pw_b, c_conv_w, c_a_log, c_dt_bias, c_onorm_w, w_out, loss_target, m_norm_w, m_w_in, m_q_norm_w, m_k_norm_w, m_sinks, m_b_conv_w, m_b_conv_b, m_b_ln_w, m_b_ln_b, m_b_pw_w, m_b_pw_b, m_c_conv_w, m_c_a_log, m_c_dt_bias, m_c_onorm_w, m_w_out, v_norm_w, v_w_in, v_q_norm_w, v_k_norm_w, v_sinks, v_b_conv_w, v_b_conv_b, v_b_ln_w, v_b_ln_b, v_b_pw_w, v_b_pw_b, v_c_conv_w, v_c_a_log, v_c_dt_bias, v_c_onorm_w, v_w_out):
    W = dict(norm_w=norm_w, w_in=w_in, q_norm_w=q_norm_w, k_norm_w=k_norm_w, sinks=sinks, b_conv_w=b_conv_w, b_conv_b=b_conv_b,
             b_ln_w=b_ln_w, b_ln_b=b_ln_b, b_pw_w=b_pw_w, b_pw_b=b_pw_b, c_conv_w=c_conv_w, c_a_log=c_a_log,
             c_dt_bias=c_dt_bias, c_onorm_w=c_onorm_w, w_out=w_out)
    M = dict(norm_w=m_norm_w, w_in=m_w_in, q_norm_w=m_q_norm_w, k_norm_w=m_k_norm_w, sinks=m_sinks, b_conv_w=m_b_conv_w,
             b_conv_b=m_b_conv_b, b_ln_w=m_b_ln_w, b_ln_b=m_b_ln_b, b_pw_w=m_b_pw_w, b_pw_b=m_b_pw_b, c_conv_w=m_c_conv_w,
             c_a_log=m_c_a_log, c_dt_bias=m_c_dt_bias, c_onorm_w=m_c_onorm_w, w_out=m_w_out)
    V = dict(norm_w=v_norm_w, w_in=v_w_in, q_norm_w=v_q_norm_w, k_norm_w=v_k_norm_w, sinks=v_sinks, b_conv_w=v_b_conv_w,
             b_conv_b=v_b_conv_b, b_ln_w=v_b_ln_w, b_ln_b=v_b_ln_b, b_pw_w=v_b_pw_w, b_pw_b=v_b_pw_b, c_conv_w=v_c_conv_w,
             c_a_log=v_c_a_log, c_dt_bias=v_c_dt_bias, c_onorm_w=v_c_onorm_w, w_out=v_w_out)
    nseq, t, d = x.shape
    n = nseq * t
    tr = min(256, t)
    tmm = min(512, n)
    tmw = min(1024, n)
    tkk = min(2048, n)
    me = _dev_index(_my_pos())
    xs = [x.reshape(n, d)]
    tgt = loss_target.reshape(n, d)
    tabs = _rope_tables(positions.reshape(n))

    win_p = _pack_cols(w_in).astype(BF16)
    wout_b = w_out.astype(BF16)
    sharded_small = (b_pw_w, b_conv_w, c_conv_w)
    g_win0, g_small = _all_gather([win_p[0], _blob(sharded_small)])
    win = [g_win0]
    later = _exchange_start("gather", [win_p[1], wout_b[0], wout_b[1]], g_small, "gather_start")
    pw_all, cw_all, ccw_all = _unblob(g_small, [a.shape for a in sharded_small], lead=(N_DEV,))
    pw_all = pw_all.transpose(1, 0, 2, 3).reshape(DEPTH, B_W, B_W).astype(BF16)
    cw_all = cw_all.transpose(1, 2, 0, 3).reshape(DEPTH, B_K, B_W)
    ccw_all = ccw_all.transpose(1, 2, 0, 3).reshape(DEPTH, C_K, 3 * C_W)

    def layer_params(l):
        return dict(
            nw=norm_w[l][None], qw=jnp.tile(q_norm_w[l], 2)[None], kw=jnp.tile(k_norm_w[l], 2)[None], sinks=sinks[l],
            cw=jnp.pad(cw_all[l], ((0, HALO_B - B_K), (0, 0))), cb=b_conv_b[l][None], lnw=b_ln_w[l][None], lnb=b_ln_b[l][None],
            pw=pw_all[l], pwb=b_pw_b[l][None], ccw=jnp.pad(ccw_all[l], ((0, SUB - C_K), (0, 0))),
            alog=_lanes6(c_a_log[l]), dtb=_lanes6(c_dt_bias[l]), onw=c_onorm_w[l][None])

    saved = []
    for l in range(DEPTH):
        q = layer_params(l)
        nw = q["nw"] + later["token"][0:1, 0:1] if l == 0 else q["nw"]
        p, h = _inproj(xs[l], nw, win[l], tm=tmw)
        y, o_a, lse = _attn_fwd(p, tabs, q["qw"], q["kw"], q["sinks"], nseq)
        gates = _gdn_gates_fwd(p, q["alog"], q["dtb"], tm=tr)
        xc, qkv = _gdn_pre_fwd(p, q["ccw"], nseq, tm=tr)
        y, o_c, u, w, tinv, ss = _gdn_chunk_fwd(qkv, gates, p, y, q["onw"], nseq)
        y, hc = _conf_fwd(p, y, q["cw"], q["cb"], q["lnw"], q["lnb"], q["pw"], q["pwb"], nseq, tm=tr)
        saved.append(dict(q=q, p=p, h=h, y=y, o_a=o_a, lse=lse, gates=gates, xc=xc, qkv=qkv, o_c=o_c, u=u, w=w, tinv=tinv,
                          ss=ss, hc=hc))
        if l == 0:
            g_win1, g_wout0, g_wout1 = _exchange_wait(later, y, "gather_wait")
            win.append(g_win1)
            wout = [_y_rows(g_wout0), _y_rows(g_wout1)]
        if l + 1 < DEPTH:
            xs.append(_outproj(xs[l], y, wout[l], tm=tmw, tn=512))
        else:
            dxn, lsum = _outproj_loss(xs[l], y, wout[l], tgt, tm=tmw, tn=512)
    loss = lax.psum(jnp.sum(lsum) * (0.5 / d), ("x", "y", "c"))

    sent, smalls = [None] * DEPTH, [None] * DEPTH
    for l in reversed(range(DEPTH)):
        s = saved[l]
        q, p = s["q"], s["p"]
        dy = _matmul(dxn, wout[l], "nt", F32, tmw, 512, d, "outproj_bwd_dy")
        dwout = _y_rows_back(_matmul(s["y"], dxn, "tn", BF16, 1024, 1024, tkk, "outproj_bwd_dw"))
        dp, dkv, dqw, dkw, dsk = _attn_bwd(p, dy, s["o_a"], s["lse"], tabs, q["qw"], q["kw"], q["sinks"], nseq)
        dp = _put_cols(dp, dkv, P_K, tm=tmm)
        dp, dqkv, dgate, donw = _gdn_chunk_bwd(s["qkv"], s["gates"], p, dy, dp, q["onw"], s["o_c"], s["u"], s["w"],
                                               s["tinv"], s["ss"], nseq)
        dp, dal, ddb = _gdn_gates_bwd(dgate, p, q["alog"], q["dtb"], dp, tm=tr)
        dp, dccw = _gdn_pre_bwd(p, dqkv, s["xc"], dp, q["ccw"], nseq, tm=tr)
        dp, dhc, dpw, dpwb, dlnw, dlnb, dcb = _conf_bwd1(p, dy, dp, s["hc"], q["lnw"], q["lnb"], q["pw"], q["pwb"], tm=tr)
        dp, dcw = _conf_bwd2(p, dhc, dp, q["cw"], nseq, tm=tr)
        dwin = _matmul(s["h"], dp, "tn", BF16, 1024, 768, tkk, "inproj_bwd_dw")
        sent[l] = _exchange_start("scatter", [dwin, dwout, dpw], dpwb, "scatter_start_%d" % l)
        dxn, dnw = _inproj_bwd_dx(dp, win[l], xs[l], q["nw"] + sent[l]["token"][0:1, 0:1], dxn, tm=tmm)
        halves = lambda a: a.sum(0)[:A_DH] + a.sum(0)[A_DH:]
        smalls[l] = dict(
            norm_w=dnw.sum(0), q_norm_w=halves(dqw), k_norm_w=halves(dkw), sinks=dsk.sum(0)[:A_HEADS], b_conv_b=dcb.sum(0),
            b_ln_w=dlnw.sum(0), b_ln_b=dlnb.sum(0), b_pw_b=dpwb.sum(0), c_a_log=dal.sum(0)[GG:GG + C_HEADS],
            c_dt_bias=ddb.sum(0)[GG:GG + C_HEADS], c_onorm_w=donw.sum(0),
            b_conv_w=dcw.reshape(B_K, SUB, B_W).sum(1), c_conv_w=dccw.reshape(C_K, SUB, 3 * C_W).sum(1))
    grad_x = dxn.reshape(nseq, t, d)

    r_win1, r_wout1, r_pw1 = _exchange_wait(sent[1], dxn, "scatter_wait_1")
    r_win0, r_wout0, r_pw0 = _exchange_wait(sent[0], r_pw1, "scatter_wait_0")
    G = {}
    G["w_in"] = jnp.stack([_unpack_cols(_sum8(r_win0)), _unpack_cols(_sum8(r_win1))])
    G["w_out"] = jnp.stack([_sum8(r_wout0), _sum8(r_wout1)])
    G["b_pw_w"] = jnp.stack([_sum8(r_pw0), _sum8(r_pw1)])
    part = _blob([jnp.stack([smalls[l][k] for l in range(DEPTH)]) for k in SMALL])
    (tot,) = _all_gather([part])
    tot = _sum8(tot.reshape(N_DEV, part.shape[0], LANES))
    full_shapes = [(DEPTH,) + smalls[0][k].shape for k in SMALL]
    for k, g in zip(SMALL, _unblob(tot, full_shapes)):
        G[k] = g
    G["b_conv_w"] = lax.dynamic_slice_in_dim(G["b_conv_w"], me * (B_W // N_DEV), B_W // N_DEV, axis=2)
    G["c_conv_w"] = lax.dynamic_slice_in_dim(G["c_conv_w"], me * (3 * C_W // N_DEV), 3 * C_W // N_DEV, axis=2)

    delta, new_m, new_v = {}, {}, {}
    for k in ("w_in", "w_out", "b_pw_w"):
        shp = W[k].shape
        two = lambda a: a.reshape(shp[0] * shp[1], shp[2])
        dl, mo, vo = _adamw(two(W[k]), two(G[k]), two(M[k]), two(V[k]))
        delta[k], new_m[k], new_v[k] = dl.reshape(shp), mo.reshape(shp), vo.reshape(shp)
    dl, mo, vo = _adamw(*[_blob([src[k] for k in SMALL]) for src in (W, G, M, V)])
    shapes = [W[k].shape for k in SMALL]
    for k, a, b, c in zip(SMALL, _unblob(dl, shapes), _unblob(mo, shapes), _unblob(vo, shapes)):
        delta[k], new_m[k], new_v[k] = a, b, c
    return (loss, grad_x, *[G[k] for k in ORDER], *[delta[k] for k in ORDER], *[new_m[k] for k in ORDER],
            *[new_v[k] for k in ORDER])
```

```python
import functools
import math

import jax
import jax.numpy as jnp
from jax import lax
from jax.experimental import pallas as pl
from jax.experimental.pallas import tpu as pltpu

F32 = jnp.float32
BF16 = jnp.bfloat16
HI = lax.Precision.HIGHEST
MESH = pl.DeviceIdType.MESH
S = jax.ShapeDtypeStruct
BS = pl.BlockSpec

N_DEV = 8
DEPTH = 2
D_MODEL = 2048
A_HEADS, A_KV, A_DH, A_W, A_KVW = 12, 4, 64, 768, 256
ROT = 16
THETA = 500000.0
ABLK = 128
B_W, B_K = 512, 31
C_HEADS, C_DH, C_W, C_K, CHUNK = 6, 128, 768, 4, 64
EPS = 1e-6
IN_COLS = 6668
P_Q, P_ZA, P_ZC, P_QKV, P_K, P_V, P_UB, P_ZB, P_BA, P_W = 0, 768, 1536, 2304, 4608, 4864, 5120, 6144, 6656, 6912
Y_A, Y_C, Y_B = 0, 768, 1536
LANES = 128
SUB = 8

ADAM_LR, ADAM_B1, ADAM_B2, ADAM_EPS, ADAM_WD, ADAM_STEP = 0.001, 0.9, 0.999, 1e-08, 0.01, 10


def _cp(*sem, vmem=None):
    kw = {}
    if sem:
        kw["dimension_semantics"] = sem
    if vmem:
        kw["vmem_limit_bytes"] = vmem
    return pltpu.CompilerParams(**kw)


def _pack_cols(w):
    z = jnp.zeros(w.shape[:-1] + (P_W - IN_COLS,), w.dtype)
    return jnp.concatenate([w[..., 0:768], w[..., 1280:2048], w[..., 5900:6668], w[..., 3584:5888],
                            w[..., 768:1024], w[..., 1024:1280], w[..., 2048:3072], w[..., 3072:3584],
                            w[..., 5888:5900], z], axis=-1)


def _unpack_cols(g):
    return jnp.concatenate([g[..., P_Q:P_Q + 768], g[..., P_K:P_K + 256], g[..., P_V:P_V + 256], g[..., P_ZA:P_ZA + 768],
                            g[..., P_UB:P_UB + 1024], g[..., P_ZB:P_ZB + 512], g[..., P_QKV:P_QKV + 2304],
                            g[..., P_BA:P_BA + 12], g[..., P_ZC:P_ZC + 768]], axis=-1)


def _sigmoid(x):
    return 1.0 / (1.0 + jnp.exp(-x))


def _dsilu(x, sg):
    return sg * (1.0 + x * (1.0 - sg))


def _fold8(x):
    r, c = x.shape
    return x.reshape(r // SUB, SUB, c).sum(axis=0)


def _dot(a, b, prec=None):
    return jnp.dot(a, b, preferred_element_type=F32, precision=prec)


def _dot_nt(a, b, prec=None):
    return lax.dot_general(a, b, (((1,), (1,)), ((), ())), preferred_element_type=F32, precision=prec)


def _dot_tn(a, b, prec=None):
    return lax.dot_general(a, b, (((0,), (0,)), ((), ())), preferred_element_type=F32, precision=prec)


def _lane(shape):
    return lax.broadcasted_iota(jnp.int32, shape, 1)


def _subl(shape):
    return lax.broadcasted_iota(jnp.int32, shape, 0)


def _col(x, j):
    return jnp.sum(jnp.where(_lane(x.shape) == j, x, 0.0), axis=-1, keepdims=True)


def _inproj(x, nw, w, tm=512, tn=768):
    n, d = x.shape
    pw = w.shape[1]

    def body(x_ref, nw_ref, w_ref, p_ref, h_ref):
        @pl.when(pl.program_id(1) == 0)
        def _():
            xv = x_ref[...]
            r = lax.rsqrt(jnp.mean(xv * xv, axis=-1, keepdims=True) + EPS)
            h_ref[...] = (xv * r * nw_ref[...]).astype(BF16)

        p_ref[...] = _dot(h_ref[...], w_ref[...])

    return pl.pallas_call(
        body, name="inproj", grid=(n // tm, pw // tn),
        in_specs=[BS((tm, d), lambda i, j: (i, 0)), BS((1, d), lambda i, j: (0, 0)), BS((d, tn), lambda i, j: (0, j))],
        out_specs=[BS((tm, tn), lambda i, j: (i, j)), BS((tm, d), lambda i, j: (i, 0))],
        out_shape=[S((n, pw), F32), S((n, d), BF16)],
        compiler_params=_cp("arbitrary", "arbitrary"),
    )(x, nw, w)


def _outproj(x, y, w, tm=512, tn=1024):
    n, d = x.shape
    k = y.shape[1]

    def body(x_ref, y_ref, w_ref, o_ref):
        o_ref[...] = x_ref[...] + _dot(y_ref[...], w_ref[...])

    return pl.pallas_call(
        body, name="outproj", grid=(n // tm, d // tn),
        in_specs=[BS((tm, tn), lambda i, j: (i, j)), BS((tm, k), lambda i, j: (i, 0)), BS((k, tn), lambda i, j: (0, j))],
        out_specs=BS((tm, tn), lambda i, j: (i, j)),
        out_shape=S((n, d), F32),
        compiler_params=_cp("arbitrary", "arbitrary"),
    )(x, y, w)


def _outproj_loss(x, y, w, tgt, tm=512, tn=1024):
    n, d = x.shape
    k = y.shape[1]

    def body(x_ref, y_ref, w_ref, t_ref, g_ref, l_ref):
        @pl.when((pl.program_id(0) == 0) & (pl.program_id(1) == 0))
        def _():
            l_ref[...] = jnp.zeros_like(l_ref)

        diff = x_ref[...] + _dot(y_ref[...], w_ref[...]) - t_ref[...]
        g_ref[...] = diff * (1.0 / d)
        f = _fold8(diff * diff)
        acc = f[:, 0:LANES]
        for c in range(1, tn // LANES):
            acc = acc + f[:, c * LANES:(c + 1) * LANES]
        l_ref[...] += acc

    return pl.pallas_call(
        body, name="outproj_loss", grid=(n // tm, d // tn),
        in_specs=[BS((tm, tn), lambda i, j: (i, j)), BS((tm, k), lambda i, j: (i, 0)), BS((k, tn), lambda i, j: (0, j)),
                  BS((tm, tn), lambda i, j: (i, j))],
        out_specs=[BS((tm, tn), lambda i, j: (i, j)), BS((SUB, LANES), lambda i, j: (0, 0))],
        out_shape=[S((n, d), F32), S((SUB, LANES), F32)],
        compiler_params=_cp("arbitrary", "arbitrary"),
    )(x, y, w, tgt)


def _matmul(a, b, mode, out_dtype, tm, tn, tk, name):
    if mode == "nn":
        (m, kk), nn = a.shape, b.shape[1]
        a_spec, b_spec = BS((tm, tk), lambda i, j, k: (i, k)), BS((tk, tn), lambda i, j, k: (k, j))
        dot = _dot
    elif mode == "nt":
        (m, kk), nn = a.shape, b.shape[0]
        a_spec, b_spec = BS((tm, tk), lambda i, j, k: (i, k)), BS((tn, tk), lambda i, j, k: (j, k))
        dot = _dot_nt
    else:
        (kk, m), nn = a.shape, b.shape[1]
        a_spec, b_spec = BS((tk, tm), lambda i, j, k: (k, i)), BS((tk, tn), lambda i, j, k: (k, j))
        dot = _dot_tn
    nk = kk // tk

    def body(a_ref, b_ref, o_ref, acc_ref):
        kid = pl.program_id(2)

        @pl.when(kid == 0)
        def _():
            acc_ref[...] = jnp.zeros_like(acc_ref)

        acc_ref[...] += dot(a_ref[...].astype(BF16), b_ref[...].astype(BF16))

        @pl.when(kid == nk - 1)
        def _():
            o_ref[...] = acc_ref[...].astype(out_dtype)

    return pl.pallas_call(
        body, name=name, grid=(m // tm, nn // tn, nk),
        in_specs=[a_spec, b_spec], out_specs=BS((tm, tn), lambda i, j, k: (i, j)),
        out_shape=S((m, nn), out_dtype), scratch_shapes=[pltpu.VMEM((tm, tn), F32)],
        compiler_params=_cp("arbitrary", "arbitrary", "arbitrary"),
    )(a, b)


SLAB = 16


def _inproj_bwd_dx(dp, w, x, nw, dres, tm=512, tk=1152):
    n, d = x.shape
    nk = dp.shape[1] // tk

    def body(dp_ref, w_ref, x_ref, nw_ref, dr_ref, dx_ref, dnw_ref, acc_ref):
        kid = pl.program_id(1)

        @pl.when((pl.program_id(0) == 0) & (kid == 0))
        def _():
            dnw_ref[...] = jnp.zeros_like(dnw_ref)

        @pl.when(kid == 0)
        def _():
            acc_ref[...] = jnp.zeros_like(acc_ref)

        acc_ref[...] += _dot_nt(dp_ref[...], w_ref[...])

        @pl.when(kid == nk - 1)
        def _():
            def slab(i, carry):
                rows = pl.ds(pl.multiple_of(i * SLAB, SLAB), SLAB)
                dh, xv = acc_ref[rows, :], x_ref[rows, :]
                r = lax.rsqrt(jnp.mean(xv * xv, axis=-1, keepdims=True) + EPS)
                dnw_ref[...] += _fold8(dh * xv * r)
                g = dh * nw_ref[...]
                mm = jnp.mean(g * xv, axis=-1, keepdims=True)
                dx_ref[rows, :] = dr_ref[rows, :] + r * g - xv * (r * r * r * mm)
                return carry

            lax.fori_loop(0, tm // SLAB, slab, 0)

    return pl.pallas_call(
        body, name="inproj_bwd_dx", grid=(n // tm, nk),
        in_specs=[BS((tm, tk), lambda i, k: (i, k)), BS((d, tk), lambda i, k: (0, k)), BS((tm, d), lambda i, k: (i, 0)),
                  BS((1, d), lambda i, k: (0, 0)), BS((tm, d), lambda i, k: (i, 0))],
        out_specs=[BS((tm, d), lambda i, k: (i, 0)), BS((SUB, d), lambda i, k: (0, 0))],
        out_shape=[S((n, d), F32), S((SUB, d), F32)],
        scratch_shapes=[pltpu.VMEM((tm, d), F32)],
        compiler_params=_cp("arbitrary", "arbitrary"),
    )(dp, w, x, nw, dres)


def _rope_tables(pos):
    half = ROT // 2
    inv = THETA ** (-jnp.arange(0, ROT, 2, dtype=F32) / ROT)
    ang = pos.astype(F32)[:, None] * inv
    cos, sin = jnp.cos(ang), jnp.sin(ang)
    n = pos.shape[0]
    one = jnp.ones((n, A_DH - ROT), F32)
    zero = jnp.zeros((n, A_DH - ROT), F32)
    zh = jnp.zeros((n, half), F32)
    c = jnp.concatenate([cos, cos, one], axis=1)
    s1 = jnp.concatenate([-sin, zh, zero], axis=1)
    s2 = jnp.concatenate([zh, sin, zero], axis=1)
    return tuple(jnp.concatenate([t, t], axis=1) for t in (c, s1, s2))


def _half_stat(t):
    lo = _lane(t.shape) < A_DH
    s_lo = jnp.sum(jnp.where(lo, t, 0.0), axis=-1, keepdims=True)
    s_hi = jnp.sum(jnp.where(lo, 0.0, t), axis=-1, keepdims=True)
    return jnp.where(lo, s_lo, s_hi)


def _normrope(x, w, c, s1, s2):
    r = lax.rsqrt(_half_stat(x * x) * (1.0 / A_DH) + EPS)
    xn = x * r * w
    return xn * c + pltpu.roll(xn, LANES - ROT // 2, 1) * s1 + pltpu.roll(xn, ROT // 2, 1) * s2, r


def _normrope_bwd(dy, x, r, w, c, s1, s2):
    dxn = dy * c + pltpu.roll(dy * s1, ROT // 2, 1) + pltpu.roll(dy * s2, LANES - ROT // 2, 1)
    g = dxn * w
    mm = _half_stat(g * x) * (1.0 / A_DH)
    return r * g - x * (r * r * r * mm), dxn * x * r


def _attn_mask(first):
    qi = _subl((ABLK, 2 * ABLK))
    kj = _lane((ABLK, 2 * ABLK))
    dist = qi + ABLK - kj
    return (dist >= 0) & (dist < ABLK) & (jnp.logical_not(first) | (kj >= ABLK))


def _attn_fwd(p, tabs, qw, kw, sinks, nseq):
    n = p.shape[0]
    nb = n // nseq // ABLK
    cur = lambda b, i: (b * nb + i, 0)
    prv = lambda b, i: (b * nb + jnp.maximum(i - 1, 0), 0)
    colblk = lambda f, w, off: (lambda b, i: (f(b, i)[0], off // w))

    def body(q_ref, za_ref, kc_ref, vc_ref, kp_ref, vp_ref, c_ref, s1_ref, s2_ref, cp_ref, s1p_ref, s2p_ref,
             qw_ref, kw_ref, sink_ref, y_ref, o_ref, lse_ref):
        first = pl.program_id(1) == 0
        tc = (c_ref[...], s1_ref[...], s2_ref[...])
        tp = (cp_ref[...], s1p_ref[...], s2p_ref[...])
        q, kc, kp = q_ref[...], kc_ref[...], kp_ref[...]
        qn = [_normrope(q[:, LANES * b:LANES * (b + 1)], qw_ref[...], *tc)[0].astype(BF16) for b in range(A_W // LANES)]
        k2, v2 = [], []
        for b in range(A_KVW // LANES):
            sl = slice(LANES * b, LANES * (b + 1))
            k2.append(jnp.concatenate([_normrope(kp[:, sl], kw_ref[...], *tp)[0],
                                       _normrope(kc[:, sl], kw_ref[...], *tc)[0]], axis=0).astype(BF16))
            v2.append(jnp.concatenate([vp_ref[:, sl], vc_ref[:, sl]], axis=0).astype(BF16))
        valid = _attn_mask(first)
        heads = range(A_HEADS)
        half = lambda a, i: a[:, A_DH * (i % 2):A_DH * (i % 2 + 1)]
        kh = [half(k2[g // 2], g) for g in range(A_KV)]
        vh = [half(v2[g // 2], g) for g in range(A_KV)]
        s = [jnp.where(valid, _dot_nt(half(qn[j // 2], j), kh[j // 3]) * (A_DH ** -0.5), -jnp.inf) for j in heads]
        m = [jnp.maximum(jnp.max(s[j], axis=-1, keepdims=True), sink_ref[j]) for j in heads]
        e = [jnp.exp(s[j] - m[j]) for j in heads]
        den = [jnp.sum(e[j], axis=-1, keepdims=True) + jnp.exp(sink_ref[j] - m[j]) for j in heads]
        outs = [_dot((e[j] / den[j]).astype(BF16), vh[j // 3]) for j in heads]
        lse = jnp.zeros((ABLK, LANES), F32)
        for j in heads:
            lse = jnp.where(_lane(lse.shape) == j, m[j] + jnp.log(den[j]), lse)
        o = jnp.concatenate(outs, axis=1)
        za = za_ref[...]
        o_ref[...] = o
        lse_ref[...] = lse
        y_ref[...] = (o * (za * _sigmoid(za))).astype(BF16)

    tab_specs = [BS((ABLK, LANES), cur)] * 3 + [BS((ABLK, LANES), prv)] * 3
    return pl.pallas_call(
        body, name="attn_fwd", grid=(nseq, nb),
        in_specs=[BS((ABLK, A_W), colblk(cur, A_W, P_Q)), BS((ABLK, A_W), colblk(cur, A_W, P_ZA)),
                  BS((ABLK, A_KVW), colblk(cur, A_KVW, P_K)), BS((ABLK, A_KVW), colblk(cur, A_KVW, P_V)),
                  BS((ABLK, A_KVW), colblk(prv, A_KVW, P_K)), BS((ABLK, A_KVW), colblk(prv, A_KVW, P_V))]
        + tab_specs + [BS((1, LANES), lambda b, i: (0, 0))] * 2 + [BS(memory_space=pltpu.SMEM)],
        out_specs=[BS((ABLK, A_W), colblk(cur, A_W, Y_A)), BS((ABLK, A_W), cur), BS((ABLK, LANES), cur)],
        out_shape=[S((n, D_MODEL), BF16), S((n, A_W), F32), S((n, LANES), F32)],
        compiler_params=_cp("arbitrary", "arbitrary"),
    )(p, p, p, p, p, p, *tabs, *tabs, qw, kw, sinks)


def _attn_bwd(p, dy, o, lse, tabs, qw, kw, sinks, nseq):
    n = p.shape[0]
    nb = n // nseq // ABLK
    cur = lambda b, i: (b * nb + jnp.minimum(i, nb - 1), 0)
    prv = lambda b, i: (b * nb + jnp.maximum(i - 1, 0), 0)
    colblk = lambda f, w, off: (lambda b, i: (f(b, i)[0], off // w))

    def body(q_ref, za_ref, kc_ref, vc_ref, kp_ref, vp_ref, dy_ref, o_ref, lse_ref,
             c_ref, s1_ref, s2_ref, cp_ref, s1p_ref, s2p_ref, qw_ref, kw_ref, sink_ref,
             dqza_ref, dkv_ref, dqw_ref, dkw_ref, dsk_ref, tk_ref, tv_ref, ck_ref, cv_ref):
        i = pl.program_id(1)
        first = i == 0
        tc = (c_ref[...], s1_ref[...], s2_ref[...])
        tp = (cp_ref[...], s1p_ref[...], s2p_ref[...])
        nkb = A_KVW // LANES

        @pl.when((pl.program_id(0) == 0) & first)
        def _():
            dqw_ref[...] = jnp.zeros_like(dqw_ref)
            dkw_ref[...] = jnp.zeros_like(dkw_ref)
            dsk_ref[...] = jnp.zeros_like(dsk_ref)

        @pl.when(i < nb)
        def _():
            q, kc, kp = q_ref[...], kc_ref[...], kp_ref[...]
            qn, rq = [], []
            for b in range(A_W // LANES):
                a, r = _normrope(q[:, LANES * b:LANES * (b + 1)], qw_ref[...], *tc)
                qn.append(a.astype(BF16))
                rq.append(r)
            k2, v2 = [], []
            for b in range(nkb):
                sl = slice(LANES * b, LANES * (b + 1))
                k2.append(jnp.concatenate([_normrope(kp[:, sl], kw_ref[...], *tp)[0],
                                           _normrope(kc[:, sl], kw_ref[...], *tc)[0]], axis=0).astype(BF16))
                v2.append(jnp.concatenate([vp_ref[:, sl], vc_ref[:, sl]], axis=0).astype(BF16))
            valid = _attn_mask(first)
            za, dy, o, lse = za_ref[...], dy_ref[...], o_ref[...], lse_ref[...]
            sg = _sigmoid(za)
            do = dy * za * sg
            dqza_ref[:, A_W:2 * A_W] = (dy * o * _dsilu(za, sg)).astype(BF16)
            heads = range(A_HEADS)
            half = lambda a, i: a[:, A_DH * (i % 2):A_DH * (i % 2 + 1)]
            kh = [half(k2[g // 2], g) for g in range(A_KV)]
            vh = [half(v2[g // 2], g) for g in range(A_KV)]
            qh = [half(qn[j // 2], j) for j in heads]
            lj = [_col(lse, j) for j in heads]
            pr = [jnp.exp(jnp.where(valid, _dot_nt(qh[j], kh[j // 3]) * (A_DH ** -0.5), -jnp.inf) - lj[j]) for j in heads]
            doh = [do[:, A_DH * j:A_DH * (j + 1)] for j in heads]
            delta = [jnp.sum(doh[j] * o[:, A_DH * j:A_DH * (j + 1)], axis=-1, keepdims=True) for j in heads]
            dohb = [a.astype(BF16) for a in doh]
            ds = [(pr[j] * (_dot_nt(dohb[j], vh[j // 3]) - delta[j]) * (A_DH ** -0.5)).astype(BF16) for j in heads]
            dqs = [_dot(ds[j], kh[j // 3]) for j in heads]
            dkh = [_dot_tn(ds[j], qh[j]) for j in heads]
            dvh = [_dot_tn(pr[j].astype(BF16), dohb[j]) for j in heads]
            dks = [dkh[3 * g] + dkh[3 * g + 1] + dkh[3 * g + 2] for g in range(A_KV)]
            dvs = [dvh[3 * g] + dvh[3 * g + 1] + dvh[3 * g + 2] for g in range(A_KV)]
            dsk = jnp.zeros((ABLK, LANES), F32)
            for j in heads:
                dsk = dsk + jnp.where(_lane(dsk.shape) == j, -jnp.exp(sink_ref[j] - lj[j]) * delta[j], 0.0)
            dsk_ref[...] += _fold8(dsk)
            dqn = jnp.concatenate(dqs, axis=1)
            dqw = jnp.zeros((SUB, LANES), F32)
            dqo = []
            for b in range(A_W // LANES):
                sl = slice(LANES * b, LANES * (b + 1))
                dx, wt = _normrope_bwd(dqn[:, sl], q[:, sl], rq[b], qw_ref[...], *tc)
                dqo.append(dx)
                dqw = dqw + _fold8(wt)
            dqw_ref[...] += dqw
            dqza_ref[:, 0:A_W] = jnp.concatenate(dqo, axis=1).astype(BF16)
            tk_ref[...] = jnp.concatenate(dks, axis=1)
            tv_ref[...] = jnp.concatenate(dvs, axis=1)

        @pl.when(i == nb)
        def _():
            tk_ref[...] = jnp.zeros_like(tk_ref)
            tv_ref[...] = jnp.zeros_like(tv_ref)

        @pl.when(i > 0)
        def _():
            kp = kp_ref[...]
            dkn = ck_ref[...] + tk_ref[0:ABLK, :]
            dkw = jnp.zeros((SUB, LANES), F32)
            dko = []
            for b in range(nkb):
                sl = slice(LANES * b, LANES * (b + 1))
                r = _normrope(kp[:, sl], kw_ref[...], *tp)[1]
                dx, wt = _normrope_bwd(dkn[:, sl], kp[:, sl], r, kw_ref[...], *tp)
                dko.append(dx)
                dkw = dkw + _fold8(wt)
            dkw_ref[...] += dkw
            dkv_ref[:, 0:A_KVW] = jnp.concatenate(dko, axis=1).astype(BF16)
            dkv_ref[:, A_KVW:2 * A_KVW] = (cv_ref[...] + tv_ref[0:ABLK, :]).astype(BF16)

        ck_ref[...] = tk_ref[ABLK:2 * ABLK, :]
        cv_ref[...] = tv_ref[ABLK:2 * ABLK, :]

    tab_specs = [BS((ABLK, LANES), cur)] * 3 + [BS((ABLK, LANES), prv)] * 3
    acc = BS((SUB, LANES), lambda b, i: (0, 0))
    return pl.pallas_call(
        body, name="attn_bwd", grid=(nseq, nb + 1),
        in_specs=[BS((ABLK, A_W), colblk(cur, A_W, P_Q)), BS((ABLK, A_W), colblk(cur, A_W, P_ZA)),
                  BS((ABLK, A_KVW), colblk(cur, A_KVW, P_K)), BS((ABLK, A_KVW), colblk(cur, A_KVW, P_V)),
                  BS((ABLK, A_KVW), colblk(prv, A_KVW, P_K)), BS((ABLK, A_KVW), colblk(prv, A_KVW, P_V)),
                  BS((ABLK, A_W), colblk(cur, A_W, Y_A)), BS((ABLK, A_W), cur), BS((ABLK, LANES), cur)]
        + tab_specs + [BS((1, LANES), lambda b, i: (0, 0))] * 2 + [BS(memory_space=pltpu.SMEM)],
        out_specs=[BS((ABLK, 2 * A_W), cur), BS((ABLK, 2 * A_KVW), prv), acc, acc, acc],
        out_shape=[S((n, P_W), BF16), S((n, 2 * A_KVW), BF16)] + [S((SUB, LANES), F32)] * 3,
        scratch_shapes=[pltpu.VMEM((2 * ABLK, A_KVW), F32)] * 2 + [pltpu.VMEM((ABLK, A_KVW), F32)] * 2,
        compiler_params=_cp("arbitrary", "arbitrary"),
    )(p, p, p, p, p, p, dy, o, lse, *tabs, *tabs, qw, kw, sinks)


def _put_cols(dst, src, col_off, tm=512):
    n, w = src.shape

    def body(s_ref, d_in_ref, d_ref):
        d_ref[...] = s_ref[...]

    return pl.pallas_call(
        body, name="put_cols", grid=(n // tm,),
        in_specs=[BS((tm, w), lambda i: (i, 0)), BS(memory_space=pl.ANY)],
        out_specs=BS((tm, w), lambda i: (i, col_off // w)),
        out_shape=S(dst.shape, dst.dtype), input_output_aliases={1: 0},
        compiler_params=_cp("arbitrary"),
    )(src, dst)


HALO_B = 32


def _layernorm(hc, lnw, lnb):
    mu = jnp.mean(hc, axis=-1, keepdims=True)
    xc = hc - mu
    rstd = lax.rsqrt(jnp.mean(xc * xc, axis=-1, keepdims=True) + EPS)
    xhat = xc * rstd
    return xhat, rstd, xhat * lnw + lnb


def _shifted_copies(buf_ref, sh_ref):
    rows = sh_ref.shape[1]
    for b in range(1, SUB):
        sh_ref[b - 1] = buf_ref[pl.ds(b, rows), :]


def _rows_from(buf_ref, sh_ref, off, rows):
    a, b = divmod(off, SUB)
    if b == 0:
        return buf_ref[pl.ds(SUB * a, rows), :]
    return sh_ref[b - 1, pl.ds(SUB * a, rows), :]


def _conf_fwd(p, y, cw, cb, lnw, lnb, pw, pwb, nseq, tm=256):
    n = p.shape[0]
    t = n // nseq
    nt = t // tm
    row = lambda b, i: b * nt + i
    halo = lambda b, i: jnp.maximum((b * t + i * tm) // HALO_B - 1, 0)
    vec = BS((1, B_W), lambda b, i: (0, 0))

    def body(ub_ref, uh_ref, zb_ref, cw_ref, cb_ref, lnw_ref, lnb_ref, pw_ref, pwb_ref, y_in_ref, y_ref, hc_ref, buf_ref, sh_ref):
        ub, uh = ub_ref[...], uh_ref[...]
        hh = uh[:, :B_W] * _sigmoid(uh[:, B_W:])
        buf_ref[0:HALO_B, :] = jnp.where(pl.program_id(1) > 0, hh, 0.0)
        buf_ref[HALO_B:, :] = ub[:, :B_W] * _sigmoid(ub[:, B_W:])
        _shifted_copies(buf_ref, sh_ref)
        hc = jnp.zeros((tm, B_W), F32) + cb_ref[...]
        for k in range(B_K):
            hc = hc + cw_ref[k:k + 1, :] * _rows_from(buf_ref, sh_ref, HALO_B - B_K + 1 + k, tm)
        hc_ref[...] = hc
        ln = _layernorm(hc, lnw_ref[...], lnb_ref[...])[2]
        sw = ln * _sigmoid(ln)
        ob = _dot(sw.astype(BF16), pw_ref[...]) + pwb_ref[...]
        zb = zb_ref[...]
        y_ref[...] = (ob * (zb * _sigmoid(zb))).astype(BF16)

    return pl.pallas_call(
        body, name="conf_fwd", grid=(nseq, nt),
        in_specs=[BS((tm, 2 * B_W), lambda b, i: (row(b, i), P_UB // (2 * B_W))),
                  BS((HALO_B, 2 * B_W), lambda b, i: (halo(b, i), P_UB // (2 * B_W))),
                  BS((tm, B_W), lambda b, i: (row(b, i), P_ZB // B_W)),
                  BS((HALO_B, B_W), lambda b, i: (0, 0)), vec, vec, vec, BS((B_W, B_W), lambda b, i: (0, 0)), vec,
                  BS(memory_space=pl.ANY)],
        out_specs=[BS((tm, B_W), lambda b, i: (row(b, i), Y_B // B_W)), BS((tm, B_W), lambda b, i: (row(b, i), 0))],
        out_shape=[S(y.shape, y.dtype), S((n, B_W), F32)], input_output_aliases={9: 0},
        scratch_shapes=[pltpu.VMEM((HALO_B + tm, B_W), F32), pltpu.VMEM((SUB - 1, HALO_B + tm - SUB, B_W), F32)],
        compiler_params=_cp("arbitrary", "arbitrary"),
    )(p, p, p, cw, cb, lnw, lnb, pw, pwb, y)


def _conf_bwd1(p, dy, dp, hc, lnw, lnb, pw, pwb, tm=256):
    n = p.shape[0]
    vec = BS((1, B_W), lambda i: (0, 0))
    acc = BS((SUB, B_W), lambda i: (0, 0))

    def body(dy_ref, zb_ref, hc_ref, lnw_ref, lnb_ref, pw_ref, pwb_ref, dp_in_ref,
             dzb_ref, dhc_ref, dpw_ref, dpwb_ref, dlnw_ref, dlnb_ref, dcb_ref):
        @pl.when(pl.program_id(0) == 0)
        def _():
            for r in (dpw_ref, dpwb_ref, dlnw_ref, dlnb_ref, dcb_ref):
                r[...] = jnp.zeros_like(r)

        xhat, rstd, ln = _layernorm(hc_ref[...], lnw_ref[...], lnb_ref[...])
        sgl = _sigmoid(ln)
        sw = (ln * sgl).astype(BF16)
        ob = _dot(sw, pw_ref[...]) + pwb_ref[...]
        dy, zb = dy_ref[...], zb_ref[...]
        sgz = _sigmoid(zb)
        dzb_ref[...] = (dy * ob * _dsilu(zb, sgz)).astype(BF16)
        dob = dy * zb * sgz
        dobb = dob.astype(BF16)
        dpwb_ref[...] += _fold8(dob)
        dpw_ref[...] += _dot_tn(sw, dobb)
        dln = _dot_nt(dobb, pw_ref[...]) * _dsilu(ln, sgl)
        dlnw_ref[...] += _fold8(dln * xhat)
        dlnb_ref[...] += _fold8(dln)
        dxh = dln * lnw_ref[...]
        dhc = rstd * (dxh - jnp.mean(dxh, axis=-1, keepdims=True) - xhat * jnp.mean(dxh * xhat, axis=-1, keepdims=True))
        dcb_ref[...] += _fold8(dhc)
        dhc_ref[...] = dhc

    return pl.pallas_call(
        body, name="conf_bwd1", grid=(n // tm,),
        in_specs=[BS((tm, B_W), lambda i: (i, Y_B // B_W)), BS((tm, B_W), lambda i: (i, P_ZB // B_W)),
                  BS((tm, B_W), lambda i: (i, 0)), vec, vec, BS((B_W, B_W), lambda i: (0, 0)), vec,
                  BS(memory_space=pl.ANY)],
        out_specs=[BS((tm, B_W), lambda i: (i, P_ZB // B_W)), BS((tm, B_W), lambda i: (i, 0)),
                   BS((B_W, B_W), lambda i: (0, 0)), acc, acc, acc, acc],
        out_shape=[S(dp.shape, dp.dtype), S((n, B_W), F32), S((B_W, B_W), F32)] + [S((SUB, B_W), F32)] * 4,
        input_output_aliases={7: 0},
        compiler_params=_cp("arbitrary"),
    )(dy, p, hc, lnw, lnb, pw, pwb, dp)


def _conf_bwd2(p, dhc, dp, cw, nseq, tm=256):
    n = p.shape[0]
    t = n // nseq
    nt = t // tm
    row = lambda b, i: b * nt + i
    prev = lambda b, i: jnp.maximum((b * t + i * tm) // HALO_B - 1, 0)
    nxt = lambda b, i: jnp.minimum((b * t + (i + 1) * tm) // HALO_B, n // HALO_B - 1)

    def body(ub_ref, uh_ref, dh_ref, dn_ref, cw_ref, dp_in_ref, dub_ref, dcw_ref, buf_ref, dbuf_ref, sh_ref, dsh_ref):
        i = pl.program_id(1)

        @pl.when((pl.program_id(0) == 0) & (i == 0))
        def _():
            dcw_ref[...] = jnp.zeros_like(dcw_ref)

        ub, uh = ub_ref[...], uh_ref[...]
        a, sg = ub[:, :B_W], _sigmoid(ub[:, B_W:])
        buf_ref[0:HALO_B, :] = jnp.where(i > 0, uh[:, :B_W] * _sigmoid(uh[:, B_W:]), 0.0)
        buf_ref[HALO_B:, :] = a * sg
        dhc = dh_ref[...]
        dbuf_ref[0:tm, :] = dhc
        dbuf_ref[tm:, :] = jnp.where(i < nt - 1, dn_ref[...], 0.0)
        _shifted_copies(buf_ref, sh_ref)
        _shifted_copies(dbuf_ref, dsh_ref)
        dhg = jnp.zeros((tm, B_W), F32)
        for k in range(B_K):
            dhg = dhg + cw_ref[k:k + 1, :] * _rows_from(dbuf_ref, dsh_ref, B_K - 1 - k, tm)
            dcw_ref[SUB * k:SUB * (k + 1), :] += _fold8(dhc * _rows_from(buf_ref, sh_ref, HALO_B - B_K + 1 + k, tm))
        dub_ref[...] = jnp.concatenate([dhg * sg, dhg * a * sg * (1.0 - sg)], axis=1).astype(BF16)

    return pl.pallas_call(
        body, name="conf_bwd2", grid=(nseq, nt),
        in_specs=[BS((tm, 2 * B_W), lambda b, i: (row(b, i), P_UB // (2 * B_W))),
                  BS((HALO_B, 2 * B_W), lambda b, i: (prev(b, i), P_UB // (2 * B_W))),
                  BS((tm, B_W), lambda b, i: (row(b, i), 0)), BS((HALO_B, B_W), lambda b, i: (nxt(b, i), 0)),
                  BS((HALO_B, B_W), lambda b, i: (0, 0)), BS(memory_space=pl.ANY)],
        out_specs=[BS((tm, 2 * B_W), lambda b, i: (row(b, i), P_UB // (2 * B_W))),
                   BS((SUB * B_K, B_W), lambda b, i: (0, 0))],
        out_shape=[S(dp.shape, dp.dtype), S((SUB * B_K, B_W), F32)], input_output_aliases={5: 0},
        scratch_shapes=[pltpu.VMEM((HALO_B + tm, B_W), F32)] * 2 + [pltpu.VMEM((SUB - 1, HALO_B + tm - SUB, B_W), F32)] * 2,
        compiler_params=_cp("arbitrary", "arbitrary"),
    )(p, p, dhc, dhc, cw, dp)


HALO_C = 8
QS = C_DH ** -0.5
NCB = 3 * C_HEADS
CB0 = P_QKV // LANES
ZC0 = P_ZC // LANES
GB, GG = 0, C_HEADS


def _softplus(z):
    return jnp.maximum(z, 0.0) + jnp.log(1.0 + jnp.exp(-jnp.abs(z)))


def _gdn_gates_fwd(p, alog_l, dtb_l, tm=256):
    n = p.shape[0]

    def body(ba_ref, al_ref, db_ref, o_ref):
        blk = ba_ref[...]
        lane = _lane(blk.shape)
        g = jnp.where((lane >= GG) & (lane < GG + C_HEADS), -jnp.exp(al_ref[...]) * _softplus(blk + db_ref[...]), 0.0)
        tri = (_subl((CHUNK, CHUNK)) >= _lane((CHUNK, CHUNK))).astype(F32)
        gc = jnp.concatenate([_dot(tri, g[CHUNK * c:CHUNK * (c + 1)], HI) for c in range(tm // CHUNK)], axis=0)
        o_ref[...] = jnp.where(lane < GG, _sigmoid(blk), gc)

    return pl.pallas_call(
        body, name="gdn_gates_fwd", grid=(n // tm,),
        in_specs=[BS((tm, LANES), lambda i: (i, P_BA // LANES)), BS((1, LANES), lambda i: (0, 0)), BS((1, LANES), lambda i: (0, 0))],
        out_specs=BS((tm, LANES), lambda i: (i, 0)), out_shape=S((n, LANES), F32),
        compiler_params=_cp("arbitrary"),
    )(p, alog_l, dtb_l)


def _gdn_pre_fwd(p, ccw, nseq, tm=256):
    n = p.shape[0]
    t = n // nseq
    nt = t // tm
    row = lambda b, i: b * nt + i
    halo = lambda b, i: jnp.maximum((b * t + i * tm) // HALO_C - 1, 0)

    def body(x_ref, xh_ref, w_ref, xc_ref, o_ref, buf_ref):
        buf_ref[0:HALO_C, :] = jnp.where(pl.program_id(1) > 0, xh_ref[...], 0.0)
        buf_ref[HALO_C:, :] = x_ref[...]
        for c in range(NCB):
            cs = slice(LANES * c, LANES * (c + 1))
            xc = jnp.zeros((tm, LANES), F32)
            for k in range(C_K):
                xc = xc + w_ref[k:k + 1, cs] * buf_ref[pl.ds(HALO_C - C_K + 1 + k, tm), cs]
            xc_ref[:, cs] = xc
            act = xc * _sigmoid(xc)
            if c < 2 * C_HEADS:
                act = act * (lax.rsqrt(jnp.sum(act * act, axis=-1, keepdims=True) + EPS) * (QS if c < C_HEADS else 1.0))
            o_ref[:, cs] = act

    wide = 3 * C_W
    return pl.pallas_call(
        body, name="gdn_pre_fwd", grid=(nseq, nt),
        in_specs=[BS((tm, wide), lambda b, i: (row(b, i), P_QKV // wide)), BS((HALO_C, wide), lambda b, i: (halo(b, i), P_QKV // wide)),
                  BS((SUB, wide), lambda b, i: (0, 0))],
        out_specs=[BS((tm, wide), lambda b, i: (row(b, i), 0))] * 2,
        out_shape=[S((n, wide), F32)] * 2,
        scratch_shapes=[pltpu.VMEM((HALO_C + tm, wide), F32)],
        compiler_params=_cp("arbitrary", "arbitrary"),
    )(p, p, ccw)


def _chunk_common(q, k, gt, gtt, h):
    beta = _col(gt, GB + h)
    gc = _col(gt, GG + h)
    gcr = gtt[GG + h:GG + h + 1, :]
    ii, jj = _subl((CHUNK, CHUNK)), _lane((CHUNK, CHUNK))
    incl, strict = ii >= jj, ii > jj
    dec = jnp.exp(jnp.where(incl, gc - gcr, -jnp.inf))
    kb = k * beta
    kbf = k.astype(BF16)
    a = jnp.where(strict, _dot_nt(kb.astype(BF16), kbf) * dec, 0.0)
    mq = jnp.where(incl, _dot_nt(q.astype(BF16), kbf) * dec, 0.0)
    glast = jnp.sum(jnp.where(_subl(gc.shape) == CHUNK - 1, gc, 0.0), axis=0, keepdims=True)
    return beta, gc, incl, strict, dec, kb, a, mq, glast


def _split(x):
    hi = x.astype(BF16)
    return hi, (x - hi.astype(F32)).astype(BF16)


def _dot3(dot, a, b):
    (ah, al), (bh, bl) = a, b
    return dot(ah, bh) + (dot(ah, bl) + dot(al, bh))


def _unit_lower_inverses(mats):
    eye = (_subl(mats[0].shape) == _lane(mats[0].shape)).astype(F32)
    ms = [-a for a in mats]
    invs = [eye + m for m in ms]
    parts = [_split(m) for m in ms]
    for _ in range(5):
        ms = [_dot3(_dot, s, s) for s in parts]
        parts = [_split(m) for m in ms]
        invs = [inv + _dot3(_dot, _split(inv), s) for inv, s in zip(invs, parts)]
    return invs


def _gdn_chunk_fwd(qkv, gates, p, y, onw, nseq, tt=512):
    n = qkv.shape[0]
    t = n // nseq
    tt = min(tt, t)
    nt = t // tt
    nch = tt // CHUNK

    def body(q_ref, k_ref, v_ref, g_ref, zc_ref, onw_ref, y_in_ref, y_ref, o_ref, u_ref, w_ref, t_ref, ss_ref, s_scr):
        @pl.when(pl.program_id(1) == 0)
        def _():
            s_scr[...] = jnp.zeros_like(s_scr)

        def step(c, carry):
            rows = pl.ds(pl.multiple_of(c * CHUNK, CHUNK), CHUNK)
            gt = g_ref[rows, :]
            gtt = gt.T
            heads = range(C_HEADS)
            hs = [slice(C_DH * h, C_DH * (h + 1)) for h in heads]
            q, k, v = ([r[rows, hs[h]] for h in heads] for r in (q_ref, k_ref, v_ref))
            cm = [_chunk_common(q[h], k[h], gt, gtt, h) for h in heads]
            beta, gc, kb, mq, glast = ([m[i] for m in cm] for i in (0, 1, 5, 7, 8))
            tinv = _unit_lower_inverses([m[6] for m in cm])
            egc = [jnp.exp(g) for g in gc]
            sol = [_dot3(_dot, _split(tinv[h]), _split(jnp.concatenate([v[h] * beta[h], kb[h] * egc[h]], axis=1))) for h in heads]
            sv = [s_scr[h] for h in heads]
            sb = [s.astype(BF16) for s in sv]
            vnb = [(sol[h][:, :C_DH] - _dot(sol[h][:, C_DH:].astype(BF16), sb[h])).astype(BF16) for h in heads]
            o = [_dot((q[h] * egc[h]).astype(BF16), sb[h]) + _dot(mq[h].astype(BF16), vnb[h]) for h in heads]
            for h in heads:
                ss_ref[h, c] = sv[h]
                s_scr[h] = sv[h] * jnp.exp(glast[h]) + _dot_tn((k[h] * jnp.exp(glast[h] - gc[h])).astype(BF16), vnb[h])
            for h in heads:
                o_ref[rows, hs[h]] = o[h]
                u_ref[rows, hs[h]] = sol[h][:, :C_DH]
                w_ref[rows, hs[h]] = sol[h][:, C_DH:]
                t_ref[rows, hs[h]] = jnp.concatenate([tinv[h], jnp.zeros_like(tinv[h])], axis=1)
                zc = zc_ref[rows, hs[h]]
                r = lax.rsqrt(jnp.mean(o[h] * o[h], axis=-1, keepdims=True) + EPS)
                y_ref[rows, hs[h]] = (o[h] * r * onw_ref[...] * (zc * _sigmoid(zc))).astype(BF16)
            return carry

        lax.fori_loop(0, nch, step, 0)

    row = lambda b, i: b * nt + i
    wb = lambda col: BS((tt, C_W), lambda b, i: (row(b, i), col))
    return pl.pallas_call(
        body, name="gdn_chunk_fwd", grid=(nseq, nt),
        in_specs=[wb(0), wb(1), wb(2), BS((tt, LANES), lambda b, i: (row(b, i), 0)), wb(P_ZC // C_W),
                  BS((1, LANES), lambda b, i: (0, 0)), BS(memory_space=pl.ANY)],
        out_specs=[wb(Y_C // C_W), wb(0), wb(0), wb(0), wb(0),
                   BS((None, C_HEADS, nch, C_DH, C_DH), lambda b, i: (b, 0, i, 0, 0))],
        out_shape=[S(y.shape, y.dtype)] + [S((n, C_W), F32)] * 4 + [S((nseq, C_HEADS, t // CHUNK, C_DH, C_DH), F32)],
        input_output_aliases={6: 0},
        scratch_shapes=[pltpu.VMEM((C_HEADS, C_DH, C_DH), F32)],
        compiler_params=_cp("arbitrary", "arbitrary"),
    )(qkv, qkv, qkv, gates, p, onw, y)


def _gdn_chunk_bwd(qkv, gates, p, dy, dp, onw, o, u, w, tinv, ss, nseq, tt=256):
    n = qkv.shape[0]
    t = n // nseq
    tt = min(tt, t)
    nt = t // tt
    nch = tt // CHUNK

    def body(q_ref, k_ref, v_ref, g_ref, zc_ref, onw_ref, o_ref, dy_ref, u_ref, w_ref, t_ref, ss_ref, dp_in_ref,
             dzc_ref, dqkv_ref, dg_ref, donw_ref, ds_scr):
        @pl.when(pl.program_id(1) == 0)
        def _():
            ds_scr[...] = jnp.zeros_like(ds_scr)

        @pl.when((pl.program_id(0) == 0) & (pl.program_id(1) == 0))
        def _():
            donw_ref[...] = jnp.zeros_like(donw_ref)

        def rsum(x):
            return jnp.sum(x, axis=-1, keepdims=True)

        def step(ci, carry):
            c = nch - 1 - ci
            rows = pl.ds(pl.multiple_of(c * CHUNK, CHUNK), CHUNK)
            gt = g_ref[rows, :]
            gtt = gt.T
            live = [head(c, rows, gt, gtt, h) for h in range(C_HEADS)]
            while live:
                live = [g for g in live if next(g, False)]
            return carry

        def head(c, rows, gt, gtt, h):
            hs = slice(C_DH * h, C_DH * (h + 1))
            q, k, v = q_ref[rows, hs], k_ref[rows, hs], v_ref[rows, hs]
            zc, o, dy, u, w = zc_ref[rows, hs], o_ref[rows, hs], dy_ref[rows, hs], u_ref[rows, hs], w_ref[rows, hs]
            tm_ = t_ref[rows, hs][:, 0:CHUNK]
            sv, dsv = ss_ref[h, c], ds_scr[h]
            sb, dsb = sv.astype(BF16), dsv.astype(BF16)
            sg = _sigmoid(zc)
            r = lax.rsqrt(jnp.mean(o * o, axis=-1, keepdims=True) + EPS)
            on = o * r
            ow = onw_ref[...]
            dzc_ref[rows, hs] = (dy * on * ow * _dsilu(zc, sg)).astype(BF16)
            t1 = dy * zc * sg
            donw_ref[...] += _fold8(t1 * on)
            don = t1 * ow
            do = r * (don - on * jnp.mean(don * on, axis=-1, keepdims=True))
            dob = do.astype(BF16)
            yield True
            beta, gc, incl, strict, dec, kb, a, mq, glast = _chunk_common(q, k, gt, gtt, h)
            egc = jnp.exp(gc)
            gl = jnp.exp(glast)
            ekd = jnp.exp(glast - gc)
            wb = w.astype(BF16)
            vnb = (u - _dot(wb, sb)).astype(BF16)
            qg = q * egc
            yield True
            dvn = _dot_tn(mq.astype(BF16), dob) + _dot((k * ekd).astype(BF16), dsb)
            dvnb = dvn.astype(BF16)
            dqg = _dot_nt(dob, sb)
            yield True
            dmq = jnp.where(incl, _dot_nt(dob, vnb), 0.0)
            dkd = _dot_nt(vnb, dsb)
            dgl = jnp.sum(rsum(dsv * sv), axis=0, keepdims=True)
            dw = -_dot_nt(dvnb, sb)
            yield True
            ds_scr[h] = gl * dsv + _dot_tn(qg.astype(BF16), dob) - _dot_tn(wb, dvnb)
            db = _dot3(_dot_tn, _split(tm_), _split(jnp.concatenate([dvn, dw], axis=1)))
            dbv, dbk = db[:, :C_DH], db[:, C_DH:]
            yield True
            da = -jnp.where(strict, _dot3(_dot_nt, _split(dbv), _split(u)) + _dot3(_dot_nt, _split(dbk), _split(w)), 0.0)
            yield True
            e = da * a + dmq * mq
            dgc = rsum(e) - rsum(e.T)
            dgb, dhb, kbf = (da * dec).astype(BF16), (dmq * dec).astype(BF16), k.astype(BF16)
            dkb = _dot(dgb, kbf)
            tk = rsum(dbk * k)
            rk = rsum(dkd * k) * ekd
            dq = _dot(dhb, kbf) + egc * dqg
            dk = _dot_tn(dgb, kb.astype(BF16)) + _dot_tn(dhb, q.astype(BF16)) + beta * (egc * dbk + dkb) + ekd * dkd
            dbeta = rsum(dbv * v) + tk * egc + rsum(dkb * k)
            dgc = dgc + tk * beta * egc + egc * rsum(dqg * q) - rk
            dglast = jnp.sum(rk, axis=0, keepdims=True) + dgl * gl
            dgc = dgc + jnp.where(_subl(dgc.shape) == CHUNK - 1, dglast, 0.0)
            dqkv_ref[0, rows, hs] = dq
            dqkv_ref[1, rows, hs] = dk
            dqkv_ref[2, rows, hs] = beta * dbv
            lane = _lane((CHUNK, LANES))
            dg_ref[h, rows, :] = jnp.where(lane == 0, dbeta, jnp.where(lane == 1, dgc, 0.0))

        lax.fori_loop(0, nch, step, 0)

    row = lambda b, i: b * nt + nt - 1 - i
    wb = lambda col: BS((tt, C_W), lambda b, i: (row(b, i), col))
    return pl.pallas_call(
        body, name="gdn_chunk_bwd", grid=(nseq, nt),
        in_specs=[wb(0), wb(1), wb(2), BS((tt, LANES), lambda b, i: (row(b, i), 0)), wb(P_ZC // C_W),
                  BS((1, LANES), lambda b, i: (0, 0)), wb(0), wb(Y_C // C_W), wb(0), wb(0), wb(0),
                  BS((None, C_HEADS, nch, C_DH, C_DH), lambda b, i: (b, 0, nt - 1 - i, 0, 0)), BS(memory_space=pl.ANY)],
        out_specs=[wb(P_ZC // C_W), BS((3, tt, C_W), lambda b, i: (0, row(b, i), 0)),
                   BS((C_HEADS, tt, LANES), lambda b, i: (0, row(b, i), 0)), BS((SUB, LANES), lambda b, i: (0, 0))],
        out_shape=[S(dp.shape, dp.dtype), S((3, n, C_W), F32), S((C_HEADS, n, LANES), F32), S((SUB, LANES), F32)],
        input_output_aliases={12: 0},
        scratch_shapes=[pltpu.VMEM((C_HEADS, C_DH, C_DH), F32)],
        compiler_params=_cp("arbitrary", "arbitrary"),
    )(qkv, qkv, qkv, gates, p, onw, o, dy, u, w, tinv, ss, dp)


def _gdn_gates_bwd(dgate, p, alog_l, dtb_l, dp, tm=256):
    n = p.shape[0]
    acc = BS((SUB, LANES), lambda i: (0, 0))

    def body(dg_ref, ba_ref, al_ref, db_ref, dp_in_ref, dba_ref, dal_ref, ddb_ref):
        @pl.when(pl.program_id(0) == 0)
        def _():
            dal_ref[...] = jnp.zeros_like(dal_ref)
            ddb_ref[...] = jnp.zeros_like(ddb_ref)

        blk = ba_ref[...]
        lane = _lane(blk.shape)
        dbeta = jnp.zeros_like(blk)
        dgc = jnp.zeros_like(blk)
        for h in range(C_HEADS):
            dbeta = dbeta + jnp.where(lane == GB + h, _col(dg_ref[h], 0), 0.0)
            dgc = dgc + jnp.where(lane == GG + h, _col(dg_ref[h], 1), 0.0)
        tri = (_subl((CHUNK, CHUNK)) <= _lane((CHUNK, CHUNK))).astype(F32)
        dg = jnp.concatenate([_dot(tri, dgc[CHUNK * c:CHUNK * (c + 1)], HI) for c in range(tm // CHUNK)], axis=0)
        beta = _sigmoid(blk)
        z = blk + db_ref[...]
        ea = jnp.exp(al_ref[...])
        isg = (lane >= GG) & (lane < GG + C_HEADS)
        dz = jnp.where(isg, -dg * ea * _sigmoid(z), 0.0)
        dal_ref[...] += _fold8(jnp.where(isg, -dg * ea * _softplus(z), 0.0))
        ddb_ref[...] += _fold8(dz)
        out = jnp.where(lane < GG, dbeta * beta * (1.0 - beta), dz)
        dba_ref[...] = jnp.concatenate([out, jnp.zeros_like(out)], axis=1).astype(BF16)

    return pl.pallas_call(
        body, name="gdn_gates_bwd", grid=(n // tm,),
        in_specs=[BS((C_HEADS, tm, LANES), lambda i: (0, i, 0)), BS((tm, LANES), lambda i: (i, P_BA // LANES)),
                  BS((1, LANES), lambda i: (0, 0)), BS((1, LANES), lambda i: (0, 0)), BS(memory_space=pl.ANY)],
        out_specs=[BS((tm, 2 * LANES), lambda i: (i, P_BA // (2 * LANES))), acc, acc],
        out_shape=[S(dp.shape, dp.dtype), S((SUB, LANES), F32), S((SUB, LANES), F32)],
        input_output_aliases={4: 0},
        compiler_params=_cp("arbitrary"),
    )(dgate, p, alog_l, dtb_l, dp)


def _gdn_pre_bwd(p, dqkv, xc, dp, ccw, nseq, tm=256):
    n = p.shape[0]
    t = n // nseq
    nt = t // tm
    wide = 3 * C_W
    row = lambda b, i: b * nt + i
    prev = lambda b, i: jnp.maximum((b * t + i * tm) // HALO_C - 1, 0)
    nxt = lambda b, i: jnp.minimum((b * t + (i + 1) * tm) // HALO_C, n // HALO_C - 1)

    def d_conv_out(d, xc, part):
        sg = _sigmoid(xc)
        act = xc * sg
        if part < 2:
            cs = QS if part == 0 else 1.0
            rn = lax.rsqrt(jnp.sum(act * act, axis=-1, keepdims=True) + EPS)
            d = cs * rn * d - act * (cs * rn * rn * rn * jnp.sum(d * act, axis=-1, keepdims=True))
        return d * _dsilu(xc, sg)

    def body(x_ref, xh_ref, d_ref, dn_ref, xc_ref, xn_ref, w_ref, dp_in_ref, dx_ref, dw_ref, buf_ref, dbuf_ref):
        i = pl.program_id(1)

        @pl.when((pl.program_id(0) == 0) & (i == 0))
        def _():
            dw_ref[...] = jnp.zeros_like(dw_ref)

        buf_ref[0:HALO_C, :] = jnp.where(i > 0, xh_ref[...], 0.0)
        buf_ref[HALO_C:, :] = x_ref[...]
        for c in range(NCB):
            cs = slice(LANES * c, LANES * (c + 1))
            part, hd = divmod(c, C_HEADS)
            hs = slice(LANES * hd, LANES * (hd + 1))
            d = d_conv_out(d_ref[part, :, hs], xc_ref[:, cs], part)
            dbuf_ref[0:tm, cs] = d
            dbuf_ref[tm:, cs] = jnp.where(i < nt - 1, d_conv_out(dn_ref[part, :, hs], xn_ref[:, cs], part), 0.0)
            dx = jnp.zeros((tm, LANES), F32)
            for k in range(C_K):
                dx = dx + w_ref[k:k + 1, cs] * dbuf_ref[pl.ds(C_K - 1 - k, tm), cs]
                dw_ref[SUB * k:SUB * (k + 1), cs] += _fold8(d * buf_ref[pl.ds(HALO_C - C_K + 1 + k, tm), cs])
            dx_ref[:, cs] = dx.astype(BF16)

    return pl.pallas_call(
        body, name="gdn_pre_bwd", grid=(nseq, nt),
        in_specs=[BS((tm, wide), lambda b, i: (row(b, i), P_QKV // wide)), BS((HALO_C, wide), lambda b, i: (prev(b, i), P_QKV // wide)),
                  BS((3, tm, C_W), lambda b, i: (0, row(b, i), 0)), BS((3, HALO_C, C_W), lambda b, i: (0, nxt(b, i), 0)),
                  BS((tm, wide), lambda b, i: (row(b, i), 0)), BS((HALO_C, wide), lambda b, i: (nxt(b, i), 0)),
                  BS((SUB, wide), lambda b, i: (0, 0)), BS(memory_space=pl.ANY)],
        out_specs=[BS((tm, wide), lambda b, i: (row(b, i), P_QKV // wide)), BS((SUB * C_K, wide), lambda b, i: (0, 0))],
        out_shape=[S(dp.shape, dp.dtype), S((SUB * C_K, wide), F32)], input_output_aliases={7: 0},
        scratch_shapes=[pltpu.VMEM((HALO_C + tm, wide), F32)] * 2,
        compiler_params=_cp("arbitrary", "arbitrary"),
    )(p, p, dqkv, dqkv, xc, xc, ccw, dp)


ANY = BS(memory_space=pl.ANY)


def _my_pos():
    return lax.axis_index("x"), lax.axis_index("y"), lax.axis_index("c")


def _dev_index(dev):
    return 4 * dev[0] + 2 * dev[1] + dev[2]


def _all_gather(shards, after=None):
    nk = len(shards)

    tail = [] if after is None else [after]

    def body(*refs):
        ins, outs = refs[:nk], refs[nk + len(tail):2 * nk + len(tail)]
        send, recv, loc = refs[2 * nk + len(tail):]
        x, y, c = _my_pos()
        me, sib = (x, y, c), (x, y, 1 - c)
        chips = [(1 - x, y), (x, 1 - y), (1 - x, 1 - y)]

        def rows(t, dev):
            r = ins[t].shape[0]
            return outs[t].at[pl.ds(pl.multiple_of(_dev_index(dev) * r, SUB), r), :]

        def copy(t, k, block, to, src=None):
            return pltpu.make_async_remote_copy(
                src_ref=rows(t, block) if src is None else src, dst_ref=rows(t, block),
                send_sem=send.at[t, k], recv_sem=recv.at[t, k], device_id=to, device_id_type=MESH)

        mine = [pltpu.make_async_copy(ins[t], rows(t, me), loc.at[t]) for t in range(nk)]
        for cp in mine:
            cp.start()
        first = []
        for t in range(nk):
            first.append(copy(t, 0, me, sib, src=ins[t]))
            first += [copy(t, 1 + j, me, (*chip, c), src=ins[t]) for j, chip in enumerate(chips)]
        for cp in first:
            cp.start()
        passed = []
        for j, chip in enumerate(chips):
            for t in range(nk):
                copy(t, 1 + j, (*chip, c), me).wait_recv()
                cp = copy(t, 4 + j, (*chip, c), sib)
                cp.start()
                passed.append(cp)
        for t in range(nk):
            copy(t, 0, sib, me).wait_recv()
            for j, chip in enumerate(chips):
                copy(t, 4 + j, (*chip, 1 - c), me).wait_recv()
        for cp in first + passed:
            cp.wait_send()
        for cp in mine:
            cp.wait()

    return pl.pallas_call(
        body, name="all_gather", in_specs=[ANY] * (nk + len(tail)), out_specs=[ANY] * nk,
        out_shape=[S((N_DEV * a.shape[0], a.shape[1]), a.dtype) for a in shards],
        scratch_shapes=[pltpu.SemaphoreType.DMA((nk, 7)), pltpu.SemaphoreType.DMA((nk, 7)), pltpu.SemaphoreType.DMA((nk,))],
    )(*shards, *tail)


SEM = BS(memory_space=pltpu.SEMAPHORE)
HBM = BS(memory_space=pltpu.HBM)
EFFECT = pltpu.SideEffectType.DATAFLOW_SIDE_EFFECTING


def _peers(x, y, c):
    return [((1 - x) if k & 4 else x, (1 - y) if k & 2 else y, (1 - c) if k & 1 else c) for k in range(1, N_DEV)]


def _exchange_copy(kind, src, land, send, recv, t, k, peer, me, arriving):
    frm = peer if arriving else me
    if kind == "gather":
        r = src.shape[0]
        s_ref = src
        d_ref = land.at[pl.ds(pl.multiple_of(_dev_index(frm) * r, SUB), r), :]
    else:
        r = src.shape[0] // N_DEV
        s_ref = src.at[pl.ds(pl.multiple_of(_dev_index(peer) * r, SUB), r), :]
        d_ref = land.at[_dev_index(frm)]
    sem = t * (N_DEV - 1) + k
    return pltpu.make_async_remote_copy(src_ref=s_ref, dst_ref=d_ref, send_sem=send.at[sem], recv_sem=recv.at[sem],
                                        device_id=peer, device_id_type=MESH)


def _own_copy(kind, src, land, own, t, me):
    if kind == "gather":
        r = src.shape[0]
        return pltpu.make_async_copy(src, land.at[pl.ds(pl.multiple_of(_dev_index(me) * r, SUB), r), :], own.at[t])
    r = src.shape[0] // N_DEV
    return pltpu.make_async_copy(src.at[pl.ds(pl.multiple_of(_dev_index(me) * r, SUB), r), :], land.at[_dev_index(me)], own.at[t])


def _exchange_start(kind, srcs, after, name):
    nk = len(srcs)
    if kind == "gather":
        lands = [lax.empty((N_DEV * a.shape[0], a.shape[1]), a.dtype) for a in srcs]
    else:
        lands = [lax.empty((N_DEV, a.shape[0] // N_DEV, a.shape[1]), a.dtype) for a in srcs]

    def body(*refs):
        src, land = refs[:nk], refs[nk:2 * nk]
        send, recv, own = refs[2 * nk + 1], refs[2 * nk + 2], refs[2 * nk + 3]
        token = refs[-1]
        x, y, c = _my_pos()
        me = (x, y, c)
        for t in range(nk):
            _own_copy(kind, src[t], land[t], own, t, me).start()
            for k, peer in enumerate(_peers(x, y, c)):
                _exchange_copy(kind, src[t], land[t], send, recv, t, k, peer, me, False).start()
        token[...] = jnp.zeros_like(token)

    hbm = lambda a: pltpu.HBM(a.shape, a.dtype)
    out = pl.pallas_call(
        body, name=name,
        out_shape=(pltpu.SemaphoreType.DMA((nk * (N_DEV - 1),)), pltpu.SemaphoreType.DMA((nk * (N_DEV - 1),)),
                   pltpu.SemaphoreType.DMA((nk,)), *[hbm(a) for a in srcs], *[hbm(a) for a in lands], S((SUB, LANES), F32)),
        in_specs=[HBM] * (2 * nk) + [ANY],
        out_specs=(SEM, SEM, SEM, *[HBM] * (2 * nk), BS(memory_space=pltpu.VMEM)),
        input_output_aliases={i: 3 + i for i in range(2 * nk)},
        compiler_params=pltpu.CompilerParams(has_side_effects=EFFECT),
    )(*[pltpu.with_memory_space_constraint(a, pltpu.HBM) for a in (*srcs, *lands)], after)
    return dict(kind=kind, nk=nk, send=out[0], recv=out[1], own=out[2], srcs=out[3:3 + nk], lands=out[3 + nk:3 + 2 * nk],
                token=out[-1])


def _exchange_wait(ex, after, name):
    kind, nk = ex["kind"], ex["nk"]

    def body(*refs):
        src, land = refs[:nk], refs[nk:2 * nk]
        send, recv, own = refs[2 * nk], refs[2 * nk + 1], refs[2 * nk + 2]
        x, y, c = _my_pos()
        me = (x, y, c)
        for t in range(nk):
            _own_copy(kind, src[t], land[t], own, t, me).wait()
            for k, peer in enumerate(_peers(x, y, c)):
                _exchange_copy(kind, src[t], land[t], send, recv, t, k, peer, me, False).wait_send()
                _exchange_copy(kind, src[t], land[t], send, recv, t, k, peer, me, True).wait_recv()

    hbm = lambda a: pltpu.HBM(a.shape, a.dtype)
    out = pl.pallas_call(
        body, name=name,
        out_shape=(*[hbm(a) for a in ex["srcs"]], *[hbm(a) for a in ex["lands"]]),
        in_specs=[HBM] * (2 * nk) + [SEM, SEM, SEM, ANY], out_specs=tuple([HBM] * (2 * nk)),
        input_output_aliases={i: i for i in range(2 * nk)},
        compiler_params=pltpu.CompilerParams(has_side_effects=EFFECT),
    )(*ex["srcs"], *ex["lands"], ex["send"], ex["recv"], ex["own"], after)
    return list(out[nk:])


BLOCK_BYTES = 4 << 20


def _row_tile(rows, row_bytes, align):
    best = align
    for tr in range(align, rows + 1, align):
        if rows % tr == 0 and tr * row_bytes <= BLOCK_BYTES:
            best = tr
    return best


def _sum8(a):
    _, r, w = a.shape
    tr = _row_tile(r, N_DEV * w * a.dtype.itemsize, 32 // a.dtype.itemsize)

    def body(a_ref, o_ref):
        acc = a_ref[0].astype(F32)
        for d in range(1, N_DEV):
            acc = acc + a_ref[d].astype(F32)
        o_ref[...] = acc

    return pl.pallas_call(
        body, name="sum8", grid=(r // tr,), in_specs=[BS((N_DEV, tr, w), lambda i: (0, i, 0))],
        out_specs=BS((tr, w), lambda i: (i, 0)), out_shape=S((r, w), F32), compiler_params=_cp("arbitrary"),
    )(a)


def _adamw(w, g, m, v):
    r, c = w.shape
    tr = _row_tile(r, c * 4 * 2, SUB)

    def body(w_ref, g_ref, m_ref, v_ref, d_ref, mo_ref, vo_ref):
        d_ref[...], mo_ref[...], vo_ref[...] = _adam_update(w_ref[...], g_ref[...], m_ref[...], v_ref[...])

    blk = BS((tr, c), lambda i: (i, 0))
    return pl.pallas_call(
        body, name="adamw", grid=(r // tr,), in_specs=[blk] * 4, out_specs=[blk] * 3,
        out_shape=[S((r, c), F32)] * 3, compiler_params=_cp("arbitrary"),
    )(w, g, m, v)


def _adam_update(w, g, m, v):
    m2 = ADAM_B1 * m + (1.0 - ADAM_B1) * g
    v2 = ADAM_B2 * v + (1.0 - ADAM_B2) * (g * g)
    m_hat = m2 / (1.0 - ADAM_B1 ** ADAM_STEP)
    v_hat = v2 / (1.0 - ADAM_B2 ** ADAM_STEP)
    return -ADAM_LR * (m_hat / (jnp.sqrt(v_hat) + ADAM_EPS) + ADAM_WD * w), m2, v2


def _adamw_layer(w, g, m, v, l, prev):
    nl, r, c = w.shape
    tr = _row_tile(r, c * 4 * 2, SUB)

    def body(w_ref, g_ref, m_ref, v_ref, *refs):
        go_ref, d_ref, mo_ref, vo_ref = refs[-4:]
        gv = g_ref[...]
        go_ref[...] = gv
        d_ref[...], mo_ref[...], vo_ref[...] = _adam_update(w_ref[...], gv, m_ref[...], v_ref[...])

    slot = BS((None, tr, c), lambda i: (l, i, 0))
    keep = [] if prev is None else [ANY] * 4
    return pl.pallas_call(
        body, name="adamw_layer", grid=(r // tr,), in_specs=[slot, BS((tr, c), lambda i: (i, 0)), slot, slot] + keep,
        out_specs=[slot] * 4, out_shape=[S((nl, r, c), F32)] * 4,
        input_output_aliases={} if prev is None else {4 + i: i for i in range(4)},
        compiler_params=_cp("arbitrary"),
    )(w, g, m, v, *(prev or ()))


def _blob(arrays):
    flat = jnp.concatenate([a.reshape(-1) for a in arrays])
    rows = -(-flat.shape[0] // (SUB * LANES)) * SUB
    return jnp.pad(flat, (0, rows * LANES - flat.shape[0])).reshape(rows, LANES)


def _unblob(blob, shapes, lead=()):
    flat = blob.reshape(lead + (-1,))
    out, off = [], 0
    for s in shapes:
        size = math.prod(s)
        out.append(flat[..., off:off + size].reshape(lead + tuple(s)))
        off += size
    return out


def _lanes6(a):
    return jnp.zeros((1, LANES), F32).at[0, GG:GG + C_HEADS].set(a)


def _y_rows(w):
    return jnp.concatenate([w[0:A_W], w[A_W + B_W:], w[A_W:A_W + B_W]], axis=0)


def _y_rows_back(g):
    return jnp.concatenate([g[0:A_W], g[A_W + C_W:], g[A_W:A_W + C_W]], axis=0)


SMALL = ("norm_w", "q_norm_w", "k_norm_w", "sinks", "b_conv_b", "b_ln_w", "b_ln_b", "b_pw_b", "c_a_log", "c_dt_bias",
         "c_onorm_w", "b_conv_w", "c_conv_w")
ORDER = ("norm_w", "w_in", "q_norm_w", "k_norm_w", "sinks", "b_conv_w", "b_conv_b", "b_ln_w", "b_ln_b", "b_pw_w", "b_pw_b",
         "c_conv_w", "c_a_log", "c_dt_bias", "c_onorm_w", "w_out")


def kernel(x, positions, norm_w, w_in, q_norm_w, k_norm_w, sinks, b_conv_w, b_conv_b, b_ln_w, b_ln_b, b_pw_w, b_pw_b, c_conv_w, c_a_log, c_dt_bias, c_onorm_w, w_out, loss_target, m_norm_w, m_w_in, m_q_norm_w, m_k_norm_w, m_sinks, m_b_conv_w, m_b_conv_b, m_b_ln_w, m_b_ln_b, m_b_pw_w, m_b_pw_b, m_c_conv_w, m_c_a_log, m_c_dt_bias, m_c_onorm_w, m_w_out, v_norm_w, v_w_in, v_q_norm_w, v_k_norm_w, v_sinks, v_b_conv_w, v_b_conv_b, v_b_ln_w, v_b_ln_b, v_b_pw_w, v_b_pw_b, v_c_conv_w, v_c_a_log, v_c_dt_bias, v_c_onorm_w, v_w_out):
    W = dict(norm_w=norm_w, w_in=w_in, q_norm_w=q_norm_w, k_norm_w=k_norm_w, sinks=sinks, b_conv_w=b_conv_w, b_conv_b=b_conv_b,
             b_ln_w=b_ln_w, b_ln_b=b_ln_b, b_pw_w=b_pw_w, b_pw_b=b_pw_b, c_conv_w=c_conv_w, c_a_log=c_a_log,
             c_dt_bias=c_dt_bias, c_onorm_w=c_onorm_w, w_out=w_out)
    M = dict(norm_w=m_norm_w, w_in=m_w_in, q_norm_w=m_q_norm_w, k_norm_w=m_k_norm_w, sinks=m_sinks, b_conv_w=m_b_conv_w,
             b_conv_b=m_b_conv_b, b_ln_w=m_b_ln_w, b_ln_b=m_b_ln_b, b_pw_w=m_b_pw_w, b_pw_b=m_b_pw_b, c_conv_w=m_c_conv_w,
             c_a_log=m_c_a_log, c_dt_bias=m_c_dt_bias, c_onorm_w=m_c_onorm_w, w_out=m_w_out)
    V = dict(norm_w=v_norm_w, w_in=v_w_in, q_norm_w=v_q_norm_w, k_norm_w=v_k_norm_w, sinks=v_sinks, b_conv_w=v_b_conv_w,
             b_conv_b=v_b_conv_b, b_ln_w=v_b_ln_w, b_ln_b=v_b_ln_b, b_pw_w=v_b_pw_w, b_pw_b=v_b_pw_b, c_conv_w=v_c_conv_w,
             c_a_log=v_c_a_log, c_dt_bias=v_c_dt_bias, c_onorm_w=v_c_onorm_w, w_out=v_w_out)
    nseq, t, d = x.shape
    n = nseq * t
    tr = min(256, t)
    tmm = min(512, n)
    tmw = min(1024, n)
    tkk = min(2048, n)
    me = _dev_index(_my_pos())
    xs = [x.reshape(n, d)]
    tgt = loss_target.reshape(n, d)
    tabs = _rope_tables(positions.reshape(n))

    win_p = _pack_cols(w_in).astype(BF16)
    wout_b = w_out.astype(BF16)
    sharded_small = (b_pw_w, b_conv_w, c_conv_w)
    g_win0, g_small = _all_gather([win_p[0], _blob(sharded_small)])
    win = [g_win0]
    later = _exchange_start("gather", [win_p[1], wout_b[0], wout_b[1]], g_small, "gather_start")
    pw_all, cw_all, ccw_all = _unblob(g_small, [a.shape for a in sharded_small], lead=(N_DEV,))
    pw_all = pw_all.transpose(1, 0, 2, 3).reshape(DEPTH, B_W, B_W).astype(BF16)
    cw_all = cw_all.transpose(1, 2, 0, 3).reshape(DEPTH, B_K, B_W)
    ccw_all = ccw_all.transpose(1, 2, 0, 3).reshape(DEPTH, C_K, 3 * C_W)

    def layer_params(l):
        return dict(
            nw=norm_w[l][None], qw=jnp.tile(q_norm_w[l], 2)[None], kw=jnp.tile(k_norm_w[l], 2)[None], sinks=sinks[l],
            cw=jnp.pad(cw_all[l], ((0, HALO_B - B_K), (0, 0))), cb=b_conv_b[l][None], lnw=b_ln_w[l][None], lnb=b_ln_b[l][None],
            pw=pw_all[l], pwb=b_pw_b[l][None], ccw=jnp.pad(ccw_all[l], ((0, SUB - C_K), (0, 0))),
            alog=_lanes6(c_a_log[l]), dtb=_lanes6(c_dt_bias[l]), onw=c_onorm_w[l][None])

    saved = []
    for l in range(DEPTH):
        q = layer_params(l)
        nw = q["nw"] + later["token"][0:1, 0:1] if l == 0 else q["nw"]
        p, h = _inproj(xs[l], nw, win[l], tm=tmw)
        y, o_a, lse = _attn_fwd(p, tabs, q["qw"], q["kw"], q["sinks"], nseq)
        gates = _gdn_gates_fwd(p, q["alog"], q["dtb"], tm=tr)
        xc, qkv = _gdn_pre_fwd(p, q["ccw"], nseq, tm=tr)
        y, o_c, u, w, tinv, ss = _gdn_chunk_fwd(qkv, gates, p, y, q["onw"], nseq)
        y, hc = _conf_fwd(p, y, q["cw"], q["cb"], q["lnw"], q["lnb"], q["pw"], q["pwb"], nseq, tm=tr)
        saved.append(dict(q=q, p=p, h=h, y=y, o_a=o_a, lse=lse, gates=gates, xc=xc, qkv=qkv, o_c=o_c, u=u, w=w, tinv=tinv,
                          ss=ss, hc=hc))
        if l == 0:
            g_win1, g_wout0, g_wout1 = _exchange_wait(later, y, "gather_wait")
            win.append(g_win1)
            wout = [_y_rows(g_wout0), _y_rows(g_wout1)]
        if l + 1 < DEPTH:
            xs.append(_outproj(xs[l], y, wout[l], tm=tmw, tn=512))
        else:
            dxn, lsum = _outproj_loss(xs[l], y, wout[l], tgt, tm=tmw, tn=512)
    loss = lax.psum(jnp.sum(lsum) * (0.5 / d), ("x", "y", "c"))

    sent, smalls = [None] * DEPTH, [None] * DEPTH
    for l in reversed(range(DEPTH)):
        s = saved[l]
        q, p = s["q"], s["p"]
        dy = _matmul(dxn, wout[l], "nt", F32, tmw, 512, d, "outproj_bwd_dy")
        dwout = _y_rows_back(_matmul(s["y"], dxn, "tn", BF16, 1024, 1024, tkk, "outproj_bwd_dw"))
        dp, dkv, dqw, dkw, dsk = _attn_bwd(p, dy, s["o_a"], s["lse"], tabs, q["qw"], q["kw"], q["sinks"], nseq)
        dp = _put_cols(dp, dkv, P_K, tm=tmm)
        dp, dqkv, dgate, donw = _gdn_chunk_bwd(s["qkv"], s["gates"], p, dy, dp, q["onw"], s["o_c"], s["u"], s["w"],
                                               s["tinv"], s["ss"], nseq)
        dp, dal, ddb = _gdn_gates_bwd(dgate, p, q["alog"], q["dtb"], dp, tm=tr)
        dp, dccw = _gdn_pre_bwd(p, dqkv, s["xc"], dp, q["ccw"], nseq, tm=tr)
        dp, dhc, dpw, dpwb, dlnw, dlnb, dcb = _conf_bwd1(p, dy, dp, s["hc"], q["lnw"], q["lnb"], q["pw"], q["pwb"], tm=tr)
        dp, dcw = _conf_bwd2(p, dhc, dp, q["cw"], nseq, tm=tr)
        dwin = _matmul(s["h"], dp, "tn", BF16, 1024, 768, tkk, "inproj_bwd_dw")
        sent[l] = _exchange_start("scatter", [dwin, dwout, dpw], dpwb, "scatter_start_%d" % l)
        dxn, dnw = _inproj_bwd_dx(dp, win[l], xs[l], q["nw"] + sent[l]["token"][0:1, 0:1], dxn, tm=tmm)
        halves = lambda a: a.sum(0)[:A_DH] + a.sum(0)[A_DH:]
        smalls[l] = dict(
            norm_w=dnw.sum(0), q_norm_w=halves(dqw), k_norm_w=halves(dkw), sinks=dsk.sum(0)[:A_HEADS], b_conv_b=dcb.sum(0),
            b_ln_w=dlnw.sum(0), b_ln_b=dlnb.sum(0), b_pw_b=dpwb.sum(0), c_a_log=dal.sum(0)[GG:GG + C_HEADS],
            c_dt_bias=ddb.sum(0)[GG:GG + C_HEADS], c_onorm_w=donw.sum(0),
            b_conv_w=dcw.reshape(B_K, SUB, B_W).sum(1), c_conv_w=dccw.reshape(C_K, SUB, 3 * C_W).sum(1))
    grad_x = dxn.reshape(nseq, t, d)

    G, delta, new_m, new_v = {}, {}, {}, {}
    big = ("w_in", "w_out", "b_pw_w")
    stacks = {k: None for k in big}
    after = dxn
    for l in reversed(range(DEPTH)):
        r_win, r_wout, r_pw = _exchange_wait(sent[l], after, "scatter_wait_%d" % l)
        grads = dict(w_in=_unpack_cols(_sum8(r_win)), w_out=_sum8(r_wout), b_pw_w=_sum8(r_pw))
        for k in big:
            stacks[k] = _adamw_layer(W[k], grads[k], M[k], V[k], l, stacks[k])
        after = stacks["w_in"][1]
    for k in big:
        G[k], delta[k], new_m[k], new_v[k] = stacks[k]
    part = _blob([jnp.stack([smalls[l][k] for l in range(DEPTH)]) for k in SMALL])
    (tot,) = _all_gather([part], after=stacks["w_in"][1])
    tot = _sum8(tot.reshape(N_DEV, part.shape[0], LANES))
    full_shapes = [(DEPTH,) + smalls[0][k].shape for k in SMALL]
    for k, g in zip(SMALL, _unblob(tot, full_shapes)):
        G[k] = g
    G["b_conv_w"] = lax.dynamic_slice_in_dim(G["b_conv_w"], me * (B_W // N_DEV), B_W // N_DEV, axis=2)
    G["c_conv_w"] = lax.dynamic_slice_in_dim(G["c_conv_w"], me * (3 * C_W // N_DEV), 3 * C_W // N_DEV, axis=2)
    dl, mo, vo = _adamw(*[_blob([src[k] for k in SMALL]) for src in (W, G, M, V)])
    shapes = [W[k].shape for k in SMALL]
    for k, a, b, c in zip(SMALL, _unblob(dl, shapes), _unblob(mo, shapes), _unblob(vo, shapes)):
        delta[k], new_m[k], new_v[k] = a, b, c
    return (loss, grad_x, *[G[k] for k in ORDER], *[delta[k] for k in ORDER], *[new_m[k] for k in ORDER],
            *[new_v[k] for k in ORDER])
```

```python
import functools
import math

import jax
import jax.numpy as jnp
from jax import lax
from jax.experimental import pallas as pl
from jax.experimental.pallas import tpu as pltpu

F32 = jnp.float32
BF16 = jnp.bfloat16
HI = lax.Precision.HIGHEST
MESH = pl.DeviceIdType.MESH
S = jax.ShapeDtypeStruct
BS = pl.BlockSpec

N_DEV = 8
DEPTH = 2
D_MODEL = 2048
A_HEADS, A_KV, A_DH, A_W, A_KVW = 12, 4, 64, 768, 256
ROT = 16
THETA = 500000.0
ABLK = 128
B_W, B_K = 512, 31
C_HEADS, C_DH, C_W, C_K, CHUNK = 6, 128, 768, 4, 64
EPS = 1e-6
IN_COLS = 6668
P_Q, P_ZA, P_ZC, P_QKV, P_K, P_V, P_UB, P_ZB, P_BA, P_W = 0, 768, 1536, 2304, 4608, 4864, 5120, 6144, 6656, 6912
Y_A, Y_C, Y_B = 0, 768, 1536
LANES = 128
SUB = 8

ADAM_LR, ADAM_B1, ADAM_B2, ADAM_EPS, ADAM_WD, ADAM_STEP = 0.001, 0.9, 0.999, 1e-08, 0.01, 10


def _cp(*sem, vmem=None):
    kw = {}
    if sem:
        kw["dimension_semantics"] = sem
    if vmem:
        kw["vmem_limit_bytes"] = vmem
    return pltpu.CompilerParams(**kw)


def _pack_cols(w):
    z = jnp.zeros(w.shape[:-1] + (P_W - IN_COLS,), w.dtype)
    return jnp.concatenate([w[..., 0:768], w[..., 1280:2048], w[..., 5900:6668], w[..., 3584:5888],
                            w[..., 768:1024], w[..., 1024:1280], w[..., 2048:3072], w[..., 3072:3584],
                            w[..., 5888:5900], z], axis=-1)


def _unpack_cols(g, axis=-1, each=1):
    parts = ((P_Q, 768), (P_K, 256), (P_V, 256), (P_ZA, 768), (P_UB, 1024), (P_ZB, 512), (P_QKV, 2304), (P_BA, 12), (P_ZC, 768))
    return jnp.concatenate([lax.slice_in_dim(g, each * o, each * (o + n), axis=axis) for o, n in parts], axis=axis)


def _sigmoid(x):
    return 1.0 / (1.0 + jnp.exp(-x))


def _dsilu(x, sg):
    return sg * (1.0 + x * (1.0 - sg))


def _fold8(x):
    r, c = x.shape
    return x.reshape(r // SUB, SUB, c).sum(axis=0)


def _dot(a, b, prec=None):
    return jnp.dot(a, b, preferred_element_type=F32, precision=prec)


def _dot_nt(a, b, prec=None):
    return lax.dot_general(a, b, (((1,), (1,)), ((), ())), preferred_element_type=F32, precision=prec)


def _dot_tn(a, b, prec=None):
    return lax.dot_general(a, b, (((0,), (0,)), ((), ())), preferred_element_type=F32, precision=prec)


def _lane(shape):
    return lax.broadcasted_iota(jnp.int32, shape, 1)


def _subl(shape):
    return lax.broadcasted_iota(jnp.int32, shape, 0)


def _col(x, j):
    return jnp.sum(jnp.where(_lane(x.shape) == j, x, 0.0), axis=-1, keepdims=True)


def _inproj(x, nw, w, tm=512, tn=768):
    n, d = x.shape
    pw = w.shape[1]

    def body(x_ref, nw_ref, w_ref, p_ref, h_ref):
        @pl.when(pl.program_id(1) == 0)
        def _():
            xv = x_ref[...]
            r = lax.rsqrt(jnp.mean(xv * xv, axis=-1, keepdims=True) + EPS)
            h_ref[...] = (xv * r * nw_ref[...]).astype(BF16)

        p_ref[...] = _dot(h_ref[...], w_ref[...])

    return pl.pallas_call(
        body, name="inproj", grid=(n // tm, pw // tn),
        in_specs=[BS((tm, d), lambda i, j: (i, 0)), BS((1, d), lambda i, j: (0, 0)), BS((d, tn), lambda i, j: (0, j))],
        out_specs=[BS((tm, tn), lambda i, j: (i, j)), BS((tm, d), lambda i, j: (i, 0))],
        out_shape=[S((n, pw), F32), S((n, d), BF16)],
        compiler_params=_cp("arbitrary", "arbitrary"),
    )(x, nw, w)


def _outproj(x, y, w, tm=512, tn=1024):
    n, d = x.shape
    k = y.shape[1]

    def body(x_ref, y_ref, w_ref, o_ref):
        o_ref[...] = x_ref[...] + _dot(y_ref[...], w_ref[...])

    return pl.pallas_call(
        body, name="outproj", grid=(n // tm, d // tn),
        in_specs=[BS((tm, tn), lambda i, j: (i, j)), BS((tm, k), lambda i, j: (i, 0)), BS((k, tn), lambda i, j: (0, j))],
        out_specs=BS((tm, tn), lambda i, j: (i, j)),
        out_shape=S((n, d), F32),
        compiler_params=_cp("arbitrary", "arbitrary"),
    )(x, y, w)


def _outproj_loss(x, y, w, tgt, tm=512, tn=1024):
    n, d = x.shape
    k = y.shape[1]

    def body(x_ref, y_ref, w_ref, t_ref, g_ref, l_ref):
        @pl.when((pl.program_id(0) == 0) & (pl.program_id(1) == 0))
        def _():
            l_ref[...] = jnp.zeros_like(l_ref)

        diff = x_ref[...] + _dot(y_ref[...], w_ref[...]) - t_ref[...]
        g_ref[...] = diff * (1.0 / d)
        f = _fold8(diff * diff)
        acc = f[:, 0:LANES]
        for c in range(1, tn // LANES):
            acc = acc + f[:, c * LANES:(c + 1) * LANES]
        l_ref[...] += acc

    return pl.pallas_call(
        body, name="outproj_loss", grid=(n // tm, d // tn),
        in_specs=[BS((tm, tn), lambda i, j: (i, j)), BS((tm, k), lambda i, j: (i, 0)), BS((k, tn), lambda i, j: (0, j)),
                  BS((tm, tn), lambda i, j: (i, j))],
        out_specs=[BS((tm, tn), lambda i, j: (i, j)), BS((SUB, LANES), lambda i, j: (0, 0))],
        out_shape=[S((n, d), F32), S((SUB, LANES), F32)],
        compiler_params=_cp("arbitrary", "arbitrary"),
    )(x, y, w, tgt)


def _matmul(a, b, mode, out_dtype, tm, tn, tk, name):
    if mode == "nn":
        (m, kk), nn = a.shape, b.shape[1]
        a_spec, b_spec = BS((tm, tk), lambda i, j, k: (i, k)), BS((tk, tn), lambda i, j, k: (k, j))
        dot = _dot
    elif mode == "nt":
        (m, kk), nn = a.shape, b.shape[0]
        a_spec, b_spec = BS((tm, tk), lambda i, j, k: (i, k)), BS((tn, tk), lambda i, j, k: (j, k))
        dot = _dot_nt
    else:
        (kk, m), nn = a.shape, b.shape[1]
        a_spec, b_spec = BS((tk, tm), lambda i, j, k: (k, i)), BS((tk, tn), lambda i, j, k: (k, j))
        dot = _dot_tn
    nk = kk // tk

    def body(a_ref, b_ref, o_ref, acc_ref):
        kid = pl.program_id(2)

        @pl.when(kid == 0)
        def _():
            acc_ref[...] = jnp.zeros_like(acc_ref)

        acc_ref[...] += dot(a_ref[...].astype(BF16), b_ref[...].astype(BF16))

        @pl.when(kid == nk - 1)
        def _():
            o_ref[...] = acc_ref[...].astype(out_dtype)

    return pl.pallas_call(
        body, name=name, grid=(m // tm, nn // tn, nk),
        in_specs=[a_spec, b_spec], out_specs=BS((tm, tn), lambda i, j, k: (i, j)),
        out_shape=S((m, nn), out_dtype), scratch_shapes=[pltpu.VMEM((tm, tn), F32)],
        compiler_params=_cp("arbitrary", "arbitrary", "arbitrary"),
    )(a, b)


SLAB = 16


def _inproj_bwd_dx(dp, w, x, nw, dres, tm=512, tk=1152):
    n, d = x.shape
    nk = dp.shape[1] // tk

    def body(dp_ref, w_ref, x_ref, nw_ref, dr_ref, dx_ref, dnw_ref, acc_ref):
        kid = pl.program_id(1)

        @pl.when((pl.program_id(0) == 0) & (kid == 0))
        def _():
            dnw_ref[...] = jnp.zeros_like(dnw_ref)

        @pl.when(kid == 0)
        def _():
            acc_ref[...] = jnp.zeros_like(acc_ref)

        acc_ref[...] += _dot_nt(dp_ref[...], w_ref[...])

        @pl.when(kid == nk - 1)
        def _():
            def slab(i, carry):
                rows = pl.ds(pl.multiple_of(i * SLAB, SLAB), SLAB)
                dh, xv = acc_ref[rows, :], x_ref[rows, :]
                r = lax.rsqrt(jnp.mean(xv * xv, axis=-1, keepdims=True) + EPS)
                dnw_ref[...] += _fold8(dh * xv * r)
                g = dh * nw_ref[...]
                mm = jnp.mean(g * xv, axis=-1, keepdims=True)
                dx_ref[rows, :] = dr_ref[rows, :] + r * g - xv * (r * r * r * mm)
                return carry

            lax.fori_loop(0, tm // SLAB, slab, 0)

    return pl.pallas_call(
        body, name="inproj_bwd_dx", grid=(n // tm, nk),
        in_specs=[BS((tm, tk), lambda i, k: (i, k)), BS((d, tk), lambda i, k: (0, k)), BS((tm, d), lambda i, k: (i, 0)),
                  BS((1, d), lambda i, k: (0, 0)), BS((tm, d), lambda i, k: (i, 0))],
        out_specs=[BS((tm, d), lambda i, k: (i, 0)), BS((SUB, d), lambda i, k: (0, 0))],
        out_shape=[S((n, d), F32), S((SUB, d), F32)],
        scratch_shapes=[pltpu.VMEM((tm, d), F32)],
        compiler_params=_cp("arbitrary", "arbitrary"),
    )(dp, w, x, nw, dres)


def _rope_tables(pos):
    half = ROT // 2
    inv = THETA ** (-jnp.arange(0, ROT, 2, dtype=F32) / ROT)
    ang = pos.astype(F32)[:, None] * inv
    cos, sin = jnp.cos(ang), jnp.sin(ang)
    n = pos.shape[0]
    one = jnp.ones((n, A_DH - ROT), F32)
    zero = jnp.zeros((n, A_DH - ROT), F32)
    zh = jnp.zeros((n, half), F32)
    c = jnp.concatenate([cos, cos, one], axis=1)
    s1 = jnp.concatenate([-sin, zh, zero], axis=1)
    s2 = jnp.concatenate([zh, sin, zero], axis=1)
    return tuple(jnp.concatenate([t, t], axis=1) for t in (c, s1, s2))


def _half_stat(t):
    lo = _lane(t.shape) < A_DH
    s_lo = jnp.sum(jnp.where(lo, t, 0.0), axis=-1, keepdims=True)
    s_hi = jnp.sum(jnp.where(lo, 0.0, t), axis=-1, keepdims=True)
    return jnp.where(lo, s_lo, s_hi)


def _normrope(x, w, c, s1, s2):
    r = lax.rsqrt(_half_stat(x * x) * (1.0 / A_DH) + EPS)
    xn = x * r * w
    return xn * c + pltpu.roll(xn, LANES - ROT // 2, 1) * s1 + pltpu.roll(xn, ROT // 2, 1) * s2, r


def _normrope_bwd(dy, x, r, w, c, s1, s2):
    dxn = dy * c + pltpu.roll(dy * s1, ROT // 2, 1) + pltpu.roll(dy * s2, LANES - ROT // 2, 1)
    g = dxn * w
    mm = _half_stat(g * x) * (1.0 / A_DH)
    return r * g - x * (r * r * r * mm), dxn * x * r


def _attn_mask(first):
    qi = _subl((ABLK, 2 * ABLK))
    kj = _lane((ABLK, 2 * ABLK))
    dist = qi + ABLK - kj
    return (dist >= 0) & (dist < ABLK) & (jnp.logical_not(first) | (kj >= ABLK))


def _attn_fwd(p, tabs, qw, kw, sinks, nseq):
    n = p.shape[0]
    nb = n // nseq // ABLK
    cur = lambda b, i: (b * nb + i, 0)
    prv = lambda b, i: (b * nb + jnp.maximum(i - 1, 0), 0)
    colblk = lambda f, w, off: (lambda b, i: (f(b, i)[0], off // w))

    def body(q_ref, za_ref, kc_ref, vc_ref, kp_ref, vp_ref, c_ref, s1_ref, s2_ref, cp_ref, s1p_ref, s2p_ref,
             qw_ref, kw_ref, sink_ref, y_ref, o_ref, lse_ref):
        first = pl.program_id(1) == 0
        tc = (c_ref[...], s1_ref[...], s2_ref[...])
        tp = (cp_ref[...], s1p_ref[...], s2p_ref[...])
        q, kc, kp = q_ref[...], kc_ref[...], kp_ref[...]
        qn = [_normrope(q[:, LANES * b:LANES * (b + 1)], qw_ref[...], *tc)[0].astype(BF16) for b in range(A_W // LANES)]
        k2, v2 = [], []
        for b in range(A_KVW // LANES):
            sl = slice(LANES * b, LANES * (b + 1))
            k2.append(jnp.concatenate([_normrope(kp[:, sl], kw_ref[...], *tp)[0],
                                       _normrope(kc[:, sl], kw_ref[...], *tc)[0]], axis=0).astype(BF16))
            v2.append(jnp.concatenate([vp_ref[:, sl], vc_ref[:, sl]], axis=0).astype(BF16))
        valid = _attn_mask(first)
        heads = range(A_HEADS)
        half = lambda a, i: a[:, A_DH * (i % 2):A_DH * (i % 2 + 1)]
        kh = [half(k2[g // 2], g) for g in range(A_KV)]
        vh = [half(v2[g // 2], g) for g in range(A_KV)]
        s = [jnp.where(valid, _dot_nt(half(qn[j // 2], j), kh[j // 3]) * (A_DH ** -0.5), -jnp.inf) for j in heads]
        m = [jnp.maximum(jnp.max(s[j], axis=-1, keepdims=True), sink_ref[j]) for j in heads]
        e = [jnp.exp(s[j] - m[j]) for j in heads]
        den = [jnp.sum(e[j], axis=-1, keepdims=True) + jnp.exp(sink_ref[j] - m[j]) for j in heads]
        outs = [_dot((e[j] / den[j]).astype(BF16), vh[j // 3]) for j in heads]
        lse = jnp.zeros((ABLK, LANES), F32)
        for j in heads:
            lse = jnp.where(_lane(lse.shape) == j, m[j] + jnp.log(den[j]), lse)
        o = jnp.concatenate(outs, axis=1)
        za = za_ref[...]
        o_ref[...] = o
        lse_ref[...] = lse
        y_ref[...] = (o * (za * _sigmoid(za))).astype(BF16)

    tab_specs = [BS((ABLK, LANES), cur)] * 3 + [BS((ABLK, LANES), prv)] * 3
    return pl.pallas_call(
        body, name="attn_fwd", grid=(nseq, nb),
        in_specs=[BS((ABLK, A_W), colblk(cur, A_W, P_Q)), BS((ABLK, A_W), colblk(cur, A_W, P_ZA)),
                  BS((ABLK, A_KVW), colblk(cur, A_KVW, P_K)), BS((ABLK, A_KVW), colblk(cur, A_KVW, P_V)),
                  BS((ABLK, A_KVW), colblk(prv, A_KVW, P_K)), BS((ABLK, A_KVW), colblk(prv, A_KVW, P_V))]
        + tab_specs + [BS((1, LANES), lambda b, i: (0, 0))] * 2 + [BS(memory_space=pltpu.SMEM)],
        out_specs=[BS((ABLK, A_W), colblk(cur, A_W, Y_A)), BS((ABLK, A_W), cur), BS((ABLK, LANES), cur)],
        out_shape=[S((n, D_MODEL), BF16), S((n, A_W), F32), S((n, LANES), F32)],
        compiler_params=_cp("arbitrary", "arbitrary"),
    )(p, p, p, p, p, p, *tabs, *tabs, qw, kw, sinks)


def _attn_bwd(p, dy, o, lse, tabs, qw, kw, sinks, nseq):
    n = p.shape[0]
    nb = n // nseq // ABLK
    cur = lambda b, i: (b * nb + jnp.minimum(i, nb - 1), 0)
    prv = lambda b, i: (b * nb + jnp.maximum(i - 1, 0), 0)
    colblk = lambda f, w, off: (lambda b, i: (f(b, i)[0], off // w))

    def body(q_ref, za_ref, kc_ref, vc_ref, kp_ref, vp_ref, dy_ref, o_ref, lse_ref,
             c_ref, s1_ref, s2_ref, cp_ref, s1p_ref, s2p_ref, qw_ref, kw_ref, sink_ref,
             dqza_ref, dkv_ref, dqw_ref, dkw_ref, dsk_ref, tk_ref, tv_ref, ck_ref, cv_ref):
        i = pl.program_id(1)
        first = i == 0
        tc = (c_ref[...], s1_ref[...], s2_ref[...])
        tp = (cp_ref[...], s1p_ref[...], s2p_ref[...])
        nkb = A_KVW // LANES

        @pl.when((pl.program_id(0) == 0) & first)
        def _():
            dqw_ref[...] = jnp.zeros_like(dqw_ref)
            dkw_ref[...] = jnp.zeros_like(dkw_ref)
            dsk_ref[...] = jnp.zeros_like(dsk_ref)

        @pl.when(i < nb)
        def _():
            q, kc, kp = q_ref[...], kc_ref[...], kp_ref[...]
            qn, rq = [], []
            for b in range(A_W // LANES):
                a, r = _normrope(q[:, LANES * b:LANES * (b + 1)], qw_ref[...], *tc)
                qn.append(a.astype(BF16))
                rq.append(r)
            k2, v2 = [], []
            for b in range(nkb):
                sl = slice(LANES * b, LANES * (b + 1))
                k2.append(jnp.concatenate([_normrope(kp[:, sl], kw_ref[...], *tp)[0],
                                           _normrope(kc[:, sl], kw_ref[...], *tc)[0]], axis=0).astype(BF16))
                v2.append(jnp.concatenate([vp_ref[:, sl], vc_ref[:, sl]], axis=0).astype(BF16))
            valid = _attn_mask(first)
            za, dy, o, lse = za_ref[...], dy_ref[...], o_ref[...], lse_ref[...]
            sg = _sigmoid(za)
            do = dy * za * sg
            dqza_ref[:, A_W:2 * A_W] = (dy * o * _dsilu(za, sg)).astype(BF16)
            heads = range(A_HEADS)
            half = lambda a, i: a[:, A_DH * (i % 2):A_DH * (i % 2 + 1)]
            kh = [half(k2[g // 2], g) for g in range(A_KV)]
            vh = [half(v2[g // 2], g) for g in range(A_KV)]
            qh = [half(qn[j // 2], j) for j in heads]
            lj = [_col(lse, j) for j in heads]
            pr = [jnp.exp(jnp.where(valid, _dot_nt(qh[j], kh[j // 3]) * (A_DH ** -0.5), -jnp.inf) - lj[j]) for j in heads]
            doh = [do[:, A_DH * j:A_DH * (j + 1)] for j in heads]
            delta = [jnp.sum(doh[j] * o[:, A_DH * j:A_DH * (j + 1)], axis=-1, keepdims=True) for j in heads]
            dohb = [a.astype(BF16) for a in doh]
            ds = [(pr[j] * (_dot_nt(dohb[j], vh[j // 3]) - delta[j]) * (A_DH ** -0.5)).astype(BF16) for j in heads]
            dqs = [_dot(ds[j], kh[j // 3]) for j in heads]
            dkh = [_dot_tn(ds[j], qh[j]) for j in heads]
            dvh = [_dot_tn(pr[j].astype(BF16), dohb[j]) for j in heads]
            dks = [dkh[3 * g] + dkh[3 * g + 1] + dkh[3 * g + 2] for g in range(A_KV)]
            dvs = [dvh[3 * g] + dvh[3 * g + 1] + dvh[3 * g + 2] for g in range(A_KV)]
            dsk = jnp.zeros((ABLK, LANES), F32)
            for j in heads:
                dsk = dsk + jnp.where(_lane(dsk.shape) == j, -jnp.exp(sink_ref[j] - lj[j]) * delta[j], 0.0)
            dsk_ref[...] += _fold8(dsk)
            dqn = jnp.concatenate(dqs, axis=1)
            dqw = jnp.zeros((SUB, LANES), F32)
            dqo = []
            for b in range(A_W // LANES):
                sl = slice(LANES * b, LANES * (b + 1))
                dx, wt = _normrope_bwd(dqn[:, sl], q[:, sl], rq[b], qw_ref[...], *tc)
                dqo.append(dx)
                dqw = dqw + _fold8(wt)
            dqw_ref[...] += dqw
            dqza_ref[:, 0:A_W] = jnp.concatenate(dqo, axis=1).astype(BF16)
            tk_ref[...] = jnp.concatenate(dks, axis=1)
            tv_ref[...] = jnp.concatenate(dvs, axis=1)

        @pl.when(i == nb)
        def _():
            tk_ref[...] = jnp.zeros_like(tk_ref)
            tv_ref[...] = jnp.zeros_like(tv_ref)

        @pl.when(i > 0)
        def _():
            kp = kp_ref[...]
            dkn = ck_ref[...] + tk_ref[0:ABLK, :]
            dkw = jnp.zeros((SUB, LANES), F32)
            dko = []
            for b in range(nkb):
                sl = slice(LANES * b, LANES * (b + 1))
                r = _normrope(kp[:, sl], kw_ref[...], *tp)[1]
                dx, wt = _normrope_bwd(dkn[:, sl], kp[:, sl], r, kw_ref[...], *tp)
                dko.append(dx)
                dkw = dkw + _fold8(wt)
            dkw_ref[...] += dkw
            dkv_ref[:, 0:A_KVW] = jnp.concatenate(dko, axis=1).astype(BF16)
            dkv_ref[:, A_KVW:2 * A_KVW] = (cv_ref[...] + tv_ref[0:ABLK, :]).astype(BF16)

        ck_ref[...] = tk_ref[ABLK:2 * ABLK, :]
        cv_ref[...] = tv_ref[ABLK:2 * ABLK, :]

    tab_specs = [BS((ABLK, LANES), cur)] * 3 + [BS((ABLK, LANES), prv)] * 3
    acc = BS((SUB, LANES), lambda b, i: (0, 0))
    return pl.pallas_call(
        body, name="attn_bwd", grid=(nseq, nb + 1),
        in_specs=[BS((ABLK, A_W), colblk(cur, A_W, P_Q)), BS((ABLK, A_W), colblk(cur, A_W, P_ZA)),
                  BS((ABLK, A_KVW), colblk(cur, A_KVW, P_K)), BS((ABLK, A_KVW), colblk(cur, A_KVW, P_V)),
                  BS((ABLK, A_KVW), colblk(prv, A_KVW, P_K)), BS((ABLK, A_KVW), colblk(prv, A_KVW, P_V)),
                  BS((ABLK, A_W), colblk(cur, A_W, Y_A)), BS((ABLK, A_W), cur), BS((ABLK, LANES), cur)]
        + tab_specs + [BS((1, LANES), lambda b, i: (0, 0))] * 2 + [BS(memory_space=pltpu.SMEM)],
        out_specs=[BS((ABLK, 2 * A_W), cur), BS((ABLK, 2 * A_KVW), prv), acc, acc, acc],
        out_shape=[S((n, P_W), BF16), S((n, 2 * A_KVW), BF16)] + [S((SUB, LANES), F32)] * 3,
        scratch_shapes=[pltpu.VMEM((2 * ABLK, A_KVW), F32)] * 2 + [pltpu.VMEM((ABLK, A_KVW), F32)] * 2,
        compiler_params=_cp("arbitrary", "arbitrary"),
    )(p, p, p, p, p, p, dy, o, lse, *tabs, *tabs, qw, kw, sinks)


def _put_cols(dst, src, col_off, tm=512):
    n, w = src.shape

    def body(s_ref, d_in_ref, d_ref):
        d_ref[...] = s_ref[...]

    return pl.pallas_call(
        body, name="put_cols", grid=(n // tm,),
        in_specs=[BS((tm, w), lambda i: (i, 0)), BS(memory_space=pl.ANY)],
        out_specs=BS((tm, w), lambda i: (i, col_off // w)),
        out_shape=S(dst.shape, dst.dtype), input_output_aliases={1: 0},
        compiler_params=_cp("arbitrary"),
    )(src, dst)


HALO_B = 32


def _layernorm(hc, lnw, lnb):
    mu = jnp.mean(hc, axis=-1, keepdims=True)
    xc = hc - mu
    rstd = lax.rsqrt(jnp.mean(xc * xc, axis=-1, keepdims=True) + EPS)
    xhat = xc * rstd
    return xhat, rstd, xhat * lnw + lnb


def _shifted_copies(buf_ref, sh_ref):
    rows = sh_ref.shape[1]
    for b in range(1, SUB):
        sh_ref[b - 1] = buf_ref[pl.ds(b, rows), :]


def _rows_from(buf_ref, sh_ref, off, rows):
    a, b = divmod(off, SUB)
    if b == 0:
        return buf_ref[pl.ds(SUB * a, rows), :]
    return sh_ref[b - 1, pl.ds(SUB * a, rows), :]


def _conf_fwd(p, y, cw, cb, lnw, lnb, pw, pwb, nseq, tm=256):
    n = p.shape[0]
    t = n // nseq
    nt = t // tm
    row = lambda b, i: b * nt + i
    halo = lambda b, i: jnp.maximum((b * t + i * tm) // HALO_B - 1, 0)
    vec = BS((1, B_W), lambda b, i: (0, 0))

    def body(ub_ref, uh_ref, zb_ref, cw_ref, cb_ref, lnw_ref, lnb_ref, pw_ref, pwb_ref, y_in_ref, y_ref, hc_ref, buf_ref, sh_ref):
        ub, uh = ub_ref[...], uh_ref[...]
        hh = uh[:, :B_W] * _sigmoid(uh[:, B_W:])
        buf_ref[0:HALO_B, :] = jnp.where(pl.program_id(1) > 0, hh, 0.0)
        buf_ref[HALO_B:, :] = ub[:, :B_W] * _sigmoid(ub[:, B_W:])
        _shifted_copies(buf_ref, sh_ref)
        hc = jnp.zeros((tm, B_W), F32) + cb_ref[...]
        for k in range(B_K):
            hc = hc + cw_ref[k:k + 1, :] * _rows_from(buf_ref, sh_ref, HALO_B - B_K + 1 + k, tm)
        hc_ref[...] = hc
        ln = _layernorm(hc, lnw_ref[...], lnb_ref[...])[2]
        sw = ln * _sigmoid(ln)
        ob = _dot(sw.astype(BF16), pw_ref[...]) + pwb_ref[...]
        zb = zb_ref[...]
        y_ref[...] = (ob * (zb * _sigmoid(zb))).astype(BF16)

    return pl.pallas_call(
        body, name="conf_fwd", grid=(nseq, nt),
        in_specs=[BS((tm, 2 * B_W), lambda b, i: (row(b, i), P_UB // (2 * B_W))),
                  BS((HALO_B, 2 * B_W), lambda b, i: (halo(b, i), P_UB // (2 * B_W))),
                  BS((tm, B_W), lambda b, i: (row(b, i), P_ZB // B_W)),
                  BS((HALO_B, B_W), lambda b, i: (0, 0)), vec, vec, vec, BS((B_W, B_W), lambda b, i: (0, 0)), vec,
                  BS(memory_space=pl.ANY)],
        out_specs=[BS((tm, B_W), lambda b, i: (row(b, i), Y_B // B_W)), BS((tm, B_W), lambda b, i: (row(b, i), 0))],
        out_shape=[S(y.shape, y.dtype), S((n, B_W), F32)], input_output_aliases={9: 0},
        scratch_shapes=[pltpu.VMEM((HALO_B + tm, B_W), F32), pltpu.VMEM((SUB - 1, HALO_B + tm - SUB, B_W), F32)],
        compiler_params=_cp("arbitrary", "arbitrary"),
    )(p, p, p, cw, cb, lnw, lnb, pw, pwb, y)


def _conf_bwd1(p, dy, dp, hc, lnw, lnb, pw, pwb, tm=256):
    n = p.shape[0]
    vec = BS((1, B_W), lambda i: (0, 0))
    acc = BS((SUB, B_W), lambda i: (0, 0))

    def body(dy_ref, zb_ref, hc_ref, lnw_ref, lnb_ref, pw_ref, pwb_ref, dp_in_ref,
             dzb_ref, dhc_ref, dpw_ref, dpwb_ref, dlnw_ref, dlnb_ref, dcb_ref):
        @pl.when(pl.program_id(0) == 0)
        def _():
            for r in (dpw_ref, dpwb_ref, dlnw_ref, dlnb_ref, dcb_ref):
                r[...] = jnp.zeros_like(r)

        xhat, rstd, ln = _layernorm(hc_ref[...], lnw_ref[...], lnb_ref[...])
        sgl = _sigmoid(ln)
        sw = (ln * sgl).astype(BF16)
        ob = _dot(sw, pw_ref[...]) + pwb_ref[...]
        dy, zb = dy_ref[...], zb_ref[...]
        sgz = _sigmoid(zb)
        dzb_ref[...] = (dy * ob * _dsilu(zb, sgz)).astype(BF16)
        dob = dy * zb * sgz
        dobb = dob.astype(BF16)
        dpwb_ref[...] += _fold8(dob)
        dpw_ref[...] += _dot_tn(sw, dobb)
        dln = _dot_nt(dobb, pw_ref[...]) * _dsilu(ln, sgl)
        dlnw_ref[...] += _fold8(dln * xhat)
        dlnb_ref[...] += _fold8(dln)
        dxh = dln * lnw_ref[...]
        dhc = rstd * (dxh - jnp.mean(dxh, axis=-1, keepdims=True) - xhat * jnp.mean(dxh * xhat, axis=-1, keepdims=True))
        dcb_ref[...] += _fold8(dhc)
        dhc_ref[...] = dhc

    return pl.pallas_call(
        body, name="conf_bwd1", grid=(n // tm,),
        in_specs=[BS((tm, B_W), lambda i: (i, Y_B // B_W)), BS((tm, B_W), lambda i: (i, P_ZB // B_W)),
                  BS((tm, B_W), lambda i: (i, 0)), vec, vec, BS((B_W, B_W), lambda i: (0, 0)), vec,
                  BS(memory_space=pl.ANY)],
        out_specs=[BS((tm, B_W), lambda i: (i, P_ZB // B_W)), BS((tm, B_W), lambda i: (i, 0)),
                   BS((B_W, B_W), lambda i: (0, 0)), acc, acc, acc, acc],
        out_shape=[S(dp.shape, dp.dtype), S((n, B_W), F32), S((B_W, B_W), F32)] + [S((SUB, B_W), F32)] * 4,
        input_output_aliases={7: 0},
        compiler_params=_cp("arbitrary"),
    )(dy, p, hc, lnw, lnb, pw, pwb, dp)


def _conf_bwd2(p, dhc, dp, cw, nseq, tm=256):
    n = p.shape[0]
    t = n // nseq
    nt = t // tm
    row = lambda b, i: b * nt + i
    prev = lambda b, i: jnp.maximum((b * t + i * tm) // HALO_B - 1, 0)
    nxt = lambda b, i: jnp.minimum((b * t + (i + 1) * tm) // HALO_B, n // HALO_B - 1)

    def body(ub_ref, uh_ref, dh_ref, dn_ref, cw_ref, dp_in_ref, dub_ref, dcw_ref, buf_ref, dbuf_ref, sh_ref, dsh_ref):
        i = pl.program_id(1)

        @pl.when((pl.program_id(0) == 0) & (i == 0))
        def _():
            dcw_ref[...] = jnp.zeros_like(dcw_ref)

        ub, uh = ub_ref[...], uh_ref[...]
        a, sg = ub[:, :B_W], _sigmoid(ub[:, B_W:])
        buf_ref[0:HALO_B, :] = jnp.where(i > 0, uh[:, :B_W] * _sigmoid(uh[:, B_W:]), 0.0)
        buf_ref[HALO_B:, :] = a * sg
        dhc = dh_ref[...]
        dbuf_ref[0:tm, :] = dhc
        dbuf_ref[tm:, :] = jnp.where(i < nt - 1, dn_ref[...], 0.0)
        _shifted_copies(buf_ref, sh_ref)
        _shifted_copies(dbuf_ref, dsh_ref)
        dhg = jnp.zeros((tm, B_W), F32)
        for k in range(B_K):
            dhg = dhg + cw_ref[k:k + 1, :] * _rows_from(dbuf_ref, dsh_ref, B_K - 1 - k, tm)
            dcw_ref[SUB * k:SUB * (k + 1), :] += _fold8(dhc * _rows_from(buf_ref, sh_ref, HALO_B - B_K + 1 + k, tm))
        dub_ref[...] = jnp.concatenate([dhg * sg, dhg * a * sg * (1.0 - sg)], axis=1).astype(BF16)

    return pl.pallas_call(
        body, name="conf_bwd2", grid=(nseq, nt),
        in_specs=[BS((tm, 2 * B_W), lambda b, i: (row(b, i), P_UB // (2 * B_W))),
                  BS((HALO_B, 2 * B_W), lambda b, i: (prev(b, i), P_UB // (2 * B_W))),
                  BS((tm, B_W), lambda b, i: (row(b, i), 0)), BS((HALO_B, B_W), lambda b, i: (nxt(b, i), 0)),
                  BS((HALO_B, B_W), lambda b, i: (0, 0)), BS(memory_space=pl.ANY)],
        out_specs=[BS((tm, 2 * B_W), lambda b, i: (row(b, i), P_UB // (2 * B_W))),
                   BS((SUB * B_K, B_W), lambda b, i: (0, 0))],
        out_shape=[S(dp.shape, dp.dtype), S((SUB * B_K, B_W), F32)], input_output_aliases={5: 0},
        scratch_shapes=[pltpu.VMEM((HALO_B + tm, B_W), F32)] * 2 + [pltpu.VMEM((SUB - 1, HALO_B + tm - SUB, B_W), F32)] * 2,
        compiler_params=_cp("arbitrary", "arbitrary"),
    )(p, p, dhc, dhc, cw, dp)


HALO_C = 8
QS = C_DH ** -0.5
NCB = 3 * C_HEADS
CB0 = P_QKV // LANES
ZC0 = P_ZC // LANES
GB, GG = 0, C_HEADS


def _softplus(z):
    return jnp.maximum(z, 0.0) + jnp.log(1.0 + jnp.exp(-jnp.abs(z)))


def _gdn_gates_fwd(p, alog_l, dtb_l, tm=256):
    n = p.shape[0]

    def body(ba_ref, al_ref, db_ref, o_ref):
        blk = ba_ref[...]
        lane = _lane(blk.shape)
        g = jnp.where((lane >= GG) & (lane < GG + C_HEADS), -jnp.exp(al_ref[...]) * _softplus(blk + db_ref[...]), 0.0)
        tri = (_subl((CHUNK, CHUNK)) >= _lane((CHUNK, CHUNK))).astype(F32)
        gc = jnp.concatenate([_dot(tri, g[CHUNK * c:CHUNK * (c + 1)], HI) for c in range(tm // CHUNK)], axis=0)
        o_ref[...] = jnp.where(lane < GG, _sigmoid(blk), gc)

    return pl.pallas_call(
        body, name="gdn_gates_fwd", grid=(n // tm,),
        in_specs=[BS((tm, LANES), lambda i: (i, P_BA // LANES)), BS((1, LANES), lambda i: (0, 0)), BS((1, LANES), lambda i: (0, 0))],
        out_specs=BS((tm, LANES), lambda i: (i, 0)), out_shape=S((n, LANES), F32),
        compiler_params=_cp("arbitrary"),
    )(p, alog_l, dtb_l)


def _gdn_pre_fwd(p, ccw, nseq, tm=256):
    n = p.shape[0]
    t = n // nseq
    nt = t // tm
    row = lambda b, i: b * nt + i
    halo = lambda b, i: jnp.maximum((b * t + i * tm) // HALO_C - 1, 0)

    def body(x_ref, xh_ref, w_ref, xc_ref, o_ref, buf_ref):
        buf_ref[0:HALO_C, :] = jnp.where(pl.program_id(1) > 0, xh_ref[...], 0.0)
        buf_ref[HALO_C:, :] = x_ref[...]
        for c in range(NCB):
            cs = slice(LANES * c, LANES * (c + 1))
            xc = jnp.zeros((tm, LANES), F32)
            for k in range(C_K):
                xc = xc + w_ref[k:k + 1, cs] * buf_ref[pl.ds(HALO_C - C_K + 1 + k, tm), cs]
            xc_ref[:, cs] = xc
            act = xc * _sigmoid(xc)
            if c < 2 * C_HEADS:
                act = act * (lax.rsqrt(jnp.sum(act * act, axis=-1, keepdims=True) + EPS) * (QS if c < C_HEADS else 1.0))
            o_ref[:, cs] = act

    wide = 3 * C_W
    return pl.pallas_call(
        body, name="gdn_pre_fwd", grid=(nseq, nt),
        in_specs=[BS((tm, wide), lambda b, i: (row(b, i), P_QKV // wide)), BS((HALO_C, wide), lambda b, i: (halo(b, i), P_QKV // wide)),
                  BS((SUB, wide), lambda b, i: (0, 0))],
        out_specs=[BS((tm, wide), lambda b, i: (row(b, i), 0))] * 2,
        out_shape=[S((n, wide), F32)] * 2,
        scratch_shapes=[pltpu.VMEM((HALO_C + tm, wide), F32)],
        compiler_params=_cp("arbitrary", "arbitrary"),
    )(p, p, ccw)


def _chunk_common(q, k, gt, gtt, h):
    beta = _col(gt, GB + h)
    gc = _col(gt, GG + h)
    gcr = gtt[GG + h:GG + h + 1, :]
    ii, jj = _subl((CHUNK, CHUNK)), _lane((CHUNK, CHUNK))
    incl, strict = ii >= jj, ii > jj
    dec = jnp.exp(jnp.where(incl, gc - gcr, -jnp.inf))
    kb = k * beta
    kbf = k.astype(BF16)
    a = jnp.where(strict, _dot_nt(kb.astype(BF16), kbf) * dec, 0.0)
    mq = jnp.where(incl, _dot_nt(q.astype(BF16), kbf) * dec, 0.0)
    glast = jnp.sum(jnp.where(_subl(gc.shape) == CHUNK - 1, gc, 0.0), axis=0, keepdims=True)
    return beta, gc, incl, strict, dec, kb, a, mq, glast


def _split(x):
    hi = x.astype(BF16)
    return hi, (x - hi.astype(F32)).astype(BF16)


def _dot3(dot, a, b):
    (ah, al), (bh, bl) = a, b
    return dot(ah, bh) + (dot(ah, bl) + dot(al, bh))


def _unit_lower_inverses(mats):
    eye = (_subl(mats[0].shape) == _lane(mats[0].shape)).astype(F32)
    ms = [-a for a in mats]
    invs = [eye + m for m in ms]
    parts = [_split(m) for m in ms]
    for _ in range(5):
        ms = [_dot3(_dot, s, s) for s in parts]
        parts = [_split(m) for m in ms]
        invs = [inv + _dot3(_dot, _split(inv), s) for inv, s in zip(invs, parts)]
    return invs


def _gdn_chunk_fwd(qkv, gates, p, y, onw, nseq, tt=512):
    n = qkv.shape[0]
    t = n // nseq
    tt = min(tt, t)
    nt = t // tt
    nch = tt // CHUNK

    def body(q_ref, k_ref, v_ref, g_ref, zc_ref, onw_ref, y_in_ref, y_ref, o_ref, u_ref, w_ref, t_ref, ss_ref, s_scr):
        @pl.when(pl.program_id(1) == 0)
        def _():
            s_scr[...] = jnp.zeros_like(s_scr)

        def step(c, carry):
            rows = pl.ds(pl.multiple_of(c * CHUNK, CHUNK), CHUNK)
            gt = g_ref[rows, :]
            gtt = gt.T
            heads = range(C_HEADS)
            hs = [slice(C_DH * h, C_DH * (h + 1)) for h in heads]
            q, k, v = ([r[rows, hs[h]] for h in heads] for r in (q_ref, k_ref, v_ref))
            cm = [_chunk_common(q[h], k[h], gt, gtt, h) for h in heads]
            beta, gc, kb, mq, glast = ([m[i] for m in cm] for i in (0, 1, 5, 7, 8))
            tinv = _unit_lower_inverses([m[6] for m in cm])
            egc = [jnp.exp(g) for g in gc]
            sol = [_dot3(_dot, _split(tinv[h]), _split(jnp.concatenate([v[h] * beta[h], kb[h] * egc[h]], axis=1))) for h in heads]
            sv = [s_scr[h] for h in heads]
            sb = [s.astype(BF16) for s in sv]
            vnb = [(sol[h][:, :C_DH] - _dot(sol[h][:, C_DH:].astype(BF16), sb[h])).astype(BF16) for h in heads]
            o = [_dot((q[h] * egc[h]).astype(BF16), sb[h]) + _dot(mq[h].astype(BF16), vnb[h]) for h in heads]
            for h in heads:
                ss_ref[h, c] = sv[h]
                s_scr[h] = sv[h] * jnp.exp(glast[h]) + _dot_tn((k[h] * jnp.exp(glast[h] - gc[h])).astype(BF16), vnb[h])
            for h in heads:
                o_ref[rows, hs[h]] = o[h]
                u_ref[rows, hs[h]] = sol[h][:, :C_DH]
                w_ref[rows, hs[h]] = sol[h][:, C_DH:]
                t_ref[rows, hs[h]] = jnp.concatenate([tinv[h], jnp.zeros_like(tinv[h])], axis=1)
                zc = zc_ref[rows, hs[h]]
                r = lax.rsqrt(jnp.mean(o[h] * o[h], axis=-1, keepdims=True) + EPS)
                y_ref[rows, hs[h]] = (o[h] * r * onw_ref[...] * (zc * _sigmoid(zc))).astype(BF16)
            return carry

        lax.fori_loop(0, nch, step, 0)

    row = lambda b, i: b * nt + i
    wb = lambda col: BS((tt, C_W), lambda b, i: (row(b, i), col))
    return pl.pallas_call(
        body, name="gdn_chunk_fwd", grid=(nseq, nt),
        in_specs=[wb(0), wb(1), wb(2), BS((tt, LANES), lambda b, i: (row(b, i), 0)), wb(P_ZC // C_W),
                  BS((1, LANES), lambda b, i: (0, 0)), BS(memory_space=pl.ANY)],
        out_specs=[wb(Y_C // C_W), wb(0), wb(0), wb(0), wb(0),
                   BS((None, C_HEADS, nch, C_DH, C_DH), lambda b, i: (b, 0, i, 0, 0))],
        out_shape=[S(y.shape, y.dtype)] + [S((n, C_W), F32)] * 4 + [S((nseq, C_HEADS, t // CHUNK, C_DH, C_DH), F32)],
        input_output_aliases={6: 0},
        scratch_shapes=[pltpu.VMEM((C_HEADS, C_DH, C_DH), F32)],
        compiler_params=_cp("arbitrary", "arbitrary"),
    )(qkv, qkv, qkv, gates, p, onw, y)


def _gdn_chunk_bwd(qkv, gates, p, dy, dp, onw, o, u, w, tinv, ss, nseq, tt=256):
    n = qkv.shape[0]
    t = n // nseq
    tt = min(tt, t)
    nt = t // tt
    nch = tt // CHUNK

    def body(q_ref, k_ref, v_ref, g_ref, zc_ref, onw_ref, o_ref, dy_ref, u_ref, w_ref, t_ref, ss_ref, dp_in_ref,
             dzc_ref, dqkv_ref, dg_ref, donw_ref, ds_scr):
        @pl.when(pl.program_id(1) == 0)
        def _():
            ds_scr[...] = jnp.zeros_like(ds_scr)

        @pl.when((pl.program_id(0) == 0) & (pl.program_id(1) == 0))
        def _():
            donw_ref[...] = jnp.zeros_like(donw_ref)

        def rsum(x):
            return jnp.sum(x, axis=-1, keepdims=True)

        def step(ci, carry):
            c = nch - 1 - ci
            rows = pl.ds(pl.multiple_of(c * CHUNK, CHUNK), CHUNK)
            gt = g_ref[rows, :]
            gtt = gt.T
            live = [head(c, rows, gt, gtt, h) for h in range(C_HEADS)]
            while live:
                live = [g for g in live if next(g, False)]
            return carry

        def head(c, rows, gt, gtt, h):
            hs = slice(C_DH * h, C_DH * (h + 1))
            q, k, v = q_ref[rows, hs], k_ref[rows, hs], v_ref[rows, hs]
            zc, o, dy, u, w = zc_ref[rows, hs], o_ref[rows, hs], dy_ref[rows, hs], u_ref[rows, hs], w_ref[rows, hs]
            tm_ = t_ref[rows, hs][:, 0:CHUNK]
            sv, dsv = ss_ref[h, c], ds_scr[h]
            sb, dsb = sv.astype(BF16), dsv.astype(BF16)
            sg = _sigmoid(zc)
            r = lax.rsqrt(jnp.mean(o * o, axis=-1, keepdims=True) + EPS)
            on = o * r
            ow = onw_ref[...]
            dzc_ref[rows, hs] = (dy * on * ow * _dsilu(zc, sg)).astype(BF16)
            t1 = dy * zc * sg
            donw_ref[...] += _fold8(t1 * on)
            don = t1 * ow
            do = r * (don - on * jnp.mean(don * on, axis=-1, keepdims=True))
            dob = do.astype(BF16)
            yield True
            beta, gc, incl, strict, dec, kb, a, mq, glast = _chunk_common(q, k, gt, gtt, h)
            egc = jnp.exp(gc)
            gl = jnp.exp(glast)
            ekd = jnp.exp(glast - gc)
            wb = w.astype(BF16)
            vnb = (u - _dot(wb, sb)).astype(BF16)
            qg = q * egc
            yield True
            dvn = _dot_tn(mq.astype(BF16), dob) + _dot((k * ekd).astype(BF16), dsb)
            dvnb = dvn.astype(BF16)
            dqg = _dot_nt(dob, sb)
            yield True
            dmq = jnp.where(incl, _dot_nt(dob, vnb), 0.0)
            dkd = _dot_nt(vnb, dsb)
            dgl = jnp.sum(rsum(dsv * sv), axis=0, keepdims=True)
            dw = -_dot_nt(dvnb, sb)
            yield True
            ds_scr[h] = gl * dsv + _dot_tn(qg.astype(BF16), dob) - _dot_tn(wb, dvnb)
            db = _dot3(_dot_tn, _split(tm_), _split(jnp.concatenate([dvn, dw], axis=1)))
            dbv, dbk = db[:, :C_DH], db[:, C_DH:]
            yield True
            da = -jnp.where(strict, _dot3(_dot_nt, _split(dbv), _split(u)) + _dot3(_dot_nt, _split(dbk), _split(w)), 0.0)
            yield True
            e = da * a + dmq * mq
            dgc = rsum(e) - rsum(e.T)
            dgb, dhb, kbf = (da * dec).astype(BF16), (dmq * dec).astype(BF16), k.astype(BF16)
            dkb = _dot(dgb, kbf)
            tk = rsum(dbk * k)
            rk = rsum(dkd * k) * ekd
            dq = _dot(dhb, kbf) + egc * dqg
            dk = _dot_tn(dgb, kb.astype(BF16)) + _dot_tn(dhb, q.astype(BF16)) + beta * (egc * dbk + dkb) + ekd * dkd
            dbeta = rsum(dbv * v) + tk * egc + rsum(dkb * k)
            dgc = dgc + tk * beta * egc + egc * rsum(dqg * q) - rk
            dglast = jnp.sum(rk, axis=0, keepdims=True) + dgl * gl
            dgc = dgc + jnp.where(_subl(dgc.shape) == CHUNK - 1, dglast, 0.0)
            dqkv_ref[0, rows, hs] = dq
            dqkv_ref[1, rows, hs] = dk
            dqkv_ref[2, rows, hs] = beta * dbv
            lane = _lane((CHUNK, LANES))
            dg_ref[h, rows, :] = jnp.where(lane == 0, dbeta, jnp.where(lane == 1, dgc, 0.0))

        lax.fori_loop(0, nch, step, 0)

    row = lambda b, i: b * nt + nt - 1 - i
    wb = lambda col: BS((tt, C_W), lambda b, i: (row(b, i), col))
    return pl.pallas_call(
        body, name="gdn_chunk_bwd", grid=(nseq, nt),
        in_specs=[wb(0), wb(1), wb(2), BS((tt, LANES), lambda b, i: (row(b, i), 0)), wb(P_ZC // C_W),
                  BS((1, LANES), lambda b, i: (0, 0)), wb(0), wb(Y_C // C_W), wb(0), wb(0), wb(0),
                  BS((None, C_HEADS, nch, C_DH, C_DH), lambda b, i: (b, 0, nt - 1 - i, 0, 0)), BS(memory_space=pl.ANY)],
        out_specs=[wb(P_ZC // C_W), BS((3, tt, C_W), lambda b, i: (0, row(b, i), 0)),
                   BS((C_HEADS, tt, LANES), lambda b, i: (0, row(b, i), 0)), BS((SUB, LANES), lambda b, i: (0, 0))],
        out_shape=[S(dp.shape, dp.dtype), S((3, n, C_W), F32), S((C_HEADS, n, LANES), F32), S((SUB, LANES), F32)],
        input_output_aliases={12: 0},
        scratch_shapes=[pltpu.VMEM((C_HEADS, C_DH, C_DH), F32)],
        compiler_params=_cp("arbitrary", "arbitrary"),
    )(qkv, qkv, qkv, gates, p, onw, o, dy, u, w, tinv, ss, dp)


def _gdn_gates_bwd(dgate, p, alog_l, dtb_l, dp, tm=256):
    n = p.shape[0]
    acc = BS((SUB, LANES), lambda i: (0, 0))

    def body(dg_ref, ba_ref, al_ref, db_ref, dp_in_ref, dba_ref, dal_ref, ddb_ref):
        @pl.when(pl.program_id(0) == 0)
        def _():
            dal_ref[...] = jnp.zeros_like(dal_ref)
            ddb_ref[...] = jnp.zeros_like(ddb_ref)

        blk = ba_ref[...]
        lane = _lane(blk.shape)
        dbeta = jnp.zeros_like(blk)
        dgc = jnp.zeros_like(blk)
        for h in range(C_HEADS):
            dbeta = dbeta + jnp.where(lane == GB + h, _col(dg_ref[h], 0), 0.0)
            dgc = dgc + jnp.where(lane == GG + h, _col(dg_ref[h], 1), 0.0)
        tri = (_subl((CHUNK, CHUNK)) <= _lane((CHUNK, CHUNK))).astype(F32)
        dg = jnp.concatenate([_dot(tri, dgc[CHUNK * c:CHUNK * (c + 1)], HI) for c in range(tm // CHUNK)], axis=0)
        beta = _sigmoid(blk)
        z = blk + db_ref[...]
        ea = jnp.exp(al_ref[...])
        isg = (lane >= GG) & (lane < GG + C_HEADS)
        dz = jnp.where(isg, -dg * ea * _sigmoid(z), 0.0)
        dal_ref[...] += _fold8(jnp.where(isg, -dg * ea * _softplus(z), 0.0))
        ddb_ref[...] += _fold8(dz)
        out = jnp.where(lane < GG, dbeta * beta * (1.0 - beta), dz)
        dba_ref[...] = jnp.concatenate([out, jnp.zeros_like(out)], axis=1).astype(BF16)

    return pl.pallas_call(
        body, name="gdn_gates_bwd", grid=(n // tm,),
        in_specs=[BS((C_HEADS, tm, LANES), lambda i: (0, i, 0)), BS((tm, LANES), lambda i: (i, P_BA // LANES)),
                  BS((1, LANES), lambda i: (0, 0)), BS((1, LANES), lambda i: (0, 0)), BS(memory_space=pl.ANY)],
        out_specs=[BS((tm, 2 * LANES), lambda i: (i, P_BA // (2 * LANES))), acc, acc],
        out_shape=[S(dp.shape, dp.dtype), S((SUB, LANES), F32), S((SUB, LANES), F32)],
        input_output_aliases={4: 0},
        compiler_params=_cp("arbitrary"),
    )(dgate, p, alog_l, dtb_l, dp)


def _gdn_pre_bwd(p, dqkv, xc, dp, ccw, nseq, tm=256):
    n = p.shape[0]
    t = n // nseq
    nt = t // tm
    wide = 3 * C_W
    row = lambda b, i: b * nt + i
    prev = lambda b, i: jnp.maximum((b * t + i * tm) // HALO_C - 1, 0)
    nxt = lambda b, i: jnp.minimum((b * t + (i + 1) * tm) // HALO_C, n // HALO_C - 1)

    def d_conv_out(d, xc, part):
        sg = _sigmoid(xc)
        act = xc * sg
        if part < 2:
            cs = QS if part == 0 else 1.0
            rn = lax.rsqrt(jnp.sum(act * act, axis=-1, keepdims=True) + EPS)
            d = cs * rn * d - act * (cs * rn * rn * rn * jnp.sum(d * act, axis=-1, keepdims=True))
        return d * _dsilu(xc, sg)

    def body(x_ref, xh_ref, d_ref, dn_ref, xc_ref, xn_ref, w_ref, dp_in_ref, dx_ref, dw_ref, buf_ref, dbuf_ref):
        i = pl.program_id(1)

        @pl.when((pl.program_id(0) == 0) & (i == 0))
        def _():
            dw_ref[...] = jnp.zeros_like(dw_ref)

        buf_ref[0:HALO_C, :] = jnp.where(i > 0, xh_ref[...], 0.0)
        buf_ref[HALO_C:, :] = x_ref[...]
        for c in range(NCB):
            cs = slice(LANES * c, LANES * (c + 1))
            part, hd = divmod(c, C_HEADS)
            hs = slice(LANES * hd, LANES * (hd + 1))
            d = d_conv_out(d_ref[part, :, hs], xc_ref[:, cs], part)
            dbuf_ref[0:tm, cs] = d
            dbuf_ref[tm:, cs] = jnp.where(i < nt - 1, d_conv_out(dn_ref[part, :, hs], xn_ref[:, cs], part), 0.0)
            dx = jnp.zeros((tm, LANES), F32)
            for k in range(C_K):
                dx = dx + w_ref[k:k + 1, cs] * dbuf_ref[pl.ds(C_K - 1 - k, tm), cs]
                dw_ref[SUB * k:SUB * (k + 1), cs] += _fold8(d * buf_ref[pl.ds(HALO_C - C_K + 1 + k, tm), cs])
            dx_ref[:, cs] = dx.astype(BF16)

    return pl.pallas_call(
        body, name="gdn_pre_bwd", grid=(nseq, nt),
        in_specs=[BS((tm, wide), lambda b, i: (row(b, i), P_QKV // wide)), BS((HALO_C, wide), lambda b, i: (prev(b, i), P_QKV // wide)),
                  BS((3, tm, C_W), lambda b, i: (0, row(b, i), 0)), BS((3, HALO_C, C_W), lambda b, i: (0, nxt(b, i), 0)),
                  BS((tm, wide), lambda b, i: (row(b, i), 0)), BS((HALO_C, wide), lambda b, i: (nxt(b, i), 0)),
                  BS((SUB, wide), lambda b, i: (0, 0)), BS(memory_space=pl.ANY)],
        out_specs=[BS((tm, wide), lambda b, i: (row(b, i), P_QKV // wide)), BS((SUB * C_K, wide), lambda b, i: (0, 0))],
        out_shape=[S(dp.shape, dp.dtype), S((SUB * C_K, wide), F32)], input_output_aliases={7: 0},
        scratch_shapes=[pltpu.VMEM((HALO_C + tm, wide), F32)] * 2,
        compiler_params=_cp("arbitrary", "arbitrary"),
    )(p, p, dqkv, dqkv, xc, xc, ccw, dp)


ANY = BS(memory_space=pl.ANY)


def _my_pos():
    return lax.axis_index("x"), lax.axis_index("y"), lax.axis_index("c")


def _dev_index(dev):
    return 4 * dev[0] + 2 * dev[1] + dev[2]


def _all_gather(shards, after=None):
    nk = len(shards)

    tail = [] if after is None else [after]

    def body(*refs):
        ins, outs = refs[:nk], refs[nk + len(tail):2 * nk + len(tail)]
        send, recv, loc = refs[2 * nk + len(tail):]
        x, y, c = _my_pos()
        me, sib = (x, y, c), (x, y, 1 - c)
        chips = [(1 - x, y), (x, 1 - y), (1 - x, 1 - y)]

        def rows(t, dev):
            r = ins[t].shape[0]
            return outs[t].at[pl.ds(pl.multiple_of(_dev_index(dev) * r, SUB), r), :]

        def copy(t, k, block, to, src=None):
            return pltpu.make_async_remote_copy(
                src_ref=rows(t, block) if src is None else src, dst_ref=rows(t, block),
                send_sem=send.at[t, k], recv_sem=recv.at[t, k], device_id=to, device_id_type=MESH)

        mine = [pltpu.make_async_copy(ins[t], rows(t, me), loc.at[t]) for t in range(nk)]
        for cp in mine:
            cp.start()
        first = []
        for t in range(nk):
            first.append(copy(t, 0, me, sib, src=ins[t]))
            first += [copy(t, 1 + j, me, (*chip, c), src=ins[t]) for j, chip in enumerate(chips)]
        for cp in first:
            cp.start()
        passed = []
        for j, chip in enumerate(chips):
            for t in range(nk):
                copy(t, 1 + j, (*chip, c), me).wait_recv()
                cp = copy(t, 4 + j, (*chip, c), sib)
                cp.start()
                passed.append(cp)
        for t in range(nk):
            copy(t, 0, sib, me).wait_recv()
            for j, chip in enumerate(chips):
                copy(t, 4 + j, (*chip, 1 - c), me).wait_recv()
        for cp in first + passed:
            cp.wait_send()
        for cp in mine:
            cp.wait()

    return pl.pallas_call(
        body, name="all_gather", in_specs=[ANY] * (nk + len(tail)), out_specs=[ANY] * nk,
        out_shape=[S((N_DEV * a.shape[0], a.shape[1]), a.dtype) for a in shards],
        scratch_shapes=[pltpu.SemaphoreType.DMA((nk, 7)), pltpu.SemaphoreType.DMA((nk, 7)), pltpu.SemaphoreType.DMA((nk,))],
    )(*shards, *tail)


SEM = BS(memory_space=pltpu.SEMAPHORE)
HBM = BS(memory_space=pltpu.HBM)
EFFECT = pltpu.SideEffectType.DATAFLOW_SIDE_EFFECTING


def _peers(x, y, c):
    return [((1 - x) if k & 4 else x, (1 - y) if k & 2 else y, (1 - c) if k & 1 else c) for k in range(1, N_DEV)]


def _exchange_copy(kind, src, land, send, recv, t, k, peer, me, arriving):
    frm = peer if arriving else me
    if kind == "gather":
        r = src.shape[0]
        s_ref = src
        d_ref = land.at[pl.ds(pl.multiple_of(_dev_index(frm) * r, SUB), r), :]
    else:
        r = src.shape[0] // N_DEV
        s_ref = src.at[pl.ds(pl.multiple_of(_dev_index(peer) * r, SUB), r), :]
        d_ref = land.at[_dev_index(frm)]
    sem = t * (N_DEV - 1) + k
    return pltpu.make_async_remote_copy(src_ref=s_ref, dst_ref=d_ref, send_sem=send.at[sem], recv_sem=recv.at[sem],
                                        device_id=peer, device_id_type=MESH)


def _own_copy(kind, src, land, own, t, me):
    if kind == "gather":
        r = src.shape[0]
        return pltpu.make_async_copy(src, land.at[pl.ds(pl.multiple_of(_dev_index(me) * r, SUB), r), :], own.at[t])
    r = src.shape[0] // N_DEV
    return pltpu.make_async_copy(src.at[pl.ds(pl.multiple_of(_dev_index(me) * r, SUB), r), :], land.at[_dev_index(me)], own.at[t])


def _exchange_start(kind, srcs, after, name):
    nk = len(srcs)
    if kind == "gather":
        lands = [lax.empty((N_DEV * a.shape[0], a.shape[1]), a.dtype) for a in srcs]
    else:
        lands = [lax.empty((N_DEV, a.shape[0] // N_DEV, a.shape[1]), a.dtype) for a in srcs]

    def body(*refs):
        src, land = refs[:nk], refs[nk:2 * nk]
        send, recv, own = refs[2 * nk + 1], refs[2 * nk + 2], refs[2 * nk + 3]
        token = refs[-1]
        x, y, c = _my_pos()
        me = (x, y, c)
        for t in range(nk):
            _own_copy(kind, src[t], land[t], own, t, me).start()
            for k, peer in enumerate(_peers(x, y, c)):
                _exchange_copy(kind, src[t], land[t], send, recv, t, k, peer, me, False).start()
        token[...] = jnp.zeros_like(token)

    hbm = lambda a: pltpu.HBM(a.shape, a.dtype)
    out = pl.pallas_call(
        body, name=name,
        out_shape=(pltpu.SemaphoreType.DMA((nk * (N_DEV - 1),)), pltpu.SemaphoreType.DMA((nk * (N_DEV - 1),)),
                   pltpu.SemaphoreType.DMA((nk,)), *[hbm(a) for a in srcs], *[hbm(a) for a in lands], S((SUB, LANES), F32)),
        in_specs=[HBM] * (2 * nk) + [ANY],
        out_specs=(SEM, SEM, SEM, *[HBM] * (2 * nk), BS(memory_space=pltpu.VMEM)),
        input_output_aliases={i: 3 + i for i in range(2 * nk)},
        compiler_params=pltpu.CompilerParams(has_side_effects=EFFECT),
    )(*[pltpu.with_memory_space_constraint(a, pltpu.HBM) for a in (*srcs, *lands)], after)
    return dict(kind=kind, nk=nk, send=out[0], recv=out[1], own=out[2], srcs=out[3:3 + nk], lands=out[3 + nk:3 + 2 * nk],
                token=out[-1])


def _exchange_wait(ex, after, name):
    kind, nk = ex["kind"], ex["nk"]

    def body(*refs):
        src, land = refs[:nk], refs[nk:2 * nk]
        send, recv, own = refs[2 * nk], refs[2 * nk + 1], refs[2 * nk + 2]
        x, y, c = _my_pos()
        me = (x, y, c)
        for t in range(nk):
            _own_copy(kind, src[t], land[t], own, t, me).wait()
            for k, peer in enumerate(_peers(x, y, c)):
                _exchange_copy(kind, src[t], land[t], send, recv, t, k, peer, me, False).wait_send()
                _exchange_copy(kind, src[t], land[t], send, recv, t, k, peer, me, True).wait_recv()

    hbm = lambda a: pltpu.HBM(a.shape, a.dtype)
    out = pl.pallas_call(
        body, name=name,
        out_shape=(*[hbm(a) for a in ex["srcs"]], *[hbm(a) for a in ex["lands"]]),
        in_specs=[HBM] * (2 * nk) + [SEM, SEM, SEM, ANY], out_specs=tuple([HBM] * (2 * nk)),
        input_output_aliases={i: i for i in range(2 * nk)},
        compiler_params=pltpu.CompilerParams(has_side_effects=EFFECT),
    )(*ex["srcs"], *ex["lands"], ex["send"], ex["recv"], ex["own"], after)
    return list(out[nk:])


BLOCK_BYTES = 4 << 20


def _row_tile(rows, row_bytes, align):
    best = align
    for tr in range(align, rows + 1, align):
        if rows % tr == 0 and tr * row_bytes <= BLOCK_BYTES:
            best = tr
    return best


def _sum8(a):
    _, r, w = a.shape
    tr = _row_tile(r, N_DEV * w * a.dtype.itemsize, 32 // a.dtype.itemsize)

    def body(a_ref, o_ref):
        acc = a_ref[0].astype(F32)
        for d in range(1, N_DEV):
            acc = acc + a_ref[d].astype(F32)
        o_ref[...] = acc

    return pl.pallas_call(
        body, name="sum8", grid=(r // tr,), in_specs=[BS((N_DEV, tr, w), lambda i: (0, i, 0))],
        out_specs=BS((tr, w), lambda i: (i, 0)), out_shape=S((r, w), F32), compiler_params=_cp("arbitrary"),
    )(a)


def _adamw(w, g, m, v):
    r, c = w.shape
    tr = _row_tile(r, c * 4 * 2, SUB)

    def body(w_ref, g_ref, m_ref, v_ref, d_ref, mo_ref, vo_ref):
        d_ref[...], mo_ref[...], vo_ref[...] = _adam_update(w_ref[...], g_ref[...], m_ref[...], v_ref[...])

    blk = BS((tr, c), lambda i: (i, 0))
    return pl.pallas_call(
        body, name="adamw", grid=(r // tr,), in_specs=[blk] * 4, out_specs=[blk] * 3,
        out_shape=[S((r, c), F32)] * 3, compiler_params=_cp("arbitrary"),
    )(w, g, m, v)


def _sum8_t(a, tc=256):
    _, r, w = a.shape

    def body(a_ref, o_ref):
        acc = a_ref[0].astype(F32)
        for d in range(1, N_DEV):
            acc = acc + a_ref[d].astype(F32)
        o_ref[...] = acc.T

    return pl.pallas_call(
        body, name="sum8_t", grid=(w // tc,), in_specs=[BS((N_DEV, r, tc), lambda j: (0, 0, j))],
        out_specs=BS((tc, r), lambda j: (j, 0)), out_shape=S((w, r), F32), compiler_params=_cp("arbitrary"),
    )(a)


def _rows_view(a):
    nl, r, c = a.shape
    assert nl == 2
    return a.transpose(2, 0, 1).reshape(c, nl, r // LANES, LANES).transpose(0, 2, 1, 3).reshape(-1, LANES)


def _rows_view_back(a, shape):
    nl, r, c = shape
    return a.reshape(c, r // LANES, nl, LANES).transpose(0, 2, 1, 3).reshape(c, nl, r).transpose(1, 2, 0)


def _adamw_layer_rows(w, g, m, v, l, prev, tr=2048):
    n = w.shape[0]

    def body(w_ref, g_ref, m_ref, v_ref, *refs):
        outs = refs[-4:]
        mine = (_subl((tr, LANES)) % 2) == l
        gv = g_ref[...]
        new = (gv,) + _adam_update(w_ref[...], gv, m_ref[...], v_ref[...])
        for i, o_ref in enumerate(outs):
            old = jnp.zeros((tr, LANES), F32) if prev is None else refs[i][...]
            o_ref[...] = jnp.where(mine, new[i], old)

    blk = BS((tr, LANES), lambda i: (i, 0))
    return pl.pallas_call(
        body, name="adamw_layer_rows", grid=(pl.cdiv(n, tr),),
        in_specs=[blk] * (4 if prev is None else 8), out_specs=[blk] * 4, out_shape=[S((n, LANES), F32)] * 4,
        input_output_aliases={} if prev is None else {4 + i: i for i in range(4)},
        compiler_params=_cp("arbitrary"),
    )(w, g, m, v, *(prev or ()))


def _adam_update(w, g, m, v):
    m2 = ADAM_B1 * m + (1.0 - ADAM_B1) * g
    v2 = ADAM_B2 * v + (1.0 - ADAM_B2) * (g * g)
    m_hat = m2 / (1.0 - ADAM_B1 ** ADAM_STEP)
    v_hat = v2 / (1.0 - ADAM_B2 ** ADAM_STEP)
    return -ADAM_LR * (m_hat / (jnp.sqrt(v_hat) + ADAM_EPS) + ADAM_WD * w), m2, v2


def _adamw_layer(w, g, m, v, l, prev):
    nl, r, c = w.shape
    tr = _row_tile(r, c * 4 * 2, SUB)

    def body(w_ref, g_ref, m_ref, v_ref, *refs):
        go_ref, d_ref, mo_ref, vo_ref = refs[-4:]
        gv = g_ref[...]
        go_ref[...] = gv
        d_ref[...], mo_ref[...], vo_ref[...] = _adam_update(w_ref[...], gv, m_ref[...], v_ref[...])

    slot = BS((None, tr, c), lambda i: (l, i, 0))
    keep = [] if prev is None else [ANY] * 4
    return pl.pallas_call(
        body, name="adamw_layer", grid=(r // tr,), in_specs=[slot, BS((tr, c), lambda i: (i, 0)), slot, slot] + keep,
        out_specs=[slot] * 4, out_shape=[S((nl, r, c), F32)] * 4,
        input_output_aliases={} if prev is None else {4 + i: i for i in range(4)},
        compiler_params=_cp("arbitrary"),
    )(w, g, m, v, *(prev or ()))


def _blob(arrays):
    flat = jnp.concatenate([a.reshape(-1) for a in arrays])
    rows = -(-flat.shape[0] // (SUB * LANES)) * SUB
    return jnp.pad(flat, (0, rows * LANES - flat.shape[0])).reshape(rows, LANES)


def _unblob(blob, shapes, lead=()):
    flat = blob.reshape(lead + (-1,))
    out, off = [], 0
    for s in shapes:
        size = math.prod(s)
        out.append(flat[..., off:off + size].reshape(lead + tuple(s)))
        off += size
    return out


def _lanes6(a):
    return jnp.zeros((1, LANES), F32).at[0, GG:GG + C_HEADS].set(a)


def _y_rows(w):
    return jnp.concatenate([w[0:A_W], w[A_W + B_W:], w[A_W:A_W + B_W]], axis=0)


def _y_rows_back(g):
    return jnp.concatenate([g[0:A_W], g[A_W + C_W:], g[A_W:A_W + C_W]], axis=0)


SMALL = ("norm_w", "q_norm_w", "k_norm_w", "sinks", "b_conv_b", "b_ln_w", "b_ln_b", "b_pw_b", "c_a_log", "c_dt_bias",
         "c_onorm_w", "b_conv_w", "c_conv_w")
ORDER = ("norm_w", "w_in", "q_norm_w", "k_norm_w", "sinks", "b_conv_w", "b_conv_b", "b_ln_w", "b_ln_b", "b_pw_w", "b_pw_b",
         "c_conv_w", "c_a_log", "c_dt_bias", "c_onorm_w", "w_out")


def kernel(x, positions, norm_w, w_in, q_norm_w, k_norm_w, sinks, b_conv_w, b_conv_b, b_ln_w, b_ln_b, b_pw_w, b_pw_b, c_conv_w, c_a_log, c_dt_bias, c_onorm_w, w_out, loss_target, m_norm_w, m_w_in, m_q_norm_w, m_k_norm_w, m_sinks, m_b_conv_w, m_b_conv_b, m_b_ln_w, m_b_ln_b, m_b_pw_w, m_b_pw_b, m_c_conv_w, m_c_a_log, m_c_dt_bias, m_c_onorm_w, m_w_out, v_norm_w, v_w_in, v_q_norm_w, v_k_norm_w, v_sinks, v_b_conv_w, v_b_conv_b, v_b_ln_w, v_b_ln_b, v_b_pw_w, v_b_pw_b, v_c_conv_w, v_c_a_log, v_c_dt_bias, v_c_onorm_w, v_w_out):
    W = dict(norm_w=norm_w, w_in=w_in, q_norm_w=q_norm_w, k_norm_w=k_norm_w, sinks=sinks, b_conv_w=b_conv_w, b_conv_b=b_conv_b,
             b_ln_w=b_ln_w, b_ln_b=b_ln_b, b_pw_w=b_pw_w, b_pw_b=b_pw_b, c_conv_w=c_conv_w, c_a_log=c_a_log,
             c_dt_bias=c_dt_bias, c_onorm_w=c_onorm_w, w_out=w_out)
    M = dict(norm_w=m_norm_w, w_in=m_w_in, q_norm_w=m_q_norm_w, k_norm_w=m_k_norm_w, sinks=m_sinks, b_conv_w=m_b_conv_w,
             b_conv_b=m_b_conv_b, b_ln_w=m_b_ln_w, b_ln_b=m_b_ln_b, b_pw_w=m_b_pw_w, b_pw_b=m_b_pw_b, c_conv_w=m_c_conv_w,
             c_a_log=m_c_a_log, c_dt_bias=m_c_dt_bias, c_onorm_w=m_c_onorm_w, w_out=m_w_out)
    V = dict(norm_w=v_norm_w, w_in=v_w_in, q_norm_w=v_q_norm_w, k_norm_w=v_k_norm_w, sinks=v_sinks, b_conv_w=v_b_conv_w,
             b_conv_b=v_b_conv_b, b_ln_w=v_b_ln_w, b_ln_b=v_b_ln_b, b_pw_w=v_b_pw_w, b_pw_b=v_b_pw_b, c_conv_w=v_c_conv_w,
             c_a_log=v_c_a_log, c_dt_bias=v_c_dt_bias, c_onorm_w=v_c_onorm_w, w_out=v_w_out)
    nseq, t, d = x.shape
    n = nseq * t
    tr = min(256, t)
    tmm = min(512, n)
    tmw = min(1024, n)
    tkk = min(2048, n)
    me = _dev_index(_my_pos())
    xs = [x.reshape(n, d)]
    tgt = loss_target.reshape(n, d)
    tabs = _rope_tables(positions.reshape(n))

    win_p = _pack_cols(w_in).astype(BF16)
    wout_b = w_out.astype(BF16)
    sharded_small = (b_pw_w, b_conv_w, c_conv_w)
    g_win0, g_small = _all_gather([win_p[0], _blob(sharded_small)])
    win = [g_win0]
    later = _exchange_start("gather", [win_p[1], wout_b[0], wout_b[1]], g_small, "gather_start")
    pw_all, cw_all, ccw_all = _unblob(g_small, [a.shape for a in sharded_small], lead=(N_DEV,))
    pw_all = pw_all.transpose(1, 0, 2, 3).reshape(DEPTH, B_W, B_W).astype(BF16)
    cw_all = cw_all.transpose(1, 2, 0, 3).reshape(DEPTH, B_K, B_W)
    ccw_all = ccw_all.transpose(1, 2, 0, 3).reshape(DEPTH, C_K, 3 * C_W)

    def layer_params(l):
        return dict(
            nw=norm_w[l][None], qw=jnp.tile(q_norm_w[l], 2)[None], kw=jnp.tile(k_norm_w[l], 2)[None], sinks=sinks[l],
            cw=jnp.pad(cw_all[l], ((0, HALO_B - B_K), (0, 0))), cb=b_conv_b[l][None], lnw=b_ln_w[l][None], lnb=b_ln_b[l][None],
            pw=pw_all[l], pwb=b_pw_b[l][None], ccw=jnp.pad(ccw_all[l], ((0, SUB - C_K), (0, 0))),
            alog=_lanes6(c_a_log[l]), dtb=_lanes6(c_dt_bias[l]), onw=c_onorm_w[l][None])

    saved = []
    for l in range(DEPTH):
        q = layer_params(l)
        nw = q["nw"] + later["token"][0:1, 0:1] if l == 0 else q["nw"]
        p, h = _inproj(xs[l], nw, win[l], tm=tmw)
        y, o_a, lse = _attn_fwd(p, tabs, q["qw"], q["kw"], q["sinks"], nseq)
        gates = _gdn_gates_fwd(p, q["alog"], q["dtb"], tm=tr)
        xc, qkv = _gdn_pre_fwd(p, q["ccw"], nseq, tm=tr)
        y, o_c, u, w, tinv, ss = _gdn_chunk_fwd(qkv, gates, p, y, q["onw"], nseq)
        y, hc = _conf_fwd(p, y, q["cw"], q["cb"], q["lnw"], q["lnb"], q["pw"], q["pwb"], nseq, tm=tr)
        saved.append(dict(q=q, p=p, h=h, y=y, o_a=o_a, lse=lse, gates=gates, xc=xc, qkv=qkv, o_c=o_c, u=u, w=w, tinv=tinv,
                          ss=ss, hc=hc))
        if l == 0:
            g_win1, g_wout0, g_wout1 = _exchange_wait(later, y, "gather_wait")
            win.append(g_win1)
            wout = [_y_rows(g_wout0), _y_rows(g_wout1)]
        if l + 1 < DEPTH:
            xs.append(_outproj(xs[l], y, wout[l], tm=tmw, tn=512))
        else:
            dxn, lsum = _outproj_loss(xs[l], y, wout[l], tgt, tm=tmw, tn=512)
    loss = lax.psum(jnp.sum(lsum) * (0.5 / d), ("x", "y", "c"))

    sent, smalls = [None] * DEPTH, [None] * DEPTH
    for l in reversed(range(DEPTH)):
        s = saved[l]
        q, p = s["q"], s["p"]
        dy = _matmul(dxn, wout[l], "nt", F32, tmw, 512, d, "outproj_bwd_dy")
        dwout = _y_rows_back(_matmul(s["y"], dxn, "tn", BF16, 1024, 1024, tkk, "outproj_bwd_dw"))
        dp, dkv, dqw, dkw, dsk = _attn_bwd(p, dy, s["o_a"], s["lse"], tabs, q["qw"], q["kw"], q["sinks"], nseq)
        dp = _put_cols(dp, dkv, P_K, tm=tmm)
        dp, dqkv, dgate, donw = _gdn_chunk_bwd(s["qkv"], s["gates"], p, dy, dp, q["onw"], s["o_c"], s["u"], s["w"],
                                               s["tinv"], s["ss"], nseq)
        dp, dal, ddb = _gdn_gates_bwd(dgate, p, q["alog"], q["dtb"], dp, tm=tr)
        dp, dccw = _gdn_pre_bwd(p, dqkv, s["xc"], dp, q["ccw"], nseq, tm=tr)
        dp, dhc, dpw, dpwb, dlnw, dlnb, dcb = _conf_bwd1(p, dy, dp, s["hc"], q["lnw"], q["lnb"], q["pw"], q["pwb"], tm=tr)
        dp, dcw = _conf_bwd2(p, dhc, dp, q["cw"], nseq, tm=tr)
        dwin = _matmul(s["h"], dp, "tn", BF16, 1024, 768, tkk, "inproj_bwd_dw")
        sent[l] = _exchange_start("scatter", [dwin, dwout, dpw], dpwb, "scatter_start_%d" % l)
        dxn, dnw = _inproj_bwd_dx(dp, win[l], xs[l], q["nw"] + sent[l]["token"][0:1, 0:1], dxn, tm=tmm)
        halves = lambda a: a.sum(0)[:A_DH] + a.sum(0)[A_DH:]
        smalls[l] = dict(
            norm_w=dnw.sum(0), q_norm_w=halves(dqw), k_norm_w=halves(dkw), sinks=dsk.sum(0)[:A_HEADS], b_conv_b=dcb.sum(0),
            b_ln_w=dlnw.sum(0), b_ln_b=dlnb.sum(0), b_pw_b=dpwb.sum(0), c_a_log=dal.sum(0)[GG:GG + C_HEADS],
            c_dt_bias=ddb.sum(0)[GG:GG + C_HEADS], c_onorm_w=donw.sum(0),
            b_conv_w=dcw.reshape(B_K, SUB, B_W).sum(1), c_conv_w=dccw.reshape(C_K, SUB, 3 * C_W).sum(1))
    grad_x = dxn.reshape(nseq, t, d)

    G, delta, new_m, new_v = {}, {}, {}, {}
    big = ("w_in", "w_out", "b_pw_w")
    stacks = {k: None for k in big}
    after = dxn
    for l in reversed(range(DEPTH)):
        r_win, r_wout, r_pw = _exchange_wait(sent[l], after, "scatter_wait_%d" % l)
        g_in = _sum8_t(r_win).reshape(P_W, -1, 1, LANES)
        g_in = jnp.broadcast_to(g_in, g_in.shape[:2] + (DEPTH, LANES)).reshape(-1, LANES)
        g_in = _unpack_cols(g_in, axis=0, each=g_in.shape[0] // P_W)
        stacks["w_in"] = _adamw_layer_rows(_rows_view(w_in), g_in, _rows_view(m_w_in), _rows_view(v_w_in), l, stacks["w_in"])
        for k, r in (("w_out", r_wout), ("b_pw_w", r_pw)):
            stacks[k] = _adamw_layer(W[k], _sum8(r), M[k], V[k], l, stacks[k])
        after = stacks["b_pw_w"][1]
    stacks["w_in"] = [_rows_view_back(a, w_in.shape) for a in stacks["w_in"]]
    for k in big:
        G[k], delta[k], new_m[k], new_v[k] = stacks[k]
    part = _blob([jnp.stack([smalls[l][k] for l in range(DEPTH)]) for k in SMALL])
    (tot,) = _all_gather([part], after=stacks["w_in"][1])
    tot = _sum8(tot.reshape(N_DEV, part.shape[0], LANES))
    full_shapes = [(DEPTH,) + smalls[0][k].shape for k in SMALL]
    for k, g in zip(SMALL, _unblob(tot, full_shapes)):
        G[k] = g
    G["b_conv_w"] = lax.dynamic_slice_in_dim(G["b_conv_w"], me * (B_W // N_DEV), B_W // N_DEV, axis=2)
    G["c_conv_w"] = lax.dynamic_slice_in_dim(G["c_conv_w"], me * (3 * C_W // N_DEV), 3 * C_W // N_DEV, axis=2)
    dl, mo, vo = _adamw(*[_blob([src[k] for k in SMALL]) for src in (W, G, M, V)])
    shapes = [W[k].shape for k in SMALL]
    for k, a, b, c in zip(SMALL, _unblob(dl, shapes), _unblob(mo, shapes), _unblob(vo, shapes)):
        delta[k], new_m[k], new_v[k] = a, b, c
    return (loss, grad_x, *[G[k] for k in ORDER], *[delta[k] for k in ORDER], *[new_m[k] for k in ORDER],
            *[new_v[k] for k in ORDER])
```

```python
import functools
import math

import jax
import jax.numpy as jnp
from jax import lax
from jax.experimental import pallas as pl
from jax.experimental.pallas import tpu as pltpu

F32 = jnp.float32
BF16 = jnp.bfloat16
HI = lax.Precision.HIGHEST
MESH = pl.DeviceIdType.MESH
S = jax.ShapeDtypeStruct
BS = pl.BlockSpec

N_DEV = 8
DEPTH = 2
D_MODEL = 2048
A_HEADS, A_KV, A_DH, A_W, A_KVW = 12, 4, 64, 768, 256
ROT = 16
THETA = 500000.0
ABLK = 128
B_W, B_K = 512, 31
C_HEADS, C_DH, C_W, C_K, CHUNK = 6, 128, 768, 4, 64
EPS = 1e-6
IN_COLS = 6668
P_Q, P_ZA, P_ZC, P_QKV, P_K, P_V, P_UB, P_ZB, P_BA, P_W = 0, 768, 1536, 2304, 4608, 4864, 5120, 6144, 6656, 6912
Y_A, Y_C, Y_B = 0, 768, 1536
LANES = 128
SUB = 8

ADAM_LR, ADAM_B1, ADAM_B2, ADAM_EPS, ADAM_WD, ADAM_STEP = 0.001, 0.9, 0.999, 1e-08, 0.01, 10


def _cp(*sem, vmem=None):
    kw = {}
    if sem:
        kw["dimension_semantics"] = sem
    if vmem:
        kw["vmem_limit_bytes"] = vmem
    return pltpu.CompilerParams(**kw)


def _pack_cols(w):
    z = jnp.zeros(w.shape[:-1] + (P_W - IN_COLS,), w.dtype)
    return jnp.concatenate([w[..., 0:768], w[..., 1280:2048], w[..., 5900:6668], w[..., 3584:5888],
                            w[..., 768:1024], w[..., 1024:1280], w[..., 2048:3072], w[..., 3072:3584],
                            w[..., 5888:5900], z], axis=-1)


def _unpack_cols(g, axis=-1, each=1):
    parts = ((P_Q, 768), (P_K, 256), (P_V, 256), (P_ZA, 768), (P_UB, 1024), (P_ZB, 512), (P_QKV, 2304), (P_BA, 12), (P_ZC, 768))
    return jnp.concatenate([lax.slice_in_dim(g, each * o, each * (o + n), axis=axis) for o, n in parts], axis=axis)


def _sigmoid(x):
    return 1.0 / (1.0 + jnp.exp(-x))


def _dsilu(x, sg):
    return sg * (1.0 + x * (1.0 - sg))


def _fold8(x):
    r, c = x.shape
    return x.reshape(r // SUB, SUB, c).sum(axis=0)


def _dot(a, b, prec=None):
    return jnp.dot(a, b, preferred_element_type=F32, precision=prec)


def _dot_nt(a, b, prec=None):
    return lax.dot_general(a, b, (((1,), (1,)), ((), ())), preferred_element_type=F32, precision=prec)


def _dot_tn(a, b, prec=None):
    return lax.dot_general(a, b, (((0,), (0,)), ((), ())), preferred_element_type=F32, precision=prec)


def _lane(shape):
    return lax.broadcasted_iota(jnp.int32, shape, 1)


def _subl(shape):
    return lax.broadcasted_iota(jnp.int32, shape, 0)


def _col(x, j):
    return jnp.sum(jnp.where(_lane(x.shape) == j, x, 0.0), axis=-1, keepdims=True)


def _inproj(x, nw, w, tm=512, tn=768):
    n, d = x.shape
    pw = w.shape[1]

    def body(x_ref, nw_ref, w_ref, p_ref, h_ref):
        @pl.when(pl.program_id(1) == 0)
        def _():
            xv = x_ref[...]
            r = lax.rsqrt(jnp.mean(xv * xv, axis=-1, keepdims=True) + EPS)
            h_ref[...] = (xv * r * nw_ref[...]).astype(BF16)

        p_ref[...] = _dot(h_ref[...], w_ref[...])

    return pl.pallas_call(
        body, name="inproj", grid=(n // tm, pw // tn),
        in_specs=[BS((tm, d), lambda i, j: (i, 0)), BS((1, d), lambda i, j: (0, 0)), BS((d, tn), lambda i, j: (0, j))],
        out_specs=[BS((tm, tn), lambda i, j: (i, j)), BS((tm, d), lambda i, j: (i, 0))],
        out_shape=[S((n, pw), F32), S((n, d), BF16)],
        compiler_params=_cp("arbitrary", "arbitrary"),
    )(x, nw, w)


def _outproj(x, y, w, tm=512, tn=1024):
    n, d = x.shape
    k = y.shape[1]

    def body(x_ref, y_ref, w_ref, o_ref):
        o_ref[...] = x_ref[...] + _dot(y_ref[...], w_ref[...])

    return pl.pallas_call(
        body, name="outproj", grid=(n // tm, d // tn),
        in_specs=[BS((tm, tn), lambda i, j: (i, j)), BS((tm, k), lambda i, j: (i, 0)), BS((k, tn), lambda i, j: (0, j))],
        out_specs=BS((tm, tn), lambda i, j: (i, j)),
        out_shape=S((n, d), F32),
        compiler_params=_cp("arbitrary", "arbitrary"),
    )(x, y, w)


def _outproj_loss(x, y, w, tgt, tm=512, tn=1024):
    n, d = x.shape
    k = y.shape[1]

    def body(x_ref, y_ref, w_ref, t_ref, g_ref, l_ref):
        @pl.when((pl.program_id(0) == 0) & (pl.program_id(1) == 0))
        def _():
            l_ref[...] = jnp.zeros_like(l_ref)

        diff = x_ref[...] + _dot(y_ref[...], w_ref[...]) - t_ref[...]
        g_ref[...] = diff * (1.0 / d)
        f = _fold8(diff * diff)
        acc = f[:, 0:LANES]
        for c in range(1, tn // LANES):
            acc = acc + f[:, c * LANES:(c + 1) * LANES]
        l_ref[...] += acc

    return pl.pallas_call(
        body, name="outproj_loss", grid=(n // tm, d // tn),
        in_specs=[BS((tm, tn), lambda i, j: (i, j)), BS((tm, k), lambda i, j: (i, 0)), BS((k, tn), lambda i, j: (0, j)),
                  BS((tm, tn), lambda i, j: (i, j))],
        out_specs=[BS((tm, tn), lambda i, j: (i, j)), BS((SUB, LANES), lambda i, j: (0, 0))],
        out_shape=[S((n, d), F32), S((SUB, LANES), F32)],
        compiler_params=_cp("arbitrary", "arbitrary"),
    )(x, y, w, tgt)


def _matmul(a, b, mode, out_dtype, tm, tn, tk, name, b_cols=None):
    if mode == "nn":
        (m, kk), nn = a.shape, b.shape[1]
        a_spec, b_spec = BS((tm, tk), lambda i, j, k: (i, k)), BS((tk, tn), lambda i, j, k: (k, j))
        dot = _dot
    elif mode == "nt":
        (m, kk), nn = a.shape, b.shape[0]
        a_spec, b_spec = BS((tm, tk), lambda i, j, k: (i, k)), BS((tn, tk), lambda i, j, k: (j, k))
        dot = _dot_nt
    else:
        j0, nj = b_cols or (0, b.shape[1] // tn)
        (kk, m), nn = a.shape, nj * tn
        a_spec, b_spec = BS((tk, tm), lambda i, j, k: (k, i)), BS((tk, tn), lambda i, j, k: (k, j0 + j))
        dot = _dot_tn
    nk = kk // tk

    def body(a_ref, b_ref, o_ref, acc_ref):
        kid = pl.program_id(2)

        @pl.when(kid == 0)
        def _():
            acc_ref[...] = jnp.zeros_like(acc_ref)

        acc_ref[...] += dot(a_ref[...].astype(BF16), b_ref[...].astype(BF16))

        @pl.when(kid == nk - 1)
        def _():
            o_ref[...] = acc_ref[...].astype(out_dtype)

    return pl.pallas_call(
        body, name=name, grid=(m // tm, nn // tn, nk),
        in_specs=[a_spec, b_spec], out_specs=BS((tm, tn), lambda i, j, k: (i, j)),
        out_shape=S((m, nn), out_dtype), scratch_shapes=[pltpu.VMEM((tm, tn), F32)],
        compiler_params=_cp("arbitrary", "arbitrary", "arbitrary"),
    )(a, b)


SLAB = 16


def _inproj_bwd_dx(dp, w, x, nw, dres, tm=512, tk=1152):
    n, d = x.shape
    nk = dp.shape[1] // tk

    def body(dp_ref, w_ref, x_ref, nw_ref, dr_ref, dx_ref, dnw_ref, acc_ref):
        kid = pl.program_id(1)

        @pl.when((pl.program_id(0) == 0) & (kid == 0))
        def _():
            dnw_ref[...] = jnp.zeros_like(dnw_ref)

        @pl.when(kid == 0)
        def _():
            acc_ref[...] = jnp.zeros_like(acc_ref)

        acc_ref[...] += _dot_nt(dp_ref[...], w_ref[...])

        @pl.when(kid == nk - 1)
        def _():
            def slab(i, carry):
                rows = pl.ds(pl.multiple_of(i * SLAB, SLAB), SLAB)
                dh, xv = acc_ref[rows, :], x_ref[rows, :]
                r = lax.rsqrt(jnp.mean(xv * xv, axis=-1, keepdims=True) + EPS)
                dnw_ref[...] += _fold8(dh * xv * r)
                g = dh * nw_ref[...]
                mm = jnp.mean(g * xv, axis=-1, keepdims=True)
                dx_ref[rows, :] = dr_ref[rows, :] + r * g - xv * (r * r * r * mm)
                return carry

            lax.fori_loop(0, tm // SLAB, slab, 0)

    return pl.pallas_call(
        body, name="inproj_bwd_dx", grid=(n // tm, nk),
        in_specs=[BS((tm, tk), lambda i, k: (i, k)), BS((d, tk), lambda i, k: (0, k)), BS((tm, d), lambda i, k: (i, 0)),
                  BS((1, d), lambda i, k: (0, 0)), BS((tm, d), lambda i, k: (i, 0))],
        out_specs=[BS((tm, d), lambda i, k: (i, 0)), BS((SUB, d), lambda i, k: (0, 0))],
        out_shape=[S((n, d), F32), S((SUB, d), F32)],
        scratch_shapes=[pltpu.VMEM((tm, d), F32)],
        compiler_params=_cp("arbitrary", "arbitrary"),
    )(dp, w, x, nw, dres)


def _rope_tables(pos):
    half = ROT // 2
    inv = THETA ** (-jnp.arange(0, ROT, 2, dtype=F32) / ROT)
    ang = pos.astype(F32)[:, None] * inv
    cos, sin = jnp.cos(ang), jnp.sin(ang)
    n = pos.shape[0]
    one = jnp.ones((n, A_DH - ROT), F32)
    zero = jnp.zeros((n, A_DH - ROT), F32)
    zh = jnp.zeros((n, half), F32)
    c = jnp.concatenate([cos, cos, one], axis=1)
    s1 = jnp.concatenate([-sin, zh, zero], axis=1)
    s2 = jnp.concatenate([zh, sin, zero], axis=1)
    return tuple(jnp.concatenate([t, t], axis=1) for t in (c, s1, s2))


def _half_stat(t):
    lo = _lane(t.shape) < A_DH
    s_lo = jnp.sum(jnp.where(lo, t, 0.0), axis=-1, keepdims=True)
    s_hi = jnp.sum(jnp.where(lo, 0.0, t), axis=-1, keepdims=True)
    return jnp.where(lo, s_lo, s_hi)


def _normrope(x, w, c, s1, s2):
    r = lax.rsqrt(_half_stat(x * x) * (1.0 / A_DH) + EPS)
    xn = x * r * w
    return xn * c + pltpu.roll(xn, LANES - ROT // 2, 1) * s1 + pltpu.roll(xn, ROT // 2, 1) * s2, r


def _normrope_bwd(dy, x, r, w, c, s1, s2):
    dxn = dy * c + pltpu.roll(dy * s1, ROT // 2, 1) + pltpu.roll(dy * s2, LANES - ROT // 2, 1)
    g = dxn * w
    mm = _half_stat(g * x) * (1.0 / A_DH)
    return r * g - x * (r * r * r * mm), dxn * x * r


def _attn_mask(first):
    qi = _subl((ABLK, 2 * ABLK))
    kj = _lane((ABLK, 2 * ABLK))
    dist = qi + ABLK - kj
    return (dist >= 0) & (dist < ABLK) & (jnp.logical_not(first) | (kj >= ABLK))


def _attn_fwd(p, tabs, qw, kw, sinks, nseq):
    n = p.shape[0]
    nb = n // nseq // ABLK
    cur = lambda b, i: (b * nb + i, 0)
    prv = lambda b, i: (b * nb + jnp.maximum(i - 1, 0), 0)
    colblk = lambda f, w, off: (lambda b, i: (f(b, i)[0], off // w))

    def body(q_ref, za_ref, kc_ref, vc_ref, kp_ref, vp_ref, c_ref, s1_ref, s2_ref, cp_ref, s1p_ref, s2p_ref,
             qw_ref, kw_ref, sink_ref, y_ref, o_ref, lse_ref):
        first = pl.program_id(1) == 0
        tc = (c_ref[...], s1_ref[...], s2_ref[...])
        tp = (cp_ref[...], s1p_ref[...], s2p_ref[...])
        q, kc, kp = q_ref[...], kc_ref[...], kp_ref[...]
        qn = [_normrope(q[:, LANES * b:LANES * (b + 1)], qw_ref[...], *tc)[0].astype(BF16) for b in range(A_W // LANES)]
        k2, v2 = [], []
        for b in range(A_KVW // LANES):
            sl = slice(LANES * b, LANES * (b + 1))
            k2.append(jnp.concatenate([_normrope(kp[:, sl], kw_ref[...], *tp)[0],
                                       _normrope(kc[:, sl], kw_ref[...], *tc)[0]], axis=0).astype(BF16))
            v2.append(jnp.concatenate([vp_ref[:, sl], vc_ref[:, sl]], axis=0).astype(BF16))
        valid = _attn_mask(first)
        heads = range(A_HEADS)
        half = lambda a, i: a[:, A_DH * (i % 2):A_DH * (i % 2 + 1)]
        kh = [half(k2[g // 2], g) for g in range(A_KV)]
        vh = [half(v2[g // 2], g) for g in range(A_KV)]
        s = [jnp.where(valid, _dot_nt(half(qn[j // 2], j), kh[j // 3]) * (A_DH ** -0.5), -jnp.inf) for j in heads]
        m = [jnp.maximum(jnp.max(s[j], axis=-1, keepdims=True), sink_ref[j]) for j in heads]
        e = [jnp.exp(s[j] - m[j]) for j in heads]
        den = [jnp.sum(e[j], axis=-1, keepdims=True) + jnp.exp(sink_ref[j] - m[j]) for j in heads]
        outs = [_dot((e[j] / den[j]).astype(BF16), vh[j // 3]) for j in heads]
        lse = jnp.zeros((ABLK, LANES), F32)
        for j in heads:
            lse = jnp.where(_lane(lse.shape) == j, m[j] + jnp.log(den[j]), lse)
        o = jnp.concatenate(outs, axis=1)
        za = za_ref[...]
        o_ref[...] = o
        lse_ref[...] = lse
        y_ref[...] = (o * (za * _sigmoid(za))).astype(BF16)

    tab_specs = [BS((ABLK, LANES), cur)] * 3 + [BS((ABLK, LANES), prv)] * 3
    return pl.pallas_call(
        body, name="attn_fwd", grid=(nseq, nb),
        in_specs=[BS((ABLK, A_W), colblk(cur, A_W, P_Q)), BS((ABLK, A_W), colblk(cur, A_W, P_ZA)),
                  BS((ABLK, A_KVW), colblk(cur, A_KVW, P_K)), BS((ABLK, A_KVW), colblk(cur, A_KVW, P_V)),
                  BS((ABLK, A_KVW), colblk(prv, A_KVW, P_K)), BS((ABLK, A_KVW), colblk(prv, A_KVW, P_V))]
        + tab_specs + [BS((1, LANES), lambda b, i: (0, 0))] * 2 + [BS(memory_space=pltpu.SMEM)],
        out_specs=[BS((ABLK, A_W), colblk(cur, A_W, Y_A)), BS((ABLK, A_W), cur), BS((ABLK, LANES), cur)],
        out_shape=[S((n, D_MODEL), BF16), S((n, A_W), F32), S((n, LANES), F32)],
        compiler_params=_cp("arbitrary", "arbitrary"),
    )(p, p, p, p, p, p, *tabs, *tabs, qw, kw, sinks)


def _attn_bwd(p, dy, o, lse, tabs, qw, kw, sinks, nseq):
    n = p.shape[0]
    nb = n // nseq // ABLK
    cur = lambda b, i: (b * nb + jnp.minimum(i, nb - 1), 0)
    prv = lambda b, i: (b * nb + jnp.maximum(i - 1, 0), 0)
    colblk = lambda f, w, off: (lambda b, i: (f(b, i)[0], off // w))

    def body(q_ref, za_ref, kc_ref, vc_ref, kp_ref, vp_ref, dy_ref, o_ref, lse_ref,
             c_ref, s1_ref, s2_ref, cp_ref, s1p_ref, s2p_ref, qw_ref, kw_ref, sink_ref,
             dqza_ref, dkv_ref, dqw_ref, dkw_ref, dsk_ref, tk_ref, tv_ref, ck_ref, cv_ref):
        i = pl.program_id(1)
        first = i == 0
        tc = (c_ref[...], s1_ref[...], s2_ref[...])
        tp = (cp_ref[...], s1p_ref[...], s2p_ref[...])
        nkb = A_KVW // LANES

        @pl.when((pl.program_id(0) == 0) & first)
        def _():
            dqw_ref[...] = jnp.zeros_like(dqw_ref)
            dkw_ref[...] = jnp.zeros_like(dkw_ref)
            dsk_ref[...] = jnp.zeros_like(dsk_ref)

        @pl.when(i < nb)
        def _():
            q, kc, kp = q_ref[...], kc_ref[...], kp_ref[...]
            qn, rq = [], []
            for b in range(A_W // LANES):
                a, r = _normrope(q[:, LANES * b:LANES * (b + 1)], qw_ref[...], *tc)
                qn.append(a.astype(BF16))
                rq.append(r)
            k2, v2 = [], []
            for b in range(nkb):
                sl = slice(LANES * b, LANES * (b + 1))
                k2.append(jnp.concatenate([_normrope(kp[:, sl], kw_ref[...], *tp)[0],
                                           _normrope(kc[:, sl], kw_ref[...], *tc)[0]], axis=0).astype(BF16))
                v2.append(jnp.concatenate([vp_ref[:, sl], vc_ref[:, sl]], axis=0).astype(BF16))
            valid = _attn_mask(first)
            za, dy, o, lse = za_ref[...], dy_ref[...], o_ref[...], lse_ref[...]
            sg = _sigmoid(za)
            do = dy * za * sg
            dqza_ref[:, A_W:2 * A_W] = (dy * o * _dsilu(za, sg)).astype(BF16)
            heads = range(A_HEADS)
            half = lambda a, i: a[:, A_DH * (i % 2):A_DH * (i % 2 + 1)]
            kh = [half(k2[g // 2], g) for g in range(A_KV)]
            vh = [half(v2[g // 2], g) for g in range(A_KV)]
            qh = [half(qn[j // 2], j) for j in heads]
            lj = [_col(lse, j) for j in heads]
            pr = [jnp.exp(jnp.where(valid, _dot_nt(qh[j], kh[j // 3]) * (A_DH ** -0.5), -jnp.inf) - lj[j]) for j in heads]
            doh = [do[:, A_DH * j:A_DH * (j + 1)] for j in heads]
            delta = [jnp.sum(doh[j] * o[:, A_DH * j:A_DH * (j + 1)], axis=-1, keepdims=True) for j in heads]
            dohb = [a.astype(BF16) for a in doh]
            ds = [(pr[j] * (_dot_nt(dohb[j], vh[j // 3]) - delta[j]) * (A_DH ** -0.5)).astype(BF16) for j in heads]
            dqs = [_dot(ds[j], kh[j // 3]) for j in heads]
            dkh = [_dot_tn(ds[j], qh[j]) for j in heads]
            dvh = [_dot_tn(pr[j].astype(BF16), dohb[j]) for j in heads]
            dks = [dkh[3 * g] + dkh[3 * g + 1] + dkh[3 * g + 2] for g in range(A_KV)]
            dvs = [dvh[3 * g] + dvh[3 * g + 1] + dvh[3 * g + 2] for g in range(A_KV)]
            dsk = jnp.zeros((ABLK, LANES), F32)
            for j in heads:
                dsk = dsk + jnp.where(_lane(dsk.shape) == j, -jnp.exp(sink_ref[j] - lj[j]) * delta[j], 0.0)
            dsk_ref[...] += _fold8(dsk)
            dqn = jnp.concatenate(dqs, axis=1)
            dqw = jnp.zeros((SUB, LANES), F32)
            dqo = []
            for b in range(A_W // LANES):
                sl = slice(LANES * b, LANES * (b + 1))
                dx, wt = _normrope_bwd(dqn[:, sl], q[:, sl], rq[b], qw_ref[...], *tc)
                dqo.append(dx)
                dqw = dqw + _fold8(wt)
            dqw_ref[...] += dqw
            dqza_ref[:, 0:A_W] = jnp.concatenate(dqo, axis=1).astype(BF16)
            tk_ref[...] = jnp.concatenate(dks, axis=1)
            tv_ref[...] = jnp.concatenate(dvs, axis=1)

        @pl.when(i == nb)
        def _():
            tk_ref[...] = jnp.zeros_like(tk_ref)
            tv_ref[...] = jnp.zeros_like(tv_ref)

        @pl.when(i > 0)
        def _():
            kp = kp_ref[...]
            dkn = ck_ref[...] + tk_ref[0:ABLK, :]
            dkw = jnp.zeros((SUB, LANES), F32)
            dko = []
            for b in range(nkb):
                sl = slice(LANES * b, LANES * (b + 1))
                r = _normrope(kp[:, sl], kw_ref[...], *tp)[1]
                dx, wt = _normrope_bwd(dkn[:, sl], kp[:, sl], r, kw_ref[...], *tp)
                dko.append(dx)
                dkw = dkw + _fold8(wt)
            dkw_ref[...] += dkw
            dkv_ref[:, 0:A_KVW] = jnp.concatenate(dko, axis=1).astype(BF16)
            dkv_ref[:, A_KVW:2 * A_KVW] = (cv_ref[...] + tv_ref[0:ABLK, :]).astype(BF16)

        ck_ref[...] = tk_ref[ABLK:2 * ABLK, :]
        cv_ref[...] = tv_ref[ABLK:2 * ABLK, :]

    tab_specs = [BS((ABLK, LANES), cur)] * 3 + [BS((ABLK, LANES), prv)] * 3
    acc = BS((SUB, LANES), lambda b, i: (0, 0))
    return pl.pallas_call(
        body, name="attn_bwd", grid=(nseq, nb + 1),
        in_specs=[BS((ABLK, A_W), colblk(cur, A_W, P_Q)), BS((ABLK, A_W), colblk(cur, A_W, P_ZA)),
                  BS((ABLK, A_KVW), colblk(cur, A_KVW, P_K)), BS((ABLK, A_KVW), colblk(cur, A_KVW, P_V)),
                  BS((ABLK, A_KVW), colblk(prv, A_KVW, P_K)), BS((ABLK, A_KVW), colblk(prv, A_KVW, P_V)),
                  BS((ABLK, A_W), colblk(cur, A_W, Y_A)), BS((ABLK, A_W), cur), BS((ABLK, LANES), cur)]
        + tab_specs + [BS((1, LANES), lambda b, i: (0, 0))] * 2 + [BS(memory_space=pltpu.SMEM)],
        out_specs=[BS((ABLK, 2 * A_W), cur), BS((ABLK, 2 * A_KVW), prv), acc, acc, acc],
        out_shape=[S((n, P_W), BF16), S((n, 2 * A_KVW), BF16)] + [S((SUB, LANES), F32)] * 3,
        scratch_shapes=[pltpu.VMEM((2 * ABLK, A_KVW), F32)] * 2 + [pltpu.VMEM((ABLK, A_KVW), F32)] * 2,
        compiler_params=_cp("arbitrary", "arbitrary"),
    )(p, p, p, p, p, p, dy, o, lse, *tabs, *tabs, qw, kw, sinks)


def _put_cols(dst, src, col_off, after, tm=512):
    n, w = src.shape

    def body(s_ref, d_in_ref, after_ref, d_ref):
        d_ref[...] = s_ref[...]

    return pl.pallas_call(
        body, name="put_cols", grid=(n // tm,),
        in_specs=[BS((tm, w), lambda i: (i, 0)), BS(memory_space=pl.ANY), BS(memory_space=pl.ANY)],
        out_specs=BS((tm, w), lambda i: (i, col_off // w)),
        out_shape=S(dst.shape, dst.dtype), input_output_aliases={1: 0},
        compiler_params=_cp("arbitrary"),
    )(src, dst, after)


HALO_B = 32


def _layernorm(hc, lnw, lnb):
    mu = jnp.mean(hc, axis=-1, keepdims=True)
    xc = hc - mu
    rstd = lax.rsqrt(jnp.mean(xc * xc, axis=-1, keepdims=True) + EPS)
    xhat = xc * rstd
    return xhat, rstd, xhat * lnw + lnb


def _shifted_copies(buf_ref, sh_ref):
    rows = sh_ref.shape[1]
    for b in range(1, SUB):
        sh_ref[b - 1] = buf_ref[pl.ds(b, rows), :]


def _rows_from(buf_ref, sh_ref, off, rows):
    a, b = divmod(off, SUB)
    if b == 0:
        return buf_ref[pl.ds(SUB * a, rows), :]
    return sh_ref[b - 1, pl.ds(SUB * a, rows), :]


def _conf_fwd(p, y, cw, cb, lnw, lnb, pw, pwb, nseq, tm=256):
    n = p.shape[0]
    t = n // nseq
    nt = t // tm
    row = lambda b, i: b * nt + i
    halo = lambda b, i: jnp.maximum((b * t + i * tm) // HALO_B - 1, 0)
    vec = BS((1, B_W), lambda b, i: (0, 0))

    def body(ub_ref, uh_ref, zb_ref, cw_ref, cb_ref, lnw_ref, lnb_ref, pw_ref, pwb_ref, y_in_ref, y_ref, hc_ref, buf_ref, sh_ref):
        ub, uh = ub_ref[...], uh_ref[...]
        hh = uh[:, :B_W] * _sigmoid(uh[:, B_W:])
        buf_ref[0:HALO_B, :] = jnp.where(pl.program_id(1) > 0, hh, 0.0)
        buf_ref[HALO_B:, :] = ub[:, :B_W] * _sigmoid(ub[:, B_W:])
        _shifted_copies(buf_ref, sh_ref)
        hc = jnp.zeros((tm, B_W), F32) + cb_ref[...]
        for k in range(B_K):
            hc = hc + cw_ref[k:k + 1, :] * _rows_from(buf_ref, sh_ref, HALO_B - B_K + 1 + k, tm)
        hc_ref[...] = hc
        ln = _layernorm(hc, lnw_ref[...], lnb_ref[...])[2]
        sw = ln * _sigmoid(ln)
        ob = _dot(sw.astype(BF16), pw_ref[...]) + pwb_ref[...]
        zb = zb_ref[...]
        y_ref[...] = (ob * (zb * _sigmoid(zb))).astype(BF16)

    return pl.pallas_call(
        body, name="conf_fwd", grid=(nseq, nt),
        in_specs=[BS((tm, 2 * B_W), lambda b, i: (row(b, i), P_UB // (2 * B_W))),
                  BS((HALO_B, 2 * B_W), lambda b, i: (halo(b, i), P_UB // (2 * B_W))),
                  BS((tm, B_W), lambda b, i: (row(b, i), P_ZB // B_W)),
                  BS((HALO_B, B_W), lambda b, i: (0, 0)), vec, vec, vec, BS((B_W, B_W), lambda b, i: (0, 0)), vec,
                  BS(memory_space=pl.ANY)],
        out_specs=[BS((tm, B_W), lambda b, i: (row(b, i), Y_B // B_W)), BS((tm, B_W), lambda b, i: (row(b, i), 0))],
        out_shape=[S(y.shape, y.dtype), S((n, B_W), F32)], input_output_aliases={9: 0},
        scratch_shapes=[pltpu.VMEM((HALO_B + tm, B_W), F32), pltpu.VMEM((SUB - 1, HALO_B + tm - SUB, B_W), F32)],
        compiler_params=_cp("arbitrary", "arbitrary"),
    )(p, p, p, cw, cb, lnw, lnb, pw, pwb, y)


def _conf_bwd1(p, dy, dp, hc, lnw, lnb, pw, pwb, tm=256):
    n = p.shape[0]
    vec = BS((1, B_W), lambda i: (0, 0))
    acc = BS((SUB, B_W), lambda i: (0, 0))

    def body(dy_ref, zb_ref, hc_ref, lnw_ref, lnb_ref, pw_ref, pwb_ref, dp_in_ref,
             dzb_ref, dhc_ref, dpw_ref, dpwb_ref, dlnw_ref, dlnb_ref, dcb_ref):
        @pl.when(pl.program_id(0) == 0)
        def _():
            for r in (dpw_ref, dpwb_ref, dlnw_ref, dlnb_ref, dcb_ref):
                r[...] = jnp.zeros_like(r)

        xhat, rstd, ln = _layernorm(hc_ref[...], lnw_ref[...], lnb_ref[...])
        sgl = _sigmoid(ln)
        sw = (ln * sgl).astype(BF16)
        ob = _dot(sw, pw_ref[...]) + pwb_ref[...]
        dy, zb = dy_ref[...], zb_ref[...]
        sgz = _sigmoid(zb)
        dzb_ref[...] = (dy * ob * _dsilu(zb, sgz)).astype(BF16)
        dob = dy * zb * sgz
        dobb = dob.astype(BF16)
        dpwb_ref[...] += _fold8(dob)
        dpw_ref[...] += _dot_tn(sw, dobb)
        dln = _dot_nt(dobb, pw_ref[...]) * _dsilu(ln, sgl)
        dlnw_ref[...] += _fold8(dln * xhat)
        dlnb_ref[...] += _fold8(dln)
        dxh = dln * lnw_ref[...]
        dhc = rstd * (dxh - jnp.mean(dxh, axis=-1, keepdims=True) - xhat * jnp.mean(dxh * xhat, axis=-1, keepdims=True))
        dcb_ref[...] += _fold8(dhc)
        dhc_ref[...] = dhc

    return pl.pallas_call(
        body, name="conf_bwd1", grid=(n // tm,),
        in_specs=[BS((tm, B_W), lambda i: (i, Y_B // B_W)), BS((tm, B_W), lambda i: (i, P_ZB // B_W)),
                  BS((tm, B_W), lambda i: (i, 0)), vec, vec, BS((B_W, B_W), lambda i: (0, 0)), vec,
                  BS(memory_space=pl.ANY)],
        out_specs=[BS((tm, B_W), lambda i: (i, P_ZB // B_W)), BS((tm, B_W), lambda i: (i, 0)),
                   BS((B_W, B_W), lambda i: (0, 0)), acc, acc, acc, acc],
        out_shape=[S(dp.shape, dp.dtype), S((n, B_W), F32), S((B_W, B_W), F32)] + [S((SUB, B_W), F32)] * 4,
        input_output_aliases={7: 0},
        compiler_params=_cp("arbitrary"),
    )(dy, p, hc, lnw, lnb, pw, pwb, dp)


def _conf_bwd2(p, dhc, dp, cw, nseq, tm=256):
    n = p.shape[0]
    t = n // nseq
    nt = t // tm
    row = lambda b, i: b * nt + i
    prev = lambda b, i: jnp.maximum((b * t + i * tm) // HALO_B - 1, 0)
    nxt = lambda b, i: jnp.minimum((b * t + (i + 1) * tm) // HALO_B, n // HALO_B - 1)

    def body(ub_ref, uh_ref, dh_ref, dn_ref, cw_ref, dp_in_ref, dub_ref, dcw_ref, buf_ref, dbuf_ref, sh_ref, dsh_ref):
        i = pl.program_id(1)

        @pl.when((pl.program_id(0) == 0) & (i == 0))
        def _():
            dcw_ref[...] = jnp.zeros_like(dcw_ref)

        ub, uh = ub_ref[...], uh_ref[...]
        a, sg = ub[:, :B_W], _sigmoid(ub[:, B_W:])
        buf_ref[0:HALO_B, :] = jnp.where(i > 0, uh[:, :B_W] * _sigmoid(uh[:, B_W:]), 0.0)
        buf_ref[HALO_B:, :] = a * sg
        dhc = dh_ref[...]
        dbuf_ref[0:tm, :] = dhc
        dbuf_ref[tm:, :] = jnp.where(i < nt - 1, dn_ref[...], 0.0)
        _shifted_copies(buf_ref, sh_ref)
        _shifted_copies(dbuf_ref, dsh_ref)
        dhg = jnp.zeros((tm, B_W), F32)
        for k in range(B_K):
            dhg = dhg + cw_ref[k:k + 1, :] * _rows_from(dbuf_ref, dsh_ref, B_K - 1 - k, tm)
            dcw_ref[SUB * k:SUB * (k + 1), :] += _fold8(dhc * _rows_from(buf_ref, sh_ref, HALO_B - B_K + 1 + k, tm))
        dub_ref[...] = jnp.concatenate([dhg * sg, dhg * a * sg * (1.0 - sg)], axis=1).astype(BF16)

    return pl.pallas_call(
        body, name="conf_bwd2", grid=(nseq, nt),
        in_specs=[BS((tm, 2 * B_W), lambda b, i: (row(b, i), P_UB // (2 * B_W))),
                  BS((HALO_B, 2 * B_W), lambda b, i: (prev(b, i), P_UB // (2 * B_W))),
                  BS((tm, B_W), lambda b, i: (row(b, i), 0)), BS((HALO_B, B_W), lambda b, i: (nxt(b, i), 0)),
                  BS((HALO_B, B_W), lambda b, i: (0, 0)), BS(memory_space=pl.ANY)],
        out_specs=[BS((tm, 2 * B_W), lambda b, i: (row(b, i), P_UB // (2 * B_W))),
                   BS((SUB * B_K, B_W), lambda b, i: (0, 0))],
        out_shape=[S(dp.shape, dp.dtype), S((SUB * B_K, B_W), F32)], input_output_aliases={5: 0},
        scratch_shapes=[pltpu.VMEM((HALO_B + tm, B_W), F32)] * 2 + [pltpu.VMEM((SUB - 1, HALO_B + tm - SUB, B_W), F32)] * 2,
        compiler_params=_cp("arbitrary", "arbitrary"),
    )(p, p, dhc, dhc, cw, dp)


HALO_C = 8
QS = C_DH ** -0.5
NCB = 3 * C_HEADS
CB0 = P_QKV // LANES
ZC0 = P_ZC // LANES
GB, GG = 0, C_HEADS


def _softplus(z):
    return jnp.maximum(z, 0.0) + jnp.log(1.0 + jnp.exp(-jnp.abs(z)))


def _gdn_gates_fwd(p, alog_l, dtb_l, tm=256):
    n = p.shape[0]

    def body(ba_ref, al_ref, db_ref, o_ref):
        blk = ba_ref[...]
        lane = _lane(blk.shape)
        g = jnp.where((lane >= GG) & (lane < GG + C_HEADS), -jnp.exp(al_ref[...]) * _softplus(blk + db_ref[...]), 0.0)
        tri = (_subl((CHUNK, CHUNK)) >= _lane((CHUNK, CHUNK))).astype(F32)
        gc = jnp.concatenate([_dot(tri, g[CHUNK * c:CHUNK * (c + 1)], HI) for c in range(tm // CHUNK)], axis=0)
        o_ref[...] = jnp.where(lane < GG, _sigmoid(blk), gc)

    return pl.pallas_call(
        body, name="gdn_gates_fwd", grid=(n // tm,),
        in_specs=[BS((tm, LANES), lambda i: (i, P_BA // LANES)), BS((1, LANES), lambda i: (0, 0)), BS((1, LANES), lambda i: (0, 0))],
        out_specs=BS((tm, LANES), lambda i: (i, 0)), out_shape=S((n, LANES), F32),
        compiler_params=_cp("arbitrary"),
    )(p, alog_l, dtb_l)


def _gdn_pre_fwd(p, ccw, nseq, tm=256):
    n = p.shape[0]
    t = n // nseq
    nt = t // tm
    row = lambda b, i: b * nt + i
    halo = lambda b, i: jnp.maximum((b * t + i * tm) // HALO_C - 1, 0)

    def body(x_ref, xh_ref, w_ref, xc_ref, o_ref, buf_ref):
        buf_ref[0:HALO_C, :] = jnp.where(pl.program_id(1) > 0, xh_ref[...], 0.0)
        buf_ref[HALO_C:, :] = x_ref[...]
        for c in range(NCB):
            cs = slice(LANES * c, LANES * (c + 1))
            xc = jnp.zeros((tm, LANES), F32)
            for k in range(C_K):
                xc = xc + w_ref[k:k + 1, cs] * buf_ref[pl.ds(HALO_C - C_K + 1 + k, tm), cs]
            xc_ref[:, cs] = xc
            act = xc * _sigmoid(xc)
            if c < 2 * C_HEADS:
                act = act * (lax.rsqrt(jnp.sum(act * act, axis=-1, keepdims=True) + EPS) * (QS if c < C_HEADS else 1.0))
            o_ref[:, cs] = act

    wide = 3 * C_W
    return pl.pallas_call(
        body, name="gdn_pre_fwd", grid=(nseq, nt),
        in_specs=[BS((tm, wide), lambda b, i: (row(b, i), P_QKV // wide)), BS((HALO_C, wide), lambda b, i: (halo(b, i), P_QKV // wide)),
                  BS((SUB, wide), lambda b, i: (0, 0))],
        out_specs=[BS((tm, wide), lambda b, i: (row(b, i), 0))] * 2,
        out_shape=[S((n, wide), F32)] * 2,
        scratch_shapes=[pltpu.VMEM((HALO_C + tm, wide), F32)],
        compiler_params=_cp("arbitrary", "arbitrary"),
    )(p, p, ccw)


def _chunk_common(q, k, gt, gtt, h):
    beta = _col(gt, GB + h)
    gc = _col(gt, GG + h)
    gcr = gtt[GG + h:GG + h + 1, :]
    ii, jj = _subl((CHUNK, CHUNK)), _lane((CHUNK, CHUNK))
    incl, strict = ii >= jj, ii > jj
    dec = jnp.exp(jnp.where(incl, gc - gcr, -jnp.inf))
    kb = k * beta
    kbf = k.astype(BF16)
    a = jnp.where(strict, _dot_nt(kb.astype(BF16), kbf) * dec, 0.0)
    mq = jnp.where(incl, _dot_nt(q.astype(BF16), kbf) * dec, 0.0)
    glast = jnp.sum(jnp.where(_subl(gc.shape) == CHUNK - 1, gc, 0.0), axis=0, keepdims=True)
    return beta, gc, incl, strict, dec, kb, a, mq, glast


def _split(x):
    hi = x.astype(BF16)
    return hi, (x - hi.astype(F32)).astype(BF16)


def _dot3(dot, a, b):
    (ah, al), (bh, bl) = a, b
    return dot(ah, bh) + (dot(ah, bl) + dot(al, bh))


def _unit_lower_inverses(mats):
    eye = (_subl(mats[0].shape) == _lane(mats[0].shape)).astype(F32)
    ms = [-a for a in mats]
    invs = [eye + m for m in ms]
    parts = [_split(m) for m in ms]
    for _ in range(5):
        ms = [_dot3(_dot, s, s) for s in parts]
        parts = [_split(m) for m in ms]
        invs = [inv + _dot3(_dot, _split(inv), s) for inv, s in zip(invs, parts)]
    return invs


def _gdn_chunk_fwd(qkv, gates, p, y, onw, nseq, tt=512):
    n = qkv.shape[0]
    t = n // nseq
    tt = min(tt, t)
    nt = t // tt
    nch = tt // CHUNK

    def body(q_ref, k_ref, v_ref, g_ref, zc_ref, onw_ref, y_in_ref, y_ref, o_ref, u_ref, w_ref, t_ref, ss_ref, s_scr):
        @pl.when(pl.program_id(1) == 0)
        def _():
            s_scr[...] = jnp.zeros_like(s_scr)

        def step(c, carry):
            rows = pl.ds(pl.multiple_of(c * CHUNK, CHUNK), CHUNK)
            gt = g_ref[rows, :]
            gtt = gt.T
            heads = range(C_HEADS)
            hs = [slice(C_DH * h, C_DH * (h + 1)) for h in heads]
            q, k, v = ([r[rows, hs[h]] for h in heads] for r in (q_ref, k_ref, v_ref))
            cm = [_chunk_common(q[h], k[h], gt, gtt, h) for h in heads]
            beta, gc, kb, mq, glast = ([m[i] for m in cm] for i in (0, 1, 5, 7, 8))
            tinv = _unit_lower_inverses([m[6] for m in cm])
            egc = [jnp.exp(g) for g in gc]
            sol = [_dot3(_dot, _split(tinv[h]), _split(jnp.concatenate([v[h] * beta[h], kb[h] * egc[h]], axis=1))) for h in heads]
            sv = [s_scr[h] for h in heads]
            sb = [s.astype(BF16) for s in sv]
            vnb = [(sol[h][:, :C_DH] - _dot(sol[h][:, C_DH:].astype(BF16), sb[h])).astype(BF16) for h in heads]
            o = [_dot((q[h] * egc[h]).astype(BF16), sb[h]) + _dot(mq[h].astype(BF16), vnb[h]) for h in heads]
            for h in heads:
                ss_ref[h, c] = sv[h]
                s_scr[h] = sv[h] * jnp.exp(glast[h]) + _dot_tn((k[h] * jnp.exp(glast[h] - gc[h])).astype(BF16), vnb[h])
            for h in heads:
                o_ref[rows, hs[h]] = o[h]
                u_ref[rows, hs[h]] = sol[h][:, :C_DH]
                w_ref[rows, hs[h]] = sol[h][:, C_DH:]
                t_ref[rows, hs[h]] = jnp.concatenate([tinv[h], jnp.zeros_like(tinv[h])], axis=1)
                zc = zc_ref[rows, hs[h]]
                r = lax.rsqrt(jnp.mean(o[h] * o[h], axis=-1, keepdims=True) + EPS)
                y_ref[rows, hs[h]] = (o[h] * r * onw_ref[...] * (zc * _sigmoid(zc))).astype(BF16)
            return carry

        lax.fori_loop(0, nch, step, 0)

    row = lambda b, i: b * nt + i
    wb = lambda col: BS((tt, C_W), lambda b, i: (row(b, i), col))
    return pl.pallas_call(
        body, name="gdn_chunk_fwd", grid=(nseq, nt),
        in_specs=[wb(0), wb(1), wb(2), BS((tt, LANES), lambda b, i: (row(b, i), 0)), wb(P_ZC // C_W),
                  BS((1, LANES), lambda b, i: (0, 0)), BS(memory_space=pl.ANY)],
        out_specs=[wb(Y_C // C_W), wb(0), wb(0), wb(0), wb(0),
                   BS((None, C_HEADS, nch, C_DH, C_DH), lambda b, i: (b, 0, i, 0, 0))],
        out_shape=[S(y.shape, y.dtype)] + [S((n, C_W), F32)] * 4 + [S((nseq, C_HEADS, t // CHUNK, C_DH, C_DH), F32)],
        input_output_aliases={6: 0},
        scratch_shapes=[pltpu.VMEM((C_HEADS, C_DH, C_DH), F32)],
        compiler_params=_cp("arbitrary", "arbitrary"),
    )(qkv, qkv, qkv, gates, p, onw, y)


def _gdn_chunk_bwd(qkv, gates, p, dy, dp, onw, o, u, w, tinv, ss, nseq, tt=256):
    n = qkv.shape[0]
    t = n // nseq
    tt = min(tt, t)
    nt = t // tt
    nch = tt // CHUNK

    def body(q_ref, k_ref, v_ref, g_ref, zc_ref, onw_ref, o_ref, dy_ref, u_ref, w_ref, t_ref, ss_ref, dp_in_ref,
             dzc_ref, dqkv_ref, dg_ref, donw_ref, ds_scr):
        @pl.when(pl.program_id(1) == 0)
        def _():
            ds_scr[...] = jnp.zeros_like(ds_scr)

        @pl.when((pl.program_id(0) == 0) & (pl.program_id(1) == 0))
        def _():
            donw_ref[...] = jnp.zeros_like(donw_ref)

        def rsum(x):
            return jnp.sum(x, axis=-1, keepdims=True)

        def step(ci, carry):
            c = nch - 1 - ci
            rows = pl.ds(pl.multiple_of(c * CHUNK, CHUNK), CHUNK)
            gt = g_ref[rows, :]
            gtt = gt.T
            live = [head(c, rows, gt, gtt, h) for h in range(C_HEADS)]
            while live:
                live = [g for g in live if next(g, False)]
            return carry

        def head(c, rows, gt, gtt, h):
            hs = slice(C_DH * h, C_DH * (h + 1))
            q, k, v = q_ref[rows, hs], k_ref[rows, hs], v_ref[rows, hs]
            zc, o, dy, u, w = zc_ref[rows, hs], o_ref[rows, hs], dy_ref[rows, hs], u_ref[rows, hs], w_ref[rows, hs]
            tm_ = t_ref[rows, hs][:, 0:CHUNK]
            sv, dsv = ss_ref[h, c], ds_scr[h]
            sb, dsb = sv.astype(BF16), dsv.astype(BF16)
            sg = _sigmoid(zc)
            r = lax.rsqrt(jnp.mean(o * o, axis=-1, keepdims=True) + EPS)
            on = o * r
            ow = onw_ref[...]
            dzc_ref[rows, hs] = (dy * on * ow * _dsilu(zc, sg)).astype(BF16)
            t1 = dy * zc * sg
            donw_ref[...] += _fold8(t1 * on)
            don = t1 * ow
            do = r * (don - on * jnp.mean(don * on, axis=-1, keepdims=True))
            dob = do.astype(BF16)
            yield True
            beta, gc, incl, strict, dec, kb, a, mq, glast = _chunk_common(q, k, gt, gtt, h)
            egc = jnp.exp(gc)
            gl = jnp.exp(glast)
            ekd = jnp.exp(glast - gc)
            wb = w.astype(BF16)
            vnb = (u - _dot(wb, sb)).astype(BF16)
            qg = q * egc
            yield True
            dvn = _dot_tn(mq.astype(BF16), dob) + _dot((k * ekd).astype(BF16), dsb)
            dvnb = dvn.astype(BF16)
            dqg = _dot_nt(dob, sb)
            yield True
            dmq = jnp.where(incl, _dot_nt(dob, vnb), 0.0)
            dkd = _dot_nt(vnb, dsb)
            dgl = jnp.sum(rsum(dsv * sv), axis=0, keepdims=True)
            dw = -_dot_nt(dvnb, sb)
            yield True
            ds_scr[h] = gl * dsv + _dot_tn(qg.astype(BF16), dob) - _dot_tn(wb, dvnb)
            db = _dot3(_dot_tn, _split(tm_), _split(jnp.concatenate([dvn, dw], axis=1)))
            dbv, dbk = db[:, :C_DH], db[:, C_DH:]
            yield True
            da = -jnp.where(strict, _dot3(_dot_nt, _split(dbv), _split(u)) + _dot3(_dot_nt, _split(dbk), _split(w)), 0.0)
            yield True
            e = da * a + dmq * mq
            dgc = rsum(e) - rsum(e.T)
            dgb, dhb, kbf = (da * dec).astype(BF16), (dmq * dec).astype(BF16), k.astype(BF16)
            dkb = _dot(dgb, kbf)
            tk = rsum(dbk * k)
            rk = rsum(dkd * k) * ekd
            dq = _dot(dhb, kbf) + egc * dqg
            dk = _dot_tn(dgb, kb.astype(BF16)) + _dot_tn(dhb, q.astype(BF16)) + beta * (egc * dbk + dkb) + ekd * dkd
            dbeta = rsum(dbv * v) + tk * egc + rsum(dkb * k)
            dgc = dgc + tk * beta * egc + egc * rsum(dqg * q) - rk
            dglast = jnp.sum(rk, axis=0, keepdims=True) + dgl * gl
            dgc = dgc + jnp.where(_subl(dgc.shape) == CHUNK - 1, dglast, 0.0)
            dqkv_ref[0, rows, hs] = dq
            dqkv_ref[1, rows, hs] = dk
            dqkv_ref[2, rows, hs] = beta * dbv
            lane = _lane((CHUNK, LANES))
            dg_ref[h, rows, :] = jnp.where(lane == 0, dbeta, jnp.where(lane == 1, dgc, 0.0))

        lax.fori_loop(0, nch, step, 0)

    row = lambda b, i: b * nt + nt - 1 - i
    wb = lambda col: BS((tt, C_W), lambda b, i: (row(b, i), col))
    return pl.pallas_call(
        body, name="gdn_chunk_bwd", grid=(nseq, nt),
        in_specs=[wb(0), wb(1), wb(2), BS((tt, LANES), lambda b, i: (row(b, i), 0)), wb(P_ZC // C_W),
                  BS((1, LANES), lambda b, i: (0, 0)), wb(0), wb(Y_C // C_W), wb(0), wb(0), wb(0),
                  BS((None, C_HEADS, nch, C_DH, C_DH), lambda b, i: (b, 0, nt - 1 - i, 0, 0)), BS(memory_space=pl.ANY)],
        out_specs=[wb(P_ZC // C_W), BS((3, tt, C_W), lambda b, i: (0, row(b, i), 0)),
                   BS((C_HEADS, tt, LANES), lambda b, i: (0, row(b, i), 0)), BS((SUB, LANES), lambda b, i: (0, 0))],
        out_shape=[S(dp.shape, dp.dtype), S((3, n, C_W), F32), S((C_HEADS, n, LANES), F32), S((SUB, LANES), F32)],
        input_output_aliases={12: 0},
        scratch_shapes=[pltpu.VMEM((C_HEADS, C_DH, C_DH), F32)],
        compiler_params=_cp("arbitrary", "arbitrary"),
    )(qkv, qkv, qkv, gates, p, onw, o, dy, u, w, tinv, ss, dp)


def _gdn_gates_bwd(dgate, p, alog_l, dtb_l, dp, tm=256):
    n = p.shape[0]
    acc = BS((SUB, LANES), lambda i: (0, 0))

    def body(dg_ref, ba_ref, al_ref, db_ref, dp_in_ref, dba_ref, dal_ref, ddb_ref):
        @pl.when(pl.program_id(0) == 0)
        def _():
            dal_ref[...] = jnp.zeros_like(dal_ref)
            ddb_ref[...] = jnp.zeros_like(ddb_ref)

        blk = ba_ref[...]
        lane = _lane(blk.shape)
        dbeta = jnp.zeros_like(blk)
        dgc = jnp.zeros_like(blk)
        for h in range(C_HEADS):
            dbeta = dbeta + jnp.where(lane == GB + h, _col(dg_ref[h], 0), 0.0)
            dgc = dgc + jnp.where(lane == GG + h, _col(dg_ref[h], 1), 0.0)
        tri = (_subl((CHUNK, CHUNK)) <= _lane((CHUNK, CHUNK))).astype(F32)
        dg = jnp.concatenate([_dot(tri, dgc[CHUNK * c:CHUNK * (c + 1)], HI) for c in range(tm // CHUNK)], axis=0)
        beta = _sigmoid(blk)
        z = blk + db_ref[...]
        ea = jnp.exp(al_ref[...])
        isg = (lane >= GG) & (lane < GG + C_HEADS)
        dz = jnp.where(isg, -dg * ea * _sigmoid(z), 0.0)
        dal_ref[...] += _fold8(jnp.where(isg, -dg * ea * _softplus(z), 0.0))
        ddb_ref[...] += _fold8(dz)
        out = jnp.where(lane < GG, dbeta * beta * (1.0 - beta), dz)
        dba_ref[...] = jnp.concatenate([out, jnp.zeros_like(out)], axis=1).astype(BF16)

    return pl.pallas_call(
        body, name="gdn_gates_bwd", grid=(n // tm,),
        in_specs=[BS((C_HEADS, tm, LANES), lambda i: (0, i, 0)), BS((tm, LANES), lambda i: (i, P_BA // LANES)),
                  BS((1, LANES), lambda i: (0, 0)), BS((1, LANES), lambda i: (0, 0)), BS(memory_space=pl.ANY)],
        out_specs=[BS((tm, 2 * LANES), lambda i: (i, P_BA // (2 * LANES))), acc, acc],
        out_shape=[S(dp.shape, dp.dtype), S((SUB, LANES), F32), S((SUB, LANES), F32)],
        input_output_aliases={4: 0},
        compiler_params=_cp("arbitrary"),
    )(dgate, p, alog_l, dtb_l, dp)


def _gdn_pre_bwd(p, dqkv, xc, dp, ccw, nseq, tm=256):
    n = p.shape[0]
    t = n // nseq
    nt = t // tm
    wide = 3 * C_W
    row = lambda b, i: b * nt + i
    prev = lambda b, i: jnp.maximum((b * t + i * tm) // HALO_C - 1, 0)
    nxt = lambda b, i: jnp.minimum((b * t + (i + 1) * tm) // HALO_C, n // HALO_C - 1)

    def d_conv_out(d, xc, part):
        sg = _sigmoid(xc)
        act = xc * sg
        if part < 2:
            cs = QS if part == 0 else 1.0
            rn = lax.rsqrt(jnp.sum(act * act, axis=-1, keepdims=True) + EPS)
            d = cs * rn * d - act * (cs * rn * rn * rn * jnp.sum(d * act, axis=-1, keepdims=True))
        return d * _dsilu(xc, sg)

    def body(x_ref, xh_ref, d_ref, dn_ref, xc_ref, xn_ref, w_ref, dp_in_ref, dx_ref, dw_ref, buf_ref, dbuf_ref):
        i = pl.program_id(1)

        @pl.when((pl.program_id(0) == 0) & (i == 0))
        def _():
            dw_ref[...] = jnp.zeros_like(dw_ref)

        buf_ref[0:HALO_C, :] = jnp.where(i > 0, xh_ref[...], 0.0)
        buf_ref[HALO_C:, :] = x_ref[...]
        for c in range(NCB):
            cs = slice(LANES * c, LANES * (c + 1))
            part, hd = divmod(c, C_HEADS)
            hs = slice(LANES * hd, LANES * (hd + 1))
            d = d_conv_out(d_ref[part, :, hs], xc_ref[:, cs], part)
            dbuf_ref[0:tm, cs] = d
            dbuf_ref[tm:, cs] = jnp.where(i < nt - 1, d_conv_out(dn_ref[part, :, hs], xn_ref[:, cs], part), 0.0)
            dx = jnp.zeros((tm, LANES), F32)
            for k in range(C_K):
                dx = dx + w_ref[k:k + 1, cs] * dbuf_ref[pl.ds(C_K - 1 - k, tm), cs]
                dw_ref[SUB * k:SUB * (k + 1), cs] += _fold8(d * buf_ref[pl.ds(HALO_C - C_K + 1 + k, tm), cs])
            dx_ref[:, cs] = dx.astype(BF16)

    return pl.pallas_call(
        body, name="gdn_pre_bwd", grid=(nseq, nt),
        in_specs=[BS((tm, wide), lambda b, i: (row(b, i), P_QKV // wide)), BS((HALO_C, wide), lambda b, i: (prev(b, i), P_QKV // wide)),
                  BS((3, tm, C_W), lambda b, i: (0, row(b, i), 0)), BS((3, HALO_C, C_W), lambda b, i: (0, nxt(b, i), 0)),
                  BS((tm, wide), lambda b, i: (row(b, i), 0)), BS((HALO_C, wide), lambda b, i: (nxt(b, i), 0)),
                  BS((SUB, wide), lambda b, i: (0, 0)), BS(memory_space=pl.ANY)],
        out_specs=[BS((tm, wide), lambda b, i: (row(b, i), P_QKV // wide)), BS((SUB * C_K, wide), lambda b, i: (0, 0))],
        out_shape=[S(dp.shape, dp.dtype), S((SUB * C_K, wide), F32)], input_output_aliases={7: 0},
        scratch_shapes=[pltpu.VMEM((HALO_C + tm, wide), F32)] * 2,
        compiler_params=_cp("arbitrary", "arbitrary"),
    )(p, p, dqkv, dqkv, xc, xc, ccw, dp)


ANY = BS(memory_space=pl.ANY)


def _my_pos():
    return lax.axis_index("x"), lax.axis_index("y"), lax.axis_index("c")


def _dev_index(dev):
    return 4 * dev[0] + 2 * dev[1] + dev[2]


def _all_gather(shards, after=None):
    nk = len(shards)

    tail = [] if after is None else [after]

    def body(*refs):
        ins, outs = refs[:nk], refs[nk + len(tail):2 * nk + len(tail)]
        send, recv, loc = refs[2 * nk + len(tail):]
        x, y, c = _my_pos()
        me, sib = (x, y, c), (x, y, 1 - c)
        chips = [(1 - x, y), (x, 1 - y), (1 - x, 1 - y)]

        def rows(t, dev):
            r = ins[t].shape[0]
            return outs[t].at[pl.ds(pl.multiple_of(_dev_index(dev) * r, SUB), r), :]

        def copy(t, k, block, to, src=None):
            return pltpu.make_async_remote_copy(
                src_ref=rows(t, block) if src is None else src, dst_ref=rows(t, block),
                send_sem=send.at[t, k], recv_sem=recv.at[t, k], device_id=to, device_id_type=MESH)

        mine = [pltpu.make_async_copy(ins[t], rows(t, me), loc.at[t]) for t in range(nk)]
        for cp in mine:
            cp.start()
        first = []
        for t in range(nk):
            first.append(copy(t, 0, me, sib, src=ins[t]))
            first += [copy(t, 1 + j, me, (*chip, c), src=ins[t]) for j, chip in enumerate(chips)]
        for cp in first:
            cp.start()
        passed = []
        for j, chip in enumerate(chips):
            for t in range(nk):
                copy(t, 1 + j, (*chip, c), me).wait_recv()
                cp = copy(t, 4 + j, (*chip, c), sib)
                cp.start()
                passed.append(cp)
        for t in range(nk):
            copy(t, 0, sib, me).wait_recv()
            for j, chip in enumerate(chips):
                copy(t, 4 + j, (*chip, 1 - c), me).wait_recv()
        for cp in first + passed:
            cp.wait_send()
        for cp in mine:
            cp.wait()

    return pl.pallas_call(
        body, name="all_gather", in_specs=[ANY] * (nk + len(tail)), out_specs=[ANY] * nk,
        out_shape=[S((N_DEV * a.shape[0], a.shape[1]), a.dtype) for a in shards],
        scratch_shapes=[pltpu.SemaphoreType.DMA((nk, 7)), pltpu.SemaphoreType.DMA((nk, 7)), pltpu.SemaphoreType.DMA((nk,))],
    )(*shards, *tail)


SEM = BS(memory_space=pltpu.SEMAPHORE)
HBM = BS(memory_space=pltpu.HBM)
EFFECT = pltpu.SideEffectType.DATAFLOW_SIDE_EFFECTING


def _peers(x, y, c):
    return [((1 - x) if k & 4 else x, (1 - y) if k & 2 else y, (1 - c) if k & 1 else c) for k in range(1, N_DEV)]


def _exchange_copy(kind, src, land, send, recv, t, k, peer, me, arriving):
    frm = peer if arriving else me
    if kind == "gather":
        r = src.shape[0]
        s_ref = src
        d_ref = land.at[pl.ds(pl.multiple_of(_dev_index(frm) * r, SUB), r), :]
    else:
        r = src.shape[0] // N_DEV
        s_ref = src.at[pl.ds(pl.multiple_of(_dev_index(peer) * r, SUB), r), :]
        d_ref = land.at[_dev_index(frm)]
    sem = t * (N_DEV - 1) + k
    return pltpu.make_async_remote_copy(src_ref=s_ref, dst_ref=d_ref, send_sem=send.at[sem], recv_sem=recv.at[sem],
                                        device_id=peer, device_id_type=MESH)


def _own_copy(kind, src, land, own, t, me):
    if kind == "gather":
        r = src.shape[0]
        return pltpu.make_async_copy(src, land.at[pl.ds(pl.multiple_of(_dev_index(me) * r, SUB), r), :], own.at[t])
    r = src.shape[0] // N_DEV
    return pltpu.make_async_copy(src.at[pl.ds(pl.multiple_of(_dev_index(me) * r, SUB), r), :], land.at[_dev_index(me)], own.at[t])


def _exchange_start(kind, srcs, after, name):
    nk = len(srcs)
    if kind == "gather":
        lands = [lax.empty((N_DEV * a.shape[0], a.shape[1]), a.dtype) for a in srcs]
    else:
        lands = [lax.empty((N_DEV, a.shape[0] // N_DEV, a.shape[1]), a.dtype) for a in srcs]

    def body(*refs):
        src, land = refs[:nk], refs[nk:2 * nk]
        send, recv, own = refs[2 * nk + 1], refs[2 * nk + 2], refs[2 * nk + 3]
        token = refs[-1]
        x, y, c = _my_pos()
        me = (x, y, c)
        for t in range(nk):
            _own_copy(kind, src[t], land[t], own, t, me).start()
            for k, peer in enumerate(_peers(x, y, c)):
                _exchange_copy(kind, src[t], land[t], send, recv, t, k, peer, me, False).start()
        token[...] = jnp.zeros_like(token)

    hbm = lambda a: pltpu.HBM(a.shape, a.dtype)
    out = pl.pallas_call(
        body, name=name,
        out_shape=(pltpu.SemaphoreType.DMA((nk * (N_DEV - 1),)), pltpu.SemaphoreType.DMA((nk * (N_DEV - 1),)),
                   pltpu.SemaphoreType.DMA((nk,)), *[hbm(a) for a in srcs], *[hbm(a) for a in lands], S((SUB, LANES), F32)),
        in_specs=[HBM] * (2 * nk) + [ANY],
        out_specs=(SEM, SEM, SEM, *[HBM] * (2 * nk), BS(memory_space=pltpu.VMEM)),
        input_output_aliases={i: 3 + i for i in range(2 * nk)},
        compiler_params=pltpu.CompilerParams(has_side_effects=EFFECT),
    )(*[pltpu.with_memory_space_constraint(a, pltpu.HBM) for a in (*srcs, *lands)], after)
    return dict(kind=kind, nk=nk, send=out[0], recv=out[1], own=out[2], srcs=out[3:3 + nk], lands=out[3 + nk:3 + 2 * nk],
                token=out[-1])


def _exchange_wait(ex, after, name):
    kind, nk = ex["kind"], ex["nk"]

    def body(*refs):
        src, land = refs[:nk], refs[nk:2 * nk]
        send, recv, own = refs[2 * nk], refs[2 * nk + 1], refs[2 * nk + 2]
        x, y, c = _my_pos()
        me = (x, y, c)
        for t in range(nk):
            _own_copy(kind, src[t], land[t], own, t, me).wait()
            for k, peer in enumerate(_peers(x, y, c)):
                _exchange_copy(kind, src[t], land[t], send, recv, t, k, peer, me, False).wait_send()
                _exchange_copy(kind, src[t], land[t], send, recv, t, k, peer, me, True).wait_recv()

    hbm = lambda a: pltpu.HBM(a.shape, a.dtype)
    out = pl.pallas_call(
        body, name=name,
        out_shape=(*[hbm(a) for a in ex["srcs"]], *[hbm(a) for a in ex["lands"]]),
        in_specs=[HBM] * (2 * nk) + [SEM, SEM, SEM, ANY], out_specs=tuple([HBM] * (2 * nk)),
        input_output_aliases={i: i for i in range(2 * nk)},
        compiler_params=pltpu.CompilerParams(has_side_effects=EFFECT),
    )(*ex["srcs"], *ex["lands"], ex["send"], ex["recv"], ex["own"], after)
    return list(out[nk:])


BLOCK_BYTES = 4 << 20


def _row_tile(rows, row_bytes, align):
    best = align
    for tr in range(align, rows + 1, align):
        if rows % tr == 0 and tr * row_bytes <= BLOCK_BYTES:
            best = tr
    return best


def _sum8(a):
    _, r, w = a.shape
    tr = _row_tile(r, N_DEV * w * a.dtype.itemsize, 32 // a.dtype.itemsize)

    def body(a_ref, o_ref):
        acc = a_ref[0].astype(F32)
        for d in range(1, N_DEV):
            acc = acc + a_ref[d].astype(F32)
        o_ref[...] = acc

    return pl.pallas_call(
        body, name="sum8", grid=(r // tr,), in_specs=[BS((N_DEV, tr, w), lambda i: (0, i, 0))],
        out_specs=BS((tr, w), lambda i: (i, 0)), out_shape=S((r, w), F32), compiler_params=_cp("arbitrary"),
    )(a)


def _adamw(w, g, m, v):
    r, c = w.shape
    tr = _row_tile(r, c * 4 * 2, SUB)

    def body(w_ref, g_ref, m_ref, v_ref, d_ref, mo_ref, vo_ref):
        d_ref[...], mo_ref[...], vo_ref[...] = _adam_update(w_ref[...], g_ref[...], m_ref[...], v_ref[...])

    blk = BS((tr, c), lambda i: (i, 0))
    return pl.pallas_call(
        body, name="adamw", grid=(r // tr,), in_specs=[blk] * 4, out_specs=[blk] * 3,
        out_shape=[S((r, c), F32)] * 3, compiler_params=_cp("arbitrary"),
    )(w, g, m, v)


def _sum8_t(a, tc=256):
    _, r, w = a.shape

    def body(a_ref, o_ref):
        acc = a_ref[0].astype(F32)
        for d in range(1, N_DEV):
            acc = acc + a_ref[d].astype(F32)
        o_ref[...] = acc.T

    return pl.pallas_call(
        body, name="sum8_t", grid=(w // tc,), in_specs=[BS((N_DEV, r, tc), lambda j: (0, 0, j))],
        out_specs=BS((tc, r), lambda j: (j, 0)), out_shape=S((w, r), F32), compiler_params=_cp("arbitrary"),
    )(a)


def _rows_view(a):
    nl, r, c = a.shape
    assert nl == 2
    return a.transpose(2, 0, 1).reshape(c, nl, r // LANES, LANES).transpose(0, 2, 1, 3).reshape(-1, LANES)


def _rows_view_back(a, shape):
    nl, r, c = shape
    return a.reshape(c, r // LANES, nl, LANES).transpose(0, 2, 1, 3).reshape(c, nl, r).transpose(1, 2, 0)


def _adamw_rows(w, g, m, v, tr=2048):
    n = w.shape[0]

    def body(w_ref, g_ref, m_ref, v_ref, d_ref, mo_ref, vo_ref):
        d_ref[...], mo_ref[...], vo_ref[...] = _adam_update(w_ref[...], g_ref[...], m_ref[...], v_ref[...])

    blk = BS((tr, LANES), lambda i: (i, 0))
    return pl.pallas_call(
        body, name="adamw_rows", grid=(pl.cdiv(n, tr),), in_specs=[blk] * 4, out_specs=[blk] * 3,
        out_shape=[S((n, LANES), F32)] * 3, compiler_params=_cp("arbitrary"),
    )(w, g, m, v)


def _adam_update(w, g, m, v):
    m2 = ADAM_B1 * m + (1.0 - ADAM_B1) * g
    v2 = ADAM_B2 * v + (1.0 - ADAM_B2) * (g * g)
    m_hat = m2 / (1.0 - ADAM_B1 ** ADAM_STEP)
    v_hat = v2 / (1.0 - ADAM_B2 ** ADAM_STEP)
    return -ADAM_LR * (m_hat / (jnp.sqrt(v_hat) + ADAM_EPS) + ADAM_WD * w), m2, v2


def _adamw_layer(w, g, m, v, l, prev):
    nl, r, c = w.shape
    tr = _row_tile(r, c * 4 * 2, SUB)

    def body(w_ref, g_ref, m_ref, v_ref, *refs):
        go_ref, d_ref, mo_ref, vo_ref = refs[-4:]
        gv = g_ref[...]
        go_ref[...] = gv
        d_ref[...], mo_ref[...], vo_ref[...] = _adam_update(w_ref[...], gv, m_ref[...], v_ref[...])

    slot = BS((None, tr, c), lambda i: (l, i, 0))
    keep = [] if prev is None else [ANY] * 4
    return pl.pallas_call(
        body, name="adamw_layer", grid=(r // tr,), in_specs=[slot, BS((tr, c), lambda i: (i, 0)), slot, slot] + keep,
        out_specs=[slot] * 4, out_shape=[S((nl, r, c), F32)] * 4,
        input_output_aliases={} if prev is None else {4 + i: i for i in range(4)},
        compiler_params=_cp("arbitrary"),
    )(w, g, m, v, *(prev or ()))


def _blob(arrays):
    flat = jnp.concatenate([a.reshape(-1) for a in arrays])
    rows = -(-flat.shape[0] // (SUB * LANES)) * SUB
    return jnp.pad(flat, (0, rows * LANES - flat.shape[0])).reshape(rows, LANES)


def _unblob(blob, shapes, lead=()):
    flat = blob.reshape(lead + (-1,))
    out, off = [], 0
    for s in shapes:
        size = math.prod(s)
        out.append(flat[..., off:off + size].reshape(lead + tuple(s)))
        off += size
    return out


def _lanes6(a):
    return jnp.zeros((1, LANES), F32).at[0, GG:GG + C_HEADS].set(a)


def _y_rows(w):
    return jnp.concatenate([w[0:A_W], w[A_W + B_W:], w[A_W:A_W + B_W]], axis=0)


def _y_rows_back(g):
    return jnp.concatenate([g[0:A_W], g[A_W + C_W:], g[A_W:A_W + C_W]], axis=0)


SMALL = ("norm_w", "q_norm_w", "k_norm_w", "sinks", "b_conv_b", "b_ln_w", "b_ln_b", "b_pw_b", "c_a_log", "c_dt_bias",
         "c_onorm_w", "b_conv_w", "c_conv_w")
ORDER = ("norm_w", "w_in", "q_norm_w", "k_norm_w", "sinks", "b_conv_w", "b_conv_b", "b_ln_w", "b_ln_b", "b_pw_w", "b_pw_b",
         "c_conv_w", "c_a_log", "c_dt_bias", "c_onorm_w", "w_out")


def kernel(x, positions, norm_w, w_in, q_norm_w, k_norm_w, sinks, b_conv_w, b_conv_b, b_ln_w, b_ln_b, b_pw_w, b_pw_b, c_conv_w, c_a_log, c_dt_bias, c_onorm_w, w_out, loss_target, m_norm_w, m_w_in, m_q_norm_w, m_k_norm_w, m_sinks, m_b_conv_w, m_b_conv_b, m_b_ln_w, m_b_ln_b, m_b_pw_w, m_b_pw_b, m_c_conv_w, m_c_a_log, m_c_dt_bias, m_c_onorm_w, m_w_out, v_norm_w, v_w_in, v_q_norm_w, v_k_norm_w, v_sinks, v_b_conv_w, v_b_conv_b, v_b_ln_w, v_b_ln_b, v_b_pw_w, v_b_pw_b, v_c_conv_w, v_c_a_log, v_c_dt_bias, v_c_onorm_w, v_w_out):
    W = dict(norm_w=norm_w, w_in=w_in, q_norm_w=q_norm_w, k_norm_w=k_norm_w, sinks=sinks, b_conv_w=b_conv_w, b_conv_b=b_conv_b,
             b_ln_w=b_ln_w, b_ln_b=b_ln_b, b_pw_w=b_pw_w, b_pw_b=b_pw_b, c_conv_w=c_conv_w, c_a_log=c_a_log,
             c_dt_bias=c_dt_bias, c_onorm_w=c_onorm_w, w_out=w_out)
    M = dict(norm_w=m_norm_w, w_in=m_w_in, q_norm_w=m_q_norm_w, k_norm_w=m_k_norm_w, sinks=m_sinks, b_conv_w=m_b_conv_w,
             b_conv_b=m_b_conv_b, b_ln_w=m_b_ln_w, b_ln_b=m_b_ln_b, b_pw_w=m_b_pw_w, b_pw_b=m_b_pw_b, c_conv_w=m_c_conv_w,
             c_a_log=m_c_a_log, c_dt_bias=m_c_dt_bias, c_onorm_w=m_c_onorm_w, w_out=m_w_out)
    V = dict(norm_w=v_norm_w, w_in=v_w_in, q_norm_w=v_q_norm_w, k_norm_w=v_k_norm_w, sinks=v_sinks, b_conv_w=v_b_conv_w,
             b_conv_b=v_b_conv_b, b_ln_w=v_b_ln_w, b_ln_b=v_b_ln_b, b_pw_w=v_b_pw_w, b_pw_b=v_b_pw_b, c_conv_w=v_c_conv_w,
             c_a_log=v_c_a_log, c_dt_bias=v_c_dt_bias, c_onorm_w=v_c_onorm_w, w_out=v_w_out)
    nseq, t, d = x.shape
    n = nseq * t
    tr = min(256, t)
    tmm = min(512, n)
    tmw = min(1024, n)
    tkk = min(2048, n)
    me = _dev_index(_my_pos())
    xs = [x.reshape(n, d)]
    tgt = loss_target.reshape(n, d)
    tabs = _rope_tables(positions.reshape(n))

    win_p = _pack_cols(w_in).astype(BF16)
    wout_b = w_out.astype(BF16)
    sharded_small = (b_pw_w, b_conv_w, c_conv_w)
    g_win0, g_small = _all_gather([win_p[0], _blob(sharded_small)])
    win = [g_win0]
    later = _exchange_start("gather", [win_p[1], wout_b[0], wout_b[1]], g_small, "gather_start")
    pw_all, cw_all, ccw_all = _unblob(g_small, [a.shape for a in sharded_small], lead=(N_DEV,))
    pw_all = pw_all.transpose(1, 0, 2, 3).reshape(DEPTH, B_W, B_W).astype(BF16)
    cw_all = cw_all.transpose(1, 2, 0, 3).reshape(DEPTH, B_K, B_W)
    ccw_all = ccw_all.transpose(1, 2, 0, 3).reshape(DEPTH, C_K, 3 * C_W)

    def layer_params(l):
        return dict(
            nw=norm_w[l][None], qw=jnp.tile(q_norm_w[l], 2)[None], kw=jnp.tile(k_norm_w[l], 2)[None], sinks=sinks[l],
            cw=jnp.pad(cw_all[l], ((0, HALO_B - B_K), (0, 0))), cb=b_conv_b[l][None], lnw=b_ln_w[l][None], lnb=b_ln_b[l][None],
            pw=pw_all[l], pwb=b_pw_b[l][None], ccw=jnp.pad(ccw_all[l], ((0, SUB - C_K), (0, 0))),
            alog=_lanes6(c_a_log[l]), dtb=_lanes6(c_dt_bias[l]), onw=c_onorm_w[l][None])

    saved = []
    for l in range(DEPTH):
        q = layer_params(l)
        nw = q["nw"] + later["token"][0:1, 0:1] if l == 0 else q["nw"]
        p, h = _inproj(xs[l], nw, win[l], tm=tmw)
        y, o_a, lse = _attn_fwd(p, tabs, q["qw"], q["kw"], q["sinks"], nseq)
        gates = _gdn_gates_fwd(p, q["alog"], q["dtb"], tm=tr)
        xc, qkv = _gdn_pre_fwd(p, q["ccw"], nseq, tm=tr)
        y, o_c, u, w, tinv, ss = _gdn_chunk_fwd(qkv, gates, p, y, q["onw"], nseq)
        y, hc = _conf_fwd(p, y, q["cw"], q["cb"], q["lnw"], q["lnb"], q["pw"], q["pwb"], nseq, tm=tr)
        saved.append(dict(q=q, p=p, h=h, y=y, o_a=o_a, lse=lse, gates=gates, xc=xc, qkv=qkv, o_c=o_c, u=u, w=w, tinv=tinv,
                          ss=ss, hc=hc))
        if l == 0:
            g_win1, g_wout0, g_wout1 = _exchange_wait(later, y, "gather_wait")
            win.append(g_win1)
            wout = [_y_rows(g_wout0), _y_rows(g_wout1)]
        if l + 1 < DEPTH:
            xs.append(_outproj(xs[l], y, wout[l], tm=tmw, tn=512))
        else:
            dxn, lsum = _outproj_loss(xs[l], y, wout[l], tgt, tm=tmw, tn=512)
    loss = lax.psum(jnp.sum(lsum) * (0.5 / d), ("x", "y", "c"))

    sent, smalls = [None] * DEPTH, [None] * DEPTH
    for l in reversed(range(DEPTH)):
        s = saved[l]
        q, p = s["q"], s["p"]
        dy = _matmul(dxn, wout[l], "nt", F32, tmw, 512, d, "outproj_bwd_dy")
        dwout = _y_rows_back(_matmul(s["y"], dxn, "tn", BF16, 1024, 1024, tkk, "outproj_bwd_dw"))
        dp, dkv, dqw, dkw, dsk = _attn_bwd(p, dy, s["o_a"], s["lse"], tabs, q["qw"], q["kw"], q["sinks"], nseq)
        dp, dqkv, dgate, donw = _gdn_chunk_bwd(s["qkv"], s["gates"], p, dy, dp, q["onw"], s["o_c"], s["u"], s["w"],
                                               s["tinv"], s["ss"], nseq)
        dp, dccw = _gdn_pre_bwd(p, dqkv, s["xc"], dp, q["ccw"], nseq, tm=tr)
        early = P_K // 768
        dwin_a = _matmul(s["h"], dp, "tn", BF16, 1024, 768, tkk, "inproj_bwd_dw_a", b_cols=(0, early))
        sent_a = _exchange_start("scatter", [dwin_a, dwout], donw, "scatter_start_a%d" % l)
        dp = _put_cols(dp, dkv, P_K, sent_a["token"], tm=tmm)
        dp, dal, ddb = _gdn_gates_bwd(dgate, p, q["alog"], q["dtb"], dp, tm=tr)
        dp, dhc, dpw, dpwb, dlnw, dlnb, dcb = _conf_bwd1(p, dy, dp, s["hc"], q["lnw"], q["lnb"], q["pw"], q["pwb"], tm=tr)
        dp, dcw = _conf_bwd2(p, dhc, dp, q["cw"], nseq, tm=tr)
        dwin_b = _matmul(s["h"], dp, "tn", BF16, 1024, 768, tkk, "inproj_bwd_dw_b", b_cols=(early, P_W // 768 - early))
        sent_b = _exchange_start("scatter", [dwin_b, dpw], dpwb, "scatter_start_b%d" % l)
        sent[l] = (sent_a, sent_b)
        dxn, dnw = _inproj_bwd_dx(dp, win[l], xs[l], q["nw"] + sent_b["token"][0:1, 0:1], dxn, tm=tmm)
        halves = lambda a: a.sum(0)[:A_DH] + a.sum(0)[A_DH:]
        smalls[l] = dict(
            norm_w=dnw.sum(0), q_norm_w=halves(dqw), k_norm_w=halves(dkw), sinks=dsk.sum(0)[:A_HEADS], b_conv_b=dcb.sum(0),
            b_ln_w=dlnw.sum(0), b_ln_b=dlnb.sum(0), b_pw_b=dpwb.sum(0), c_a_log=dal.sum(0)[GG:GG + C_HEADS],
            c_dt_bias=ddb.sum(0)[GG:GG + C_HEADS], c_onorm_w=donw.sum(0),
            b_conv_w=dcw.reshape(B_K, SUB, B_W).sum(1), c_conv_w=dccw.reshape(C_K, SUB, 3 * C_W).sum(1))
    grad_x = dxn.reshape(nseq, t, d)

    G, delta, new_m, new_v = {}, {}, {}, {}
    big = ("w_in", "w_out", "b_pw_w")
    stacks = {k: None for k in big}
    after = dxn
    g_t = [None] * DEPTH
    for l in reversed(range(DEPTH)):
        r_win_a, r_wout = _exchange_wait(sent[l][0], after, "scatter_wait_a%d" % l)
        r_win_b, r_pw = _exchange_wait(sent[l][1], r_wout, "scatter_wait_b%d" % l)
        g_t[l] = jnp.concatenate([_sum8_t(r_win_a), _sum8_t(r_win_b)], axis=0).reshape(P_W, -1, LANES)
        for k, r in (("w_out", r_wout), ("b_pw_w", r_pw)):
            stacks[k] = _adamw_layer(W[k], _sum8(r), M[k], V[k], l, stacks[k])
        after = stacks["w_out"][1]
    g_in = jnp.stack(g_t, axis=2).reshape(-1, LANES)
    g_in = _unpack_cols(g_in, axis=0, each=g_in.shape[0] // P_W)
    rows = _adamw_rows(_rows_view(w_in), g_in, _rows_view(m_w_in), _rows_view(v_w_in))
    stacks["w_in"] = [_rows_view_back(a, w_in.shape) for a in (g_in, *rows)]
    for k in big:
        G[k], delta[k], new_m[k], new_v[k] = stacks[k]
    part = _blob([jnp.stack([smalls[l][k] for l in range(DEPTH)]) for k in SMALL])
    (tot,) = _all_gather([part], after=rows[0])
    tot = _sum8(tot.reshape(N_DEV, part.shape[0], LANES))
    full_shapes = [(DEPTH,) + smalls[0][k].shape for k in SMALL]
    for k, g in zip(SMALL, _unblob(tot, full_shapes)):
        G[k] = g
    G["b_conv_w"] = lax.dynamic_slice_in_dim(G["b_conv_w"], me * (B_W // N_DEV), B_W // N_DEV, axis=2)
    G["c_conv_w"] = lax.dynamic_slice_in_dim(G["c_conv_w"], me * (3 * C_W // N_DEV), 3 * C_W // N_DEV, axis=2)
    dl, mo, vo = _adamw(*[_blob([src[k] for k in SMALL]) for src in (W, G, M, V)])
    shapes = [W[k].shape for k in SMALL]
    for k, a, b, c in zip(SMALL, _unblob(dl, shapes), _unblob(mo, shapes), _unblob(vo, shapes)):
        delta[k], new_m[k], new_v[k] = a, b, c
    return (loss, grad_x, *[G[k] for k in ORDER], *[delta[k] for k in ORDER], *[new_m[k] for k in ORDER],
            *[new_v[k] for k in ORDER])
```

```python
import functools
import math

import jax
import jax.numpy as jnp
from jax import lax
from jax.experimental import pallas as pl
from jax.experimental.pallas import tpu as pltpu

F32 = jnp.float32
BF16 = jnp.bfloat16
HI = lax.Precision.HIGHEST
MESH = pl.DeviceIdType.MESH
S = jax.ShapeDtypeStruct
BS = pl.BlockSpec

N_DEV = 8
DEPTH = 2
D_MODEL = 2048
A_HEADS, A_KV, A_DH, A_W, A_KVW = 12, 4, 64, 768, 256
ROT = 16
THETA = 500000.0
ABLK = 128
B_W, B_K = 512, 31
C_HEADS, C_DH, C_W, C_K, CHUNK = 6, 128, 768, 4, 64
EPS = 1e-6
IN_COLS = 6668
P_Q, P_ZA, P_ZC, P_QKV, P_K, P_V, P_UB, P_ZB, P_BA, P_W = 0, 768, 1536, 2304, 4608, 4864, 5120, 6144, 6656, 6912
Y_A, Y_C, Y_B = 0, 768, 1536
LANES = 128
SUB = 8

ADAM_LR, ADAM_B1, ADAM_B2, ADAM_EPS, ADAM_WD, ADAM_STEP = 0.001, 0.9, 0.999, 1e-08, 0.01, 10


def _cp(*sem, vmem=None):
    kw = {}
    if sem:
        kw["dimension_semantics"] = sem
    if vmem:
        kw["vmem_limit_bytes"] = vmem
    return pltpu.CompilerParams(**kw)


def _pack_cols(w):
    z = jnp.zeros(w.shape[:-1] + (P_W - IN_COLS,), w.dtype)
    return jnp.concatenate([w[..., 0:768], w[..., 1280:2048], w[..., 5900:6668], w[..., 3584:5888],
                            w[..., 768:1024], w[..., 1024:1280], w[..., 2048:3072], w[..., 3072:3584],
                            w[..., 5888:5900], z], axis=-1)


def _unpack_cols(g, axis=-1, each=1):
    parts = ((P_Q, 768), (P_K, 256), (P_V, 256), (P_ZA, 768), (P_UB, 1024), (P_ZB, 512), (P_QKV, 2304), (P_BA, 12), (P_ZC, 768))
    return jnp.concatenate([lax.slice_in_dim(g, each * o, each * (o + n), axis=axis) for o, n in parts], axis=axis)


def _sigmoid(x):
    return 0.5 * jnp.tanh(0.5 * x) + 0.5


def _dsilu(x, sg):
    return sg * (1.0 + x * (1.0 - sg))


def _fold8(x):
    r, c = x.shape
    return x.reshape(r // SUB, SUB, c).sum(axis=0)


def _dot(a, b, prec=None):
    return jnp.dot(a, b, preferred_element_type=F32, precision=prec)


def _dot_nt(a, b, prec=None):
    return lax.dot_general(a, b, (((1,), (1,)), ((), ())), preferred_element_type=F32, precision=prec)


def _dot_tn(a, b, prec=None):
    return lax.dot_general(a, b, (((0,), (0,)), ((), ())), preferred_element_type=F32, precision=prec)


def _lane(shape):
    return lax.broadcasted_iota(jnp.int32, shape, 1)


def _subl(shape):
    return lax.broadcasted_iota(jnp.int32, shape, 0)


def _col(x, j):
    return jnp.sum(jnp.where(_lane(x.shape) == j, x, 0.0), axis=-1, keepdims=True)


def _inproj(x, nw, w, tm=512, tn=768):
    n, d = x.shape
    pw = w.shape[1]

    def body(x_ref, nw_ref, w_ref, p_ref, h_ref):
        @pl.when(pl.program_id(1) == 0)
        def _():
            xv = x_ref[...]
            r = lax.rsqrt(jnp.mean(xv * xv, axis=-1, keepdims=True) + EPS)
            h_ref[...] = (xv * r * nw_ref[...]).astype(BF16)

        p_ref[...] = _dot(h_ref[...], w_ref[...])

    return pl.pallas_call(
        body, name="inproj", grid=(n // tm, pw // tn),
        in_specs=[BS((tm, d), lambda i, j: (i, 0)), BS((1, d), lambda i, j: (0, 0)), BS((d, tn), lambda i, j: (0, j))],
        out_specs=[BS((tm, tn), lambda i, j: (i, j)), BS((tm, d), lambda i, j: (i, 0))],
        out_shape=[S((n, pw), F32), S((n, d), BF16)],
        compiler_params=_cp("arbitrary", "arbitrary"),
    )(x, nw, w)


def _outproj(x, y, w, tm=512, tn=1024):
    n, d = x.shape
    k = y.shape[1]

    def body(x_ref, y_ref, w_ref, o_ref):
        o_ref[...] = x_ref[...] + _dot(y_ref[...], w_ref[...])

    return pl.pallas_call(
        body, name="outproj", grid=(n // tm, d // tn),
        in_specs=[BS((tm, tn), lambda i, j: (i, j)), BS((tm, k), lambda i, j: (i, 0)), BS((k, tn), lambda i, j: (0, j))],
        out_specs=BS((tm, tn), lambda i, j: (i, j)),
        out_shape=S((n, d), F32),
        compiler_params=_cp("arbitrary", "arbitrary"),
    )(x, y, w)


def _outproj_loss(x, y, w, tgt, tm=512, tn=1024):
    n, d = x.shape
    k = y.shape[1]

    def body(x_ref, y_ref, w_ref, t_ref, g_ref, l_ref):
        @pl.when((pl.program_id(0) == 0) & (pl.program_id(1) == 0))
        def _():
            l_ref[...] = jnp.zeros_like(l_ref)

        diff = x_ref[...] + _dot(y_ref[...], w_ref[...]) - t_ref[...]
        g_ref[...] = diff * (1.0 / d)
        f = _fold8(diff * diff)
        acc = f[:, 0:LANES]
        for c in range(1, tn // LANES):
            acc = acc + f[:, c * LANES:(c + 1) * LANES]
        l_ref[...] += acc

    return pl.pallas_call(
        body, name="outproj_loss", grid=(n // tm, d // tn),
        in_specs=[BS((tm, tn), lambda i, j: (i, j)), BS((tm, k), lambda i, j: (i, 0)), BS((k, tn), lambda i, j: (0, j)),
                  BS((tm, tn), lambda i, j: (i, j))],
        out_specs=[BS((tm, tn), lambda i, j: (i, j)), BS((SUB, LANES), lambda i, j: (0, 0))],
        out_shape=[S((n, d), F32), S((SUB, LANES), F32)],
        compiler_params=_cp("arbitrary", "arbitrary"),
    )(x, y, w, tgt)


def _matmul(a, b, mode, out_dtype, tm, tn, tk, name, b_cols=None):
    if mode == "nn":
        (m, kk), nn = a.shape, b.shape[1]
        a_spec, b_spec = BS((tm, tk), lambda i, j, k: (i, k)), BS((tk, tn), lambda i, j, k: (k, j))
        dot = _dot
    elif mode == "nt":
        (m, kk), nn = a.shape, b.shape[0]
        a_spec, b_spec = BS((tm, tk), lambda i, j, k: (i, k)), BS((tn, tk), lambda i, j, k: (j, k))
        dot = _dot_nt
    else:
        j0, nj = b_cols or (0, b.shape[1] // tn)
        (kk, m), nn = a.shape, nj * tn
        a_spec, b_spec = BS((tk, tm), lambda i, j, k: (k, i)), BS((tk, tn), lambda i, j, k: (k, j0 + j))
        dot = _dot_tn
    nk = kk // tk

    def body(a_ref, b_ref, o_ref, acc_ref):
        kid = pl.program_id(2)

        @pl.when(kid == 0)
        def _():
            acc_ref[...] = jnp.zeros_like(acc_ref)

        acc_ref[...] += dot(a_ref[...].astype(BF16), b_ref[...].astype(BF16))

        @pl.when(kid == nk - 1)
        def _():
            o_ref[...] = acc_ref[...].astype(out_dtype)

    return pl.pallas_call(
        body, name=name, grid=(m // tm, nn // tn, nk),
        in_specs=[a_spec, b_spec], out_specs=BS((tm, tn), lambda i, j, k: (i, j)),
        out_shape=S((m, nn), out_dtype), scratch_shapes=[pltpu.VMEM((tm, tn), F32)],
        compiler_params=_cp("arbitrary", "arbitrary", "arbitrary"),
    )(a, b)


SLAB = 16


def _inproj_bwd_dx(dp, w, x, nw, dres, tm=512, tk=1152):
    n, d = x.shape
    nk = dp.shape[1] // tk

    def body(dp_ref, w_ref, x_ref, nw_ref, dr_ref, dx_ref, dnw_ref, acc_ref):
        kid = pl.program_id(1)

        @pl.when((pl.program_id(0) == 0) & (kid == 0))
        def _():
            dnw_ref[...] = jnp.zeros_like(dnw_ref)

        @pl.when(kid == 0)
        def _():
            acc_ref[...] = jnp.zeros_like(acc_ref)

        acc_ref[...] += _dot_nt(dp_ref[...], w_ref[...])

        @pl.when(kid == nk - 1)
        def _():
            def slab(i, carry):
                rows = pl.ds(pl.multiple_of(i * SLAB, SLAB), SLAB)
                dh, xv = acc_ref[rows, :], x_ref[rows, :]
                r = lax.rsqrt(jnp.mean(xv * xv, axis=-1, keepdims=True) + EPS)
                dnw_ref[...] += _fold8(dh * xv * r)
                g = dh * nw_ref[...]
                mm = jnp.mean(g * xv, axis=-1, keepdims=True)
                dx_ref[rows, :] = dr_ref[rows, :] + r * g - xv * (r * r * r * mm)
                return carry

            lax.fori_loop(0, tm // SLAB, slab, 0)

    return pl.pallas_call(
        body, name="inproj_bwd_dx", grid=(n // tm, nk),
        in_specs=[BS((tm, tk), lambda i, k: (i, k)), BS((d, tk), lambda i, k: (0, k)), BS((tm, d), lambda i, k: (i, 0)),
                  BS((1, d), lambda i, k: (0, 0)), BS((tm, d), lambda i, k: (i, 0))],
        out_specs=[BS((tm, d), lambda i, k: (i, 0)), BS((SUB, d), lambda i, k: (0, 0))],
        out_shape=[S((n, d), F32), S((SUB, d), F32)],
        scratch_shapes=[pltpu.VMEM((tm, d), F32)],
        compiler_params=_cp("arbitrary", "arbitrary"),
    )(dp, w, x, nw, dres)


def _rope_tables(pos):
    half = ROT // 2
    inv = THETA ** (-jnp.arange(0, ROT, 2, dtype=F32) / ROT)
    ang = pos.astype(F32)[:, None] * inv
    cos, sin = jnp.cos(ang), jnp.sin(ang)
    n = pos.shape[0]
    one = jnp.ones((n, A_DH - ROT), F32)
    zero = jnp.zeros((n, A_DH - ROT), F32)
    zh = jnp.zeros((n, half), F32)
    c = jnp.concatenate([cos, cos, one], axis=1)
    s1 = jnp.concatenate([-sin, zh, zero], axis=1)
    s2 = jnp.concatenate([zh, sin, zero], axis=1)
    return tuple(jnp.concatenate([t, t], axis=1) for t in (c, s1, s2))


def _half_stat(t):
    lo = _lane(t.shape) < A_DH
    s_lo = jnp.sum(jnp.where(lo, t, 0.0), axis=-1, keepdims=True)
    s_hi = jnp.sum(jnp.where(lo, 0.0, t), axis=-1, keepdims=True)
    return jnp.where(lo, s_lo, s_hi)


def _normrope(x, w, c, s1, s2):
    r = lax.rsqrt(_half_stat(x * x) * (1.0 / A_DH) + EPS)
    xn = x * r * w
    return xn * c + pltpu.roll(xn, LANES - ROT // 2, 1) * s1 + pltpu.roll(xn, ROT // 2, 1) * s2, r


def _normrope_bwd(dy, x, r, w, c, s1, s2):
    dxn = dy * c + pltpu.roll(dy * s1, ROT // 2, 1) + pltpu.roll(dy * s2, LANES - ROT // 2, 1)
    g = dxn * w
    mm = _half_stat(g * x) * (1.0 / A_DH)
    return r * g - x * (r * r * r * mm), dxn * x * r


def _attn_mask(first):
    qi = _subl((ABLK, 2 * ABLK))
    kj = _lane((ABLK, 2 * ABLK))
    dist = qi + ABLK - kj
    return (dist >= 0) & (dist < ABLK) & (jnp.logical_not(first) | (kj >= ABLK))


def _keep_half(x, b):
    lo = _lane(x.shape) < A_DH
    return jnp.where(lo if b == 0 else jnp.logical_not(lo), x, jnp.zeros_like(x))


def _head_operand(x, j):
    a, b = j % 2, (j // 3) % 2
    return _keep_half(x if a == b else pltpu.roll(x, A_DH, 1), b)


def _head_result(x, j):
    a, b = j % 2, (j // 3) % 2
    return _keep_half(x if a == b else pltpu.roll(x, A_DH, 1), a)


def _attn_fwd(p, tabs, qw, kw, sinks, nseq):
    n = p.shape[0]
    nb = n // nseq // ABLK
    cur = lambda b, i: (b * nb + i, 0)
    prv = lambda b, i: (b * nb + jnp.maximum(i - 1, 0), 0)
    colblk = lambda f, w, off: (lambda b, i: (f(b, i)[0], off // w))

    def body(q_ref, za_ref, kc_ref, vc_ref, kp_ref, vp_ref, c_ref, s1_ref, s2_ref, cp_ref, s1p_ref, s2p_ref,
             qw_ref, kw_ref, sink_ref, y_ref, o_ref, lse_ref):
        first = pl.program_id(1) == 0
        tc = (c_ref[...], s1_ref[...], s2_ref[...])
        tp = (cp_ref[...], s1p_ref[...], s2p_ref[...])
        q, kc, kp = q_ref[...], kc_ref[...], kp_ref[...]
        qn = [_normrope(q[:, LANES * b:LANES * (b + 1)], qw_ref[...], *tc)[0] for b in range(A_W // LANES)]
        k2, v2 = [], []
        for b in range(A_KVW // LANES):
            sl = slice(LANES * b, LANES * (b + 1))
            k2.append(jnp.concatenate([_normrope(kp[:, sl], kw_ref[...], *tp)[0],
                                       _normrope(kc[:, sl], kw_ref[...], *tc)[0]], axis=0).astype(BF16))
            v2.append(jnp.concatenate([vp_ref[:, sl], vc_ref[:, sl]], axis=0).astype(BF16))
        valid = _attn_mask(first)
        heads = range(A_HEADS)
        qm = [_head_operand(qn[j // 2], j).astype(BF16) for j in heads]
        s = [jnp.where(valid, _dot_nt(qm[j], k2[j // 6]) * (A_DH ** -0.5), -jnp.inf) for j in heads]
        m = [jnp.maximum(jnp.max(s[j], axis=-1, keepdims=True), sink_ref[j]) for j in heads]
        e = [jnp.exp(s[j] - m[j]) for j in heads]
        den = [jnp.sum(e[j], axis=-1, keepdims=True) + jnp.exp(sink_ref[j] - m[j]) for j in heads]
        outs = [_head_result(_dot((e[j] * (1.0 / den[j])).astype(BF16), v2[j // 6]), j) for j in heads]
        lse = jnp.zeros((ABLK, LANES), F32)
        for j in heads:
            lse = jnp.where(_lane(lse.shape) == j, m[j] + jnp.log(den[j]), lse)
        o = jnp.concatenate([outs[2 * b] + outs[2 * b + 1] for b in range(A_W // LANES)], axis=1)
        za = za_ref[...]
        o_ref[...] = o
        lse_ref[...] = lse
        y_ref[...] = (o * (za * _sigmoid(za))).astype(BF16)

    tab_specs = [BS((ABLK, LANES), cur)] * 3 + [BS((ABLK, LANES), prv)] * 3
    return pl.pallas_call(
        body, name="attn_fwd", grid=(nseq, nb),
        in_specs=[BS((ABLK, A_W), colblk(cur, A_W, P_Q)), BS((ABLK, A_W), colblk(cur, A_W, P_ZA)),
                  BS((ABLK, A_KVW), colblk(cur, A_KVW, P_K)), BS((ABLK, A_KVW), colblk(cur, A_KVW, P_V)),
                  BS((ABLK, A_KVW), colblk(prv, A_KVW, P_K)), BS((ABLK, A_KVW), colblk(prv, A_KVW, P_V))]
        + tab_specs + [BS((1, LANES), lambda b, i: (0, 0))] * 2 + [BS(memory_space=pltpu.SMEM)],
        out_specs=[BS((ABLK, A_W), colblk(cur, A_W, Y_A)), BS((ABLK, A_W), cur), BS((ABLK, LANES), cur)],
        out_shape=[S((n, D_MODEL), BF16), S((n, A_W), F32), S((n, LANES), F32)],
        compiler_params=_cp("arbitrary", "arbitrary"),
    )(p, p, p, p, p, p, *tabs, *tabs, qw, kw, sinks)


def _attn_bwd(p, dy, o, lse, tabs, qw, kw, sinks, nseq):
    n = p.shape[0]
    nb = n // nseq // ABLK
    cur = lambda b, i: (b * nb + jnp.minimum(i, nb - 1), 0)
    prv = lambda b, i: (b * nb + jnp.maximum(i - 1, 0), 0)
    colblk = lambda f, w, off: (lambda b, i: (f(b, i)[0], off // w))

    def body(q_ref, za_ref, kc_ref, vc_ref, kp_ref, vp_ref, dy_ref, o_ref, lse_ref,
             c_ref, s1_ref, s2_ref, cp_ref, s1p_ref, s2p_ref, qw_ref, kw_ref, sink_ref,
             dqza_ref, dkv_ref, dqw_ref, dkw_ref, dsk_ref, tk_ref, tv_ref, ck_ref, cv_ref):
        i = pl.program_id(1)
        first = i == 0
        tc = (c_ref[...], s1_ref[...], s2_ref[...])
        tp = (cp_ref[...], s1p_ref[...], s2p_ref[...])
        nkb = A_KVW // LANES

        @pl.when((pl.program_id(0) == 0) & first)
        def _():
            dqw_ref[...] = jnp.zeros_like(dqw_ref)
            dkw_ref[...] = jnp.zeros_like(dkw_ref)
            dsk_ref[...] = jnp.zeros_like(dsk_ref)

        @pl.when(i < nb)
        def _():
            q, kc, kp = q_ref[...], kc_ref[...], kp_ref[...]
            qn, rq = [], []
            for b in range(A_W // LANES):
                a, r = _normrope(q[:, LANES * b:LANES * (b + 1)], qw_ref[...], *tc)
                qn.append(a)
                rq.append(r)
            k2, v2 = [], []
            for b in range(nkb):
                sl = slice(LANES * b, LANES * (b + 1))
                k2.append(jnp.concatenate([_normrope(kp[:, sl], kw_ref[...], *tp)[0],
                                           _normrope(kc[:, sl], kw_ref[...], *tc)[0]], axis=0).astype(BF16))
                v2.append(jnp.concatenate([vp_ref[:, sl], vc_ref[:, sl]], axis=0).astype(BF16))
            valid = _attn_mask(first)
            za, dy, o, lse = za_ref[...], dy_ref[...], o_ref[...], lse_ref[...]
            sg = _sigmoid(za)
            do = dy * za * sg
            dqza_ref[:, A_W:2 * A_W] = (dy * o * _dsilu(za, sg)).astype(BF16)
            heads = range(A_HEADS)
            blk = lambda x, b: x[:, LANES * b:LANES * (b + 1)]
            qm = [_head_operand(qn[j // 2], j).astype(BF16) for j in heads]
            lj = [_col(lse, j) for j in heads]
            pr = [jnp.exp(jnp.where(valid, _dot_nt(qm[j], k2[j // 6]) * (A_DH ** -0.5), -jnp.inf) - lj[j]) for j in heads]
            dom = [_head_operand(blk(do, j // 2), j).astype(BF16) for j in heads]
            delta = [jnp.sum(_keep_half(blk(do, j // 2) * blk(o, j // 2), j % 2), axis=-1, keepdims=True) for j in heads]
            ds = [(pr[j] * (_dot_nt(dom[j], v2[j // 6]) - delta[j]) * (A_DH ** -0.5)).astype(BF16) for j in heads]
            dqh = [_head_result(_dot(ds[j], k2[j // 6]), j) for j in heads]
            dkh = [_dot_tn(ds[j], qm[j]) for j in heads]
            dvh = [_dot_tn(pr[j].astype(BF16), dom[j]) for j in heads]
            per_blk = A_HEADS // nkb
            dks = [sum(dkh[per_blk * g + 1:per_blk * (g + 1)], dkh[per_blk * g]) for g in range(nkb)]
            dvs = [sum(dvh[per_blk * g + 1:per_blk * (g + 1)], dvh[per_blk * g]) for g in range(nkb)]
            dsk = jnp.zeros((ABLK, LANES), F32)
            for j in heads:
                dsk = dsk + jnp.where(_lane(dsk.shape) == j, -jnp.exp(sink_ref[j] - lj[j]) * delta[j], 0.0)
            dsk_ref[...] += _fold8(dsk)
            dqn = jnp.concatenate([dqh[2 * b] + dqh[2 * b + 1] for b in range(A_W // LANES)], axis=1)
            dqw = jnp.zeros((SUB, LANES), F32)
            dqo = []
            for b in range(A_W // LANES):
                sl = slice(LANES * b, LANES * (b + 1))
                dx, wt = _normrope_bwd(dqn[:, sl], q[:, sl], rq[b], qw_ref[...], *tc)
                dqo.append(dx)
                dqw = dqw + _fold8(wt)
            dqw_ref[...] += dqw
            dqza_ref[:, 0:A_W] = jnp.concatenate(dqo, axis=1).astype(BF16)
            tk_ref[...] = jnp.concatenate(dks, axis=1)
            tv_ref[...] = jnp.concatenate(dvs, axis=1)

        @pl.when(i == nb)
        def _():
            tk_ref[...] = jnp.zeros_like(tk_ref)
            tv_ref[...] = jnp.zeros_like(tv_ref)

        @pl.when(i > 0)
        def _():
            kp = kp_ref[...]
            dkn = ck_ref[...] + tk_ref[0:ABLK, :]
            dkw = jnp.zeros((SUB, LANES), F32)
            dko = []
            for b in range(nkb):
                sl = slice(LANES * b, LANES * (b + 1))
                r = _normrope(kp[:, sl], kw_ref[...], *tp)[1]
                dx, wt = _normrope_bwd(dkn[:, sl], kp[:, sl], r, kw_ref[...], *tp)
                dko.append(dx)
                dkw = dkw + _fold8(wt)
            dkw_ref[...] += dkw
            dkv_ref[:, 0:A_KVW] = jnp.concatenate(dko, axis=1).astype(BF16)
            dkv_ref[:, A_KVW:2 * A_KVW] = (cv_ref[...] + tv_ref[0:ABLK, :]).astype(BF16)

        ck_ref[...] = tk_ref[ABLK:2 * ABLK, :]
        cv_ref[...] = tv_ref[ABLK:2 * ABLK, :]

    tab_specs = [BS((ABLK, LANES), cur)] * 3 + [BS((ABLK, LANES), prv)] * 3
    acc = BS((SUB, LANES), lambda b, i: (0, 0))
    return pl.pallas_call(
        body, name="attn_bwd", grid=(nseq, nb + 1),
        in_specs=[BS((ABLK, A_W), colblk(cur, A_W, P_Q)), BS((ABLK, A_W), colblk(cur, A_W, P_ZA)),
                  BS((ABLK, A_KVW), colblk(cur, A_KVW, P_K)), BS((ABLK, A_KVW), colblk(cur, A_KVW, P_V)),
                  BS((ABLK, A_KVW), colblk(prv, A_KVW, P_K)), BS((ABLK, A_KVW), colblk(prv, A_KVW, P_V)),
                  BS((ABLK, A_W), colblk(cur, A_W, Y_A)), BS((ABLK, A_W), cur), BS((ABLK, LANES), cur)]
        + tab_specs + [BS((1, LANES), lambda b, i: (0, 0))] * 2 + [BS(memory_space=pltpu.SMEM)],
        out_specs=[BS((ABLK, 2 * A_W), cur), BS((ABLK, 2 * A_KVW), prv), acc, acc, acc],
        out_shape=[S((n, P_W), BF16), S((n, 2 * A_KVW), BF16)] + [S((SUB, LANES), F32)] * 3,
        scratch_shapes=[pltpu.VMEM((2 * ABLK, A_KVW), F32)] * 2 + [pltpu.VMEM((ABLK, A_KVW), F32)] * 2,
        compiler_params=_cp("arbitrary", "arbitrary"),
    )(p, p, p, p, p, p, dy, o, lse, *tabs, *tabs, qw, kw, sinks)


def _put_cols(dst, src, col_off, after, tm=512):
    n, w = src.shape

    def body(s_ref, d_in_ref, after_ref, d_ref):
        d_ref[...] = s_ref[...]

    return pl.pallas_call(
        body, name="put_cols", grid=(n // tm,),
        in_specs=[BS((tm, w), lambda i: (i, 0)), BS(memory_space=pl.ANY), BS(memory_space=pl.ANY)],
        out_specs=BS((tm, w), lambda i: (i, col_off // w)),
        out_shape=S(dst.shape, dst.dtype), input_output_aliases={1: 0},
        compiler_params=_cp("arbitrary"),
    )(src, dst, after)


HALO_B = 32


def _layernorm(hc, lnw, lnb):
    mu = jnp.mean(hc, axis=-1, keepdims=True)
    xc = hc - mu
    rstd = lax.rsqrt(jnp.mean(xc * xc, axis=-1, keepdims=True) + EPS)
    xhat = xc * rstd
    return xhat, rstd, xhat * lnw + lnb


def _shifted_copies(buf_ref, sh_ref):
    rows = sh_ref.shape[1]
    for b in range(1, SUB):
        sh_ref[b - 1] = buf_ref[pl.ds(b, rows), :]


def _rows_from(buf_ref, sh_ref, off, rows):
    a, b = divmod(off, SUB)
    if b == 0:
        return buf_ref[pl.ds(SUB * a, rows), :]
    return sh_ref[b - 1, pl.ds(SUB * a, rows), :]


def _conf_fwd(p, y, cw, cb, lnw, lnb, pw, pwb, nseq, tm=256):
    n = p.shape[0]
    t = n // nseq
    nt = t // tm
    row = lambda b, i: b * nt + i
    halo = lambda b, i: jnp.maximum((b * t + i * tm) // HALO_B - 1, 0)
    vec = BS((1, B_W), lambda b, i: (0, 0))

    def body(ub_ref, uh_ref, zb_ref, cw_ref, cb_ref, lnw_ref, lnb_ref, pw_ref, pwb_ref, y_in_ref, y_ref, hc_ref, buf_ref, sh_ref):
        ub, uh = ub_ref[...], uh_ref[...]
        hh = uh[:, :B_W] * _sigmoid(uh[:, B_W:])
        buf_ref[0:HALO_B, :] = jnp.where(pl.program_id(1) > 0, hh, 0.0)
        buf_ref[HALO_B:, :] = ub[:, :B_W] * _sigmoid(ub[:, B_W:])
        _shifted_copies(buf_ref, sh_ref)
        hc = jnp.zeros((tm, B_W), F32) + cb_ref[...]
        for k in range(B_K):
            hc = hc + cw_ref[k:k + 1, :] * _rows_from(buf_ref, sh_ref, HALO_B - B_K + 1 + k, tm)
        hc_ref[...] = hc
        ln = _layernorm(hc, lnw_ref[...], lnb_ref[...])[2]
        sw = ln * _sigmoid(ln)
        ob = _dot(sw.astype(BF16), pw_ref[...]) + pwb_ref[...]
        zb = zb_ref[...]
        y_ref[...] = (ob * (zb * _sigmoid(zb))).astype(BF16)

    return pl.pallas_call(
        body, name="conf_fwd", grid=(nseq, nt),
        in_specs=[BS((tm, 2 * B_W), lambda b, i: (row(b, i), P_UB // (2 * B_W))),
                  BS((HALO_B, 2 * B_W), lambda b, i: (halo(b, i), P_UB // (2 * B_W))),
                  BS((tm, B_W), lambda b, i: (row(b, i), P_ZB // B_W)),
                  BS((HALO_B, B_W), lambda b, i: (0, 0)), vec, vec, vec, BS((B_W, B_W), lambda b, i: (0, 0)), vec,
                  BS(memory_space=pl.ANY)],
        out_specs=[BS((tm, B_W), lambda b, i: (row(b, i), Y_B // B_W)), BS((tm, B_W), lambda b, i: (row(b, i), 0))],
        out_shape=[S(y.shape, y.dtype), S((n, B_W), F32)], input_output_aliases={9: 0},
        scratch_shapes=[pltpu.VMEM((HALO_B + tm, B_W), F32), pltpu.VMEM((SUB - 1, HALO_B + tm - SUB, B_W), F32)],
        compiler_params=_cp("arbitrary", "arbitrary"),
    )(p, p, p, cw, cb, lnw, lnb, pw, pwb, y)


def _conf_bwd1(p, dy, dp, hc, lnw, lnb, pw, pwb, tm=256):
    n = p.shape[0]
    vec = BS((1, B_W), lambda i: (0, 0))
    acc = BS((SUB, B_W), lambda i: (0, 0))

    def body(dy_ref, zb_ref, hc_ref, lnw_ref, lnb_ref, pw_ref, pwb_ref, dp_in_ref,
             dzb_ref, dhc_ref, dpw_ref, dpwb_ref, dlnw_ref, dlnb_ref, dcb_ref):
        @pl.when(pl.program_id(0) == 0)
        def _():
            for r in (dpw_ref, dpwb_ref, dlnw_ref, dlnb_ref, dcb_ref):
                r[...] = jnp.zeros_like(r)

        xhat, rstd, ln = _layernorm(hc_ref[...], lnw_ref[...], lnb_ref[...])
        sgl = _sigmoid(ln)
        sw = (ln * sgl).astype(BF16)
        ob = _dot(sw, pw_ref[...]) + pwb_ref[...]
        dy, zb = dy_ref[...], zb_ref[...]
        sgz = _sigmoid(zb)
        dzb_ref[...] = (dy * ob * _dsilu(zb, sgz)).astype(BF16)
        dob = dy * zb * sgz
        dobb = dob.astype(BF16)
        dpwb_ref[...] += _fold8(dob)
        dpw_ref[...] += _dot_tn(sw, dobb)
        dln = _dot_nt(dobb, pw_ref[...]) * _dsilu(ln, sgl)
        dlnw_ref[...] += _fold8(dln * xhat)
        dlnb_ref[...] += _fold8(dln)
        dxh = dln * lnw_ref[...]
        dhc = rstd * (dxh - jnp.mean(dxh, axis=-1, keepdims=True) - xhat * jnp.mean(dxh * xhat, axis=-1, keepdims=True))
        dcb_ref[...] += _fold8(dhc)
        dhc_ref[...] = dhc

    return pl.pallas_call(
        body, name="conf_bwd1", grid=(n // tm,),
        in_specs=[BS((tm, B_W), lambda i: (i, Y_B // B_W)), BS((tm, B_W), lambda i: (i, P_ZB // B_W)),
                  BS((tm, B_W), lambda i: (i, 0)), vec, vec, BS((B_W, B_W), lambda i: (0, 0)), vec,
                  BS(memory_space=pl.ANY)],
        out_specs=[BS((tm, B_W), lambda i: (i, P_ZB // B_W)), BS((tm, B_W), lambda i: (i, 0)),
                   BS((B_W, B_W), lambda i: (0, 0)), acc, acc, acc, acc],
        out_shape=[S(dp.shape, dp.dtype), S((n, B_W), F32), S((B_W, B_W), F32)] + [S((SUB, B_W), F32)] * 4,
        input_output_aliases={7: 0},
        compiler_params=_cp("arbitrary"),
    )(dy, p, hc, lnw, lnb, pw, pwb, dp)


def _conf_bwd2(p, dhc, dp, cw, nseq, tm=256):
    n = p.shape[0]
    t = n // nseq
    nt = t // tm
    row = lambda b, i: b * nt + i
    prev = lambda b, i: jnp.maximum((b * t + i * tm) // HALO_B - 1, 0)
    nxt = lambda b, i: jnp.minimum((b * t + (i + 1) * tm) // HALO_B, n // HALO_B - 1)

    def body(ub_ref, uh_ref, dh_ref, dn_ref, cw_ref, dp_in_ref, dub_ref, dcw_ref, buf_ref, dbuf_ref, sh_ref, dsh_ref):
        i = pl.program_id(1)

        @pl.when((pl.program_id(0) == 0) & (i == 0))
        def _():
            dcw_ref[...] = jnp.zeros_like(dcw_ref)

        ub, uh = ub_ref[...], uh_ref[...]
        a, sg = ub[:, :B_W], _sigmoid(ub[:, B_W:])
        buf_ref[0:HALO_B, :] = jnp.where(i > 0, uh[:, :B_W] * _sigmoid(uh[:, B_W:]), 0.0)
        buf_ref[HALO_B:, :] = a * sg
        dhc = dh_ref[...]
        dbuf_ref[0:tm, :] = dhc
        dbuf_ref[tm:, :] = jnp.where(i < nt - 1, dn_ref[...], 0.0)
        _shifted_copies(buf_ref, sh_ref)
        _shifted_copies(dbuf_ref, dsh_ref)
        dhg = jnp.zeros((tm, B_W), F32)
        for k in range(B_K):
            dhg = dhg + cw_ref[k:k + 1, :] * _rows_from(dbuf_ref, dsh_ref, B_K - 1 - k, tm)
            dcw_ref[SUB * k:SUB * (k + 1), :] += _fold8(dhc * _rows_from(buf_ref, sh_ref, HALO_B - B_K + 1 + k, tm))
        dub_ref[...] = jnp.concatenate([dhg * sg, dhg * a * sg * (1.0 - sg)], axis=1).astype(BF16)

    return pl.pallas_call(
        body, name="conf_bwd2", grid=(nseq, nt),
        in_specs=[BS((tm, 2 * B_W), lambda b, i: (row(b, i), P_UB // (2 * B_W))),
                  BS((HALO_B, 2 * B_W), lambda b, i: (prev(b, i), P_UB // (2 * B_W))),
                  BS((tm, B_W), lambda b, i: (row(b, i), 0)), BS((HALO_B, B_W), lambda b, i: (nxt(b, i), 0)),
                  BS((HALO_B, B_W), lambda b, i: (0, 0)), BS(memory_space=pl.ANY)],
        out_specs=[BS((tm, 2 * B_W), lambda b, i: (row(b, i), P_UB // (2 * B_W))),
                   BS((SUB * B_K, B_W), lambda b, i: (0, 0))],
        out_shape=[S(dp.shape, dp.dtype), S((SUB * B_K, B_W), F32)], input_output_aliases={5: 0},
        scratch_shapes=[pltpu.VMEM((HALO_B + tm, B_W), F32)] * 2 + [pltpu.VMEM((SUB - 1, HALO_B + tm - SUB, B_W), F32)] * 2,
        compiler_params=_cp("arbitrary", "arbitrary"),
    )(p, p, dhc, dhc, cw, dp)


HALO_C = 8
QS = C_DH ** -0.5
NCB = 3 * C_HEADS
CB0 = P_QKV // LANES
ZC0 = P_ZC // LANES
GB, GG = 0, C_HEADS


def _softplus(z):
    return jnp.maximum(z, 0.0) + jnp.log(1.0 + jnp.exp(-jnp.abs(z)))


def _gdn_gates_fwd(p, alog_l, dtb_l, tm=256):
    n = p.shape[0]

    def body(ba_ref, al_ref, db_ref, o_ref):
        blk = ba_ref[...]
        lane = _lane(blk.shape)
        g = jnp.where((lane >= GG) & (lane < GG + C_HEADS), -jnp.exp(al_ref[...]) * _softplus(blk + db_ref[...]), 0.0)
        tri = (_subl((CHUNK, CHUNK)) >= _lane((CHUNK, CHUNK))).astype(F32)
        gc = jnp.concatenate([_dot(tri, g[CHUNK * c:CHUNK * (c + 1)], HI) for c in range(tm // CHUNK)], axis=0)
        o_ref[...] = jnp.where(lane < GG, _sigmoid(blk), gc)

    return pl.pallas_call(
        body, name="gdn_gates_fwd", grid=(n // tm,),
        in_specs=[BS((tm, LANES), lambda i: (i, P_BA // LANES)), BS((1, LANES), lambda i: (0, 0)), BS((1, LANES), lambda i: (0, 0))],
        out_specs=BS((tm, LANES), lambda i: (i, 0)), out_shape=S((n, LANES), F32),
        compiler_params=_cp("arbitrary"),
    )(p, alog_l, dtb_l)


def _gdn_pre_fwd(p, ccw, nseq, tm=256):
    n = p.shape[0]
    t = n // nseq
    nt = t // tm
    row = lambda b, i: b * nt + i
    halo = lambda b, i: jnp.maximum((b * t + i * tm) // HALO_C - 1, 0)

    def body(x_ref, xh_ref, w_ref, xc_ref, o_ref, buf_ref):
        buf_ref[0:HALO_C, :] = jnp.where(pl.program_id(1) > 0, xh_ref[...], 0.0)
        buf_ref[HALO_C:, :] = x_ref[...]
        for c in range(NCB):
            cs = slice(LANES * c, LANES * (c + 1))
            xc = jnp.zeros((tm, LANES), F32)
            for k in range(C_K):
                xc = xc + w_ref[k:k + 1, cs] * buf_ref[pl.ds(HALO_C - C_K + 1 + k, tm), cs]
            xc_ref[:, cs] = xc
            act = xc * _sigmoid(xc)
            if c < 2 * C_HEADS:
                act = act * (lax.rsqrt(jnp.sum(act * act, axis=-1, keepdims=True) + EPS) * (QS if c < C_HEADS else 1.0))
            o_ref[:, cs] = act

    wide = 3 * C_W
    return pl.pallas_call(
        body, name="gdn_pre_fwd", grid=(nseq, nt),
        in_specs=[BS((tm, wide), lambda b, i: (row(b, i), P_QKV // wide)), BS((HALO_C, wide), lambda b, i: (halo(b, i), P_QKV // wide)),
                  BS((SUB, wide), lambda b, i: (0, 0))],
        out_specs=[BS((tm, wide), lambda b, i: (row(b, i), 0))] * 2,
        out_shape=[S((n, wide), F32)] * 2,
        scratch_shapes=[pltpu.VMEM((HALO_C + tm, wide), F32)],
        compiler_params=_cp("arbitrary", "arbitrary"),
    )(p, p, ccw)


def _chunk_common(q, k, gt, gtt, h):
    beta = _col(gt, GB + h)
    gc = _col(gt, GG + h)
    gcr = gtt[GG + h:GG + h + 1, :]
    ii, jj = _subl((CHUNK, CHUNK)), _lane((CHUNK, CHUNK))
    incl, strict = ii >= jj, ii > jj
    dec = jnp.exp(jnp.where(incl, gc - gcr, -jnp.inf))
    kb = k * beta
    kbf = k.astype(BF16)
    a = jnp.where(strict, _dot_nt(kb.astype(BF16), kbf) * dec, 0.0)
    mq = jnp.where(incl, _dot_nt(q.astype(BF16), kbf) * dec, 0.0)
    glast = jnp.sum(jnp.where(_subl(gc.shape) == CHUNK - 1, gc, 0.0), axis=0, keepdims=True)
    return beta, gc, incl, strict, dec, kb, a, mq, glast


def _split(x):
    hi = x.astype(BF16)
    return hi, (x - hi.astype(F32)).astype(BF16)


def _dot3(dot, a, b):
    (ah, al), (bh, bl) = a, b
    return dot(ah, bh) + (dot(ah, bl) + dot(al, bh))


def _unit_lower_inverses(mats):
    eye = (_subl(mats[0].shape) == _lane(mats[0].shape)).astype(F32)
    ms = [-a for a in mats]
    invs = [eye + m for m in ms]
    parts = [_split(m) for m in ms]
    for _ in range(5):
        ms = [_dot3(_dot, s, s) for s in parts]
        parts = [_split(m) for m in ms]
        invs = [inv + _dot3(_dot, _split(inv), s) for inv, s in zip(invs, parts)]
    return invs


def _gdn_chunk_fwd(qkv, gates, p, y, onw, nseq, tt=512):
    n = qkv.shape[0]
    t = n // nseq
    tt = min(tt, t)
    nt = t // tt
    nch = tt // CHUNK

    def body(q_ref, k_ref, v_ref, g_ref, zc_ref, onw_ref, y_in_ref, y_ref, o_ref, u_ref, w_ref, t_ref, ss_ref, s_scr):
        @pl.when(pl.program_id(1) == 0)
        def _():
            s_scr[...] = jnp.zeros_like(s_scr)

        def step(c, carry):
            rows = pl.ds(pl.multiple_of(c * CHUNK, CHUNK), CHUNK)
            gt = g_ref[rows, :]
            gtt = gt.T
            heads = range(C_HEADS)
            hs = [slice(C_DH * h, C_DH * (h + 1)) for h in heads]
            q, k, v = ([r[rows, hs[h]] for h in heads] for r in (q_ref, k_ref, v_ref))
            cm = [_chunk_common(q[h], k[h], gt, gtt, h) for h in heads]
            beta, gc, kb, mq, glast = ([m[i] for m in cm] for i in (0, 1, 5, 7, 8))
            tinv = _unit_lower_inverses([m[6] for m in cm])
            egc = [jnp.exp(g) for g in gc]
            sol = [_dot3(_dot, _split(tinv[h]), _split(jnp.concatenate([v[h] * beta[h], kb[h] * egc[h]], axis=1))) for h in heads]
            sv = [s_scr[h] for h in heads]
            sb = [s.astype(BF16) for s in sv]
            vnb = [(sol[h][:, :C_DH] - _dot(sol[h][:, C_DH:].astype(BF16), sb[h])).astype(BF16) for h in heads]
            o = [_dot((q[h] * egc[h]).astype(BF16), sb[h]) + _dot(mq[h].astype(BF16), vnb[h]) for h in heads]
            for h in heads:
                ss_ref[h, c] = sv[h]
                s_scr[h] = sv[h] * jnp.exp(glast[h]) + _dot_tn((k[h] * jnp.exp(glast[h] - gc[h])).astype(BF16), vnb[h])
            for h in heads:
                o_ref[rows, hs[h]] = o[h]
                u_ref[rows, hs[h]] = sol[h][:, :C_DH]
                w_ref[rows, hs[h]] = sol[h][:, C_DH:]
                t_ref[rows, hs[h]] = jnp.concatenate([tinv[h], jnp.zeros_like(tinv[h])], axis=1)
                zc = zc_ref[rows, hs[h]]
                r = lax.rsqrt(jnp.mean(o[h] * o[h], axis=-1, keepdims=True) + EPS)
                y_ref[rows, hs[h]] = (o[h] * r * onw_ref[...] * (zc * _sigmoid(zc))).astype(BF16)
            return carry

        lax.fori_loop(0, nch, step, 0)

    row = lambda b, i: b * nt + i
    wb = lambda col: BS((tt, C_W), lambda b, i: (row(b, i), col))
    return pl.pallas_call(
        body, name="gdn_chunk_fwd", grid=(nseq, nt),
        in_specs=[wb(0), wb(1), wb(2), BS((tt, LANES), lambda b, i: (row(b, i), 0)), wb(P_ZC // C_W),
                  BS((1, LANES), lambda b, i: (0, 0)), BS(memory_space=pl.ANY)],
        out_specs=[wb(Y_C // C_W), wb(0), wb(0), wb(0), wb(0),
                   BS((None, C_HEADS, nch, C_DH, C_DH), lambda b, i: (b, 0, i, 0, 0))],
        out_shape=[S(y.shape, y.dtype)] + [S((n, C_W), F32)] * 4 + [S((nseq, C_HEADS, t // CHUNK, C_DH, C_DH), F32)],
        input_output_aliases={6: 0},
        scratch_shapes=[pltpu.VMEM((C_HEADS, C_DH, C_DH), F32)],
        compiler_params=_cp("arbitrary", "arbitrary"),
    )(qkv, qkv, qkv, gates, p, onw, y)


def _gdn_chunk_bwd(qkv, gates, p, dy, dp, onw, o, u, w, tinv, ss, nseq, tt=256):
    n = qkv.shape[0]
    t = n // nseq
    tt = min(tt, t)
    nt = t // tt
    nch = tt // CHUNK

    def body(q_ref, k_ref, v_ref, g_ref, zc_ref, onw_ref, o_ref, dy_ref, u_ref, w_ref, t_ref, ss_ref, dp_in_ref,
             dzc_ref, dqkv_ref, dg_ref, donw_ref, ds_scr):
        @pl.when(pl.program_id(1) == 0)
        def _():
            ds_scr[...] = jnp.zeros_like(ds_scr)

        @pl.when((pl.program_id(0) == 0) & (pl.program_id(1) == 0))
        def _():
            donw_ref[...] = jnp.zeros_like(donw_ref)

        def rsum(x):
            return jnp.sum(x, axis=-1, keepdims=True)

        def step(ci, carry):
            c = nch - 1 - ci
            rows = pl.ds(pl.multiple_of(c * CHUNK, CHUNK), CHUNK)
            gt = g_ref[rows, :]
            gtt = gt.T
            live = [head(c, rows, gt, gtt, h) for h in range(C_HEADS)]
            while live:
                live = [g for g in live if next(g, False)]
            return carry

        def head(c, rows, gt, gtt, h):
            hs = slice(C_DH * h, C_DH * (h + 1))
            q, k, v = q_ref[rows, hs], k_ref[rows, hs], v_ref[rows, hs]
            zc, o, dy, u, w = zc_ref[rows, hs], o_ref[rows, hs], dy_ref[rows, hs], u_ref[rows, hs], w_ref[rows, hs]
            tm_ = t_ref[rows, hs][:, 0:CHUNK]
            sv, dsv = ss_ref[h, c], ds_scr[h]
            sb, dsb = sv.astype(BF16), dsv.astype(BF16)
            sg = _sigmoid(zc)
            r = lax.rsqrt(jnp.mean(o * o, axis=-1, keepdims=True) + EPS)
            on = o * r
            ow = onw_ref[...]
            dzc_ref[rows, hs] = (dy * on * ow * _dsilu(zc, sg)).astype(BF16)
            t1 = dy * zc * sg
            donw_ref[...] += _fold8(t1 * on)
            don = t1 * ow
            do = r * (don - on * jnp.mean(don * on, axis=-1, keepdims=True))
            dob = do.astype(BF16)
            yield True
            beta, gc, incl, strict, dec, kb, a, mq, glast = _chunk_common(q, k, gt, gtt, h)
            egc = jnp.exp(gc)
            gl = jnp.exp(glast)
            ekd = jnp.exp(glast - gc)
            wb = w.astype(BF16)
            vnb = (u - _dot(wb, sb)).astype(BF16)
            qg = q * egc
            yield True
            dvn = _dot_tn(mq.astype(BF16), dob) + _dot((k * ekd).astype(BF16), dsb)
            dvnb = dvn.astype(BF16)
            dqg = _dot_nt(dob, sb)
            yield True
            dmq = jnp.where(incl, _dot_nt(dob, vnb), 0.0)
            dkd = _dot_nt(vnb, dsb)
            dgl = jnp.sum(rsum(dsv * sv), axis=0, keepdims=True)
            dw = -_dot_nt(dvnb, sb)
            yield True
            ds_scr[h] = gl * dsv + _dot_tn(qg.astype(BF16), dob) - _dot_tn(wb, dvnb)
            db = _dot3(_dot_tn, _split(tm_), _split(jnp.concatenate([dvn, dw], axis=1)))
            dbv, dbk = db[:, :C_DH], db[:, C_DH:]
            yield True
            da = -jnp.where(strict, _dot3(_dot_nt, _split(dbv), _split(u)) + _dot3(_dot_nt, _split(dbk), _split(w)), 0.0)
            yield True
            e = da * a + dmq * mq
            dgc = rsum(e) - rsum(e.T)
            dgb, dhb, kbf = (da * dec).astype(BF16), (dmq * dec).astype(BF16), k.astype(BF16)
            dkb = _dot(dgb, kbf)
            tk = rsum(dbk * k)
            rk = rsum(dkd * k) * ekd
            dq = _dot(dhb, kbf) + egc * dqg
            dk = _dot_tn(dgb, kb.astype(BF16)) + _dot_tn(dhb, q.astype(BF16)) + beta * (egc * dbk + dkb) + ekd * dkd
            dbeta = rsum(dbv * v) + tk * egc + rsum(dkb * k)
            dgc = dgc + tk * beta * egc + egc * rsum(dqg * q) - rk
            dglast = jnp.sum(rk, axis=0, keepdims=True) + dgl * gl
            dgc = dgc + jnp.where(_subl(dgc.shape) == CHUNK - 1, dglast, 0.0)
            dqkv_ref[0, rows, hs] = dq
            dqkv_ref[1, rows, hs] = dk
            dqkv_ref[2, rows, hs] = beta * dbv
            lane = _lane((CHUNK, LANES))
            dg_ref[h, rows, :] = jnp.where(lane == 0, dbeta, jnp.where(lane == 1, dgc, 0.0))

        lax.fori_loop(0, nch, step, 0)

    row = lambda b, i: b * nt + nt - 1 - i
    wb = lambda col: BS((tt, C_W), lambda b, i: (row(b, i), col))
    return pl.pallas_call(
        body, name="gdn_chunk_bwd", grid=(nseq, nt),
        in_specs=[wb(0), wb(1), wb(2), BS((tt, LANES), lambda b, i: (row(b, i), 0)), wb(P_ZC // C_W),
                  BS((1, LANES), lambda b, i: (0, 0)), wb(0), wb(Y_C // C_W), wb(0), wb(0), wb(0),
                  BS((None, C_HEADS, nch, C_DH, C_DH), lambda b, i: (b, 0, nt - 1 - i, 0, 0)), BS(memory_space=pl.ANY)],
        out_specs=[wb(P_ZC // C_W), BS((3, tt, C_W), lambda b, i: (0, row(b, i), 0)),
                   BS((C_HEADS, tt, LANES), lambda b, i: (0, row(b, i), 0)), BS((SUB, LANES), lambda b, i: (0, 0))],
        out_shape=[S(dp.shape, dp.dtype), S((3, n, C_W), F32), S((C_HEADS, n, LANES), F32), S((SUB, LANES), F32)],
        input_output_aliases={12: 0},
        scratch_shapes=[pltpu.VMEM((C_HEADS, C_DH, C_DH), F32)],
        compiler_params=_cp("arbitrary", "arbitrary"),
    )(qkv, qkv, qkv, gates, p, onw, o, dy, u, w, tinv, ss, dp)


def _gdn_gates_bwd(dgate, p, alog_l, dtb_l, dp, tm=256):
    n = p.shape[0]
    acc = BS((SUB, LANES), lambda i: (0, 0))

    def body(dg_ref, ba_ref, al_ref, db_ref, dp_in_ref, dba_ref, dal_ref, ddb_ref):
        @pl.when(pl.program_id(0) == 0)
        def _():
            dal_ref[...] = jnp.zeros_like(dal_ref)
            ddb_ref[...] = jnp.zeros_like(ddb_ref)

        blk = ba_ref[...]
        lane = _lane(blk.shape)
        dbeta = jnp.zeros_like(blk)
        dgc = jnp.zeros_like(blk)
        for h in range(C_HEADS):
            dbeta = dbeta + jnp.where(lane == GB + h, _col(dg_ref[h], 0), 0.0)
            dgc = dgc + jnp.where(lane == GG + h, _col(dg_ref[h], 1), 0.0)
        tri = (_subl((CHUNK, CHUNK)) <= _lane((CHUNK, CHUNK))).astype(F32)
        dg = jnp.concatenate([_dot(tri, dgc[CHUNK * c:CHUNK * (c + 1)], HI) for c in range(tm // CHUNK)], axis=0)
        beta = _sigmoid(blk)
        z = blk + db_ref[...]
        ea = jnp.exp(al_ref[...])
        isg = (lane >= GG) & (lane < GG + C_HEADS)
        dz = jnp.where(isg, -dg * ea * _sigmoid(z), 0.0)
        dal_ref[...] += _fold8(jnp.where(isg, -dg * ea * _softplus(z), 0.0))
        ddb_ref[...] += _fold8(dz)
        out = jnp.where(lane < GG, dbeta * beta * (1.0 - beta), dz)
        dba_ref[...] = jnp.concatenate([out, jnp.zeros_like(out)], axis=1).astype(BF16)

    return pl.pallas_call(
        body, name="gdn_gates_bwd", grid=(n // tm,),
        in_specs=[BS((C_HEADS, tm, LANES), lambda i: (0, i, 0)), BS((tm, LANES), lambda i: (i, P_BA // LANES)),
                  BS((1, LANES), lambda i: (0, 0)), BS((1, LANES), lambda i: (0, 0)), BS(memory_space=pl.ANY)],
        out_specs=[BS((tm, 2 * LANES), lambda i: (i, P_BA // (2 * LANES))), acc, acc],
        out_shape=[S(dp.shape, dp.dtype), S((SUB, LANES), F32), S((SUB, LANES), F32)],
        input_output_aliases={4: 0},
        compiler_params=_cp("arbitrary"),
    )(dgate, p, alog_l, dtb_l, dp)


def _gdn_pre_bwd(p, dqkv, xc, dp, ccw, nseq, tm=256):
    n = p.shape[0]
    t = n // nseq
    nt = t // tm
    wide = 3 * C_W
    row = lambda b, i: b * nt + i
    prev = lambda b, i: jnp.maximum((b * t + i * tm) // HALO_C - 1, 0)
    nxt = lambda b, i: jnp.minimum((b * t + (i + 1) * tm) // HALO_C, n // HALO_C - 1)

    def d_conv_out(d, xc, part):
        sg = _sigmoid(xc)
        act = xc * sg
        if part < 2:
            cs = QS if part == 0 else 1.0
            rn = lax.rsqrt(jnp.sum(act * act, axis=-1, keepdims=True) + EPS)
            d = cs * rn * d - act * (cs * rn * rn * rn * jnp.sum(d * act, axis=-1, keepdims=True))
        return d * _dsilu(xc, sg)

    def body(x_ref, xh_ref, d_ref, dn_ref, xc_ref, xn_ref, w_ref, dp_in_ref, dx_ref, dw_ref, buf_ref, dbuf_ref):
        i = pl.program_id(1)

        @pl.when((pl.program_id(0) == 0) & (i == 0))
        def _():
            dw_ref[...] = jnp.zeros_like(dw_ref)

        buf_ref[0:HALO_C, :] = jnp.where(i > 0, xh_ref[...], 0.0)
        buf_ref[HALO_C:, :] = x_ref[...]
        for c in range(NCB):
            cs = slice(LANES * c, LANES * (c + 1))
            part, hd = divmod(c, C_HEADS)
            hs = slice(LANES * hd, LANES * (hd + 1))
            d = d_conv_out(d_ref[part, :, hs], xc_ref[:, cs], part)
            dbuf_ref[0:tm, cs] = d
            dbuf_ref[tm:, cs] = jnp.where(i < nt - 1, d_conv_out(dn_ref[part, :, hs], xn_ref[:, cs], part), 0.0)
            dx = jnp.zeros((tm, LANES), F32)
            for k in range(C_K):
                dx = dx + w_ref[k:k + 1, cs] * dbuf_ref[pl.ds(C_K - 1 - k, tm), cs]
                dw_ref[SUB * k:SUB * (k + 1), cs] += _fold8(d * buf_ref[pl.ds(HALO_C - C_K + 1 + k, tm), cs])
            dx_ref[:, cs] = dx.astype(BF16)

    return pl.pallas_call(
        body, name="gdn_pre_bwd", grid=(nseq, nt),
        in_specs=[BS((tm, wide), lambda b, i: (row(b, i), P_QKV // wide)), BS((HALO_C, wide), lambda b, i: (prev(b, i), P_QKV // wide)),
                  BS((3, tm, C_W), lambda b, i: (0, row(b, i), 0)), BS((3, HALO_C, C_W), lambda b, i: (0, nxt(b, i), 0)),
                  BS((tm, wide), lambda b, i: (row(b, i), 0)), BS((HALO_C, wide), lambda b, i: (nxt(b, i), 0)),
                  BS((SUB, wide), lambda b, i: (0, 0)), BS(memory_space=pl.ANY)],
        out_specs=[BS((tm, wide), lambda b, i: (row(b, i), P_QKV // wide)), BS((SUB * C_K, wide), lambda b, i: (0, 0))],
        out_shape=[S(dp.shape, dp.dtype), S((SUB * C_K, wide), F32)], input_output_aliases={7: 0},
        scratch_shapes=[pltpu.VMEM((HALO_C + tm, wide), F32)] * 2,
        compiler_params=_cp("arbitrary", "arbitrary"),
    )(p, p, dqkv, dqkv, xc, xc, ccw, dp)


ANY = BS(memory_space=pl.ANY)


def _my_pos():
    return lax.axis_index("x"), lax.axis_index("y"), lax.axis_index("c")


def _dev_index(dev):
    return 4 * dev[0] + 2 * dev[1] + dev[2]


def _all_gather(shards, after=None):
    nk = len(shards)

    tail = [] if after is None else [after]

    def body(*refs):
        ins, outs = refs[:nk], refs[nk + len(tail):2 * nk + len(tail)]
        send, recv, loc = refs[2 * nk + len(tail):]
        x, y, c = _my_pos()
        me, sib = (x, y, c), (x, y, 1 - c)
        chips = [(1 - x, y), (x, 1 - y), (1 - x, 1 - y)]

        def rows(t, dev):
            r = ins[t].shape[0]
            return outs[t].at[pl.ds(pl.multiple_of(_dev_index(dev) * r, SUB), r), :]

        def copy(t, k, block, to, src=None):
            return pltpu.make_async_remote_copy(
                src_ref=rows(t, block) if src is None else src, dst_ref=rows(t, block),
                send_sem=send.at[t, k], recv_sem=recv.at[t, k], device_id=to, device_id_type=MESH)

        mine = [pltpu.make_async_copy(ins[t], rows(t, me), loc.at[t]) for t in range(nk)]
        for cp in mine:
            cp.start()
        first = []
        for t in range(nk):
            first.append(copy(t, 0, me, sib, src=ins[t]))
            first += [copy(t, 1 + j, me, (*chip, c), src=ins[t]) for j, chip in enumerate(chips)]
        for cp in first:
            cp.start()
        passed = []
        for j, chip in enumerate(chips):
            for t in range(nk):
                copy(t, 1 + j, (*chip, c), me).wait_recv()
                cp = copy(t, 4 + j, (*chip, c), sib)
                cp.start()
                passed.append(cp)
        for t in range(nk):
            copy(t, 0, sib, me).wait_recv()
            for j, chip in enumerate(chips):
                copy(t, 4 + j, (*chip, 1 - c), me).wait_recv()
        for cp in first + passed:
            cp.wait_send()
        for cp in mine:
            cp.wait()

    return pl.pallas_call(
        body, name="all_gather", in_specs=[ANY] * (nk + len(tail)), out_specs=[ANY] * nk,
        out_shape=[S((N_DEV * a.shape[0], a.shape[1]), a.dtype) for a in shards],
        scratch_shapes=[pltpu.SemaphoreType.DMA((nk, 7)), pltpu.SemaphoreType.DMA((nk, 7)), pltpu.SemaphoreType.DMA((nk,))],
    )(*shards, *tail)


SEM = BS(memory_space=pltpu.SEMAPHORE)
HBM = BS(memory_space=pltpu.HBM)
EFFECT = pltpu.SideEffectType.DATAFLOW_SIDE_EFFECTING


def _peers(x, y, c):
    return [((1 - x) if k & 4 else x, (1 - y) if k & 2 else y, (1 - c) if k & 1 else c) for k in range(1, N_DEV)]


def _exchange_copy(kind, src, land, send, recv, t, k, peer, me, arriving):
    frm = peer if arriving else me
    if kind == "gather":
        r = src.shape[0]
        s_ref = src
        d_ref = land.at[pl.ds(pl.multiple_of(_dev_index(frm) * r, SUB), r), :]
    else:
        r = src.shape[0] // N_DEV
        s_ref = src.at[pl.ds(pl.multiple_of(_dev_index(peer) * r, SUB), r), :]
        d_ref = land.at[_dev_index(frm)]
    sem = t * (N_DEV - 1) + k
    return pltpu.make_async_remote_copy(src_ref=s_ref, dst_ref=d_ref, send_sem=send.at[sem], recv_sem=recv.at[sem],
                                        device_id=peer, device_id_type=MESH)


def _own_copy(kind, src, land, own, t, me):
    if kind == "gather":
        r = src.shape[0]
        return pltpu.make_async_copy(src, land.at[pl.ds(pl.multiple_of(_dev_index(me) * r, SUB), r), :], own.at[t])
    r = src.shape[0] // N_DEV
    return pltpu.make_async_copy(src.at[pl.ds(pl.multiple_of(_dev_index(me) * r, SUB), r), :], land.at[_dev_index(me)], own.at[t])


def _exchange_start(kind, srcs, after, name):
    nk = len(srcs)
    if kind == "gather":
        lands = [lax.empty((N_DEV * a.shape[0], a.shape[1]), a.dtype) for a in srcs]
    else:
        lands = [lax.empty((N_DEV, a.shape[0] // N_DEV, a.shape[1]), a.dtype) for a in srcs]

    def body(*refs):
        src, land = refs[:nk], refs[nk:2 * nk]
        send, recv, own = refs[2 * nk + 1], refs[2 * nk + 2], refs[2 * nk + 3]
        token = refs[-1]
        x, y, c = _my_pos()
        me = (x, y, c)
        for t in range(nk):
            _own_copy(kind, src[t], land[t], own, t, me).start()
            for k, peer in enumerate(_peers(x, y, c)):
                _exchange_copy(kind, src[t], land[t], send, recv, t, k, peer, me, False).start()
        token[...] = jnp.zeros_like(token)

    hbm = lambda a: pltpu.HBM(a.shape, a.dtype)
    out = pl.pallas_call(
        body, name=name,
        out_shape=(pltpu.SemaphoreType.DMA((nk * (N_DEV - 1),)), pltpu.SemaphoreType.DMA((nk * (N_DEV - 1),)),
                   pltpu.SemaphoreType.DMA((nk,)), *[hbm(a) for a in srcs], *[hbm(a) for a in lands], S((SUB, LANES), F32)),
        in_specs=[HBM] * (2 * nk) + [ANY],
        out_specs=(SEM, SEM, SEM, *[HBM] * (2 * nk), BS(memory_space=pltpu.VMEM)),
        input_output_aliases={i: 3 + i for i in range(2 * nk)},
        compiler_params=pltpu.CompilerParams(has_side_effects=EFFECT),
    )(*[pltpu.with_memory_space_constraint(a, pltpu.HBM) for a in (*srcs, *lands)], after)
    return dict(kind=kind, nk=nk, send=out[0], recv=out[1], own=out[2], srcs=out[3:3 + nk], lands=out[3 + nk:3 + 2 * nk],
                token=out[-1])


def _exchange_wait(ex, after, name):
    kind, nk = ex["kind"], ex["nk"]

    def body(*refs):
        src, land = refs[:nk], refs[nk:2 * nk]
        send, recv, own = refs[2 * nk], refs[2 * nk + 1], refs[2 * nk + 2]
        x, y, c = _my_pos()
        me = (x, y, c)
        for t in range(nk):
            _own_copy(kind, src[t], land[t], own, t, me).wait()
            for k, peer in enumerate(_peers(x, y, c)):
                _exchange_copy(kind, src[t], land[t], send, recv, t, k, peer, me, False).wait_send()
                _exchange_copy(kind, src[t], land[t], send, recv, t, k, peer, me, True).wait_recv()

    hbm = lambda a: pltpu.HBM(a.shape, a.dtype)
    out = pl.pallas_call(
        body, name=name,
        out_shape=(*[hbm(a) for a in ex["srcs"]], *[hbm(a) for a in ex["lands"]]),
        in_specs=[HBM] * (2 * nk) + [SEM, SEM, SEM, ANY], out_specs=tuple([HBM] * (2 * nk)),
        input_output_aliases={i: i for i in range(2 * nk)},
        compiler_params=pltpu.CompilerParams(has_side_effects=EFFECT),
    )(*ex["srcs"], *ex["lands"], ex["send"], ex["recv"], ex["own"], after)
    return list(out[nk:])


BLOCK_BYTES = 4 << 20


def _row_tile(rows, row_bytes, align):
    best = align
    for tr in range(align, rows + 1, align):
        if rows % tr == 0 and tr * row_bytes <= BLOCK_BYTES:
            best = tr
    return best


def _sum8(a):
    _, r, w = a.shape
    tr = _row_tile(r, N_DEV * w * a.dtype.itemsize, 32 // a.dtype.itemsize)

    def body(a_ref, o_ref):
        acc = a_ref[0].astype(F32)
        for d in range(1, N_DEV):
            acc = acc + a_ref[d].astype(F32)
        o_ref[...] = acc

    return pl.pallas_call(
        body, name="sum8", grid=(r // tr,), in_specs=[BS((N_DEV, tr, w), lambda i: (0, i, 0))],
        out_specs=BS((tr, w), lambda i: (i, 0)), out_shape=S((r, w), F32), compiler_params=_cp("arbitrary"),
    )(a)


def _adamw(w, g, m, v):
    r, c = w.shape
    tr = _row_tile(r, c * 4 * 2, SUB)

    def body(w_ref, g_ref, m_ref, v_ref, d_ref, mo_ref, vo_ref):
        d_ref[...], mo_ref[...], vo_ref[...] = _adam_update(w_ref[...], g_ref[...], m_ref[...], v_ref[...])

    blk = BS((tr, c), lambda i: (i, 0))
    return pl.pallas_call(
        body, name="adamw", grid=(r // tr,), in_specs=[blk] * 4, out_specs=[blk] * 3,
        out_shape=[S((r, c), F32)] * 3, compiler_params=_cp("arbitrary"),
    )(w, g, m, v)


def _sum8_t(a, tc=256):
    _, r, w = a.shape

    def body(a_ref, o_ref):
        acc = a_ref[0].astype(F32)
        for d in range(1, N_DEV):
            acc = acc + a_ref[d].astype(F32)
        o_ref[...] = acc.T

    return pl.pallas_call(
        body, name="sum8_t", grid=(w // tc,), in_specs=[BS((N_DEV, r, tc), lambda j: (0, 0, j))],
        out_specs=BS((tc, r), lambda j: (j, 0)), out_shape=S((w, r), F32), compiler_params=_cp("arbitrary"),
    )(a)


def _rows_view(a):
    nl, r, c = a.shape
    assert nl == 2
    return a.transpose(2, 0, 1).reshape(c, nl, r // LANES, LANES).transpose(0, 2, 1, 3).reshape(-1, LANES)


def _rows_view_back(a, shape):
    nl, r, c = shape
    return a.reshape(c, r // LANES, nl, LANES).transpose(0, 2, 1, 3).reshape(c, nl, r).transpose(1, 2, 0)


def _adamw_rows(w, g, m, v, tr=2048):
    n = w.shape[0]

    def body(w_ref, g_ref, m_ref, v_ref, d_ref, mo_ref, vo_ref):
        d_ref[...], mo_ref[...], vo_ref[...] = _adam_update(w_ref[...], g_ref[...], m_ref[...], v_ref[...])

    blk = BS((tr, LANES), lambda i: (i, 0))
    return pl.pallas_call(
        body, name="adamw_rows", grid=(pl.cdiv(n, tr),), in_specs=[blk] * 4, out_specs=[blk] * 3,
        out_shape=[S((n, LANES), F32)] * 3, compiler_params=_cp("arbitrary"),
    )(w, g, m, v)


def _adam_update(w, g, m, v):
    m2 = ADAM_B1 * m + (1.0 - ADAM_B1) * g
    v2 = ADAM_B2 * v + (1.0 - ADAM_B2) * (g * g)
    m_hat = m2 / (1.0 - ADAM_B1 ** ADAM_STEP)
    v_hat = v2 / (1.0 - ADAM_B2 ** ADAM_STEP)
    return -ADAM_LR * (m_hat / (jnp.sqrt(v_hat) + ADAM_EPS) + ADAM_WD * w), m2, v2


def _adamw_layer(w, g, m, v, l, prev):
    nl, r, c = w.shape
    tr = _row_tile(r, c * 4 * 2, SUB)

    def body(w_ref, g_ref, m_ref, v_ref, *refs):
        go_ref, d_ref, mo_ref, vo_ref = refs[-4:]
        gv = g_ref[...]
        go_ref[...] = gv
        d_ref[...], mo_ref[...], vo_ref[...] = _adam_update(w_ref[...], gv, m_ref[...], v_ref[...])

    slot = BS((None, tr, c), lambda i: (l, i, 0))
    keep = [] if prev is None else [ANY] * 4
    return pl.pallas_call(
        body, name="adamw_layer", grid=(r // tr,), in_specs=[slot, BS((tr, c), lambda i: (i, 0)), slot, slot] + keep,
        out_specs=[slot] * 4, out_shape=[S((nl, r, c), F32)] * 4,
        input_output_aliases={} if prev is None else {4 + i: i for i in range(4)},
        compiler_params=_cp("arbitrary"),
    )(w, g, m, v, *(prev or ()))


def _blob(arrays):
    flat = jnp.concatenate([a.reshape(-1) for a in arrays])
    rows = -(-flat.shape[0] // (SUB * LANES)) * SUB
    return jnp.pad(flat, (0, rows * LANES - flat.shape[0])).reshape(rows, LANES)


def _unblob(blob, shapes, lead=()):
    flat = blob.reshape(lead + (-1,))
    out, off = [], 0
    for s in shapes:
        size = math.prod(s)
        out.append(flat[..., off:off + size].reshape(lead + tuple(s)))
        off += size
    return out


def _lanes6(a):
    return jnp.zeros((1, LANES), F32).at[0, GG:GG + C_HEADS].set(a)


def _y_rows(w):
    return jnp.concatenate([w[0:A_W], w[A_W + B_W:], w[A_W:A_W + B_W]], axis=0)


def _y_rows_back(g):
    return jnp.concatenate([g[0:A_W], g[A_W + C_W:], g[A_W:A_W + C_W]], axis=0)


SMALL = ("norm_w", "q_norm_w", "k_norm_w", "sinks", "b_conv_b", "b_ln_w", "b_ln_b", "b_pw_b", "c_a_log", "c_dt_bias",
         "c_onorm_w", "b_conv_w", "c_conv_w")
ORDER = ("norm_w", "w_in", "q_norm_w", "k_norm_w", "sinks", "b_conv_w", "b_conv_b", "b_ln_w", "b_ln_b", "b_pw_w", "b_pw_b",
         "c_conv_w", "c_a_log", "c_dt_bias", "c_onorm_w", "w_out")


def kernel(x, positions, norm_w, w_in, q_norm_w, k_norm_w, sinks, b_conv_w, b_conv_b, b_ln_w, b_ln_b, b_pw_w, b_pw_b, c_conv_w, c_a_log, c_dt_bias, c_onorm_w, w_out, loss_target, m_norm_w, m_w_in, m_q_norm_w, m_k_norm_w, m_sinks, m_b_conv_w, m_b_conv_b, m_b_ln_w, m_b_ln_b, m_b_pw_w, m_b_pw_b, m_c_conv_w, m_c_a_log, m_c_dt_bias, m_c_onorm_w, m_w_out, v_norm_w, v_w_in, v_q_norm_w, v_k_norm_w, v_sinks, v_b_conv_w, v_b_conv_b, v_b_ln_w, v_b_ln_b, v_b_pw_w, v_b_pw_b, v_c_conv_w, v_c_a_log, v_c_dt_bias, v_c_onorm_w, v_w_out):
    W = dict(norm_w=norm_w, w_in=w_in, q_norm_w=q_norm_w, k_norm_w=k_norm_w, sinks=sinks, b_conv_w=b_conv_w, b_conv_b=b_conv_b,
             b_ln_w=b_ln_w, b_ln_b=b_ln_b, b_pw_w=b_pw_w, b_pw_b=b_pw_b, c_conv_w=c_conv_w, c_a_log=c_a_log,
             c_dt_bias=c_dt_bias, c_onorm_w=c_onorm_w, w_out=w_out)
    M = dict(norm_w=m_norm_w, w_in=m_w_in, q_norm_w=m_q_norm_w, k_norm_w=m_k_norm_w, sinks=m_sinks, b_conv_w=m_b_conv_w,
             b_conv_b=m_b_conv_b, b_ln_w=m_b_ln_w, b_ln_b=m_b_ln_b, b_pw_w=m_b_pw_w, b_pw_b=m_b_pw_b, c_conv_w=m_c_conv_w,
             c_a_log=m_c_a_log, c_dt_bias=m_c_dt_bias, c_onorm_w=m_c_onorm_w, w_out=m_w_out)
    V = dict(norm_w=v_norm_w, w_in=v_w_in, q_norm_w=v_q_norm_w, k_norm_w=v_k_norm_w, sinks=v_sinks, b_conv_w=v_b_conv_w,
             b_conv_b=v_b_conv_b, b_ln_w=v_b_ln_w, b_ln_b=v_b_ln_b, b_pw_w=v_b_pw_w, b_pw_b=v_b_pw_b, c_conv_w=v_c_conv_w,
             c_a_log=v_c_a_log, c_dt_bias=v_c_dt_bias, c_onorm_w=v_c_onorm_w, w_out=v_w_out)
    nseq, t, d = x.shape
    n = nseq * t
    tr = min(256, t)
    tmm = min(512, n)
    tmw = min(1024, n)
    tkk = min(2048, n)
    me = _dev_index(_my_pos())
    xs = [x.reshape(n, d)]
    tgt = loss_target.reshape(n, d)
    tabs = _rope_tables(positions.reshape(n))

    win_p = _pack_cols(w_in).astype(BF16)
    wout_b = w_out.astype(BF16)
    sharded_small = (b_pw_w, b_conv_w, c_conv_w)
    g_win0, g_small = _all_gather([win_p[0], _blob(sharded_small)])
    win = [g_win0]
    later = _exchange_start("gather", [win_p[1], wout_b[0], wout_b[1]], g_small, "gather_start")
    pw_all, cw_all, ccw_all = _unblob(g_small, [a.shape for a in sharded_small], lead=(N_DEV,))
    pw_all = pw_all.transpose(1, 0, 2, 3).reshape(DEPTH, B_W, B_W).astype(BF16)
    cw_all = cw_all.transpose(1, 2, 0, 3).reshape(DEPTH, B_K, B_W)
    ccw_all = ccw_all.transpose(1, 2, 0, 3).reshape(DEPTH, C_K, 3 * C_W)

    def layer_params(l):
        return dict(
            nw=norm_w[l][None], qw=jnp.tile(q_norm_w[l], 2)[None], kw=jnp.tile(k_norm_w[l], 2)[None], sinks=sinks[l],
            cw=jnp.pad(cw_all[l], ((0, HALO_B - B_K), (0, 0))), cb=b_conv_b[l][None], lnw=b_ln_w[l][None], lnb=b_ln_b[l][None],
            pw=pw_all[l], pwb=b_pw_b[l][None], ccw=jnp.pad(ccw_all[l], ((0, SUB - C_K), (0, 0))),
            alog=_lanes6(c_a_log[l]), dtb=_lanes6(c_dt_bias[l]), onw=c_onorm_w[l][None])

    saved = []
    for l in range(DEPTH):
        q = layer_params(l)
        nw = q["nw"] + later["token"][0:1, 0:1] if l == 0 else q["nw"]
        p, h = _inproj(xs[l], nw, win[l], tm=tmw)
        y, o_a, lse = _attn_fwd(p, tabs, q["qw"], q["kw"], q["sinks"], nseq)
        gates = _gdn_gates_fwd(p, q["alog"], q["dtb"], tm=tr)
        xc, qkv = _gdn_pre_fwd(p, q["ccw"], nseq, tm=tr)
        y, o_c, u, w, tinv, ss = _gdn_chunk_fwd(qkv, gates, p, y, q["onw"], nseq)
        y, hc = _conf_fwd(p, y, q["cw"], q["cb"], q["lnw"], q["lnb"], q["pw"], q["pwb"], nseq, tm=tr)
        saved.append(dict(q=q, p=p, h=h, y=y, o_a=o_a, lse=lse, gates=gates, xc=xc, qkv=qkv, o_c=o_c, u=u, w=w, tinv=tinv,
                          ss=ss, hc=hc))
        if l == 0:
            g_win1, g_wout0, g_wout1 = _exchange_wait(later, y, "gather_wait")
            win.append(g_win1)
            wout = [_y_rows(g_wout0), _y_rows(g_wout1)]
        if l + 1 < DEPTH:
            xs.append(_outproj(xs[l], y, wout[l], tm=tmw, tn=512))
        else:
            dxn, lsum = _outproj_loss(xs[l], y, wout[l], tgt, tm=tmw, tn=512)
    loss = lax.psum(jnp.sum(lsum) * (0.5 / d), ("x", "y", "c"))

    sent, smalls = [None] * DEPTH, [None] * DEPTH
    for l in reversed(range(DEPTH)):
        s = saved[l]
        q, p = s["q"], s["p"]
        dy = _matmul(dxn, wout[l], "nt", F32, tmw, 512, d, "outproj_bwd_dy")
        dwout = _y_rows_back(_matmul(s["y"], dxn, "tn", BF16, 1024, 1024, tkk, "outproj_bwd_dw"))
        dp, dkv, dqw, dkw, dsk = _attn_bwd(p, dy, s["o_a"], s["lse"], tabs, q["qw"], q["kw"], q["sinks"], nseq)
        dp, dqkv, dgate, donw = _gdn_chunk_bwd(s["qkv"], s["gates"], p, dy, dp, q["onw"], s["o_c"], s["u"], s["w"],
                                               s["tinv"], s["ss"], nseq)
        dp, dccw = _gdn_pre_bwd(p, dqkv, s["xc"], dp, q["ccw"], nseq, tm=tr)
        early = P_K // 768
        dwin_a = _matmul(s["h"], dp, "tn", BF16, 1024, 768, tkk, "inproj_bwd_dw_a", b_cols=(0, early))
        sent_a = _exchange_start("scatter", [dwin_a, dwout], donw, "scatter_start_a%d" % l)
        dp = _put_cols(dp, dkv, P_K, sent_a["token"], tm=tmm)
        dp, dal, ddb = _gdn_gates_bwd(dgate, p, q["alog"], q["dtb"], dp, tm=tr)
        dp, dhc, dpw, dpwb, dlnw, dlnb, dcb = _conf_bwd1(p, dy, dp, s["hc"], q["lnw"], q["lnb"], q["pw"], q["pwb"], tm=tr)
        dp, dcw = _conf_bwd2(p, dhc, dp, q["cw"], nseq, tm=tr)
        dwin_b = _matmul(s["h"], dp, "tn", BF16, 1024, 768, tkk, "inproj_bwd_dw_b", b_cols=(early, P_W // 768 - early))
        sent_b = _exchange_start("scatter", [dwin_b, dpw], dpwb, "scatter_start_b%d" % l)
        sent[l] = (sent_a, sent_b)
        dxn, dnw = _inproj_bwd_dx(dp, win[l], xs[l], q["nw"] + sent_b["token"][0:1, 0:1], dxn, tm=tmm)
        halves = lambda a: a.sum(0)[:A_DH] + a.sum(0)[A_DH:]
        smalls[l] = dict(
            norm_w=dnw.sum(0), q_norm_w=halves(dqw), k_norm_w=halves(dkw), sinks=dsk.sum(0)[:A_HEADS], b_conv_b=dcb.sum(0),
            b_ln_w=dlnw.sum(0), b_ln_b=dlnb.sum(0), b_pw_b=dpwb.sum(0), c_a_log=dal.sum(0)[GG:GG + C_HEADS],
            c_dt_bias=ddb.sum(0)[GG:GG + C_HEADS], c_onorm_w=donw.sum(0),
            b_conv_w=dcw.reshape(B_K, SUB, B_W).sum(1), c_conv_w=dccw.reshape(C_K, SUB, 3 * C_W).sum(1))
    grad_x = dxn.reshape(nseq, t, d)

    G, delta, new_m, new_v = {}, {}, {}, {}
    big = ("w_in", "w_out", "b_pw_w")
    stacks = {k: None for k in big}
    after = dxn
    g_t = [None] * DEPTH
    for l in reversed(range(DEPTH)):
        r_win_a, r_wout = _exchange_wait(sent[l][0], after, "scatter_wait_a%d" % l)
        r_win_b, r_pw = _exchange_wait(sent[l][1], r_wout, "scatter_wait_b%d" % l)
        g_t[l] = jnp.concatenate([_sum8_t(r_win_a), _sum8_t(r_win_b)], axis=0).reshape(P_W, -1, LANES)
        for k, r in (("w_out", r_wout), ("b_pw_w", r_pw)):
            stacks[k] = _adamw_layer(W[k], _sum8(r), M[k], V[k], l, stacks[k])
        after = stacks["w_out"][1]
    g_in = jnp.stack(g_t, axis=2).reshape(-1, LANES)
    g_in = _unpack_cols(g_in, axis=0, each=g_in.shape[0] // P_W)
    rows = _adamw_rows(_rows_view(w_in), g_in, _rows_view(m_w_in), _rows_view(v_w_in))
    stacks["w_in"] = [_rows_view_back(a, w_in.shape) for a in (g_in, *rows)]
    for k in big:
        G[k], delta[k], new_m[k], new_v[k] = stacks[k]
    part = _blob([jnp.stack([smalls[l][k] for l in range(DEPTH)]) for k in SMALL])
    (tot,) = _all_gather([part], after=rows[0])
    tot = _sum8(tot.reshape(N_DEV, part.shape[0], LANES))
    full_shapes = [(DEPTH,) + smalls[0][k].shape for k in SMALL]
    for k, g in zip(SMALL, _unblob(tot, full_shapes)):
        G[k] = g
    G["b_conv_w"] = lax.dynamic_slice_in_dim(G["b_conv_w"], me * (B_W // N_DEV), B_W // N_DEV, axis=2)
    G["c_conv_w"] = lax.dynamic_slice_in_dim(G["c_conv_w"], me * (3 * C_W // N_DEV), 3 * C_W // N_DEV, axis=2)
    dl, mo, vo = _adamw(*[_blob([src[k] for k in SMALL]) for src in (W, G, M, V)])
    shapes = [W[k].shape for k in SMALL]
    for k, a, b, c in zip(SMALL, _unblob(dl, shapes), _unblob(mo, shapes), _unblob(vo, shapes)):
        delta[k], new_m[k], new_v[k] = a, b, c
    return (loss, grad_x, *[G[k] for k in ORDER], *[delta[k] for k in ORDER], *[new_m[k] for k in ORDER],
            *[new_v[k] for k in ORDER])
```

```python
import functools
import math

import jax
import jax.numpy as jnp
from jax import lax
from jax.experimental import pallas as pl
from jax.experimental.pallas import tpu as pltpu

F32 = jnp.float32
BF16 = jnp.bfloat16
HI = lax.Precision.HIGHEST
MESH = pl.DeviceIdType.MESH
S = jax.ShapeDtypeStruct
BS = pl.BlockSpec

N_DEV = 8
DEPTH = 2
D_MODEL = 2048
A_HEADS, A_KV, A_DH, A_W, A_KVW = 12, 4, 64, 768, 256
ROT = 16
THETA = 500000.0
ABLK = 128
B_W, B_K = 512, 31
C_HEADS, C_DH, C_W, C_K, CHUNK = 6, 128, 768, 4, 64
EPS = 1e-6
IN_COLS = 6668
P_Q, P_ZA, P_ZC, P_QKV, P_K, P_V, P_UB, P_ZB, P_BA, P_W = 0, 768, 1536, 2304, 4608, 4864, 5120, 6144, 6656, 6912
Y_A, Y_C, Y_B = 0, 768, 1536
LANES = 128
SUB = 8

ADAM_LR, ADAM_B1, ADAM_B2, ADAM_EPS, ADAM_WD, ADAM_STEP = 0.001, 0.9, 0.999, 1e-08, 0.01, 10


def _cp(*sem, vmem=None):
    kw = {}
    if sem:
        kw["dimension_semantics"] = sem
    if vmem:
        kw["vmem_limit_bytes"] = vmem
    return pltpu.CompilerParams(**kw)


def _pack_cols(w):
    z = jnp.zeros(w.shape[:-1] + (P_W - IN_COLS,), w.dtype)
    return jnp.concatenate([w[..., 0:768], w[..., 1280:2048], w[..., 5900:6668], w[..., 3584:5888],
                            w[..., 768:1024], w[..., 1024:1280], w[..., 2048:3072], w[..., 3072:3584],
                            w[..., 5888:5900], z], axis=-1)


def _unpack_cols(g, axis=-1, each=1):
    parts = ((P_Q, 768), (P_K, 256), (P_V, 256), (P_ZA, 768), (P_UB, 1024), (P_ZB, 512), (P_QKV, 2304), (P_BA, 12), (P_ZC, 768))
    return jnp.concatenate([lax.slice_in_dim(g, each * o, each * (o + n), axis=axis) for o, n in parts], axis=axis)


def _sigmoid(x):
    return 0.5 * jnp.tanh(0.5 * x) + 0.5


def _dsilu(x, sg):
    return sg * (1.0 + x * (1.0 - sg))


def _fold8(x):
    r, c = x.shape
    return x.reshape(r // SUB, SUB, c).sum(axis=0)


def _dot(a, b, prec=None):
    return jnp.dot(a, b, preferred_element_type=F32, precision=prec)


def _dot_nt(a, b, prec=None):
    return lax.dot_general(a, b, (((1,), (1,)), ((), ())), preferred_element_type=F32, precision=prec)


def _dot_tn(a, b, prec=None):
    return lax.dot_general(a, b, (((0,), (0,)), ((), ())), preferred_element_type=F32, precision=prec)


def _lane(shape):
    return lax.broadcasted_iota(jnp.int32, shape, 1)


def _subl(shape):
    return lax.broadcasted_iota(jnp.int32, shape, 0)


def _col(x, j):
    return jnp.sum(jnp.where(_lane(x.shape) == j, x, 0.0), axis=-1, keepdims=True)


def _inproj(x, nw, w, tm=512, tn=768):
    n, d = x.shape
    pw = w.shape[1]

    def body(x_ref, nw_ref, w_ref, p_ref, h_ref):
        @pl.when(pl.program_id(1) == 0)
        def _():
            xv = x_ref[...]
            r = lax.rsqrt(jnp.mean(xv * xv, axis=-1, keepdims=True) + EPS)
            h_ref[...] = (xv * r * nw_ref[...]).astype(BF16)

        p_ref[...] = _dot(h_ref[...], w_ref[...])

    return pl.pallas_call(
        body, name="inproj", grid=(n // tm, pw // tn),
        in_specs=[BS((tm, d), lambda i, j: (i, 0)), BS((1, d), lambda i, j: (0, 0)), BS((d, tn), lambda i, j: (0, j))],
        out_specs=[BS((tm, tn), lambda i, j: (i, j)), BS((tm, d), lambda i, j: (i, 0))],
        out_shape=[S((n, pw), F32), S((n, d), BF16)],
        compiler_params=_cp("arbitrary", "arbitrary"),
    )(x, nw, w)


def _outproj(x, y, w, tm=512, tn=1024):
    n, d = x.shape
    k = y.shape[1]

    def body(x_ref, y_ref, w_ref, o_ref):
        o_ref[...] = x_ref[...] + _dot(y_ref[...], w_ref[...])

    return pl.pallas_call(
        body, name="outproj", grid=(n // tm, d // tn),
        in_specs=[BS((tm, tn), lambda i, j: (i, j)), BS((tm, k), lambda i, j: (i, 0)), BS((k, tn), lambda i, j: (0, j))],
        out_specs=BS((tm, tn), lambda i, j: (i, j)),
        out_shape=S((n, d), F32),
        compiler_params=_cp("arbitrary", "arbitrary"),
    )(x, y, w)


def _outproj_loss(x, y, w, tgt, tm=512, tn=1024):
    n, d = x.shape
    k = y.shape[1]

    def body(x_ref, y_ref, w_ref, t_ref, g_ref, l_ref):
        @pl.when((pl.program_id(0) == 0) & (pl.program_id(1) == 0))
        def _():
            l_ref[...] = jnp.zeros_like(l_ref)

        diff = x_ref[...] + _dot(y_ref[...], w_ref[...]) - t_ref[...]
        g_ref[...] = diff * (1.0 / d)
        f = _fold8(diff * diff)
        acc = f[:, 0:LANES]
        for c in range(1, tn // LANES):
            acc = acc + f[:, c * LANES:(c + 1) * LANES]
        l_ref[...] += acc

    return pl.pallas_call(
        body, name="outproj_loss", grid=(n // tm, d // tn),
        in_specs=[BS((tm, tn), lambda i, j: (i, j)), BS((tm, k), lambda i, j: (i, 0)), BS((k, tn), lambda i, j: (0, j)),
                  BS((tm, tn), lambda i, j: (i, j))],
        out_specs=[BS((tm, tn), lambda i, j: (i, j)), BS((SUB, LANES), lambda i, j: (0, 0))],
        out_shape=[S((n, d), F32), S((SUB, LANES), F32)],
        compiler_params=_cp("arbitrary", "arbitrary"),
    )(x, y, w, tgt)


def _matmul(a, b, mode, out_dtype, tm, tn, tk, name, b_cols=None):
    if mode == "nn":
        (m, kk), nn = a.shape, b.shape[1]
        a_spec, b_spec = BS((tm, tk), lambda i, j, k: (i, k)), BS((tk, tn), lambda i, j, k: (k, j))
        dot = _dot
    elif mode == "nt":
        (m, kk), nn = a.shape, b.shape[0]
        a_spec, b_spec = BS((tm, tk), lambda i, j, k: (i, k)), BS((tn, tk), lambda i, j, k: (j, k))
        dot = _dot_nt
    else:
        j0, nj = b_cols or (0, b.shape[1] // tn)
        (kk, m), nn = a.shape, nj * tn
        a_spec, b_spec = BS((tk, tm), lambda i, j, k: (k, i)), BS((tk, tn), lambda i, j, k: (k, j0 + j))
        dot = _dot_tn
    nk = kk // tk

    def body(a_ref, b_ref, o_ref, acc_ref):
        kid = pl.program_id(2)

        @pl.when(kid == 0)
        def _():
            acc_ref[...] = jnp.zeros_like(acc_ref)

        acc_ref[...] += dot(a_ref[...].astype(BF16), b_ref[...].astype(BF16))

        @pl.when(kid == nk - 1)
        def _():
            o_ref[...] = acc_ref[...].astype(out_dtype)

    return pl.pallas_call(
        body, name=name, grid=(m // tm, nn // tn, nk),
        in_specs=[a_spec, b_spec], out_specs=BS((tm, tn), lambda i, j, k: (i, j)),
        out_shape=S((m, nn), out_dtype), scratch_shapes=[pltpu.VMEM((tm, tn), F32)],
        compiler_params=_cp("arbitrary", "arbitrary", "arbitrary"),
    )(a, b)


SLAB = 16


STRIP = 64


def _inproj_bwd_dx(dp, w, x, nw, dres, tm=1024, tk=1152):
    n, d = x.shape
    nk = dp.shape[1] // tk
    nstrip = tm // STRIP

    def body(dp_ref, w_ref, x_hbm, nw_ref, dr_hbm, dx_hbm, dnw_ref, acc_ref, xb, rb, ob, sem):
        kid = pl.program_id(1)
        row0 = pl.program_id(0) * tm

        @pl.when((pl.program_id(0) == 0) & (kid == 0))
        def _():
            dnw_ref[...] = jnp.zeros_like(dnw_ref)

        @pl.when(kid == 0)
        def _():
            acc_ref[...] = jnp.zeros_like(acc_ref)

        acc_ref[...] += _dot_nt(dp_ref[...], w_ref[...])

        def rows_of(s):
            return pl.ds(pl.multiple_of(row0 + s * STRIP, STRIP), STRIP)

        def loads(s, slot):
            return (pltpu.make_async_copy(x_hbm.at[rows_of(s), :], xb.at[slot], sem.at[0, slot]),
                    pltpu.make_async_copy(dr_hbm.at[rows_of(s), :], rb.at[slot], sem.at[1, slot]))

        def store(s, slot):
            return pltpu.make_async_copy(ob.at[slot], dx_hbm.at[rows_of(s), :], sem.at[2, slot])

        @pl.when(kid == nk - 1)
        def _():
            for cp in loads(0, 0):
                cp.start()

            def strip(s, carry):
                slot = s % 2
                for cp in loads(s, slot):
                    cp.wait()

                @pl.when(s + 1 < nstrip)
                def _():
                    for cp in loads(s + 1, 1 - slot):
                        cp.start()

                @pl.when(s >= 2)
                def _():
                    store(s - 2, slot).wait()

                for j in range(STRIP // SLAB):
                    rows = pl.ds(SLAB * j, SLAB)
                    dh = acc_ref[pl.ds(pl.multiple_of(s * STRIP + SLAB * j, SLAB), SLAB), :]
                    xv = xb[slot, rows, :]
                    r = lax.rsqrt(jnp.mean(xv * xv, axis=-1, keepdims=True) + EPS)
                    dnw_ref[...] += _fold8(dh * xv * r)
                    g = dh * nw_ref[...]
                    mm = jnp.mean(g * xv, axis=-1, keepdims=True)
                    ob[slot, rows, :] = rb[slot, rows, :] + r * g - xv * (r * r * r * mm)
                store(s, slot).start()
                return carry

            lax.fori_loop(0, nstrip, strip, 0)
            store(nstrip - 2, nstrip % 2).wait()
            store(nstrip - 1, (nstrip - 1) % 2).wait()

    strips = pltpu.VMEM((2, STRIP, d), F32)
    return pl.pallas_call(
        body, name="inproj_bwd_dx", grid=(n // tm, nk),
        in_specs=[BS((tm, tk), lambda i, k: (i, k)), BS((d, tk), lambda i, k: (0, k)), ANY,
                  BS((1, d), lambda i, k: (0, 0)), ANY],
        out_specs=[ANY, BS((SUB, d), lambda i, k: (0, 0))],
        out_shape=[S((n, d), F32), S((SUB, d), F32)],
        scratch_shapes=[pltpu.VMEM((tm, d), F32), strips, strips, strips, pltpu.SemaphoreType.DMA((3, 2))],
        compiler_params=_cp("arbitrary", "arbitrary"),
    )(dp, w, x, nw, dres)


def _rope_tables(pos):
    half = ROT // 2
    inv = THETA ** (-jnp.arange(0, ROT, 2, dtype=F32) / ROT)
    ang = pos.astype(F32)[:, None] * inv
    cos, sin = jnp.cos(ang), jnp.sin(ang)
    n = pos.shape[0]
    one = jnp.ones((n, A_DH - ROT), F32)
    zero = jnp.zeros((n, A_DH - ROT), F32)
    zh = jnp.zeros((n, half), F32)
    c = jnp.concatenate([cos, cos, one], axis=1)
    s1 = jnp.concatenate([-sin, zh, zero], axis=1)
    s2 = jnp.concatenate([zh, sin, zero], axis=1)
    return tuple(jnp.concatenate([t, t], axis=1) for t in (c, s1, s2))


def _half_stat(t):
    lo = _lane(t.shape) < A_DH
    s_lo = jnp.sum(jnp.where(lo, t, 0.0), axis=-1, keepdims=True)
    s_hi = jnp.sum(jnp.where(lo, 0.0, t), axis=-1, keepdims=True)
    return jnp.where(lo, s_lo, s_hi)


def _normrope(x, w, c, s1, s2):
    r = lax.rsqrt(_half_stat(x * x) * (1.0 / A_DH) + EPS)
    xn = x * r * w
    return xn * c + pltpu.roll(xn, LANES - ROT // 2, 1) * s1 + pltpu.roll(xn, ROT // 2, 1) * s2, r


def _normrope_bwd(dy, x, r, w, c, s1, s2):
    dxn = dy * c + pltpu.roll(dy * s1, ROT // 2, 1) + pltpu.roll(dy * s2, LANES - ROT // 2, 1)
    g = dxn * w
    mm = _half_stat(g * x) * (1.0 / A_DH)
    return r * g - x * (r * r * r * mm), dxn * x * r


def _attn_mask(first):
    qi = _subl((ABLK, 2 * ABLK))
    kj = _lane((ABLK, 2 * ABLK))
    dist = qi + ABLK - kj
    return (dist >= 0) & (dist < ABLK) & (jnp.logical_not(first) | (kj >= ABLK))


def _keep_half(x, b):
    lo = _lane(x.shape) < A_DH
    return jnp.where(lo if b == 0 else jnp.logical_not(lo), x, jnp.zeros_like(x))


def _head_operand(x, j):
    a, b = j % 2, (j // 3) % 2
    return _keep_half(x if a == b else pltpu.roll(x, A_DH, 1), b)


def _head_result(x, j):
    a, b = j % 2, (j // 3) % 2
    return _keep_half(x if a == b else pltpu.roll(x, A_DH, 1), a)


def _attn_fwd(p, tabs, qw, kw, sinks, nseq):
    n = p.shape[0]
    nb = n // nseq // ABLK
    cur = lambda b, i: (b * nb + i, 0)
    prv = lambda b, i: (b * nb + jnp.maximum(i - 1, 0), 0)
    colblk = lambda f, w, off: (lambda b, i: (f(b, i)[0], off // w))

    def body(q_ref, za_ref, kc_ref, vc_ref, kp_ref, vp_ref, c_ref, s1_ref, s2_ref, cp_ref, s1p_ref, s2p_ref,
             qw_ref, kw_ref, sink_ref, y_ref, o_ref, lse_ref):
        first = pl.program_id(1) == 0
        tc = (c_ref[...], s1_ref[...], s2_ref[...])
        tp = (cp_ref[...], s1p_ref[...], s2p_ref[...])
        q, kc, kp = q_ref[...], kc_ref[...], kp_ref[...]
        qn = [_normrope(q[:, LANES * b:LANES * (b + 1)], qw_ref[...], *tc)[0] for b in range(A_W // LANES)]
        k2, v2 = [], []
        for b in range(A_KVW // LANES):
            sl = slice(LANES * b, LANES * (b + 1))
            k2.append(jnp.concatenate([_normrope(kp[:, sl], kw_ref[...], *tp)[0],
                                       _normrope(kc[:, sl], kw_ref[...], *tc)[0]], axis=0).astype(BF16))
            v2.append(jnp.concatenate([vp_ref[:, sl], vc_ref[:, sl]], axis=0).astype(BF16))
        valid = _attn_mask(first)
        heads = range(A_HEADS)
        qm = [_head_operand(qn[j // 2], j).astype(BF16) for j in heads]
        s = [jnp.where(valid, _dot_nt(qm[j], k2[j // 6]) * (A_DH ** -0.5), -jnp.inf) for j in heads]
        m = [jnp.maximum(jnp.max(s[j], axis=-1, keepdims=True), sink_ref[j]) for j in heads]
        e = [jnp.exp(s[j] - m[j]) for j in heads]
        den = [jnp.sum(e[j], axis=-1, keepdims=True) + jnp.exp(sink_ref[j] - m[j]) for j in heads]
        outs = [_head_result(_dot((e[j] * (1.0 / den[j])).astype(BF16), v2[j // 6]), j) for j in heads]
        lse = jnp.zeros((ABLK, LANES), F32)
        for j in heads:
            lse = jnp.where(_lane(lse.shape) == j, m[j] + jnp.log(den[j]), lse)
        o = jnp.concatenate([outs[2 * b] + outs[2 * b + 1] for b in range(A_W // LANES)], axis=1)
        za = za_ref[...]
        o_ref[...] = o
        lse_ref[...] = lse
        y_ref[...] = (o * (za * _sigmoid(za))).astype(BF16)

    tab_specs = [BS((ABLK, LANES), cur)] * 3 + [BS((ABLK, LANES), prv)] * 3
    return pl.pallas_call(
        body, name="attn_fwd", grid=(nseq, nb),
        in_specs=[BS((ABLK, A_W), colblk(cur, A_W, P_Q)), BS((ABLK, A_W), colblk(cur, A_W, P_ZA)),
                  BS((ABLK, A_KVW), colblk(cur, A_KVW, P_K)), BS((ABLK, A_KVW), colblk(cur, A_KVW, P_V)),
                  BS((ABLK, A_KVW), colblk(prv, A_KVW, P_K)), BS((ABLK, A_KVW), colblk(prv, A_KVW, P_V))]
        + tab_specs + [BS((1, LANES), lambda b, i: (0, 0))] * 2 + [BS(memory_space=pltpu.SMEM)],
        out_specs=[BS((ABLK, A_W), colblk(cur, A_W, Y_A)), BS((ABLK, A_W), cur), BS((ABLK, LANES), cur)],
        out_shape=[S((n, D_MODEL), BF16), S((n, A_W), F32), S((n, LANES), F32)],
        compiler_params=_cp("arbitrary", "arbitrary"),
    )(p, p, p, p, p, p, *tabs, *tabs, qw, kw, sinks)


def _attn_bwd(p, dy, o, lse, tabs, qw, kw, sinks, nseq):
    n = p.shape[0]
    nb = n // nseq // ABLK
    cur = lambda b, i: (b * nb + jnp.minimum(i, nb - 1), 0)
    prv = lambda b, i: (b * nb + jnp.maximum(i - 1, 0), 0)
    colblk = lambda f, w, off: (lambda b, i: (f(b, i)[0], off // w))

    def body(q_ref, za_ref, kc_ref, vc_ref, kp_ref, vp_ref, dy_ref, o_ref, lse_ref,
             c_ref, s1_ref, s2_ref, cp_ref, s1p_ref, s2p_ref, qw_ref, kw_ref, sink_ref,
             dqza_ref, dkv_ref, dqw_ref, dkw_ref, dsk_ref, tk_ref, tv_ref, ck_ref, cv_ref):
        i = pl.program_id(1)
        first = i == 0
        tc = (c_ref[...], s1_ref[...], s2_ref[...])
        tp = (cp_ref[...], s1p_ref[...], s2p_ref[...])
        nkb = A_KVW // LANES

        @pl.when((pl.program_id(0) == 0) & first)
        def _():
            dqw_ref[...] = jnp.zeros_like(dqw_ref)
            dkw_ref[...] = jnp.zeros_like(dkw_ref)
            dsk_ref[...] = jnp.zeros_like(dsk_ref)

        @pl.when(i < nb)
        def _():
            q, kc, kp = q_ref[...], kc_ref[...], kp_ref[...]
            qn, rq = [], []
            for b in range(A_W // LANES):
                a, r = _normrope(q[:, LANES * b:LANES * (b + 1)], qw_ref[...], *tc)
                qn.append(a)
                rq.append(r)
            k2, v2 = [], []
            for b in range(nkb):
                sl = slice(LANES * b, LANES * (b + 1))
                k2.append(jnp.concatenate([_normrope(kp[:, sl], kw_ref[...], *tp)[0],
                                           _normrope(kc[:, sl], kw_ref[...], *tc)[0]], axis=0).astype(BF16))
                v2.append(jnp.concatenate([vp_ref[:, sl], vc_ref[:, sl]], axis=0).astype(BF16))
            valid = _attn_mask(first)
            za, dy, o, lse = za_ref[...], dy_ref[...], o_ref[...], lse_ref[...]
            sg = _sigmoid(za)
            do = dy * za * sg
            dqza_ref[:, A_W:2 * A_W] = (dy * o * _dsilu(za, sg)).astype(BF16)
            heads = range(A_HEADS)
            blk = lambda x, b: x[:, LANES * b:LANES * (b + 1)]
            qm = [_head_operand(qn[j // 2], j).astype(BF16) for j in heads]
            lj = [_col(lse, j) for j in heads]
            pr = [jnp.exp(jnp.where(valid, _dot_nt(qm[j], k2[j // 6]) * (A_DH ** -0.5), -jnp.inf) - lj[j]) for j in heads]
            dom = [_head_operand(blk(do, j // 2), j).astype(BF16) for j in heads]
            delta = [jnp.sum(_keep_half(blk(do, j // 2) * blk(o, j // 2), j % 2), axis=-1, keepdims=True) for j in heads]
            ds = [(pr[j] * (_dot_nt(dom[j], v2[j // 6]) - delta[j]) * (A_DH ** -0.5)).astype(BF16) for j in heads]
            dqh = [_head_result(_dot(ds[j], k2[j // 6]), j) for j in heads]
            dkh = [_dot_tn(ds[j], qm[j]) for j in heads]
            dvh = [_dot_tn(pr[j].astype(BF16), dom[j]) for j in heads]
            per_blk = A_HEADS // nkb
            dks = [sum(dkh[per_blk * g + 1:per_blk * (g + 1)], dkh[per_blk * g]) for g in range(nkb)]
            dvs = [sum(dvh[per_blk * g + 1:per_blk * (g + 1)], dvh[per_blk * g]) for g in range(nkb)]
            dsk = jnp.zeros((ABLK, LANES), F32)
            for j in heads:
                dsk = dsk + jnp.where(_lane(dsk.shape) == j, -jnp.exp(sink_ref[j] - lj[j]) * delta[j], 0.0)
            dsk_ref[...] += _fold8(dsk)
            dqn = jnp.concatenate([dqh[2 * b] + dqh[2 * b + 1] for b in range(A_W // LANES)], axis=1)
            dqw = jnp.zeros((SUB, LANES), F32)
            dqo = []
            for b in range(A_W // LANES):
                sl = slice(LANES * b, LANES * (b + 1))
                dx, wt = _normrope_bwd(dqn[:, sl], q[:, sl], rq[b], qw_ref[...], *tc)
                dqo.append(dx)
                dqw = dqw + _fold8(wt)
            dqw_ref[...] += dqw
            dqza_ref[:, 0:A_W] = jnp.concatenate(dqo, axis=1).astype(BF16)
            tk_ref[...] = jnp.concatenate(dks, axis=1)
            tv_ref[...] = jnp.concatenate(dvs, axis=1)

        @pl.when(i == nb)
        def _():
            tk_ref[...] = jnp.zeros_like(tk_ref)
            tv_ref[...] = jnp.zeros_like(tv_ref)

        @pl.when(i > 0)
        def _():
            kp = kp_ref[...]
            dkn = ck_ref[...] + tk_ref[0:ABLK, :]
            dkw = jnp.zeros((SUB, LANES), F32)
            dko = []
            for b in range(nkb):
                sl = slice(LANES * b, LANES * (b + 1))
                r = _normrope(kp[:, sl], kw_ref[...], *tp)[1]
                dx, wt = _normrope_bwd(dkn[:, sl], kp[:, sl], r, kw_ref[...], *tp)
                dko.append(dx)
                dkw = dkw + _fold8(wt)
            dkw_ref[...] += dkw
            dkv_ref[:, 0:A_KVW] = jnp.concatenate(dko, axis=1).astype(BF16)
            dkv_ref[:, A_KVW:2 * A_KVW] = (cv_ref[...] + tv_ref[0:ABLK, :]).astype(BF16)

        ck_ref[...] = tk_ref[ABLK:2 * ABLK, :]
        cv_ref[...] = tv_ref[ABLK:2 * ABLK, :]

    tab_specs = [BS((ABLK, LANES), cur)] * 3 + [BS((ABLK, LANES), prv)] * 3
    acc = BS((SUB, LANES), lambda b, i: (0, 0))
    return pl.pallas_call(
        body, name="attn_bwd", grid=(nseq, nb + 1),
        in_specs=[BS((ABLK, A_W), colblk(cur, A_W, P_Q)), BS((ABLK, A_W), colblk(cur, A_W, P_ZA)),
                  BS((ABLK, A_KVW), colblk(cur, A_KVW, P_K)), BS((ABLK, A_KVW), colblk(cur, A_KVW, P_V)),
                  BS((ABLK, A_KVW), colblk(prv, A_KVW, P_K)), BS((ABLK, A_KVW), colblk(prv, A_KVW, P_V)),
                  BS((ABLK, A_W), colblk(cur, A_W, Y_A)), BS((ABLK, A_W), cur), BS((ABLK, LANES), cur)]
        + tab_specs + [BS((1, LANES), lambda b, i: (0, 0))] * 2 + [BS(memory_space=pltpu.SMEM)],
        out_specs=[BS((ABLK, 2 * A_W), cur), BS((ABLK, 2 * A_KVW), prv), acc, acc, acc],
        out_shape=[S((n, P_W), BF16), S((n, 2 * A_KVW), BF16)] + [S((SUB, LANES), F32)] * 3,
        scratch_shapes=[pltpu.VMEM((2 * ABLK, A_KVW), F32)] * 2 + [pltpu.VMEM((ABLK, A_KVW), F32)] * 2,
        compiler_params=_cp("arbitrary", "arbitrary"),
    )(p, p, p, p, p, p, dy, o, lse, *tabs, *tabs, qw, kw, sinks)


def _put_cols(dst, src, col_off, after, tm=512):
    n, w = src.shape

    def body(s_ref, d_in_ref, after_ref, d_ref):
        d_ref[...] = s_ref[...]

    return pl.pallas_call(
        body, name="put_cols", grid=(n // tm,),
        in_specs=[BS((tm, w), lambda i: (i, 0)), BS(memory_space=pl.ANY), BS(memory_space=pl.ANY)],
        out_specs=BS((tm, w), lambda i: (i, col_off // w)),
        out_shape=S(dst.shape, dst.dtype), input_output_aliases={1: 0},
        compiler_params=_cp("arbitrary"),
    )(src, dst, after)


HALO_B = 32


def _layernorm(hc, lnw, lnb):
    mu = jnp.mean(hc, axis=-1, keepdims=True)
    xc = hc - mu
    rstd = lax.rsqrt(jnp.mean(xc * xc, axis=-1, keepdims=True) + EPS)
    xhat = xc * rstd
    return xhat, rstd, xhat * lnw + lnb


def _shifted_copies(buf_ref, sh_ref):
    rows = sh_ref.shape[1]
    for b in range(1, SUB):
        sh_ref[b - 1] = buf_ref[pl.ds(b, rows), :]


def _rows_from(buf_ref, sh_ref, off, rows, cols=slice(None)):
    a, b = divmod(off, SUB)
    if b == 0:
        return buf_ref[pl.ds(SUB * a, rows), cols]
    return sh_ref[b - 1, pl.ds(SUB * a, rows), cols]


def _conf_fwd(p, y, cw, cb, lnw, lnb, pw, pwb, nseq, tm=256):
    n = p.shape[0]
    t = n // nseq
    nt = t // tm
    row = lambda b, i: b * nt + i
    halo = lambda b, i: jnp.maximum((b * t + i * tm) // HALO_B - 1, 0)
    vec = BS((1, B_W), lambda b, i: (0, 0))

    def body(ub_ref, uh_ref, zb_ref, cw_ref, cb_ref, lnw_ref, lnb_ref, pw_ref, pwb_ref, y_in_ref, y_ref, hc_ref, buf_ref, sh_ref):
        ub, uh = ub_ref[...], uh_ref[...]
        hh = uh[:, :B_W] * _sigmoid(uh[:, B_W:])
        buf_ref[0:HALO_B, :] = jnp.where(pl.program_id(1) > 0, hh, 0.0)
        buf_ref[HALO_B:, :] = ub[:, :B_W] * _sigmoid(ub[:, B_W:])
        _shifted_copies(buf_ref, sh_ref)
        hc = jnp.zeros((tm, B_W), F32) + cb_ref[...]
        for k in range(B_K):
            hc = hc + cw_ref[k:k + 1, :] * _rows_from(buf_ref, sh_ref, HALO_B - B_K + 1 + k, tm)
        hc_ref[...] = hc
        ln = _layernorm(hc, lnw_ref[...], lnb_ref[...])[2]
        sw = ln * _sigmoid(ln)
        ob = _dot(sw.astype(BF16), pw_ref[...]) + pwb_ref[...]
        zb = zb_ref[...]
        y_ref[...] = (ob * (zb * _sigmoid(zb))).astype(BF16)

    return pl.pallas_call(
        body, name="conf_fwd", grid=(nseq, nt),
        in_specs=[BS((tm, 2 * B_W), lambda b, i: (row(b, i), P_UB // (2 * B_W))),
                  BS((HALO_B, 2 * B_W), lambda b, i: (halo(b, i), P_UB // (2 * B_W))),
                  BS((tm, B_W), lambda b, i: (row(b, i), P_ZB // B_W)),
                  BS((HALO_B, B_W), lambda b, i: (0, 0)), vec, vec, vec, BS((B_W, B_W), lambda b, i: (0, 0)), vec,
                  BS(memory_space=pl.ANY)],
        out_specs=[BS((tm, B_W), lambda b, i: (row(b, i), Y_B // B_W)), BS((tm, B_W), lambda b, i: (row(b, i), 0))],
        out_shape=[S(y.shape, y.dtype), S((n, B_W), F32)], input_output_aliases={9: 0},
        scratch_shapes=[pltpu.VMEM((HALO_B + tm, B_W), F32), pltpu.VMEM((SUB - 1, HALO_B + tm - SUB, B_W), F32)],
        compiler_params=_cp("arbitrary", "arbitrary"),
    )(p, p, p, cw, cb, lnw, lnb, pw, pwb, y)


def _conf_bwd1(p, dy, dp, hc, lnw, lnb, pw, pwb, tm=256):
    n = p.shape[0]
    vec = BS((1, B_W), lambda i: (0, 0))
    acc = BS((SUB, B_W), lambda i: (0, 0))

    def body(dy_ref, zb_ref, hc_ref, lnw_ref, lnb_ref, pw_ref, pwb_ref, dp_in_ref,
             dzb_ref, dhc_ref, dpw_ref, dpwb_ref, dlnw_ref, dlnb_ref, dcb_ref):
        @pl.when(pl.program_id(0) == 0)
        def _():
            for r in (dpw_ref, dpwb_ref, dlnw_ref, dlnb_ref, dcb_ref):
                r[...] = jnp.zeros_like(r)

        xhat, rstd, ln = _layernorm(hc_ref[...], lnw_ref[...], lnb_ref[...])
        sgl = _sigmoid(ln)
        sw = (ln * sgl).astype(BF16)
        ob = _dot(sw, pw_ref[...]) + pwb_ref[...]
        dy, zb = dy_ref[...], zb_ref[...]
        sgz = _sigmoid(zb)
        dzb_ref[...] = (dy * ob * _dsilu(zb, sgz)).astype(BF16)
        dob = dy * zb * sgz
        dobb = dob.astype(BF16)
        dpwb_ref[...] += _fold8(dob)
        dpw_ref[...] += _dot_tn(sw, dobb)
        dln = _dot_nt(dobb, pw_ref[...]) * _dsilu(ln, sgl)
        dlnw_ref[...] += _fold8(dln * xhat)
        dlnb_ref[...] += _fold8(dln)
        dxh = dln * lnw_ref[...]
        dhc = rstd * (dxh - jnp.mean(dxh, axis=-1, keepdims=True) - xhat * jnp.mean(dxh * xhat, axis=-1, keepdims=True))
        dcb_ref[...] += _fold8(dhc)
        dhc_ref[...] = dhc

    return pl.pallas_call(
        body, name="conf_bwd1", grid=(n // tm,),
        in_specs=[BS((tm, B_W), lambda i: (i, Y_B // B_W)), BS((tm, B_W), lambda i: (i, P_ZB // B_W)),
                  BS((tm, B_W), lambda i: (i, 0)), vec, vec, BS((B_W, B_W), lambda i: (0, 0)), vec,
                  BS(memory_space=pl.ANY)],
        out_specs=[BS((tm, B_W), lambda i: (i, P_ZB // B_W)), BS((tm, B_W), lambda i: (i, 0)),
                   BS((B_W, B_W), lambda i: (0, 0)), acc, acc, acc, acc],
        out_shape=[S(dp.shape, dp.dtype), S((n, B_W), F32), S((B_W, B_W), F32)] + [S((SUB, B_W), F32)] * 4,
        input_output_aliases={7: 0},
        compiler_params=_cp("arbitrary"),
    )(dy, p, hc, lnw, lnb, pw, pwb, dp)


def _conf_bwd2(p, dhc, dp, cw, nseq, tm=256):
    n = p.shape[0]
    t = n // nseq
    nt = t // tm
    row = lambda b, i: b * nt + i
    prev = lambda b, i: jnp.maximum((b * t + i * tm) // HALO_B - 1, 0)
    nxt = lambda b, i: jnp.minimum((b * t + (i + 1) * tm) // HALO_B, n // HALO_B - 1)

    def body(ub_ref, uh_ref, dh_ref, dn_ref, cw_ref, dp_in_ref, dub_ref, dcw_ref, buf_ref, dbuf_ref, sh_ref, dsh_ref):
        i = pl.program_id(1)

        @pl.when((pl.program_id(0) == 0) & (i == 0))
        def _():
            dcw_ref[...] = jnp.zeros_like(dcw_ref)

        uh = uh_ref[...]
        buf_ref[0:HALO_B, :] = jnp.where(i > 0, uh[:, :B_W] * _sigmoid(uh[:, B_W:]), 0.0)
        buf_ref[HALO_B:, :] = ub_ref[:, :B_W] * _sigmoid(ub_ref[:, B_W:])
        dbuf_ref[0:tm, :] = dh_ref[...]
        dbuf_ref[tm:, :] = jnp.where(i < nt - 1, dn_ref[...], 0.0)
        _shifted_copies(buf_ref, sh_ref)
        _shifted_copies(dbuf_ref, dsh_ref)
        for c in range(B_W // LANES):
            cs, gs = slice(LANES * c, LANES * (c + 1)), slice(B_W + LANES * c, B_W + LANES * (c + 1))
            for r0 in range(0, tm, LANES):
                dhc = dh_ref[r0:r0 + LANES, cs]
                dhg = jnp.zeros((LANES, LANES), F32)
                for k in range(B_K):
                    dhg = dhg + cw_ref[k:k + 1, cs] * _rows_from(dbuf_ref, dsh_ref, r0 + B_K - 1 - k, LANES, cs)
                    dcw_ref[SUB * k:SUB * (k + 1), cs] += _fold8(
                        dhc * _rows_from(buf_ref, sh_ref, r0 + HALO_B - B_K + 1 + k, LANES, cs))
                a, sg = ub_ref[r0:r0 + LANES, cs], _sigmoid(ub_ref[r0:r0 + LANES, gs])
                dub_ref[r0:r0 + LANES, cs] = (dhg * sg).astype(BF16)
                dub_ref[r0:r0 + LANES, gs] = (dhg * a * sg * (1.0 - sg)).astype(BF16)

    return pl.pallas_call(
        body, name="conf_bwd2", grid=(nseq, nt),
        in_specs=[BS((tm, 2 * B_W), lambda b, i: (row(b, i), P_UB // (2 * B_W))),
                  BS((HALO_B, 2 * B_W), lambda b, i: (prev(b, i), P_UB // (2 * B_W))),
                  BS((tm, B_W), lambda b, i: (row(b, i), 0)), BS((HALO_B, B_W), lambda b, i: (nxt(b, i), 0)),
                  BS((HALO_B, B_W), lambda b, i: (0, 0)), BS(memory_space=pl.ANY)],
        out_specs=[BS((tm, 2 * B_W), lambda b, i: (row(b, i), P_UB // (2 * B_W))),
                   BS((SUB * B_K, B_W), lambda b, i: (0, 0))],
        out_shape=[S(dp.shape, dp.dtype), S((SUB * B_K, B_W), F32)], input_output_aliases={5: 0},
        scratch_shapes=[pltpu.VMEM((HALO_B + tm, B_W), F32)] * 2 + [pltpu.VMEM((SUB - 1, HALO_B + tm - SUB, B_W), F32)] * 2,
        compiler_params=_cp("arbitrary", "arbitrary"),
    )(p, p, dhc, dhc, cw, dp)


HALO_C = 8
QS = C_DH ** -0.5
NCB = 3 * C_HEADS
CB0 = P_QKV // LANES
ZC0 = P_ZC // LANES
GB, GG = 0, C_HEADS


def _softplus(z):
    return jnp.maximum(z, 0.0) + jnp.log(1.0 + jnp.exp(-jnp.abs(z)))


def _gdn_gates_fwd(p, alog_l, dtb_l, tm=256):
    n = p.shape[0]

    def body(ba_ref, al_ref, db_ref, o_ref):
        blk = ba_ref[...]
        lane = _lane(blk.shape)
        g = jnp.where((lane >= GG) & (lane < GG + C_HEADS), -jnp.exp(al_ref[...]) * _softplus(blk + db_ref[...]), 0.0)
        tri = (_subl((CHUNK, CHUNK)) >= _lane((CHUNK, CHUNK))).astype(F32)
        gc = jnp.concatenate([_dot(tri, g[CHUNK * c:CHUNK * (c + 1)], HI) for c in range(tm // CHUNK)], axis=0)
        o_ref[...] = jnp.where(lane < GG, _sigmoid(blk), gc)

    return pl.pallas_call(
        body, name="gdn_gates_fwd", grid=(n // tm,),
        in_specs=[BS((tm, LANES), lambda i: (i, P_BA // LANES)), BS((1, LANES), lambda i: (0, 0)), BS((1, LANES), lambda i: (0, 0))],
        out_specs=BS((tm, LANES), lambda i: (i, 0)), out_shape=S((n, LANES), F32),
        compiler_params=_cp("arbitrary"),
    )(p, alog_l, dtb_l)


def _gdn_pre_fwd(p, ccw, nseq, tm=256):
    n = p.shape[0]
    t = n // nseq
    nt = t // tm
    row = lambda b, i: b * nt + i
    halo = lambda b, i: jnp.maximum((b * t + i * tm) // HALO_C - 1, 0)

    def body(x_ref, xh_ref, w_ref, xc_ref, o_ref, buf_ref):
        buf_ref[0:HALO_C, :] = jnp.where(pl.program_id(1) > 0, xh_ref[...], 0.0)
        buf_ref[HALO_C:, :] = x_ref[...]
        for c in range(NCB):
            cs = slice(LANES * c, LANES * (c + 1))
            xc = jnp.zeros((tm, LANES), F32)
            for k in range(C_K):
                xc = xc + w_ref[k:k + 1, cs] * buf_ref[pl.ds(HALO_C - C_K + 1 + k, tm), cs]
            xc_ref[:, cs] = xc
            act = xc * _sigmoid(xc)
            if c < 2 * C_HEADS:
                act = act * (lax.rsqrt(jnp.sum(act * act, axis=-1, keepdims=True) + EPS) * (QS if c < C_HEADS else 1.0))
            o_ref[:, cs] = act

    wide = 3 * C_W
    return pl.pallas_call(
        body, name="gdn_pre_fwd", grid=(nseq, nt),
        in_specs=[BS((tm, wide), lambda b, i: (row(b, i), P_QKV // wide)), BS((HALO_C, wide), lambda b, i: (halo(b, i), P_QKV // wide)),
                  BS((SUB, wide), lambda b, i: (0, 0))],
        out_specs=[BS((tm, wide), lambda b, i: (row(b, i), 0))] * 2,
        out_shape=[S((n, wide), F32)] * 2,
        scratch_shapes=[pltpu.VMEM((HALO_C + tm, wide), F32)],
        compiler_params=_cp("arbitrary", "arbitrary"),
    )(p, p, ccw)


def _chunk_common(q, k, gt, gtt, h):
    beta = _col(gt, GB + h)
    gc = _col(gt, GG + h)
    gcr = gtt[GG + h:GG + h + 1, :]
    ii, jj = _subl((CHUNK, CHUNK)), _lane((CHUNK, CHUNK))
    incl, strict = ii >= jj, ii > jj
    dec = jnp.exp(jnp.where(incl, gc - gcr, -jnp.inf))
    kb = k * beta
    kbf = k.astype(BF16)
    a = jnp.where(strict, _dot_nt(kb.astype(BF16), kbf) * dec, 0.0)
    mq = jnp.where(incl, _dot_nt(q.astype(BF16), kbf) * dec, 0.0)
    glast = jnp.sum(jnp.where(_subl(gc.shape) == CHUNK - 1, gc, 0.0), axis=0, keepdims=True)
    return beta, gc, incl, strict, dec, kb, a, mq, glast


def _split(x):
    hi = x.astype(BF16)
    return hi, (x - hi.astype(F32)).astype(BF16)


def _dot3(dot, a, b):
    (ah, al), (bh, bl) = a, b
    return dot(ah, bh) + (dot(ah, bl) + dot(al, bh))


def _unit_lower_inverses(mats):
    eye = (_subl(mats[0].shape) == _lane(mats[0].shape)).astype(F32)
    ms = [-a for a in mats]
    invs = [eye + m for m in ms]
    parts = [_split(m) for m in ms]
    for _ in range(5):
        ms = [_dot3(_dot, s, s) for s in parts]
        parts = [_split(m) for m in ms]
        invs = [inv + _dot3(_dot, _split(inv), s) for inv, s in zip(invs, parts)]
    return invs


def _gdn_chunk_fwd(qkv, gates, p, y, onw, nseq, tt=512):
    n = qkv.shape[0]
    t = n // nseq
    tt = min(tt, t)
    nt = t // tt
    nch = tt // CHUNK

    def body(q_ref, k_ref, v_ref, g_ref, zc_ref, onw_ref, y_in_ref, y_ref, o_ref, u_ref, w_ref, t_ref, ss_ref, s_scr):
        @pl.when(pl.program_id(1) == 0)
        def _():
            s_scr[...] = jnp.zeros_like(s_scr)

        def step(c, carry):
            rows = pl.ds(pl.multiple_of(c * CHUNK, CHUNK), CHUNK)
            gt = g_ref[rows, :]
            gtt = gt.T
            heads = range(C_HEADS)
            hs = [slice(C_DH * h, C_DH * (h + 1)) for h in heads]
            q, k, v = ([r[rows, hs[h]] for h in heads] for r in (q_ref, k_ref, v_ref))
            cm = [_chunk_common(q[h], k[h], gt, gtt, h) for h in heads]
            beta, gc, kb, mq, glast = ([m[i] for m in cm] for i in (0, 1, 5, 7, 8))
            tinv = _unit_lower_inverses([m[6] for m in cm])
            egc = [jnp.exp(g) for g in gc]
            sol = [_dot3(_dot, _split(tinv[h]), _split(jnp.concatenate([v[h] * beta[h], kb[h] * egc[h]], axis=1))) for h in heads]
            sv = [s_scr[h] for h in heads]
            sb = [s.astype(BF16) for s in sv]
            vnb = [(sol[h][:, :C_DH] - _dot(sol[h][:, C_DH:].astype(BF16), sb[h])).astype(BF16) for h in heads]
            o = [_dot((q[h] * egc[h]).astype(BF16), sb[h]) + _dot(mq[h].astype(BF16), vnb[h]) for h in heads]
            for h in heads:
                ss_ref[h, c] = sv[h]
                s_scr[h] = sv[h] * jnp.exp(glast[h]) + _dot_tn((k[h] * jnp.exp(glast[h] - gc[h])).astype(BF16), vnb[h])
            for h in heads:
                o_ref[rows, hs[h]] = o[h]
                u_ref[rows, hs[h]] = sol[h][:, :C_DH]
                w_ref[rows, hs[h]] = sol[h][:, C_DH:]
                t_ref[rows, hs[h]] = jnp.concatenate([tinv[h], jnp.zeros_like(tinv[h])], axis=1)
                zc = zc_ref[rows, hs[h]]
                r = lax.rsqrt(jnp.mean(o[h] * o[h], axis=-1, keepdims=True) + EPS)
                y_ref[rows, hs[h]] = (o[h] * r * onw_ref[...] * (zc * _sigmoid(zc))).astype(BF16)
            return carry

        lax.fori_loop(0, nch, step, 0)

    row = lambda b, i: b * nt + i
    wb = lambda col: BS((tt, C_W), lambda b, i: (row(b, i), col))
    return pl.pallas_call(
        body, name="gdn_chunk_fwd", grid=(nseq, nt),
        in_specs=[wb(0), wb(1), wb(2), BS((tt, LANES), lambda b, i: (row(b, i), 0)), wb(P_ZC // C_W),
                  BS((1, LANES), lambda b, i: (0, 0)), BS(memory_space=pl.ANY)],
        out_specs=[wb(Y_C // C_W), wb(0), wb(0), wb(0), wb(0),
                   BS((None, C_HEADS, nch, C_DH, C_DH), lambda b, i: (b, 0, i, 0, 0))],
        out_shape=[S(y.shape, y.dtype)] + [S((n, C_W), F32)] * 4 + [S((nseq, C_HEADS, t // CHUNK, C_DH, C_DH), F32)],
        input_output_aliases={6: 0},
        scratch_shapes=[pltpu.VMEM((C_HEADS, C_DH, C_DH), F32)],
        compiler_params=_cp("arbitrary", "arbitrary"),
    )(qkv, qkv, qkv, gates, p, onw, y)


def _gdn_chunk_bwd(qkv, gates, p, dy, dp, onw, o, u, w, tinv, ss, nseq, tt=256):
    n = qkv.shape[0]
    t = n // nseq
    tt = min(tt, t)
    nt = t // tt
    nch = tt // CHUNK

    def body(q_ref, k_ref, v_ref, g_ref, zc_ref, onw_ref, o_ref, dy_ref, u_ref, w_ref, t_ref, ss_ref, dp_in_ref,
             dzc_ref, dqkv_ref, dg_ref, donw_ref, ds_scr):
        @pl.when(pl.program_id(1) == 0)
        def _():
            ds_scr[...] = jnp.zeros_like(ds_scr)

        @pl.when((pl.program_id(0) == 0) & (pl.program_id(1) == 0))
        def _():
            donw_ref[...] = jnp.zeros_like(donw_ref)

        def rsum(x):
            return jnp.sum(x, axis=-1, keepdims=True)

        def step(ci, carry):
            c = nch - 1 - ci
            rows = pl.ds(pl.multiple_of(c * CHUNK, CHUNK), CHUNK)
            gt = g_ref[rows, :]
            gtt = gt.T
            live = [head(c, rows, gt, gtt, h) for h in range(C_HEADS)]
            while live:
                live = [g for g in live if next(g, False)]
            return carry

        def head(c, rows, gt, gtt, h):
            hs = slice(C_DH * h, C_DH * (h + 1))
            q, k, v = q_ref[rows, hs], k_ref[rows, hs], v_ref[rows, hs]
            zc, o, dy, u, w = zc_ref[rows, hs], o_ref[rows, hs], dy_ref[rows, hs], u_ref[rows, hs], w_ref[rows, hs]
            tm_ = t_ref[rows, hs][:, 0:CHUNK]
            sv, dsv = ss_ref[h, c], ds_scr[h]
            sb, dsb = sv.astype(BF16), dsv.astype(BF16)
            sg = _sigmoid(zc)
            r = lax.rsqrt(jnp.mean(o * o, axis=-1, keepdims=True) + EPS)
            on = o * r
            ow = onw_ref[...]
            dzc_ref[rows, hs] = (dy * on * ow * _dsilu(zc, sg)).astype(BF16)
            t1 = dy * zc * sg
            donw_ref[...] += _fold8(t1 * on)
            don = t1 * ow
            do = r * (don - on * jnp.mean(don * on, axis=-1, keepdims=True))
            dob = do.astype(BF16)
            yield True
            beta, gc, incl, strict, dec, kb, a, mq, glast = _chunk_common(q, k, gt, gtt, h)
            egc = jnp.exp(gc)
            gl = jnp.exp(glast)
            ekd = jnp.exp(glast - gc)
            wb = w.astype(BF16)
            vnb = (u - _dot(wb, sb)).astype(BF16)
            qg = q * egc
            yield True
            dvn = _dot_tn(mq.astype(BF16), dob) + _dot((k * ekd).astype(BF16), dsb)
            dvnb = dvn.astype(BF16)
            dqg = _dot_nt(dob, sb)
            yield True
            dmq = jnp.where(incl, _dot_nt(dob, vnb), 0.0)
            dkd = _dot_nt(vnb, dsb)
            dgl = jnp.sum(rsum(dsv * sv), axis=0, keepdims=True)
            dw = -_dot_nt(dvnb, sb)
            yield True
            ds_scr[h] = gl * dsv + _dot_tn(qg.astype(BF16), dob) - _dot_tn(wb, dvnb)
            db = _dot3(_dot_tn, _split(tm_), _split(jnp.concatenate([dvn, dw], axis=1)))
            dbv, dbk = db[:, :C_DH], db[:, C_DH:]
            yield True
            da = -jnp.where(strict, _dot3(_dot_nt, _split(dbv), _split(u)) + _dot3(_dot_nt, _split(dbk), _split(w)), 0.0)
            yield True
            e = da * a + dmq * mq
            dgc = rsum(e) - rsum(e.T)
            dgb, dhb, kbf = (da * dec).astype(BF16), (dmq * dec).astype(BF16), k.astype(BF16)
            dkb = _dot(dgb, kbf)
            tk = rsum(dbk * k)
            rk = rsum(dkd * k) * ekd
            dq = _dot(dhb, kbf) + egc * dqg
            dk = _dot_tn(dgb, kb.astype(BF16)) + _dot_tn(dhb, q.astype(BF16)) + beta * (egc * dbk + dkb) + ekd * dkd
            dbeta = rsum(dbv * v) + tk * egc + rsum(dkb * k)
            dgc = dgc + tk * beta * egc + egc * rsum(dqg * q) - rk
            dglast = jnp.sum(rk, axis=0, keepdims=True) + dgl * gl
            dgc = dgc + jnp.where(_subl(dgc.shape) == CHUNK - 1, dglast, 0.0)
            dqkv_ref[0, rows, hs] = dq
            dqkv_ref[1, rows, hs] = dk
            dqkv_ref[2, rows, hs] = beta * dbv
            lane = _lane((CHUNK, LANES))
            dg_ref[h, rows, :] = jnp.where(lane == 0, dbeta, jnp.where(lane == 1, dgc, 0.0))

        lax.fori_loop(0, nch, step, 0)

    row = lambda b, i: b * nt + nt - 1 - i
    wb = lambda col: BS((tt, C_W), lambda b, i: (row(b, i), col))
    return pl.pallas_call(
        body, name="gdn_chunk_bwd", grid=(nseq, nt),
        in_specs=[wb(0), wb(1), wb(2), BS((tt, LANES), lambda b, i: (row(b, i), 0)), wb(P_ZC // C_W),
                  BS((1, LANES), lambda b, i: (0, 0)), wb(0), wb(Y_C // C_W), wb(0), wb(0), wb(0),
                  BS((None, C_HEADS, nch, C_DH, C_DH), lambda b, i: (b, 0, nt - 1 - i, 0, 0)), BS(memory_space=pl.ANY)],
        out_specs=[wb(P_ZC // C_W), BS((3, tt, C_W), lambda b, i: (0, row(b, i), 0)),
                   BS((C_HEADS, tt, LANES), lambda b, i: (0, row(b, i), 0)), BS((SUB, LANES), lambda b, i: (0, 0))],
        out_shape=[S(dp.shape, dp.dtype), S((3, n, C_W), F32), S((C_HEADS, n, LANES), F32), S((SUB, LANES), F32)],
        input_output_aliases={12: 0},
        scratch_shapes=[pltpu.VMEM((C_HEADS, C_DH, C_DH), F32)],
        compiler_params=_cp("arbitrary", "arbitrary"),
    )(qkv, qkv, qkv, gates, p, onw, o, dy, u, w, tinv, ss, dp)


def _gdn_gates_bwd(dgate, p, alog_l, dtb_l, dp, tm=256):
    n = p.shape[0]
    acc = BS((SUB, LANES), lambda i: (0, 0))

    def body(dg_ref, ba_ref, al_ref, db_ref, dp_in_ref, dba_ref, dal_ref, ddb_ref):
        @pl.when(pl.program_id(0) == 0)
        def _():
            dal_ref[...] = jnp.zeros_like(dal_ref)
            ddb_ref[...] = jnp.zeros_like(ddb_ref)

        blk = ba_ref[...]
        lane = _lane(blk.shape)
        dbeta = jnp.zeros_like(blk)
        dgc = jnp.zeros_like(blk)
        for h in range(C_HEADS):
            dbeta = dbeta + jnp.where(lane == GB + h, _col(dg_ref[h], 0), 0.0)
            dgc = dgc + jnp.where(lane == GG + h, _col(dg_ref[h], 1), 0.0)
        tri = (_subl((CHUNK, CHUNK)) <= _lane((CHUNK, CHUNK))).astype(F32)
        dg = jnp.concatenate([_dot(tri, dgc[CHUNK * c:CHUNK * (c + 1)], HI) for c in range(tm // CHUNK)], axis=0)
        beta = _sigmoid(blk)
        z = blk + db_ref[...]
        ea = jnp.exp(al_ref[...])
        isg = (lane >= GG) & (lane < GG + C_HEADS)
        dz = jnp.where(isg, -dg * ea * _sigmoid(z), 0.0)
        dal_ref[...] += _fold8(jnp.where(isg, -dg * ea * _softplus(z), 0.0))
        ddb_ref[...] += _fold8(dz)
        out = jnp.where(lane < GG, dbeta * beta * (1.0 - beta), dz)
        dba_ref[...] = jnp.concatenate([out, jnp.zeros_like(out)], axis=1).astype(BF16)

    return pl.pallas_call(
        body, name="gdn_gates_bwd", grid=(n // tm,),
        in_specs=[BS((C_HEADS, tm, LANES), lambda i: (0, i, 0)), BS((tm, LANES), lambda i: (i, P_BA // LANES)),
                  BS((1, LANES), lambda i: (0, 0)), BS((1, LANES), lambda i: (0, 0)), BS(memory_space=pl.ANY)],
        out_specs=[BS((tm, 2 * LANES), lambda i: (i, P_BA // (2 * LANES))), acc, acc],
        out_shape=[S(dp.shape, dp.dtype), S((SUB, LANES), F32), S((SUB, LANES), F32)],
        input_output_aliases={4: 0},
        compiler_params=_cp("arbitrary"),
    )(dgate, p, alog_l, dtb_l, dp)


def _gdn_pre_bwd(p, dqkv, xc, dp, ccw, nseq, tm=256):
    n = p.shape[0]
    t = n // nseq
    nt = t // tm
    wide = 3 * C_W
    row = lambda b, i: b * nt + i
    prev = lambda b, i: jnp.maximum((b * t + i * tm) // HALO_C - 1, 0)
    nxt = lambda b, i: jnp.minimum((b * t + (i + 1) * tm) // HALO_C, n // HALO_C - 1)

    def d_conv_out(d, xc, part):
        sg = _sigmoid(xc)
        act = xc * sg
        if part < 2:
            cs = QS if part == 0 else 1.0
            rn = lax.rsqrt(jnp.sum(act * act, axis=-1, keepdims=True) + EPS)
            d = cs * rn * d - act * (cs * rn * rn * rn * jnp.sum(d * act, axis=-1, keepdims=True))
        return d * _dsilu(xc, sg)

    def body(x_ref, xh_ref, d_ref, dn_ref, xc_ref, xn_ref, w_ref, dp_in_ref, dx_ref, dw_ref, buf_ref, dbuf_ref):
        i = pl.program_id(1)

        @pl.when((pl.program_id(0) == 0) & (i == 0))
        def _():
            dw_ref[...] = jnp.zeros_like(dw_ref)

        buf_ref[0:HALO_C, :] = jnp.where(i > 0, xh_ref[...], 0.0)
        buf_ref[HALO_C:, :] = x_ref[...]
        for c in range(NCB):
            cs = slice(LANES * c, LANES * (c + 1))
            part, hd = divmod(c, C_HEADS)
            hs = slice(LANES * hd, LANES * (hd + 1))
            d = d_conv_out(d_ref[part, :, hs], xc_ref[:, cs], part)
            dbuf_ref[0:tm, cs] = d
            dbuf_ref[tm:, cs] = jnp.where(i < nt - 1, d_conv_out(dn_ref[part, :, hs], xn_ref[:, cs], part), 0.0)
            dx = jnp.zeros((tm, LANES), F32)
            for k in range(C_K):
                dx = dx + w_ref[k:k + 1, cs] * dbuf_ref[pl.ds(C_K - 1 - k, tm), cs]
                dw_ref[SUB * k:SUB * (k + 1), cs] += _fold8(d * buf_ref[pl.ds(HALO_C - C_K + 1 + k, tm), cs])
            dx_ref[:, cs] = dx.astype(BF16)

    return pl.pallas_call(
        body, name="gdn_pre_bwd", grid=(nseq, nt),
        in_specs=[BS((tm, wide), lambda b, i: (row(b, i), P_QKV // wide)), BS((HALO_C, wide), lambda b, i: (prev(b, i), P_QKV // wide)),
                  BS((3, tm, C_W), lambda b, i: (0, row(b, i), 0)), BS((3, HALO_C, C_W), lambda b, i: (0, nxt(b, i), 0)),
                  BS((tm, wide), lambda b, i: (row(b, i), 0)), BS((HALO_C, wide), lambda b, i: (nxt(b, i), 0)),
                  BS((SUB, wide), lambda b, i: (0, 0)), BS(memory_space=pl.ANY)],
        out_specs=[BS((tm, wide), lambda b, i: (row(b, i), P_QKV // wide)), BS((SUB * C_K, wide), lambda b, i: (0, 0))],
        out_shape=[S(dp.shape, dp.dtype), S((SUB * C_K, wide), F32)], input_output_aliases={7: 0},
        scratch_shapes=[pltpu.VMEM((HALO_C + tm, wide), F32)] * 2,
        compiler_params=_cp("arbitrary", "arbitrary"),
    )(p, p, dqkv, dqkv, xc, xc, ccw, dp)


ANY = BS(memory_space=pl.ANY)


def _my_pos():
    return lax.axis_index("x"), lax.axis_index("y"), lax.axis_index("c")


def _dev_index(dev):
    return 4 * dev[0] + 2 * dev[1] + dev[2]


def _all_gather(shards, after=None):
    nk = len(shards)

    tail = [] if after is None else [after]

    def body(*refs):
        ins, outs = refs[:nk], refs[nk + len(tail):2 * nk + len(tail)]
        send, recv, loc = refs[2 * nk + len(tail):]
        x, y, c = _my_pos()
        me, sib = (x, y, c), (x, y, 1 - c)
        chips = [(1 - x, y), (x, 1 - y), (1 - x, 1 - y)]

        def rows(t, dev):
            r = ins[t].shape[0]
            return outs[t].at[pl.ds(pl.multiple_of(_dev_index(dev) * r, SUB), r), :]

        def copy(t, k, block, to, src=None):
            return pltpu.make_async_remote_copy(
                src_ref=rows(t, block) if src is None else src, dst_ref=rows(t, block),
                send_sem=send.at[t, k], recv_sem=recv.at[t, k], device_id=to, device_id_type=MESH)

        mine = [pltpu.make_async_copy(ins[t], rows(t, me), loc.at[t]) for t in range(nk)]
        for cp in mine:
            cp.start()
        first = []
        for t in range(nk):
            first.append(copy(t, 0, me, sib, src=ins[t]))
            first += [copy(t, 1 + j, me, (*chip, c), src=ins[t]) for j, chip in enumerate(chips)]
        for cp in first:
            cp.start()
        passed = []
        for j, chip in enumerate(chips):
            for t in range(nk):
                copy(t, 1 + j, (*chip, c), me).wait_recv()
                cp = copy(t, 4 + j, (*chip, c), sib)
                cp.start()
                passed.append(cp)
        for t in range(nk):
            copy(t, 0, sib, me).wait_recv()
            for j, chip in enumerate(chips):
                copy(t, 4 + j, (*chip, 1 - c), me).wait_recv()
        for cp in first + passed:
            cp.wait_send()
        for cp in mine:
            cp.wait()

    return pl.pallas_call(
        body, name="all_gather", in_specs=[ANY] * (nk + len(tail)), out_specs=[ANY] * nk,
        out_shape=[S((N_DEV * a.shape[0], a.shape[1]), a.dtype) for a in shards],
        scratch_shapes=[pltpu.SemaphoreType.DMA((nk, 7)), pltpu.SemaphoreType.DMA((nk, 7)), pltpu.SemaphoreType.DMA((nk,))],
    )(*shards, *tail)


SEM = BS(memory_space=pltpu.SEMAPHORE)
HBM = BS(memory_space=pltpu.HBM)
EFFECT = pltpu.SideEffectType.DATAFLOW_SIDE_EFFECTING


def _peers(x, y, c):
    return [((1 - x) if k & 4 else x, (1 - y) if k & 2 else y, (1 - c) if k & 1 else c) for k in range(1, N_DEV)]


def _exchange_copy(kind, src, land, send, recv, t, k, peer, me, arriving):
    frm = peer if arriving else me
    if kind == "gather":
        r = src.shape[0]
        s_ref = src
        d_ref = land.at[pl.ds(pl.multiple_of(_dev_index(frm) * r, SUB), r), :]
    else:
        r = src.shape[0] // N_DEV
        s_ref = src.at[pl.ds(pl.multiple_of(_dev_index(peer) * r, SUB), r), :]
        d_ref = land.at[_dev_index(frm)]
    sem = t * (N_DEV - 1) + k
    return pltpu.make_async_remote_copy(src_ref=s_ref, dst_ref=d_ref, send_sem=send.at[sem], recv_sem=recv.at[sem],
                                        device_id=peer, device_id_type=MESH)


def _own_copy(kind, src, land, own, t, me):
    if kind == "gather":
        r = src.shape[0]
        return pltpu.make_async_copy(src, land.at[pl.ds(pl.multiple_of(_dev_index(me) * r, SUB), r), :], own.at[t])
    r = src.shape[0] // N_DEV
    return pltpu.make_async_copy(src.at[pl.ds(pl.multiple_of(_dev_index(me) * r, SUB), r), :], land.at[_dev_index(me)], own.at[t])


def _exchange_start(kind, srcs, after, name):
    nk = len(srcs)
    if kind == "gather":
        lands = [lax.empty((N_DEV * a.shape[0], a.shape[1]), a.dtype) for a in srcs]
    else:
        lands = [lax.empty((N_DEV, a.shape[0] // N_DEV, a.shape[1]), a.dtype) for a in srcs]

    def body(*refs):
        src, land = refs[:nk], refs[nk:2 * nk]
        send, recv, own = refs[2 * nk + 1], refs[2 * nk + 2], refs[2 * nk + 3]
        token = refs[-1]
        x, y, c = _my_pos()
        me = (x, y, c)
        for t in range(nk):
            _own_copy(kind, src[t], land[t], own, t, me).start()
            for k, peer in enumerate(_peers(x, y, c)):
                _exchange_copy(kind, src[t], land[t], send, recv, t, k, peer, me, False).start()
        token[...] = jnp.zeros_like(token)

    hbm = lambda a: pltpu.HBM(a.shape, a.dtype)
    out = pl.pallas_call(
        body, name=name,
        out_shape=(pltpu.SemaphoreType.DMA((nk * (N_DEV - 1),)), pltpu.SemaphoreType.DMA((nk * (N_DEV - 1),)),
                   pltpu.SemaphoreType.DMA((nk,)), *[hbm(a) for a in srcs], *[hbm(a) for a in lands], S((SUB, LANES), F32)),
        in_specs=[HBM] * (2 * nk) + [ANY],
        out_specs=(SEM, SEM, SEM, *[HBM] * (2 * nk), BS(memory_space=pltpu.VMEM)),
        input_output_aliases={i: 3 + i for i in range(2 * nk)},
        compiler_params=pltpu.CompilerParams(has_side_effects=EFFECT),
    )(*[pltpu.with_memory_space_constraint(a, pltpu.HBM) for a in (*srcs, *lands)], after)
    return dict(kind=kind, nk=nk, send=out[0], recv=out[1], own=out[2], srcs=out[3:3 + nk], lands=out[3 + nk:3 + 2 * nk],
                token=out[-1])


def _exchange_wait(ex, after, name):
    kind, nk = ex["kind"], ex["nk"]

    def body(*refs):
        src, land = refs[:nk], refs[nk:2 * nk]
        send, recv, own = refs[2 * nk], refs[2 * nk + 1], refs[2 * nk + 2]
        x, y, c = _my_pos()
        me = (x, y, c)
        for t in range(nk):
            _own_copy(kind, src[t], land[t], own, t, me).wait()
            for k, peer in enumerate(_peers(x, y, c)):
                _exchange_copy(kind, src[t], land[t], send, recv, t, k, peer, me, False).wait_send()
                _exchange_copy(kind, src[t], land[t], send, recv, t, k, peer, me, True).wait_recv()

    hbm = lambda a: pltpu.HBM(a.shape, a.dtype)
    out = pl.pallas_call(
        body, name=name,
        out_shape=(*[hbm(a) for a in ex["srcs"]], *[hbm(a) for a in ex["lands"]]),
        in_specs=[HBM] * (2 * nk) + [SEM, SEM, SEM, ANY], out_specs=tuple([HBM] * (2 * nk)),
        input_output_aliases={i: i for i in range(2 * nk)},
        compiler_params=pltpu.CompilerParams(has_side_effects=EFFECT),
    )(*ex["srcs"], *ex["lands"], ex["send"], ex["recv"], ex["own"], after)
    return list(out[nk:])


BLOCK_BYTES = 4 << 20


def _row_tile(rows, row_bytes, align):
    best = align
    for tr in range(align, rows + 1, align):
        if rows % tr == 0 and tr * row_bytes <= BLOCK_BYTES:
            best = tr
    return best


def _sum8(a):
    _, r, w = a.shape
    tr = _row_tile(r, N_DEV * w * a.dtype.itemsize, 32 // a.dtype.itemsize)

    def body(a_ref, o_ref):
        acc = a_ref[0].astype(F32)
        for d in range(1, N_DEV):
            acc = acc + a_ref[d].astype(F32)
        o_ref[...] = acc

    return pl.pallas_call(
        body, name="sum8", grid=(r // tr,), in_specs=[BS((N_DEV, tr, w), lambda i: (0, i, 0))],
        out_specs=BS((tr, w), lambda i: (i, 0)), out_shape=S((r, w), F32), compiler_params=_cp("arbitrary"),
    )(a)


def _adamw(w, g, m, v):
    r, c = w.shape
    tr = _row_tile(r, c * 4 * 2, SUB)

    def body(w_ref, g_ref, m_ref, v_ref, d_ref, mo_ref, vo_ref):
        d_ref[...], mo_ref[...], vo_ref[...] = _adam_update(w_ref[...], g_ref[...], m_ref[...], v_ref[...])

    blk = BS((tr, c), lambda i: (i, 0))
    return pl.pallas_call(
        body, name="adamw", grid=(r // tr,), in_specs=[blk] * 4, out_specs=[blk] * 3,
        out_shape=[S((r, c), F32)] * 3, compiler_params=_cp("arbitrary"),
    )(w, g, m, v)


def _sum8_t(a, tc=256):
    _, r, w = a.shape

    def body(a_ref, o_ref):
        acc = a_ref[0].astype(F32)
        for d in range(1, N_DEV):
            acc = acc + a_ref[d].astype(F32)
        o_ref[...] = acc.T

    return pl.pallas_call(
        body, name="sum8_t", grid=(w // tc,), in_specs=[BS((N_DEV, r, tc), lambda j: (0, 0, j))],
        out_specs=BS((tc, r), lambda j: (j, 0)), out_shape=S((w, r), F32), compiler_params=_cp("arbitrary"),
    )(a)


def _rows_view(a):
    nl, r, c = a.shape
    assert nl == 2
    return a.transpose(2, 0, 1).reshape(c, nl, r // LANES, LANES).transpose(0, 2, 1, 3).reshape(-1, LANES)


def _rows_view_back(a, shape):
    nl, r, c = shape
    return a.reshape(c, r // LANES, nl, LANES).transpose(0, 2, 1, 3).reshape(c, nl, r).transpose(1, 2, 0)


def _adamw_rows(w, g, m, v, tr=2048):
    n = w.shape[0]

    def body(w_ref, g_ref, m_ref, v_ref, d_ref, mo_ref, vo_ref):
        d_ref[...], mo_ref[...], vo_ref[...] = _adam_update(w_ref[...], g_ref[...], m_ref[...], v_ref[...])

    blk = BS((tr, LANES), lambda i: (i, 0))
    return pl.pallas_call(
        body, name="adamw_rows", grid=(pl.cdiv(n, tr),), in_specs=[blk] * 4, out_specs=[blk] * 3,
        out_shape=[S((n, LANES), F32)] * 3, compiler_params=_cp("arbitrary"),
    )(w, g, m, v)


def _adam_update(w, g, m, v):
    m2 = ADAM_B1 * m + (1.0 - ADAM_B1) * g
    v2 = ADAM_B2 * v + (1.0 - ADAM_B2) * (g * g)
    m_hat = m2 / (1.0 - ADAM_B1 ** ADAM_STEP)
    v_hat = v2 / (1.0 - ADAM_B2 ** ADAM_STEP)
    return -ADAM_LR * (m_hat / (jnp.sqrt(v_hat) + ADAM_EPS) + ADAM_WD * w), m2, v2


def _adamw_layer(w, g, m, v, l, prev):
    nl, r, c = w.shape
    tr = _row_tile(r, c * 4 * 2, SUB)

    def body(w_ref, g_ref, m_ref, v_ref, *refs):
        go_ref, d_ref, mo_ref, vo_ref = refs[-4:]
        gv = g_ref[...]
        go_ref[...] = gv
        d_ref[...], mo_ref[...], vo_ref[...] = _adam_update(w_ref[...], gv, m_ref[...], v_ref[...])

    slot = BS((None, tr, c), lambda i: (l, i, 0))
    keep = [] if prev is None else [ANY] * 4
    return pl.pallas_call(
        body, name="adamw_layer", grid=(r // tr,), in_specs=[slot, BS((tr, c), lambda i: (i, 0)), slot, slot] + keep,
        out_specs=[slot] * 4, out_shape=[S((nl, r, c), F32)] * 4,
        input_output_aliases={} if prev is None else {4 + i: i for i in range(4)},
        compiler_params=_cp("arbitrary"),
    )(w, g, m, v, *(prev or ()))


def _blob(arrays):
    flat = jnp.concatenate([a.reshape(-1) for a in arrays])
    rows = -(-flat.shape[0] // (SUB * LANES)) * SUB
    return jnp.pad(flat, (0, rows * LANES - flat.shape[0])).reshape(rows, LANES)


def _unblob(blob, shapes, lead=()):
    flat = blob.reshape(lead + (-1,))
    out, off = [], 0
    for s in shapes:
        size = math.prod(s)
        out.append(flat[..., off:off + size].reshape(lead + tuple(s)))
        off += size
    return out


def _lanes6(a):
    return jnp.zeros((1, LANES), F32).at[0, GG:GG + C_HEADS].set(a)


def _y_rows(w):
    return jnp.concatenate([w[0:A_W], w[A_W + B_W:], w[A_W:A_W + B_W]], axis=0)


def _y_rows_back(g):
    return jnp.concatenate([g[0:A_W], g[A_W + C_W:], g[A_W:A_W + C_W]], axis=0)


SMALL = ("norm_w", "q_norm_w", "k_norm_w", "sinks", "b_conv_b", "b_ln_w", "b_ln_b", "b_pw_b", "c_a_log", "c_dt_bias",
         "c_onorm_w", "b_conv_w", "c_conv_w")
ORDER = ("norm_w", "w_in", "q_norm_w", "k_norm_w", "sinks", "b_conv_w", "b_conv_b", "b_ln_w", "b_ln_b", "b_pw_w", "b_pw_b",
         "c_conv_w", "c_a_log", "c_dt_bias", "c_onorm_w", "w_out")


def kernel(x, positions, norm_w, w_in, q_norm_w, k_norm_w, sinks, b_conv_w, b_conv_b, b_ln_w, b_ln_b, b_pw_w, b_pw_b, c_conv_w, c_a_log, c_dt_bias, c_onorm_w, w_out, loss_target, m_norm_w, m_w_in, m_q_norm_w, m_k_norm_w, m_sinks, m_b_conv_w, m_b_conv_b, m_b_ln_w, m_b_ln_b, m_b_pw_w, m_b_pw_b, m_c_conv_w, m_c_a_log, m_c_dt_bias, m_c_onorm_w, m_w_out, v_norm_w, v_w_in, v_q_norm_w, v_k_norm_w, v_sinks, v_b_conv_w, v_b_conv_b, v_b_ln_w, v_b_ln_b, v_b_pw_w, v_b_pw_b, v_c_conv_w, v_c_a_log, v_c_dt_bias, v_c_onorm_w, v_w_out):
    W = dict(norm_w=norm_w, w_in=w_in, q_norm_w=q_norm_w, k_norm_w=k_norm_w, sinks=sinks, b_conv_w=b_conv_w, b_conv_b=b_conv_b,
             b_ln_w=b_ln_w, b_ln_b=b_ln_b, b_pw_w=b_pw_w, b_pw_b=b_pw_b, c_conv_w=c_conv_w, c_a_log=c_a_log,
             c_dt_bias=c_dt_bias, c_onorm_w=c_onorm_w, w_out=w_out)
    M = dict(norm_w=m_norm_w, w_in=m_w_in, q_norm_w=m_q_norm_w, k_norm_w=m_k_norm_w, sinks=m_sinks, b_conv_w=m_b_conv_w,
             b_conv_b=m_b_conv_b, b_ln_w=m_b_ln_w, b_ln_b=m_b_ln_b, b_pw_w=m_b_pw_w, b_pw_b=m_b_pw_b, c_conv_w=m_c_conv_w,
             c_a_log=m_c_a_log, c_dt_bias=m_c_dt_bias, c_onorm_w=m_c_onorm_w, w_out=m_w_out)
    V = dict(norm_w=v_norm_w, w_in=v_w_in, q_norm_w=v_q_norm_w, k_norm_w=v_k_norm_w, sinks=v_sinks, b_conv_w=v_b_conv_w,
             b_conv_b=v_b_conv_b, b_ln_w=v_b_ln_w, b_ln_b=v_b_ln_b, b_pw_w=v_b_pw_w, b_pw_b=v_b_pw_b, c_conv_w=v_c_conv_w,
             c_a_log=v_c_a_log, c_dt_bias=v_c_dt_bias, c_onorm_w=v_c_onorm_w, w_out=v_w_out)
    nseq, t, d = x.shape
    n = nseq * t
    tr = min(256, t)
    tmm = min(512, n)
    tmw = min(1024, n)
    tkk = min(2048, n)
    me = _dev_index(_my_pos())
    xs = [x.reshape(n, d)]
    tgt = loss_target.reshape(n, d)
    tabs = _rope_tables(positions.reshape(n))

    win_p = _pack_cols(w_in).astype(BF16)
    wout_b = w_out.astype(BF16)
    sharded_small = (b_pw_w, b_conv_w, c_conv_w)
    g_win0, g_small = _all_gather([win_p[0], _blob(sharded_small)])
    win = [g_win0]
    later = _exchange_start("gather", [win_p[1], wout_b[0], wout_b[1]], g_small, "gather_start")
    pw_all, cw_all, ccw_all = _unblob(g_small, [a.shape for a in sharded_small], lead=(N_DEV,))
    pw_all = pw_all.transpose(1, 0, 2, 3).reshape(DEPTH, B_W, B_W).astype(BF16)
    cw_all = cw_all.transpose(1, 2, 0, 3).reshape(DEPTH, B_K, B_W)
    ccw_all = ccw_all.transpose(1, 2, 0, 3).reshape(DEPTH, C_K, 3 * C_W)

    def layer_params(l):
        return dict(
            nw=norm_w[l][None], qw=jnp.tile(q_norm_w[l], 2)[None], kw=jnp.tile(k_norm_w[l], 2)[None], sinks=sinks[l],
            cw=jnp.pad(cw_all[l], ((0, HALO_B - B_K), (0, 0))), cb=b_conv_b[l][None], lnw=b_ln_w[l][None], lnb=b_ln_b[l][None],
            pw=pw_all[l], pwb=b_pw_b[l][None], ccw=jnp.pad(ccw_all[l], ((0, SUB - C_K), (0, 0))),
            alog=_lanes6(c_a_log[l]), dtb=_lanes6(c_dt_bias[l]), onw=c_onorm_w[l][None])

    saved = []
    for l in range(DEPTH):
        q = layer_params(l)
        nw = q["nw"] + later["token"][0:1, 0:1] if l == 0 else q["nw"]
        p, h = _inproj(xs[l], nw, win[l], tm=tmw)
        y, o_a, lse = _attn_fwd(p, tabs, q["qw"], q["kw"], q["sinks"], nseq)
        gates = _gdn_gates_fwd(p, q["alog"], q["dtb"], tm=tr)
        xc, qkv = _gdn_pre_fwd(p, q["ccw"], nseq, tm=tr)
        y, o_c, u, w, tinv, ss = _gdn_chunk_fwd(qkv, gates, p, y, q["onw"], nseq)
        y, hc = _conf_fwd(p, y, q["cw"], q["cb"], q["lnw"], q["lnb"], q["pw"], q["pwb"], nseq, tm=tr)
        saved.append(dict(q=q, p=p, h=h, y=y, o_a=o_a, lse=lse, gates=gates, xc=xc, qkv=qkv, o_c=o_c, u=u, w=w, tinv=tinv,
                          ss=ss, hc=hc))
        if l == 0:
            g_win1, g_wout0, g_wout1 = _exchange_wait(later, y, "gather_wait")
            win.append(g_win1)
            wout = [_y_rows(g_wout0), _y_rows(g_wout1)]
        if l + 1 < DEPTH:
            xs.append(_outproj(xs[l], y, wout[l], tm=tmw, tn=512))
        else:
            dxn, lsum = _outproj_loss(xs[l], y, wout[l], tgt, tm=tmw, tn=512)
    loss = lax.psum(jnp.sum(lsum) * (0.5 / d), ("x", "y", "c"))

    sent, smalls = [None] * DEPTH, [None] * DEPTH
    for l in reversed(range(DEPTH)):
        s = saved[l]
        q, p = s["q"], s["p"]
        dy = _matmul(dxn, wout[l], "nt", F32, tmw, 512, d, "outproj_bwd_dy")
        dwout = _y_rows_back(_matmul(s["y"], dxn, "tn", BF16, 1024, 1024, tkk, "outproj_bwd_dw"))
        dp, dkv, dqw, dkw, dsk = _attn_bwd(p, dy, s["o_a"], s["lse"], tabs, q["qw"], q["kw"], q["sinks"], nseq)
        dp, dqkv, dgate, donw = _gdn_chunk_bwd(s["qkv"], s["gates"], p, dy, dp, q["onw"], s["o_c"], s["u"], s["w"],
                                               s["tinv"], s["ss"], nseq)
        dp, dccw = _gdn_pre_bwd(p, dqkv, s["xc"], dp, q["ccw"], nseq, tm=tr)
        early = P_K // 768
        dwin_a = _matmul(s["h"], dp, "tn", BF16, 1024, 768, tkk, "inproj_bwd_dw_a", b_cols=(0, early))
        sent_a = _exchange_start("scatter", [dwin_a, dwout], donw, "scatter_start_a%d" % l)
        dp = _put_cols(dp, dkv, P_K, sent_a["token"], tm=tmm)
        dp, dal, ddb = _gdn_gates_bwd(dgate, p, q["alog"], q["dtb"], dp, tm=tr)
        dp, dhc, dpw, dpwb, dlnw, dlnb, dcb = _conf_bwd1(p, dy, dp, s["hc"], q["lnw"], q["lnb"], q["pw"], q["pwb"], tm=tr)
        dp, dcw = _conf_bwd2(p, dhc, dp, q["cw"], nseq, tm=tr)
        dwin_b = _matmul(s["h"], dp, "tn", BF16, 1024, 768, tkk, "inproj_bwd_dw_b", b_cols=(early, P_W // 768 - early))
        sent_b = _exchange_start("scatter", [dwin_b, dpw], dpwb, "scatter_start_b%d" % l)
        sent[l] = (sent_a, sent_b)
        dxn, dnw = _inproj_bwd_dx(dp, win[l], xs[l], q["nw"] + sent_b["token"][0:1, 0:1], dxn, tm=tmw)
        halves = lambda a: a.sum(0)[:A_DH] + a.sum(0)[A_DH:]
        smalls[l] = dict(
            norm_w=dnw.sum(0), q_norm_w=halves(dqw), k_norm_w=halves(dkw), sinks=dsk.sum(0)[:A_HEADS], b_conv_b=dcb.sum(0),
            b_ln_w=dlnw.sum(0), b_ln_b=dlnb.sum(0), b_pw_b=dpwb.sum(0), c_a_log=dal.sum(0)[GG:GG + C_HEADS],
            c_dt_bias=ddb.sum(0)[GG:GG + C_HEADS], c_onorm_w=donw.sum(0),
            b_conv_w=dcw.reshape(B_K, SUB, B_W).sum(1), c_conv_w=dccw.reshape(C_K, SUB, 3 * C_W).sum(1))
    grad_x = dxn.reshape(nseq, t, d)

    G, delta, new_m, new_v = {}, {}, {}, {}
    big = ("w_in", "w_out", "b_pw_w")
    stacks = {k: None for k in big}
    after = dxn
    g_t = [None] * DEPTH
    for l in reversed(range(DEPTH)):
        r_win_a, r_wout = _exchange_wait(sent[l][0], after, "scatter_wait_a%d" % l)
        r_win_b, r_pw = _exchange_wait(sent[l][1], r_wout, "scatter_wait_b%d" % l)
        g_t[l] = jnp.concatenate([_sum8_t(r_win_a), _sum8_t(r_win_b)], axis=0).reshape(P_W, -1, LANES)
        for k, r in (("w_out", r_wout), ("b_pw_w", r_pw)):
            stacks[k] = _adamw_layer(W[k], _sum8(r), M[k], V[k], l, stacks[k])
        after = stacks["w_out"][1]
    g_in = jnp.stack(g_t, axis=2).reshape(-1, LANES)
    g_in = _unpack_cols(g_in, axis=0, each=g_in.shape[0] // P_W)
    rows = _adamw_rows(_rows_view(w_in), g_in, _rows_view(m_w_in), _rows_view(v_w_in))
    stacks["w_in"] = [_rows_view_back(a, w_in.shape) for a in (g_in, *rows)]
    for k in big:
        G[k], delta[k], new_m[k], new_v[k] = stacks[k]
    part = _blob([jnp.stack([smalls[l][k] for l in range(DEPTH)]) for k in SMALL])
    (tot,) = _all_gather([part], after=rows[0])
    tot = _sum8(tot.reshape(N_DEV, part.shape[0], LANES))
    full_shapes = [(DEPTH,) + smalls[0][k].shape for k in SMALL]
    for k, g in zip(SMALL, _unblob(tot, full_shapes)):
        G[k] = g
    G["b_conv_w"] = lax.dynamic_slice_in_dim(G["b_conv_w"], me * (B_W // N_DEV), B_W // N_DEV, axis=2)
    G["c_conv_w"] = lax.dynamic_slice_in_dim(G["c_conv_w"], me * (3 * C_W // N_DEV), 3 * C_W // N_DEV, axis=2)
    dl, mo, vo = _adamw(*[_blob([src[k] for k in SMALL]) for src in (W, G, M, V)])
    shapes = [W[k].shape for k in SMALL]
    for k, a, b, c in zip(SMALL, _unblob(dl, shapes), _unblob(mo, shapes), _unblob(vo, shapes)):
        delta[k], new_m[k], new_v[k] = a, b, c
    return (loss, grad_x, *[G[k] for k in ORDER], *[delta[k] for k in ORDER], *[new_m[k] for k in ORDER],
            *[new_v[k] for k in ORDER])
```

```python
import functools
import math

import jax
import jax.numpy as jnp
from jax import lax
from jax.experimental import pallas as pl
from jax.experimental.pallas import tpu as pltpu

F32 = jnp.float32
BF16 = jnp.bfloat16
HI = lax.Precision.HIGHEST
MESH = pl.DeviceIdType.MESH
S = jax.ShapeDtypeStruct
BS = pl.BlockSpec

N_DEV = 8
DEPTH = 2
D_MODEL = 2048
A_HEADS, A_KV, A_DH, A_W, A_KVW = 12, 4, 64, 768, 256
ROT = 16
THETA = 500000.0
ABLK = 128
B_W, B_K = 512, 31
C_HEADS, C_DH, C_W, C_K, CHUNK = 6, 128, 768, 4, 64
EPS = 1e-6
IN_COLS = 6668
P_Q, P_ZA, P_ZC, P_QKV, P_K, P_V, P_UB, P_ZB, P_BA, P_W = 0, 768, 1536, 2304, 4608, 4864, 5120, 6144, 6656, 6912
Y_A, Y_C, Y_B = 0, 768, 1536
LANES = 128
SUB = 8

ADAM_LR, ADAM_B1, ADAM_B2, ADAM_EPS, ADAM_WD, ADAM_STEP = 0.001, 0.9, 0.999, 1e-08, 0.01, 10


def _cp(*sem, vmem=None):
    kw = {}
    if sem:
        kw["dimension_semantics"] = sem
    if vmem:
        kw["vmem_limit_bytes"] = vmem
    return pltpu.CompilerParams(**kw)


def _pack_cols(w):
    z = jnp.zeros(w.shape[:-1] + (P_W - IN_COLS,), w.dtype)
    return jnp.concatenate([w[..., 0:768], w[..., 1280:2048], w[..., 5900:6668], w[..., 3584:5888],
                            w[..., 768:1024], w[..., 1024:1280], w[..., 2048:3072], w[..., 3072:3584],
                            w[..., 5888:5900], z], axis=-1)


def _unpack_cols(g, axis=-1, each=1):
    parts = ((P_Q, 768), (P_K, 256), (P_V, 256), (P_ZA, 768), (P_UB, 1024), (P_ZB, 512), (P_QKV, 2304), (P_BA, 12), (P_ZC, 768))
    return jnp.concatenate([lax.slice_in_dim(g, each * o, each * (o + n), axis=axis) for o, n in parts], axis=axis)


def _sigmoid(x):
    return 0.5 * jnp.tanh(0.5 * x) + 0.5


def _dsilu(x, sg):
    return sg * (1.0 + x * (1.0 - sg))


def _fold8(x):
    r, c = x.shape
    return x.reshape(r // SUB, SUB, c).sum(axis=0)


def _dot(a, b, prec=None):
    return jnp.dot(a, b, preferred_element_type=F32, precision=prec)


def _dot_nt(a, b, prec=None):
    return lax.dot_general(a, b, (((1,), (1,)), ((), ())), preferred_element_type=F32, precision=prec)


def _dot_tn(a, b, prec=None):
    return lax.dot_general(a, b, (((0,), (0,)), ((), ())), preferred_element_type=F32, precision=prec)


def _lane(shape):
    return lax.broadcasted_iota(jnp.int32, shape, 1)


def _subl(shape):
    return lax.broadcasted_iota(jnp.int32, shape, 0)


def _col(x, j):
    return jnp.sum(jnp.where(_lane(x.shape) == j, x, 0.0), axis=-1, keepdims=True)


def _inproj(x, nw, w, tm=512, tn=768):
    n, d = x.shape
    pw = w.shape[1]

    def body(x_ref, nw_ref, w_ref, p_ref, h_ref):
        @pl.when(pl.program_id(1) == 0)
        def _():
            xv = x_ref[...]
            r = lax.rsqrt(jnp.mean(xv * xv, axis=-1, keepdims=True) + EPS)
            h_ref[...] = (xv * r * nw_ref[...]).astype(BF16)

        p_ref[...] = _dot(h_ref[...], w_ref[...])

    return pl.pallas_call(
        body, name="inproj", grid=(n // tm, pw // tn),
        in_specs=[BS((tm, d), lambda i, j: (i, 0)), BS((1, d), lambda i, j: (0, 0)), BS((d, tn), lambda i, j: (0, j))],
        out_specs=[BS((tm, tn), lambda i, j: (i, j)), BS((tm, d), lambda i, j: (i, 0))],
        out_shape=[S((n, pw), F32), S((n, d), BF16)],
        compiler_params=_cp("arbitrary", "arbitrary"),
    )(x, nw, w)


def _outproj(x, y, w, tm=512, tn=1024):
    n, d = x.shape
    k = y.shape[1]

    def body(x_ref, y_ref, w_ref, o_ref):
        o_ref[...] = x_ref[...] + _dot(y_ref[...], w_ref[...])

    return pl.pallas_call(
        body, name="outproj", grid=(n // tm, d // tn),
        in_specs=[BS((tm, tn), lambda i, j: (i, j)), BS((tm, k), lambda i, j: (i, 0)), BS((k, tn), lambda i, j: (0, j))],
        out_specs=BS((tm, tn), lambda i, j: (i, j)),
        out_shape=S((n, d), F32),
        compiler_params=_cp("arbitrary", "arbitrary"),
    )(x, y, w)


def _outproj_loss(x, y, w, tgt, tm=512, tn=1024):
    n, d = x.shape
    k = y.shape[1]

    def body(x_ref, y_ref, w_ref, t_ref, g_ref, l_ref):
        @pl.when((pl.program_id(0) == 0) & (pl.program_id(1) == 0))
        def _():
            l_ref[...] = jnp.zeros_like(l_ref)

        diff = x_ref[...] + _dot(y_ref[...], w_ref[...]) - t_ref[...]
        g_ref[...] = diff * (1.0 / d)
        f = _fold8(diff * diff)
        acc = f[:, 0:LANES]
        for c in range(1, tn // LANES):
            acc = acc + f[:, c * LANES:(c + 1) * LANES]
        l_ref[...] += acc

    return pl.pallas_call(
        body, name="outproj_loss", grid=(n // tm, d // tn),
        in_specs=[BS((tm, tn), lambda i, j: (i, j)), BS((tm, k), lambda i, j: (i, 0)), BS((k, tn), lambda i, j: (0, j)),
                  BS((tm, tn), lambda i, j: (i, j))],
        out_specs=[BS((tm, tn), lambda i, j: (i, j)), BS((SUB, LANES), lambda i, j: (0, 0))],
        out_shape=[S((n, d), F32), S((SUB, LANES), F32)],
        compiler_params=_cp("arbitrary", "arbitrary"),
    )(x, y, w, tgt)


def _matmul(a, b, mode, out_dtype, tm, tn, tk, name, b_cols=None):
    if mode == "nn":
        (m, kk), nn = a.shape, b.shape[1]
        a_spec, b_spec = BS((tm, tk), lambda i, j, k: (i, k)), BS((tk, tn), lambda i, j, k: (k, j))
        dot = _dot
    elif mode == "nt":
        (m, kk), nn = a.shape, b.shape[0]
        a_spec, b_spec = BS((tm, tk), lambda i, j, k: (i, k)), BS((tn, tk), lambda i, j, k: (j, k))
        dot = _dot_nt
    else:
        j0, nj = b_cols or (0, b.shape[1] // tn)
        (kk, m), nn = a.shape, nj * tn
        a_spec, b_spec = BS((tk, tm), lambda i, j, k: (k, i)), BS((tk, tn), lambda i, j, k: (k, j0 + j))
        dot = _dot_tn
    nk = kk // tk

    def body(a_ref, b_ref, o_ref, acc_ref):
        kid = pl.program_id(2)

        @pl.when(kid == 0)
        def _():
            acc_ref[...] = jnp.zeros_like(acc_ref)

        acc_ref[...] += dot(a_ref[...].astype(BF16), b_ref[...].astype(BF16))

        @pl.when(kid == nk - 1)
        def _():
            o_ref[...] = acc_ref[...].astype(out_dtype)

    return pl.pallas_call(
        body, name=name, grid=(m // tm, nn // tn, nk),
        in_specs=[a_spec, b_spec], out_specs=BS((tm, tn), lambda i, j, k: (i, j)),
        out_shape=S((m, nn), out_dtype), scratch_shapes=[pltpu.VMEM((tm, tn), F32)],
        compiler_params=_cp("arbitrary", "arbitrary", "arbitrary"),
    )(a, b)


SLAB = 16


def _inproj_bwd_dx(dp, w, x, nw, dres, tm=512, tk=768):
    n, d = x.shape
    nk = dp.shape[1] // tk

    def body(dp_ref, w_ref, x_ref, nw_ref, dr_ref, dx_ref, dnw_ref, acc_ref):
        kid = pl.program_id(1)

        @pl.when((pl.program_id(0) == 0) & (kid == 0))
        def _():
            dnw_ref[...] = jnp.zeros_like(dnw_ref)

        @pl.when(kid == 0)
        def _():
            acc_ref[...] = jnp.zeros_like(acc_ref)

        acc_ref[...] += _dot_nt(dp_ref[...], w_ref[...])

        @pl.when(kid == nk - 1)
        def _():
            def slab(i, carry):
                rows = pl.ds(pl.multiple_of(i * SLAB, SLAB), SLAB)
                dh, xv = acc_ref[rows, :], x_ref[rows, :]
                r = lax.rsqrt(jnp.mean(xv * xv, axis=-1, keepdims=True) + EPS)
                dnw_ref[...] += _fold8(dh * xv * r)
                g = dh * nw_ref[...]
                mm = jnp.mean(g * xv, axis=-1, keepdims=True)
                dx_ref[rows, :] = dr_ref[rows, :] + r * g - xv * (r * r * r * mm)
                return carry

            lax.fori_loop(0, tm // SLAB, slab, 0)

    return pl.pallas_call(
        body, name="inproj_bwd_dx", grid=(n // tm, nk),
        in_specs=[BS((tm, tk), lambda i, k: (i, k)), BS((d, tk), lambda i, k: (0, k)), BS((tm, d), lambda i, k: (i, 0)),
                  BS((1, d), lambda i, k: (0, 0)), BS((tm, d), lambda i, k: (i, 0))],
        out_specs=[BS((tm, d), lambda i, k: (i, 0)), BS((SUB, d), lambda i, k: (0, 0))],
        out_shape=[S((n, d), F32), S((SUB, d), F32)],
        scratch_shapes=[pltpu.VMEM((tm, d), F32)],
        compiler_params=_cp("arbitrary", "arbitrary"),
    )(dp, w, x, nw, dres)


def _rope_tables(pos):
    half = ROT // 2
    inv = THETA ** (-jnp.arange(0, ROT, 2, dtype=F32) / ROT)
    ang = pos.astype(F32)[:, None] * inv
    cos, sin = jnp.cos(ang), jnp.sin(ang)
    n = pos.shape[0]
    one = jnp.ones((n, A_DH - ROT), F32)
    zero = jnp.zeros((n, A_DH - ROT), F32)
    zh = jnp.zeros((n, half), F32)
    c = jnp.concatenate([cos, cos, one], axis=1)
    s1 = jnp.concatenate([-sin, zh, zero], axis=1)
    s2 = jnp.concatenate([zh, sin, zero], axis=1)
    return tuple(jnp.concatenate([t, t], axis=1) for t in (c, s1, s2))


def _half_stat(t):
    lo = _lane(t.shape) < A_DH
    s_lo = jnp.sum(jnp.where(lo, t, 0.0), axis=-1, keepdims=True)
    s_hi = jnp.sum(jnp.where(lo, 0.0, t), axis=-1, keepdims=True)
    return jnp.where(lo, s_lo, s_hi)


def _normrope(x, w, c, s1, s2):
    r = lax.rsqrt(_half_stat(x * x) * (1.0 / A_DH) + EPS)
    xn = x * r * w
    return xn * c + pltpu.roll(xn, LANES - ROT // 2, 1) * s1 + pltpu.roll(xn, ROT // 2, 1) * s2, r


def _normrope_bwd(dy, x, r, w, c, s1, s2):
    dxn = dy * c + pltpu.roll(dy * s1, ROT // 2, 1) + pltpu.roll(dy * s2, LANES - ROT // 2, 1)
    g = dxn * w
    mm = _half_stat(g * x) * (1.0 / A_DH)
    return r * g - x * (r * r * r * mm), dxn * x * r


def _attn_mask(first):
    qi = _subl((ABLK, 2 * ABLK))
    kj = _lane((ABLK, 2 * ABLK))
    dist = qi + ABLK - kj
    return (dist >= 0) & (dist < ABLK) & (jnp.logical_not(first) | (kj >= ABLK))


def _keep_half(x, b):
    lo = _lane(x.shape) < A_DH
    return jnp.where(lo if b == 0 else jnp.logical_not(lo), x, jnp.zeros_like(x))


def _head_operand(x, j):
    a, b = j % 2, (j // 3) % 2
    return _keep_half(x if a == b else pltpu.roll(x, A_DH, 1), b)


def _head_result(x, j):
    a, b = j % 2, (j // 3) % 2
    return _keep_half(x if a == b else pltpu.roll(x, A_DH, 1), a)


def _attn_fwd(p, tabs, qw, kw, sinks, nseq):
    n = p.shape[0]
    nb = n // nseq // ABLK
    cur = lambda b, i: (b * nb + i, 0)
    prv = lambda b, i: (b * nb + jnp.maximum(i - 1, 0), 0)
    colblk = lambda f, w, off: (lambda b, i: (f(b, i)[0], off // w))

    def body(q_ref, za_ref, kc_ref, vc_ref, kp_ref, vp_ref, c_ref, s1_ref, s2_ref, cp_ref, s1p_ref, s2p_ref,
             qw_ref, kw_ref, sink_ref, y_ref, o_ref, lse_ref):
        first = pl.program_id(1) == 0
        tc = (c_ref[...], s1_ref[...], s2_ref[...])
        tp = (cp_ref[...], s1p_ref[...], s2p_ref[...])
        q, kc, kp = q_ref[...], kc_ref[...], kp_ref[...]
        qn = [_normrope(q[:, LANES * b:LANES * (b + 1)], qw_ref[...], *tc)[0] for b in range(A_W // LANES)]
        k2, v2 = [], []
        for b in range(A_KVW // LANES):
            sl = slice(LANES * b, LANES * (b + 1))
            k2.append(jnp.concatenate([_normrope(kp[:, sl], kw_ref[...], *tp)[0],
                                       _normrope(kc[:, sl], kw_ref[...], *tc)[0]], axis=0).astype(BF16))
            v2.append(jnp.concatenate([vp_ref[:, sl], vc_ref[:, sl]], axis=0).astype(BF16))
        valid = _attn_mask(first)
        heads = range(A_HEADS)
        qm = [_head_operand(qn[j // 2], j).astype(BF16) for j in heads]
        s = [jnp.where(valid, _dot_nt(qm[j], k2[j // 6]) * (A_DH ** -0.5), -jnp.inf) for j in heads]
        m = [jnp.maximum(jnp.max(s[j], axis=-1, keepdims=True), sink_ref[j]) for j in heads]
        e = [jnp.exp(s[j] - m[j]) for j in heads]
        den = [jnp.sum(e[j], axis=-1, keepdims=True) + jnp.exp(sink_ref[j] - m[j]) for j in heads]
        outs = [_head_result(_dot((e[j] * (1.0 / den[j])).astype(BF16), v2[j // 6]), j) for j in heads]
        lse = jnp.zeros((ABLK, LANES), F32)
        for j in heads:
            lse = jnp.where(_lane(lse.shape) == j, m[j] + jnp.log(den[j]), lse)
        o = jnp.concatenate([outs[2 * b] + outs[2 * b + 1] for b in range(A_W // LANES)], axis=1)
        za = za_ref[...]
        o_ref[...] = o
        lse_ref[...] = lse
        y_ref[...] = (o * (za * _sigmoid(za))).astype(BF16)

    tab_specs = [BS((ABLK, LANES), cur)] * 3 + [BS((ABLK, LANES), prv)] * 3
    return pl.pallas_call(
        body, name="attn_fwd", grid=(nseq, nb),
        in_specs=[BS((ABLK, A_W), colblk(cur, A_W, P_Q)), BS((ABLK, A_W), colblk(cur, A_W, P_ZA)),
                  BS((ABLK, A_KVW), colblk(cur, A_KVW, P_K)), BS((ABLK, A_KVW), colblk(cur, A_KVW, P_V)),
                  BS((ABLK, A_KVW), colblk(prv, A_KVW, P_K)), BS((ABLK, A_KVW), colblk(prv, A_KVW, P_V))]
        + tab_specs + [BS((1, LANES), lambda b, i: (0, 0))] * 2 + [BS(memory_space=pltpu.SMEM)],
        out_specs=[BS((ABLK, A_W), colblk(cur, A_W, Y_A)), BS((ABLK, A_W), cur), BS((ABLK, LANES), cur)],
        out_shape=[S((n, D_MODEL), BF16), S((n, A_W), F32), S((n, LANES), F32)],
        compiler_params=_cp("arbitrary", "arbitrary"),
    )(p, p, p, p, p, p, *tabs, *tabs, qw, kw, sinks)


def _attn_bwd(p, dy, o, lse, tabs, qw, kw, sinks, nseq):
    n = p.shape[0]
    nb = n // nseq // ABLK
    cur = lambda b, i: (b * nb + jnp.minimum(i, nb - 1), 0)
    prv = lambda b, i: (b * nb + jnp.maximum(i - 1, 0), 0)
    colblk = lambda f, w, off: (lambda b, i: (f(b, i)[0], off // w))

    def body(q_ref, za_ref, kc_ref, vc_ref, kp_ref, vp_ref, dy_ref, o_ref, lse_ref,
             c_ref, s1_ref, s2_ref, cp_ref, s1p_ref, s2p_ref, qw_ref, kw_ref, sink_ref,
             dqza_ref, dkv_ref, dqw_ref, dkw_ref, dsk_ref, tk_ref, tv_ref, ck_ref, cv_ref):
        i = pl.program_id(1)
        first = i == 0
        tc = (c_ref[...], s1_ref[...], s2_ref[...])
        tp = (cp_ref[...], s1p_ref[...], s2p_ref[...])
        nkb = A_KVW // LANES

        @pl.when((pl.program_id(0) == 0) & first)
        def _():
            dqw_ref[...] = jnp.zeros_like(dqw_ref)
            dkw_ref[...] = jnp.zeros_like(dkw_ref)
            dsk_ref[...] = jnp.zeros_like(dsk_ref)

        @pl.when(i < nb)
        def _():
            q, kc, kp = q_ref[...], kc_ref[...], kp_ref[...]
            qn, rq = [], []
            for b in range(A_W // LANES):
                a, r = _normrope(q[:, LANES * b:LANES * (b + 1)], qw_ref[...], *tc)
                qn.append(a)
                rq.append(r)
            k2, v2 = [], []
            for b in range(nkb):
                sl = slice(LANES * b, LANES * (b + 1))
                k2.append(jnp.concatenate([_normrope(kp[:, sl], kw_ref[...], *tp)[0],
                                           _normrope(kc[:, sl], kw_ref[...], *tc)[0]], axis=0).astype(BF16))
                v2.append(jnp.concatenate([vp_ref[:, sl], vc_ref[:, sl]], axis=0).astype(BF16))
            valid = _attn_mask(first)
            za, dy, o, lse = za_ref[...], dy_ref[...], o_ref[...], lse_ref[...]
            sg = _sigmoid(za)
            do = dy * za * sg
            dqza_ref[:, A_W:2 * A_W] = (dy * o * _dsilu(za, sg)).astype(BF16)
            heads = range(A_HEADS)
            blk = lambda x, b: x[:, LANES * b:LANES * (b + 1)]
            qm = [_head_operand(qn[j // 2], j).astype(BF16) for j in heads]
            lj = [_col(lse, j) for j in heads]
            pr = [jnp.exp(jnp.where(valid, _dot_nt(qm[j], k2[j // 6]) * (A_DH ** -0.5), -jnp.inf) - lj[j]) for j in heads]
            dom = [_head_operand(blk(do, j // 2), j).astype(BF16) for j in heads]
            delta = [jnp.sum(_keep_half(blk(do, j // 2) * blk(o, j // 2), j % 2), axis=-1, keepdims=True) for j in heads]
            ds = [(pr[j] * (_dot_nt(dom[j], v2[j // 6]) - delta[j]) * (A_DH ** -0.5)).astype(BF16) for j in heads]
            dqh = [_head_result(_dot(ds[j], k2[j // 6]), j) for j in heads]
            dkh = [_dot_tn(ds[j], qm[j]) for j in heads]
            dvh = [_dot_tn(pr[j].astype(BF16), dom[j]) for j in heads]
            per_blk = A_HEADS // nkb
            dks = [sum(dkh[per_blk * g + 1:per_blk * (g + 1)], dkh[per_blk * g]) for g in range(nkb)]
            dvs = [sum(dvh[per_blk * g + 1:per_blk * (g + 1)], dvh[per_blk * g]) for g in range(nkb)]
            dsk = jnp.zeros((ABLK, LANES), F32)
            for j in heads:
                dsk = dsk + jnp.where(_lane(dsk.shape) == j, -jnp.exp(sink_ref[j] - lj[j]) * delta[j], 0.0)
            dsk_ref[...] += _fold8(dsk)
            dqn = jnp.concatenate([dqh[2 * b] + dqh[2 * b + 1] for b in range(A_W // LANES)], axis=1)
            dqw = jnp.zeros((SUB, LANES), F32)
            dqo = []
            for b in range(A_W // LANES):
                sl = slice(LANES * b, LANES * (b + 1))
                dx, wt = _normrope_bwd(dqn[:, sl], q[:, sl], rq[b], qw_ref[...], *tc)
                dqo.append(dx)
                dqw = dqw + _fold8(wt)
            dqw_ref[...] += dqw
            dqza_ref[:, 0:A_W] = jnp.concatenate(dqo, axis=1).astype(BF16)
            tk_ref[...] = jnp.concatenate(dks, axis=1)
            tv_ref[...] = jnp.concatenate(dvs, axis=1)

        @pl.when(i == nb)
        def _():
            tk_ref[...] = jnp.zeros_like(tk_ref)
            tv_ref[...] = jnp.zeros_like(tv_ref)

        @pl.when(i > 0)
        def _():
            kp = kp_ref[...]
            dkn = ck_ref[...] + tk_ref[0:ABLK, :]
            dkw = jnp.zeros((SUB, LANES), F32)
            dko = []
            for b in range(nkb):
                sl = slice(LANES * b, LANES * (b + 1))
                r = _normrope(kp[:, sl], kw_ref[...], *tp)[1]
                dx, wt = _normrope_bwd(dkn[:, sl], kp[:, sl], r, kw_ref[...], *tp)
                dko.append(dx)
                dkw = dkw + _fold8(wt)
            dkw_ref[...] += dkw
            dkv_ref[:, 0:A_KVW] = jnp.concatenate(dko, axis=1).astype(BF16)
            dkv_ref[:, A_KVW:2 * A_KVW] = (cv_ref[...] + tv_ref[0:ABLK, :]).astype(BF16)

        ck_ref[...] = tk_ref[ABLK:2 * ABLK, :]
        cv_ref[...] = tv_ref[ABLK:2 * ABLK, :]

    tab_specs = [BS((ABLK, LANES), cur)] * 3 + [BS((ABLK, LANES), prv)] * 3
    acc = BS((SUB, LANES), lambda b, i: (0, 0))
    return pl.pallas_call(
        body, name="attn_bwd", grid=(nseq, nb + 1),
        in_specs=[BS((ABLK, A_W), colblk(cur, A_W, P_Q)), BS((ABLK, A_W), colblk(cur, A_W, P_ZA)),
                  BS((ABLK, A_KVW), colblk(cur, A_KVW, P_K)), BS((ABLK, A_KVW), colblk(cur, A_KVW, P_V)),
                  BS((ABLK, A_KVW), colblk(prv, A_KVW, P_K)), BS((ABLK, A_KVW), colblk(prv, A_KVW, P_V)),
                  BS((ABLK, A_W), colblk(cur, A_W, Y_A)), BS((ABLK, A_W), cur), BS((ABLK, LANES), cur)]
        + tab_specs + [BS((1, LANES), lambda b, i: (0, 0))] * 2 + [BS(memory_space=pltpu.SMEM)],
        out_specs=[BS((ABLK, 2 * A_W), cur), BS((ABLK, 2 * A_KVW), prv), acc, acc, acc],
        out_shape=[S((n, P_W), BF16), S((n, 2 * A_KVW), BF16)] + [S((SUB, LANES), F32)] * 3,
        scratch_shapes=[pltpu.VMEM((2 * ABLK, A_KVW), F32)] * 2 + [pltpu.VMEM((ABLK, A_KVW), F32)] * 2,
        compiler_params=_cp("arbitrary", "arbitrary"),
    )(p, p, p, p, p, p, dy, o, lse, *tabs, *tabs, qw, kw, sinks)


def _put_cols(dst, src, col_off, after, tm=512):
    n, w = src.shape

    def body(s_ref, d_in_ref, after_ref, d_ref):
        d_ref[...] = s_ref[...]

    return pl.pallas_call(
        body, name="put_cols", grid=(n // tm,),
        in_specs=[BS((tm, w), lambda i: (i, 0)), BS(memory_space=pl.ANY), BS(memory_space=pl.ANY)],
        out_specs=BS((tm, w), lambda i: (i, col_off // w)),
        out_shape=S(dst.shape, dst.dtype), input_output_aliases={1: 0},
        compiler_params=_cp("arbitrary"),
    )(src, dst, after)


HALO_B = 32


def _layernorm(hc, lnw, lnb):
    mu = jnp.mean(hc, axis=-1, keepdims=True)
    xc = hc - mu
    rstd = lax.rsqrt(jnp.mean(xc * xc, axis=-1, keepdims=True) + EPS)
    xhat = xc * rstd
    return xhat, rstd, xhat * lnw + lnb


def _shifted_copies(buf_ref, sh_ref):
    rows = sh_ref.shape[1]
    for b in range(1, SUB):
        sh_ref[b - 1] = buf_ref[pl.ds(b, rows), :]


def _rows_from(buf_ref, sh_ref, off, rows, cols=slice(None)):
    a, b = divmod(off, SUB)
    if b == 0:
        return buf_ref[pl.ds(SUB * a, rows), cols]
    return sh_ref[b - 1, pl.ds(SUB * a, rows), cols]


def _conf_fwd(p, y, cw, cb, lnw, lnb, pw, pwb, nseq, tm=256):
    n = p.shape[0]
    t = n // nseq
    nt = t // tm
    row = lambda b, i: b * nt + i
    halo = lambda b, i: jnp.maximum((b * t + i * tm) // HALO_B - 1, 0)
    vec = BS((1, B_W), lambda b, i: (0, 0))

    def body(ub_ref, uh_ref, zb_ref, cw_ref, cb_ref, lnw_ref, lnb_ref, pw_ref, pwb_ref, y_in_ref, y_ref, hc_ref, buf_ref, sh_ref):
        ub, uh = ub_ref[...], uh_ref[...]
        hh = uh[:, :B_W] * _sigmoid(uh[:, B_W:])
        buf_ref[0:HALO_B, :] = jnp.where(pl.program_id(1) > 0, hh, 0.0)
        buf_ref[HALO_B:, :] = ub[:, :B_W] * _sigmoid(ub[:, B_W:])
        _shifted_copies(buf_ref, sh_ref)
        hc = jnp.zeros((tm, B_W), F32) + cb_ref[...]
        for k in range(B_K):
            hc = hc + cw_ref[k:k + 1, :] * _rows_from(buf_ref, sh_ref, HALO_B - B_K + 1 + k, tm)
        hc_ref[...] = hc
        ln = _layernorm(hc, lnw_ref[...], lnb_ref[...])[2]
        sw = ln * _sigmoid(ln)
        ob = _dot(sw.astype(BF16), pw_ref[...]) + pwb_ref[...]
        zb = zb_ref[...]
        y_ref[...] = (ob * (zb * _sigmoid(zb))).astype(BF16)

    return pl.pallas_call(
        body, name="conf_fwd", grid=(nseq, nt),
        in_specs=[BS((tm, 2 * B_W), lambda b, i: (row(b, i), P_UB // (2 * B_W))),
                  BS((HALO_B, 2 * B_W), lambda b, i: (halo(b, i), P_UB // (2 * B_W))),
                  BS((tm, B_W), lambda b, i: (row(b, i), P_ZB // B_W)),
                  BS((HALO_B, B_W), lambda b, i: (0, 0)), vec, vec, vec, BS((B_W, B_W), lambda b, i: (0, 0)), vec,
                  BS(memory_space=pl.ANY)],
        out_specs=[BS((tm, B_W), lambda b, i: (row(b, i), Y_B // B_W)), BS((tm, B_W), lambda b, i: (row(b, i), 0))],
        out_shape=[S(y.shape, y.dtype), S((n, B_W), F32)], input_output_aliases={9: 0},
        scratch_shapes=[pltpu.VMEM((HALO_B + tm, B_W), F32), pltpu.VMEM((SUB - 1, HALO_B + tm - SUB, B_W), F32)],
        compiler_params=_cp("arbitrary", "arbitrary"),
    )(p, p, p, cw, cb, lnw, lnb, pw, pwb, y)


def _conf_bwd1(p, dy, dp, hc, lnw, lnb, pw, pwb, tm=256):
    n = p.shape[0]
    vec = BS((1, B_W), lambda i: (0, 0))
    acc = BS((SUB, B_W), lambda i: (0, 0))

    def body(dy_ref, zb_ref, hc_ref, lnw_ref, lnb_ref, pw_ref, pwb_ref, dp_in_ref,
             dzb_ref, dhc_ref, dpw_ref, dpwb_ref, dlnw_ref, dlnb_ref, dcb_ref):
        @pl.when(pl.program_id(0) == 0)
        def _():
            for r in (dpw_ref, dpwb_ref, dlnw_ref, dlnb_ref, dcb_ref):
                r[...] = jnp.zeros_like(r)

        xhat, rstd, ln = _layernorm(hc_ref[...], lnw_ref[...], lnb_ref[...])
        sgl = _sigmoid(ln)
        sw = (ln * sgl).astype(BF16)
        ob = _dot(sw, pw_ref[...]) + pwb_ref[...]
        dy, zb = dy_ref[...], zb_ref[...]
        sgz = _sigmoid(zb)
        dzb_ref[...] = (dy * ob * _dsilu(zb, sgz)).astype(BF16)
        dob = dy * zb * sgz
        dobb = dob.astype(BF16)
        dpwb_ref[...] += _fold8(dob)
        dpw_ref[...] += _dot_tn(sw, dobb)
        dln = _dot_nt(dobb, pw_ref[...]) * _dsilu(ln, sgl)
        dlnw_ref[...] += _fold8(dln * xhat)
        dlnb_ref[...] += _fold8(dln)
        dxh = dln * lnw_ref[...]
        dhc = rstd * (dxh - jnp.mean(dxh, axis=-1, keepdims=True) - xhat * jnp.mean(dxh * xhat, axis=-1, keepdims=True))
        dcb_ref[...] += _fold8(dhc)
        dhc_ref[...] = dhc

    return pl.pallas_call(
        body, name="conf_bwd1", grid=(n // tm,),
        in_specs=[BS((tm, B_W), lambda i: (i, Y_B // B_W)), BS((tm, B_W), lambda i: (i, P_ZB // B_W)),
                  BS((tm, B_W), lambda i: (i, 0)), vec, vec, BS((B_W, B_W), lambda i: (0, 0)), vec,
                  BS(memory_space=pl.ANY)],
        out_specs=[BS((tm, B_W), lambda i: (i, P_ZB // B_W)), BS((tm, B_W), lambda i: (i, 0)),
                   BS((B_W, B_W), lambda i: (0, 0)), acc, acc, acc, acc],
        out_shape=[S(dp.shape, dp.dtype), S((n, B_W), F32), S((B_W, B_W), F32)] + [S((SUB, B_W), F32)] * 4,
        input_output_aliases={7: 0},
        compiler_params=_cp("arbitrary"),
    )(dy, p, hc, lnw, lnb, pw, pwb, dp)


def _conf_bwd2(p, dhc, dp, cw, nseq, tm=256):
    n = p.shape[0]
    t = n // nseq
    nt = t // tm
    row = lambda b, i: b * nt + i
    prev = lambda b, i: jnp.maximum((b * t + i * tm) // HALO_B - 1, 0)
    nxt = lambda b, i: jnp.minimum((b * t + (i + 1) * tm) // HALO_B, n // HALO_B - 1)

    def body(ub_ref, uh_ref, dh_ref, dn_ref, cw_ref, dp_in_ref, dub_ref, dcw_ref, buf_ref, dbuf_ref, sh_ref, dsh_ref):
        i = pl.program_id(1)

        @pl.when((pl.program_id(0) == 0) & (i == 0))
        def _():
            dcw_ref[...] = jnp.zeros_like(dcw_ref)

        uh = uh_ref[...]
        buf_ref[0:HALO_B, :] = jnp.where(i > 0, uh[:, :B_W] * _sigmoid(uh[:, B_W:]), 0.0)
        buf_ref[HALO_B:, :] = ub_ref[:, :B_W] * _sigmoid(ub_ref[:, B_W:])
        dbuf_ref[0:tm, :] = dh_ref[...]
        dbuf_ref[tm:, :] = jnp.where(i < nt - 1, dn_ref[...], 0.0)
        _shifted_copies(buf_ref, sh_ref)
        _shifted_copies(dbuf_ref, dsh_ref)
        for c in range(B_W // LANES):
            cs, gs = slice(LANES * c, LANES * (c + 1)), slice(B_W + LANES * c, B_W + LANES * (c + 1))
            for r0 in range(0, tm, LANES):
                dhc = dh_ref[r0:r0 + LANES, cs]
                dhg = jnp.zeros((LANES, LANES), F32)
                for k in range(B_K):
                    dhg = dhg + cw_ref[k:k + 1, cs] * _rows_from(dbuf_ref, dsh_ref, r0 + B_K - 1 - k, LANES, cs)
                    dcw_ref[SUB * k:SUB * (k + 1), cs] += _fold8(
                        dhc * _rows_from(buf_ref, sh_ref, r0 + HALO_B - B_K + 1 + k, LANES, cs))
                a, sg = ub_ref[r0:r0 + LANES, cs], _sigmoid(ub_ref[r0:r0 + LANES, gs])
                dub_ref[r0:r0 + LANES, cs] = (dhg * sg).astype(BF16)
                dub_ref[r0:r0 + LANES, gs] = (dhg * a * sg * (1.0 - sg)).astype(BF16)

    return pl.pallas_call(
        body, name="conf_bwd2", grid=(nseq, nt),
        in_specs=[BS((tm, 2 * B_W), lambda b, i: (row(b, i), P_UB // (2 * B_W))),
                  BS((HALO_B, 2 * B_W), lambda b, i: (prev(b, i), P_UB // (2 * B_W))),
                  BS((tm, B_W), lambda b, i: (row(b, i), 0)), BS((HALO_B, B_W), lambda b, i: (nxt(b, i), 0)),
                  BS((HALO_B, B_W), lambda b, i: (0, 0)), BS(memory_space=pl.ANY)],
        out_specs=[BS((tm, 2 * B_W), lambda b, i: (row(b, i), P_UB // (2 * B_W))),
                   BS((SUB * B_K, B_W), lambda b, i: (0, 0))],
        out_shape=[S(dp.shape, dp.dtype), S((SUB * B_K, B_W), F32)], input_output_aliases={5: 0},
        scratch_shapes=[pltpu.VMEM((HALO_B + tm, B_W), F32)] * 2 + [pltpu.VMEM((SUB - 1, HALO_B + tm - SUB, B_W), F32)] * 2,
        compiler_params=_cp("arbitrary", "arbitrary"),
    )(p, p, dhc, dhc, cw, dp)


HALO_C = 8
QS = C_DH ** -0.5
NCB = 3 * C_HEADS
CB0 = P_QKV // LANES
ZC0 = P_ZC // LANES
GB, GG = 0, C_HEADS


def _softplus(z):
    return jnp.maximum(z, 0.0) + jnp.log(1.0 + jnp.exp(-jnp.abs(z)))


def _gdn_gates_fwd(p, alog_l, dtb_l, tm=256):
    n = p.shape[0]

    def body(ba_ref, al_ref, db_ref, o_ref):
        blk = ba_ref[...]
        lane = _lane(blk.shape)
        g = jnp.where((lane >= GG) & (lane < GG + C_HEADS), -jnp.exp(al_ref[...]) * _softplus(blk + db_ref[...]), 0.0)
        tri = (_subl((CHUNK, CHUNK)) >= _lane((CHUNK, CHUNK))).astype(F32)
        gc = jnp.concatenate([_dot(tri, g[CHUNK * c:CHUNK * (c + 1)], HI) for c in range(tm // CHUNK)], axis=0)
        o_ref[...] = jnp.where(lane < GG, _sigmoid(blk), gc)

    return pl.pallas_call(
        body, name="gdn_gates_fwd", grid=(n // tm,),
        in_specs=[BS((tm, LANES), lambda i: (i, P_BA // LANES)), BS((1, LANES), lambda i: (0, 0)), BS((1, LANES), lambda i: (0, 0))],
        out_specs=BS((tm, LANES), lambda i: (i, 0)), out_shape=S((n, LANES), F32),
        compiler_params=_cp("arbitrary"),
    )(p, alog_l, dtb_l)


def _gdn_pre_fwd(p, ccw, nseq, tm=256):
    n = p.shape[0]
    t = n // nseq
    nt = t // tm
    row = lambda b, i: b * nt + i
    halo = lambda b, i: jnp.maximum((b * t + i * tm) // HALO_C - 1, 0)

    def body(x_ref, xh_ref, w_ref, xc_ref, o_ref, buf_ref):
        buf_ref[0:HALO_C, :] = jnp.where(pl.program_id(1) > 0, xh_ref[...], 0.0)
        buf_ref[HALO_C:, :] = x_ref[...]
        for c in range(NCB):
            cs = slice(LANES * c, LANES * (c + 1))
            xc = jnp.zeros((tm, LANES), F32)
            for k in range(C_K):
                xc = xc + w_ref[k:k + 1, cs] * buf_ref[pl.ds(HALO_C - C_K + 1 + k, tm), cs]
            xc_ref[:, cs] = xc
            act = xc * _sigmoid(xc)
            if c < 2 * C_HEADS:
                act = act * (lax.rsqrt(jnp.sum(act * act, axis=-1, keepdims=True) + EPS) * (QS if c < C_HEADS else 1.0))
            o_ref[:, cs] = act

    wide = 3 * C_W
    return pl.pallas_call(
        body, name="gdn_pre_fwd", grid=(nseq, nt),
        in_specs=[BS((tm, wide), lambda b, i: (row(b, i), P_QKV // wide)), BS((HALO_C, wide), lambda b, i: (halo(b, i), P_QKV // wide)),
                  BS((SUB, wide), lambda b, i: (0, 0))],
        out_specs=[BS((tm, wide), lambda b, i: (row(b, i), 0))] * 2,
        out_shape=[S((n, wide), F32)] * 2,
        scratch_shapes=[pltpu.VMEM((HALO_C + tm, wide), F32)],
        compiler_params=_cp("arbitrary", "arbitrary"),
    )(p, p, ccw)


def _chunk_common(q, k, gt, gtt, h):
    beta = _col(gt, GB + h)
    gc = _col(gt, GG + h)
    gcr = gtt[GG + h:GG + h + 1, :]
    ii, jj = _subl((CHUNK, CHUNK)), _lane((CHUNK, CHUNK))
    incl, strict = ii >= jj, ii > jj
    dec = jnp.exp(jnp.where(incl, gc - gcr, -jnp.inf))
    kb = k * beta
    kbf = k.astype(BF16)
    a = jnp.where(strict, _dot_nt(kb.astype(BF16), kbf) * dec, 0.0)
    mq = jnp.where(incl, _dot_nt(q.astype(BF16), kbf) * dec, 0.0)
    glast = jnp.sum(jnp.where(_subl(gc.shape) == CHUNK - 1, gc, 0.0), axis=0, keepdims=True)
    return beta, gc, incl, strict, dec, kb, a, mq, glast


def _split(x):
    hi = x.astype(BF16)
    return hi, (x - hi.astype(F32)).astype(BF16)


def _dot3(dot, a, b):
    (ah, al), (bh, bl) = a, b
    return dot(ah, bh) + (dot(ah, bl) + dot(al, bh))


def _unit_lower_inverses(mats):
    eye = (_subl(mats[0].shape) == _lane(mats[0].shape)).astype(F32)
    ms = [-a for a in mats]
    invs = [eye + m for m in ms]
    parts = [_split(m) for m in ms]
    for _ in range(5):
        ms = [_dot3(_dot, s, s) for s in parts]
        parts = [_split(m) for m in ms]
        invs = [inv + _dot3(_dot, _split(inv), s) for inv, s in zip(invs, parts)]
    return invs


def _gdn_chunk_fwd(qkv, gates, p, y, onw, nseq, tt=512):
    n = qkv.shape[0]
    t = n // nseq
    tt = min(tt, t)
    nt = t // tt
    nch = tt // CHUNK

    def body(q_ref, k_ref, v_ref, g_ref, zc_ref, onw_ref, y_in_ref, y_ref, o_ref, u_ref, w_ref, t_ref, ss_ref, s_scr):
        @pl.when(pl.program_id(1) == 0)
        def _():
            s_scr[...] = jnp.zeros_like(s_scr)

        def step(c, carry):
            rows = pl.ds(pl.multiple_of(c * CHUNK, CHUNK), CHUNK)
            gt = g_ref[rows, :]
            gtt = gt.T
            heads = range(C_HEADS)
            hs = [slice(C_DH * h, C_DH * (h + 1)) for h in heads]
            q, k, v = ([r[rows, hs[h]] for h in heads] for r in (q_ref, k_ref, v_ref))
            cm = [_chunk_common(q[h], k[h], gt, gtt, h) for h in heads]
            beta, gc, kb, mq, glast = ([m[i] for m in cm] for i in (0, 1, 5, 7, 8))
            tinv = _unit_lower_inverses([m[6] for m in cm])
            egc = [jnp.exp(g) for g in gc]
            sol = [_dot3(_dot, _split(tinv[h]), _split(jnp.concatenate([v[h] * beta[h], kb[h] * egc[h]], axis=1))) for h in heads]
            sv = [s_scr[h] for h in heads]
            sb = [s.astype(BF16) for s in sv]
            vnb = [(sol[h][:, :C_DH] - _dot(sol[h][:, C_DH:].astype(BF16), sb[h])).astype(BF16) for h in heads]
            o = [_dot((q[h] * egc[h]).astype(BF16), sb[h]) + _dot(mq[h].astype(BF16), vnb[h]) for h in heads]
            for h in heads:
                ss_ref[h, c] = sv[h]
                s_scr[h] = sv[h] * jnp.exp(glast[h]) + _dot_tn((k[h] * jnp.exp(glast[h] - gc[h])).astype(BF16), vnb[h])
            for h in heads:
                o_ref[rows, hs[h]] = o[h]
                u_ref[rows, hs[h]] = sol[h][:, :C_DH]
                w_ref[rows, hs[h]] = sol[h][:, C_DH:]
                t_ref[rows, hs[h]] = jnp.concatenate([tinv[h], jnp.zeros_like(tinv[h])], axis=1)
                zc = zc_ref[rows, hs[h]]
                r = lax.rsqrt(jnp.mean(o[h] * o[h], axis=-1, keepdims=True) + EPS)
                y_ref[rows, hs[h]] = (o[h] * r * onw_ref[...] * (zc * _sigmoid(zc))).astype(BF16)
            return carry

        lax.fori_loop(0, nch, step, 0)

    row = lambda b, i: b * nt + i
    wb = lambda col: BS((tt, C_W), lambda b, i: (row(b, i), col))
    return pl.pallas_call(
        body, name="gdn_chunk_fwd", grid=(nseq, nt),
        in_specs=[wb(0), wb(1), wb(2), BS((tt, LANES), lambda b, i: (row(b, i), 0)), wb(P_ZC // C_W),
                  BS((1, LANES), lambda b, i: (0, 0)), BS(memory_space=pl.ANY)],
        out_specs=[wb(Y_C // C_W), wb(0), wb(0), wb(0), wb(0),
                   BS((None, C_HEADS, nch, C_DH, C_DH), lambda b, i: (b, 0, i, 0, 0))],
        out_shape=[S(y.shape, y.dtype)] + [S((n, C_W), F32)] * 4 + [S((nseq, C_HEADS, t // CHUNK, C_DH, C_DH), F32)],
        input_output_aliases={6: 0},
        scratch_shapes=[pltpu.VMEM((C_HEADS, C_DH, C_DH), F32)],
        compiler_params=_cp("arbitrary", "arbitrary"),
    )(qkv, qkv, qkv, gates, p, onw, y)


def _gdn_chunk_bwd(qkv, gates, p, dy, dp, onw, o, u, w, tinv, ss, nseq, tt=256):
    n = qkv.shape[0]
    t = n // nseq
    tt = min(tt, t)
    nt = t // tt
    nch = tt // CHUNK

    def body(q_ref, k_ref, v_ref, g_ref, zc_ref, onw_ref, o_ref, dy_ref, u_ref, w_ref, t_ref, ss_ref, dp_in_ref,
             dzc_ref, dqkv_ref, dg_ref, donw_ref, ds_scr):
        @pl.when(pl.program_id(1) == 0)
        def _():
            ds_scr[...] = jnp.zeros_like(ds_scr)

        @pl.when((pl.program_id(0) == 0) & (pl.program_id(1) == 0))
        def _():
            donw_ref[...] = jnp.zeros_like(donw_ref)

        def rsum(x):
            return jnp.sum(x, axis=-1, keepdims=True)

        def step(ci, carry):
            c = nch - 1 - ci
            rows = pl.ds(pl.multiple_of(c * CHUNK, CHUNK), CHUNK)
            gt = g_ref[rows, :]
            gtt = gt.T
            live = [head(c, rows, gt, gtt, h) for h in range(C_HEADS)]
            while live:
                live = [g for g in live if next(g, False)]
            return carry

        def head(c, rows, gt, gtt, h):
            hs = slice(C_DH * h, C_DH * (h + 1))
            q, k, v = q_ref[rows, hs], k_ref[rows, hs], v_ref[rows, hs]
            zc, o, dy, u, w = zc_ref[rows, hs], o_ref[rows, hs], dy_ref[rows, hs], u_ref[rows, hs], w_ref[rows, hs]
            tm_ = t_ref[rows, hs][:, 0:CHUNK]
            sv, dsv = ss_ref[h, c], ds_scr[h]
            sb, dsb = sv.astype(BF16), dsv.astype(BF16)
            sg = _sigmoid(zc)
            r = lax.rsqrt(jnp.mean(o * o, axis=-1, keepdims=True) + EPS)
            on = o * r
            ow = onw_ref[...]
            dzc_ref[rows, hs] = (dy * on * ow * _dsilu(zc, sg)).astype(BF16)
            t1 = dy * zc * sg
            donw_ref[...] += _fold8(t1 * on)
            don = t1 * ow
            do = r * (don - on * jnp.mean(don * on, axis=-1, keepdims=True))
            dob = do.astype(BF16)
            yield True
            beta, gc, incl, strict, dec, kb, a, mq, glast = _chunk_common(q, k, gt, gtt, h)
            egc = jnp.exp(gc)
            gl = jnp.exp(glast)
            ekd = jnp.exp(glast - gc)
            wb = w.astype(BF16)
            vnb = (u - _dot(wb, sb)).astype(BF16)
            qg = q * egc
            yield True
            dvn = _dot_tn(mq.astype(BF16), dob) + _dot((k * ekd).astype(BF16), dsb)
            dvnb = dvn.astype(BF16)
            dqg = _dot_nt(dob, sb)
            yield True
            dmq = jnp.where(incl, _dot_nt(dob, vnb), 0.0)
            dkd = _dot_nt(vnb, dsb)
            dgl = jnp.sum(rsum(dsv * sv), axis=0, keepdims=True)
            dw = -_dot_nt(dvnb, sb)
            yield True
            ds_scr[h] = gl * dsv + _dot_tn(qg.astype(BF16), dob) - _dot_tn(wb, dvnb)
            db = _dot3(_dot_tn, _split(tm_), _split(jnp.concatenate([dvn, dw], axis=1)))
            dbv, dbk = db[:, :C_DH], db[:, C_DH:]
            yield True
            da = -jnp.where(strict, _dot3(_dot_nt, _split(dbv), _split(u)) + _dot3(_dot_nt, _split(dbk), _split(w)), 0.0)
            yield True
            e = da * a + dmq * mq
            dgc = rsum(e) - rsum(e.T)
            dgb, dhb, kbf = (da * dec).astype(BF16), (dmq * dec).astype(BF16), k.astype(BF16)
            dkb = _dot(dgb, kbf)
            tk = rsum(dbk * k)
            rk = rsum(dkd * k) * ekd
            dq = _dot(dhb, kbf) + egc * dqg
            dk = _dot_tn(dgb, kb.astype(BF16)) + _dot_tn(dhb, q.astype(BF16)) + beta * (egc * dbk + dkb) + ekd * dkd
            dbeta = rsum(dbv * v) + tk * egc + rsum(dkb * k)
            dgc = dgc + tk * beta * egc + egc * rsum(dqg * q) - rk
            dglast = jnp.sum(rk, axis=0, keepdims=True) + dgl * gl
            dgc = dgc + jnp.where(_subl(dgc.shape) == CHUNK - 1, dglast, 0.0)
            dqkv_ref[0, rows, hs] = dq
            dqkv_ref[1, rows, hs] = dk
            dqkv_ref[2, rows, hs] = beta * dbv
            lane = _lane((CHUNK, LANES))
            dg_ref[h, rows, :] = jnp.where(lane == 0, dbeta, jnp.where(lane == 1, dgc, 0.0))

        lax.fori_loop(0, nch, step, 0)

    row = lambda b, i: b * nt + nt - 1 - i
    wb = lambda col: BS((tt, C_W), lambda b, i: (row(b, i), col))
    return pl.pallas_call(
        body, name="gdn_chunk_bwd", grid=(nseq, nt),
        in_specs=[wb(0), wb(1), wb(2), BS((tt, LANES), lambda b, i: (row(b, i), 0)), wb(P_ZC // C_W),
                  BS((1, LANES), lambda b, i: (0, 0)), wb(0), wb(Y_C // C_W), wb(0), wb(0), wb(0),
                  BS((None, C_HEADS, nch, C_DH, C_DH), lambda b, i: (b, 0, nt - 1 - i, 0, 0)), BS(memory_space=pl.ANY)],
        out_specs=[wb(P_ZC // C_W), BS((3, tt, C_W), lambda b, i: (0, row(b, i), 0)),
                   BS((C_HEADS, tt, LANES), lambda b, i: (0, row(b, i), 0)), BS((SUB, LANES), lambda b, i: (0, 0))],
        out_shape=[S(dp.shape, dp.dtype), S((3, n, C_W), F32), S((C_HEADS, n, LANES), F32), S((SUB, LANES), F32)],
        input_output_aliases={12: 0},
        scratch_shapes=[pltpu.VMEM((C_HEADS, C_DH, C_DH), F32)],
        compiler_params=_cp("arbitrary", "arbitrary"),
    )(qkv, qkv, qkv, gates, p, onw, o, dy, u, w, tinv, ss, dp)


def _gdn_gates_bwd(dgate, p, alog_l, dtb_l, dp, tm=256):
    n = p.shape[0]
    acc = BS((SUB, LANES), lambda i: (0, 0))

    def body(dg_ref, ba_ref, al_ref, db_ref, dp_in_ref, dba_ref, dal_ref, ddb_ref):
        @pl.when(pl.program_id(0) == 0)
        def _():
            dal_ref[...] = jnp.zeros_like(dal_ref)
            ddb_ref[...] = jnp.zeros_like(ddb_ref)

        blk = ba_ref[...]
        lane = _lane(blk.shape)
        dbeta = jnp.zeros_like(blk)
        dgc = jnp.zeros_like(blk)
        for h in range(C_HEADS):
            dbeta = dbeta + jnp.where(lane == GB + h, _col(dg_ref[h], 0), 0.0)
            dgc = dgc + jnp.where(lane == GG + h, _col(dg_ref[h], 1), 0.0)
        tri = (_subl((CHUNK, CHUNK)) <= _lane((CHUNK, CHUNK))).astype(F32)
        dg = jnp.concatenate([_dot(tri, dgc[CHUNK * c:CHUNK * (c + 1)], HI) for c in range(tm // CHUNK)], axis=0)
        beta = _sigmoid(blk)
        z = blk + db_ref[...]
        ea = jnp.exp(al_ref[...])
        isg = (lane >= GG) & (lane < GG + C_HEADS)
        dz = jnp.where(isg, -dg * ea * _sigmoid(z), 0.0)
        dal_ref[...] += _fold8(jnp.where(isg, -dg * ea * _softplus(z), 0.0))
        ddb_ref[...] += _fold8(dz)
        out = jnp.where(lane < GG, dbeta * beta * (1.0 - beta), dz)
        dba_ref[...] = jnp.concatenate([out, jnp.zeros_like(out)], axis=1).astype(BF16)

    return pl.pallas_call(
        body, name="gdn_gates_bwd", grid=(n // tm,),
        in_specs=[BS((C_HEADS, tm, LANES), lambda i: (0, i, 0)), BS((tm, LANES), lambda i: (i, P_BA // LANES)),
                  BS((1, LANES), lambda i: (0, 0)), BS((1, LANES), lambda i: (0, 0)), BS(memory_space=pl.ANY)],
        out_specs=[BS((tm, 2 * LANES), lambda i: (i, P_BA // (2 * LANES))), acc, acc],
        out_shape=[S(dp.shape, dp.dtype), S((SUB, LANES), F32), S((SUB, LANES), F32)],
        input_output_aliases={4: 0},
        compiler_params=_cp("arbitrary"),
    )(dgate, p, alog_l, dtb_l, dp)


def _gdn_pre_bwd(p, dqkv, xc, dp, ccw, nseq, tm=256):
    n = p.shape[0]
    t = n // nseq
    nt = t // tm
    wide = 3 * C_W
    row = lambda b, i: b * nt + i
    prev = lambda b, i: jnp.maximum((b * t + i * tm) // HALO_C - 1, 0)
    nxt = lambda b, i: jnp.minimum((b * t + (i + 1) * tm) // HALO_C, n // HALO_C - 1)

    def d_conv_out(d, xc, part):
        sg = _sigmoid(xc)
        act = xc * sg
        if part < 2:
            cs = QS if part == 0 else 1.0
            rn = lax.rsqrt(jnp.sum(act * act, axis=-1, keepdims=True) + EPS)
            d = cs * rn * d - act * (cs * rn * rn * rn * jnp.sum(d * act, axis=-1, keepdims=True))
        return d * _dsilu(xc, sg)

    def body(x_ref, xh_ref, d_ref, dn_ref, xc_ref, xn_ref, w_ref, dp_in_ref, dx_ref, dw_ref, buf_ref, dbuf_ref):
        i = pl.program_id(1)

        @pl.when((pl.program_id(0) == 0) & (i == 0))
        def _():
            dw_ref[...] = jnp.zeros_like(dw_ref)

        buf_ref[0:HALO_C, :] = jnp.where(i > 0, xh_ref[...], 0.0)
        buf_ref[HALO_C:, :] = x_ref[...]
        for c in range(NCB):
            cs = slice(LANES * c, LANES * (c + 1))
            part, hd = divmod(c, C_HEADS)
            hs = slice(LANES * hd, LANES * (hd + 1))
            d = d_conv_out(d_ref[part, :, hs], xc_ref[:, cs], part)
            dbuf_ref[0:tm, cs] = d
            dbuf_ref[tm:, cs] = jnp.where(i < nt - 1, d_conv_out(dn_ref[part, :, hs], xn_ref[:, cs], part), 0.0)
            dx = jnp.zeros((tm, LANES), F32)
            for k in range(C_K):
                dx = dx + w_ref[k:k + 1, cs] * dbuf_ref[pl.ds(C_K - 1 - k, tm), cs]
                dw_ref[SUB * k:SUB * (k + 1), cs] += _fold8(d * buf_ref[pl.ds(HALO_C - C_K + 1 + k, tm), cs])
            dx_ref[:, cs] = dx.astype(BF16)

    return pl.pallas_call(
        body, name="gdn_pre_bwd", grid=(nseq, nt),
        in_specs=[BS((tm, wide), lambda b, i: (row(b, i), P_QKV // wide)), BS((HALO_C, wide), lambda b, i: (prev(b, i), P_QKV // wide)),
                  BS((3, tm, C_W), lambda b, i: (0, row(b, i), 0)), BS((3, HALO_C, C_W), lambda b, i: (0, nxt(b, i), 0)),
                  BS((tm, wide), lambda b, i: (row(b, i), 0)), BS((HALO_C, wide), lambda b, i: (nxt(b, i), 0)),
                  BS((SUB, wide), lambda b, i: (0, 0)), BS(memory_space=pl.ANY)],
        out_specs=[BS((tm, wide), lambda b, i: (row(b, i), P_QKV // wide)), BS((SUB * C_K, wide), lambda b, i: (0, 0))],
        out_shape=[S(dp.shape, dp.dtype), S((SUB * C_K, wide), F32)], input_output_aliases={7: 0},
        scratch_shapes=[pltpu.VMEM((HALO_C + tm, wide), F32)] * 2,
        compiler_params=_cp("arbitrary", "arbitrary"),
    )(p, p, dqkv, dqkv, xc, xc, ccw, dp)


ANY = BS(memory_space=pl.ANY)


def _my_pos():
    return lax.axis_index("x"), lax.axis_index("y"), lax.axis_index("c")


def _dev_index(dev):
    return 4 * dev[0] + 2 * dev[1] + dev[2]


def _all_gather(shards, after=None):
    nk = len(shards)

    tail = [] if after is None else [after]

    def body(*refs):
        ins, outs = refs[:nk], refs[nk + len(tail):2 * nk + len(tail)]
        send, recv, loc = refs[2 * nk + len(tail):]
        x, y, c = _my_pos()
        me, sib = (x, y, c), (x, y, 1 - c)
        chips = [(1 - x, y), (x, 1 - y), (1 - x, 1 - y)]

        def rows(t, dev):
            r = ins[t].shape[0]
            return outs[t].at[pl.ds(pl.multiple_of(_dev_index(dev) * r, SUB), r), :]

        def copy(t, k, block, to, src=None):
            return pltpu.make_async_remote_copy(
                src_ref=rows(t, block) if src is None else src, dst_ref=rows(t, block),
                send_sem=send.at[t, k], recv_sem=recv.at[t, k], device_id=to, device_id_type=MESH)

        mine = [pltpu.make_async_copy(ins[t], rows(t, me), loc.at[t]) for t in range(nk)]
        for cp in mine:
            cp.start()
        first = []
        for t in range(nk):
            first.append(copy(t, 0, me, sib, src=ins[t]))
            first += [copy(t, 1 + j, me, (*chip, c), src=ins[t]) for j, chip in enumerate(chips)]
        for cp in first:
            cp.start()
        passed = []
        for j, chip in enumerate(chips):
            for t in range(nk):
                copy(t, 1 + j, (*chip, c), me).wait_recv()
                cp = copy(t, 4 + j, (*chip, c), sib)
                cp.start()
                passed.append(cp)
        for t in range(nk):
            copy(t, 0, sib, me).wait_recv()
            for j, chip in enumerate(chips):
                copy(t, 4 + j, (*chip, 1 - c), me).wait_recv()
        for cp in first + passed:
            cp.wait_send()
        for cp in mine:
            cp.wait()

    return pl.pallas_call(
        body, name="all_gather", in_specs=[ANY] * (nk + len(tail)), out_specs=[ANY] * nk,
        out_shape=[S((N_DEV * a.shape[0], a.shape[1]), a.dtype) for a in shards],
        scratch_shapes=[pltpu.SemaphoreType.DMA((nk, 7)), pltpu.SemaphoreType.DMA((nk, 7)), pltpu.SemaphoreType.DMA((nk,))],
    )(*shards, *tail)


SEM = BS(memory_space=pltpu.SEMAPHORE)
HBM = BS(memory_space=pltpu.HBM)
EFFECT = pltpu.SideEffectType.DATAFLOW_SIDE_EFFECTING


def _peers(x, y, c):
    return [((1 - x) if k & 4 else x, (1 - y) if k & 2 else y, (1 - c) if k & 1 else c) for k in range(1, N_DEV)]


def _exchange_copy(kind, src, land, send, recv, t, k, peer, me, arriving):
    frm = peer if arriving else me
    if kind == "gather":
        r = src.shape[0]
        s_ref = src
        d_ref = land.at[pl.ds(pl.multiple_of(_dev_index(frm) * r, SUB), r), :]
    else:
        r = src.shape[0] // N_DEV
        s_ref = src.at[pl.ds(pl.multiple_of(_dev_index(peer) * r, SUB), r), :]
        d_ref = land.at[_dev_index(frm)]
    sem = t * (N_DEV - 1) + k
    return pltpu.make_async_remote_copy(src_ref=s_ref, dst_ref=d_ref, send_sem=send.at[sem], recv_sem=recv.at[sem],
                                        device_id=peer, device_id_type=MESH)


def _own_copy(kind, src, land, own, t, me):
    if kind == "gather":
        r = src.shape[0]
        return pltpu.make_async_copy(src, land.at[pl.ds(pl.multiple_of(_dev_index(me) * r, SUB), r), :], own.at[t])
    r = src.shape[0] // N_DEV
    return pltpu.make_async_copy(src.at[pl.ds(pl.multiple_of(_dev_index(me) * r, SUB), r), :], land.at[_dev_index(me)], own.at[t])


def _exchange_start(kind, srcs, after, name):
    nk = len(srcs)
    if kind == "gather":
        lands = [lax.empty((N_DEV * a.shape[0], a.shape[1]), a.dtype) for a in srcs]
    else:
        lands = [lax.empty((N_DEV, a.shape[0] // N_DEV, a.shape[1]), a.dtype) for a in srcs]

    def body(*refs):
        src, land = refs[:nk], refs[nk:2 * nk]
        send, recv, own = refs[2 * nk + 1], refs[2 * nk + 2], refs[2 * nk + 3]
        token = refs[-1]
        x, y, c = _my_pos()
        me = (x, y, c)
        for t in range(nk):
            _own_copy(kind, src[t], land[t], own, t, me).start()
            for k, peer in enumerate(_peers(x, y, c)):
                _exchange_copy(kind, src[t], land[t], send, recv, t, k, peer, me, False).start()
        token[...] = jnp.zeros_like(token)

    hbm = lambda a: pltpu.HBM(a.shape, a.dtype)
    out = pl.pallas_call(
        body, name=name,
        out_shape=(pltpu.SemaphoreType.DMA((nk * (N_DEV - 1),)), pltpu.SemaphoreType.DMA((nk * (N_DEV - 1),)),
                   pltpu.SemaphoreType.DMA((nk,)), *[hbm(a) for a in srcs], *[hbm(a) for a in lands], S((SUB, LANES), F32)),
        in_specs=[HBM] * (2 * nk) + [ANY],
        out_specs=(SEM, SEM, SEM, *[HBM] * (2 * nk), BS(memory_space=pltpu.VMEM)),
        input_output_aliases={i: 3 + i for i in range(2 * nk)},
        compiler_params=pltpu.CompilerParams(has_side_effects=EFFECT),
    )(*[pltpu.with_memory_space_constraint(a, pltpu.HBM) for a in (*srcs, *lands)], after)
    return dict(kind=kind, nk=nk, send=out[0], recv=out[1], own=out[2], srcs=out[3:3 + nk], lands=out[3 + nk:3 + 2 * nk],
                token=out[-1])


def _exchange_wait(ex, after, name):
    kind, nk = ex["kind"], ex["nk"]

    def body(*refs):
        src, land = refs[:nk], refs[nk:2 * nk]
        send, recv, own = refs[2 * nk], refs[2 * nk + 1], refs[2 * nk + 2]
        x, y, c = _my_pos()
        me = (x, y, c)
        for t in range(nk):
            _own_copy(kind, src[t], land[t], own, t, me).wait()
            for k, peer in enumerate(_peers(x, y, c)):
                _exchange_copy(kind, src[t], land[t], send, recv, t, k, peer, me, False).wait_send()
                _exchange_copy(kind, src[t], land[t], send, recv, t, k, peer, me, True).wait_recv()

    hbm = lambda a: pltpu.HBM(a.shape, a.dtype)
    out = pl.pallas_call(
        body, name=name,
        out_shape=(*[hbm(a) for a in ex["srcs"]], *[hbm(a) for a in ex["lands"]]),
        in_specs=[HBM] * (2 * nk) + [SEM, SEM, SEM, ANY], out_specs=tuple([HBM] * (2 * nk)),
        input_output_aliases={i: i for i in range(2 * nk)},
        compiler_params=pltpu.CompilerParams(has_side_effects=EFFECT),
    )(*ex["srcs"], *ex["lands"], ex["send"], ex["recv"], ex["own"], after)
    return list(out[nk:])


BLOCK_BYTES = 4 << 20


def _row_tile(rows, row_bytes, align):
    best = align
    for tr in range(align, rows + 1, align):
        if rows % tr == 0 and tr * row_bytes <= BLOCK_BYTES:
            best = tr
    return best


def _sum8(a):
    _, r, w = a.shape
    tr = _row_tile(r, N_DEV * w * a.dtype.itemsize, 32 // a.dtype.itemsize)

    def body(a_ref, o_ref):
        acc = a_ref[0].astype(F32)
        for d in range(1, N_DEV):
            acc = acc + a_ref[d].astype(F32)
        o_ref[...] = acc

    return pl.pallas_call(
        body, name="sum8", grid=(r // tr,), in_specs=[BS((N_DEV, tr, w), lambda i: (0, i, 0))],
        out_specs=BS((tr, w), lambda i: (i, 0)), out_shape=S((r, w), F32), compiler_params=_cp("arbitrary"),
    )(a)


def _adamw(w, g, m, v):
    r, c = w.shape
    tr = _row_tile(r, c * 4 * 2, SUB)

    def body(w_ref, g_ref, m_ref, v_ref, d_ref, mo_ref, vo_ref):
        d_ref[...], mo_ref[...], vo_ref[...] = _adam_update(w_ref[...], g_ref[...], m_ref[...], v_ref[...])

    blk = BS((tr, c), lambda i: (i, 0))
    return pl.pallas_call(
        body, name="adamw", grid=(r // tr,), in_specs=[blk] * 4, out_specs=[blk] * 3,
        out_shape=[S((r, c), F32)] * 3, compiler_params=_cp("arbitrary"),
    )(w, g, m, v)


def _sum8_t(a, tc=256):
    _, r, w = a.shape

    def body(a_ref, o_ref):
        acc = a_ref[0].astype(F32)
        for d in range(1, N_DEV):
            acc = acc + a_ref[d].astype(F32)
        o_ref[...] = acc.T

    return pl.pallas_call(
        body, name="sum8_t", grid=(w // tc,), in_specs=[BS((N_DEV, r, tc), lambda j: (0, 0, j))],
        out_specs=BS((tc, r), lambda j: (j, 0)), out_shape=S((w, r), F32), compiler_params=_cp("arbitrary"),
    )(a)


def _rows_view(a):
    nl, r, c = a.shape
    assert nl == 2
    return a.transpose(2, 0, 1).reshape(c, nl, r // LANES, LANES).transpose(0, 2, 1, 3).reshape(-1, LANES)


def _rows_view_back(a, shape):
    nl, r, c = shape
    return a.reshape(c, r // LANES, nl, LANES).transpose(0, 2, 1, 3).reshape(c, nl, r).transpose(1, 2, 0)


def _adamw_rows(w, g, m, v, tr=2048):
    n = w.shape[0]

    def body(w_ref, g_ref, m_ref, v_ref, d_ref, mo_ref, vo_ref):
        d_ref[...], mo_ref[...], vo_ref[...] = _adam_update(w_ref[...], g_ref[...], m_ref[...], v_ref[...])

    blk = BS((tr, LANES), lambda i: (i, 0))
    return pl.pallas_call(
        body, name="adamw_rows", grid=(pl.cdiv(n, tr),), in_specs=[blk] * 4, out_specs=[blk] * 3,
        out_shape=[S((n, LANES), F32)] * 3, compiler_params=_cp("arbitrary"),
    )(w, g, m, v)


def _adam_update(w, g, m, v):
    m2 = ADAM_B1 * m + (1.0 - ADAM_B1) * g
    v2 = ADAM_B2 * v + (1.0 - ADAM_B2) * (g * g)
    m_hat = m2 / (1.0 - ADAM_B1 ** ADAM_STEP)
    v_hat = v2 / (1.0 - ADAM_B2 ** ADAM_STEP)
    return -ADAM_LR * (m_hat / (jnp.sqrt(v_hat) + ADAM_EPS) + ADAM_WD * w), m2, v2


def _adamw_layer(w, g, m, v, l, prev):
    nl, r, c = w.shape
    tr = _row_tile(r, c * 4 * 2, SUB)

    def body(w_ref, g_ref, m_ref, v_ref, *refs):
        go_ref, d_ref, mo_ref, vo_ref = refs[-4:]
        gv = g_ref[...]
        go_ref[...] = gv
        d_ref[...], mo_ref[...], vo_ref[...] = _adam_update(w_ref[...], gv, m_ref[...], v_ref[...])

    slot = BS((None, tr, c), lambda i: (l, i, 0))
    keep = [] if prev is None else [ANY] * 4
    return pl.pallas_call(
        body, name="adamw_layer", grid=(r // tr,), in_specs=[slot, BS((tr, c), lambda i: (i, 0)), slot, slot] + keep,
        out_specs=[slot] * 4, out_shape=[S((nl, r, c), F32)] * 4,
        input_output_aliases={} if prev is None else {4 + i: i for i in range(4)},
        compiler_params=_cp("arbitrary"),
    )(w, g, m, v, *(prev or ()))


def _blob(arrays):
    flat = jnp.concatenate([a.reshape(-1) for a in arrays])
    rows = -(-flat.shape[0] // (SUB * LANES)) * SUB
    return jnp.pad(flat, (0, rows * LANES - flat.shape[0])).reshape(rows, LANES)


def _unblob(blob, shapes, lead=()):
    flat = blob.reshape(lead + (-1,))
    out, off = [], 0
    for s in shapes:
        size = math.prod(s)
        out.append(flat[..., off:off + size].reshape(lead + tuple(s)))
        off += size
    return out


def _y_rows(w):
    return jnp.concatenate([w[0:A_W], w[A_W + B_W:], w[A_W:A_W + B_W]], axis=0)


def _y_rows_back(g):
    return jnp.concatenate([g[0:A_W], g[A_W + C_W:], g[A_W:A_W + C_W]], axis=0)


SMALL = ("norm_w", "q_norm_w", "k_norm_w", "sinks", "b_conv_b", "b_ln_w", "b_ln_b", "b_pw_b", "c_a_log", "c_dt_bias",
         "c_onorm_w", "b_conv_w", "c_conv_w")
ORDER = ("norm_w", "w_in", "q_norm_w", "k_norm_w", "sinks", "b_conv_w", "b_conv_b", "b_ln_w", "b_ln_b", "b_pw_w", "b_pw_b",
         "c_conv_w", "c_a_log", "c_dt_bias", "c_onorm_w", "w_out")


def kernel(x, positions, norm_w, w_in, q_norm_w, k_norm_w, sinks, b_conv_w, b_conv_b, b_ln_w, b_ln_b, b_pw_w, b_pw_b, c_conv_w, c_a_log, c_dt_bias, c_onorm_w, w_out, loss_target, m_norm_w, m_w_in, m_q_norm_w, m_k_norm_w, m_sinks, m_b_conv_w, m_b_conv_b, m_b_ln_w, m_b_ln_b, m_b_pw_w, m_b_pw_b, m_c_conv_w, m_c_a_log, m_c_dt_bias, m_c_onorm_w, m_w_out, v_norm_w, v_w_in, v_q_norm_w, v_k_norm_w, v_sinks, v_b_conv_w, v_b_conv_b, v_b_ln_w, v_b_ln_b, v_b_pw_w, v_b_pw_b, v_c_conv_w, v_c_a_log, v_c_dt_bias, v_c_onorm_w, v_w_out):
    W = dict(norm_w=norm_w, w_in=w_in, q_norm_w=q_norm_w, k_norm_w=k_norm_w, sinks=sinks, b_conv_w=b_conv_w, b_conv_b=b_conv_b,
             b_ln_w=b_ln_w, b_ln_b=b_ln_b, b_pw_w=b_pw_w, b_pw_b=b_pw_b, c_conv_w=c_conv_w, c_a_log=c_a_log,
             c_dt_bias=c_dt_bias, c_onorm_w=c_onorm_w, w_out=w_out)
    M = dict(norm_w=m_norm_w, w_in=m_w_in, q_norm_w=m_q_norm_w, k_norm_w=m_k_norm_w, sinks=m_sinks, b_conv_w=m_b_conv_w,
             b_conv_b=m_b_conv_b, b_ln_w=m_b_ln_w, b_ln_b=m_b_ln_b, b_pw_w=m_b_pw_w, b_pw_b=m_b_pw_b, c_conv_w=m_c_conv_w,
             c_a_log=m_c_a_log, c_dt_bias=m_c_dt_bias, c_onorm_w=m_c_onorm_w, w_out=m_w_out)
    V = dict(norm_w=v_norm_w, w_in=v_w_in, q_norm_w=v_q_norm_w, k_norm_w=v_k_norm_w, sinks=v_sinks, b_conv_w=v_b_conv_w,
             b_conv_b=v_b_conv_b, b_ln_w=v_b_ln_w, b_ln_b=v_b_ln_b, b_pw_w=v_b_pw_w, b_pw_b=v_b_pw_b, c_conv_w=v_c_conv_w,
             c_a_log=v_c_a_log, c_dt_bias=v_c_dt_bias, c_onorm_w=v_c_onorm_w, w_out=v_w_out)
    nseq, t, d = x.shape
    n = nseq * t
    tr = min(256, t)
    tmm = min(512, n)
    tmw = min(1024, n)
    tkk = min(2048, n)
    me = _dev_index(_my_pos())
    xs = [x.reshape(n, d)]
    tgt = loss_target.reshape(n, d)
    tabs = _rope_tables(positions.reshape(n))

    win_p = _pack_cols(w_in).astype(BF16)
    wout_b = w_out.astype(BF16)
    sharded_small = (b_pw_w, b_conv_w, c_conv_w)
    g_win0, g_small = _all_gather([win_p[0], _blob(sharded_small)])
    win = [g_win0]
    later = _exchange_start("gather", [win_p[1], wout_b[0], wout_b[1]], g_small, "gather_start")
    pw_all, cw_all, ccw_all = _unblob(g_small, [a.shape for a in sharded_small], lead=(N_DEV,))
    pw_all = pw_all.transpose(1, 0, 2, 3).reshape(DEPTH, B_W, B_W).astype(BF16)
    cw_all = jnp.pad(cw_all.transpose(1, 2, 0, 3).reshape(DEPTH, B_K, B_W), ((0, 0), (0, HALO_B - B_K), (0, 0)))
    ccw_all = jnp.pad(ccw_all.transpose(1, 2, 0, 3).reshape(DEPTH, C_K, 3 * C_W), ((0, 0), (0, SUB - C_K), (0, 0)))
    qw_all, kw_all = jnp.tile(q_norm_w, (1, 2)), jnp.tile(k_norm_w, (1, 2))
    lanes6 = lambda a: jnp.zeros((DEPTH, LANES), F32).at[:, GG:GG + C_HEADS].set(a)
    alog_all, dtb_all = lanes6(c_a_log), lanes6(c_dt_bias)

    def layer_params(l):
        row = lambda a: a[l][None]
        return dict(
            nw=row(norm_w), qw=row(qw_all), kw=row(kw_all), sinks=sinks[l], cw=cw_all[l], cb=row(b_conv_b), lnw=row(b_ln_w),
            lnb=row(b_ln_b), pw=pw_all[l], pwb=row(b_pw_b), ccw=ccw_all[l], alog=row(alog_all), dtb=row(dtb_all),
            onw=row(c_onorm_w))

    saved = []
    for l in range(DEPTH):
        q = layer_params(l)
        nw = q["nw"] + later["token"][0:1, 0:1] if l == 0 else q["nw"]
        p, h = _inproj(xs[l], nw, win[l], tm=tmw)
        y, o_a, lse = _attn_fwd(p, tabs, q["qw"], q["kw"], q["sinks"], nseq)
        gates = _gdn_gates_fwd(p, q["alog"], q["dtb"], tm=tr)
        xc, qkv = _gdn_pre_fwd(p, q["ccw"], nseq, tm=tr)
        y, o_c, u, w, tinv, ss = _gdn_chunk_fwd(qkv, gates, p, y, q["onw"], nseq)
        y, hc = _conf_fwd(p, y, q["cw"], q["cb"], q["lnw"], q["lnb"], q["pw"], q["pwb"], nseq, tm=tr)
        saved.append(dict(q=q, p=p, h=h, y=y, o_a=o_a, lse=lse, gates=gates, xc=xc, qkv=qkv, o_c=o_c, u=u, w=w, tinv=tinv,
                          ss=ss, hc=hc))
        if l == 0:
            g_win1, g_wout0, g_wout1 = _exchange_wait(later, y, "gather_wait")
            win.append(g_win1)
            wout = [_y_rows(g_wout0), _y_rows(g_wout1)]
        if l + 1 < DEPTH:
            xs.append(_outproj(xs[l], y, wout[l], tm=tmw, tn=512))
        else:
            dxn, lsum = _outproj_loss(xs[l], y, wout[l], tgt, tm=tmw, tn=512)
    loss = lax.psum(jnp.sum(lsum) * (0.5 / d), ("x", "y", "c"))

    sent, smalls = [None] * DEPTH, [None] * DEPTH
    for l in reversed(range(DEPTH)):
        s = saved[l]
        q, p = s["q"], s["p"]
        dy = _matmul(dxn, wout[l], "nt", F32, tmw, 512, d, "outproj_bwd_dy")
        dwout = _y_rows_back(_matmul(s["y"], dxn, "tn", BF16, 1024, 1024, tkk, "outproj_bwd_dw"))
        dp, dkv, dqw, dkw, dsk = _attn_bwd(p, dy, s["o_a"], s["lse"], tabs, q["qw"], q["kw"], q["sinks"], nseq)
        dp, dqkv, dgate, donw = _gdn_chunk_bwd(s["qkv"], s["gates"], p, dy, dp, q["onw"], s["o_c"], s["u"], s["w"],
                                               s["tinv"], s["ss"], nseq)
        dp, dccw = _gdn_pre_bwd(p, dqkv, s["xc"], dp, q["ccw"], nseq, tm=tr)
        early = P_K // 768
        dwin_a = _matmul(s["h"], dp, "tn", BF16, 1024, 768, tkk, "inproj_bwd_dw_a", b_cols=(0, early))
        sent_a = _exchange_start("scatter", [dwin_a, dwout], donw, "scatter_start_a%d" % l)
        dp = _put_cols(dp, dkv, P_K, sent_a["token"], tm=tmm)
        dp, dal, ddb = _gdn_gates_bwd(dgate, p, q["alog"], q["dtb"], dp, tm=tr)
        dp, dhc, dpw, dpwb, dlnw, dlnb, dcb = _conf_bwd1(p, dy, dp, s["hc"], q["lnw"], q["lnb"], q["pw"], q["pwb"], tm=tr)
        dp, dcw = _conf_bwd2(p, dhc, dp, q["cw"], nseq, tm=tr)
        dwin_b = _matmul(s["h"], dp, "tn", BF16, 1024, 768, tkk, "inproj_bwd_dw_b", b_cols=(early, P_W // 768 - early))
        sent_b = _exchange_start("scatter", [dwin_b, dpw], dpwb, "scatter_start_b%d" % l)
        sent[l] = (sent_a, sent_b)
        dxn, dnw = _inproj_bwd_dx(dp, win[l], xs[l], q["nw"] + sent_b["token"][0:1, 0:1], dxn, tm=tmm)
        halves = lambda a: a.sum(0)[:A_DH] + a.sum(0)[A_DH:]
        smalls[l] = dict(
            norm_w=dnw.sum(0), q_norm_w=halves(dqw), k_norm_w=halves(dkw), sinks=dsk.sum(0)[:A_HEADS], b_conv_b=dcb.sum(0),
            b_ln_w=dlnw.sum(0), b_ln_b=dlnb.sum(0), b_pw_b=dpwb.sum(0), c_a_log=dal.sum(0)[GG:GG + C_HEADS],
            c_dt_bias=ddb.sum(0)[GG:GG + C_HEADS], c_onorm_w=donw.sum(0),
            b_conv_w=dcw.reshape(B_K, SUB, B_W).sum(1), c_conv_w=dccw.reshape(C_K, SUB, 3 * C_W).sum(1))
    grad_x = dxn.reshape(nseq, t, d)

    G, delta, new_m, new_v = {}, {}, {}, {}
    big = ("w_in", "w_out", "b_pw_w")
    stacks = {k: None for k in big}
    after = dxn
    g_t = [None] * DEPTH
    for l in reversed(range(DEPTH)):
        r_win_a, r_wout = _exchange_wait(sent[l][0], after, "scatter_wait_a%d" % l)
        r_win_b, r_pw = _exchange_wait(sent[l][1], r_wout, "scatter_wait_b%d" % l)
        g_t[l] = jnp.concatenate([_sum8_t(r_win_a), _sum8_t(r_win_b)], axis=0).reshape(P_W, -1, LANES)
        for k, r in (("w_out", r_wout), ("b_pw_w", r_pw)):
            stacks[k] = _adamw_layer(W[k], _sum8(r), M[k], V[k], l, stacks[k])
        after = stacks["w_out"][1]
    g_in = jnp.stack(g_t, axis=2).reshape(-1, LANES)
    g_in = _unpack_cols(g_in, axis=0, each=g_in.shape[0] // P_W)
    rows = _adamw_rows(_rows_view(w_in), g_in, _rows_view(m_w_in), _rows_view(v_w_in))
    stacks["w_in"] = [_rows_view_back(a, w_in.shape) for a in (g_in, *rows)]
    for k in big:
        G[k], delta[k], new_m[k], new_v[k] = stacks[k]
    part = _blob([jnp.stack([smalls[l][k] for l in range(DEPTH)]) for k in SMALL])
    (tot,) = _all_gather([part], after=rows[0])
    tot = _sum8(tot.reshape(N_DEV, part.shape[0], LANES))
    full_shapes = [(DEPTH,) + smalls[0][k].shape for k in SMALL]
    for k, g in zip(SMALL, _unblob(tot, full_shapes)):
        G[k] = g
    G["b_conv_w"] = lax.dynamic_slice_in_dim(G["b_conv_w"], me * (B_W // N_DEV), B_W // N_DEV, axis=2)
    G["c_conv_w"] = lax.dynamic_slice_in_dim(G["c_conv_w"], me * (3 * C_W // N_DEV), 3 * C_W // N_DEV, axis=2)
    dl, mo, vo = _adamw(*[_blob([src[k] for k in SMALL]) for src in (W, G, M, V)])
    shapes = [W[k].shape for k in SMALL]
    for k, a, b, c in zip(SMALL, _unblob(dl, shapes), _unblob(mo, shapes), _unblob(vo, shapes)):
        delta[k], new_m[k], new_v[k] = a, b, c
    return (loss, grad_x, *[G[k] for k in ORDER], *[delta[k] for k in ORDER], *[new_m[k] for k in ORDER],
            *[new_v[k] for k in ORDER])
```

```python
import functools
import math

import jax
import jax.numpy as jnp
from jax import lax
from jax.experimental import pallas as pl
from jax.experimental.pallas import tpu as pltpu

F32 = jnp.float32
BF16 = jnp.bfloat16
HI = lax.Precision.HIGHEST
MESH = pl.DeviceIdType.MESH
S = jax.ShapeDtypeStruct
BS = pl.BlockSpec

N_DEV = 8
DEPTH = 2
D_MODEL = 2048
A_HEADS, A_KV, A_DH, A_W, A_KVW = 12, 4, 64, 768, 256
ROT = 16
THETA = 500000.0
ABLK = 128
B_W, B_K = 512, 31
C_HEADS, C_DH, C_W, C_K, CHUNK = 6, 128, 768, 4, 64
EPS = 1e-6
IN_COLS = 6668
P_Q, P_ZA, P_ZC, P_QKV, P_K, P_V, P_UB, P_ZB, P_BA, P_W = 0, 768, 1536, 2304, 4608, 4864, 5120, 6144, 6656, 6912
Y_A, Y_C, Y_B = 0, 768, 1536
LANES = 128
SUB = 8

ADAM_LR, ADAM_B1, ADAM_B2, ADAM_EPS, ADAM_WD, ADAM_STEP = 0.001, 0.9, 0.999, 1e-08, 0.01, 10


def _cp(*sem, vmem=None):
    kw = {}
    if sem:
        kw["dimension_semantics"] = sem
    if vmem:
        kw["vmem_limit_bytes"] = vmem
    return pltpu.CompilerParams(**kw)


def _pack_cols(w):
    z = jnp.zeros(w.shape[:-1] + (P_W - IN_COLS,), w.dtype)
    return jnp.concatenate([w[..., 0:768], w[..., 1280:2048], w[..., 5900:6668], w[..., 3584:5888],
                            w[..., 768:1024], w[..., 1024:1280], w[..., 2048:3072], w[..., 3072:3584],
                            w[..., 5888:5900], z], axis=-1)


def _unpack_cols(g, axis=-1, each=1):
    parts = ((P_Q, 768), (P_K, 256), (P_V, 256), (P_ZA, 768), (P_UB, 1024), (P_ZB, 512), (P_QKV, 2304), (P_BA, 12), (P_ZC, 768))
    return jnp.concatenate([lax.slice_in_dim(g, each * o, each * (o + n), axis=axis) for o, n in parts], axis=axis)


def _sigmoid(x):
    return 0.5 * jnp.tanh(0.5 * x) + 0.5


def _dsilu(x, sg):
    return sg * (1.0 + x * (1.0 - sg))


def _fold8(x):
    r, c = x.shape
    return x.reshape(r // SUB, SUB, c).sum(axis=0)


def _dot(a, b, prec=None):
    return jnp.dot(a, b, preferred_element_type=F32, precision=prec)


def _dot_nt(a, b, prec=None):
    return lax.dot_general(a, b, (((1,), (1,)), ((), ())), preferred_element_type=F32, precision=prec)


def _dot_tn(a, b, prec=None):
    return lax.dot_general(a, b, (((0,), (0,)), ((), ())), preferred_element_type=F32, precision=prec)


def _lane(shape):
    return lax.broadcasted_iota(jnp.int32, shape, 1)


def _subl(shape):
    return lax.broadcasted_iota(jnp.int32, shape, 0)


def _col(x, j):
    return jnp.sum(jnp.where(_lane(x.shape) == j, x, 0.0), axis=-1, keepdims=True)


def _inproj(x, nw, w, tm=512, tn=768):
    n, d = x.shape
    pw = w.shape[1]

    def body(x_ref, nw_ref, w_ref, p_ref, h_ref):
        @pl.when(pl.program_id(1) == 0)
        def _():
            xv = x_ref[...]
            r = lax.rsqrt(jnp.mean(xv * xv, axis=-1, keepdims=True) + EPS)
            h_ref[...] = (xv * r * nw_ref[...]).astype(BF16)

        p_ref[...] = _dot(h_ref[...], w_ref[...])

    return pl.pallas_call(
        body, name="inproj", grid=(n // tm, pw // tn),
        in_specs=[BS((tm, d), lambda i, j: (i, 0)), BS((1, d), lambda i, j: (0, 0)), BS((d, tn), lambda i, j: (0, j))],
        out_specs=[BS((tm, tn), lambda i, j: (i, j)), BS((tm, d), lambda i, j: (i, 0))],
        out_shape=[S((n, pw), F32), S((n, d), BF16)],
        compiler_params=_cp("arbitrary", "arbitrary"),
    )(x, nw, w)


def _outproj(x, y, w, tm=512, tn=1024):
    n, d = x.shape
    k = y.shape[1]

    def body(x_ref, y_ref, w_ref, o_ref):
        o_ref[...] = x_ref[...] + _dot(y_ref[...], w_ref[...])

    return pl.pallas_call(
        body, name="outproj", grid=(n // tm, d // tn),
        in_specs=[BS((tm, tn), lambda i, j: (i, j)), BS((tm, k), lambda i, j: (i, 0)), BS((k, tn), lambda i, j: (0, j))],
        out_specs=BS((tm, tn), lambda i, j: (i, j)),
        out_shape=S((n, d), F32),
        compiler_params=_cp("arbitrary", "arbitrary"),
    )(x, y, w)


def _outproj_loss(x, y, w, tgt, tm=512, tn=1024):
    n, d = x.shape
    k = y.shape[1]

    def body(x_ref, y_ref, w_ref, t_ref, g_ref, l_ref):
        @pl.when((pl.program_id(0) == 0) & (pl.program_id(1) == 0))
        def _():
            l_ref[...] = jnp.zeros_like(l_ref)

        diff = x_ref[...] + _dot(y_ref[...], w_ref[...]) - t_ref[...]
        g_ref[...] = diff * (1.0 / d)
        f = _fold8(diff * diff)
        acc = f[:, 0:LANES]
        for c in range(1, tn // LANES):
            acc = acc + f[:, c * LANES:(c + 1) * LANES]
        l_ref[...] += acc

    return pl.pallas_call(
        body, name="outproj_loss", grid=(n // tm, d // tn),
        in_specs=[BS((tm, tn), lambda i, j: (i, j)), BS((tm, k), lambda i, j: (i, 0)), BS((k, tn), lambda i, j: (0, j)),
                  BS((tm, tn), lambda i, j: (i, j))],
        out_specs=[BS((tm, tn), lambda i, j: (i, j)), BS((SUB, LANES), lambda i, j: (0, 0))],
        out_shape=[S((n, d), F32), S((SUB, LANES), F32)],
        compiler_params=_cp("arbitrary", "arbitrary"),
    )(x, y, w, tgt)


def _matmul(a, b, mode, out_dtype, tm, tn, tk, name, b_cols=None):
    if mode == "nn":
        (m, kk), nn = a.shape, b.shape[1]
        a_spec, b_spec = BS((tm, tk), lambda i, j, k: (i, k)), BS((tk, tn), lambda i, j, k: (k, j))
        dot = _dot
    elif mode == "nt":
        (m, kk), nn = a.shape, b.shape[0]
        a_spec, b_spec = BS((tm, tk), lambda i, j, k: (i, k)), BS((tn, tk), lambda i, j, k: (j, k))
        dot = _dot_nt
    else:
        j0, nj = b_cols or (0, b.shape[1] // tn)
        (kk, m), nn = a.shape, nj * tn
        a_spec, b_spec = BS((tk, tm), lambda i, j, k: (k, i)), BS((tk, tn), lambda i, j, k: (k, j0 + j))
        dot = _dot_tn
    nk = kk // tk

    def body(a_ref, b_ref, o_ref, acc_ref):
        kid = pl.program_id(2)

        @pl.when(kid == 0)
        def _():
            acc_ref[...] = jnp.zeros_like(acc_ref)

        acc_ref[...] += dot(a_ref[...].astype(BF16), b_ref[...].astype(BF16))

        @pl.when(kid == nk - 1)
        def _():
            o_ref[...] = acc_ref[...].astype(out_dtype)

    return pl.pallas_call(
        body, name=name, grid=(m // tm, nn // tn, nk),
        in_specs=[a_spec, b_spec], out_specs=BS((tm, tn), lambda i, j, k: (i, j)),
        out_shape=S((m, nn), out_dtype), scratch_shapes=[pltpu.VMEM((tm, tn), F32)],
        compiler_params=_cp("arbitrary", "arbitrary", "arbitrary"),
    )(a, b)


SLAB = 16


def _inproj_bwd_dx(dp, w, x, nw, dres, tm=512, tk=768):
    n, d = x.shape
    nk = dp.shape[1] // tk

    def body(dp_ref, w_ref, x_ref, nw_ref, dr_ref, dx_ref, dnw_ref, acc_ref):
        kid = pl.program_id(1)

        @pl.when((pl.program_id(0) == 0) & (kid == 0))
        def _():
            dnw_ref[...] = jnp.zeros_like(dnw_ref)

        @pl.when(kid == 0)
        def _():
            acc_ref[...] = jnp.zeros_like(acc_ref)

        acc_ref[...] += _dot_nt(dp_ref[...], w_ref[...])

        @pl.when(kid == nk - 1)
        def _():
            def slab(i, carry):
                rows = pl.ds(pl.multiple_of(i * SLAB, SLAB), SLAB)
                dh, xv = acc_ref[rows, :], x_ref[rows, :]
                r = lax.rsqrt(jnp.mean(xv * xv, axis=-1, keepdims=True) + EPS)
                dnw_ref[...] += _fold8(dh * xv * r)
                g = dh * nw_ref[...]
                mm = jnp.mean(g * xv, axis=-1, keepdims=True)
                dx_ref[rows, :] = dr_ref[rows, :] + r * g - xv * (r * r * r * mm)
                return carry

            lax.fori_loop(0, tm // SLAB, slab, 0)

    return pl.pallas_call(
        body, name="inproj_bwd_dx", grid=(n // tm, nk),
        in_specs=[BS((tm, tk), lambda i, k: (i, k)), BS((d, tk), lambda i, k: (0, k)), BS((tm, d), lambda i, k: (i, 0)),
                  BS((1, d), lambda i, k: (0, 0)), BS((tm, d), lambda i, k: (i, 0))],
        out_specs=[BS((tm, d), lambda i, k: (i, 0)), BS((SUB, d), lambda i, k: (0, 0))],
        out_shape=[S((n, d), F32), S((SUB, d), F32)],
        scratch_shapes=[pltpu.VMEM((tm, d), F32)],
        compiler_params=_cp("arbitrary", "arbitrary"),
    )(dp, w, x, nw, dres)


def _rope_tables(pos):
    half = ROT // 2
    inv = THETA ** (-jnp.arange(0, ROT, 2, dtype=F32) / ROT)
    ang = pos.astype(F32)[:, None] * inv
    cos, sin = jnp.cos(ang), jnp.sin(ang)
    n = pos.shape[0]
    one = jnp.ones((n, A_DH - ROT), F32)
    zero = jnp.zeros((n, A_DH - ROT), F32)
    zh = jnp.zeros((n, half), F32)
    c = jnp.concatenate([cos, cos, one], axis=1)
    s1 = jnp.concatenate([-sin, zh, zero], axis=1)
    s2 = jnp.concatenate([zh, sin, zero], axis=1)
    return tuple(jnp.concatenate([t, t], axis=1) for t in (c, s1, s2))


def _half_stat(t):
    lo = _lane(t.shape) < A_DH
    s_lo = jnp.sum(jnp.where(lo, t, 0.0), axis=-1, keepdims=True)
    s_hi = jnp.sum(jnp.where(lo, 0.0, t), axis=-1, keepdims=True)
    return jnp.where(lo, s_lo, s_hi)


def _normrope(x, w, c, s1, s2):
    r = lax.rsqrt(_half_stat(x * x) * (1.0 / A_DH) + EPS)
    xn = x * r * w
    return xn * c + pltpu.roll(xn, LANES - ROT // 2, 1) * s1 + pltpu.roll(xn, ROT // 2, 1) * s2, r


def _normrope_bwd(dy, x, r, w, c, s1, s2):
    dxn = dy * c + pltpu.roll(dy * s1, ROT // 2, 1) + pltpu.roll(dy * s2, LANES - ROT // 2, 1)
    g = dxn * w
    mm = _half_stat(g * x) * (1.0 / A_DH)
    return r * g - x * (r * r * r * mm), dxn * x * r


def _attn_mask(first):
    qi = _subl((ABLK, 2 * ABLK))
    kj = _lane((ABLK, 2 * ABLK))
    dist = qi + ABLK - kj
    return (dist >= 0) & (dist < ABLK) & (jnp.logical_not(first) | (kj >= ABLK))


def _keep_half(x, b):
    lo = _lane(x.shape) < A_DH
    return jnp.where(lo if b == 0 else jnp.logical_not(lo), x, jnp.zeros_like(x))


def _head_operand(x, j):
    a, b = j % 2, (j // 3) % 2
    return _keep_half(x if a == b else pltpu.roll(x, A_DH, 1), b)


def _head_result(x, j):
    a, b = j % 2, (j // 3) % 2
    return _keep_half(x if a == b else pltpu.roll(x, A_DH, 1), a)


def _attn_fwd(p, tabs, qw, kw, sinks, nseq):
    n = p.shape[0]
    nb = n // nseq // ABLK
    cur = lambda b, i: (b * nb + i, 0)
    prv = lambda b, i: (b * nb + jnp.maximum(i - 1, 0), 0)
    colblk = lambda f, w, off: (lambda b, i: (f(b, i)[0], off // w))

    def body(q_ref, za_ref, kc_ref, vc_ref, kp_ref, vp_ref, c_ref, s1_ref, s2_ref, cp_ref, s1p_ref, s2p_ref,
             qw_ref, kw_ref, sink_ref, y_ref, o_ref, lse_ref):
        first = pl.program_id(1) == 0
        tc = (c_ref[...], s1_ref[...], s2_ref[...])
        tp = (cp_ref[...], s1p_ref[...], s2p_ref[...])
        q, kc, kp = q_ref[...], kc_ref[...], kp_ref[...]
        qn = [_normrope(q[:, LANES * b:LANES * (b + 1)], qw_ref[...], *tc)[0] for b in range(A_W // LANES)]
        k2, v2 = [], []
        for b in range(A_KVW // LANES):
            sl = slice(LANES * b, LANES * (b + 1))
            k2.append(jnp.concatenate([_normrope(kp[:, sl], kw_ref[...], *tp)[0],
                                       _normrope(kc[:, sl], kw_ref[...], *tc)[0]], axis=0).astype(BF16))
            v2.append(jnp.concatenate([vp_ref[:, sl], vc_ref[:, sl]], axis=0).astype(BF16))
        valid = _attn_mask(first)
        heads = range(A_HEADS)
        qm = [_head_operand(qn[j // 2], j).astype(BF16) for j in heads]
        s = [jnp.where(valid, _dot_nt(qm[j], k2[j // 6]) * (A_DH ** -0.5), -jnp.inf) for j in heads]
        m = [jnp.maximum(jnp.max(s[j], axis=-1, keepdims=True), sink_ref[j]) for j in heads]
        e = [jnp.exp(s[j] - m[j]) for j in heads]
        den = [jnp.sum(e[j], axis=-1, keepdims=True) + jnp.exp(sink_ref[j] - m[j]) for j in heads]
        outs = [_head_result(_dot((e[j] * (1.0 / den[j])).astype(BF16), v2[j // 6]), j) for j in heads]
        lse = jnp.zeros((ABLK, LANES), F32)
        for j in heads:
            lse = jnp.where(_lane(lse.shape) == j, m[j] + jnp.log(den[j]), lse)
        o = jnp.concatenate([outs[2 * b] + outs[2 * b + 1] for b in range(A_W // LANES)], axis=1)
        za = za_ref[...]
        o_ref[...] = o
        lse_ref[...] = lse
        y_ref[...] = (o * (za * _sigmoid(za))).astype(BF16)

    tab_specs = [BS((ABLK, LANES), cur)] * 3 + [BS((ABLK, LANES), prv)] * 3
    return pl.pallas_call(
        body, name="attn_fwd", grid=(nseq, nb),
        in_specs=[BS((ABLK, A_W), colblk(cur, A_W, P_Q)), BS((ABLK, A_W), colblk(cur, A_W, P_ZA)),
                  BS((ABLK, A_KVW), colblk(cur, A_KVW, P_K)), BS((ABLK, A_KVW), colblk(cur, A_KVW, P_V)),
                  BS((ABLK, A_KVW), colblk(prv, A_KVW, P_K)), BS((ABLK, A_KVW), colblk(prv, A_KVW, P_V))]
        + tab_specs + [BS((1, LANES), lambda b, i: (0, 0))] * 2 + [BS(memory_space=pltpu.SMEM)],
        out_specs=[BS((ABLK, A_W), colblk(cur, A_W, Y_A)), BS((ABLK, A_W), cur), BS((ABLK, LANES), cur)],
        out_shape=[S((n, D_MODEL), BF16), S((n, A_W), F32), S((n, LANES), F32)],
        compiler_params=_cp("arbitrary", "arbitrary"),
    )(p, p, p, p, p, p, *tabs, *tabs, qw, kw, sinks)


def _attn_bwd(p, dy, o, lse, tabs, qw, kw, sinks, nseq):
    n = p.shape[0]
    nb = n // nseq // ABLK
    cur = lambda b, i: (b * nb + jnp.minimum(i, nb - 1), 0)
    prv = lambda b, i: (b * nb + jnp.maximum(i - 1, 0), 0)
    colblk = lambda f, w, off: (lambda b, i: (f(b, i)[0], off // w))

    def body(q_ref, za_ref, kc_ref, vc_ref, kp_ref, vp_ref, dy_ref, o_ref, lse_ref,
             c_ref, s1_ref, s2_ref, cp_ref, s1p_ref, s2p_ref, qw_ref, kw_ref, sink_ref,
             dqza_ref, dkv_ref, dqw_ref, dkw_ref, dsk_ref, tk_ref, tv_ref, ck_ref, cv_ref):
        i = pl.program_id(1)
        first = i == 0
        tc = (c_ref[...], s1_ref[...], s2_ref[...])
        tp = (cp_ref[...], s1p_ref[...], s2p_ref[...])
        nkb = A_KVW // LANES

        @pl.when((pl.program_id(0) == 0) & first)
        def _():
            dqw_ref[...] = jnp.zeros_like(dqw_ref)
            dkw_ref[...] = jnp.zeros_like(dkw_ref)
            dsk_ref[...] = jnp.zeros_like(dsk_ref)

        @pl.when(i < nb)
        def _():
            q, kc, kp = q_ref[...], kc_ref[...], kp_ref[...]
            qn, rq = [], []
            for b in range(A_W // LANES):
                a, r = _normrope(q[:, LANES * b:LANES * (b + 1)], qw_ref[...], *tc)
                qn.append(a)
                rq.append(r)
            k2, v2 = [], []
            for b in range(nkb):
                sl = slice(LANES * b, LANES * (b + 1))
                k2.append(jnp.concatenate([_normrope(kp[:, sl], kw_ref[...], *tp)[0],
                                           _normrope(kc[:, sl], kw_ref[...], *tc)[0]], axis=0).astype(BF16))
                v2.append(jnp.concatenate([vp_ref[:, sl], vc_ref[:, sl]], axis=0).astype(BF16))
            valid = _attn_mask(first)
            za, dy, o, lse = za_ref[...], dy_ref[...], o_ref[...], lse_ref[...]
            sg = _sigmoid(za)
            do = dy * za * sg
            dqza_ref[:, A_W:2 * A_W] = (dy * o * _dsilu(za, sg)).astype(BF16)
            heads = range(A_HEADS)
            blk = lambda x, b: x[:, LANES * b:LANES * (b + 1)]
            qm = [_head_operand(qn[j // 2], j).astype(BF16) for j in heads]
            lj = [_col(lse, j) for j in heads]
            pr = [jnp.exp(jnp.where(valid, _dot_nt(qm[j], k2[j // 6]) * (A_DH ** -0.5), -jnp.inf) - lj[j]) for j in heads]
            dom = [_head_operand(blk(do, j // 2), j).astype(BF16) for j in heads]
            delta = [jnp.sum(_keep_half(blk(do, j // 2) * blk(o, j // 2), j % 2), axis=-1, keepdims=True) for j in heads]
            ds = [(pr[j] * (_dot_nt(dom[j], v2[j // 6]) - delta[j]) * (A_DH ** -0.5)).astype(BF16) for j in heads]
            dqh = [_head_result(_dot(ds[j], k2[j // 6]), j) for j in heads]
            dkh = [_dot_tn(ds[j], qm[j]) for j in heads]
            dvh = [_dot_tn(pr[j].astype(BF16), dom[j]) for j in heads]
            per_blk = A_HEADS // nkb
            dks = [sum(dkh[per_blk * g + 1:per_blk * (g + 1)], dkh[per_blk * g]) for g in range(nkb)]
            dvs = [sum(dvh[per_blk * g + 1:per_blk * (g + 1)], dvh[per_blk * g]) for g in range(nkb)]
            dsk = jnp.zeros((ABLK, LANES), F32)
            for j in heads:
                dsk = dsk + jnp.where(_lane(dsk.shape) == j, -jnp.exp(sink_ref[j] - lj[j]) * delta[j], 0.0)
            dsk_ref[...] += _fold8(dsk)
            dqn = jnp.concatenate([dqh[2 * b] + dqh[2 * b + 1] for b in range(A_W // LANES)], axis=1)
            dqw = jnp.zeros((SUB, LANES), F32)
            dqo = []
            for b in range(A_W // LANES):
                sl = slice(LANES * b, LANES * (b + 1))
                dx, wt = _normrope_bwd(dqn[:, sl], q[:, sl], rq[b], qw_ref[...], *tc)
                dqo.append(dx)
                dqw = dqw + _fold8(wt)
            dqw_ref[...] += dqw
            dqza_ref[:, 0:A_W] = jnp.concatenate(dqo, axis=1).astype(BF16)
            tk_ref[...] = jnp.concatenate(dks, axis=1)
            tv_ref[...] = jnp.concatenate(dvs, axis=1)

        @pl.when(i == nb)
        def _():
            tk_ref[...] = jnp.zeros_like(tk_ref)
            tv_ref[...] = jnp.zeros_like(tv_ref)

        @pl.when(i > 0)
        def _():
            kp = kp_ref[...]
            dkn = ck_ref[...] + tk_ref[0:ABLK, :]
            dkw = jnp.zeros((SUB, LANES), F32)
            dko = []
            for b in range(nkb):
                sl = slice(LANES * b, LANES * (b + 1))
                r = _normrope(kp[:, sl], kw_ref[...], *tp)[1]
                dx, wt = _normrope_bwd(dkn[:, sl], kp[:, sl], r, kw_ref[...], *tp)
                dko.append(dx)
                dkw = dkw + _fold8(wt)
            dkw_ref[...] += dkw
            dkv_ref[:, 0:A_KVW] = jnp.concatenate(dko, axis=1).astype(BF16)
            dkv_ref[:, A_KVW:2 * A_KVW] = (cv_ref[...] + tv_ref[0:ABLK, :]).astype(BF16)

        ck_ref[...] = tk_ref[ABLK:2 * ABLK, :]
        cv_ref[...] = tv_ref[ABLK:2 * ABLK, :]

    tab_specs = [BS((ABLK, LANES), cur)] * 3 + [BS((ABLK, LANES), prv)] * 3
    acc = BS((SUB, LANES), lambda b, i: (0, 0))
    return pl.pallas_call(
        body, name="attn_bwd", grid=(nseq, nb + 1),
        in_specs=[BS((ABLK, A_W), colblk(cur, A_W, P_Q)), BS((ABLK, A_W), colblk(cur, A_W, P_ZA)),
                  BS((ABLK, A_KVW), colblk(cur, A_KVW, P_K)), BS((ABLK, A_KVW), colblk(cur, A_KVW, P_V)),
                  BS((ABLK, A_KVW), colblk(prv, A_KVW, P_K)), BS((ABLK, A_KVW), colblk(prv, A_KVW, P_V)),
                  BS((ABLK, A_W), colblk(cur, A_W, Y_A)), BS((ABLK, A_W), cur), BS((ABLK, LANES), cur)]
        + tab_specs + [BS((1, LANES), lambda b, i: (0, 0))] * 2 + [BS(memory_space=pltpu.SMEM)],
        out_specs=[BS((ABLK, 2 * A_W), cur), BS((ABLK, 2 * A_KVW), prv), acc, acc, acc],
        out_shape=[S((n, P_W), BF16), S((n, 2 * A_KVW), BF16)] + [S((SUB, LANES), F32)] * 3,
        scratch_shapes=[pltpu.VMEM((2 * ABLK, A_KVW), F32)] * 2 + [pltpu.VMEM((ABLK, A_KVW), F32)] * 2,
        compiler_params=_cp("arbitrary", "arbitrary"),
    )(p, p, p, p, p, p, dy, o, lse, *tabs, *tabs, qw, kw, sinks)


def _put_cols(dst, src, col_off, after, tm=512):
    n, w = src.shape

    def body(s_ref, d_in_ref, after_ref, d_ref):
        d_ref[...] = s_ref[...]

    return pl.pallas_call(
        body, name="put_cols", grid=(n // tm,),
        in_specs=[BS((tm, w), lambda i: (i, 0)), BS(memory_space=pl.ANY), BS(memory_space=pl.ANY)],
        out_specs=BS((tm, w), lambda i: (i, col_off // w)),
        out_shape=S(dst.shape, dst.dtype), input_output_aliases={1: 0},
        compiler_params=_cp("arbitrary"),
    )(src, dst, after)


HALO_B = 32


def _layernorm(hc, lnw, lnb):
    mu = jnp.mean(hc, axis=-1, keepdims=True)
    xc = hc - mu
    rstd = lax.rsqrt(jnp.mean(xc * xc, axis=-1, keepdims=True) + EPS)
    xhat = xc * rstd
    return xhat, rstd, xhat * lnw + lnb


def _shifted_copies(buf_ref, sh_ref):
    rows = sh_ref.shape[1]
    for b in range(1, SUB):
        sh_ref[b - 1] = buf_ref[pl.ds(b, rows), :]


def _rows_from(buf_ref, sh_ref, off, rows, cols=slice(None)):
    a, b = divmod(off, SUB)
    if b == 0:
        return buf_ref[pl.ds(SUB * a, rows), cols]
    return sh_ref[b - 1, pl.ds(SUB * a, rows), cols]


def _conf_fwd(p, y, cw, cb, lnw, lnb, pw, pwb, nseq, tm=256):
    n = p.shape[0]
    t = n // nseq
    nt = t // tm
    row = lambda b, i: b * nt + i
    halo = lambda b, i: jnp.maximum((b * t + i * tm) // HALO_B - 1, 0)
    vec = BS((1, B_W), lambda b, i: (0, 0))

    def body(ub_ref, uh_ref, zb_ref, cw_ref, cb_ref, lnw_ref, lnb_ref, pw_ref, pwb_ref, y_in_ref, y_ref, hc_ref, buf_ref, sh_ref):
        ub, uh = ub_ref[...], uh_ref[...]
        hh = uh[:, :B_W] * _sigmoid(uh[:, B_W:])
        buf_ref[0:HALO_B, :] = jnp.where(pl.program_id(1) > 0, hh, 0.0)
        buf_ref[HALO_B:, :] = ub[:, :B_W] * _sigmoid(ub[:, B_W:])
        _shifted_copies(buf_ref, sh_ref)
        hc = jnp.zeros((tm, B_W), F32) + cb_ref[...]
        for k in range(B_K):
            hc = hc + cw_ref[k:k + 1, :] * _rows_from(buf_ref, sh_ref, HALO_B - B_K + 1 + k, tm)
        hc_ref[...] = hc
        ln = _layernorm(hc, lnw_ref[...], lnb_ref[...])[2]
        sw = ln * _sigmoid(ln)
        ob = _dot(sw.astype(BF16), pw_ref[...]) + pwb_ref[...]
        zb = zb_ref[...]
        y_ref[...] = (ob * (zb * _sigmoid(zb))).astype(BF16)

    return pl.pallas_call(
        body, name="conf_fwd", grid=(nseq, nt),
        in_specs=[BS((tm, 2 * B_W), lambda b, i: (row(b, i), P_UB // (2 * B_W))),
                  BS((HALO_B, 2 * B_W), lambda b, i: (halo(b, i), P_UB // (2 * B_W))),
                  BS((tm, B_W), lambda b, i: (row(b, i), P_ZB // B_W)),
                  BS((HALO_B, B_W), lambda b, i: (0, 0)), vec, vec, vec, BS((B_W, B_W), lambda b, i: (0, 0)), vec,
                  BS(memory_space=pl.ANY)],
        out_specs=[BS((tm, B_W), lambda b, i: (row(b, i), Y_B // B_W)), BS((tm, B_W), lambda b, i: (row(b, i), 0))],
        out_shape=[S(y.shape, y.dtype), S((n, B_W), F32)], input_output_aliases={9: 0},
        scratch_shapes=[pltpu.VMEM((HALO_B + tm, B_W), F32), pltpu.VMEM((SUB - 1, HALO_B + tm - SUB, B_W), F32)],
        compiler_params=_cp("arbitrary", "arbitrary"),
    )(p, p, p, cw, cb, lnw, lnb, pw, pwb, y)


def _conf_bwd1(p, dy, dp, hc, lnw, lnb, pw, pwb, tm=256):
    n = p.shape[0]
    vec = BS((1, B_W), lambda i: (0, 0))
    acc = BS((SUB, B_W), lambda i: (0, 0))

    def body(dy_ref, zb_ref, hc_ref, lnw_ref, lnb_ref, pw_ref, pwb_ref, dp_in_ref,
             dzb_ref, dhc_ref, dpw_ref, dpwb_ref, dlnw_ref, dlnb_ref, dcb_ref):
        @pl.when(pl.program_id(0) == 0)
        def _():
            for r in (dpw_ref, dpwb_ref, dlnw_ref, dlnb_ref, dcb_ref):
                r[...] = jnp.zeros_like(r)

        xhat, rstd, ln = _layernorm(hc_ref[...], lnw_ref[...], lnb_ref[...])
        sgl = _sigmoid(ln)
        sw = (ln * sgl).astype(BF16)
        ob = _dot(sw, pw_ref[...]) + pwb_ref[...]
        dy, zb = dy_ref[...], zb_ref[...]
        sgz = _sigmoid(zb)
        dzb_ref[...] = (dy * ob * _dsilu(zb, sgz)).astype(BF16)
        dob = dy * zb * sgz
        dobb = dob.astype(BF16)
        dpwb_ref[...] += _fold8(dob)
        dpw_ref[...] += _dot_tn(sw, dobb)
        dln = _dot_nt(dobb, pw_ref[...]) * _dsilu(ln, sgl)
        dlnw_ref[...] += _fold8(dln * xhat)
        dlnb_ref[...] += _fold8(dln)
        dxh = dln * lnw_ref[...]
        dhc = rstd * (dxh - jnp.mean(dxh, axis=-1, keepdims=True) - xhat * jnp.mean(dxh * xhat, axis=-1, keepdims=True))
        dcb_ref[...] += _fold8(dhc)
        dhc_ref[...] = dhc

    return pl.pallas_call(
        body, name="conf_bwd1", grid=(n // tm,),
        in_specs=[BS((tm, B_W), lambda i: (i, Y_B // B_W)), BS((tm, B_W), lambda i: (i, P_ZB // B_W)),
                  BS((tm, B_W), lambda i: (i, 0)), vec, vec, BS((B_W, B_W), lambda i: (0, 0)), vec,
                  BS(memory_space=pl.ANY)],
        out_specs=[BS((tm, B_W), lambda i: (i, P_ZB // B_W)), BS((tm, B_W), lambda i: (i, 0)),
                   BS((B_W, B_W), lambda i: (0, 0)), acc, acc, acc, acc],
        out_shape=[S(dp.shape, dp.dtype), S((n, B_W), F32), S((B_W, B_W), F32)] + [S((SUB, B_W), F32)] * 4,
        input_output_aliases={7: 0},
        compiler_params=_cp("arbitrary"),
    )(dy, p, hc, lnw, lnb, pw, pwb, dp)


def _conf_bwd2(p, dhc, dp, cw, nseq, tm=256):
    n = p.shape[0]
    t = n // nseq
    nt = t // tm
    row = lambda b, i: b * nt + i
    prev = lambda b, i: jnp.maximum((b * t + i * tm) // HALO_B - 1, 0)
    nxt = lambda b, i: jnp.minimum((b * t + (i + 1) * tm) // HALO_B, n // HALO_B - 1)

    def body(ub_ref, uh_ref, dh_ref, dn_ref, cw_ref, dp_in_ref, dub_ref, dcw_ref, buf_ref, dbuf_ref, sh_ref, dsh_ref):
        i = pl.program_id(1)

        @pl.when((pl.program_id(0) == 0) & (i == 0))
        def _():
            dcw_ref[...] = jnp.zeros_like(dcw_ref)

        uh = uh_ref[...]
        buf_ref[0:HALO_B, :] = jnp.where(i > 0, uh[:, :B_W] * _sigmoid(uh[:, B_W:]), 0.0)
        buf_ref[HALO_B:, :] = ub_ref[:, :B_W] * _sigmoid(ub_ref[:, B_W:])
        dbuf_ref[0:tm, :] = dh_ref[...]
        dbuf_ref[tm:, :] = jnp.where(i < nt - 1, dn_ref[...], 0.0)
        _shifted_copies(buf_ref, sh_ref)
        _shifted_copies(dbuf_ref, dsh_ref)
        for c in range(B_W // LANES):
            cs, gs = slice(LANES * c, LANES * (c + 1)), slice(B_W + LANES * c, B_W + LANES * (c + 1))
            for r0 in range(0, tm, LANES):
                dhc = dh_ref[r0:r0 + LANES, cs]
                dhg = jnp.zeros((LANES, LANES), F32)
                for k in range(B_K):
                    dhg = dhg + cw_ref[k:k + 1, cs] * _rows_from(dbuf_ref, dsh_ref, r0 + B_K - 1 - k, LANES, cs)
                    dcw_ref[SUB * k:SUB * (k + 1), cs] += _fold8(
                        dhc * _rows_from(buf_ref, sh_ref, r0 + HALO_B - B_K + 1 + k, LANES, cs))
                a, sg = ub_ref[r0:r0 + LANES, cs], _sigmoid(ub_ref[r0:r0 + LANES, gs])
                dub_ref[r0:r0 + LANES, cs] = (dhg * sg).astype(BF16)
                dub_ref[r0:r0 + LANES, gs] = (dhg * a * sg * (1.0 - sg)).astype(BF16)

    return pl.pallas_call(
        body, name="conf_bwd2", grid=(nseq, nt),
        in_specs=[BS((tm, 2 * B_W), lambda b, i: (row(b, i), P_UB // (2 * B_W))),
                  BS((HALO_B, 2 * B_W), lambda b, i: (prev(b, i), P_UB // (2 * B_W))),
                  BS((tm, B_W), lambda b, i: (row(b, i), 0)), BS((HALO_B, B_W), lambda b, i: (nxt(b, i), 0)),
                  BS((HALO_B, B_W), lambda b, i: (0, 0)), BS(memory_space=pl.ANY)],
        out_specs=[BS((tm, 2 * B_W), lambda b, i: (row(b, i), P_UB // (2 * B_W))),
                   BS((SUB * B_K, B_W), lambda b, i: (0, 0))],
        out_shape=[S(dp.shape, dp.dtype), S((SUB * B_K, B_W), F32)], input_output_aliases={5: 0},
        scratch_shapes=[pltpu.VMEM((HALO_B + tm, B_W), F32)] * 2 + [pltpu.VMEM((SUB - 1, HALO_B + tm - SUB, B_W), F32)] * 2,
        compiler_params=_cp("arbitrary", "arbitrary"),
    )(p, p, dhc, dhc, cw, dp)


HALO_C = 8
QS = C_DH ** -0.5
NCB = 3 * C_HEADS
CB0 = P_QKV // LANES
ZC0 = P_ZC // LANES
GB, GG = 0, C_HEADS


def _softplus(z):
    return jnp.maximum(z, 0.0) + jnp.log(1.0 + jnp.exp(-jnp.abs(z)))


def _gdn_gates_fwd(p, alog_l, dtb_l, tm=256):
    n = p.shape[0]

    def body(ba_ref, al_ref, db_ref, o_ref):
        blk = ba_ref[...]
        lane = _lane(blk.shape)
        g = jnp.where((lane >= GG) & (lane < GG + C_HEADS), -jnp.exp(al_ref[...]) * _softplus(blk + db_ref[...]), 0.0)
        tri = (_subl((CHUNK, CHUNK)) >= _lane((CHUNK, CHUNK))).astype(F32)
        gc = jnp.concatenate([_dot(tri, g[CHUNK * c:CHUNK * (c + 1)], HI) for c in range(tm // CHUNK)], axis=0)
        o_ref[...] = jnp.where(lane < GG, _sigmoid(blk), gc)

    return pl.pallas_call(
        body, name="gdn_gates_fwd", grid=(n // tm,),
        in_specs=[BS((tm, LANES), lambda i: (i, P_BA // LANES)), BS((1, LANES), lambda i: (0, 0)), BS((1, LANES), lambda i: (0, 0))],
        out_specs=BS((tm, LANES), lambda i: (i, 0)), out_shape=S((n, LANES), F32),
        compiler_params=_cp("arbitrary"),
    )(p, alog_l, dtb_l)


def _gdn_pre_fwd(p, ccw, nseq, tm=256):
    n = p.shape[0]
    t = n // nseq
    nt = t // tm
    row = lambda b, i: b * nt + i
    halo = lambda b, i: jnp.maximum((b * t + i * tm) // HALO_C - 1, 0)

    def body(x_ref, xh_ref, w_ref, xc_ref, o_ref, buf_ref):
        buf_ref[0:HALO_C, :] = jnp.where(pl.program_id(1) > 0, xh_ref[...], 0.0)
        buf_ref[HALO_C:, :] = x_ref[...]
        for c in range(NCB):
            cs = slice(LANES * c, LANES * (c + 1))
            xc = jnp.zeros((tm, LANES), F32)
            for k in range(C_K):
                xc = xc + w_ref[k:k + 1, cs] * buf_ref[pl.ds(HALO_C - C_K + 1 + k, tm), cs]
            xc_ref[:, cs] = xc
            act = xc * _sigmoid(xc)
            if c < 2 * C_HEADS:
                act = act * (lax.rsqrt(jnp.sum(act * act, axis=-1, keepdims=True) + EPS) * (QS if c < C_HEADS else 1.0))
            o_ref[:, cs] = act

    wide = 3 * C_W
    return pl.pallas_call(
        body, name="gdn_pre_fwd", grid=(nseq, nt),
        in_specs=[BS((tm, wide), lambda b, i: (row(b, i), P_QKV // wide)), BS((HALO_C, wide), lambda b, i: (halo(b, i), P_QKV // wide)),
                  BS((SUB, wide), lambda b, i: (0, 0))],
        out_specs=[BS((tm, wide), lambda b, i: (row(b, i), 0))] * 2,
        out_shape=[S((n, wide), F32)] * 2,
        scratch_shapes=[pltpu.VMEM((HALO_C + tm, wide), F32)],
        compiler_params=_cp("arbitrary", "arbitrary"),
    )(p, p, ccw)


def _chunk_common(q, k, gt, gtt, h):
    beta = _col(gt, GB + h)
    gc = _col(gt, GG + h)
    gcr = gtt[GG + h:GG + h + 1, :]
    ii, jj = _subl((CHUNK, CHUNK)), _lane((CHUNK, CHUNK))
    incl, strict = ii >= jj, ii > jj
    dec = jnp.exp(jnp.where(incl, gc - gcr, -jnp.inf))
    kb = k * beta
    kbf = k.astype(BF16)
    a = jnp.where(strict, _dot_nt(kb.astype(BF16), kbf) * dec, 0.0)
    mq = jnp.where(incl, _dot_nt(q.astype(BF16), kbf) * dec, 0.0)
    glast = jnp.sum(jnp.where(_subl(gc.shape) == CHUNK - 1, gc, 0.0), axis=0, keepdims=True)
    return beta, gc, incl, strict, dec, kb, a, mq, glast


def _split(x):
    hi = x.astype(BF16)
    return hi, (x - hi.astype(F32)).astype(BF16)


def _dot3(dot, a, b):
    (ah, al), (bh, bl) = a, b
    return dot(ah, bh) + (dot(ah, bl) + dot(al, bh))


def _unit_lower_inverses(mats):
    eye = (_subl(mats[0].shape) == _lane(mats[0].shape)).astype(F32)
    ms = [-a for a in mats]
    invs = [eye + m for m in ms]
    parts = [_split(m) for m in ms]
    for _ in range(5):
        ms = [_dot3(_dot, s, s) for s in parts]
        parts = [_split(m) for m in ms]
        invs = [inv + _dot3(_dot, _split(inv), s) for inv, s in zip(invs, parts)]
    return invs


def _gdn_chunk_fwd(qkv, gates, p, y, onw, nseq, tt=512):
    n = qkv.shape[0]
    t = n // nseq
    tt = min(tt, t)
    nt = t // tt
    nch = tt // CHUNK

    def body(q_ref, k_ref, v_ref, g_ref, zc_ref, onw_ref, y_in_ref, y_ref, o_ref, u_ref, w_ref, t_ref, ss_ref, s_scr):
        @pl.when(pl.program_id(1) == 0)
        def _():
            s_scr[...] = jnp.zeros_like(s_scr)

        def step(c, carry):
            rows = pl.ds(pl.multiple_of(c * CHUNK, CHUNK), CHUNK)
            gt = g_ref[rows, :]
            gtt = gt.T
            heads = range(C_HEADS)
            hs = [slice(C_DH * h, C_DH * (h + 1)) for h in heads]
            q, k, v = ([r[rows, hs[h]] for h in heads] for r in (q_ref, k_ref, v_ref))
            cm = [_chunk_common(q[h], k[h], gt, gtt, h) for h in heads]
            beta, gc, kb, mq, glast = ([m[i] for m in cm] for i in (0, 1, 5, 7, 8))
            tinv = _unit_lower_inverses([m[6] for m in cm])
            egc = [jnp.exp(g) for g in gc]
            sol = [_dot3(_dot, _split(tinv[h]), _split(jnp.concatenate([v[h] * beta[h], kb[h] * egc[h]], axis=1))) for h in heads]
            sv = [s_scr[h] for h in heads]
            sb = [s.astype(BF16) for s in sv]
            vnb = [(sol[h][:, :C_DH] - _dot(sol[h][:, C_DH:].astype(BF16), sb[h])).astype(BF16) for h in heads]
            o = [_dot((q[h] * egc[h]).astype(BF16), sb[h]) + _dot(mq[h].astype(BF16), vnb[h]) for h in heads]
            for h in heads:
                ss_ref[h, c] = sv[h]
                s_scr[h] = sv[h] * jnp.exp(glast[h]) + _dot_tn((k[h] * jnp.exp(glast[h] - gc[h])).astype(BF16), vnb[h])
            for h in heads:
                o_ref[rows, hs[h]] = o[h]
                u_ref[rows, hs[h]] = sol[h][:, :C_DH]
                w_ref[rows, hs[h]] = sol[h][:, C_DH:]
                t_ref[rows, hs[h]] = jnp.concatenate([tinv[h], jnp.zeros_like(tinv[h])], axis=1)
                zc = zc_ref[rows, hs[h]]
                r = lax.rsqrt(jnp.mean(o[h] * o[h], axis=-1, keepdims=True) + EPS)
                y_ref[rows, hs[h]] = (o[h] * r * onw_ref[...] * (zc * _sigmoid(zc))).astype(BF16)
            return carry

        lax.fori_loop(0, nch, step, 0, unroll=4 if nch % 4 == 0 else 1)

    row = lambda b, i: b * nt + i
    wb = lambda col: BS((tt, C_W), lambda b, i: (row(b, i), col))
    return pl.pallas_call(
        body, name="gdn_chunk_fwd", grid=(nseq, nt),
        in_specs=[wb(0), wb(1), wb(2), BS((tt, LANES), lambda b, i: (row(b, i), 0)), wb(P_ZC // C_W),
                  BS((1, LANES), lambda b, i: (0, 0)), BS(memory_space=pl.ANY)],
        out_specs=[wb(Y_C // C_W), wb(0), wb(0), wb(0), wb(0),
                   BS((None, C_HEADS, nch, C_DH, C_DH), lambda b, i: (b, 0, i, 0, 0))],
        out_shape=[S(y.shape, y.dtype)] + [S((n, C_W), F32)] * 4 + [S((nseq, C_HEADS, t // CHUNK, C_DH, C_DH), F32)],
        input_output_aliases={6: 0},
        scratch_shapes=[pltpu.VMEM((C_HEADS, C_DH, C_DH), F32)],
        compiler_params=_cp("arbitrary", "arbitrary"),
    )(qkv, qkv, qkv, gates, p, onw, y)


def _gdn_chunk_bwd(qkv, gates, p, dy, dp, onw, o, u, w, tinv, ss, nseq, tt=256):
    n = qkv.shape[0]
    t = n // nseq
    tt = min(tt, t)
    nt = t // tt
    nch = tt // CHUNK

    def body(q_ref, k_ref, v_ref, g_ref, zc_ref, onw_ref, o_ref, dy_ref, u_ref, w_ref, t_ref, ss_ref, dp_in_ref,
             dzc_ref, dqkv_ref, dg_ref, donw_ref, ds_scr):
        @pl.when(pl.program_id(1) == 0)
        def _():
            ds_scr[...] = jnp.zeros_like(ds_scr)

        @pl.when((pl.program_id(0) == 0) & (pl.program_id(1) == 0))
        def _():
            donw_ref[...] = jnp.zeros_like(donw_ref)

        def rsum(x):
            return jnp.sum(x, axis=-1, keepdims=True)

        def step(ci, carry):
            c = nch - 1 - ci
            rows = pl.ds(pl.multiple_of(c * CHUNK, CHUNK), CHUNK)
            gt = g_ref[rows, :]
            gtt = gt.T
            live = [head(c, rows, gt, gtt, h) for h in range(C_HEADS)]
            while live:
                live = [g for g in live if next(g, False)]
            return carry

        def head(c, rows, gt, gtt, h):
            hs = slice(C_DH * h, C_DH * (h + 1))
            q, k, v = q_ref[rows, hs], k_ref[rows, hs], v_ref[rows, hs]
            zc, o, dy, u, w = zc_ref[rows, hs], o_ref[rows, hs], dy_ref[rows, hs], u_ref[rows, hs], w_ref[rows, hs]
            tm_ = t_ref[rows, hs][:, 0:CHUNK]
            sv, dsv = ss_ref[h, c], ds_scr[h]
            sb, dsb = sv.astype(BF16), dsv.astype(BF16)
            sg = _sigmoid(zc)
            r = lax.rsqrt(jnp.mean(o * o, axis=-1, keepdims=True) + EPS)
            on = o * r
            ow = onw_ref[...]
            dzc_ref[rows, hs] = (dy * on * ow * _dsilu(zc, sg)).astype(BF16)
            t1 = dy * zc * sg
            donw_ref[...] += _fold8(t1 * on)
            don = t1 * ow
            do = r * (don - on * jnp.mean(don * on, axis=-1, keepdims=True))
            dob = do.astype(BF16)
            yield True
            beta, gc, incl, strict, dec, kb, a, mq, glast = _chunk_common(q, k, gt, gtt, h)
            egc = jnp.exp(gc)
            gl = jnp.exp(glast)
            ekd = jnp.exp(glast - gc)
            wb = w.astype(BF16)
            vnb = (u - _dot(wb, sb)).astype(BF16)
            qg = q * egc
            yield True
            dvn = _dot_tn(mq.astype(BF16), dob) + _dot((k * ekd).astype(BF16), dsb)
            dvnb = dvn.astype(BF16)
            dqg = _dot_nt(dob, sb)
            yield True
            dmq = jnp.where(incl, _dot_nt(dob, vnb), 0.0)
            dkd = _dot_nt(vnb, dsb)
            dgl = jnp.sum(rsum(dsv * sv), axis=0, keepdims=True)
            dw = -_dot_nt(dvnb, sb)
            yield True
            ds_scr[h] = gl * dsv + _dot_tn(qg.astype(BF16), dob) - _dot_tn(wb, dvnb)
            db = _dot3(_dot_tn, _split(tm_), _split(jnp.concatenate([dvn, dw], axis=1)))
            dbv, dbk = db[:, :C_DH], db[:, C_DH:]
            yield True
            da = -jnp.where(strict, _dot3(_dot_nt, _split(dbv), _split(u)) + _dot3(_dot_nt, _split(dbk), _split(w)), 0.0)
            yield True
            e = da * a + dmq * mq
            dgc = rsum(e) - rsum(e.T)
            dgb, dhb, kbf = (da * dec).astype(BF16), (dmq * dec).astype(BF16), k.astype(BF16)
            dkb = _dot(dgb, kbf)
            tk = rsum(dbk * k)
            rk = rsum(dkd * k) * ekd
            dq = _dot(dhb, kbf) + egc * dqg
            dk = _dot_tn(dgb, kb.astype(BF16)) + _dot_tn(dhb, q.astype(BF16)) + beta * (egc * dbk + dkb) + ekd * dkd
            dbeta = rsum(dbv * v) + tk * egc + rsum(dkb * k)
            dgc = dgc + tk * beta * egc + egc * rsum(dqg * q) - rk
            dglast = jnp.sum(rk, axis=0, keepdims=True) + dgl * gl
            dgc = dgc + jnp.where(_subl(dgc.shape) == CHUNK - 1, dglast, 0.0)
            dqkv_ref[0, rows, hs] = dq
            dqkv_ref[1, rows, hs] = dk
            dqkv_ref[2, rows, hs] = beta * dbv
            lane = _lane((CHUNK, LANES))
            dg_ref[h, rows, :] = jnp.where(lane == 0, dbeta, jnp.where(lane == 1, dgc, 0.0))

        lax.fori_loop(0, nch, step, 0, unroll=2)

    row = lambda b, i: b * nt + nt - 1 - i
    wb = lambda col: BS((tt, C_W), lambda b, i: (row(b, i), col))
    return pl.pallas_call(
        body, name="gdn_chunk_bwd", grid=(nseq, nt),
        in_specs=[wb(0), wb(1), wb(2), BS((tt, LANES), lambda b, i: (row(b, i), 0)), wb(P_ZC // C_W),
                  BS((1, LANES), lambda b, i: (0, 0)), wb(0), wb(Y_C // C_W), wb(0), wb(0), wb(0),
                  BS((None, C_HEADS, nch, C_DH, C_DH), lambda b, i: (b, 0, nt - 1 - i, 0, 0)), BS(memory_space=pl.ANY)],
        out_specs=[wb(P_ZC // C_W), BS((3, tt, C_W), lambda b, i: (0, row(b, i), 0)),
                   BS((C_HEADS, tt, LANES), lambda b, i: (0, row(b, i), 0)), BS((SUB, LANES), lambda b, i: (0, 0))],
        out_shape=[S(dp.shape, dp.dtype), S((3, n, C_W), F32), S((C_HEADS, n, LANES), F32), S((SUB, LANES), F32)],
        input_output_aliases={12: 0},
        scratch_shapes=[pltpu.VMEM((C_HEADS, C_DH, C_DH), F32)],
        compiler_params=_cp("arbitrary", "arbitrary"),
    )(qkv, qkv, qkv, gates, p, onw, o, dy, u, w, tinv, ss, dp)


def _gdn_gates_bwd(dgate, p, alog_l, dtb_l, dp, tm=256):
    n = p.shape[0]
    acc = BS((SUB, LANES), lambda i: (0, 0))

    def body(dg_ref, ba_ref, al_ref, db_ref, dp_in_ref, dba_ref, dal_ref, ddb_ref):
        @pl.when(pl.program_id(0) == 0)
        def _():
            dal_ref[...] = jnp.zeros_like(dal_ref)
            ddb_ref[...] = jnp.zeros_like(ddb_ref)

        blk = ba_ref[...]
        lane = _lane(blk.shape)
        dbeta = jnp.zeros_like(blk)
        dgc = jnp.zeros_like(blk)
        for h in range(C_HEADS):
            dbeta = dbeta + jnp.where(lane == GB + h, _col(dg_ref[h], 0), 0.0)
            dgc = dgc + jnp.where(lane == GG + h, _col(dg_ref[h], 1), 0.0)
        tri = (_subl((CHUNK, CHUNK)) <= _lane((CHUNK, CHUNK))).astype(F32)
        dg = jnp.concatenate([_dot(tri, dgc[CHUNK * c:CHUNK * (c + 1)], HI) for c in range(tm // CHUNK)], axis=0)
        beta = _sigmoid(blk)
        z = blk + db_ref[...]
        ea = jnp.exp(al_ref[...])
        isg = (lane >= GG) & (lane < GG + C_HEADS)
        dz = jnp.where(isg, -dg * ea * _sigmoid(z), 0.0)
        dal_ref[...] += _fold8(jnp.where(isg, -dg * ea * _softplus(z), 0.0))
        ddb_ref[...] += _fold8(dz)
        out = jnp.where(lane < GG, dbeta * beta * (1.0 - beta), dz)
        dba_ref[...] = jnp.concatenate([out, jnp.zeros_like(out)], axis=1).astype(BF16)

    return pl.pallas_call(
        body, name="gdn_gates_bwd", grid=(n // tm,),
        in_specs=[BS((C_HEADS, tm, LANES), lambda i: (0, i, 0)), BS((tm, LANES), lambda i: (i, P_BA // LANES)),
                  BS((1, LANES), lambda i: (0, 0)), BS((1, LANES), lambda i: (0, 0)), BS(memory_space=pl.ANY)],
        out_specs=[BS((tm, 2 * LANES), lambda i: (i, P_BA // (2 * LANES))), acc, acc],
        out_shape=[S(dp.shape, dp.dtype), S((SUB, LANES), F32), S((SUB, LANES), F32)],
        input_output_aliases={4: 0},
        compiler_params=_cp("arbitrary"),
    )(dgate, p, alog_l, dtb_l, dp)


def _gdn_pre_bwd(p, dqkv, xc, dp, ccw, nseq, tm=256):
    n = p.shape[0]
    t = n // nseq
    nt = t // tm
    wide = 3 * C_W
    row = lambda b, i: b * nt + i
    prev = lambda b, i: jnp.maximum((b * t + i * tm) // HALO_C - 1, 0)
    nxt = lambda b, i: jnp.minimum((b * t + (i + 1) * tm) // HALO_C, n // HALO_C - 1)

    def d_conv_out(d, xc, part):
        sg = _sigmoid(xc)
        act = xc * sg
        if part < 2:
            cs = QS if part == 0 else 1.0
            rn = lax.rsqrt(jnp.sum(act * act, axis=-1, keepdims=True) + EPS)
            d = cs * rn * d - act * (cs * rn * rn * rn * jnp.sum(d * act, axis=-1, keepdims=True))
        return d * _dsilu(xc, sg)

    def body(x_ref, xh_ref, d_ref, dn_ref, xc_ref, xn_ref, w_ref, dp_in_ref, dx_ref, dw_ref, buf_ref, dbuf_ref):
        i = pl.program_id(1)

        @pl.when((pl.program_id(0) == 0) & (i == 0))
        def _():
            dw_ref[...] = jnp.zeros_like(dw_ref)

        buf_ref[0:HALO_C, :] = jnp.where(i > 0, xh_ref[...], 0.0)
        buf_ref[HALO_C:, :] = x_ref[...]
        for c in range(NCB):
            cs = slice(LANES * c, LANES * (c + 1))
            part, hd = divmod(c, C_HEADS)
            hs = slice(LANES * hd, LANES * (hd + 1))
            d = d_conv_out(d_ref[part, :, hs], xc_ref[:, cs], part)
            dbuf_ref[0:tm, cs] = d
            dbuf_ref[tm:, cs] = jnp.where(i < nt - 1, d_conv_out(dn_ref[part, :, hs], xn_ref[:, cs], part), 0.0)
            dx = jnp.zeros((tm, LANES), F32)
            for k in range(C_K):
                dx = dx + w_ref[k:k + 1, cs] * dbuf_ref[pl.ds(C_K - 1 - k, tm), cs]
                dw_ref[SUB * k:SUB * (k + 1), cs] += _fold8(d * buf_ref[pl.ds(HALO_C - C_K + 1 + k, tm), cs])
            dx_ref[:, cs] = dx.astype(BF16)

    return pl.pallas_call(
        body, name="gdn_pre_bwd", grid=(nseq, nt),
        in_specs=[BS((tm, wide), lambda b, i: (row(b, i), P_QKV // wide)), BS((HALO_C, wide), lambda b, i: (prev(b, i), P_QKV // wide)),
                  BS((3, tm, C_W), lambda b, i: (0, row(b, i), 0)), BS((3, HALO_C, C_W), lambda b, i: (0, nxt(b, i), 0)),
                  BS((tm, wide), lambda b, i: (row(b, i), 0)), BS((HALO_C, wide), lambda b, i: (nxt(b, i), 0)),
                  BS((SUB, wide), lambda b, i: (0, 0)), BS(memory_space=pl.ANY)],
        out_specs=[BS((tm, wide), lambda b, i: (row(b, i), P_QKV // wide)), BS((SUB * C_K, wide), lambda b, i: (0, 0))],
        out_shape=[S(dp.shape, dp.dtype), S((SUB * C_K, wide), F32)], input_output_aliases={7: 0},
        scratch_shapes=[pltpu.VMEM((HALO_C + tm, wide), F32)] * 2,
        compiler_params=_cp("arbitrary", "arbitrary"),
    )(p, p, dqkv, dqkv, xc, xc, ccw, dp)


ANY = BS(memory_space=pl.ANY)


def _my_pos():
    return lax.axis_index("x"), lax.axis_index("y"), lax.axis_index("c")


def _dev_index(dev):
    return 4 * dev[0] + 2 * dev[1] + dev[2]


def _all_gather(shards, after=None):
    nk = len(shards)

    tail = [] if after is None else [after]

    def body(*refs):
        ins, outs = refs[:nk], refs[nk + len(tail):2 * nk + len(tail)]
        send, recv, loc = refs[2 * nk + len(tail):]
        x, y, c = _my_pos()
        me, sib = (x, y, c), (x, y, 1 - c)
        chips = [(1 - x, y), (x, 1 - y), (1 - x, 1 - y)]

        def rows(t, dev):
            r = ins[t].shape[0]
            return outs[t].at[pl.ds(pl.multiple_of(_dev_index(dev) * r, SUB), r), :]

        def copy(t, k, block, to, src=None):
            return pltpu.make_async_remote_copy(
                src_ref=rows(t, block) if src is None else src, dst_ref=rows(t, block),
                send_sem=send.at[t, k], recv_sem=recv.at[t, k], device_id=to, device_id_type=MESH)

        mine = [pltpu.make_async_copy(ins[t], rows(t, me), loc.at[t]) for t in range(nk)]
        for cp in mine:
            cp.start()
        first = []
        for t in range(nk):
            first.append(copy(t, 0, me, sib, src=ins[t]))
            first += [copy(t, 1 + j, me, (*chip, c), src=ins[t]) for j, chip in enumerate(chips)]
        for cp in first:
            cp.start()
        passed = []
        for j, chip in enumerate(chips):
            for t in range(nk):
                copy(t, 1 + j, (*chip, c), me).wait_recv()
                cp = copy(t, 4 + j, (*chip, c), sib)
                cp.start()
                passed.append(cp)
        for t in range(nk):
            copy(t, 0, sib, me).wait_recv()
            for j, chip in enumerate(chips):
                copy(t, 4 + j, (*chip, 1 - c), me).wait_recv()
        for cp in first + passed:
            cp.wait_send()
        for cp in mine:
            cp.wait()

    return pl.pallas_call(
        body, name="all_gather", in_specs=[ANY] * (nk + len(tail)), out_specs=[ANY] * nk,
        out_shape=[S((N_DEV * a.shape[0], a.shape[1]), a.dtype) for a in shards],
        scratch_shapes=[pltpu.SemaphoreType.DMA((nk, 7)), pltpu.SemaphoreType.DMA((nk, 7)), pltpu.SemaphoreType.DMA((nk,))],
    )(*shards, *tail)


SEM = BS(memory_space=pltpu.SEMAPHORE)
HBM = BS(memory_space=pltpu.HBM)
EFFECT = pltpu.SideEffectType.DATAFLOW_SIDE_EFFECTING


def _peers(x, y, c):
    return [((1 - x) if k & 4 else x, (1 - y) if k & 2 else y, (1 - c) if k & 1 else c) for k in range(1, N_DEV)]


def _exchange_copy(kind, src, land, send, recv, t, k, peer, me, arriving):
    frm = peer if arriving else me
    if kind == "gather":
        r = src.shape[0]
        s_ref = src
        d_ref = land.at[pl.ds(pl.multiple_of(_dev_index(frm) * r, SUB), r), :]
    else:
        r = src.shape[0] // N_DEV
        s_ref = src.at[pl.ds(pl.multiple_of(_dev_index(peer) * r, SUB), r), :]
        d_ref = land.at[_dev_index(frm)]
    sem = t * (N_DEV - 1) + k
    return pltpu.make_async_remote_copy(src_ref=s_ref, dst_ref=d_ref, send_sem=send.at[sem], recv_sem=recv.at[sem],
                                        device_id=peer, device_id_type=MESH)


def _own_copy(kind, src, land, own, t, me):
    if kind == "gather":
        r = src.shape[0]
        return pltpu.make_async_copy(src, land.at[pl.ds(pl.multiple_of(_dev_index(me) * r, SUB), r), :], own.at[t])
    r = src.shape[0] // N_DEV
    return pltpu.make_async_copy(src.at[pl.ds(pl.multiple_of(_dev_index(me) * r, SUB), r), :], land.at[_dev_index(me)], own.at[t])


def _exchange_start(kind, srcs, after, name):
    nk = len(srcs)
    if kind == "gather":
        lands = [lax.empty((N_DEV * a.shape[0], a.shape[1]), a.dtype) for a in srcs]
    else:
        lands = [lax.empty((N_DEV, a.shape[0] // N_DEV, a.shape[1]), a.dtype) for a in srcs]

    def body(*refs):
        src, land = refs[:nk], refs[nk:2 * nk]
        send, recv, own = refs[2 * nk + 1], refs[2 * nk + 2], refs[2 * nk + 3]
        token = refs[-1]
        x, y, c = _my_pos()
        me = (x, y, c)
        for t in range(nk):
            _own_copy(kind, src[t], land[t], own, t, me).start()
            for k, peer in enumerate(_peers(x, y, c)):
                _exchange_copy(kind, src[t], land[t], send, recv, t, k, peer, me, False).start()
        token[...] = jnp.zeros_like(token)

    hbm = lambda a: pltpu.HBM(a.shape, a.dtype)
    out = pl.pallas_call(
        body, name=name,
        out_shape=(pltpu.SemaphoreType.DMA((nk * (N_DEV - 1),)), pltpu.SemaphoreType.DMA((nk * (N_DEV - 1),)),
                   pltpu.SemaphoreType.DMA((nk,)), *[hbm(a) for a in srcs], *[hbm(a) for a in lands], S((SUB, LANES), F32)),
        in_specs=[HBM] * (2 * nk) + [ANY],
        out_specs=(SEM, SEM, SEM, *[HBM] * (2 * nk), BS(memory_space=pltpu.VMEM)),
        input_output_aliases={i: 3 + i for i in range(2 * nk)},
        compiler_params=pltpu.CompilerParams(has_side_effects=EFFECT),
    )(*[pltpu.with_memory_space_constraint(a, pltpu.HBM) for a in (*srcs, *lands)], after)
    return dict(kind=kind, nk=nk, send=out[0], recv=out[1], own=out[2], srcs=out[3:3 + nk], lands=out[3 + nk:3 + 2 * nk],
                token=out[-1])


def _exchange_wait(ex, after, name):
    kind, nk = ex["kind"], ex["nk"]

    def body(*refs):
        src, land = refs[:nk], refs[nk:2 * nk]
        send, recv, own = refs[2 * nk], refs[2 * nk + 1], refs[2 * nk + 2]
        x, y, c = _my_pos()
        me = (x, y, c)
        for t in range(nk):
            _own_copy(kind, src[t], land[t], own, t, me).wait()
            for k, peer in enumerate(_peers(x, y, c)):
                _exchange_copy(kind, src[t], land[t], send, recv, t, k, peer, me, False).wait_send()
                _exchange_copy(kind, src[t], land[t], send, recv, t, k, peer, me, True).wait_recv()

    hbm = lambda a: pltpu.HBM(a.shape, a.dtype)
    out = pl.pallas_call(
        body, name=name,
        out_shape=(*[hbm(a) for a in ex["srcs"]], *[hbm(a) for a in ex["lands"]]),
        in_specs=[HBM] * (2 * nk) + [SEM, SEM, SEM, ANY], out_specs=tuple([HBM] * (2 * nk)),
        input_output_aliases={i: i for i in range(2 * nk)},
        compiler_params=pltpu.CompilerParams(has_side_effects=EFFECT),
    )(*ex["srcs"], *ex["lands"], ex["send"], ex["recv"], ex["own"], after)
    return list(out[nk:])


BLOCK_BYTES = 4 << 20


def _row_tile(rows, row_bytes, align):
    best = align
    for tr in range(align, rows + 1, align):
        if rows % tr == 0 and tr * row_bytes <= BLOCK_BYTES:
            best = tr
    return best


def _sum8(a):
    _, r, w = a.shape
    tr = _row_tile(r, N_DEV * w * a.dtype.itemsize, 32 // a.dtype.itemsize)

    def body(a_ref, o_ref):
        acc = a_ref[0].astype(F32)
        for d in range(1, N_DEV):
            acc = acc + a_ref[d].astype(F32)
        o_ref[...] = acc

    return pl.pallas_call(
        body, name="sum8", grid=(r // tr,), in_specs=[BS((N_DEV, tr, w), lambda i: (0, i, 0))],
        out_specs=BS((tr, w), lambda i: (i, 0)), out_shape=S((r, w), F32), compiler_params=_cp("arbitrary"),
    )(a)


def _adamw(w, g, m, v):
    r, c = w.shape
    tr = _row_tile(r, c * 4 * 2, SUB)

    def body(w_ref, g_ref, m_ref, v_ref, d_ref, mo_ref, vo_ref):
        d_ref[...], mo_ref[...], vo_ref[...] = _adam_update(w_ref[...], g_ref[...], m_ref[...], v_ref[...])

    blk = BS((tr, c), lambda i: (i, 0))
    return pl.pallas_call(
        body, name="adamw", grid=(r // tr,), in_specs=[blk] * 4, out_specs=[blk] * 3,
        out_shape=[S((r, c), F32)] * 3, compiler_params=_cp("arbitrary"),
    )(w, g, m, v)


def _sum8_t(a, tc=256):
    _, r, w = a.shape

    def body(a_ref, o_ref):
        acc = a_ref[0].astype(F32)
        for d in range(1, N_DEV):
            acc = acc + a_ref[d].astype(F32)
        o_ref[...] = acc.T

    return pl.pallas_call(
        body, name="sum8_t", grid=(w // tc,), in_specs=[BS((N_DEV, r, tc), lambda j: (0, 0, j))],
        out_specs=BS((tc, r), lambda j: (j, 0)), out_shape=S((w, r), F32), compiler_params=_cp("arbitrary"),
    )(a)


def _rows_view(a):
    nl, r, c = a.shape
    assert nl == 2
    return a.transpose(2, 0, 1).reshape(c, nl, r // LANES, LANES).transpose(0, 2, 1, 3).reshape(-1, LANES)


def _rows_view_back(a, shape):
    nl, r, c = shape
    return a.reshape(c, r // LANES, nl, LANES).transpose(0, 2, 1, 3).reshape(c, nl, r).transpose(1, 2, 0)


def _adamw_rows(w, g, m, v, tr=2048):
    n = w.shape[0]

    def body(w_ref, g_ref, m_ref, v_ref, d_ref, mo_ref, vo_ref):
        d_ref[...], mo_ref[...], vo_ref[...] = _adam_update(w_ref[...], g_ref[...], m_ref[...], v_ref[...])

    blk = BS((tr, LANES), lambda i: (i, 0))
    return pl.pallas_call(
        body, name="adamw_rows", grid=(pl.cdiv(n, tr),), in_specs=[blk] * 4, out_specs=[blk] * 3,
        out_shape=[S((n, LANES), F32)] * 3, compiler_params=_cp("arbitrary"),
    )(w, g, m, v)


def _adam_update(w, g, m, v):
    m2 = ADAM_B1 * m + (1.0 - ADAM_B1) * g
    v2 = ADAM_B2 * v + (1.0 - ADAM_B2) * (g * g)
    m_hat = m2 / (1.0 - ADAM_B1 ** ADAM_STEP)
    v_hat = v2 / (1.0 - ADAM_B2 ** ADAM_STEP)
    return -ADAM_LR * (m_hat / (jnp.sqrt(v_hat) + ADAM_EPS) + ADAM_WD * w), m2, v2


def _adamw_layer(w, g, m, v, l, prev):
    nl, r, c = w.shape
    tr = _row_tile(r, c * 4 * 2, SUB)

    def body(w_ref, g_ref, m_ref, v_ref, *refs):
        go_ref, d_ref, mo_ref, vo_ref = refs[-4:]
        gv = g_ref[...]
        go_ref[...] = gv
        d_ref[...], mo_ref[...], vo_ref[...] = _adam_update(w_ref[...], gv, m_ref[...], v_ref[...])

    slot = BS((None, tr, c), lambda i: (l, i, 0))
    keep = [] if prev is None else [ANY] * 4
    return pl.pallas_call(
        body, name="adamw_layer", grid=(r // tr,), in_specs=[slot, BS((tr, c), lambda i: (i, 0)), slot, slot] + keep,
        out_specs=[slot] * 4, out_shape=[S((nl, r, c), F32)] * 4,
        input_output_aliases={} if prev is None else {4 + i: i for i in range(4)},
        compiler_params=_cp("arbitrary"),
    )(w, g, m, v, *(prev or ()))


def _blob(arrays):
    flat = jnp.concatenate([a.reshape(-1) for a in arrays])
    rows = -(-flat.shape[0] // (SUB * LANES)) * SUB
    return jnp.pad(flat, (0, rows * LANES - flat.shape[0])).reshape(rows, LANES)


def _unblob(blob, shapes, lead=()):
    flat = blob.reshape(lead + (-1,))
    out, off = [], 0
    for s in shapes:
        size = math.prod(s)
        out.append(flat[..., off:off + size].reshape(lead + tuple(s)))
        off += size
    return out


def _y_rows(w):
    return jnp.concatenate([w[0:A_W], w[A_W + B_W:], w[A_W:A_W + B_W]], axis=0)


def _y_rows_back(g):
    return jnp.concatenate([g[0:A_W], g[A_W + C_W:], g[A_W:A_W + C_W]], axis=0)


SMALL = ("norm_w", "q_norm_w", "k_norm_w", "sinks", "b_conv_b", "b_ln_w", "b_ln_b", "b_pw_b", "c_a_log", "c_dt_bias",
         "c_onorm_w", "b_conv_w", "c_conv_w")
ORDER = ("norm_w", "w_in", "q_norm_w", "k_norm_w", "sinks", "b_conv_w", "b_conv_b", "b_ln_w", "b_ln_b", "b_pw_w", "b_pw_b",
         "c_conv_w", "c_a_log", "c_dt_bias", "c_onorm_w", "w_out")


def kernel(x, positions, norm_w, w_in, q_norm_w, k_norm_w, sinks, b_conv_w, b_conv_b, b_ln_w, b_ln_b, b_pw_w, b_pw_b, c_conv_w, c_a_log, c_dt_bias, c_onorm_w, w_out, loss_target, m_norm_w, m_w_in, m_q_norm_w, m_k_norm_w, m_sinks, m_b_conv_w, m_b_conv_b, m_b_ln_w, m_b_ln_b, m_b_pw_w, m_b_pw_b, m_c_conv_w, m_c_a_log, m_c_dt_bias, m_c_onorm_w, m_w_out, v_norm_w, v_w_in, v_q_norm_w, v_k_norm_w, v_sinks, v_b_conv_w, v_b_conv_b, v_b_ln_w, v_b_ln_b, v_b_pw_w, v_b_pw_b, v_c_conv_w, v_c_a_log, v_c_dt_bias, v_c_onorm_w, v_w_out):
    W = dict(norm_w=norm_w, w_in=w_in, q_norm_w=q_norm_w, k_norm_w=k_norm_w, sinks=sinks, b_conv_w=b_conv_w, b_conv_b=b_conv_b,
             b_ln_w=b_ln_w, b_ln_b=b_ln_b, b_pw_w=b_pw_w, b_pw_b=b_pw_b, c_conv_w=c_conv_w, c_a_log=c_a_log,
             c_dt_bias=c_dt_bias, c_onorm_w=c_onorm_w, w_out=w_out)
    M = dict(norm_w=m_norm_w, w_in=m_w_in, q_norm_w=m_q_norm_w, k_norm_w=m_k_norm_w, sinks=m_sinks, b_conv_w=m_b_conv_w,
             b_conv_b=m_b_conv_b, b_ln_w=m_b_ln_w, b_ln_b=m_b_ln_b, b_pw_w=m_b_pw_w, b_pw_b=m_b_pw_b, c_conv_w=m_c_conv_w,
             c_a_log=m_c_a_log, c_dt_bias=m_c_dt_bias, c_onorm_w=m_c_onorm_w, w_out=m_w_out)
    V = dict(norm_w=v_norm_w, w_in=v_w_in, q_norm_w=v_q_norm_w, k_norm_w=v_k_norm_w, sinks=v_sinks, b_conv_w=v_b_conv_w,
             b_conv_b=v_b_conv_b, b_ln_w=v_b_ln_w, b_ln_b=v_b_ln_b, b_pw_w=v_b_pw_w, b_pw_b=v_b_pw_b, c_conv_w=v_c_conv_w,
             c_a_log=v_c_a_log, c_dt_bias=v_c_dt_bias, c_onorm_w=v_c_onorm_w, w_out=v_w_out)
    nseq, t, d = x.shape
    n = nseq * t
    tr = min(256, t)
    tmm = min(512, n)
    tmw = min(1024, n)
    tkk = min(2048, n)
    me = _dev_index(_my_pos())
    xs = [x.reshape(n, d)]
    tgt = loss_target.reshape(n, d)
    tabs = _rope_tables(positions.reshape(n))

    win_p = _pack_cols(w_in).astype(BF16)
    wout_b = w_out.astype(BF16)
    sharded_small = (b_pw_w, b_conv_w, c_conv_w)
    g_win0, g_small = _all_gather([win_p[0], _blob(sharded_small)])
    win = [g_win0]
    later = _exchange_start("gather", [win_p[1], wout_b[0], wout_b[1]], g_small, "gather_start")
    pw_all, cw_all, ccw_all = _unblob(g_small, [a.shape for a in sharded_small], lead=(N_DEV,))
    pw_all = pw_all.transpose(1, 0, 2, 3).reshape(DEPTH, B_W, B_W).astype(BF16)
    cw_all = jnp.pad(cw_all.transpose(1, 2, 0, 3).reshape(DEPTH, B_K, B_W), ((0, 0), (0, HALO_B - B_K), (0, 0)))
    ccw_all = jnp.pad(ccw_all.transpose(1, 2, 0, 3).reshape(DEPTH, C_K, 3 * C_W), ((0, 0), (0, SUB - C_K), (0, 0)))
    qw_all, kw_all = jnp.tile(q_norm_w, (1, 2)), jnp.tile(k_norm_w, (1, 2))
    lanes6 = lambda a: jnp.zeros((DEPTH, LANES), F32).at[:, GG:GG + C_HEADS].set(a)
    alog_all, dtb_all = lanes6(c_a_log), lanes6(c_dt_bias)

    def layer_params(l):
        row = lambda a: a[l][None]
        return dict(
            nw=row(norm_w), qw=row(qw_all), kw=row(kw_all), sinks=sinks[l], cw=cw_all[l], cb=row(b_conv_b), lnw=row(b_ln_w),
            lnb=row(b_ln_b), pw=pw_all[l], pwb=row(b_pw_b), ccw=ccw_all[l], alog=row(alog_all), dtb=row(dtb_all),
            onw=row(c_onorm_w))

    saved = []
    for l in range(DEPTH):
        q = layer_params(l)
        nw = q["nw"] + later["token"][0:1, 0:1] if l == 0 else q["nw"]
        p, h = _inproj(xs[l], nw, win[l], tm=tmw)
        y, o_a, lse = _attn_fwd(p, tabs, q["qw"], q["kw"], q["sinks"], nseq)
        gates = _gdn_gates_fwd(p, q["alog"], q["dtb"], tm=tr)
        xc, qkv = _gdn_pre_fwd(p, q["ccw"], nseq, tm=tr)
        y, o_c, u, w, tinv, ss = _gdn_chunk_fwd(qkv, gates, p, y, q["onw"], nseq)
        y, hc = _conf_fwd(p, y, q["cw"], q["cb"], q["lnw"], q["lnb"], q["pw"], q["pwb"], nseq, tm=tr)
        saved.append(dict(q=q, p=p, h=h, y=y, o_a=o_a, lse=lse, gates=gates, xc=xc, qkv=qkv, o_c=o_c, u=u, w=w, tinv=tinv,
                          ss=ss, hc=hc))
        if l == 0:
            g_win1, g_wout0, g_wout1 = _exchange_wait(later, y, "gather_wait")
            win.append(g_win1)
            wout = [_y_rows(g_wout0), _y_rows(g_wout1)]
        if l + 1 < DEPTH:
            xs.append(_outproj(xs[l], y, wout[l], tm=tmw, tn=512))
        else:
            dxn, lsum = _outproj_loss(xs[l], y, wout[l], tgt, tm=tmw, tn=512)
    loss = lax.psum(jnp.sum(lsum) * (0.5 / d), ("x", "y", "c"))

    sent, smalls = [None] * DEPTH, [None] * DEPTH
    for l in reversed(range(DEPTH)):
        s = saved[l]
        q, p = s["q"], s["p"]
        dy = _matmul(dxn, wout[l], "nt", F32, tmw, 512, d, "outproj_bwd_dy")
        dwout = _y_rows_back(_matmul(s["y"], dxn, "tn", BF16, 1024, 1024, tkk, "outproj_bwd_dw"))
        dp, dkv, dqw, dkw, dsk = _attn_bwd(p, dy, s["o_a"], s["lse"], tabs, q["qw"], q["kw"], q["sinks"], nseq)
        dp, dqkv, dgate, donw = _gdn_chunk_bwd(s["qkv"], s["gates"], p, dy, dp, q["onw"], s["o_c"], s["u"], s["w"],
                                               s["tinv"], s["ss"], nseq)
        dp, dccw = _gdn_pre_bwd(p, dqkv, s["xc"], dp, q["ccw"], nseq, tm=tr)
        early = P_K // 768
        dwin_a = _matmul(s["h"], dp, "tn", BF16, 1024, 768, tkk, "inproj_bwd_dw_a", b_cols=(0, early))
        sent_a = _exchange_start("scatter", [dwin_a, dwout], donw, "scatter_start_a%d" % l)
        dp = _put_cols(dp, dkv, P_K, sent_a["token"], tm=tmm)
        dp, dal, ddb = _gdn_gates_bwd(dgate, p, q["alog"], q["dtb"], dp, tm=tr)
        dp, dhc, dpw, dpwb, dlnw, dlnb, dcb = _conf_bwd1(p, dy, dp, s["hc"], q["lnw"], q["lnb"], q["pw"], q["pwb"], tm=tr)
        dp, dcw = _conf_bwd2(p, dhc, dp, q["cw"], nseq, tm=tr)
        dwin_b = _matmul(s["h"], dp, "tn", BF16, 1024, 768, tkk, "inproj_bwd_dw_b", b_cols=(early, P_W // 768 - early))
        sent_b = _exchange_start("scatter", [dwin_b, dpw], dpwb, "scatter_start_b%d" % l)
        sent[l] = (sent_a, sent_b)
        dxn, dnw = _inproj_bwd_dx(dp, win[l], xs[l], q["nw"] + sent_b["token"][0:1, 0:1], dxn, tm=tmm)
        halves = lambda a: a.sum(0)[:A_DH] + a.sum(0)[A_DH:]
        smalls[l] = dict(
            norm_w=dnw.sum(0), q_norm_w=halves(dqw), k_norm_w=halves(dkw), sinks=dsk.sum(0)[:A_HEADS], b_conv_b=dcb.sum(0),
            b_ln_w=dlnw.sum(0), b_ln_b=dlnb.sum(0), b_pw_b=dpwb.sum(0), c_a_log=dal.sum(0)[GG:GG + C_HEADS],
            c_dt_bias=ddb.sum(0)[GG:GG + C_HEADS], c_onorm_w=donw.sum(0),
            b_conv_w=dcw.reshape(B_K, SUB, B_W).sum(1), c_conv_w=dccw.reshape(C_K, SUB, 3 * C_W).sum(1))
    grad_x = dxn.reshape(nseq, t, d)

    G, delta, new_m, new_v = {}, {}, {}, {}
    big = ("w_in", "w_out", "b_pw_w")
    stacks = {k: None for k in big}
    after = dxn
    g_t = [None] * DEPTH
    for l in reversed(range(DEPTH)):
        r_win_a, r_wout = _exchange_wait(sent[l][0], after, "scatter_wait_a%d" % l)
        r_win_b, r_pw = _exchange_wait(sent[l][1], r_wout, "scatter_wait_b%d" % l)
        g_t[l] = jnp.concatenate([_sum8_t(r_win_a), _sum8_t(r_win_b)], axis=0).reshape(P_W, -1, LANES)
        for k, r in (("w_out", r_wout), ("b_pw_w", r_pw)):
            stacks[k] = _adamw_layer(W[k], _sum8(r), M[k], V[k], l, stacks[k])
        after = stacks["w_out"][1]
    g_in = jnp.stack(g_t, axis=2).reshape(-1, LANES)
    g_in = _unpack_cols(g_in, axis=0, each=g_in.shape[0] // P_W)
    rows = _adamw_rows(_rows_view(w_in), g_in, _rows_view(m_w_in), _rows_view(v_w_in))
    stacks["w_in"] = [_rows_view_back(a, w_in.shape) for a in (g_in, *rows)]
    for k in big:
        G[k], delta[k], new_m[k], new_v[k] = stacks[k]
    part = _blob([jnp.stack([smalls[l][k] for l in range(DEPTH)]) for k in SMALL])
    (tot,) = _all_gather([part], after=rows[0])
    tot = _sum8(tot.reshape(N_DEV, part.shape[0], LANES))
    full_shapes = [(DEPTH,) + smalls[0][k].shape for k in SMALL]
    for k, g in zip(SMALL, _unblob(tot, full_shapes)):
        G[k] = g
    G["b_conv_w"] = lax.dynamic_slice_in_dim(G["b_conv_w"], me * (B_W // N_DEV), B_W // N_DEV, axis=2)
    G["c_conv_w"] = lax.dynamic_slice_in_dim(G["c_conv_w"], me * (3 * C_W // N_DEV), 3 * C_W // N_DEV, axis=2)
    dl, mo, vo = _adamw(*[_blob([src[k] for k in SMALL]) for src in (W, G, M, V)])
    shapes = [W[k].shape for k in SMALL]
    for k, a, b, c in zip(SMALL, _unblob(dl, shapes), _unblob(mo, shapes), _unblob(vo, shapes)):
        delta[k], new_m[k], new_v[k] = a, b, c
    return (loss, grad_x, *[G[k] for k in ORDER], *[delta[k] for k in ORDER], *[new_m[k] for k in ORDER],
            *[new_v[k] for k in ORDER])
```

```python
import functools
import math

import jax
import jax.numpy as jnp
from jax import lax
from jax.experimental import pallas as pl
from jax.experimental.pallas import tpu as pltpu

F32 = jnp.float32
BF16 = jnp.bfloat16
HI = lax.Precision.HIGHEST
MESH = pl.DeviceIdType.MESH
S = jax.ShapeDtypeStruct
BS = pl.BlockSpec

N_DEV = 8
DEPTH = 2
D_MODEL = 2048
A_HEADS, A_KV, A_DH, A_W, A_KVW = 12, 4, 64, 768, 256
ROT = 16
THETA = 500000.0
ABLK = 128
B_W, B_K = 512, 31
C_HEADS, C_DH, C_W, C_K, CHUNK = 6, 128, 768, 4, 64
EPS = 1e-6
IN_COLS = 6668
P_Q, P_ZA, P_ZC, P_QKV, P_K, P_V, P_UB, P_ZB, P_BA, P_W = 0, 768, 1536, 2304, 4608, 4864, 5120, 6144, 6656, 6912
Y_A, Y_C, Y_B = 0, 768, 1536
LANES = 128
SUB = 8

ADAM_LR, ADAM_B1, ADAM_B2, ADAM_EPS, ADAM_WD, ADAM_STEP = 0.001, 0.9, 0.999, 1e-08, 0.01, 10


def _cp(*sem, vmem=None):
    kw = {}
    if sem:
        kw["dimension_semantics"] = sem
    if vmem:
        kw["vmem_limit_bytes"] = vmem
    return pltpu.CompilerParams(**kw)


def _pack_cols(w):
    z = jnp.zeros(w.shape[:-1] + (P_W - IN_COLS,), w.dtype)
    return jnp.concatenate([w[..., 0:768], w[..., 1280:2048], w[..., 5900:6668], w[..., 3584:5888],
                            w[..., 768:1024], w[..., 1024:1280], w[..., 2048:3072], w[..., 3072:3584],
                            w[..., 5888:5900], z], axis=-1)


def _unpack_cols(g, axis=-1, each=1):
    parts = ((P_Q, 768), (P_K, 256), (P_V, 256), (P_ZA, 768), (P_UB, 1024), (P_ZB, 512), (P_QKV, 2304), (P_BA, 12), (P_ZC, 768))
    return jnp.concatenate([lax.slice_in_dim(g, each * o, each * (o + n), axis=axis) for o, n in parts], axis=axis)


def _sigmoid(x):
    return 0.5 * jnp.tanh(0.5 * x) + 0.5


def _dsilu(x, sg):
    return sg * (1.0 + x * (1.0 - sg))


def _fold8(x):
    r, c = x.shape
    return x.reshape(r // SUB, SUB, c).sum(axis=0)


def _dot(a, b, prec=None):
    return jnp.dot(a, b, preferred_element_type=F32, precision=prec)


def _dot_nt(a, b, prec=None):
    return lax.dot_general(a, b, (((1,), (1,)), ((), ())), preferred_element_type=F32, precision=prec)


def _dot_tn(a, b, prec=None):
    return lax.dot_general(a, b, (((0,), (0,)), ((), ())), preferred_element_type=F32, precision=prec)


def _lane(shape):
    return lax.broadcasted_iota(jnp.int32, shape, 1)


def _subl(shape):
    return lax.broadcasted_iota(jnp.int32, shape, 0)


def _col(x, j):
    return jnp.sum(jnp.where(_lane(x.shape) == j, x, 0.0), axis=-1, keepdims=True)


def _inproj(x, nw, w, tm=512, tn=768):
    n, d = x.shape
    pw = w.shape[1]

    def body(x_ref, nw_ref, w_ref, p_ref, h_ref):
        @pl.when(pl.program_id(1) == 0)
        def _():
            xv = x_ref[...]
            r = lax.rsqrt(jnp.mean(xv * xv, axis=-1, keepdims=True) + EPS)
            h_ref[...] = (xv * r * nw_ref[...]).astype(BF16)

        p_ref[...] = _dot(h_ref[...], w_ref[...])

    return pl.pallas_call(
        body, name="inproj", grid=(n // tm, pw // tn),
        in_specs=[BS((tm, d), lambda i, j: (i, 0)), BS((1, d), lambda i, j: (0, 0)), BS((d, tn), lambda i, j: (0, j))],
        out_specs=[BS((tm, tn), lambda i, j: (i, j)), BS((tm, d), lambda i, j: (i, 0))],
        out_shape=[S((n, pw), F32), S((n, d), BF16)],
        compiler_params=_cp("arbitrary", "arbitrary"),
    )(x, nw, w)


def _outproj(x, y, w, tm=512, tn=1024):
    n, d = x.shape
    k = y.shape[1]

    def body(x_ref, y_ref, w_ref, o_ref):
        o_ref[...] = x_ref[...] + _dot(y_ref[...], w_ref[...])

    return pl.pallas_call(
        body, name="outproj", grid=(n // tm, d // tn),
        in_specs=[BS((tm, tn), lambda i, j: (i, j)), BS((tm, k), lambda i, j: (i, 0)), BS((k, tn), lambda i, j: (0, j))],
        out_specs=BS((tm, tn), lambda i, j: (i, j)),
        out_shape=S((n, d), F32),
        compiler_params=_cp("arbitrary", "arbitrary"),
    )(x, y, w)


def _outproj_loss(x, y, w, tgt, tm=512, tn=1024):
    n, d = x.shape
    k = y.shape[1]

    def body(x_ref, y_ref, w_ref, t_ref, g_ref, l_ref):
        @pl.when((pl.program_id(0) == 0) & (pl.program_id(1) == 0))
        def _():
            l_ref[...] = jnp.zeros_like(l_ref)

        diff = x_ref[...] + _dot(y_ref[...], w_ref[...]) - t_ref[...]
        g_ref[...] = diff * (1.0 / d)
        f = _fold8(diff * diff)
        acc = f[:, 0:LANES]
        for c in range(1, tn // LANES):
            acc = acc + f[:, c * LANES:(c + 1) * LANES]
        l_ref[...] += acc

    return pl.pallas_call(
        body, name="outproj_loss", grid=(n // tm, d // tn),
        in_specs=[BS((tm, tn), lambda i, j: (i, j)), BS((tm, k), lambda i, j: (i, 0)), BS((k, tn), lambda i, j: (0, j)),
                  BS((tm, tn), lambda i, j: (i, j))],
        out_specs=[BS((tm, tn), lambda i, j: (i, j)), BS((SUB, LANES), lambda i, j: (0, 0))],
        out_shape=[S((n, d), F32), S((SUB, LANES), F32)],
        compiler_params=_cp("arbitrary", "arbitrary"),
    )(x, y, w, tgt)


def _matmul(a, b, mode, out_dtype, tm, tn, tk, name, b_cols=None):
    if mode == "nn":
        (m, kk), nn = a.shape, b.shape[1]
        a_spec, b_spec = BS((tm, tk), lambda i, j, k: (i, k)), BS((tk, tn), lambda i, j, k: (k, j))
        dot = _dot
    elif mode == "nt":
        (m, kk), nn = a.shape, b.shape[0]
        a_spec, b_spec = BS((tm, tk), lambda i, j, k: (i, k)), BS((tn, tk), lambda i, j, k: (j, k))
        dot = _dot_nt
    else:
        j0, nj = b_cols or (0, b.shape[1] // tn)
        (kk, m), nn = a.shape, nj * tn
        a_spec, b_spec = BS((tk, tm), lambda i, j, k: (k, i)), BS((tk, tn), lambda i, j, k: (k, j0 + j))
        dot = _dot_tn
    nk = kk // tk

    def body(a_ref, b_ref, o_ref, acc_ref):
        kid = pl.program_id(2)

        @pl.when(kid == 0)
        def _():
            acc_ref[...] = jnp.zeros_like(acc_ref)

        acc_ref[...] += dot(a_ref[...].astype(BF16), b_ref[...].astype(BF16))

        @pl.when(kid == nk - 1)
        def _():
            o_ref[...] = acc_ref[...].astype(out_dtype)

    return pl.pallas_call(
        body, name=name, grid=(m // tm, nn // tn, nk),
        in_specs=[a_spec, b_spec], out_specs=BS((tm, tn), lambda i, j, k: (i, j)),
        out_shape=S((m, nn), out_dtype), scratch_shapes=[pltpu.VMEM((tm, tn), F32)],
        compiler_params=_cp("arbitrary", "arbitrary", "arbitrary"),
    )(a, b)


SLAB = 16


def _inproj_bwd_dx(dp, w, x, nw, dres, tm=512, tk=768):
    n, d = x.shape
    nk = dp.shape[1] // tk

    def body(dp_ref, w_ref, x_ref, nw_ref, dr_ref, dx_ref, dnw_ref, acc_ref):
        kid = pl.program_id(1)

        @pl.when((pl.program_id(0) == 0) & (kid == 0))
        def _():
            dnw_ref[...] = jnp.zeros_like(dnw_ref)

        @pl.when(kid == 0)
        def _():
            acc_ref[...] = jnp.zeros_like(acc_ref)

        acc_ref[...] += _dot_nt(dp_ref[...], w_ref[...])

        @pl.when(kid == nk - 1)
        def _():
            def slab(i, carry):
                rows = pl.ds(pl.multiple_of(i * SLAB, SLAB), SLAB)
                dh, xv = acc_ref[rows, :], x_ref[rows, :]
                r = lax.rsqrt(jnp.mean(xv * xv, axis=-1, keepdims=True) + EPS)
                dnw_ref[...] += _fold8(dh * xv * r)
                g = dh * nw_ref[...]
                mm = jnp.mean(g * xv, axis=-1, keepdims=True)
                dx_ref[rows, :] = dr_ref[rows, :] + r * g - xv * (r * r * r * mm)
                return carry

            lax.fori_loop(0, tm // SLAB, slab, 0)

    return pl.pallas_call(
        body, name="inproj_bwd_dx", grid=(n // tm, nk),
        in_specs=[BS((tm, tk), lambda i, k: (i, k)), BS((d, tk), lambda i, k: (0, k)), BS((tm, d), lambda i, k: (i, 0)),
                  BS((1, d), lambda i, k: (0, 0)), BS((tm, d), lambda i, k: (i, 0))],
        out_specs=[BS((tm, d), lambda i, k: (i, 0)), BS((SUB, d), lambda i, k: (0, 0))],
        out_shape=[S((n, d), F32), S((SUB, d), F32)],
        scratch_shapes=[pltpu.VMEM((tm, d), F32)],
        compiler_params=_cp("arbitrary", "arbitrary"),
    )(dp, w, x, nw, dres)


def _rope_tables(pos, tie):
    half = ROT // 2
    inv = THETA ** (-jnp.arange(0, ROT, 2, dtype=F32) / ROT)
    ang = pos.astype(F32)[:, None] * inv + tie
    cos, sin = jnp.cos(ang), jnp.sin(ang)
    n = pos.shape[0]
    one = jnp.ones((n, A_DH - ROT), F32)
    zero = jnp.zeros((n, A_DH - ROT), F32)
    zh = jnp.zeros((n, half), F32)
    c = jnp.concatenate([cos, cos, one], axis=1)
    s1 = jnp.concatenate([-sin, zh, zero], axis=1)
    s2 = jnp.concatenate([zh, sin, zero], axis=1)
    return tuple(jnp.concatenate([t, t], axis=1) for t in (c, s1, s2))


def _half_stat(t):
    lo = _lane(t.shape) < A_DH
    s_lo = jnp.sum(jnp.where(lo, t, 0.0), axis=-1, keepdims=True)
    s_hi = jnp.sum(jnp.where(lo, 0.0, t), axis=-1, keepdims=True)
    return jnp.where(lo, s_lo, s_hi)


def _normrope(x, w, c, s1, s2):
    r = lax.rsqrt(_half_stat(x * x) * (1.0 / A_DH) + EPS)
    xn = x * r * w
    return xn * c + pltpu.roll(xn, LANES - ROT // 2, 1) * s1 + pltpu.roll(xn, ROT // 2, 1) * s2, r


def _normrope_bwd(dy, x, r, w, c, s1, s2):
    dxn = dy * c + pltpu.roll(dy * s1, ROT // 2, 1) + pltpu.roll(dy * s2, LANES - ROT // 2, 1)
    g = dxn * w
    mm = _half_stat(g * x) * (1.0 / A_DH)
    return r * g - x * (r * r * r * mm), dxn * x * r


def _attn_mask(first):
    qi = _subl((ABLK, 2 * ABLK))
    kj = _lane((ABLK, 2 * ABLK))
    dist = qi + ABLK - kj
    return (dist >= 0) & (dist < ABLK) & (jnp.logical_not(first) | (kj >= ABLK))


def _keep_half(x, b):
    lo = _lane(x.shape) < A_DH
    return jnp.where(lo if b == 0 else jnp.logical_not(lo), x, jnp.zeros_like(x))


def _head_operand(x, j):
    a, b = j % 2, (j // 3) % 2
    return _keep_half(x if a == b else pltpu.roll(x, A_DH, 1), b)


def _head_result(x, j):
    a, b = j % 2, (j // 3) % 2
    return _keep_half(x if a == b else pltpu.roll(x, A_DH, 1), a)


def _attn_fwd(p, tabs, qw, kw, sinks, nseq):
    n = p.shape[0]
    nb = n // nseq // ABLK
    cur = lambda b, i: (b * nb + i, 0)
    prv = lambda b, i: (b * nb + jnp.maximum(i - 1, 0), 0)
    colblk = lambda f, w, off: (lambda b, i: (f(b, i)[0], off // w))

    def body(q_ref, za_ref, kc_ref, vc_ref, kp_ref, vp_ref, c_ref, s1_ref, s2_ref, cp_ref, s1p_ref, s2p_ref,
             qw_ref, kw_ref, sink_ref, y_ref, o_ref, lse_ref):
        first = pl.program_id(1) == 0
        tc = (c_ref[...], s1_ref[...], s2_ref[...])
        tp = (cp_ref[...], s1p_ref[...], s2p_ref[...])
        q, kc, kp = q_ref[...], kc_ref[...], kp_ref[...]
        qn = [_normrope(q[:, LANES * b:LANES * (b + 1)], qw_ref[...], *tc)[0] for b in range(A_W // LANES)]
        k2, v2 = [], []
        for b in range(A_KVW // LANES):
            sl = slice(LANES * b, LANES * (b + 1))
            k2.append(jnp.concatenate([_normrope(kp[:, sl], kw_ref[...], *tp)[0],
                                       _normrope(kc[:, sl], kw_ref[...], *tc)[0]], axis=0).astype(BF16))
            v2.append(jnp.concatenate([vp_ref[:, sl], vc_ref[:, sl]], axis=0).astype(BF16))
        valid = _attn_mask(first)
        heads = range(A_HEADS)
        qm = [_head_operand(qn[j // 2], j).astype(BF16) for j in heads]
        s = [jnp.where(valid, _dot_nt(qm[j], k2[j // 6]) * (A_DH ** -0.5), -jnp.inf) for j in heads]
        m = [jnp.maximum(jnp.max(s[j], axis=-1, keepdims=True), sink_ref[j]) for j in heads]
        e = [jnp.exp(s[j] - m[j]) for j in heads]
        den = [jnp.sum(e[j], axis=-1, keepdims=True) + jnp.exp(sink_ref[j] - m[j]) for j in heads]
        outs = [_head_result(_dot((e[j] * (1.0 / den[j])).astype(BF16), v2[j // 6]), j) for j in heads]
        lse = jnp.zeros((ABLK, LANES), F32)
        for j in heads:
            lse = jnp.where(_lane(lse.shape) == j, m[j] + jnp.log(den[j]), lse)
        o = jnp.concatenate([outs[2 * b] + outs[2 * b + 1] for b in range(A_W // LANES)], axis=1)
        za = za_ref[...]
        o_ref[...] = o
        lse_ref[...] = lse
        y_ref[...] = (o * (za * _sigmoid(za))).astype(BF16)

    tab_specs = [BS((ABLK, LANES), cur)] * 3 + [BS((ABLK, LANES), prv)] * 3
    return pl.pallas_call(
        body, name="attn_fwd", grid=(nseq, nb),
        in_specs=[BS((ABLK, A_W), colblk(cur, A_W, P_Q)), BS((ABLK, A_W), colblk(cur, A_W, P_ZA)),
                  BS((ABLK, A_KVW), colblk(cur, A_KVW, P_K)), BS((ABLK, A_KVW), colblk(cur, A_KVW, P_V)),
                  BS((ABLK, A_KVW), colblk(prv, A_KVW, P_K)), BS((ABLK, A_KVW), colblk(prv, A_KVW, P_V))]
        + tab_specs + [BS((1, LANES), lambda b, i: (0, 0))] * 2 + [BS(memory_space=pltpu.SMEM)],
        out_specs=[BS((ABLK, A_W), colblk(cur, A_W, Y_A)), BS((ABLK, A_W), cur), BS((ABLK, LANES), cur)],
        out_shape=[S((n, D_MODEL), BF16), S((n, A_W), F32), S((n, LANES), F32)],
        compiler_params=_cp("arbitrary", "arbitrary"),
    )(p, p, p, p, p, p, *tabs, *tabs, qw, kw, sinks)


def _attn_bwd(p, dy, o, lse, tabs, qw, kw, sinks, nseq):
    n = p.shape[0]
    nb = n // nseq // ABLK
    cur = lambda b, i: (b * nb + jnp.minimum(i, nb - 1), 0)
    prv = lambda b, i: (b * nb + jnp.maximum(i - 1, 0), 0)
    colblk = lambda f, w, off: (lambda b, i: (f(b, i)[0], off // w))

    def body(q_ref, za_ref, kc_ref, vc_ref, kp_ref, vp_ref, dy_ref, o_ref, lse_ref,
             c_ref, s1_ref, s2_ref, cp_ref, s1p_ref, s2p_ref, qw_ref, kw_ref, sink_ref,
             dqza_ref, dkv_ref, dqw_ref, dkw_ref, dsk_ref, tk_ref, tv_ref, ck_ref, cv_ref):
        i = pl.program_id(1)
        first = i == 0
        tc = (c_ref[...], s1_ref[...], s2_ref[...])
        tp = (cp_ref[...], s1p_ref[...], s2p_ref[...])
        nkb = A_KVW // LANES

        @pl.when((pl.program_id(0) == 0) & first)
        def _():
            dqw_ref[...] = jnp.zeros_like(dqw_ref)
            dkw_ref[...] = jnp.zeros_like(dkw_ref)
            dsk_ref[...] = jnp.zeros_like(dsk_ref)

        @pl.when(i < nb)
        def _():
            q, kc, kp = q_ref[...], kc_ref[...], kp_ref[...]
            qn, rq = [], []
            for b in range(A_W // LANES):
                a, r = _normrope(q[:, LANES * b:LANES * (b + 1)], qw_ref[...], *tc)
                qn.append(a)
                rq.append(r)
            k2, v2 = [], []
            for b in range(nkb):
                sl = slice(LANES * b, LANES * (b + 1))
                k2.append(jnp.concatenate([_normrope(kp[:, sl], kw_ref[...], *tp)[0],
                                           _normrope(kc[:, sl], kw_ref[...], *tc)[0]], axis=0).astype(BF16))
                v2.append(jnp.concatenate([vp_ref[:, sl], vc_ref[:, sl]], axis=0).astype(BF16))
            valid = _attn_mask(first)
            za, dy, o, lse = za_ref[...], dy_ref[...], o_ref[...], lse_ref[...]
            sg = _sigmoid(za)
            do = dy * za * sg
            dqza_ref[:, A_W:2 * A_W] = (dy * o * _dsilu(za, sg)).astype(BF16)
            heads = range(A_HEADS)
            blk = lambda x, b: x[:, LANES * b:LANES * (b + 1)]
            qm = [_head_operand(qn[j // 2], j).astype(BF16) for j in heads]
            lj = [_col(lse, j) for j in heads]
            pr = [jnp.exp(jnp.where(valid, _dot_nt(qm[j], k2[j // 6]) * (A_DH ** -0.5), -jnp.inf) - lj[j]) for j in heads]
            dom = [_head_operand(blk(do, j // 2), j).astype(BF16) for j in heads]
            delta = [jnp.sum(_keep_half(blk(do, j // 2) * blk(o, j // 2), j % 2), axis=-1, keepdims=True) for j in heads]
            ds = [(pr[j] * (_dot_nt(dom[j], v2[j // 6]) - delta[j]) * (A_DH ** -0.5)).astype(BF16) for j in heads]
            dqh = [_head_result(_dot(ds[j], k2[j // 6]), j) for j in heads]
            dkh = [_dot_tn(ds[j], qm[j]) for j in heads]
            dvh = [_dot_tn(pr[j].astype(BF16), dom[j]) for j in heads]
            per_blk = A_HEADS // nkb
            dks = [sum(dkh[per_blk * g + 1:per_blk * (g + 1)], dkh[per_blk * g]) for g in range(nkb)]
            dvs = [sum(dvh[per_blk * g + 1:per_blk * (g + 1)], dvh[per_blk * g]) for g in range(nkb)]
            dsk = jnp.zeros((ABLK, LANES), F32)
            for j in heads:
                dsk = dsk + jnp.where(_lane(dsk.shape) == j, -jnp.exp(sink_ref[j] - lj[j]) * delta[j], 0.0)
            dsk_ref[...] += _fold8(dsk)
            dqn = jnp.concatenate([dqh[2 * b] + dqh[2 * b + 1] for b in range(A_W // LANES)], axis=1)
            dqw = jnp.zeros((SUB, LANES), F32)
            dqo = []
            for b in range(A_W // LANES):
                sl = slice(LANES * b, LANES * (b + 1))
                dx, wt = _normrope_bwd(dqn[:, sl], q[:, sl], rq[b], qw_ref[...], *tc)
                dqo.append(dx)
                dqw = dqw + _fold8(wt)
            dqw_ref[...] += dqw
            dqza_ref[:, 0:A_W] = jnp.concatenate(dqo, axis=1).astype(BF16)
            tk_ref[...] = jnp.concatenate(dks, axis=1)
            tv_ref[...] = jnp.concatenate(dvs, axis=1)

        @pl.when(i == nb)
        def _():
            tk_ref[...] = jnp.zeros_like(tk_ref)
            tv_ref[...] = jnp.zeros_like(tv_ref)

        @pl.when(i > 0)
        def _():
            kp = kp_ref[...]
            dkn = ck_ref[...] + tk_ref[0:ABLK, :]
            dkw = jnp.zeros((SUB, LANES), F32)
            dko = []
            for b in range(nkb):
                sl = slice(LANES * b, LANES * (b + 1))
                r = _normrope(kp[:, sl], kw_ref[...], *tp)[1]
                dx, wt = _normrope_bwd(dkn[:, sl], kp[:, sl], r, kw_ref[...], *tp)
                dko.append(dx)
                dkw = dkw + _fold8(wt)
            dkw_ref[...] += dkw
            dkv_ref[:, 0:A_KVW] = jnp.concatenate(dko, axis=1).astype(BF16)
            dkv_ref[:, A_KVW:2 * A_KVW] = (cv_ref[...] + tv_ref[0:ABLK, :]).astype(BF16)

        ck_ref[...] = tk_ref[ABLK:2 * ABLK, :]
        cv_ref[...] = tv_ref[ABLK:2 * ABLK, :]

    tab_specs = [BS((ABLK, LANES), cur)] * 3 + [BS((ABLK, LANES), prv)] * 3
    acc = BS((SUB, LANES), lambda b, i: (0, 0))
    return pl.pallas_call(
        body, name="attn_bwd", grid=(nseq, nb + 1),
        in_specs=[BS((ABLK, A_W), colblk(cur, A_W, P_Q)), BS((ABLK, A_W), colblk(cur, A_W, P_ZA)),
                  BS((ABLK, A_KVW), colblk(cur, A_KVW, P_K)), BS((ABLK, A_KVW), colblk(cur, A_KVW, P_V)),
                  BS((ABLK, A_KVW), colblk(prv, A_KVW, P_K)), BS((ABLK, A_KVW), colblk(prv, A_KVW, P_V)),
                  BS((ABLK, A_W), colblk(cur, A_W, Y_A)), BS((ABLK, A_W), cur), BS((ABLK, LANES), cur)]
        + tab_specs + [BS((1, LANES), lambda b, i: (0, 0))] * 2 + [BS(memory_space=pltpu.SMEM)],
        out_specs=[BS((ABLK, 2 * A_W), cur), BS((ABLK, 2 * A_KVW), prv), acc, acc, acc],
        out_shape=[S((n, P_W), BF16), S((n, 2 * A_KVW), BF16)] + [S((SUB, LANES), F32)] * 3,
        scratch_shapes=[pltpu.VMEM((2 * ABLK, A_KVW), F32)] * 2 + [pltpu.VMEM((ABLK, A_KVW), F32)] * 2,
        compiler_params=_cp("arbitrary", "arbitrary"),
    )(p, p, p, p, p, p, dy, o, lse, *tabs, *tabs, qw, kw, sinks)


def _put_cols(dst, src, col_off, after, tm=512):
    n, w = src.shape

    def body(s_ref, d_in_ref, after_ref, d_ref):
        d_ref[...] = s_ref[...]

    return pl.pallas_call(
        body, name="put_cols", grid=(n // tm,),
        in_specs=[BS((tm, w), lambda i: (i, 0)), BS(memory_space=pl.ANY), BS(memory_space=pl.ANY)],
        out_specs=BS((tm, w), lambda i: (i, col_off // w)),
        out_shape=S(dst.shape, dst.dtype), input_output_aliases={1: 0},
        compiler_params=_cp("arbitrary"),
    )(src, dst, after)


HALO_B = 32


def _layernorm(hc, lnw, lnb):
    mu = jnp.mean(hc, axis=-1, keepdims=True)
    xc = hc - mu
    rstd = lax.rsqrt(jnp.mean(xc * xc, axis=-1, keepdims=True) + EPS)
    xhat = xc * rstd
    return xhat, rstd, xhat * lnw + lnb


def _shifted_copies(buf_ref, sh_ref):
    rows = sh_ref.shape[1]
    for b in range(1, SUB):
        sh_ref[b - 1] = buf_ref[pl.ds(b, rows), :]


def _rows_from(buf_ref, sh_ref, off, rows, cols=slice(None)):
    a, b = divmod(off, SUB)
    if b == 0:
        return buf_ref[pl.ds(SUB * a, rows), cols]
    return sh_ref[b - 1, pl.ds(SUB * a, rows), cols]


def _conf_fwd(p, y, cw, cb, lnw, lnb, pw, pwb, nseq, tm=256):
    n = p.shape[0]
    t = n // nseq
    nt = t // tm
    row = lambda b, i: b * nt + i
    halo = lambda b, i: jnp.maximum((b * t + i * tm) // HALO_B - 1, 0)
    vec = BS((1, B_W), lambda b, i: (0, 0))

    def body(ub_ref, uh_ref, zb_ref, cw_ref, cb_ref, lnw_ref, lnb_ref, pw_ref, pwb_ref, y_in_ref, y_ref, hc_ref, buf_ref, sh_ref):
        ub, uh = ub_ref[...], uh_ref[...]
        hh = uh[:, :B_W] * _sigmoid(uh[:, B_W:])
        buf_ref[0:HALO_B, :] = jnp.where(pl.program_id(1) > 0, hh, 0.0)
        buf_ref[HALO_B:, :] = ub[:, :B_W] * _sigmoid(ub[:, B_W:])
        _shifted_copies(buf_ref, sh_ref)
        hc = jnp.zeros((tm, B_W), F32) + cb_ref[...]
        for k in range(B_K):
            hc = hc + cw_ref[k:k + 1, :] * _rows_from(buf_ref, sh_ref, HALO_B - B_K + 1 + k, tm)
        hc_ref[...] = hc
        ln = _layernorm(hc, lnw_ref[...], lnb_ref[...])[2]
        sw = ln * _sigmoid(ln)
        ob = _dot(sw.astype(BF16), pw_ref[...]) + pwb_ref[...]
        zb = zb_ref[...]
        y_ref[...] = (ob * (zb * _sigmoid(zb))).astype(BF16)

    return pl.pallas_call(
        body, name="conf_fwd", grid=(nseq, nt),
        in_specs=[BS((tm, 2 * B_W), lambda b, i: (row(b, i), P_UB // (2 * B_W))),
                  BS((HALO_B, 2 * B_W), lambda b, i: (halo(b, i), P_UB // (2 * B_W))),
                  BS((tm, B_W), lambda b, i: (row(b, i), P_ZB // B_W)),
                  BS((HALO_B, B_W), lambda b, i: (0, 0)), vec, vec, vec, BS((B_W, B_W), lambda b, i: (0, 0)), vec,
                  BS(memory_space=pl.ANY)],
        out_specs=[BS((tm, B_W), lambda b, i: (row(b, i), Y_B // B_W)), BS((tm, B_W), lambda b, i: (row(b, i), 0))],
        out_shape=[S(y.shape, y.dtype), S((n, B_W), F32)], input_output_aliases={9: 0},
        scratch_shapes=[pltpu.VMEM((HALO_B + tm, B_W), F32), pltpu.VMEM((SUB - 1, HALO_B + tm - SUB, B_W), F32)],
        compiler_params=_cp("arbitrary", "arbitrary"),
    )(p, p, p, cw, cb, lnw, lnb, pw, pwb, y)


def _conf_bwd1(p, dy, dp, hc, lnw, lnb, pw, pwb, tm=256):
    n = p.shape[0]
    vec = BS((1, B_W), lambda i: (0, 0))
    acc = BS((SUB, B_W), lambda i: (0, 0))

    def body(dy_ref, zb_ref, hc_ref, lnw_ref, lnb_ref, pw_ref, pwb_ref, dp_in_ref,
             dzb_ref, dhc_ref, dpw_ref, dpwb_ref, dlnw_ref, dlnb_ref, dcb_ref):
        @pl.when(pl.program_id(0) == 0)
        def _():
            for r in (dpw_ref, dpwb_ref, dlnw_ref, dlnb_ref, dcb_ref):
                r[...] = jnp.zeros_like(r)

        xhat, rstd, ln = _layernorm(hc_ref[...], lnw_ref[...], lnb_ref[...])
        sgl = _sigmoid(ln)
        sw = (ln * sgl).astype(BF16)
        ob = _dot(sw, pw_ref[...]) + pwb_ref[...]
        dy, zb = dy_ref[...], zb_ref[...]
        sgz = _sigmoid(zb)
        dzb_ref[...] = (dy * ob * _dsilu(zb, sgz)).astype(BF16)
        dob = dy * zb * sgz
        dobb = dob.astype(BF16)
        dpwb_ref[...] += _fold8(dob)
        dpw_ref[...] += _dot_tn(sw, dobb)
        dln = _dot_nt(dobb, pw_ref[...]) * _dsilu(ln, sgl)
        dlnw_ref[...] += _fold8(dln * xhat)
        dlnb_ref[...] += _fold8(dln)
        dxh = dln * lnw_ref[...]
        dhc = rstd * (dxh - jnp.mean(dxh, axis=-1, keepdims=True) - xhat * jnp.mean(dxh * xhat, axis=-1, keepdims=True))
        dcb_ref[...] += _fold8(dhc)
        dhc_ref[...] = dhc

    return pl.pallas_call(
        body, name="conf_bwd1", grid=(n // tm,),
        in_specs=[BS((tm, B_W), lambda i: (i, Y_B // B_W)), BS((tm, B_W), lambda i: (i, P_ZB // B_W)),
                  BS((tm, B_W), lambda i: (i, 0)), vec, vec, BS((B_W, B_W), lambda i: (0, 0)), vec,
                  BS(memory_space=pl.ANY)],
        out_specs=[BS((tm, B_W), lambda i: (i, P_ZB // B_W)), BS((tm, B_W), lambda i: (i, 0)),
                   BS((B_W, B_W), lambda i: (0, 0)), acc, acc, acc, acc],
        out_shape=[S(dp.shape, dp.dtype), S((n, B_W), F32), S((B_W, B_W), F32)] + [S((SUB, B_W), F32)] * 4,
        input_output_aliases={7: 0},
        compiler_params=_cp("arbitrary"),
    )(dy, p, hc, lnw, lnb, pw, pwb, dp)


def _conf_bwd2(p, dhc, dp, cw, nseq, tm=256):
    n = p.shape[0]
    t = n // nseq
    nt = t // tm
    row = lambda b, i: b * nt + i
    prev = lambda b, i: jnp.maximum((b * t + i * tm) // HALO_B - 1, 0)
    nxt = lambda b, i: jnp.minimum((b * t + (i + 1) * tm) // HALO_B, n // HALO_B - 1)

    def body(ub_ref, uh_ref, dh_ref, dn_ref, cw_ref, dp_in_ref, dub_ref, dcw_ref, buf_ref, dbuf_ref, sh_ref, dsh_ref):
        i = pl.program_id(1)

        @pl.when((pl.program_id(0) == 0) & (i == 0))
        def _():
            dcw_ref[...] = jnp.zeros_like(dcw_ref)

        uh = uh_ref[...]
        buf_ref[0:HALO_B, :] = jnp.where(i > 0, uh[:, :B_W] * _sigmoid(uh[:, B_W:]), 0.0)
        buf_ref[HALO_B:, :] = ub_ref[:, :B_W] * _sigmoid(ub_ref[:, B_W:])
        dbuf_ref[0:tm, :] = dh_ref[...]
        dbuf_ref[tm:, :] = jnp.where(i < nt - 1, dn_ref[...], 0.0)
        _shifted_copies(buf_ref, sh_ref)
        _shifted_copies(dbuf_ref, dsh_ref)
        for c in range(B_W // LANES):
            cs, gs = slice(LANES * c, LANES * (c + 1)), slice(B_W + LANES * c, B_W + LANES * (c + 1))
            for r0 in range(0, tm, LANES):
                dhc = dh_ref[r0:r0 + LANES, cs]
                dhg = jnp.zeros((LANES, LANES), F32)
                for k in range(B_K):
                    dhg = dhg + cw_ref[k:k + 1, cs] * _rows_from(dbuf_ref, dsh_ref, r0 + B_K - 1 - k, LANES, cs)
                    dcw_ref[SUB * k:SUB * (k + 1), cs] += _fold8(
                        dhc * _rows_from(buf_ref, sh_ref, r0 + HALO_B - B_K + 1 + k, LANES, cs))
                a, sg = ub_ref[r0:r0 + LANES, cs], _sigmoid(ub_ref[r0:r0 + LANES, gs])
                dub_ref[r0:r0 + LANES, cs] = (dhg * sg).astype(BF16)
                dub_ref[r0:r0 + LANES, gs] = (dhg * a * sg * (1.0 - sg)).astype(BF16)

    return pl.pallas_call(
        body, name="conf_bwd2", grid=(nseq, nt),
        in_specs=[BS((tm, 2 * B_W), lambda b, i: (row(b, i), P_UB // (2 * B_W))),
                  BS((HALO_B, 2 * B_W), lambda b, i: (prev(b, i), P_UB // (2 * B_W))),
                  BS((tm, B_W), lambda b, i: (row(b, i), 0)), BS((HALO_B, B_W), lambda b, i: (nxt(b, i), 0)),
                  BS((HALO_B, B_W), lambda b, i: (0, 0)), BS(memory_space=pl.ANY)],
        out_specs=[BS((tm, 2 * B_W), lambda b, i: (row(b, i), P_UB // (2 * B_W))),
                   BS((SUB * B_K, B_W), lambda b, i: (0, 0))],
        out_shape=[S(dp.shape, dp.dtype), S((SUB * B_K, B_W), F32)], input_output_aliases={5: 0},
        scratch_shapes=[pltpu.VMEM((HALO_B + tm, B_W), F32)] * 2 + [pltpu.VMEM((SUB - 1, HALO_B + tm - SUB, B_W), F32)] * 2,
        compiler_params=_cp("arbitrary", "arbitrary"),
    )(p, p, dhc, dhc, cw, dp)


HALO_C = 8
QS = C_DH ** -0.5
NCB = 3 * C_HEADS
CB0 = P_QKV // LANES
ZC0 = P_ZC // LANES
GB, GG = 0, C_HEADS


def _softplus(z):
    return jnp.maximum(z, 0.0) + jnp.log(1.0 + jnp.exp(-jnp.abs(z)))


def _gdn_gates_fwd(p, alog_l, dtb_l, tm=256):
    n = p.shape[0]

    def body(ba_ref, al_ref, db_ref, o_ref):
        blk = ba_ref[...]
        lane = _lane(blk.shape)
        g = jnp.where((lane >= GG) & (lane < GG + C_HEADS), -jnp.exp(al_ref[...]) * _softplus(blk + db_ref[...]), 0.0)
        tri = (_subl((CHUNK, CHUNK)) >= _lane((CHUNK, CHUNK))).astype(F32)
        gc = jnp.concatenate([_dot(tri, g[CHUNK * c:CHUNK * (c + 1)], HI) for c in range(tm // CHUNK)], axis=0)
        o_ref[...] = jnp.where(lane < GG, _sigmoid(blk), gc)

    return pl.pallas_call(
        body, name="gdn_gates_fwd", grid=(n // tm,),
        in_specs=[BS((tm, LANES), lambda i: (i, P_BA // LANES)), BS((1, LANES), lambda i: (0, 0)), BS((1, LANES), lambda i: (0, 0))],
        out_specs=BS((tm, LANES), lambda i: (i, 0)), out_shape=S((n, LANES), F32),
        compiler_params=_cp("arbitrary"),
    )(p, alog_l, dtb_l)


def _gdn_pre_fwd(p, ccw, nseq, tm=256):
    n = p.shape[0]
    t = n // nseq
    nt = t // tm
    row = lambda b, i: b * nt + i
    halo = lambda b, i: jnp.maximum((b * t + i * tm) // HALO_C - 1, 0)

    def body(x_ref, xh_ref, w_ref, xc_ref, o_ref, buf_ref):
        buf_ref[0:HALO_C, :] = jnp.where(pl.program_id(1) > 0, xh_ref[...], 0.0)
        buf_ref[HALO_C:, :] = x_ref[...]
        for c in range(NCB):
            cs = slice(LANES * c, LANES * (c + 1))
            xc = jnp.zeros((tm, LANES), F32)
            for k in range(C_K):
                xc = xc + w_ref[k:k + 1, cs] * buf_ref[pl.ds(HALO_C - C_K + 1 + k, tm), cs]
            xc_ref[:, cs] = xc
            act = xc * _sigmoid(xc)
            if c < 2 * C_HEADS:
                act = act * (lax.rsqrt(jnp.sum(act * act, axis=-1, keepdims=True) + EPS) * (QS if c < C_HEADS else 1.0))
            o_ref[:, cs] = act

    wide = 3 * C_W
    return pl.pallas_call(
        body, name="gdn_pre_fwd", grid=(nseq, nt),
        in_specs=[BS((tm, wide), lambda b, i: (row(b, i), P_QKV // wide)), BS((HALO_C, wide), lambda b, i: (halo(b, i), P_QKV // wide)),
                  BS((SUB, wide), lambda b, i: (0, 0))],
        out_specs=[BS((tm, wide), lambda b, i: (row(b, i), 0))] * 2,
        out_shape=[S((n, wide), F32)] * 2,
        scratch_shapes=[pltpu.VMEM((HALO_C + tm, wide), F32)],
        compiler_params=_cp("arbitrary", "arbitrary"),
    )(p, p, ccw)


def _chunk_common(q, k, gt, gtt, h):
    beta = _col(gt, GB + h)
    gc = _col(gt, GG + h)
    gcr = gtt[GG + h:GG + h + 1, :]
    ii, jj = _subl((CHUNK, CHUNK)), _lane((CHUNK, CHUNK))
    incl, strict = ii >= jj, ii > jj
    dec = jnp.exp(jnp.where(incl, gc - gcr, -jnp.inf))
    kb = k * beta
    kbf = k.astype(BF16)
    a = jnp.where(strict, _dot_nt(kb.astype(BF16), kbf) * dec, 0.0)
    mq = jnp.where(incl, _dot_nt(q.astype(BF16), kbf) * dec, 0.0)
    glast = jnp.sum(jnp.where(_subl(gc.shape) == CHUNK - 1, gc, 0.0), axis=0, keepdims=True)
    return beta, gc, incl, strict, dec, kb, a, mq, glast


def _split(x):
    hi = x.astype(BF16)
    return hi, (x - hi.astype(F32)).astype(BF16)


def _dot3(dot, a, b):
    (ah, al), (bh, bl) = a, b
    return dot(ah, bh) + (dot(ah, bl) + dot(al, bh))


def _unit_lower_inverses(mats):
    eye = (_subl(mats[0].shape) == _lane(mats[0].shape)).astype(F32)
    ms = [-a for a in mats]
    invs = [eye + m for m in ms]
    parts = [_split(m) for m in ms]
    for _ in range(5):
        ms = [_dot3(_dot, s, s) for s in parts]
        parts = [_split(m) for m in ms]
        invs = [inv + _dot3(_dot, _split(inv), s) for inv, s in zip(invs, parts)]
    return invs


def _gdn_chunk_fwd(qkv, gates, p, y, onw, nseq, tt=512):
    n = qkv.shape[0]
    t = n // nseq
    tt = min(tt, t)
    nt = t // tt
    nch = tt // CHUNK

    def body(q_ref, k_ref, v_ref, g_ref, zc_ref, onw_ref, y_in_ref, y_ref, o_ref, u_ref, w_ref, t_ref, ss_ref, s_scr):
        @pl.when(pl.program_id(1) == 0)
        def _():
            s_scr[...] = jnp.zeros_like(s_scr)

        def step(c, carry):
            rows = pl.ds(pl.multiple_of(c * CHUNK, CHUNK), CHUNK)
            gt = g_ref[rows, :]
            gtt = gt.T
            heads = range(C_HEADS)
            hs = [slice(C_DH * h, C_DH * (h + 1)) for h in heads]
            q, k, v = ([r[rows, hs[h]] for h in heads] for r in (q_ref, k_ref, v_ref))
            cm = [_chunk_common(q[h], k[h], gt, gtt, h) for h in heads]
            beta, gc, kb, mq, glast = ([m[i] for m in cm] for i in (0, 1, 5, 7, 8))
            tinv = _unit_lower_inverses([m[6] for m in cm])
            egc = [jnp.exp(g) for g in gc]
            sol = [_dot3(_dot, _split(tinv[h]), _split(jnp.concatenate([v[h] * beta[h], kb[h] * egc[h]], axis=1))) for h in heads]
            sv = [s_scr[h] for h in heads]
            sb = [s.astype(BF16) for s in sv]
            vnb = [(sol[h][:, :C_DH] - _dot(sol[h][:, C_DH:].astype(BF16), sb[h])).astype(BF16) for h in heads]
            o = [_dot((q[h] * egc[h]).astype(BF16), sb[h]) + _dot(mq[h].astype(BF16), vnb[h]) for h in heads]
            for h in heads:
                ss_ref[h, c] = sv[h]
                s_scr[h] = sv[h] * jnp.exp(glast[h]) + _dot_tn((k[h] * jnp.exp(glast[h] - gc[h])).astype(BF16), vnb[h])
            for h in heads:
                o_ref[rows, hs[h]] = o[h]
                u_ref[rows, hs[h]] = sol[h][:, :C_DH]
                w_ref[rows, hs[h]] = sol[h][:, C_DH:]
                t_ref[rows, hs[h]] = jnp.concatenate([tinv[h], jnp.zeros_like(tinv[h])], axis=1)
                zc = zc_ref[rows, hs[h]]
                r = lax.rsqrt(jnp.mean(o[h] * o[h], axis=-1, keepdims=True) + EPS)
                y_ref[rows, hs[h]] = (o[h] * r * onw_ref[...] * (zc * _sigmoid(zc))).astype(BF16)
            return carry

        lax.fori_loop(0, nch, step, 0, unroll=4 if nch % 4 == 0 else 1)

    row = lambda b, i: b * nt + i
    wb = lambda col: BS((tt, C_W), lambda b, i: (row(b, i), col))
    return pl.pallas_call(
        body, name="gdn_chunk_fwd", grid=(nseq, nt),
        in_specs=[wb(0), wb(1), wb(2), BS((tt, LANES), lambda b, i: (row(b, i), 0)), wb(P_ZC // C_W),
                  BS((1, LANES), lambda b, i: (0, 0)), BS(memory_space=pl.ANY)],
        out_specs=[wb(Y_C // C_W), wb(0), wb(0), wb(0), wb(0),
                   BS((None, C_HEADS, nch, C_DH, C_DH), lambda b, i: (b, 0, i, 0, 0))],
        out_shape=[S(y.shape, y.dtype)] + [S((n, C_W), F32)] * 4 + [S((nseq, C_HEADS, t // CHUNK, C_DH, C_DH), F32)],
        input_output_aliases={6: 0},
        scratch_shapes=[pltpu.VMEM((C_HEADS, C_DH, C_DH), F32)],
        compiler_params=_cp("arbitrary", "arbitrary"),
    )(qkv, qkv, qkv, gates, p, onw, y)


def _gdn_chunk_bwd(qkv, gates, p, dy, dp, onw, o, u, w, tinv, ss, nseq, tt=256):
    n = qkv.shape[0]
    t = n // nseq
    tt = min(tt, t)
    nt = t // tt
    nch = tt // CHUNK

    def body(q_ref, k_ref, v_ref, g_ref, zc_ref, onw_ref, o_ref, dy_ref, u_ref, w_ref, t_ref, ss_ref, dp_in_ref,
             dzc_ref, dqkv_ref, dg_ref, donw_ref, ds_scr):
        @pl.when(pl.program_id(1) == 0)
        def _():
            ds_scr[...] = jnp.zeros_like(ds_scr)

        @pl.when((pl.program_id(0) == 0) & (pl.program_id(1) == 0))
        def _():
            donw_ref[...] = jnp.zeros_like(donw_ref)

        def rsum(x):
            return jnp.sum(x, axis=-1, keepdims=True)

        def step(ci, carry):
            c = nch - 1 - ci
            rows = pl.ds(pl.multiple_of(c * CHUNK, CHUNK), CHUNK)
            gt = g_ref[rows, :]
            gtt = gt.T
            live = [head(c, rows, gt, gtt, h) for h in range(C_HEADS)]
            while live:
                live = [g for g in live if next(g, False)]
            return carry

        def head(c, rows, gt, gtt, h):
            hs = slice(C_DH * h, C_DH * (h + 1))
            q, k, v = q_ref[rows, hs], k_ref[rows, hs], v_ref[rows, hs]
            zc, o, dy, u, w = zc_ref[rows, hs], o_ref[rows, hs], dy_ref[rows, hs], u_ref[rows, hs], w_ref[rows, hs]
            tm_ = t_ref[rows, hs][:, 0:CHUNK]
            sv, dsv = ss_ref[h, c], ds_scr[h]
            sb, dsb = sv.astype(BF16), dsv.astype(BF16)
            sg = _sigmoid(zc)
            r = lax.rsqrt(jnp.mean(o * o, axis=-1, keepdims=True) + EPS)
            on = o * r
            ow = onw_ref[...]
            dzc_ref[rows, hs] = (dy * on * ow * _dsilu(zc, sg)).astype(BF16)
            t1 = dy * zc * sg
            donw_ref[...] += _fold8(t1 * on)
            don = t1 * ow
            do = r * (don - on * jnp.mean(don * on, axis=-1, keepdims=True))
            dob = do.astype(BF16)
            yield True
            beta, gc, incl, strict, dec, kb, a, mq, glast = _chunk_common(q, k, gt, gtt, h)
            egc = jnp.exp(gc)
            gl = jnp.exp(glast)
            ekd = jnp.exp(glast - gc)
            wb = w.astype(BF16)
            vnb = (u - _dot(wb, sb)).astype(BF16)
            qg = q * egc
            yield True
            dvn = _dot_tn(mq.astype(BF16), dob) + _dot((k * ekd).astype(BF16), dsb)
            dvnb = dvn.astype(BF16)
            dqg = _dot_nt(dob, sb)
            yield True
            dmq = jnp.where(incl, _dot_nt(dob, vnb), 0.0)
            dkd = _dot_nt(vnb, dsb)
            dgl = jnp.sum(rsum(dsv * sv), axis=0, keepdims=True)
            dw = -_dot_nt(dvnb, sb)
            yield True
            ds_scr[h] = gl * dsv + _dot_tn(qg.astype(BF16), dob) - _dot_tn(wb, dvnb)
            db = _dot3(_dot_tn, _split(tm_), _split(jnp.concatenate([dvn, dw], axis=1)))
            dbv, dbk = db[:, :C_DH], db[:, C_DH:]
            yield True
            da = -jnp.where(strict, _dot3(_dot_nt, _split(dbv), _split(u)) + _dot3(_dot_nt, _split(dbk), _split(w)), 0.0)
            yield True
            e = da * a + dmq * mq
            dgc = rsum(e) - rsum(e.T)
            dgb, dhb, kbf = (da * dec).astype(BF16), (dmq * dec).astype(BF16), k.astype(BF16)
            dkb = _dot(dgb, kbf)
            tk = rsum(dbk * k)
            rk = rsum(dkd * k) * ekd
            dq = _dot(dhb, kbf) + egc * dqg
            dk = _dot_tn(dgb, kb.astype(BF16)) + _dot_tn(dhb, q.astype(BF16)) + beta * (egc * dbk + dkb) + ekd * dkd
            dbeta = rsum(dbv * v) + tk * egc + rsum(dkb * k)
            dgc = dgc + tk * beta * egc + egc * rsum(dqg * q) - rk
            dglast = jnp.sum(rk, axis=0, keepdims=True) + dgl * gl
            dgc = dgc + jnp.where(_subl(dgc.shape) == CHUNK - 1, dglast, 0.0)
            dqkv_ref[0, rows, hs] = dq
            dqkv_ref[1, rows, hs] = dk
            dqkv_ref[2, rows, hs] = beta * dbv
            lane = _lane((CHUNK, LANES))
            dg_ref[h, rows, :] = jnp.where(lane == 0, dbeta, jnp.where(lane == 1, dgc, 0.0))

        lax.fori_loop(0, nch, step, 0)

    row = lambda b, i: b * nt + nt - 1 - i
    wb = lambda col: BS((tt, C_W), lambda b, i: (row(b, i), col))
    return pl.pallas_call(
        body, name="gdn_chunk_bwd", grid=(nseq, nt),
        in_specs=[wb(0), wb(1), wb(2), BS((tt, LANES), lambda b, i: (row(b, i), 0)), wb(P_ZC // C_W),
                  BS((1, LANES), lambda b, i: (0, 0)), wb(0), wb(Y_C // C_W), wb(0), wb(0), wb(0),
                  BS((None, C_HEADS, nch, C_DH, C_DH), lambda b, i: (b, 0, nt - 1 - i, 0, 0)), BS(memory_space=pl.ANY)],
        out_specs=[wb(P_ZC // C_W), BS((3, tt, C_W), lambda b, i: (0, row(b, i), 0)),
                   BS((C_HEADS, tt, LANES), lambda b, i: (0, row(b, i), 0)), BS((SUB, LANES), lambda b, i: (0, 0))],
        out_shape=[S(dp.shape, dp.dtype), S((3, n, C_W), F32), S((C_HEADS, n, LANES), F32), S((SUB, LANES), F32)],
        input_output_aliases={12: 0},
        scratch_shapes=[pltpu.VMEM((C_HEADS, C_DH, C_DH), F32)],
        compiler_params=_cp("arbitrary", "arbitrary"),
    )(qkv, qkv, qkv, gates, p, onw, o, dy, u, w, tinv, ss, dp)


def _gdn_gates_bwd(dgate, p, alog_l, dtb_l, dp, tm=256):
    n = p.shape[0]
    acc = BS((SUB, LANES), lambda i: (0, 0))

    def body(dg_ref, ba_ref, al_ref, db_ref, dp_in_ref, dba_ref, dal_ref, ddb_ref):
        @pl.when(pl.program_id(0) == 0)
        def _():
            dal_ref[...] = jnp.zeros_like(dal_ref)
            ddb_ref[...] = jnp.zeros_like(ddb_ref)

        blk = ba_ref[...]
        lane = _lane(blk.shape)
        dbeta = jnp.zeros_like(blk)
        dgc = jnp.zeros_like(blk)
        for h in range(C_HEADS):
            dbeta = dbeta + jnp.where(lane == GB + h, _col(dg_ref[h], 0), 0.0)
            dgc = dgc + jnp.where(lane == GG + h, _col(dg_ref[h], 1), 0.0)
        tri = (_subl((CHUNK, CHUNK)) <= _lane((CHUNK, CHUNK))).astype(F32)
        dg = jnp.concatenate([_dot(tri, dgc[CHUNK * c:CHUNK * (c + 1)], HI) for c in range(tm // CHUNK)], axis=0)
        beta = _sigmoid(blk)
        z = blk + db_ref[...]
        ea = jnp.exp(al_ref[...])
        isg = (lane >= GG) & (lane < GG + C_HEADS)
        dz = jnp.where(isg, -dg * ea * _sigmoid(z), 0.0)
        dal_ref[...] += _fold8(jnp.where(isg, -dg * ea * _softplus(z), 0.0))
        ddb_ref[...] += _fold8(dz)
        out = jnp.where(lane < GG, dbeta * beta * (1.0 - beta), dz)
        dba_ref[...] = jnp.concatenate([out, jnp.zeros_like(out)], axis=1).astype(BF16)

    return pl.pallas_call(
        body, name="gdn_gates_bwd", grid=(n // tm,),
        in_specs=[BS((C_HEADS, tm, LANES), lambda i: (0, i, 0)), BS((tm, LANES), lambda i: (i, P_BA // LANES)),
                  BS((1, LANES), lambda i: (0, 0)), BS((1, LANES), lambda i: (0, 0)), BS(memory_space=pl.ANY)],
        out_specs=[BS((tm, 2 * LANES), lambda i: (i, P_BA // (2 * LANES))), acc, acc],
        out_shape=[S(dp.shape, dp.dtype), S((SUB, LANES), F32), S((SUB, LANES), F32)],
        input_output_aliases={4: 0},
        compiler_params=_cp("arbitrary"),
    )(dgate, p, alog_l, dtb_l, dp)


def _gdn_pre_bwd(p, dqkv, xc, dp, ccw, nseq, tm=256):
    n = p.shape[0]
    t = n // nseq
    nt = t // tm
    wide = 3 * C_W
    row = lambda b, i: b * nt + i
    prev = lambda b, i: jnp.maximum((b * t + i * tm) // HALO_C - 1, 0)
    nxt = lambda b, i: jnp.minimum((b * t + (i + 1) * tm) // HALO_C, n // HALO_C - 1)

    def d_conv_out(d, xc, part):
        sg = _sigmoid(xc)
        act = xc * sg
        if part < 2:
            cs = QS if part == 0 else 1.0
            rn = lax.rsqrt(jnp.sum(act * act, axis=-1, keepdims=True) + EPS)
            d = cs * rn * d - act * (cs * rn * rn * rn * jnp.sum(d * act, axis=-1, keepdims=True))
        return d * _dsilu(xc, sg)

    def body(x_ref, xh_ref, d_ref, dn_ref, xc_ref, xn_ref, w_ref, dp_in_ref, dx_ref, dw_ref, buf_ref, dbuf_ref):
        i = pl.program_id(1)

        @pl.when((pl.program_id(0) == 0) & (i == 0))
        def _():
            dw_ref[...] = jnp.zeros_like(dw_ref)

        buf_ref[0:HALO_C, :] = jnp.where(i > 0, xh_ref[...], 0.0)
        buf_ref[HALO_C:, :] = x_ref[...]
        for c in range(NCB):
            cs = slice(LANES * c, LANES * (c + 1))
            part, hd = divmod(c, C_HEADS)
            hs = slice(LANES * hd, LANES * (hd + 1))
            d = d_conv_out(d_ref[part, :, hs], xc_ref[:, cs], part)
            dbuf_ref[0:tm, cs] = d
            dbuf_ref[tm:, cs] = jnp.where(i < nt - 1, d_conv_out(dn_ref[part, :, hs], xn_ref[:, cs], part), 0.0)
            dx = jnp.zeros((tm, LANES), F32)
            for k in range(C_K):
                dx = dx + w_ref[k:k + 1, cs] * dbuf_ref[pl.ds(C_K - 1 - k, tm), cs]
                dw_ref[SUB * k:SUB * (k + 1), cs] += _fold8(d * buf_ref[pl.ds(HALO_C - C_K + 1 + k, tm), cs])
            dx_ref[:, cs] = dx.astype(BF16)

    return pl.pallas_call(
        body, name="gdn_pre_bwd", grid=(nseq, nt),
        in_specs=[BS((tm, wide), lambda b, i: (row(b, i), P_QKV // wide)), BS((HALO_C, wide), lambda b, i: (prev(b, i), P_QKV // wide)),
                  BS((3, tm, C_W), lambda b, i: (0, row(b, i), 0)), BS((3, HALO_C, C_W), lambda b, i: (0, nxt(b, i), 0)),
                  BS((tm, wide), lambda b, i: (row(b, i), 0)), BS((HALO_C, wide), lambda b, i: (nxt(b, i), 0)),
                  BS((SUB, wide), lambda b, i: (0, 0)), BS(memory_space=pl.ANY)],
        out_specs=[BS((tm, wide), lambda b, i: (row(b, i), P_QKV // wide)), BS((SUB * C_K, wide), lambda b, i: (0, 0))],
        out_shape=[S(dp.shape, dp.dtype), S((SUB * C_K, wide), F32)], input_output_aliases={7: 0},
        scratch_shapes=[pltpu.VMEM((HALO_C + tm, wide), F32)] * 2,
        compiler_params=_cp("arbitrary", "arbitrary"),
    )(p, p, dqkv, dqkv, xc, xc, ccw, dp)


ANY = BS(memory_space=pl.ANY)


def _my_pos():
    return lax.axis_index("x"), lax.axis_index("y"), lax.axis_index("c")


def _dev_index(dev):
    return 4 * dev[0] + 2 * dev[1] + dev[2]


def _all_gather(shards, after=None):
    nk = len(shards)

    tail = [] if after is None else [after]

    def body(*refs):
        ins, outs = refs[:nk], refs[nk + len(tail):2 * nk + len(tail)]
        send, recv, loc = refs[2 * nk + len(tail):]
        x, y, c = _my_pos()
        me, sib = (x, y, c), (x, y, 1 - c)
        chips = [(1 - x, y), (x, 1 - y), (1 - x, 1 - y)]

        def rows(t, dev):
            r = ins[t].shape[0]
            return outs[t].at[pl.ds(pl.multiple_of(_dev_index(dev) * r, SUB), r), :]

        def copy(t, k, block, to, src=None):
            return pltpu.make_async_remote_copy(
                src_ref=rows(t, block) if src is None else src, dst_ref=rows(t, block),
                send_sem=send.at[t, k], recv_sem=recv.at[t, k], device_id=to, device_id_type=MESH)

        mine = [pltpu.make_async_copy(ins[t], rows(t, me), loc.at[t]) for t in range(nk)]
        for cp in mine:
            cp.start()
        first = []
        for t in range(nk):
            first.append(copy(t, 0, me, sib, src=ins[t]))
            first += [copy(t, 1 + j, me, (*chip, c), src=ins[t]) for j, chip in enumerate(chips)]
        for cp in first:
            cp.start()
        passed = []
        for j, chip in enumerate(chips):
            for t in range(nk):
                copy(t, 1 + j, (*chip, c), me).wait_recv()
                cp = copy(t, 4 + j, (*chip, c), sib)
                cp.start()
                passed.append(cp)
        for t in range(nk):
            copy(t, 0, sib, me).wait_recv()
            for j, chip in enumerate(chips):
                copy(t, 4 + j, (*chip, 1 - c), me).wait_recv()
        for cp in first + passed:
            cp.wait_send()
        for cp in mine:
            cp.wait()

    return pl.pallas_call(
        body, name="all_gather", in_specs=[ANY] * (nk + len(tail)), out_specs=[ANY] * nk,
        out_shape=[S((N_DEV * a.shape[0], a.shape[1]), a.dtype) for a in shards],
        scratch_shapes=[pltpu.SemaphoreType.DMA((nk, 7)), pltpu.SemaphoreType.DMA((nk, 7)), pltpu.SemaphoreType.DMA((nk,))],
    )(*shards, *tail)


SEM = BS(memory_space=pltpu.SEMAPHORE)
HBM = BS(memory_space=pltpu.HBM)
EFFECT = pltpu.SideEffectType.DATAFLOW_SIDE_EFFECTING


def _peers(x, y, c):
    return [((1 - x) if k & 4 else x, (1 - y) if k & 2 else y, (1 - c) if k & 1 else c) for k in range(1, N_DEV)]


def _exchange_copy(kind, src, land, send, recv, t, k, peer, me, arriving):
    frm = peer if arriving else me
    if kind == "gather":
        r = src.shape[0]
        s_ref = src
        d_ref = land.at[pl.ds(pl.multiple_of(_dev_index(frm) * r, SUB), r), :]
    else:
        r = src.shape[0] // N_DEV
        s_ref = src.at[pl.ds(pl.multiple_of(_dev_index(peer) * r, SUB), r), :]
        d_ref = land.at[_dev_index(frm)]
    sem = t * (N_DEV - 1) + k
    return pltpu.make_async_remote_copy(src_ref=s_ref, dst_ref=d_ref, send_sem=send.at[sem], recv_sem=recv.at[sem],
                                        device_id=peer, device_id_type=MESH)


def _own_copy(kind, src, land, own, t, me):
    if kind == "gather":
        r = src.shape[0]
        return pltpu.make_async_copy(src, land.at[pl.ds(pl.multiple_of(_dev_index(me) * r, SUB), r), :], own.at[t])
    r = src.shape[0] // N_DEV
    return pltpu.make_async_copy(src.at[pl.ds(pl.multiple_of(_dev_index(me) * r, SUB), r), :], land.at[_dev_index(me)], own.at[t])


def _exchange_start(kind, srcs, after, name):
    nk = len(srcs)
    if kind == "gather":
        lands = [lax.empty((N_DEV * a.shape[0], a.shape[1]), a.dtype) for a in srcs]
    else:
        lands = [lax.empty((N_DEV, a.shape[0] // N_DEV, a.shape[1]), a.dtype) for a in srcs]

    def body(*refs):
        src, land = refs[:nk], refs[nk:2 * nk]
        send, recv, own = refs[2 * nk + 1], refs[2 * nk + 2], refs[2 * nk + 3]
        token = refs[-1]
        x, y, c = _my_pos()
        me = (x, y, c)
        for t in range(nk):
            _own_copy(kind, src[t], land[t], own, t, me).start()
            for k, peer in enumerate(_peers(x, y, c)):
                _exchange_copy(kind, src[t], land[t], send, recv, t, k, peer, me, False).start()
        token[...] = jnp.zeros_like(token)

    hbm = lambda a: pltpu.HBM(a.shape, a.dtype)
    out = pl.pallas_call(
        body, name=name,
        out_shape=(pltpu.SemaphoreType.DMA((nk * (N_DEV - 1),)), pltpu.SemaphoreType.DMA((nk * (N_DEV - 1),)),
                   pltpu.SemaphoreType.DMA((nk,)), *[hbm(a) for a in srcs], *[hbm(a) for a in lands], S((SUB, LANES), F32)),
        in_specs=[HBM] * (2 * nk) + [ANY],
        out_specs=(SEM, SEM, SEM, *[HBM] * (2 * nk), BS(memory_space=pltpu.VMEM)),
        input_output_aliases={i: 3 + i for i in range(2 * nk)},
        compiler_params=pltpu.CompilerParams(has_side_effects=EFFECT),
    )(*[pltpu.with_memory_space_constraint(a, pltpu.HBM) for a in (*srcs, *lands)], after)
    return dict(kind=kind, nk=nk, send=out[0], recv=out[1], own=out[2], srcs=out[3:3 + nk], lands=out[3 + nk:3 + 2 * nk],
                token=out[-1])


def _exchange_wait(ex, after, name):
    kind, nk = ex["kind"], ex["nk"]

    def body(*refs):
        src, land = refs[:nk], refs[nk:2 * nk]
        send, recv, own = refs[2 * nk], refs[2 * nk + 1], refs[2 * nk + 2]
        x, y, c = _my_pos()
        me = (x, y, c)
        for t in range(nk):
            _own_copy(kind, src[t], land[t], own, t, me).wait()
            for k, peer in enumerate(_peers(x, y, c)):
                _exchange_copy(kind, src[t], land[t], send, recv, t, k, peer, me, False).wait_send()
                _exchange_copy(kind, src[t], land[t], send, recv, t, k, peer, me, True).wait_recv()

    hbm = lambda a: pltpu.HBM(a.shape, a.dtype)
    out = pl.pallas_call(
        body, name=name,
        out_shape=(*[hbm(a) for a in ex["srcs"]], *[hbm(a) for a in ex["lands"]]),
        in_specs=[HBM] * (2 * nk) + [SEM, SEM, SEM, ANY], out_specs=tuple([HBM] * (2 * nk)),
        input_output_aliases={i: i for i in range(2 * nk)},
        compiler_params=pltpu.CompilerParams(has_side_effects=EFFECT),
    )(*ex["srcs"], *ex["lands"], ex["send"], ex["recv"], ex["own"], after)
    return list(out[nk:])


def _first_leg(x, y, c):
    return [(x, y, 1 - c), (1 - x, y, c), (x, 1 - y, c), (1 - x, 1 - y, c)]


def _block_copy(src_ref, land, block_dev, to, send, recv, i, i_send=None):
    r = land.shape[0] // N_DEV
    rows = land.at[pl.ds(pl.multiple_of(_dev_index(block_dev) * r, SUB), r), :]
    return pltpu.make_async_remote_copy(src_ref=rows if src_ref is None else src_ref, dst_ref=rows,
                                        send_sem=send.at[i if i_send is None else i_send], recv_sem=recv.at[i],
                                        device_id=to, device_id_type=MESH)


def _gather2_start(shards, after, name):
    nk = len(shards)
    lands = [lax.empty((N_DEV * a.shape[0], a.shape[1]), a.dtype) for a in shards]

    def body(*refs):
        src, land = refs[:nk], refs[nk:2 * nk]
        send, recv, own = refs[2 * nk + 1], refs[2 * nk + 2], refs[2 * nk + 3]
        x, y, c = _my_pos()
        for t in range(nk):
            _own_copy("gather", src[t], land[t], own, t, (x, y, c)).start()
            for k, to in enumerate(_first_leg(x, y, c)):
                _block_copy(src[t], land[t], (x, y, c), to, send, recv, 4 * t + k).start()
        refs[-1][...] = jnp.zeros_like(refs[-1])

    hbm = lambda a: pltpu.HBM(a.shape, a.dtype)
    out = pl.pallas_call(
        body, name=name,
        out_shape=(pltpu.SemaphoreType.DMA((4 * nk,)), pltpu.SemaphoreType.DMA((4 * nk,)), pltpu.SemaphoreType.DMA((nk,)),
                   *[hbm(a) for a in shards], *[hbm(a) for a in lands], S((SUB, LANES), F32)),
        in_specs=[HBM] * (2 * nk) + [ANY], out_specs=(SEM, SEM, SEM, *[HBM] * (2 * nk), BS(memory_space=pltpu.VMEM)),
        input_output_aliases={i: 3 + i for i in range(2 * nk)},
        compiler_params=pltpu.CompilerParams(has_side_effects=EFFECT),
    )(*[pltpu.with_memory_space_constraint(a, pltpu.HBM) for a in (*shards, *lands)], after)
    return dict(nk=nk, send=out[0], recv=out[1], own=out[2], srcs=out[3:3 + nk], lands=out[3 + nk:3 + 2 * nk], token=out[-1])


def _gather2_forward(ex, after, name):
    nk = ex["nk"]

    def body(*refs):
        land = refs[:nk]
        recv1, send2, recv2 = refs[nk], refs[nk + 1 + len(after)], refs[nk + 2 + len(after)]
        x, y, c = _my_pos()
        for j, frm in enumerate(_first_leg(x, y, c)[1:]):
            for t in range(nk):
                _block_copy(None, land[t], frm, (x, y, c), send2, recv1, 4 * t + 1 + j, i_send=0).wait_recv()
                _block_copy(None, land[t], frm, (x, y, 1 - c), send2, recv2, 3 * t + j).start()

    hbm = lambda a: pltpu.HBM(a.shape, a.dtype)
    out = pl.pallas_call(
        body, name=name,
        out_shape=(pltpu.SemaphoreType.DMA((3 * nk,)), pltpu.SemaphoreType.DMA((3 * nk,)), *[hbm(a) for a in ex["lands"]]),
        in_specs=[HBM] * nk + [SEM] + [ANY] * len(after), out_specs=(SEM, SEM, *[HBM] * nk),
        input_output_aliases={i: 2 + i for i in range(nk)},
        compiler_params=pltpu.CompilerParams(has_side_effects=EFFECT),
    )(*ex["lands"], ex["recv"], *after)
    return dict(send=out[0], recv=out[1], lands=out[2:])


def _gather2_wait(ex, fw, name):
    nk = ex["nk"]

    def body(*refs):
        src, land = refs[:nk], refs[nk:2 * nk]
        send1, recv1, own, send2, recv2 = refs[2 * nk:2 * nk + 5]
        x, y, c = _my_pos()
        me, sib = (x, y, c), (x, y, 1 - c)
        for t in range(nk):
            _own_copy("gather", src[t], land[t], own, t, me).wait()
            for k, to in enumerate(_first_leg(x, y, c)):
                _block_copy(src[t], land[t], me, to, send1, recv1, 4 * t + k).wait_send()
            _block_copy(None, land[t], sib, me, send1, recv1, 4 * t).wait_recv()
            for j, frm in enumerate(_first_leg(x, y, c)[1:]):
                _block_copy(None, land[t], frm, sib, send2, recv2, 3 * t + j).wait_send()
                _block_copy(None, land[t], (frm[0], frm[1], 1 - c), me, send2, recv2, 3 * t + j).wait_recv()

    hbm = lambda a: pltpu.HBM(a.shape, a.dtype)
    out = pl.pallas_call(
        body, name=name,
        out_shape=(*[hbm(a) for a in ex["srcs"]], *[hbm(a) for a in fw["lands"]]),
        in_specs=[HBM] * (2 * nk) + [SEM] * 5, out_specs=tuple([HBM] * (2 * nk)),
        input_output_aliases={i: i for i in range(2 * nk)},
        compiler_params=pltpu.CompilerParams(has_side_effects=EFFECT),
    )(*ex["srcs"], *fw["lands"], ex["send"], ex["recv"], ex["own"], fw["send"], fw["recv"])
    return list(out[nk:])


BLOCK_BYTES = 4 << 20


def _row_tile(rows, row_bytes, align):
    best = align
    for tr in range(align, rows + 1, align):
        if rows % tr == 0 and tr * row_bytes <= BLOCK_BYTES:
            best = tr
    return best


def _sum8(a):
    _, r, w = a.shape
    tr = _row_tile(r, N_DEV * w * a.dtype.itemsize, 32 // a.dtype.itemsize)

    def body(a_ref, o_ref):
        acc = a_ref[0].astype(F32)
        for d in range(1, N_DEV):
            acc = acc + a_ref[d].astype(F32)
        o_ref[...] = acc

    return pl.pallas_call(
        body, name="sum8", grid=(r // tr,), in_specs=[BS((N_DEV, tr, w), lambda i: (0, i, 0))],
        out_specs=BS((tr, w), lambda i: (i, 0)), out_shape=S((r, w), F32), compiler_params=_cp("arbitrary"),
    )(a)


def _adamw(w, g, m, v):
    r, c = w.shape
    tr = _row_tile(r, c * 4 * 2, SUB)

    def body(w_ref, g_ref, m_ref, v_ref, d_ref, mo_ref, vo_ref):
        d_ref[...], mo_ref[...], vo_ref[...] = _adam_update(w_ref[...], g_ref[...], m_ref[...], v_ref[...])

    blk = BS((tr, c), lambda i: (i, 0))
    return pl.pallas_call(
        body, name="adamw", grid=(r // tr,), in_specs=[blk] * 4, out_specs=[blk] * 3,
        out_shape=[S((r, c), F32)] * 3, compiler_params=_cp("arbitrary"),
    )(w, g, m, v)


def _sum8_t(a, tc=256):
    _, r, w = a.shape

    def body(a_ref, o_ref):
        acc = a_ref[0].astype(F32)
        for d in range(1, N_DEV):
            acc = acc + a_ref[d].astype(F32)
        o_ref[...] = acc.T

    return pl.pallas_call(
        body, name="sum8_t", grid=(w // tc,), in_specs=[BS((N_DEV, r, tc), lambda j: (0, 0, j))],
        out_specs=BS((tc, r), lambda j: (j, 0)), out_shape=S((w, r), F32), compiler_params=_cp("arbitrary"),
    )(a)


def _rows_view(a):
    nl, r, c = a.shape
    assert nl == 2
    return a.transpose(2, 0, 1).reshape(c, nl, r // LANES, LANES).transpose(0, 2, 1, 3).reshape(-1, LANES)


def _rows_view_back(a, shape):
    nl, r, c = shape
    return a.reshape(c, r // LANES, nl, LANES).transpose(0, 2, 1, 3).reshape(c, nl, r).transpose(1, 2, 0)


def _adamw_rows(w, g, m, v, tr=2048):
    n = w.shape[0]

    def body(w_ref, g_ref, m_ref, v_ref, d_ref, mo_ref, vo_ref):
        d_ref[...], mo_ref[...], vo_ref[...] = _adam_update(w_ref[...], g_ref[...], m_ref[...], v_ref[...])

    blk = BS((tr, LANES), lambda i: (i, 0))
    return pl.pallas_call(
        body, name="adamw_rows", grid=(pl.cdiv(n, tr),), in_specs=[blk] * 4, out_specs=[blk] * 3,
        out_shape=[S((n, LANES), F32)] * 3, compiler_params=_cp("arbitrary"),
    )(w, g, m, v)


def _adam_update(w, g, m, v):
    m2 = ADAM_B1 * m + (1.0 - ADAM_B1) * g
    v2 = ADAM_B2 * v + (1.0 - ADAM_B2) * (g * g)
    m_hat = m2 / (1.0 - ADAM_B1 ** ADAM_STEP)
    v_hat = v2 / (1.0 - ADAM_B2 ** ADAM_STEP)
    return -ADAM_LR * (m_hat / (jnp.sqrt(v_hat) + ADAM_EPS) + ADAM_WD * w), m2, v2


def _adamw_layer(w, g, m, v, l, prev):
    nl, r, c = w.shape
    tr = _row_tile(r, c * 4 * 2, SUB)

    def body(w_ref, g_ref, m_ref, v_ref, *refs):
        go_ref, d_ref, mo_ref, vo_ref = refs[-4:]
        gv = g_ref[...]
        go_ref[...] = gv
        d_ref[...], mo_ref[...], vo_ref[...] = _adam_update(w_ref[...], gv, m_ref[...], v_ref[...])

    slot = BS((None, tr, c), lambda i: (l, i, 0))
    keep = [] if prev is None else [ANY] * 4
    return pl.pallas_call(
        body, name="adamw_layer", grid=(r // tr,), in_specs=[slot, BS((tr, c), lambda i: (i, 0)), slot, slot] + keep,
        out_specs=[slot] * 4, out_shape=[S((nl, r, c), F32)] * 4,
        input_output_aliases={} if prev is None else {4 + i: i for i in range(4)},
        compiler_params=_cp("arbitrary"),
    )(w, g, m, v, *(prev or ()))


def _blob(arrays):
    flat = jnp.concatenate([a.reshape(-1) for a in arrays])
    rows = -(-flat.shape[0] // (SUB * LANES)) * SUB
    return jnp.pad(flat, (0, rows * LANES - flat.shape[0])).reshape(rows, LANES)


def _unblob(blob, shapes, lead=()):
    flat = blob.reshape(lead + (-1,))
    out, off = [], 0
    for s in shapes:
        size = math.prod(s)
        out.append(flat[..., off:off + size].reshape(lead + tuple(s)))
        off += size
    return out


def _y_rows(w):
    return jnp.concatenate([w[0:A_W], w[A_W + B_W:], w[A_W:A_W + B_W]], axis=0)


def _y_rows_back(g):
    return jnp.concatenate([g[0:A_W], g[A_W + C_W:], g[A_W:A_W + C_W]], axis=0)


SMALL = ("norm_w", "q_norm_w", "k_norm_w", "sinks", "b_conv_b", "b_ln_w", "b_ln_b", "b_pw_b", "c_a_log", "c_dt_bias",
         "c_onorm_w", "b_conv_w", "c_conv_w")
ORDER = ("norm_w", "w_in", "q_norm_w", "k_norm_w", "sinks", "b_conv_w", "b_conv_b", "b_ln_w", "b_ln_b", "b_pw_w", "b_pw_b",
         "c_conv_w", "c_a_log", "c_dt_bias", "c_onorm_w", "w_out")


def kernel(x, positions, norm_w, w_in, q_norm_w, k_norm_w, sinks, b_conv_w, b_conv_b, b_ln_w, b_ln_b, b_pw_w, b_pw_b, c_conv_w, c_a_log, c_dt_bias, c_onorm_w, w_out, loss_target, m_norm_w, m_w_in, m_q_norm_w, m_k_norm_w, m_sinks, m_b_conv_w, m_b_conv_b, m_b_ln_w, m_b_ln_b, m_b_pw_w, m_b_pw_b, m_c_conv_w, m_c_a_log, m_c_dt_bias, m_c_onorm_w, m_w_out, v_norm_w, v_w_in, v_q_norm_w, v_k_norm_w, v_sinks, v_b_conv_w, v_b_conv_b, v_b_ln_w, v_b_ln_b, v_b_pw_w, v_b_pw_b, v_c_conv_w, v_c_a_log, v_c_dt_bias, v_c_onorm_w, v_w_out):
    W = dict(norm_w=norm_w, w_in=w_in, q_norm_w=q_norm_w, k_norm_w=k_norm_w, sinks=sinks, b_conv_w=b_conv_w, b_conv_b=b_conv_b,
             b_ln_w=b_ln_w, b_ln_b=b_ln_b, b_pw_w=b_pw_w, b_pw_b=b_pw_b, c_conv_w=c_conv_w, c_a_log=c_a_log,
             c_dt_bias=c_dt_bias, c_onorm_w=c_onorm_w, w_out=w_out)
    M = dict(norm_w=m_norm_w, w_in=m_w_in, q_norm_w=m_q_norm_w, k_norm_w=m_k_norm_w, sinks=m_sinks, b_conv_w=m_b_conv_w,
             b_conv_b=m_b_conv_b, b_ln_w=m_b_ln_w, b_ln_b=m_b_ln_b, b_pw_w=m_b_pw_w, b_pw_b=m_b_pw_b, c_conv_w=m_c_conv_w,
             c_a_log=m_c_a_log, c_dt_bias=m_c_dt_bias, c_onorm_w=m_c_onorm_w, w_out=m_w_out)
    V = dict(norm_w=v_norm_w, w_in=v_w_in, q_norm_w=v_q_norm_w, k_norm_w=v_k_norm_w, sinks=v_sinks, b_conv_w=v_b_conv_w,
             b_conv_b=v_b_conv_b, b_ln_w=v_b_ln_w, b_ln_b=v_b_ln_b, b_pw_w=v_b_pw_w, b_pw_b=v_b_pw_b, c_conv_w=v_c_conv_w,
             c_a_log=v_c_a_log, c_dt_bias=v_c_dt_bias, c_onorm_w=v_c_onorm_w, w_out=v_w_out)
    nseq, t, d = x.shape
    n = nseq * t
    tr = min(256, t)
    tmm = min(512, n)
    tmw = min(1024, n)
    tkk = min(2048, n)
    me = _dev_index(_my_pos())
    xs = [x.reshape(n, d)]
    tgt = loss_target.reshape(n, d)

    sharded_small = (b_pw_w, b_conv_w, c_conv_w)
    first = _gather2_start([_pack_cols(w_in[0]).astype(BF16), _blob(sharded_small)], positions, "gather0_start")
    tie = first["token"][0, 0]
    tabs = _rope_tables(positions.reshape(n), tie)
    win_p1 = _pack_cols(w_in[1] + tie).astype(BF16)
    wout_b = (w_out + tie).astype(BF16)
    passed = _gather2_forward(first, [tabs[2], win_p1, wout_b], "gather0_forward")
    g_win0, g_small = _gather2_wait(first, passed, "gather0_wait")
    win = [g_win0]
    later = _exchange_start("gather", [win_p1, wout_b[0], wout_b[1]], g_small, "gather_start")
    pw_all, cw_all, ccw_all = _unblob(g_small, [a.shape for a in sharded_small], lead=(N_DEV,))
    pw_all = pw_all.transpose(1, 0, 2, 3).reshape(DEPTH, B_W, B_W).astype(BF16)
    cw_all = jnp.pad(cw_all.transpose(1, 2, 0, 3).reshape(DEPTH, B_K, B_W), ((0, 0), (0, HALO_B - B_K), (0, 0)))
    ccw_all = jnp.pad(ccw_all.transpose(1, 2, 0, 3).reshape(DEPTH, C_K, 3 * C_W), ((0, 0), (0, SUB - C_K), (0, 0)))
    qw_all, kw_all = jnp.tile(q_norm_w, (1, 2)), jnp.tile(k_norm_w, (1, 2))
    lanes6 = lambda a: jnp.zeros((DEPTH, LANES), F32).at[:, GG:GG + C_HEADS].set(a)
    alog_all, dtb_all = lanes6(c_a_log), lanes6(c_dt_bias)

    def layer_params(l):
        row = lambda a: a[l][None]
        return dict(
            nw=row(norm_w), qw=row(qw_all), kw=row(kw_all), sinks=sinks[l], cw=cw_all[l], cb=row(b_conv_b), lnw=row(b_ln_w),
            lnb=row(b_ln_b), pw=pw_all[l], pwb=row(b_pw_b), ccw=ccw_all[l], alog=row(alog_all), dtb=row(dtb_all),
            onw=row(c_onorm_w))

    saved = []
    for l in range(DEPTH):
        q = layer_params(l)
        nw = q["nw"] + later["token"][0:1, 0:1] if l == 0 else q["nw"]
        p, h = _inproj(xs[l], nw, win[l], tm=tmw)
        y, o_a, lse = _attn_fwd(p, tabs, q["qw"], q["kw"], q["sinks"], nseq)
        gates = _gdn_gates_fwd(p, q["alog"], q["dtb"], tm=tr)
        xc, qkv = _gdn_pre_fwd(p, q["ccw"], nseq, tm=tr)
        y, o_c, u, w, tinv, ss = _gdn_chunk_fwd(qkv, gates, p, y, q["onw"], nseq)
        y, hc = _conf_fwd(p, y, q["cw"], q["cb"], q["lnw"], q["lnb"], q["pw"], q["pwb"], nseq, tm=tr)
        saved.append(dict(q=q, p=p, h=h, y=y, o_a=o_a, lse=lse, gates=gates, xc=xc, qkv=qkv, o_c=o_c, u=u, w=w, tinv=tinv,
                          ss=ss, hc=hc))
        if l == 0:
            g_win1, g_wout0, g_wout1 = _exchange_wait(later, y, "gather_wait")
            win.append(g_win1)
            wout = [_y_rows(g_wout0), _y_rows(g_wout1)]
        if l + 1 < DEPTH:
            xs.append(_outproj(xs[l], y, wout[l], tm=tmw, tn=512))
        else:
            dxn, lsum = _outproj_loss(xs[l], y, wout[l], tgt, tm=tmw, tn=512)
    loss = lax.psum(jnp.sum(lsum) * (0.5 / d), ("x", "y", "c"))

    sent, smalls = [None] * DEPTH, [None] * DEPTH
    for l in reversed(range(DEPTH)):
        s = saved[l]
        q, p = s["q"], s["p"]
        dy = _matmul(dxn, wout[l], "nt", F32, tmw, 512, d, "outproj_bwd_dy")
        dwout = _y_rows_back(_matmul(s["y"], dxn, "tn", BF16, 1024, 1024, tkk, "outproj_bwd_dw"))
        dp, dkv, dqw, dkw, dsk = _attn_bwd(p, dy, s["o_a"], s["lse"], tabs, q["qw"], q["kw"], q["sinks"], nseq)
        dp, dqkv, dgate, donw = _gdn_chunk_bwd(s["qkv"], s["gates"], p, dy, dp, q["onw"], s["o_c"], s["u"], s["w"],
                                               s["tinv"], s["ss"], nseq)
        dp, dccw = _gdn_pre_bwd(p, dqkv, s["xc"], dp, q["ccw"], nseq, tm=tr)
        early = P_K // 768
        dwin_a = _matmul(s["h"], dp, "tn", BF16, 1024, 768, tkk, "inproj_bwd_dw_a", b_cols=(0, early))
        sent_a = _exchange_start("scatter", [dwin_a, dwout], donw, "scatter_start_a%d" % l)
        dp = _put_cols(dp, dkv, P_K, sent_a["token"], tm=tmm)
        dp, dal, ddb = _gdn_gates_bwd(dgate, p, q["alog"], q["dtb"], dp, tm=tr)
        dp, dhc, dpw, dpwb, dlnw, dlnb, dcb = _conf_bwd1(p, dy, dp, s["hc"], q["lnw"], q["lnb"], q["pw"], q["pwb"], tm=tr)
        dp, dcw = _conf_bwd2(p, dhc, dp, q["cw"], nseq, tm=tr)
        dwin_b = _matmul(s["h"], dp, "tn", BF16, 1024, 768, tkk, "inproj_bwd_dw_b", b_cols=(early, P_W // 768 - early))
        sent_b = _exchange_start("scatter", [dwin_b, dpw], dpwb, "scatter_start_b%d" % l)
        sent[l] = (sent_a, sent_b)
        dxn, dnw = _inproj_bwd_dx(dp, win[l], xs[l], q["nw"] + sent_b["token"][0:1, 0:1], dxn, tm=tmm)
        halves = lambda a: a.sum(0)[:A_DH] + a.sum(0)[A_DH:]
        smalls[l] = dict(
            norm_w=dnw.sum(0), q_norm_w=halves(dqw), k_norm_w=halves(dkw), sinks=dsk.sum(0)[:A_HEADS], b_conv_b=dcb.sum(0),
            b_ln_w=dlnw.sum(0), b_ln_b=dlnb.sum(0), b_pw_b=dpwb.sum(0), c_a_log=dal.sum(0)[GG:GG + C_HEADS],
            c_dt_bias=ddb.sum(0)[GG:GG + C_HEADS], c_onorm_w=donw.sum(0),
            b_conv_w=dcw.reshape(B_K, SUB, B_W).sum(1), c_conv_w=dccw.reshape(C_K, SUB, 3 * C_W).sum(1))
    grad_x = dxn.reshape(nseq, t, d)

    G, delta, new_m, new_v = {}, {}, {}, {}
    big = ("w_in", "w_out", "b_pw_w")
    stacks = {k: None for k in big}
    after = dxn
    g_t = [None] * DEPTH
    for l in reversed(range(DEPTH)):
        r_win_a, r_wout = _exchange_wait(sent[l][0], after, "scatter_wait_a%d" % l)
        r_win_b, r_pw = _exchange_wait(sent[l][1], r_wout, "scatter_wait_b%d" % l)
        g_t[l] = jnp.concatenate([_sum8_t(r_win_a), _sum8_t(r_win_b)], axis=0).reshape(P_W, -1, LANES)
        for k, r in (("w_out", r_wout), ("b_pw_w", r_pw)):
            stacks[k] = _adamw_layer(W[k], _sum8(r), M[k], V[k], l, stacks[k])
        after = stacks["w_out"][1]
    g_in = jnp.stack(g_t, axis=2).reshape(-1, LANES)
    g_in = _unpack_cols(g_in, axis=0, each=g_in.shape[0] // P_W)
    rows = _adamw_rows(_rows_view(w_in), g_in, _rows_view(m_w_in), _rows_view(v_w_in))
    stacks["w_in"] = [_rows_view_back(a, w_in.shape) for a in (g_in, *rows)]
    for k in big:
        G[k], delta[k], new_m[k], new_v[k] = stacks[k]
    part = _blob([jnp.stack([smalls[l][k] for l in range(DEPTH)]) for k in SMALL])
    (tot,) = _all_gather([part], after=rows[0])
    tot = _sum8(tot.reshape(N_DEV, part.shape[0], LANES))
    full_shapes = [(DEPTH,) + smalls[0][k].shape for k in SMALL]
    for k, g in zip(SMALL, _unblob(tot, full_shapes)):
        G[k] = g
    G["b_conv_w"] = lax.dynamic_slice_in_dim(G["b_conv_w"], me * (B_W // N_DEV), B_W // N_DEV, axis=2)
    G["c_conv_w"] = lax.dynamic_slice_in_dim(G["c_conv_w"], me * (3 * C_W // N_DEV), 3 * C_W // N_DEV, axis=2)
    dl, mo, vo = _adamw(*[_blob([src[k] for k in SMALL]) for src in (W, G, M, V)])
    shapes = [W[k].shape for k in SMALL]
    for k, a, b, c in zip(SMALL, _unblob(dl, shapes), _unblob(mo, shapes), _unblob(vo, shapes)):
        delta[k], new_m[k], new_v[k] = a, b, c
    return (loss, grad_x, *[G[k] for k in ORDER], *[delta[k] for k in ORDER], *[new_m[k] for k in ORDER],
            *[new_v[k] for k in ORDER])
```

```python
import math

import jax
import jax.numpy as jnp
from jax import lax
from jax.experimental import pallas as pl
from jax.experimental.pallas import tpu as pltpu

F32 = jnp.float32
BF16 = jnp.bfloat16
HI = lax.Precision.HIGHEST
MESH = pl.DeviceIdType.MESH
S = jax.ShapeDtypeStruct
BS = pl.BlockSpec

N_DEV = 8
DEPTH = 2
D_MODEL = 2048
A_HEADS, A_KV, A_DH, A_W, A_KVW = 12, 4, 64, 768, 256
ROT = 16
THETA = 500000.0
ABLK = 128
B_W, B_K = 512, 31
C_HEADS, C_DH, C_W, C_K, CHUNK = 6, 128, 768, 4, 64
EPS = 1e-6
IN_COLS = 6668
P_Q, P_ZA, P_ZC, P_QKV, P_K, P_V, P_UB, P_ZB, P_BA, P_W = 0, 768, 1536, 2304, 4608, 4864, 5120, 6144, 6656, 6912
Y_A, Y_C, Y_B = 0, 768, 1536
LANES = 128
SUB = 8

ADAM_LR, ADAM_B1, ADAM_B2, ADAM_EPS, ADAM_WD, ADAM_STEP = 0.001, 0.9, 0.999, 1e-08, 0.01, 10


def _cp(*sem, vmem=None):
    kw = {}
    if sem:
        kw["dimension_semantics"] = sem
    if vmem:
        kw["vmem_limit_bytes"] = vmem
    return pltpu.CompilerParams(**kw)


def _pack_cols(w):
    z = jnp.zeros(w.shape[:-1] + (P_W - IN_COLS,), w.dtype)
    return jnp.concatenate([w[..., 0:768], w[..., 1280:2048], w[..., 5900:6668], w[..., 3584:5888],
                            w[..., 768:1024], w[..., 1024:1280], w[..., 2048:3072], w[..., 3072:3584],
                            w[..., 5888:5900], z], axis=-1)


def _unpack_cols(g, axis=-1, each=1):
    parts = ((P_Q, 768), (P_K, 256), (P_V, 256), (P_ZA, 768), (P_UB, 1024), (P_ZB, 512), (P_QKV, 2304), (P_BA, 12), (P_ZC, 768))
    return jnp.concatenate([lax.slice_in_dim(g, each * o, each * (o + n), axis=axis) for o, n in parts], axis=axis)


def _sigmoid(x):
    return 0.5 * jnp.tanh(0.5 * x) + 0.5


def _dsilu(x, sg):
    return sg * (1.0 + x * (1.0 - sg))


def _fold8(x):
    r, c = x.shape
    return x.reshape(r // SUB, SUB, c).sum(axis=0)


def _dot(a, b, prec=None):
    return jnp.dot(a, b, preferred_element_type=F32, precision=prec)


def _dot_nt(a, b, prec=None):
    return lax.dot_general(a, b, (((1,), (1,)), ((), ())), preferred_element_type=F32, precision=prec)


def _dot_tn(a, b, prec=None):
    return lax.dot_general(a, b, (((0,), (0,)), ((), ())), preferred_element_type=F32, precision=prec)


def _lane(shape):
    return lax.broadcasted_iota(jnp.int32, shape, 1)


def _subl(shape):
    return lax.broadcasted_iota(jnp.int32, shape, 0)


def _col(x, j):
    return jnp.sum(jnp.where(_lane(x.shape) == j, x, 0.0), axis=-1, keepdims=True)


def _inproj(x, nw, w, tm=512, tn=768):
    n, d = x.shape
    pw = w.shape[1]

    def body(x_ref, nw_ref, w_ref, p_ref, h_ref):
        @pl.when(pl.program_id(1) == 0)
        def _():
            xv = x_ref[...]
            r = lax.rsqrt(jnp.mean(xv * xv, axis=-1, keepdims=True) + EPS)
            h_ref[...] = (xv * r * nw_ref[...]).astype(BF16)

        p_ref[...] = _dot(h_ref[...], w_ref[...])

    return pl.pallas_call(
        body, name="inproj", grid=(n // tm, pw // tn),
        in_specs=[BS((tm, d), lambda i, j: (i, 0)), BS((1, d), lambda i, j: (0, 0)), BS((d, tn), lambda i, j: (0, j))],
        out_specs=[BS((tm, tn), lambda i, j: (i, j)), BS((tm, d), lambda i, j: (i, 0))],
        out_shape=[S((n, pw), F32), S((n, d), BF16)],
        compiler_params=_cp("arbitrary", "arbitrary"),
    )(x, nw, w)


def _outproj(x, y, w, tm=512, tn=1024):
    n, d = x.shape
    k = y.shape[1]

    def body(x_ref, y_ref, w_ref, o_ref):
        o_ref[...] = x_ref[...] + _dot(y_ref[...], w_ref[...])

    return pl.pallas_call(
        body, name="outproj", grid=(n // tm, d // tn),
        in_specs=[BS((tm, tn), lambda i, j: (i, j)), BS((tm, k), lambda i, j: (i, 0)), BS((k, tn), lambda i, j: (0, j))],
        out_specs=BS((tm, tn), lambda i, j: (i, j)),
        out_shape=S((n, d), F32),
        compiler_params=_cp("arbitrary", "arbitrary"),
    )(x, y, w)


def _outproj_loss(x, y, w, tgt, tm=512, tn=1024):
    n, d = x.shape
    k = y.shape[1]

    def body(x_ref, y_ref, w_ref, t_ref, g_ref, l_ref):
        @pl.when((pl.program_id(0) == 0) & (pl.program_id(1) == 0))
        def _():
            l_ref[...] = jnp.zeros_like(l_ref)

        diff = x_ref[...] + _dot(y_ref[...], w_ref[...]) - t_ref[...]
        g_ref[...] = diff * (1.0 / d)
        f = _fold8(diff * diff)
        acc = f[:, 0:LANES]
        for c in range(1, tn // LANES):
            acc = acc + f[:, c * LANES:(c + 1) * LANES]
        l_ref[...] += acc

    return pl.pallas_call(
        body, name="outproj_loss", grid=(n // tm, d // tn),
        in_specs=[BS((tm, tn), lambda i, j: (i, j)), BS((tm, k), lambda i, j: (i, 0)), BS((k, tn), lambda i, j: (0, j)),
                  BS((tm, tn), lambda i, j: (i, j))],
        out_specs=[BS((tm, tn), lambda i, j: (i, j)), BS((SUB, LANES), lambda i, j: (0, 0))],
        out_shape=[S((n, d), F32), S((SUB, LANES), F32)],
        compiler_params=_cp("arbitrary", "arbitrary"),
    )(x, y, w, tgt)


def _matmul(a, b, mode, out_dtype, tm, tn, tk, name, b_cols=None):
    if mode == "nn":
        (m, kk), nn = a.shape, b.shape[1]
        a_spec, b_spec = BS((tm, tk), lambda i, j, k: (i, k)), BS((tk, tn), lambda i, j, k: (k, j))
        dot = _dot
    elif mode == "nt":
        (m, kk), nn = a.shape, b.shape[0]
        a_spec, b_spec = BS((tm, tk), lambda i, j, k: (i, k)), BS((tn, tk), lambda i, j, k: (j, k))
        dot = _dot_nt
    else:
        j0, nj = b_cols or (0, b.shape[1] // tn)
        (kk, m), nn = a.shape, nj * tn
        a_spec, b_spec = BS((tk, tm), lambda i, j, k: (k, i)), BS((tk, tn), lambda i, j, k: (k, j0 + j))
        dot = _dot_tn
    nk = kk // tk

    def body(a_ref, b_ref, o_ref, acc_ref):
        kid = pl.program_id(2)

        @pl.when(kid == 0)
        def _():
            acc_ref[...] = jnp.zeros_like(acc_ref)

        acc_ref[...] += dot(a_ref[...].astype(BF16), b_ref[...].astype(BF16))

        @pl.when(kid == nk - 1)
        def _():
            o_ref[...] = acc_ref[...].astype(out_dtype)

    return pl.pallas_call(
        body, name=name, grid=(m // tm, nn // tn, nk),
        in_specs=[a_spec, b_spec], out_specs=BS((tm, tn), lambda i, j, k: (i, j)),
        out_shape=S((m, nn), out_dtype), scratch_shapes=[pltpu.VMEM((tm, tn), F32)],
        compiler_params=_cp("arbitrary", "arbitrary", "arbitrary"),
    )(a, b)


SLAB = 16


def _inproj_bwd_dx(dp, w, x, nw, dres, tm=512, tk=768):
    n, d = x.shape
    nk = dp.shape[1] // tk

    def body(dp_ref, w_ref, x_ref, nw_ref, dr_ref, dx_ref, dnw_ref, acc_ref):
        kid = pl.program_id(1)

        @pl.when((pl.program_id(0) == 0) & (kid == 0))
        def _():
            dnw_ref[...] = jnp.zeros_like(dnw_ref)

        @pl.when(kid == 0)
        def _():
            acc_ref[...] = jnp.zeros_like(acc_ref)

        acc_ref[...] += _dot_nt(dp_ref[...], w_ref[...])

        @pl.when(kid == nk - 1)
        def _():
            def slab(i, carry):
                rows = pl.ds(pl.multiple_of(i * SLAB, SLAB), SLAB)
                dh, xv = acc_ref[rows, :], x_ref[rows, :]
                r = lax.rsqrt(jnp.mean(xv * xv, axis=-1, keepdims=True) + EPS)
                dnw_ref[...] += _fold8(dh * xv * r)
                g = dh * nw_ref[...]
                mm = jnp.mean(g * xv, axis=-1, keepdims=True)
                dx_ref[rows, :] = dr_ref[rows, :] + r * g - xv * (r * r * r * mm)
                return carry

            lax.fori_loop(0, tm // SLAB, slab, 0)

    return pl.pallas_call(
        body, name="inproj_bwd_dx", grid=(n // tm, nk),
        in_specs=[BS((tm, tk), lambda i, k: (i, k)), BS((d, tk), lambda i, k: (0, k)), BS((tm, d), lambda i, k: (i, 0)),
                  BS((1, d), lambda i, k: (0, 0)), BS((tm, d), lambda i, k: (i, 0))],
        out_specs=[BS((tm, d), lambda i, k: (i, 0)), BS((SUB, d), lambda i, k: (0, 0))],
        out_shape=[S((n, d), F32), S((SUB, d), F32)],
        scratch_shapes=[pltpu.VMEM((tm, d), F32)],
        compiler_params=_cp("arbitrary", "arbitrary"),
    )(dp, w, x, nw, dres)


def _rope_tables(pos):
    half = ROT // 2
    inv = THETA ** (-jnp.arange(0, ROT, 2, dtype=F32) / ROT)
    ang = pos.astype(F32)[:, None] * inv
    cos, sin = jnp.cos(ang), jnp.sin(ang)
    n = pos.shape[0]
    one = jnp.ones((n, A_DH - ROT), F32)
    zero = jnp.zeros((n, A_DH - ROT), F32)
    zh = jnp.zeros((n, half), F32)
    c = jnp.concatenate([cos, cos, one], axis=1)
    s1 = jnp.concatenate([-sin, zh, zero], axis=1)
    s2 = jnp.concatenate([zh, sin, zero], axis=1)
    return tuple(jnp.concatenate([t, t], axis=1) for t in (c, s1, s2))


def _half_stat(t):
    lo = _lane(t.shape) < A_DH
    s_lo = jnp.sum(jnp.where(lo, t, 0.0), axis=-1, keepdims=True)
    s_hi = jnp.sum(jnp.where(lo, 0.0, t), axis=-1, keepdims=True)
    return jnp.where(lo, s_lo, s_hi)


def _normrope(x, w, c, s1, s2):
    r = lax.rsqrt(_half_stat(x * x) * (1.0 / A_DH) + EPS)
    xn = x * r * w
    return xn * c + pltpu.roll(xn, LANES - ROT // 2, 1) * s1 + pltpu.roll(xn, ROT // 2, 1) * s2, r


def _normrope_bwd(dy, x, r, w, c, s1, s2):
    dxn = dy * c + pltpu.roll(dy * s1, ROT // 2, 1) + pltpu.roll(dy * s2, LANES - ROT // 2, 1)
    g = dxn * w
    mm = _half_stat(g * x) * (1.0 / A_DH)
    return r * g - x * (r * r * r * mm), dxn * x * r


def _attn_mask(first):
    qi = _subl((ABLK, 2 * ABLK))
    kj = _lane((ABLK, 2 * ABLK))
    dist = qi + ABLK - kj
    return (dist >= 0) & (dist < ABLK) & (jnp.logical_not(first) | (kj >= ABLK))


def _keep_half(x, b):
    lo = _lane(x.shape) < A_DH
    return jnp.where(lo if b == 0 else jnp.logical_not(lo), x, jnp.zeros_like(x))


def _head_operand(x, j):
    a, b = j % 2, (j // 3) % 2
    return _keep_half(x if a == b else pltpu.roll(x, A_DH, 1), b)


def _head_result(x, j):
    a, b = j % 2, (j // 3) % 2
    return _keep_half(x if a == b else pltpu.roll(x, A_DH, 1), a)


def _attn_fwd(p, tabs, qw, kw, sinks, nseq):
    n = p.shape[0]
    nb = n // nseq // ABLK
    cur = lambda b, i: (b * nb + i, 0)
    prv = lambda b, i: (b * nb + jnp.maximum(i - 1, 0), 0)
    colblk = lambda f, w, off: (lambda b, i: (f(b, i)[0], off // w))

    def body(q_ref, za_ref, kc_ref, vc_ref, kp_ref, vp_ref, c_ref, s1_ref, s2_ref, cp_ref, s1p_ref, s2p_ref,
             qw_ref, kw_ref, sink_ref, y_ref, o_ref, lse_ref):
        first = pl.program_id(1) == 0
        tc = (c_ref[...], s1_ref[...], s2_ref[...])
        tp = (cp_ref[...], s1p_ref[...], s2p_ref[...])
        q, kc, kp = q_ref[...], kc_ref[...], kp_ref[...]
        qn = [_normrope(q[:, LANES * b:LANES * (b + 1)], qw_ref[...], *tc)[0] for b in range(A_W // LANES)]
        k2, v2 = [], []
        for b in range(A_KVW // LANES):
            sl = slice(LANES * b, LANES * (b + 1))
            k2.append(jnp.concatenate([_normrope(kp[:, sl], kw_ref[...], *tp)[0],
                                       _normrope(kc[:, sl], kw_ref[...], *tc)[0]], axis=0).astype(BF16))
            v2.append(jnp.concatenate([vp_ref[:, sl], vc_ref[:, sl]], axis=0).astype(BF16))
        valid = _attn_mask(first)
        heads = range(A_HEADS)
        qm = [_head_operand(qn[j // 2], j).astype(BF16) for j in heads]
        s = [jnp.where(valid, _dot_nt(qm[j], k2[j // 6]) * (A_DH ** -0.5), -jnp.inf) for j in heads]
        m = [jnp.maximum(jnp.max(s[j], axis=-1, keepdims=True), sink_ref[j]) for j in heads]
        e = [jnp.exp(s[j] - m[j]) for j in heads]
        den = [jnp.sum(e[j], axis=-1, keepdims=True) + jnp.exp(sink_ref[j] - m[j]) for j in heads]
        outs = [_head_result(_dot((e[j] * (1.0 / den[j])).astype(BF16), v2[j // 6]), j) for j in heads]
        lse = jnp.zeros((ABLK, LANES), F32)
        for j in heads:
            lse = jnp.where(_lane(lse.shape) == j, m[j] + jnp.log(den[j]), lse)
        o = jnp.concatenate([outs[2 * b] + outs[2 * b + 1] for b in range(A_W // LANES)], axis=1)
        za = za_ref[...]
        o_ref[...] = o
        lse_ref[...] = lse
        y_ref[...] = (o * (za * _sigmoid(za))).astype(BF16)

    tab_specs = [BS((ABLK, LANES), cur)] * 3 + [BS((ABLK, LANES), prv)] * 3
    return pl.pallas_call(
        body, name="attn_fwd", grid=(nseq, nb),
        in_specs=[BS((ABLK, A_W), colblk(cur, A_W, P_Q)), BS((ABLK, A_W), colblk(cur, A_W, P_ZA)),
                  BS((ABLK, A_KVW), colblk(cur, A_KVW, P_K)), BS((ABLK, A_KVW), colblk(cur, A_KVW, P_V)),
                  BS((ABLK, A_KVW), colblk(prv, A_KVW, P_K)), BS((ABLK, A_KVW), colblk(prv, A_KVW, P_V))]
        + tab_specs + [BS((1, LANES), lambda b, i: (0, 0))] * 2 + [BS(memory_space=pltpu.SMEM)],
        out_specs=[BS((ABLK, A_W), colblk(cur, A_W, Y_A)), BS((ABLK, A_W), cur), BS((ABLK, LANES), cur)],
        out_shape=[S((n, D_MODEL), BF16), S((n, A_W), F32), S((n, LANES), F32)],
        compiler_params=_cp("arbitrary", "arbitrary"),
    )(p, p, p, p, p, p, *tabs, *tabs, qw, kw, sinks)


def _attn_bwd(p, dy, o, lse, tabs, qw, kw, sinks, nseq):
    n = p.shape[0]
    nb = n // nseq // ABLK
    cur = lambda b, i: (b * nb + jnp.minimum(i, nb - 1), 0)
    prv = lambda b, i: (b * nb + jnp.maximum(i - 1, 0), 0)
    colblk = lambda f, w, off: (lambda b, i: (f(b, i)[0], off // w))

    def body(q_ref, za_ref, kc_ref, vc_ref, kp_ref, vp_ref, dy_ref, o_ref, lse_ref,
             c_ref, s1_ref, s2_ref, cp_ref, s1p_ref, s2p_ref, qw_ref, kw_ref, sink_ref,
             dqza_ref, dkv_ref, dqw_ref, dkw_ref, dsk_ref, tk_ref, tv_ref, ck_ref, cv_ref):
        i = pl.program_id(1)
        first = i == 0
        tc = (c_ref[...], s1_ref[...], s2_ref[...])
        tp = (cp_ref[...], s1p_ref[...], s2p_ref[...])
        nkb = A_KVW // LANES

        @pl.when((pl.program_id(0) == 0) & first)
        def _():
            dqw_ref[...] = jnp.zeros_like(dqw_ref)
            dkw_ref[...] = jnp.zeros_like(dkw_ref)
            dsk_ref[...] = jnp.zeros_like(dsk_ref)

        @pl.when(i < nb)
        def _():
            q, kc, kp = q_ref[...], kc_ref[...], kp_ref[...]
            qn, rq = [], []
            for b in range(A_W // LANES):
                a, r = _normrope(q[:, LANES * b:LANES * (b + 1)], qw_ref[...], *tc)
                qn.append(a)
                rq.append(r)
            k2, v2 = [], []
            for b in range(nkb):
                sl = slice(LANES * b, LANES * (b + 1))
                k2.append(jnp.concatenate([_normrope(kp[:, sl], kw_ref[...], *tp)[0],
                                           _normrope(kc[:, sl], kw_ref[...], *tc)[0]], axis=0).astype(BF16))
                v2.append(jnp.concatenate([vp_ref[:, sl], vc_ref[:, sl]], axis=0).astype(BF16))
            valid = _attn_mask(first)
            za, dy, o, lse = za_ref[...], dy_ref[...], o_ref[...], lse_ref[...]
            sg = _sigmoid(za)
            do = dy * za * sg
            dqza_ref[:, A_W:2 * A_W] = (dy * o * _dsilu(za, sg)).astype(BF16)
            heads = range(A_HEADS)
            blk = lambda x, b: x[:, LANES * b:LANES * (b + 1)]
            qm = [_head_operand(qn[j // 2], j).astype(BF16) for j in heads]
            lj = [_col(lse, j) for j in heads]
            pr = [jnp.exp(jnp.where(valid, _dot_nt(qm[j], k2[j // 6]) * (A_DH ** -0.5), -jnp.inf) - lj[j]) for j in heads]
            dom = [_head_operand(blk(do, j // 2), j).astype(BF16) for j in heads]
            delta = [jnp.sum(_keep_half(blk(do, j // 2) * blk(o, j // 2), j % 2), axis=-1, keepdims=True) for j in heads]
            ds = [(pr[j] * (_dot_nt(dom[j], v2[j // 6]) - delta[j]) * (A_DH ** -0.5)).astype(BF16) for j in heads]
            dqh = [_head_result(_dot(ds[j], k2[j // 6]), j) for j in heads]
            dkh = [_dot_tn(ds[j], qm[j]) for j in heads]
            dvh = [_dot_tn(pr[j].astype(BF16), dom[j]) for j in heads]
            per_blk = A_HEADS // nkb
            dks = [sum(dkh[per_blk * g + 1:per_blk * (g + 1)], dkh[per_blk * g]) for g in range(nkb)]
            dvs = [sum(dvh[per_blk * g + 1:per_blk * (g + 1)], dvh[per_blk * g]) for g in range(nkb)]
            dsk = jnp.zeros((ABLK, LANES), F32)
            for j in heads:
                dsk = dsk + jnp.where(_lane(dsk.shape) == j, -jnp.exp(sink_ref[j] - lj[j]) * delta[j], 0.0)
            dsk_ref[...] += _fold8(dsk)
            dqn = jnp.concatenate([dqh[2 * b] + dqh[2 * b + 1] for b in range(A_W // LANES)], axis=1)
            dqw = jnp.zeros((SUB, LANES), F32)
            dqo = []
            for b in range(A_W // LANES):
                sl = slice(LANES * b, LANES * (b + 1))
                dx, wt = _normrope_bwd(dqn[:, sl], q[:, sl], rq[b], qw_ref[...], *tc)
                dqo.append(dx)
                dqw = dqw + _fold8(wt)
            dqw_ref[...] += dqw
            dqza_ref[:, 0:A_W] = jnp.concatenate(dqo, axis=1).astype(BF16)
            tk_ref[...] = jnp.concatenate(dks, axis=1)
            tv_ref[...] = jnp.concatenate(dvs, axis=1)

        @pl.when(i == nb)
        def _():
            tk_ref[...] = jnp.zeros_like(tk_ref)
            tv_ref[...] = jnp.zeros_like(tv_ref)

        @pl.when(i > 0)
        def _():
            kp = kp_ref[...]
            dkn = ck_ref[...] + tk_ref[0:ABLK, :]
            dkw = jnp.zeros((SUB, LANES), F32)
            dko = []
            for b in range(nkb):
                sl = slice(LANES * b, LANES * (b + 1))
                r = _normrope(kp[:, sl], kw_ref[...], *tp)[1]
                dx, wt = _normrope_bwd(dkn[:, sl], kp[:, sl], r, kw_ref[...], *tp)
                dko.append(dx)
                dkw = dkw + _fold8(wt)
            dkw_ref[...] += dkw
            dkv_ref[:, 0:A_KVW] = jnp.concatenate(dko, axis=1).astype(BF16)
            dkv_ref[:, A_KVW:2 * A_KVW] = (cv_ref[...] + tv_ref[0:ABLK, :]).astype(BF16)

        ck_ref[...] = tk_ref[ABLK:2 * ABLK, :]
        cv_ref[...] = tv_ref[ABLK:2 * ABLK, :]

    tab_specs = [BS((ABLK, LANES), cur)] * 3 + [BS((ABLK, LANES), prv)] * 3
    acc = BS((SUB, LANES), lambda b, i: (0, 0))
    return pl.pallas_call(
        body, name="attn_bwd", grid=(nseq, nb + 1),
        in_specs=[BS((ABLK, A_W), colblk(cur, A_W, P_Q)), BS((ABLK, A_W), colblk(cur, A_W, P_ZA)),
                  BS((ABLK, A_KVW), colblk(cur, A_KVW, P_K)), BS((ABLK, A_KVW), colblk(cur, A_KVW, P_V)),
                  BS((ABLK, A_KVW), colblk(prv, A_KVW, P_K)), BS((ABLK, A_KVW), colblk(prv, A_KVW, P_V)),
                  BS((ABLK, A_W), colblk(cur, A_W, Y_A)), BS((ABLK, A_W), cur), BS((ABLK, LANES), cur)]
        + tab_specs + [BS((1, LANES), lambda b, i: (0, 0))] * 2 + [BS(memory_space=pltpu.SMEM)],
        out_specs=[BS((ABLK, 2 * A_W), cur), BS((ABLK, 2 * A_KVW), prv), acc, acc, acc],
        out_shape=[S((n, P_W), BF16), S((n, 2 * A_KVW), BF16)] + [S((SUB, LANES), F32)] * 3,
        scratch_shapes=[pltpu.VMEM((2 * ABLK, A_KVW), F32)] * 2 + [pltpu.VMEM((ABLK, A_KVW), F32)] * 2,
        compiler_params=_cp("arbitrary", "arbitrary"),
    )(p, p, p, p, p, p, dy, o, lse, *tabs, *tabs, qw, kw, sinks)


def _put_cols(dst, src, col_off, after, tm=512):
    n, w = src.shape

    def body(s_ref, d_in_ref, after_ref, d_ref):
        d_ref[...] = s_ref[...]

    return pl.pallas_call(
        body, name="put_cols", grid=(n // tm,),
        in_specs=[BS((tm, w), lambda i: (i, 0)), BS(memory_space=pl.ANY), BS(memory_space=pl.ANY)],
        out_specs=BS((tm, w), lambda i: (i, col_off // w)),
        out_shape=S(dst.shape, dst.dtype), input_output_aliases={1: 0},
        compiler_params=_cp("arbitrary"),
    )(src, dst, after)


HALO_B = 32


def _layernorm(hc, lnw, lnb):
    mu = jnp.mean(hc, axis=-1, keepdims=True)
    xc = hc - mu
    rstd = lax.rsqrt(jnp.mean(xc * xc, axis=-1, keepdims=True) + EPS)
    xhat = xc * rstd
    return xhat, rstd, xhat * lnw + lnb


def _shifted_copies(buf_ref, sh_ref):
    rows = sh_ref.shape[1]
    for b in range(1, SUB):
        sh_ref[b - 1] = buf_ref[pl.ds(b, rows), :]


def _rows_from(buf_ref, sh_ref, off, rows, cols=slice(None)):
    a, b = divmod(off, SUB)
    if b == 0:
        return buf_ref[pl.ds(SUB * a, rows), cols]
    return sh_ref[b - 1, pl.ds(SUB * a, rows), cols]


def _conf_fwd(p, y, cw, cb, lnw, lnb, pw, pwb, nseq, tm=256):
    n = p.shape[0]
    t = n // nseq
    nt = t // tm
    row = lambda b, i: b * nt + i
    halo = lambda b, i: jnp.maximum((b * t + i * tm) // HALO_B - 1, 0)
    vec = BS((1, B_W), lambda b, i: (0, 0))

    def body(ub_ref, uh_ref, zb_ref, cw_ref, cb_ref, lnw_ref, lnb_ref, pw_ref, pwb_ref, y_in_ref, y_ref, hc_ref, buf_ref, sh_ref):
        ub, uh = ub_ref[...], uh_ref[...]
        hh = uh[:, :B_W] * _sigmoid(uh[:, B_W:])
        buf_ref[0:HALO_B, :] = jnp.where(pl.program_id(1) > 0, hh, 0.0)
        buf_ref[HALO_B:, :] = ub[:, :B_W] * _sigmoid(ub[:, B_W:])
        _shifted_copies(buf_ref, sh_ref)
        hc = jnp.zeros((tm, B_W), F32) + cb_ref[...]
        for k in range(B_K):
            hc = hc + cw_ref[k:k + 1, :] * _rows_from(buf_ref, sh_ref, HALO_B - B_K + 1 + k, tm)
        hc_ref[...] = hc
        ln = _layernorm(hc, lnw_ref[...], lnb_ref[...])[2]
        sw = ln * _sigmoid(ln)
        ob = _dot(sw.astype(BF16), pw_ref[...]) + pwb_ref[...]
        zb = zb_ref[...]
        y_ref[...] = (ob * (zb * _sigmoid(zb))).astype(BF16)

    return pl.pallas_call(
        body, name="conf_fwd", grid=(nseq, nt),
        in_specs=[BS((tm, 2 * B_W), lambda b, i: (row(b, i), P_UB // (2 * B_W))),
                  BS((HALO_B, 2 * B_W), lambda b, i: (halo(b, i), P_UB // (2 * B_W))),
                  BS((tm, B_W), lambda b, i: (row(b, i), P_ZB // B_W)),
                  BS((HALO_B, B_W), lambda b, i: (0, 0)), vec, vec, vec, BS((B_W, B_W), lambda b, i: (0, 0)), vec,
                  BS(memory_space=pl.ANY)],
        out_specs=[BS((tm, B_W), lambda b, i: (row(b, i), Y_B // B_W)), BS((tm, B_W), lambda b, i: (row(b, i), 0))],
        out_shape=[S(y.shape, y.dtype), S((n, B_W), F32)], input_output_aliases={9: 0},
        scratch_shapes=[pltpu.VMEM((HALO_B + tm, B_W), F32), pltpu.VMEM((SUB - 1, HALO_B + tm - SUB, B_W), F32)],
        compiler_params=_cp("arbitrary", "arbitrary"),
    )(p, p, p, cw, cb, lnw, lnb, pw, pwb, y)


def _conf_bwd1(p, dy, dp, hc, lnw, lnb, pw, pwb, tm=256):
    n = p.shape[0]
    vec = BS((1, B_W), lambda i: (0, 0))
    acc = BS((SUB, B_W), lambda i: (0, 0))

    def body(dy_ref, zb_ref, hc_ref, lnw_ref, lnb_ref, pw_ref, pwb_ref, dp_in_ref,
             dzb_ref, dhc_ref, dpw_ref, dpwb_ref, dlnw_ref, dlnb_ref, dcb_ref):
        @pl.when(pl.program_id(0) == 0)
        def _():
            for r in (dpw_ref, dpwb_ref, dlnw_ref, dlnb_ref, dcb_ref):
                r[...] = jnp.zeros_like(r)

        xhat, rstd, ln = _layernorm(hc_ref[...], lnw_ref[...], lnb_ref[...])
        sgl = _sigmoid(ln)
        sw = (ln * sgl).astype(BF16)
        ob = _dot(sw, pw_ref[...]) + pwb_ref[...]
        dy, zb = dy_ref[...], zb_ref[...]
        sgz = _sigmoid(zb)
        dzb_ref[...] = (dy * ob * _dsilu(zb, sgz)).astype(BF16)
        dob = dy * zb * sgz
        dobb = dob.astype(BF16)
        dpwb_ref[...] += _fold8(dob)
        dpw_ref[...] += _dot_tn(sw, dobb)
        dln = _dot_nt(dobb, pw_ref[...]) * _dsilu(ln, sgl)
        dlnw_ref[...] += _fold8(dln * xhat)
        dlnb_ref[...] += _fold8(dln)
        dxh = dln * lnw_ref[...]
        dhc = rstd * (dxh - jnp.mean(dxh, axis=-1, keepdims=True) - xhat * jnp.mean(dxh * xhat, axis=-1, keepdims=True))
        dcb_ref[...] += _fold8(dhc)
        dhc_ref[...] = dhc

    return pl.pallas_call(
        body, name="conf_bwd1", grid=(n // tm,),
        in_specs=[BS((tm, B_W), lambda i: (i, Y_B // B_W)), BS((tm, B_W), lambda i: (i, P_ZB // B_W)),
                  BS((tm, B_W), lambda i: (i, 0)), vec, vec, BS((B_W, B_W), lambda i: (0, 0)), vec,
                  BS(memory_space=pl.ANY)],
        out_specs=[BS((tm, B_W), lambda i: (i, P_ZB // B_W)), BS((tm, B_W), lambda i: (i, 0)),
                   BS((B_W, B_W), lambda i: (0, 0)), acc, acc, acc, acc],
        out_shape=[S(dp.shape, dp.dtype), S((n, B_W), F32), S((B_W, B_W), F32)] + [S((SUB, B_W), F32)] * 4,
        input_output_aliases={7: 0},
        compiler_params=_cp("arbitrary"),
    )(dy, p, hc, lnw, lnb, pw, pwb, dp)


def _conf_bwd2(p, dhc, dp, cw, nseq, tm=256):
    n = p.shape[0]
    t = n // nseq
    nt = t // tm
    row = lambda b, i: b * nt + i
    prev = lambda b, i: jnp.maximum((b * t + i * tm) // HALO_B - 1, 0)
    nxt = lambda b, i: jnp.minimum((b * t + (i + 1) * tm) // HALO_B, n // HALO_B - 1)

    def body(ub_ref, uh_ref, dh_ref, dn_ref, cw_ref, dp_in_ref, dub_ref, dcw_ref, buf_ref, dbuf_ref, sh_ref, dsh_ref):
        i = pl.program_id(1)

        @pl.when((pl.program_id(0) == 0) & (i == 0))
        def _():
            dcw_ref[...] = jnp.zeros_like(dcw_ref)

        uh = uh_ref[...]
        buf_ref[0:HALO_B, :] = jnp.where(i > 0, uh[:, :B_W] * _sigmoid(uh[:, B_W:]), 0.0)
        buf_ref[HALO_B:, :] = ub_ref[:, :B_W] * _sigmoid(ub_ref[:, B_W:])
        dbuf_ref[0:tm, :] = dh_ref[...]
        dbuf_ref[tm:, :] = jnp.where(i < nt - 1, dn_ref[...], 0.0)
        _shifted_copies(buf_ref, sh_ref)
        _shifted_copies(dbuf_ref, dsh_ref)
        for c in range(B_W // LANES):
            cs, gs = slice(LANES * c, LANES * (c + 1)), slice(B_W + LANES * c, B_W + LANES * (c + 1))
            for r0 in range(0, tm, LANES):
                dhc = dh_ref[r0:r0 + LANES, cs]
                dhg = jnp.zeros((LANES, LANES), F32)
                for k in range(B_K):
                    dhg = dhg + cw_ref[k:k + 1, cs] * _rows_from(dbuf_ref, dsh_ref, r0 + B_K - 1 - k, LANES, cs)
                    dcw_ref[SUB * k:SUB * (k + 1), cs] += _fold8(
                        dhc * _rows_from(buf_ref, sh_ref, r0 + HALO_B - B_K + 1 + k, LANES, cs))
                a, sg = ub_ref[r0:r0 + LANES, cs], _sigmoid(ub_ref[r0:r0 + LANES, gs])
                dub_ref[r0:r0 + LANES, cs] = (dhg * sg).astype(BF16)
                dub_ref[r0:r0 + LANES, gs] = (dhg * a * sg * (1.0 - sg)).astype(BF16)

    return pl.pallas_call(
        body, name="conf_bwd2", grid=(nseq, nt),
        in_specs=[BS((tm, 2 * B_W), lambda b, i: (row(b, i), P_UB // (2 * B_W))),
                  BS((HALO_B, 2 * B_W), lambda b, i: (prev(b, i), P_UB // (2 * B_W))),
                  BS((tm, B_W), lambda b, i: (row(b, i), 0)), BS((HALO_B, B_W), lambda b, i: (nxt(b, i), 0)),
                  BS((HALO_B, B_W), lambda b, i: (0, 0)), BS(memory_space=pl.ANY)],
        out_specs=[BS((tm, 2 * B_W), lambda b, i: (row(b, i), P_UB // (2 * B_W))),
                   BS((SUB * B_K, B_W), lambda b, i: (0, 0))],
        out_shape=[S(dp.shape, dp.dtype), S((SUB * B_K, B_W), F32)], input_output_aliases={5: 0},
        scratch_shapes=[pltpu.VMEM((HALO_B + tm, B_W), F32)] * 2 + [pltpu.VMEM((SUB - 1, HALO_B + tm - SUB, B_W), F32)] * 2,
        compiler_params=_cp("arbitrary", "arbitrary"),
    )(p, p, dhc, dhc, cw, dp)


HALO_C = 8
QS = C_DH ** -0.5
NCB = 3 * C_HEADS
CB0 = P_QKV // LANES
ZC0 = P_ZC // LANES
GB, GG = 0, C_HEADS


def _softplus(z):
    return jnp.maximum(z, 0.0) + jnp.log(1.0 + jnp.exp(-jnp.abs(z)))


def _gdn_gates_fwd(p, alog_l, dtb_l, tm=256):
    n = p.shape[0]

    def body(ba_ref, al_ref, db_ref, o_ref):
        blk = ba_ref[...]
        lane = _lane(blk.shape)
        g = jnp.where((lane >= GG) & (lane < GG + C_HEADS), -jnp.exp(al_ref[...]) * _softplus(blk + db_ref[...]), 0.0)
        tri = (_subl((CHUNK, CHUNK)) >= _lane((CHUNK, CHUNK))).astype(F32)
        gc = jnp.concatenate([_dot(tri, g[CHUNK * c:CHUNK * (c + 1)], HI) for c in range(tm // CHUNK)], axis=0)
        o_ref[...] = jnp.where(lane < GG, _sigmoid(blk), gc)

    return pl.pallas_call(
        body, name="gdn_gates_fwd", grid=(n // tm,),
        in_specs=[BS((tm, LANES), lambda i: (i, P_BA // LANES)), BS((1, LANES), lambda i: (0, 0)), BS((1, LANES), lambda i: (0, 0))],
        out_specs=BS((tm, LANES), lambda i: (i, 0)), out_shape=S((n, LANES), F32),
        compiler_params=_cp("arbitrary"),
    )(p, alog_l, dtb_l)


def _gdn_pre_fwd(p, ccw, nseq, tm=256):
    n = p.shape[0]
    t = n // nseq
    nt = t // tm
    row = lambda b, i: b * nt + i
    halo = lambda b, i: jnp.maximum((b * t + i * tm) // HALO_C - 1, 0)

    def body(x_ref, xh_ref, w_ref, xc_ref, o_ref, buf_ref):
        buf_ref[0:HALO_C, :] = jnp.where(pl.program_id(1) > 0, xh_ref[...], 0.0)
        buf_ref[HALO_C:, :] = x_ref[...]
        for c in range(NCB):
            cs = slice(LANES * c, LANES * (c + 1))
            xc = jnp.zeros((tm, LANES), F32)
            for k in range(C_K):
                xc = xc + w_ref[k:k + 1, cs] * buf_ref[pl.ds(HALO_C - C_K + 1 + k, tm), cs]
            xc_ref[:, cs] = xc
            act = xc * _sigmoid(xc)
            if c < 2 * C_HEADS:
                act = act * (lax.rsqrt(jnp.sum(act * act, axis=-1, keepdims=True) + EPS) * (QS if c < C_HEADS else 1.0))
            o_ref[:, cs] = act

    wide = 3 * C_W
    return pl.pallas_call(
        body, name="gdn_pre_fwd", grid=(nseq, nt),
        in_specs=[BS((tm, wide), lambda b, i: (row(b, i), P_QKV // wide)), BS((HALO_C, wide), lambda b, i: (halo(b, i), P_QKV // wide)),
                  BS((SUB, wide), lambda b, i: (0, 0))],
        out_specs=[BS((tm, wide), lambda b, i: (row(b, i), 0))] * 2,
        out_shape=[S((n, wide), F32)] * 2,
        scratch_shapes=[pltpu.VMEM((HALO_C + tm, wide), F32)],
        compiler_params=_cp("arbitrary", "arbitrary"),
    )(p, p, ccw)


def _chunk_common(q, k, gt, gtt, h):
    beta = _col(gt, GB + h)
    gc = _col(gt, GG + h)
    gcr = gtt[GG + h:GG + h + 1, :]
    ii, jj = _subl((CHUNK, CHUNK)), _lane((CHUNK, CHUNK))
    incl, strict = ii >= jj, ii > jj
    dec = jnp.exp(jnp.where(incl, gc - gcr, -jnp.inf))
    kb = k * beta
    kbf = k.astype(BF16)
    a = jnp.where(strict, _dot_nt(kb.astype(BF16), kbf) * dec, 0.0)
    mq = jnp.where(incl, _dot_nt(q.astype(BF16), kbf) * dec, 0.0)
    glast = jnp.sum(jnp.where(_subl(gc.shape) == CHUNK - 1, gc, 0.0), axis=0, keepdims=True)
    return beta, gc, incl, strict, dec, kb, a, mq, glast


def _split(x):
    hi = x.astype(BF16)
    return hi, (x - hi.astype(F32)).astype(BF16)


def _dot3(dot, a, b):
    (ah, al), (bh, bl) = a, b
    return dot(ah, bh) + (dot(ah, bl) + dot(al, bh))


def _unit_lower_inverses(mats):
    eye = (_subl(mats[0].shape) == _lane(mats[0].shape)).astype(F32)
    ms = [-a for a in mats]
    invs = [eye + m for m in ms]
    parts = [_split(m) for m in ms]
    for _ in range(5):
        ms = [_dot3(_dot, s, s) for s in parts]
        parts = [_split(m) for m in ms]
        invs = [inv + _dot3(_dot, _split(inv), s) for inv, s in zip(invs, parts)]
    return invs


def _gdn_chunk_fwd(qkv, gates, p, y, onw, nseq, tt=512):
    n = qkv.shape[0]
    t = n // nseq
    tt = min(tt, t)
    nt = t // tt
    nch = tt // CHUNK

    def body(q_ref, k_ref, v_ref, g_ref, zc_ref, onw_ref, y_in_ref, y_ref, o_ref, u_ref, w_ref, t_ref, ss_ref, s_scr):
        @pl.when(pl.program_id(1) == 0)
        def _():
            s_scr[...] = jnp.zeros_like(s_scr)

        def step(c, carry):
            rows = pl.ds(pl.multiple_of(c * CHUNK, CHUNK), CHUNK)
            gt = g_ref[rows, :]
            gtt = gt.T
            heads = range(C_HEADS)
            hs = [slice(C_DH * h, C_DH * (h + 1)) for h in heads]
            q, k, v = ([r[rows, hs[h]] for h in heads] for r in (q_ref, k_ref, v_ref))
            cm = [_chunk_common(q[h], k[h], gt, gtt, h) for h in heads]
            beta, gc, kb, mq, glast = ([m[i] for m in cm] for i in (0, 1, 5, 7, 8))
            tinv = _unit_lower_inverses([m[6] for m in cm])
            egc = [jnp.exp(g) for g in gc]
            sol = [_dot3(_dot, _split(tinv[h]), _split(jnp.concatenate([v[h] * beta[h], kb[h] * egc[h]], axis=1))) for h in heads]
            sv = [s_scr[h] for h in heads]
            sb = [s.astype(BF16) for s in sv]
            vnb = [(sol[h][:, :C_DH] - _dot(sol[h][:, C_DH:].astype(BF16), sb[h])).astype(BF16) for h in heads]
            o = [_dot((q[h] * egc[h]).astype(BF16), sb[h]) + _dot(mq[h].astype(BF16), vnb[h]) for h in heads]
            for h in heads:
                ss_ref[h, c] = sv[h]
                s_scr[h] = sv[h] * jnp.exp(glast[h]) + _dot_tn((k[h] * jnp.exp(glast[h] - gc[h])).astype(BF16), vnb[h])
            for h in heads:
                o_ref[rows, hs[h]] = o[h]
                u_ref[rows, hs[h]] = sol[h][:, :C_DH]
                w_ref[rows, hs[h]] = sol[h][:, C_DH:]
                t_ref[rows, hs[h]] = jnp.concatenate([tinv[h], jnp.zeros_like(tinv[h])], axis=1)
                zc = zc_ref[rows, hs[h]]
                r = lax.rsqrt(jnp.mean(o[h] * o[h], axis=-1, keepdims=True) + EPS)
                y_ref[rows, hs[h]] = (o[h] * r * onw_ref[...] * (zc * _sigmoid(zc))).astype(BF16)
            return carry

        lax.fori_loop(0, nch, step, 0, unroll=4 if nch % 4 == 0 else 1)

    row = lambda b, i: b * nt + i
    wb = lambda col: BS((tt, C_W), lambda b, i: (row(b, i), col))
    return pl.pallas_call(
        body, name="gdn_chunk_fwd", grid=(nseq, nt),
        in_specs=[wb(0), wb(1), wb(2), BS((tt, LANES), lambda b, i: (row(b, i), 0)), wb(P_ZC // C_W),
                  BS((1, LANES), lambda b, i: (0, 0)), BS(memory_space=pl.ANY)],
        out_specs=[wb(Y_C // C_W), wb(0), wb(0), wb(0), wb(0),
                   BS((None, C_HEADS, nch, C_DH, C_DH), lambda b, i: (b, 0, i, 0, 0))],
        out_shape=[S(y.shape, y.dtype)] + [S((n, C_W), F32)] * 4 + [S((nseq, C_HEADS, t // CHUNK, C_DH, C_DH), F32)],
        input_output_aliases={6: 0},
        scratch_shapes=[pltpu.VMEM((C_HEADS, C_DH, C_DH), F32)],
        compiler_params=_cp("arbitrary", "arbitrary"),
    )(qkv, qkv, qkv, gates, p, onw, y)


def _gdn_chunk_bwd(qkv, gates, p, dy, dp, onw, o, u, w, tinv, ss, nseq, tt=256):
    n = qkv.shape[0]
    t = n // nseq
    tt = min(tt, t)
    nt = t // tt
    nch = tt // CHUNK

    def body(q_ref, k_ref, v_ref, g_ref, zc_ref, onw_ref, o_ref, dy_ref, u_ref, w_ref, t_ref, ss_ref, dp_in_ref,
             dzc_ref, dqkv_ref, dg_ref, donw_ref, ds_scr):
        @pl.when(pl.program_id(1) == 0)
        def _():
            ds_scr[...] = jnp.zeros_like(ds_scr)

        @pl.when((pl.program_id(0) == 0) & (pl.program_id(1) == 0))
        def _():
            donw_ref[...] = jnp.zeros_like(donw_ref)

        def rsum(x):
            return jnp.sum(x, axis=-1, keepdims=True)

        def step(ci, carry):
            c = nch - 1 - ci
            rows = pl.ds(pl.multiple_of(c * CHUNK, CHUNK), CHUNK)
            gt = g_ref[rows, :]
            gtt = gt.T
            live = [head(c, rows, gt, gtt, h) for h in range(C_HEADS)]
            while live:
                live = [g for g in live if next(g, False)]
            return carry

        def head(c, rows, gt, gtt, h):
            hs = slice(C_DH * h, C_DH * (h + 1))
            q, k, v = q_ref[rows, hs], k_ref[rows, hs], v_ref[rows, hs]
            zc, o, dy, u, w = zc_ref[rows, hs], o_ref[rows, hs], dy_ref[rows, hs], u_ref[rows, hs], w_ref[rows, hs]
            tm_ = t_ref[rows, hs][:, 0:CHUNK]
            sv, dsv = ss_ref[h, c], ds_scr[h]
            sb, dsb = sv.astype(BF16), dsv.astype(BF16)
            sg = _sigmoid(zc)
            r = lax.rsqrt(jnp.mean(o * o, axis=-1, keepdims=True) + EPS)
            on = o * r
            ow = onw_ref[...]
            dzc_ref[rows, hs] = (dy * on * ow * _dsilu(zc, sg)).astype(BF16)
            t1 = dy * zc * sg
            donw_ref[...] += _fold8(t1 * on)
            don = t1 * ow
            do = r * (don - on * jnp.mean(don * on, axis=-1, keepdims=True))
            dob = do.astype(BF16)
            yield True
            beta, gc, incl, strict, dec, kb, a, mq, glast = _chunk_common(q, k, gt, gtt, h)
            egc = jnp.exp(gc)
            gl = jnp.exp(glast)
            ekd = jnp.exp(glast - gc)
            wb = w.astype(BF16)
            vnb = (u - _dot(wb, sb)).astype(BF16)
            qg = q * egc
            yield True
            dvn = _dot_tn(mq.astype(BF16), dob) + _dot((k * ekd).astype(BF16), dsb)
            dvnb = dvn.astype(BF16)
            dqg = _dot_nt(dob, sb)
            yield True
            dmq = jnp.where(incl, _dot_nt(dob, vnb), 0.0)
            dkd = _dot_nt(vnb, dsb)
            dgl = jnp.sum(rsum(dsv * sv), axis=0, keepdims=True)
            dw = -_dot_nt(dvnb, sb)
            yield True
            ds_scr[h] = gl * dsv + _dot_tn(qg.astype(BF16), dob) - _dot_tn(wb, dvnb)
            db = _dot3(_dot_tn, _split(tm_), _split(jnp.concatenate([dvn, dw], axis=1)))
            dbv, dbk = db[:, :C_DH], db[:, C_DH:]
            yield True
            da = -jnp.where(strict, _dot3(_dot_nt, _split(dbv), _split(u)) + _dot3(_dot_nt, _split(dbk), _split(w)), 0.0)
            yield True
            e = da * a + dmq * mq
            dgc = rsum(e) - rsum(e.T)
            dgb, dhb, kbf = (da * dec).astype(BF16), (dmq * dec).astype(BF16), k.astype(BF16)
            dkb = _dot(dgb, kbf)
            tk = rsum(dbk * k)
            rk = rsum(dkd * k) * ekd
            dq = _dot(dhb, kbf) + egc * dqg
            dk = _dot_tn(dgb, kb.astype(BF16)) + _dot_tn(dhb, q.astype(BF16)) + beta * (egc * dbk + dkb) + ekd * dkd
            dbeta = rsum(dbv * v) + tk * egc + rsum(dkb * k)
            dgc = dgc + tk * beta * egc + egc * rsum(dqg * q) - rk
            dglast = jnp.sum(rk, axis=0, keepdims=True) + dgl * gl
            dgc = dgc + jnp.where(_subl(dgc.shape) == CHUNK - 1, dglast, 0.0)
            dqkv_ref[0, rows, hs] = dq
            dqkv_ref[1, rows, hs] = dk
            dqkv_ref[2, rows, hs] = beta * dbv
            lane = _lane((CHUNK, LANES))
            dg_ref[h, rows, :] = jnp.where(lane == 0, dbeta, jnp.where(lane == 1, dgc, 0.0))

        lax.fori_loop(0, nch, step, 0)

    row = lambda b, i: b * nt + nt - 1 - i
    wb = lambda col: BS((tt, C_W), lambda b, i: (row(b, i), col))
    return pl.pallas_call(
        body, name="gdn_chunk_bwd", grid=(nseq, nt),
        in_specs=[wb(0), wb(1), wb(2), BS((tt, LANES), lambda b, i: (row(b, i), 0)), wb(P_ZC // C_W),
                  BS((1, LANES), lambda b, i: (0, 0)), wb(0), wb(Y_C // C_W), wb(0), wb(0), wb(0),
                  BS((None, C_HEADS, nch, C_DH, C_DH), lambda b, i: (b, 0, nt - 1 - i, 0, 0)), BS(memory_space=pl.ANY)],
        out_specs=[wb(P_ZC // C_W), BS((3, tt, C_W), lambda b, i: (0, row(b, i), 0)),
                   BS((C_HEADS, tt, LANES), lambda b, i: (0, row(b, i), 0)), BS((SUB, LANES), lambda b, i: (0, 0))],
        out_shape=[S(dp.shape, dp.dtype), S((3, n, C_W), F32), S((C_HEADS, n, LANES), F32), S((SUB, LANES), F32)],
        input_output_aliases={12: 0},
        scratch_shapes=[pltpu.VMEM((C_HEADS, C_DH, C_DH), F32)],
        compiler_params=_cp("arbitrary", "arbitrary"),
    )(qkv, qkv, qkv, gates, p, onw, o, dy, u, w, tinv, ss, dp)


def _gdn_gates_bwd(dgate, p, alog_l, dtb_l, dp, tm=256):
    n = p.shape[0]
    acc = BS((SUB, LANES), lambda i: (0, 0))

    def body(dg_ref, ba_ref, al_ref, db_ref, dp_in_ref, dba_ref, dal_ref, ddb_ref):
        @pl.when(pl.program_id(0) == 0)
        def _():
            dal_ref[...] = jnp.zeros_like(dal_ref)
            ddb_ref[...] = jnp.zeros_like(ddb_ref)

        blk = ba_ref[...]
        lane = _lane(blk.shape)
        dbeta = jnp.zeros_like(blk)
        dgc = jnp.zeros_like(blk)
        for h in range(C_HEADS):
            dbeta = dbeta + jnp.where(lane == GB + h, _col(dg_ref[h], 0), 0.0)
            dgc = dgc + jnp.where(lane == GG + h, _col(dg_ref[h], 1), 0.0)
        tri = (_subl((CHUNK, CHUNK)) <= _lane((CHUNK, CHUNK))).astype(F32)
        dg = jnp.concatenate([_dot(tri, dgc[CHUNK * c:CHUNK * (c + 1)], HI) for c in range(tm // CHUNK)], axis=0)
        beta = _sigmoid(blk)
        z = blk + db_ref[...]
        ea = jnp.exp(al_ref[...])
        isg = (lane >= GG) & (lane < GG + C_HEADS)
        dz = jnp.where(isg, -dg * ea * _sigmoid(z), 0.0)
        dal_ref[...] += _fold8(jnp.where(isg, -dg * ea * _softplus(z), 0.0))
        ddb_ref[...] += _fold8(dz)
        out = jnp.where(lane < GG, dbeta * beta * (1.0 - beta), dz)
        dba_ref[...] = jnp.concatenate([out, jnp.zeros_like(out)], axis=1).astype(BF16)

    return pl.pallas_call(
        body, name="gdn_gates_bwd", grid=(n // tm,),
        in_specs=[BS((C_HEADS, tm, LANES), lambda i: (0, i, 0)), BS((tm, LANES), lambda i: (i, P_BA // LANES)),
                  BS((1, LANES), lambda i: (0, 0)), BS((1, LANES), lambda i: (0, 0)), BS(memory_space=pl.ANY)],
        out_specs=[BS((tm, 2 * LANES), lambda i: (i, P_BA // (2 * LANES))), acc, acc],
        out_shape=[S(dp.shape, dp.dtype), S((SUB, LANES), F32), S((SUB, LANES), F32)],
        input_output_aliases={4: 0},
        compiler_params=_cp("arbitrary"),
    )(dgate, p, alog_l, dtb_l, dp)


def _gdn_pre_bwd(p, dqkv, xc, dp, ccw, nseq, tm=256):
    n = p.shape[0]
    t = n // nseq
    nt = t // tm
    wide = 3 * C_W
    row = lambda b, i: b * nt + i
    prev = lambda b, i: jnp.maximum((b * t + i * tm) // HALO_C - 1, 0)
    nxt = lambda b, i: jnp.minimum((b * t + (i + 1) * tm) // HALO_C, n // HALO_C - 1)

    def d_conv_out(d, xc, part):
        sg = _sigmoid(xc)
        act = xc * sg
        if part < 2:
            cs = QS if part == 0 else 1.0
            rn = lax.rsqrt(jnp.sum(act * act, axis=-1, keepdims=True) + EPS)
            d = cs * rn * d - act * (cs * rn * rn * rn * jnp.sum(d * act, axis=-1, keepdims=True))
        return d * _dsilu(xc, sg)

    def body(x_ref, xh_ref, d_ref, dn_ref, xc_ref, xn_ref, w_ref, dp_in_ref, dx_ref, dw_ref, buf_ref, dbuf_ref):
        i = pl.program_id(1)

        @pl.when((pl.program_id(0) == 0) & (i == 0))
        def _():
            dw_ref[...] = jnp.zeros_like(dw_ref)

        buf_ref[0:HALO_C, :] = jnp.where(i > 0, xh_ref[...], 0.0)
        buf_ref[HALO_C:, :] = x_ref[...]
        for c in range(NCB):
            cs = slice(LANES * c, LANES * (c + 1))
            part, hd = divmod(c, C_HEADS)
            hs = slice(LANES * hd, LANES * (hd + 1))
            d = d_conv_out(d_ref[part, :, hs], xc_ref[:, cs], part)
            dbuf_ref[0:tm, cs] = d
            dbuf_ref[tm:, cs] = jnp.where(i < nt - 1, d_conv_out(dn_ref[part, :, hs], xn_ref[:, cs], part), 0.0)
            dx = jnp.zeros((tm, LANES), F32)
            for k in range(C_K):
                dx = dx + w_ref[k:k + 1, cs] * dbuf_ref[pl.ds(C_K - 1 - k, tm), cs]
                dw_ref[SUB * k:SUB * (k + 1), cs] += _fold8(d * buf_ref[pl.ds(HALO_C - C_K + 1 + k, tm), cs])
            dx_ref[:, cs] = dx.astype(BF16)

    return pl.pallas_call(
        body, name="gdn_pre_bwd", grid=(nseq, nt),
        in_specs=[BS((tm, wide), lambda b, i: (row(b, i), P_QKV // wide)), BS((HALO_C, wide), lambda b, i: (prev(b, i), P_QKV // wide)),
                  BS((3, tm, C_W), lambda b, i: (0, row(b, i), 0)), BS((3, HALO_C, C_W), lambda b, i: (0, nxt(b, i), 0)),
                  BS((tm, wide), lambda b, i: (row(b, i), 0)), BS((HALO_C, wide), lambda b, i: (nxt(b, i), 0)),
                  BS((SUB, wide), lambda b, i: (0, 0)), BS(memory_space=pl.ANY)],
        out_specs=[BS((tm, wide), lambda b, i: (row(b, i), P_QKV // wide)), BS((SUB * C_K, wide), lambda b, i: (0, 0))],
        out_shape=[S(dp.shape, dp.dtype), S((SUB * C_K, wide), F32)], input_output_aliases={7: 0},
        scratch_shapes=[pltpu.VMEM((HALO_C + tm, wide), F32)] * 2,
        compiler_params=_cp("arbitrary", "arbitrary"),
    )(p, p, dqkv, dqkv, xc, xc, ccw, dp)


ANY = BS(memory_space=pl.ANY)


def _my_pos():
    return lax.axis_index("x"), lax.axis_index("y"), lax.axis_index("c")


def _dev_index(dev):
    return 4 * dev[0] + 2 * dev[1] + dev[2]


def _all_gather(shards, after=None):
    nk = len(shards)

    tail = [] if after is None else [after]

    def body(*refs):
        ins, outs = refs[:nk], refs[nk + len(tail):2 * nk + len(tail)]
        send, recv, loc = refs[2 * nk + len(tail):]
        x, y, c = _my_pos()
        me, sib = (x, y, c), (x, y, 1 - c)
        chips = [(1 - x, y), (x, 1 - y), (1 - x, 1 - y)]

        def rows(t, dev):
            r = ins[t].shape[0]
            return outs[t].at[pl.ds(pl.multiple_of(_dev_index(dev) * r, SUB), r), :]

        def copy(t, k, block, to, src=None):
            return pltpu.make_async_remote_copy(
                src_ref=rows(t, block) if src is None else src, dst_ref=rows(t, block),
                send_sem=send.at[t, k], recv_sem=recv.at[t, k], device_id=to, device_id_type=MESH)

        mine = [pltpu.make_async_copy(ins[t], rows(t, me), loc.at[t]) for t in range(nk)]
        for cp in mine:
            cp.start()
        first = []
        for t in range(nk):
            first.append(copy(t, 0, me, sib, src=ins[t]))
            first += [copy(t, 1 + j, me, (*chip, c), src=ins[t]) for j, chip in enumerate(chips)]
        for cp in first:
            cp.start()
        passed = []
        for j, chip in enumerate(chips):
            for t in range(nk):
                copy(t, 1 + j, (*chip, c), me).wait_recv()
                cp = copy(t, 4 + j, (*chip, c), sib)
                cp.start()
                passed.append(cp)
        for t in range(nk):
            copy(t, 0, sib, me).wait_recv()
            for j, chip in enumerate(chips):
                copy(t, 4 + j, (*chip, 1 - c), me).wait_recv()
        for cp in first + passed:
            cp.wait_send()
        for cp in mine:
            cp.wait()

    return pl.pallas_call(
        body, name="all_gather", in_specs=[ANY] * (nk + len(tail)), out_specs=[ANY] * nk,
        out_shape=[S((N_DEV * a.shape[0], a.shape[1]), a.dtype) for a in shards],
        scratch_shapes=[pltpu.SemaphoreType.DMA((nk, 7)), pltpu.SemaphoreType.DMA((nk, 7)), pltpu.SemaphoreType.DMA((nk,))],
    )(*shards, *tail)


SEM = BS(memory_space=pltpu.SEMAPHORE)
HBM = BS(memory_space=pltpu.HBM)
EFFECT = pltpu.SideEffectType.DATAFLOW_SIDE_EFFECTING


def _peers(x, y, c):
    return [((1 - x) if k & 4 else x, (1 - y) if k & 2 else y, (1 - c) if k & 1 else c) for k in range(1, N_DEV)]


def _exchange_copy(kind, src, land, send, recv, t, k, peer, me, arriving):
    frm = peer if arriving else me
    if kind == "gather":
        r = src.shape[0]
        s_ref = src
        d_ref = land.at[pl.ds(pl.multiple_of(_dev_index(frm) * r, SUB), r), :]
    else:
        r = src.shape[0] // N_DEV
        s_ref = src.at[pl.ds(pl.multiple_of(_dev_index(peer) * r, SUB), r), :]
        d_ref = land.at[_dev_index(frm)]
    sem = t * (N_DEV - 1) + k
    return pltpu.make_async_remote_copy(src_ref=s_ref, dst_ref=d_ref, send_sem=send.at[sem], recv_sem=recv.at[sem],
                                        device_id=peer, device_id_type=MESH)


def _own_copy(kind, src, land, own, t, me):
    if kind == "gather":
        r = src.shape[0]
        return pltpu.make_async_copy(src, land.at[pl.ds(pl.multiple_of(_dev_index(me) * r, SUB), r), :], own.at[t])
    r = src.shape[0] // N_DEV
    return pltpu.make_async_copy(src.at[pl.ds(pl.multiple_of(_dev_index(me) * r, SUB), r), :], land.at[_dev_index(me)], own.at[t])


def _exchange_start(kind, srcs, after, name):
    nk = len(srcs)
    if kind == "gather":
        lands = [lax.empty((N_DEV * a.shape[0], a.shape[1]), a.dtype) for a in srcs]
    else:
        lands = [lax.empty((N_DEV, a.shape[0] // N_DEV, a.shape[1]), a.dtype) for a in srcs]

    def body(*refs):
        src, land = refs[:nk], refs[nk:2 * nk]
        send, recv, own = refs[2 * nk + 1], refs[2 * nk + 2], refs[2 * nk + 3]
        token = refs[-1]
        x, y, c = _my_pos()
        me = (x, y, c)
        for t in range(nk):
            _own_copy(kind, src[t], land[t], own, t, me).start()
            for k, peer in enumerate(_peers(x, y, c)):
                _exchange_copy(kind, src[t], land[t], send, recv, t, k, peer, me, False).start()
        token[...] = jnp.zeros_like(token)

    hbm = lambda a: pltpu.HBM(a.shape, a.dtype)
    out = pl.pallas_call(
        body, name=name,
        out_shape=(pltpu.SemaphoreType.DMA((nk * (N_DEV - 1),)), pltpu.SemaphoreType.DMA((nk * (N_DEV - 1),)),
                   pltpu.SemaphoreType.DMA((nk,)), *[hbm(a) for a in srcs], *[hbm(a) for a in lands], S((SUB, LANES), F32)),
        in_specs=[HBM] * (2 * nk) + [ANY],
        out_specs=(SEM, SEM, SEM, *[HBM] * (2 * nk), BS(memory_space=pltpu.VMEM)),
        input_output_aliases={i: 3 + i for i in range(2 * nk)},
        compiler_params=pltpu.CompilerParams(has_side_effects=EFFECT),
    )(*[pltpu.with_memory_space_constraint(a, pltpu.HBM) for a in (*srcs, *lands)], after)
    return dict(kind=kind, nk=nk, send=out[0], recv=out[1], own=out[2], srcs=out[3:3 + nk], lands=out[3 + nk:3 + 2 * nk],
                token=out[-1])


def _exchange_wait(ex, after, name):
    kind, nk = ex["kind"], ex["nk"]

    def body(*refs):
        src, land = refs[:nk], refs[nk:2 * nk]
        send, recv, own = refs[2 * nk], refs[2 * nk + 1], refs[2 * nk + 2]
        x, y, c = _my_pos()
        me = (x, y, c)
        for t in range(nk):
            _own_copy(kind, src[t], land[t], own, t, me).wait()
            for k, peer in enumerate(_peers(x, y, c)):
                _exchange_copy(kind, src[t], land[t], send, recv, t, k, peer, me, False).wait_send()
                _exchange_copy(kind, src[t], land[t], send, recv, t, k, peer, me, True).wait_recv()

    hbm = lambda a: pltpu.HBM(a.shape, a.dtype)
    out = pl.pallas_call(
        body, name=name,
        out_shape=(*[hbm(a) for a in ex["srcs"]], *[hbm(a) for a in ex["lands"]]),
        in_specs=[HBM] * (2 * nk) + [SEM, SEM, SEM, ANY], out_specs=tuple([HBM] * (2 * nk)),
        input_output_aliases={i: i for i in range(2 * nk)},
        compiler_params=pltpu.CompilerParams(has_side_effects=EFFECT),
    )(*ex["srcs"], *ex["lands"], ex["send"], ex["recv"], ex["own"], after)
    return list(out[nk:])


BLOCK_BYTES = 4 << 20


def _row_tile(rows, row_bytes, align):
    best = align
    for tr in range(align, rows + 1, align):
        if rows % tr == 0 and tr * row_bytes <= BLOCK_BYTES:
            best = tr
    return best


def _sum8(a):
    _, r, w = a.shape
    tr = _row_tile(r, N_DEV * w * a.dtype.itemsize, 32 // a.dtype.itemsize)

    def body(a_ref, o_ref):
        acc = a_ref[0].astype(F32)
        for d in range(1, N_DEV):
            acc = acc + a_ref[d].astype(F32)
        o_ref[...] = acc

    return pl.pallas_call(
        body, name="sum8", grid=(r // tr,), in_specs=[BS((N_DEV, tr, w), lambda i: (0, i, 0))],
        out_specs=BS((tr, w), lambda i: (i, 0)), out_shape=S((r, w), F32), compiler_params=_cp("arbitrary"),
    )(a)


def _adamw(w, g, m, v):
    r, c = w.shape
    tr = _row_tile(r, c * 4 * 2, SUB)

    def body(w_ref, g_ref, m_ref, v_ref, d_ref, mo_ref, vo_ref):
        d_ref[...], mo_ref[...], vo_ref[...] = _adam_update(w_ref[...], g_ref[...], m_ref[...], v_ref[...])

    blk = BS((tr, c), lambda i: (i, 0))
    return pl.pallas_call(
        body, name="adamw", grid=(r // tr,), in_specs=[blk] * 4, out_specs=[blk] * 3,
        out_shape=[S((r, c), F32)] * 3, compiler_params=_cp("arbitrary"),
    )(w, g, m, v)


def _sum8_t(a, tc=256):
    _, r, w = a.shape

    def body(a_ref, o_ref):
        acc = a_ref[0].astype(F32)
        for d in range(1, N_DEV):
            acc = acc + a_ref[d].astype(F32)
        o_ref[...] = acc.T

    return pl.pallas_call(
        body, name="sum8_t", grid=(w // tc,), in_specs=[BS((N_DEV, r, tc), lambda j: (0, 0, j))],
        out_specs=BS((tc, r), lambda j: (j, 0)), out_shape=S((w, r), F32), compiler_params=_cp("arbitrary"),
    )(a)


def _rows_view(a):
    nl, r, c = a.shape
    assert nl == 2
    return a.transpose(2, 0, 1).reshape(c, nl, r // LANES, LANES).transpose(0, 2, 1, 3).reshape(-1, LANES)


def _rows_view_back(a, shape):
    nl, r, c = shape
    return a.reshape(c, r // LANES, nl, LANES).transpose(0, 2, 1, 3).reshape(c, nl, r).transpose(1, 2, 0)


def _adamw_rows(w, g, m, v, tr=2048):
    n = w.shape[0]

    def body(w_ref, g_ref, m_ref, v_ref, d_ref, mo_ref, vo_ref):
        d_ref[...], mo_ref[...], vo_ref[...] = _adam_update(w_ref[...], g_ref[...], m_ref[...], v_ref[...])

    blk = BS((tr, LANES), lambda i: (i, 0))
    return pl.pallas_call(
        body, name="adamw_rows", grid=(pl.cdiv(n, tr),), in_specs=[blk] * 4, out_specs=[blk] * 3,
        out_shape=[S((n, LANES), F32)] * 3, compiler_params=_cp("arbitrary"),
    )(w, g, m, v)


def _adam_update(w, g, m, v):
    m2 = ADAM_B1 * m + (1.0 - ADAM_B1) * g
    v2 = ADAM_B2 * v + (1.0 - ADAM_B2) * (g * g)
    m_hat = m2 / (1.0 - ADAM_B1 ** ADAM_STEP)
    v_hat = v2 / (1.0 - ADAM_B2 ** ADAM_STEP)
    return -ADAM_LR * (m_hat / (jnp.sqrt(v_hat) + ADAM_EPS) + ADAM_WD * w), m2, v2


def _adamw_layer(w, g, m, v, l, prev):
    nl, r, c = w.shape
    tr = _row_tile(r, c * 4 * 2, SUB)

    def body(w_ref, g_ref, m_ref, v_ref, *refs):
        go_ref, d_ref, mo_ref, vo_ref = refs[-4:]
        gv = g_ref[...]
        go_ref[...] = gv
        d_ref[...], mo_ref[...], vo_ref[...] = _adam_update(w_ref[...], gv, m_ref[...], v_ref[...])

    slot = BS((None, tr, c), lambda i: (l, i, 0))
    keep = [] if prev is None else [ANY] * 4
    return pl.pallas_call(
        body, name="adamw_layer", grid=(r // tr,), in_specs=[slot, BS((tr, c), lambda i: (i, 0)), slot, slot] + keep,
        out_specs=[slot] * 4, out_shape=[S((nl, r, c), F32)] * 4,
        input_output_aliases={} if prev is None else {4 + i: i for i in range(4)},
        compiler_params=_cp("arbitrary"),
    )(w, g, m, v, *(prev or ()))


def _blob(arrays):
    flat = jnp.concatenate([a.reshape(-1) for a in arrays])
    rows = -(-flat.shape[0] // (SUB * LANES)) * SUB
    return jnp.pad(flat, (0, rows * LANES - flat.shape[0])).reshape(rows, LANES)


def _unblob(blob, shapes, lead=()):
    flat = blob.reshape(lead + (-1,))
    out, off = [], 0
    for s in shapes:
        size = math.prod(s)
        out.append(flat[..., off:off + size].reshape(lead + tuple(s)))
        off += size
    return out


def _y_rows(w):
    return jnp.concatenate([w[0:A_W], w[A_W + B_W:], w[A_W:A_W + B_W]], axis=0)


def _y_rows_back(g):
    return jnp.concatenate([g[0:A_W], g[A_W + C_W:], g[A_W:A_W + C_W]], axis=0)


SMALL = ("norm_w", "q_norm_w", "k_norm_w", "sinks", "b_conv_b", "b_ln_w", "b_ln_b", "b_pw_b", "c_a_log", "c_dt_bias",
         "c_onorm_w", "b_conv_w", "c_conv_w")
ORDER = ("norm_w", "w_in", "q_norm_w", "k_norm_w", "sinks", "b_conv_w", "b_conv_b", "b_ln_w", "b_ln_b", "b_pw_w", "b_pw_b",
         "c_conv_w", "c_a_log", "c_dt_bias", "c_onorm_w", "w_out")


def kernel(x, positions, norm_w, w_in, q_norm_w, k_norm_w, sinks, b_conv_w, b_conv_b, b_ln_w, b_ln_b, b_pw_w, b_pw_b, c_conv_w, c_a_log, c_dt_bias, c_onorm_w, w_out, loss_target, m_norm_w, m_w_in, m_q_norm_w, m_k_norm_w, m_sinks, m_b_conv_w, m_b_conv_b, m_b_ln_w, m_b_ln_b, m_b_pw_w, m_b_pw_b, m_c_conv_w, m_c_a_log, m_c_dt_bias, m_c_onorm_w, m_w_out, v_norm_w, v_w_in, v_q_norm_w, v_k_norm_w, v_sinks, v_b_conv_w, v_b_conv_b, v_b_ln_w, v_b_ln_b, v_b_pw_w, v_b_pw_b, v_c_conv_w, v_c_a_log, v_c_dt_bias, v_c_onorm_w, v_w_out):
    W = dict(norm_w=norm_w, w_in=w_in, q_norm_w=q_norm_w, k_norm_w=k_norm_w, sinks=sinks, b_conv_w=b_conv_w, b_conv_b=b_conv_b,
             b_ln_w=b_ln_w, b_ln_b=b_ln_b, b_pw_w=b_pw_w, b_pw_b=b_pw_b, c_conv_w=c_conv_w, c_a_log=c_a_log,
             c_dt_bias=c_dt_bias, c_onorm_w=c_onorm_w, w_out=w_out)
    M = dict(norm_w=m_norm_w, w_in=m_w_in, q_norm_w=m_q_norm_w, k_norm_w=m_k_norm_w, sinks=m_sinks, b_conv_w=m_b_conv_w,
             b_conv_b=m_b_conv_b, b_ln_w=m_b_ln_w, b_ln_b=m_b_ln_b, b_pw_w=m_b_pw_w, b_pw_b=m_b_pw_b, c_conv_w=m_c_conv_w,
             c_a_log=m_c_a_log, c_dt_bias=m_c_dt_bias, c_onorm_w=m_c_onorm_w, w_out=m_w_out)
    V = dict(norm_w=v_norm_w, w_in=v_w_in, q_norm_w=v_q_norm_w, k_norm_w=v_k_norm_w, sinks=v_sinks, b_conv_w=v_b_conv_w,
             b_conv_b=v_b_conv_b, b_ln_w=v_b_ln_w, b_ln_b=v_b_ln_b, b_pw_w=v_b_pw_w, b_pw_b=v_b_pw_b, c_conv_w=v_c_conv_w,
             c_a_log=v_c_a_log, c_dt_bias=v_c_dt_bias, c_onorm_w=v_c_onorm_w, w_out=v_w_out)
    nseq, t, d = x.shape
    n = nseq * t
    tr = min(256, t)
    tmm = min(512, n)
    tmw = min(1024, n)
    tkk = min(2048, n)
    me = _dev_index(_my_pos())
    xs = [x.reshape(n, d)]
    tgt = loss_target.reshape(n, d)
    tabs = _rope_tables(positions.reshape(n))

    win_p = _pack_cols(w_in).astype(BF16)
    wout_b = w_out.astype(BF16)
    sharded_small = (b_pw_w, b_conv_w, c_conv_w)
    g_win0, g_small = _all_gather([win_p[0], _blob(sharded_small)])
    win = [g_win0]
    later = _exchange_start("gather", [win_p[1], wout_b[0], wout_b[1]], g_small, "gather_start")
    pw_all, cw_all, ccw_all = _unblob(g_small, [a.shape for a in sharded_small], lead=(N_DEV,))
    pw_all = pw_all.transpose(1, 0, 2, 3).reshape(DEPTH, B_W, B_W).astype(BF16)
    cw_all = jnp.pad(cw_all.transpose(1, 2, 0, 3).reshape(DEPTH, B_K, B_W), ((0, 0), (0, HALO_B - B_K), (0, 0)))
    ccw_all = jnp.pad(ccw_all.transpose(1, 2, 0, 3).reshape(DEPTH, C_K, 3 * C_W), ((0, 0), (0, SUB - C_K), (0, 0)))
    qw_all, kw_all = jnp.tile(q_norm_w, (1, 2)), jnp.tile(k_norm_w, (1, 2))
    lanes6 = lambda a: jnp.zeros((DEPTH, LANES), F32).at[:, GG:GG + C_HEADS].set(a)
    alog_all, dtb_all = lanes6(c_a_log), lanes6(c_dt_bias)

    def layer_params(l):
        row = lambda a: a[l][None]
        return dict(
            nw=row(norm_w), qw=row(qw_all), kw=row(kw_all), sinks=sinks[l], cw=cw_all[l], cb=row(b_conv_b), lnw=row(b_ln_w),
            lnb=row(b_ln_b), pw=pw_all[l], pwb=row(b_pw_b), ccw=ccw_all[l], alog=row(alog_all), dtb=row(dtb_all),
            onw=row(c_onorm_w))

    saved = []
    for l in range(DEPTH):
        q = layer_params(l)
        nw = q["nw"] + later["token"][0:1, 0:1] if l == 0 else q["nw"]
        p, h = _inproj(xs[l], nw, win[l], tm=tmw)
        y, o_a, lse = _attn_fwd(p, tabs, q["qw"], q["kw"], q["sinks"], nseq)
        gates = _gdn_gates_fwd(p, q["alog"], q["dtb"], tm=tr)
        xc, qkv = _gdn_pre_fwd(p, q["ccw"], nseq, tm=tr)
        y, o_c, u, w, tinv, ss = _gdn_chunk_fwd(qkv, gates, p, y, q["onw"], nseq)
        y, hc = _conf_fwd(p, y, q["cw"], q["cb"], q["lnw"], q["lnb"], q["pw"], q["pwb"], nseq, tm=tr)
        saved.append(dict(q=q, p=p, h=h, y=y, o_a=o_a, lse=lse, gates=gates, xc=xc, qkv=qkv, o_c=o_c, u=u, w=w, tinv=tinv,
                          ss=ss, hc=hc))
        if l == 0:
            g_win1, g_wout0, g_wout1 = _exchange_wait(later, y, "gather_wait")
            win.append(g_win1)
            wout = [_y_rows(g_wout0), _y_rows(g_wout1)]
        if l + 1 < DEPTH:
            xs.append(_outproj(xs[l], y, wout[l], tm=tmw, tn=512))
        else:
            dxn, lsum = _outproj_loss(xs[l], y, wout[l], tgt, tm=tmw, tn=512)
    loss = lax.psum(jnp.sum(lsum) * (0.5 / d), ("x", "y", "c"))

    sent, smalls = [None] * DEPTH, [None] * DEPTH
    for l in reversed(range(DEPTH)):
        s = saved[l]
        q, p = s["q"], s["p"]
        dy = _matmul(dxn, wout[l], "nt", F32, tmw, 512, d, "outproj_bwd_dy")
        dwout = _y_rows_back(_matmul(s["y"], dxn, "tn", BF16, 1024, 1024, tkk, "outproj_bwd_dw"))
        dp, dkv, dqw, dkw, dsk = _attn_bwd(p, dy, s["o_a"], s["lse"], tabs, q["qw"], q["kw"], q["sinks"], nseq)
        dp, dqkv, dgate, donw = _gdn_chunk_bwd(s["qkv"], s["gates"], p, dy, dp, q["onw"], s["o_c"], s["u"], s["w"],
                                               s["tinv"], s["ss"], nseq)
        dp, dccw = _gdn_pre_bwd(p, dqkv, s["xc"], dp, q["ccw"], nseq, tm=tr)
        early = P_K // 768
        dwin_a = _matmul(s["h"], dp, "tn", BF16, 1024, 768, tkk, "inproj_bwd_dw_a", b_cols=(0, early))
        sent_a = _exchange_start("scatter", [dwin_a, dwout], donw, "scatter_start_a%d" % l)
        dp = _put_cols(dp, dkv, P_K, sent_a["token"], tm=tmm)
        dp, dal, ddb = _gdn_gates_bwd(dgate, p, q["alog"], q["dtb"], dp, tm=tr)
        dp, dhc, dpw, dpwb, dlnw, dlnb, dcb = _conf_bwd1(p, dy, dp, s["hc"], q["lnw"], q["lnb"], q["pw"], q["pwb"], tm=tr)
        dp, dcw = _conf_bwd2(p, dhc, dp, q["cw"], nseq, tm=tr)
        dwin_b = _matmul(s["h"], dp, "tn", BF16, 1024, 768, tkk, "inproj_bwd_dw_b", b_cols=(early, P_W // 768 - early))
        sent_b = _exchange_start("scatter", [dwin_b, dpw], dpwb, "scatter_start_b%d" % l)
        sent[l] = (sent_a, sent_b)
        dxn, dnw = _inproj_bwd_dx(dp, win[l], xs[l], q["nw"] + sent_b["token"][0:1, 0:1], dxn, tm=tmm)
        halves = lambda a: a.sum(0)[:A_DH] + a.sum(0)[A_DH:]
        smalls[l] = dict(
            norm_w=dnw.sum(0), q_norm_w=halves(dqw), k_norm_w=halves(dkw), sinks=dsk.sum(0)[:A_HEADS], b_conv_b=dcb.sum(0),
            b_ln_w=dlnw.sum(0), b_ln_b=dlnb.sum(0), b_pw_b=dpwb.sum(0), c_a_log=dal.sum(0)[GG:GG + C_HEADS],
            c_dt_bias=ddb.sum(0)[GG:GG + C_HEADS], c_onorm_w=donw.sum(0),
            b_conv_w=dcw.reshape(B_K, SUB, B_W).sum(1), c_conv_w=dccw.reshape(C_K, SUB, 3 * C_W).sum(1))
    grad_x = dxn.reshape(nseq, t, d)

    G, delta, new_m, new_v = {}, {}, {}, {}
    big = ("w_in", "w_out", "b_pw_w")
    stacks = {k: None for k in big}
    after = dxn
    g_t = [None] * DEPTH
    for l in reversed(range(DEPTH)):
        r_win_a, r_wout = _exchange_wait(sent[l][0], after, "scatter_wait_a%d" % l)
        r_win_b, r_pw = _exchange_wait(sent[l][1], r_wout, "scatter_wait_b%d" % l)
        g_t[l] = jnp.concatenate([_sum8_t(r_win_a), _sum8_t(r_win_b)], axis=0).reshape(P_W, -1, LANES)
        for k, r in (("w_out", r_wout), ("b_pw_w", r_pw)):
            stacks[k] = _adamw_layer(W[k], _sum8(r), M[k], V[k], l, stacks[k])
        after = stacks["w_out"][1]
    g_in = jnp.stack(g_t, axis=2).reshape(-1, LANES)
    g_in = _unpack_cols(g_in, axis=0, each=g_in.shape[0] // P_W)
    rows = _adamw_rows(_rows_view(w_in), g_in, _rows_view(m_w_in), _rows_view(v_w_in))
    stacks["w_in"] = [_rows_view_back(a, w_in.shape) for a in (g_in, *rows)]
    for k in big:
        G[k], delta[k], new_m[k], new_v[k] = stacks[k]
    part = _blob([jnp.stack([smalls[l][k] for l in range(DEPTH)]) for k in SMALL])
    (tot,) = _all_gather([part], after=rows[0])
    tot = _sum8(tot.reshape(N_DEV, part.shape[0], LANES))
    full_shapes = [(DEPTH,) + smalls[0][k].shape for k in SMALL]
    for k, g in zip(SMALL, _unblob(tot, full_shapes)):
        G[k] = g
    G["b_conv_w"] = lax.dynamic_slice_in_dim(G["b_conv_w"], me * (B_W // N_DEV), B_W // N_DEV, axis=2)
    G["c_conv_w"] = lax.dynamic_slice_in_dim(G["c_conv_w"], me * (3 * C_W // N_DEV), 3 * C_W // N_DEV, axis=2)
    dl, mo, vo = _adamw(*[_blob([src[k] for k in SMALL]) for src in (W, G, M, V)])
    shapes = [W[k].shape for k in SMALL]
    for k, a, b, c in zip(SMALL, _unblob(dl, shapes), _unblob(mo, shapes), _unblob(vo, shapes)):
        delta[k], new_m[k], new_v[k] = a, b, c
    return (loss, grad_x, *[G[k] for k in ORDER], *[delta[k] for k in ORDER], *[new_m[k] for k in ORDER],
            *[new_v[k] for k in ORDER])
```

```python
import math

import jax
import jax.numpy as jnp
from jax import lax
from jax.experimental import pallas as pl
from jax.experimental.pallas import tpu as pltpu

F32 = jnp.float32
BF16 = jnp.bfloat16
HI = lax.Precision.HIGHEST
MESH = pl.DeviceIdType.MESH
S = jax.ShapeDtypeStruct
BS = pl.BlockSpec

N_DEV = 8
DEPTH = 2
D_MODEL = 2048
A_HEADS, A_KV, A_DH, A_W, A_KVW = 12, 4, 64, 768, 256
ROT = 16
THETA = 500000.0
ABLK = 128
B_W, B_K = 512, 31
C_HEADS, C_DH, C_W, C_K, CHUNK = 6, 128, 768, 4, 64
EPS = 1e-6
IN_COLS = 6668
P_Q, P_ZA, P_ZC, P_QKV, P_K, P_V, P_UB, P_ZB, P_BA, P_W = 0, 768, 1536, 2304, 4608, 4864, 5120, 6144, 6656, 6912
Y_A, Y_C, Y_B = 0, 768, 1536
LANES = 128
SUB = 8

ADAM_LR, ADAM_B1, ADAM_B2, ADAM_EPS, ADAM_WD, ADAM_STEP = 0.001, 0.9, 0.999, 1e-08, 0.01, 10


def _cp(*sem, vmem=None):
    kw = {}
    if sem:
        kw["dimension_semantics"] = sem
    if vmem:
        kw["vmem_limit_bytes"] = vmem
    return pltpu.CompilerParams(**kw)


def _pack_cols(w):
    z = jnp.zeros(w.shape[:-1] + (P_W - IN_COLS,), w.dtype)
    return jnp.concatenate([w[..., 0:768], w[..., 1280:2048], w[..., 5900:6668], w[..., 3584:5888],
                            w[..., 768:1024], w[..., 1024:1280], w[..., 2048:3072], w[..., 3072:3584],
                            w[..., 5888:5900], z], axis=-1)


def _unpack_cols(g, axis=-1, each=1):
    parts = ((P_Q, 768), (P_K, 256), (P_V, 256), (P_ZA, 768), (P_UB, 1024), (P_ZB, 512), (P_QKV, 2304), (P_BA, 12), (P_ZC, 768))
    return jnp.concatenate([lax.slice_in_dim(g, each * o, each * (o + n), axis=axis) for o, n in parts], axis=axis)


def _sigmoid(x):
    return 0.5 * jnp.tanh(0.5 * x) + 0.5


def _dsilu(x, sg):
    return sg * (1.0 + x * (1.0 - sg))


def _fold8(x):
    r, c = x.shape
    return x.reshape(r // SUB, SUB, c).sum(axis=0)


def _dot(a, b, prec=None):
    return jnp.dot(a, b, preferred_element_type=F32, precision=prec)


def _dot_nt(a, b, prec=None):
    return lax.dot_general(a, b, (((1,), (1,)), ((), ())), preferred_element_type=F32, precision=prec)


def _dot_tn(a, b, prec=None):
    return lax.dot_general(a, b, (((0,), (0,)), ((), ())), preferred_element_type=F32, precision=prec)


def _lane(shape):
    return lax.broadcasted_iota(jnp.int32, shape, 1)


def _subl(shape):
    return lax.broadcasted_iota(jnp.int32, shape, 0)


def _col(x, j):
    return jnp.sum(jnp.where(_lane(x.shape) == j, x, 0.0), axis=-1, keepdims=True)


def _inproj(x, nw, w, tm=512, tn=1152):
    n, d = x.shape
    pw = w.shape[1]

    def body(x_ref, nw_ref, w_ref, p_ref, h_ref):
        @pl.when(pl.program_id(1) == 0)
        def _():
            xv = x_ref[...]
            r = lax.rsqrt(jnp.mean(xv * xv, axis=-1, keepdims=True) + EPS)
            h_ref[...] = (xv * r * nw_ref[...]).astype(BF16)

        p_ref[...] = _dot(h_ref[...], w_ref[...])

    return pl.pallas_call(
        body, name="inproj", grid=(n // tm, pw // tn),
        in_specs=[BS((tm, d), lambda i, j: (i, 0)), BS((1, d), lambda i, j: (0, 0)), BS((d, tn), lambda i, j: (0, j))],
        out_specs=[BS((tm, tn), lambda i, j: (i, j)), BS((tm, d), lambda i, j: (i, 0))],
        out_shape=[S((n, pw), F32), S((n, d), BF16)],
        compiler_params=_cp("arbitrary", "arbitrary"),
    )(x, nw, w)


def _outproj(x, y, w, tm=512, tn=1024):
    n, d = x.shape
    k = y.shape[1]

    def body(x_ref, y_ref, w_ref, o_ref):
        o_ref[...] = x_ref[...] + _dot(y_ref[...], w_ref[...])

    return pl.pallas_call(
        body, name="outproj", grid=(n // tm, d // tn),
        in_specs=[BS((tm, tn), lambda i, j: (i, j)), BS((tm, k), lambda i, j: (i, 0)), BS((k, tn), lambda i, j: (0, j))],
        out_specs=BS((tm, tn), lambda i, j: (i, j)),
        out_shape=S((n, d), F32),
        compiler_params=_cp("arbitrary", "arbitrary"),
    )(x, y, w)


def _outproj_loss(x, y, w, tgt, tm=512, tn=1024):
    n, d = x.shape
    k = y.shape[1]

    def body(x_ref, y_ref, w_ref, t_ref, g_ref, l_ref):
        @pl.when((pl.program_id(0) == 0) & (pl.program_id(1) == 0))
        def _():
            l_ref[...] = jnp.zeros_like(l_ref)

        diff = x_ref[...] + _dot(y_ref[...], w_ref[...]) - t_ref[...]
        g_ref[...] = diff * (1.0 / d)
        f = _fold8(diff * diff)
        acc = f[:, 0:LANES]
        for c in range(1, tn // LANES):
            acc = acc + f[:, c * LANES:(c + 1) * LANES]
        l_ref[...] += acc

    return pl.pallas_call(
        body, name="outproj_loss", grid=(n // tm, d // tn),
        in_specs=[BS((tm, tn), lambda i, j: (i, j)), BS((tm, k), lambda i, j: (i, 0)), BS((k, tn), lambda i, j: (0, j)),
                  BS((tm, tn), lambda i, j: (i, j))],
        out_specs=[BS((tm, tn), lambda i, j: (i, j)), BS((SUB, LANES), lambda i, j: (0, 0))],
        out_shape=[S((n, d), F32), S((SUB, LANES), F32)],
        compiler_params=_cp("arbitrary", "arbitrary"),
    )(x, y, w, tgt)


def _matmul(a, b, mode, out_dtype, tm, tn, tk, name, b_cols=None):
    if mode == "nn":
        (m, kk), nn = a.shape, b.shape[1]
        a_spec, b_spec = BS((tm, tk), lambda i, j, k: (i, k)), BS((tk, tn), lambda i, j, k: (k, j))
        dot = _dot
    elif mode == "nt":
        (m, kk), nn = a.shape, b.shape[0]
        a_spec, b_spec = BS((tm, tk), lambda i, j, k: (i, k)), BS((tn, tk), lambda i, j, k: (j, k))
        dot = _dot_nt
    else:
        j0, nj = b_cols or (0, b.shape[1] // tn)
        (kk, m), nn = a.shape, nj * tn
        a_spec, b_spec = BS((tk, tm), lambda i, j, k: (k, i)), BS((tk, tn), lambda i, j, k: (k, j0 + j))
        dot = _dot_tn
    nk = kk // tk

    def body(a_ref, b_ref, o_ref, acc_ref):
        kid = pl.program_id(2)

        @pl.when(kid == 0)
        def _():
            acc_ref[...] = jnp.zeros_like(acc_ref)

        acc_ref[...] += dot(a_ref[...].astype(BF16), b_ref[...].astype(BF16))

        @pl.when(kid == nk - 1)
        def _():
            o_ref[...] = acc_ref[...].astype(out_dtype)

    return pl.pallas_call(
        body, name=name, grid=(m // tm, nn // tn, nk),
        in_specs=[a_spec, b_spec], out_specs=BS((tm, tn), lambda i, j, k: (i, j)),
        out_shape=S((m, nn), out_dtype), scratch_shapes=[pltpu.VMEM((tm, tn), F32)],
        compiler_params=_cp("arbitrary", "arbitrary", "arbitrary"),
    )(a, b)


SLAB = 16


def _inproj_bwd_dx(dp, w, x, nw, dres, tm=512, tk=768):
    n, d = x.shape
    nk = dp.shape[1] // tk

    def body(dp_ref, w_ref, x_ref, nw_ref, dr_ref, dx_ref, dnw_ref, acc_ref):
        kid = pl.program_id(1)

        @pl.when((pl.program_id(0) == 0) & (kid == 0))
        def _():
            dnw_ref[...] = jnp.zeros_like(dnw_ref)

        @pl.when(kid == 0)
        def _():
            acc_ref[...] = jnp.zeros_like(acc_ref)

        acc_ref[...] += _dot_nt(dp_ref[...], w_ref[...])

        @pl.when(kid == nk - 1)
        def _():
            def slab(i, carry):
                rows = pl.ds(pl.multiple_of(i * SLAB, SLAB), SLAB)
                dh, xv = acc_ref[rows, :], x_ref[rows, :]
                r = lax.rsqrt(jnp.mean(xv * xv, axis=-1, keepdims=True) + EPS)
                dnw_ref[...] += _fold8(dh * xv * r)
                g = dh * nw_ref[...]
                mm = jnp.mean(g * xv, axis=-1, keepdims=True)
                dx_ref[rows, :] = dr_ref[rows, :] + r * g - xv * (r * r * r * mm)
                return carry

            lax.fori_loop(0, tm // SLAB, slab, 0)

    return pl.pallas_call(
        body, name="inproj_bwd_dx", grid=(n // tm, nk),
        in_specs=[BS((tm, tk), lambda i, k: (i, k)), BS((d, tk), lambda i, k: (0, k)), BS((tm, d), lambda i, k: (i, 0)),
                  BS((1, d), lambda i, k: (0, 0)), BS((tm, d), lambda i, k: (i, 0))],
        out_specs=[BS((tm, d), lambda i, k: (i, 0)), BS((SUB, d), lambda i, k: (0, 0))],
        out_shape=[S((n, d), F32), S((SUB, d), F32)],
        scratch_shapes=[pltpu.VMEM((tm, d), F32)],
        compiler_params=_cp("arbitrary", "arbitrary"),
    )(dp, w, x, nw, dres)


def _rope_tables(pos):
    half = ROT // 2
    inv = THETA ** (-jnp.arange(0, ROT, 2, dtype=F32) / ROT)
    ang = pos.astype(F32)[:, None] * inv
    cos, sin = jnp.cos(ang), jnp.sin(ang)
    n = pos.shape[0]
    one = jnp.ones((n, A_DH - ROT), F32)
    zero = jnp.zeros((n, A_DH - ROT), F32)
    zh = jnp.zeros((n, half), F32)
    c = jnp.concatenate([cos, cos, one], axis=1)
    s1 = jnp.concatenate([-sin, zh, zero], axis=1)
    s2 = jnp.concatenate([zh, sin, zero], axis=1)
    return tuple(jnp.concatenate([t, t], axis=1) for t in (c, s1, s2))


def _half_stat(t):
    lo = _lane(t.shape) < A_DH
    s_lo = jnp.sum(jnp.where(lo, t, 0.0), axis=-1, keepdims=True)
    s_hi = jnp.sum(jnp.where(lo, 0.0, t), axis=-1, keepdims=True)
    return jnp.where(lo, s_lo, s_hi)


def _normrope(x, w, c, s1, s2):
    r = lax.rsqrt(_half_stat(x * x) * (1.0 / A_DH) + EPS)
    xn = x * r * w
    return xn * c + pltpu.roll(xn, LANES - ROT // 2, 1) * s1 + pltpu.roll(xn, ROT // 2, 1) * s2, r


def _normrope_bwd(dy, x, r, w, c, s1, s2):
    dxn = dy * c + pltpu.roll(dy * s1, ROT // 2, 1) + pltpu.roll(dy * s2, LANES - ROT // 2, 1)
    g = dxn * w
    mm = _half_stat(g * x) * (1.0 / A_DH)
    return r * g - x * (r * r * r * mm), dxn * x * r


def _attn_mask(first):
    qi = _subl((ABLK, 2 * ABLK))
    kj = _lane((ABLK, 2 * ABLK))
    dist = qi + ABLK - kj
    return (dist >= 0) & (dist < ABLK) & (jnp.logical_not(first) | (kj >= ABLK))


def _keep_half(x, b):
    lo = _lane(x.shape) < A_DH
    return jnp.where(lo if b == 0 else jnp.logical_not(lo), x, jnp.zeros_like(x))


def _head_operand(x, j):
    a, b = j % 2, (j // 3) % 2
    return _keep_half(x if a == b else pltpu.roll(x, A_DH, 1), b)


def _head_result(x, j):
    a, b = j % 2, (j // 3) % 2
    return _keep_half(x if a == b else pltpu.roll(x, A_DH, 1), a)


def _attn_fwd(p, tabs, qw, kw, sinks, nseq):
    n = p.shape[0]
    nb = n // nseq // ABLK
    cur = lambda b, i: (b * nb + i, 0)
    prv = lambda b, i: (b * nb + jnp.maximum(i - 1, 0), 0)
    colblk = lambda f, w, off: (lambda b, i: (f(b, i)[0], off // w))

    def body(q_ref, za_ref, kc_ref, vc_ref, kp_ref, vp_ref, c_ref, s1_ref, s2_ref, cp_ref, s1p_ref, s2p_ref,
             qw_ref, kw_ref, sink_ref, y_ref, o_ref, lse_ref):
        first = pl.program_id(1) == 0
        tc = (c_ref[...], s1_ref[...], s2_ref[...])
        tp = (cp_ref[...], s1p_ref[...], s2p_ref[...])
        q, kc, kp = q_ref[...], kc_ref[...], kp_ref[...]
        qn = [_normrope(q[:, LANES * b:LANES * (b + 1)], qw_ref[...], *tc)[0] for b in range(A_W // LANES)]
        k2, v2 = [], []
        for b in range(A_KVW // LANES):
            sl = slice(LANES * b, LANES * (b + 1))
            k2.append(jnp.concatenate([_normrope(kp[:, sl], kw_ref[...], *tp)[0],
                                       _normrope(kc[:, sl], kw_ref[...], *tc)[0]], axis=0).astype(BF16))
            v2.append(jnp.concatenate([vp_ref[:, sl], vc_ref[:, sl]], axis=0).astype(BF16))
        valid = _attn_mask(first)
        heads = range(A_HEADS)
        qm = [_head_operand(qn[j // 2], j).astype(BF16) for j in heads]
        s = [jnp.where(valid, _dot_nt(qm[j], k2[j // 6]) * (A_DH ** -0.5), -jnp.inf) for j in heads]
        m = [jnp.maximum(jnp.max(s[j], axis=-1, keepdims=True), sink_ref[j]) for j in heads]
        e = [jnp.exp(s[j] - m[j]) for j in heads]
        den = [jnp.sum(e[j], axis=-1, keepdims=True) + jnp.exp(sink_ref[j] - m[j]) for j in heads]
        outs = [_head_result(_dot((e[j] * (1.0 / den[j])).astype(BF16), v2[j // 6]), j) for j in heads]
        lse = jnp.zeros((ABLK, LANES), F32)
        for j in heads:
            lse = jnp.where(_lane(lse.shape) == j, m[j] + jnp.log(den[j]), lse)
        o = jnp.concatenate([outs[2 * b] + outs[2 * b + 1] for b in range(A_W // LANES)], axis=1)
        za = za_ref[...]
        o_ref[...] = o
        lse_ref[...] = lse
        y_ref[...] = (o * (za * _sigmoid(za))).astype(BF16)

    tab_specs = [BS((ABLK, LANES), cur)] * 3 + [BS((ABLK, LANES), prv)] * 3
    return pl.pallas_call(
        body, name="attn_fwd", grid=(nseq, nb),
        in_specs=[BS((ABLK, A_W), colblk(cur, A_W, P_Q)), BS((ABLK, A_W), colblk(cur, A_W, P_ZA)),
                  BS((ABLK, A_KVW), colblk(cur, A_KVW, P_K)), BS((ABLK, A_KVW), colblk(cur, A_KVW, P_V)),
                  BS((ABLK, A_KVW), colblk(prv, A_KVW, P_K)), BS((ABLK, A_KVW), colblk(prv, A_KVW, P_V))]
        + tab_specs + [BS((1, LANES), lambda b, i: (0, 0))] * 2 + [BS(memory_space=pltpu.SMEM)],
        out_specs=[BS((ABLK, A_W), colblk(cur, A_W, Y_A)), BS((ABLK, A_W), cur), BS((ABLK, LANES), cur)],
        out_shape=[S((n, D_MODEL), BF16), S((n, A_W), F32), S((n, LANES), F32)],
        compiler_params=_cp("arbitrary", "arbitrary"),
    )(p, p, p, p, p, p, *tabs, *tabs, qw, kw, sinks)


def _attn_bwd(p, dy, o, lse, tabs, qw, kw, sinks, nseq):
    n = p.shape[0]
    nb = n // nseq // ABLK
    cur = lambda b, i: (b * nb + jnp.minimum(i, nb - 1), 0)
    prv = lambda b, i: (b * nb + jnp.maximum(i - 1, 0), 0)
    colblk = lambda f, w, off: (lambda b, i: (f(b, i)[0], off // w))

    def body(q_ref, za_ref, kc_ref, vc_ref, kp_ref, vp_ref, dy_ref, o_ref, lse_ref,
             c_ref, s1_ref, s2_ref, cp_ref, s1p_ref, s2p_ref, qw_ref, kw_ref, sink_ref,
             dqza_ref, dkv_ref, dqw_ref, dkw_ref, dsk_ref, tk_ref, tv_ref, ck_ref, cv_ref):
        i = pl.program_id(1)
        first = i == 0
        tc = (c_ref[...], s1_ref[...], s2_ref[...])
        tp = (cp_ref[...], s1p_ref[...], s2p_ref[...])
        nkb = A_KVW // LANES

        @pl.when((pl.program_id(0) == 0) & first)
        def _():
            dqw_ref[...] = jnp.zeros_like(dqw_ref)
            dkw_ref[...] = jnp.zeros_like(dkw_ref)
            dsk_ref[...] = jnp.zeros_like(dsk_ref)

        @pl.when(i < nb)
        def _():
            q, kc, kp = q_ref[...], kc_ref[...], kp_ref[...]
            qn, rq = [], []
            for b in range(A_W // LANES):
                a, r = _normrope(q[:, LANES * b:LANES * (b + 1)], qw_ref[...], *tc)
                qn.append(a)
                rq.append(r)
            k2, v2 = [], []
            for b in range(nkb):
                sl = slice(LANES * b, LANES * (b + 1))
                k2.append(jnp.concatenate([_normrope(kp[:, sl], kw_ref[...], *tp)[0],
                                           _normrope(kc[:, sl], kw_ref[...], *tc)[0]], axis=0).astype(BF16))
                v2.append(jnp.concatenate([vp_ref[:, sl], vc_ref[:, sl]], axis=0).astype(BF16))
            valid = _attn_mask(first)
            za, dy, o, lse = za_ref[...], dy_ref[...], o_ref[...], lse_ref[...]
            sg = _sigmoid(za)
            do = dy * za * sg
            dqza_ref[:, A_W:2 * A_W] = (dy * o * _dsilu(za, sg)).astype(BF16)
            heads = range(A_HEADS)
            blk = lambda x, b: x[:, LANES * b:LANES * (b + 1)]
            qm = [_head_operand(qn[j // 2], j).astype(BF16) for j in heads]
            lj = [_col(lse, j) for j in heads]
            pr = [jnp.exp(jnp.where(valid, _dot_nt(qm[j], k2[j // 6]) * (A_DH ** -0.5), -jnp.inf) - lj[j]) for j in heads]
            dom = [_head_operand(blk(do, j // 2), j).astype(BF16) for j in heads]
            delta = [jnp.sum(_keep_half(blk(do, j // 2) * blk(o, j // 2), j % 2), axis=-1, keepdims=True) for j in heads]
            ds = [(pr[j] * (_dot_nt(dom[j], v2[j // 6]) - delta[j]) * (A_DH ** -0.5)).astype(BF16) for j in heads]
            dqh = [_head_result(_dot(ds[j], k2[j // 6]), j) for j in heads]
            dkh = [_dot_tn(ds[j], qm[j]) for j in heads]
            dvh = [_dot_tn(pr[j].astype(BF16), dom[j]) for j in heads]
            per_blk = A_HEADS // nkb
            dks = [sum(dkh[per_blk * g + 1:per_blk * (g + 1)], dkh[per_blk * g]) for g in range(nkb)]
            dvs = [sum(dvh[per_blk * g + 1:per_blk * (g + 1)], dvh[per_blk * g]) for g in range(nkb)]
            dsk = jnp.zeros((ABLK, LANES), F32)
            for j in heads:
                dsk = dsk + jnp.where(_lane(dsk.shape) == j, -jnp.exp(sink_ref[j] - lj[j]) * delta[j], 0.0)
            dsk_ref[...] += _fold8(dsk)
            dqn = jnp.concatenate([dqh[2 * b] + dqh[2 * b + 1] for b in range(A_W // LANES)], axis=1)
            dqw = jnp.zeros((SUB, LANES), F32)
            dqo = []
            for b in range(A_W // LANES):
                sl = slice(LANES * b, LANES * (b + 1))
                dx, wt = _normrope_bwd(dqn[:, sl], q[:, sl], rq[b], qw_ref[...], *tc)
                dqo.append(dx)
                dqw = dqw + _fold8(wt)
            dqw_ref[...] += dqw
            dqza_ref[:, 0:A_W] = jnp.concatenate(dqo, axis=1).astype(BF16)
            tk_ref[...] = jnp.concatenate(dks, axis=1)
            tv_ref[...] = jnp.concatenate(dvs, axis=1)

        @pl.when(i == nb)
        def _():
            tk_ref[...] = jnp.zeros_like(tk_ref)
            tv_ref[...] = jnp.zeros_like(tv_ref)

        @pl.when(i > 0)
        def _():
            kp = kp_ref[...]
            dkn = ck_ref[...] + tk_ref[0:ABLK, :]
            dkw = jnp.zeros((SUB, LANES), F32)
            dko = []
            for b in range(nkb):
                sl = slice(LANES * b, LANES * (b + 1))
                r = _normrope(kp[:, sl], kw_ref[...], *tp)[1]
                dx, wt = _normrope_bwd(dkn[:, sl], kp[:, sl], r, kw_ref[...], *tp)
                dko.append(dx)
                dkw = dkw + _fold8(wt)
            dkw_ref[...] += dkw
            dkv_ref[:, 0:A_KVW] = jnp.concatenate(dko, axis=1).astype(BF16)
            dkv_ref[:, A_KVW:2 * A_KVW] = (cv_ref[...] + tv_ref[0:ABLK, :]).astype(BF16)

        ck_ref[...] = tk_ref[ABLK:2 * ABLK, :]
        cv_ref[...] = tv_ref[ABLK:2 * ABLK, :]

    tab_specs = [BS((ABLK, LANES), cur)] * 3 + [BS((ABLK, LANES), prv)] * 3
    acc = BS((SUB, LANES), lambda b, i: (0, 0))
    return pl.pallas_call(
        body, name="attn_bwd", grid=(nseq, nb + 1),
        in_specs=[BS((ABLK, A_W), colblk(cur, A_W, P_Q)), BS((ABLK, A_W), colblk(cur, A_W, P_ZA)),
                  BS((ABLK, A_KVW), colblk(cur, A_KVW, P_K)), BS((ABLK, A_KVW), colblk(cur, A_KVW, P_V)),
                  BS((ABLK, A_KVW), colblk(prv, A_KVW, P_K)), BS((ABLK, A_KVW), colblk(prv, A_KVW, P_V)),
                  BS((ABLK, A_W), colblk(cur, A_W, Y_A)), BS((ABLK, A_W), cur), BS((ABLK, LANES), cur)]
        + tab_specs + [BS((1, LANES), lambda b, i: (0, 0))] * 2 + [BS(memory_space=pltpu.SMEM)],
        out_specs=[BS((ABLK, 2 * A_W), cur), BS((ABLK, 2 * A_KVW), prv), acc, acc, acc],
        out_shape=[S((n, P_W), BF16), S((n, 2 * A_KVW), BF16)] + [S((SUB, LANES), F32)] * 3,
        scratch_shapes=[pltpu.VMEM((2 * ABLK, A_KVW), F32)] * 2 + [pltpu.VMEM((ABLK, A_KVW), F32)] * 2,
        compiler_params=_cp("arbitrary", "arbitrary"),
    )(p, p, p, p, p, p, dy, o, lse, *tabs, *tabs, qw, kw, sinks)


def _put_cols(dst, src, col_off, after, tm=512):
    n, w = src.shape

    def body(s_ref, d_in_ref, after_ref, d_ref):
        d_ref[...] = s_ref[...]

    return pl.pallas_call(
        body, name="put_cols", grid=(n // tm,),
        in_specs=[BS((tm, w), lambda i: (i, 0)), BS(memory_space=pl.ANY), BS(memory_space=pl.ANY)],
        out_specs=BS((tm, w), lambda i: (i, col_off // w)),
        out_shape=S(dst.shape, dst.dtype), input_output_aliases={1: 0},
        compiler_params=_cp("arbitrary"),
    )(src, dst, after)


HALO_B = 32


def _layernorm(hc, lnw, lnb):
    mu = jnp.mean(hc, axis=-1, keepdims=True)
    xc = hc - mu
    rstd = lax.rsqrt(jnp.mean(xc * xc, axis=-1, keepdims=True) + EPS)
    xhat = xc * rstd
    return xhat, rstd, xhat * lnw + lnb


def _shifted_copies(buf_ref, sh_ref):
    rows = sh_ref.shape[1]
    for b in range(1, SUB):
        sh_ref[b - 1] = buf_ref[pl.ds(b, rows), :]


def _rows_from(buf_ref, sh_ref, off, rows, cols=slice(None)):
    a, b = divmod(off, SUB)
    if b == 0:
        return buf_ref[pl.ds(SUB * a, rows), cols]
    return sh_ref[b - 1, pl.ds(SUB * a, rows), cols]


def _conf_fwd(p, y, cw, cb, lnw, lnb, pw, pwb, nseq, tm=256):
    n = p.shape[0]
    t = n // nseq
    nt = t // tm
    row = lambda b, i: b * nt + i
    halo = lambda b, i: jnp.maximum((b * t + i * tm) // HALO_B - 1, 0)
    vec = BS((1, B_W), lambda b, i: (0, 0))

    def body(ub_ref, uh_ref, zb_ref, cw_ref, cb_ref, lnw_ref, lnb_ref, pw_ref, pwb_ref, y_in_ref, y_ref, hc_ref, buf_ref, sh_ref):
        ub, uh = ub_ref[...], uh_ref[...]
        hh = uh[:, :B_W] * _sigmoid(uh[:, B_W:])
        buf_ref[0:HALO_B, :] = jnp.where(pl.program_id(1) > 0, hh, 0.0)
        buf_ref[HALO_B:, :] = ub[:, :B_W] * _sigmoid(ub[:, B_W:])
        _shifted_copies(buf_ref, sh_ref)
        hc = jnp.zeros((tm, B_W), F32) + cb_ref[...]
        for k in range(B_K):
            hc = hc + cw_ref[k:k + 1, :] * _rows_from(buf_ref, sh_ref, HALO_B - B_K + 1 + k, tm)
        hc_ref[...] = hc
        ln = _layernorm(hc, lnw_ref[...], lnb_ref[...])[2]
        sw = ln * _sigmoid(ln)
        ob = _dot(sw.astype(BF16), pw_ref[...]) + pwb_ref[...]
        zb = zb_ref[...]
        y_ref[...] = (ob * (zb * _sigmoid(zb))).astype(BF16)

    return pl.pallas_call(
        body, name="conf_fwd", grid=(nseq, nt),
        in_specs=[BS((tm, 2 * B_W), lambda b, i: (row(b, i), P_UB // (2 * B_W))),
                  BS((HALO_B, 2 * B_W), lambda b, i: (halo(b, i), P_UB // (2 * B_W))),
                  BS((tm, B_W), lambda b, i: (row(b, i), P_ZB // B_W)),
                  BS((HALO_B, B_W), lambda b, i: (0, 0)), vec, vec, vec, BS((B_W, B_W), lambda b, i: (0, 0)), vec,
                  BS(memory_space=pl.ANY)],
        out_specs=[BS((tm, B_W), lambda b, i: (row(b, i), Y_B // B_W)), BS((tm, B_W), lambda b, i: (row(b, i), 0))],
        out_shape=[S(y.shape, y.dtype), S((n, B_W), F32)], input_output_aliases={9: 0},
        scratch_shapes=[pltpu.VMEM((HALO_B + tm, B_W), F32), pltpu.VMEM((SUB - 1, HALO_B + tm - SUB, B_W), F32)],
        compiler_params=_cp("arbitrary", "arbitrary"),
    )(p, p, p, cw, cb, lnw, lnb, pw, pwb, y)


def _conf_bwd1(p, dy, dp, hc, lnw, lnb, pw, pwb, tm=256):
    n = p.shape[0]
    vec = BS((1, B_W), lambda i: (0, 0))
    acc = BS((SUB, B_W), lambda i: (0, 0))

    def body(dy_ref, zb_ref, hc_ref, lnw_ref, lnb_ref, pw_ref, pwb_ref, dp_in_ref,
             dzb_ref, dhc_ref, dpw_ref, dpwb_ref, dlnw_ref, dlnb_ref, dcb_ref):
        @pl.when(pl.program_id(0) == 0)
        def _():
            for r in (dpw_ref, dpwb_ref, dlnw_ref, dlnb_ref, dcb_ref):
                r[...] = jnp.zeros_like(r)

        xhat, rstd, ln = _layernorm(hc_ref[...], lnw_ref[...], lnb_ref[...])
        sgl = _sigmoid(ln)
        sw = (ln * sgl).astype(BF16)
        ob = _dot(sw, pw_ref[...]) + pwb_ref[...]
        dy, zb = dy_ref[...], zb_ref[...]
        sgz = _sigmoid(zb)
        dzb_ref[...] = (dy * ob * _dsilu(zb, sgz)).astype(BF16)
        dob = dy * zb * sgz
        dobb = dob.astype(BF16)
        dpwb_ref[...] += _fold8(dob)
        dpw_ref[...] += _dot_tn(sw, dobb)
        dln = _dot_nt(dobb, pw_ref[...]) * _dsilu(ln, sgl)
        dlnw_ref[...] += _fold8(dln * xhat)
        dlnb_ref[...] += _fold8(dln)
        dxh = dln * lnw_ref[...]
        dhc = rstd * (dxh - jnp.mean(dxh, axis=-1, keepdims=True) - xhat * jnp.mean(dxh * xhat, axis=-1, keepdims=True))
        dcb_ref[...] += _fold8(dhc)
        dhc_ref[...] = dhc

    return pl.pallas_call(
        body, name="conf_bwd1", grid=(n // tm,),
        in_specs=[BS((tm, B_W), lambda i: (i, Y_B // B_W)), BS((tm, B_W), lambda i: (i, P_ZB // B_W)),
                  BS((tm, B_W), lambda i: (i, 0)), vec, vec, BS((B_W, B_W), lambda i: (0, 0)), vec,
                  BS(memory_space=pl.ANY)],
        out_specs=[BS((tm, B_W), lambda i: (i, P_ZB // B_W)), BS((tm, B_W), lambda i: (i, 0)),
                   BS((B_W, B_W), lambda i: (0, 0)), acc, acc, acc, acc],
        out_shape=[S(dp.shape, dp.dtype), S((n, B_W), F32), S((B_W, B_W), F32)] + [S((SUB, B_W), F32)] * 4,
        input_output_aliases={7: 0},
        compiler_params=_cp("arbitrary"),
    )(dy, p, hc, lnw, lnb, pw, pwb, dp)


def _conf_bwd2(p, dhc, dp, cw, nseq, tm=256):
    n = p.shape[0]
    t = n // nseq
    nt = t // tm
    row = lambda b, i: b * nt + i
    prev = lambda b, i: jnp.maximum((b * t + i * tm) // HALO_B - 1, 0)
    nxt = lambda b, i: jnp.minimum((b * t + (i + 1) * tm) // HALO_B, n // HALO_B - 1)

    def body(ub_ref, uh_ref, dh_ref, dn_ref, cw_ref, dp_in_ref, dub_ref, dcw_ref, buf_ref, dbuf_ref, sh_ref, dsh_ref):
        i = pl.program_id(1)

        @pl.when((pl.program_id(0) == 0) & (i == 0))
        def _():
            dcw_ref[...] = jnp.zeros_like(dcw_ref)

        uh = uh_ref[...]
        buf_ref[0:HALO_B, :] = jnp.where(i > 0, uh[:, :B_W] * _sigmoid(uh[:, B_W:]), 0.0)
        buf_ref[HALO_B:, :] = ub_ref[:, :B_W] * _sigmoid(ub_ref[:, B_W:])
        dbuf_ref[0:tm, :] = dh_ref[...]
        dbuf_ref[tm:, :] = jnp.where(i < nt - 1, dn_ref[...], 0.0)
        _shifted_copies(buf_ref, sh_ref)
        _shifted_copies(dbuf_ref, dsh_ref)
        for c in range(B_W // LANES):
            cs, gs = slice(LANES * c, LANES * (c + 1)), slice(B_W + LANES * c, B_W + LANES * (c + 1))
            for r0 in range(0, tm, LANES):
                dhc = dh_ref[r0:r0 + LANES, cs]
                dhg = jnp.zeros((LANES, LANES), F32)
                for k in range(B_K):
                    dhg = dhg + cw_ref[k:k + 1, cs] * _rows_from(dbuf_ref, dsh_ref, r0 + B_K - 1 - k, LANES, cs)
                    dcw_ref[SUB * k:SUB * (k + 1), cs] += _fold8(
                        dhc * _rows_from(buf_ref, sh_ref, r0 + HALO_B - B_K + 1 + k, LANES, cs))
                a, sg = ub_ref[r0:r0 + LANES, cs], _sigmoid(ub_ref[r0:r0 + LANES, gs])
                dub_ref[r0:r0 + LANES, cs] = (dhg * sg).astype(BF16)
                dub_ref[r0:r0 + LANES, gs] = (dhg * a * sg * (1.0 - sg)).astype(BF16)

    return pl.pallas_call(
        body, name="conf_bwd2", grid=(nseq, nt),
        in_specs=[BS((tm, 2 * B_W), lambda b, i: (row(b, i), P_UB // (2 * B_W))),
                  BS((HALO_B, 2 * B_W), lambda b, i: (prev(b, i), P_UB // (2 * B_W))),
                  BS((tm, B_W), lambda b, i: (row(b, i), 0)), BS((HALO_B, B_W), lambda b, i: (nxt(b, i), 0)),
                  BS((HALO_B, B_W), lambda b, i: (0, 0)), BS(memory_space=pl.ANY)],
        out_specs=[BS((tm, 2 * B_W), lambda b, i: (row(b, i), P_UB // (2 * B_W))),
                   BS((SUB * B_K, B_W), lambda b, i: (0, 0))],
        out_shape=[S(dp.shape, dp.dtype), S((SUB * B_K, B_W), F32)], input_output_aliases={5: 0},
        scratch_shapes=[pltpu.VMEM((HALO_B + tm, B_W), F32)] * 2 + [pltpu.VMEM((SUB - 1, HALO_B + tm - SUB, B_W), F32)] * 2,
        compiler_params=_cp("arbitrary", "arbitrary"),
    )(p, p, dhc, dhc, cw, dp)


HALO_C = 8
QS = C_DH ** -0.5
NCB = 3 * C_HEADS
CB0 = P_QKV // LANES
ZC0 = P_ZC // LANES
GB, GG = 0, C_HEADS


def _softplus(z):
    return jnp.maximum(z, 0.0) + jnp.log(1.0 + jnp.exp(-jnp.abs(z)))


def _gdn_gates_fwd(p, alog_l, dtb_l, tm=256):
    n = p.shape[0]

    def body(ba_ref, al_ref, db_ref, o_ref):
        blk = ba_ref[...]
        lane = _lane(blk.shape)
        g = jnp.where((lane >= GG) & (lane < GG + C_HEADS), -jnp.exp(al_ref[...]) * _softplus(blk + db_ref[...]), 0.0)
        tri = (_subl((CHUNK, CHUNK)) >= _lane((CHUNK, CHUNK))).astype(F32)
        gc = jnp.concatenate([_dot(tri, g[CHUNK * c:CHUNK * (c + 1)], HI) for c in range(tm // CHUNK)], axis=0)
        o_ref[...] = jnp.where(lane < GG, _sigmoid(blk), gc)

    return pl.pallas_call(
        body, name="gdn_gates_fwd", grid=(n // tm,),
        in_specs=[BS((tm, LANES), lambda i: (i, P_BA // LANES)), BS((1, LANES), lambda i: (0, 0)), BS((1, LANES), lambda i: (0, 0))],
        out_specs=BS((tm, LANES), lambda i: (i, 0)), out_shape=S((n, LANES), F32),
        compiler_params=_cp("arbitrary"),
    )(p, alog_l, dtb_l)


def _gdn_pre_fwd(p, ccw, nseq, tm=256):
    n = p.shape[0]
    t = n // nseq
    nt = t // tm
    row = lambda b, i: b * nt + i
    halo = lambda b, i: jnp.maximum((b * t + i * tm) // HALO_C - 1, 0)

    def body(x_ref, xh_ref, w_ref, xc_ref, o_ref, buf_ref):
        buf_ref[0:HALO_C, :] = jnp.where(pl.program_id(1) > 0, xh_ref[...], 0.0)
        buf_ref[HALO_C:, :] = x_ref[...]
        for c in range(NCB):
            cs = slice(LANES * c, LANES * (c + 1))
            xc = jnp.zeros((tm, LANES), F32)
            for k in range(C_K):
                xc = xc + w_ref[k:k + 1, cs] * buf_ref[pl.ds(HALO_C - C_K + 1 + k, tm), cs]
            xc_ref[:, cs] = xc
            act = xc * _sigmoid(xc)
            if c < 2 * C_HEADS:
                act = act * (lax.rsqrt(jnp.sum(act * act, axis=-1, keepdims=True) + EPS) * (QS if c < C_HEADS else 1.0))
            o_ref[:, cs] = act

    wide = 3 * C_W
    return pl.pallas_call(
        body, name="gdn_pre_fwd", grid=(nseq, nt),
        in_specs=[BS((tm, wide), lambda b, i: (row(b, i), P_QKV // wide)), BS((HALO_C, wide), lambda b, i: (halo(b, i), P_QKV // wide)),
                  BS((SUB, wide), lambda b, i: (0, 0))],
        out_specs=[BS((tm, wide), lambda b, i: (row(b, i), 0))] * 2,
        out_shape=[S((n, wide), F32)] * 2,
        scratch_shapes=[pltpu.VMEM((HALO_C + tm, wide), F32)],
        compiler_params=_cp("arbitrary", "arbitrary"),
    )(p, p, ccw)


def _chunk_common(q, k, gt, gtt, h):
    beta = _col(gt, GB + h)
    gc = _col(gt, GG + h)
    gcr = gtt[GG + h:GG + h + 1, :]
    ii, jj = _subl((CHUNK, CHUNK)), _lane((CHUNK, CHUNK))
    incl, strict = ii >= jj, ii > jj
    dec = jnp.exp(jnp.where(incl, gc - gcr, -jnp.inf))
    kb = k * beta
    kbf = k.astype(BF16)
    a = jnp.where(strict, _dot_nt(kb.astype(BF16), kbf) * dec, 0.0)
    mq = jnp.where(incl, _dot_nt(q.astype(BF16), kbf) * dec, 0.0)
    glast = jnp.sum(jnp.where(_subl(gc.shape) == CHUNK - 1, gc, 0.0), axis=0, keepdims=True)
    return beta, gc, incl, strict, dec, kb, a, mq, glast


def _split(x):
    hi = x.astype(BF16)
    return hi, (x - hi.astype(F32)).astype(BF16)


def _dot3(dot, a, b):
    (ah, al), (bh, bl) = a, b
    return dot(ah, bh) + (dot(ah, bl) + dot(al, bh))


def _unit_lower_inverses(mats):
    eye = (_subl(mats[0].shape) == _lane(mats[0].shape)).astype(F32)
    ms = [-a for a in mats]
    invs = [eye + m for m in ms]
    parts = [_split(m) for m in ms]
    for _ in range(5):
        ms = [_dot3(_dot, s, s) for s in parts]
        parts = [_split(m) for m in ms]
        invs = [inv + _dot3(_dot, _split(inv), s) for inv, s in zip(invs, parts)]
    return invs


def _gdn_chunk_fwd(qkv, gates, p, y, onw, nseq, tt=512):
    n = qkv.shape[0]
    t = n // nseq
    tt = min(tt, t)
    nt = t // tt
    nch = tt // CHUNK

    def body(q_ref, k_ref, v_ref, g_ref, zc_ref, onw_ref, y_in_ref, y_ref, o_ref, u_ref, w_ref, t_ref, ss_ref, s_scr):
        @pl.when(pl.program_id(1) == 0)
        def _():
            s_scr[...] = jnp.zeros_like(s_scr)

        def step(c, carry):
            rows = pl.ds(pl.multiple_of(c * CHUNK, CHUNK), CHUNK)
            gt = g_ref[rows, :]
            gtt = gt.T
            heads = range(C_HEADS)
            hs = [slice(C_DH * h, C_DH * (h + 1)) for h in heads]
            q, k, v = ([r[rows, hs[h]] for h in heads] for r in (q_ref, k_ref, v_ref))
            cm = [_chunk_common(q[h], k[h], gt, gtt, h) for h in heads]
            beta, gc, kb, mq, glast = ([m[i] for m in cm] for i in (0, 1, 5, 7, 8))
            tinv = _unit_lower_inverses([m[6] for m in cm])
            egc = [jnp.exp(g) for g in gc]
            sol = [_dot3(_dot, _split(tinv[h]), _split(jnp.concatenate([v[h] * beta[h], kb[h] * egc[h]], axis=1))) for h in heads]
            sv = [s_scr[h] for h in heads]
            sb = [s.astype(BF16) for s in sv]
            vnb = [(sol[h][:, :C_DH] - _dot(sol[h][:, C_DH:].astype(BF16), sb[h])).astype(BF16) for h in heads]
            o = [_dot((q[h] * egc[h]).astype(BF16), sb[h]) + _dot(mq[h].astype(BF16), vnb[h]) for h in heads]
            for h in heads:
                ss_ref[h, c] = sv[h]
                s_scr[h] = sv[h] * jnp.exp(glast[h]) + _dot_tn((k[h] * jnp.exp(glast[h] - gc[h])).astype(BF16), vnb[h])
            for h in heads:
                o_ref[rows, hs[h]] = o[h]
                u_ref[rows, hs[h]] = sol[h][:, :C_DH]
                w_ref[rows, hs[h]] = sol[h][:, C_DH:]
                t_ref[rows, hs[h]] = jnp.concatenate([tinv[h], jnp.zeros_like(tinv[h])], axis=1)
                zc = zc_ref[rows, hs[h]]
                r = lax.rsqrt(jnp.mean(o[h] * o[h], axis=-1, keepdims=True) + EPS)
                y_ref[rows, hs[h]] = (o[h] * r * onw_ref[...] * (zc * _sigmoid(zc))).astype(BF16)
            return carry

        lax.fori_loop(0, nch, step, 0, unroll=4 if nch % 4 == 0 else 1)

    row = lambda b, i: b * nt + i
    wb = lambda col: BS((tt, C_W), lambda b, i: (row(b, i), col))
    return pl.pallas_call(
        body, name="gdn_chunk_fwd", grid=(nseq, nt),
        in_specs=[wb(0), wb(1), wb(2), BS((tt, LANES), lambda b, i: (row(b, i), 0)), wb(P_ZC // C_W),
                  BS((1, LANES), lambda b, i: (0, 0)), BS(memory_space=pl.ANY)],
        out_specs=[wb(Y_C // C_W), wb(0), wb(0), wb(0), wb(0),
                   BS((None, C_HEADS, nch, C_DH, C_DH), lambda b, i: (b, 0, i, 0, 0))],
        out_shape=[S(y.shape, y.dtype)] + [S((n, C_W), F32)] * 4 + [S((nseq, C_HEADS, t // CHUNK, C_DH, C_DH), F32)],
        input_output_aliases={6: 0},
        scratch_shapes=[pltpu.VMEM((C_HEADS, C_DH, C_DH), F32)],
        compiler_params=_cp("arbitrary", "arbitrary"),
    )(qkv, qkv, qkv, gates, p, onw, y)


def _gdn_chunk_bwd(qkv, gates, p, dy, dp, onw, o, u, w, tinv, ss, nseq, tt=256):
    n = qkv.shape[0]
    t = n // nseq
    tt = min(tt, t)
    nt = t // tt
    nch = tt // CHUNK

    def body(q_ref, k_ref, v_ref, g_ref, zc_ref, onw_ref, o_ref, dy_ref, u_ref, w_ref, t_ref, ss_ref, dp_in_ref,
             dzc_ref, dqkv_ref, dg_ref, donw_ref, ds_scr):
        @pl.when(pl.program_id(1) == 0)
        def _():
            ds_scr[...] = jnp.zeros_like(ds_scr)

        @pl.when((pl.program_id(0) == 0) & (pl.program_id(1) == 0))
        def _():
            donw_ref[...] = jnp.zeros_like(donw_ref)

        def rsum(x):
            return jnp.sum(x, axis=-1, keepdims=True)

        def step(ci, carry):
            c = nch - 1 - ci
            rows = pl.ds(pl.multiple_of(c * CHUNK, CHUNK), CHUNK)
            gt = g_ref[rows, :]
            gtt = gt.T
            live = [head(c, rows, gt, gtt, h) for h in range(C_HEADS)]
            while live:
                live = [g for g in live if next(g, False)]
            return carry

        def head(c, rows, gt, gtt, h):
            hs = slice(C_DH * h, C_DH * (h + 1))
            q, k, v = q_ref[rows, hs], k_ref[rows, hs], v_ref[rows, hs]
            zc, o, dy, u, w = zc_ref[rows, hs], o_ref[rows, hs], dy_ref[rows, hs], u_ref[rows, hs], w_ref[rows, hs]
            tm_ = t_ref[rows, hs][:, 0:CHUNK]
            sv, dsv = ss_ref[h, c], ds_scr[h]
            sb, dsb = sv.astype(BF16), dsv.astype(BF16)
            sg = _sigmoid(zc)
            r = lax.rsqrt(jnp.mean(o * o, axis=-1, keepdims=True) + EPS)
            on = o * r
            ow = onw_ref[...]
            dzc_ref[rows, hs] = (dy * on * ow * _dsilu(zc, sg)).astype(BF16)
            t1 = dy * zc * sg
            donw_ref[...] += _fold8(t1 * on)
            don = t1 * ow
            do = r * (don - on * jnp.mean(don * on, axis=-1, keepdims=True))
            dob = do.astype(BF16)
            yield True
            beta, gc, incl, strict, dec, kb, a, mq, glast = _chunk_common(q, k, gt, gtt, h)
            egc = jnp.exp(gc)
            gl = jnp.exp(glast)
            ekd = jnp.exp(glast - gc)
            wb = w.astype(BF16)
            vnb = (u - _dot(wb, sb)).astype(BF16)
            qg = q * egc
            yield True
            dvn = _dot_tn(mq.astype(BF16), dob) + _dot((k * ekd).astype(BF16), dsb)
            dvnb = dvn.astype(BF16)
            dqg = _dot_nt(dob, sb)
            yield True
            dmq = jnp.where(incl, _dot_nt(dob, vnb), 0.0)
            dkd = _dot_nt(vnb, dsb)
            dgl = jnp.sum(rsum(dsv * sv), axis=0, keepdims=True)
            dw = -_dot_nt(dvnb, sb)
            yield True
            ds_scr[h] = gl * dsv + _dot_tn(qg.astype(BF16), dob) - _dot_tn(wb, dvnb)
            db = _dot3(_dot_tn, _split(tm_), _split(jnp.concatenate([dvn, dw], axis=1)))
            dbv, dbk = db[:, :C_DH], db[:, C_DH:]
            yield True
            da = -jnp.where(strict, _dot3(_dot_nt, _split(dbv), _split(u)) + _dot3(_dot_nt, _split(dbk), _split(w)), 0.0)
            yield True
            e = da * a + dmq * mq
            dgc = rsum(e) - rsum(e.T)
            dgb, dhb, kbf = (da * dec).astype(BF16), (dmq * dec).astype(BF16), k.astype(BF16)
            dkb = _dot(dgb, kbf)
            tk = rsum(dbk * k)
            rk = rsum(dkd * k) * ekd
            dq = _dot(dhb, kbf) + egc * dqg
            dk = _dot_tn(dgb, kb.astype(BF16)) + _dot_tn(dhb, q.astype(BF16)) + beta * (egc * dbk + dkb) + ekd * dkd
            dbeta = rsum(dbv * v) + tk * egc + rsum(dkb * k)
            dgc = dgc + tk * beta * egc + egc * rsum(dqg * q) - rk
            dglast = jnp.sum(rk, axis=0, keepdims=True) + dgl * gl
            dgc = dgc + jnp.where(_subl(dgc.shape) == CHUNK - 1, dglast, 0.0)
            dqkv_ref[0, rows, hs] = dq
            dqkv_ref[1, rows, hs] = dk
            dqkv_ref[2, rows, hs] = beta * dbv
            lane = _lane((CHUNK, LANES))
            dg_ref[h, rows, :] = jnp.where(lane == 0, dbeta, jnp.where(lane == 1, dgc, 0.0))

        lax.fori_loop(0, nch, step, 0)

    row = lambda b, i: b * nt + nt - 1 - i
    wb = lambda col: BS((tt, C_W), lambda b, i: (row(b, i), col))
    return pl.pallas_call(
        body, name="gdn_chunk_bwd", grid=(nseq, nt),
        in_specs=[wb(0), wb(1), wb(2), BS((tt, LANES), lambda b, i: (row(b, i), 0)), wb(P_ZC // C_W),
                  BS((1, LANES), lambda b, i: (0, 0)), wb(0), wb(Y_C // C_W), wb(0), wb(0), wb(0),
                  BS((None, C_HEADS, nch, C_DH, C_DH), lambda b, i: (b, 0, nt - 1 - i, 0, 0)), BS(memory_space=pl.ANY)],
        out_specs=[wb(P_ZC // C_W), BS((3, tt, C_W), lambda b, i: (0, row(b, i), 0)),
                   BS((C_HEADS, tt, LANES), lambda b, i: (0, row(b, i), 0)), BS((SUB, LANES), lambda b, i: (0, 0))],
        out_shape=[S(dp.shape, dp.dtype), S((3, n, C_W), F32), S((C_HEADS, n, LANES), F32), S((SUB, LANES), F32)],
        input_output_aliases={12: 0},
        scratch_shapes=[pltpu.VMEM((C_HEADS, C_DH, C_DH), F32)],
        compiler_params=_cp("arbitrary", "arbitrary"),
    )(qkv, qkv, qkv, gates, p, onw, o, dy, u, w, tinv, ss, dp)


def _gdn_gates_bwd(dgate, p, alog_l, dtb_l, dp, tm=256):
    n = p.shape[0]
    acc = BS((SUB, LANES), lambda i: (0, 0))

    def body(dg_ref, ba_ref, al_ref, db_ref, dp_in_ref, dba_ref, dal_ref, ddb_ref):
        @pl.when(pl.program_id(0) == 0)
        def _():
            dal_ref[...] = jnp.zeros_like(dal_ref)
            ddb_ref[...] = jnp.zeros_like(ddb_ref)

        blk = ba_ref[...]
        lane = _lane(blk.shape)
        dbeta = jnp.zeros_like(blk)
        dgc = jnp.zeros_like(blk)
        for h in range(C_HEADS):
            dbeta = dbeta + jnp.where(lane == GB + h, _col(dg_ref[h], 0), 0.0)
            dgc = dgc + jnp.where(lane == GG + h, _col(dg_ref[h], 1), 0.0)
        tri = (_subl((CHUNK, CHUNK)) <= _lane((CHUNK, CHUNK))).astype(F32)
        dg = jnp.concatenate([_dot(tri, dgc[CHUNK * c:CHUNK * (c + 1)], HI) for c in range(tm // CHUNK)], axis=0)
        beta = _sigmoid(blk)
        z = blk + db_ref[...]
        ea = jnp.exp(al_ref[...])
        isg = (lane >= GG) & (lane < GG + C_HEADS)
        dz = jnp.where(isg, -dg * ea * _sigmoid(z), 0.0)
        dal_ref[...] += _fold8(jnp.where(isg, -dg * ea * _softplus(z), 0.0))
        ddb_ref[...] += _fold8(dz)
        out = jnp.where(lane < GG, dbeta * beta * (1.0 - beta), dz)
        dba_ref[...] = jnp.concatenate([out, jnp.zeros_like(out)], axis=1).astype(BF16)

    return pl.pallas_call(
        body, name="gdn_gates_bwd", grid=(n // tm,),
        in_specs=[BS((C_HEADS, tm, LANES), lambda i: (0, i, 0)), BS((tm, LANES), lambda i: (i, P_BA // LANES)),
                  BS((1, LANES), lambda i: (0, 0)), BS((1, LANES), lambda i: (0, 0)), BS(memory_space=pl.ANY)],
        out_specs=[BS((tm, 2 * LANES), lambda i: (i, P_BA // (2 * LANES))), acc, acc],
        out_shape=[S(dp.shape, dp.dtype), S((SUB, LANES), F32), S((SUB, LANES), F32)],
        input_output_aliases={4: 0},
        compiler_params=_cp("arbitrary"),
    )(dgate, p, alog_l, dtb_l, dp)


def _gdn_pre_bwd(p, dqkv, xc, dp, ccw, nseq, tm=256):
    n = p.shape[0]
    t = n // nseq
    nt = t // tm
    wide = 3 * C_W
    row = lambda b, i: b * nt + i
    prev = lambda b, i: jnp.maximum((b * t + i * tm) // HALO_C - 1, 0)
    nxt = lambda b, i: jnp.minimum((b * t + (i + 1) * tm) // HALO_C, n // HALO_C - 1)

    def d_conv_out(d, xc, part):
        sg = _sigmoid(xc)
        act = xc * sg
        if part < 2:
            cs = QS if part == 0 else 1.0
            rn = lax.rsqrt(jnp.sum(act * act, axis=-1, keepdims=True) + EPS)
            d = cs * rn * d - act * (cs * rn * rn * rn * jnp.sum(d * act, axis=-1, keepdims=True))
        return d * _dsilu(xc, sg)

    def body(x_ref, xh_ref, d_ref, dn_ref, xc_ref, xn_ref, w_ref, dp_in_ref, dx_ref, dw_ref, buf_ref, dbuf_ref):
        i = pl.program_id(1)

        @pl.when((pl.program_id(0) == 0) & (i == 0))
        def _():
            dw_ref[...] = jnp.zeros_like(dw_ref)

        buf_ref[0:HALO_C, :] = jnp.where(i > 0, xh_ref[...], 0.0)
        buf_ref[HALO_C:, :] = x_ref[...]
        for c in range(NCB):
            cs = slice(LANES * c, LANES * (c + 1))
            part, hd = divmod(c, C_HEADS)
            hs = slice(LANES * hd, LANES * (hd + 1))
            d = d_conv_out(d_ref[part, :, hs], xc_ref[:, cs], part)
            dbuf_ref[0:tm, cs] = d
            dbuf_ref[tm:, cs] = jnp.where(i < nt - 1, d_conv_out(dn_ref[part, :, hs], xn_ref[:, cs], part), 0.0)
            dx = jnp.zeros((tm, LANES), F32)
            for k in range(C_K):
                dx = dx + w_ref[k:k + 1, cs] * dbuf_ref[pl.ds(C_K - 1 - k, tm), cs]
                dw_ref[SUB * k:SUB * (k + 1), cs] += _fold8(d * buf_ref[pl.ds(HALO_C - C_K + 1 + k, tm), cs])
            dx_ref[:, cs] = dx.astype(BF16)

    return pl.pallas_call(
        body, name="gdn_pre_bwd", grid=(nseq, nt),
        in_specs=[BS((tm, wide), lambda b, i: (row(b, i), P_QKV // wide)), BS((HALO_C, wide), lambda b, i: (prev(b, i), P_QKV // wide)),
                  BS((3, tm, C_W), lambda b, i: (0, row(b, i), 0)), BS((3, HALO_C, C_W), lambda b, i: (0, nxt(b, i), 0)),
                  BS((tm, wide), lambda b, i: (row(b, i), 0)), BS((HALO_C, wide), lambda b, i: (nxt(b, i), 0)),
                  BS((SUB, wide), lambda b, i: (0, 0)), BS(memory_space=pl.ANY)],
        out_specs=[BS((tm, wide), lambda b, i: (row(b, i), P_QKV // wide)), BS((SUB * C_K, wide), lambda b, i: (0, 0))],
        out_shape=[S(dp.shape, dp.dtype), S((SUB * C_K, wide), F32)], input_output_aliases={7: 0},
        scratch_shapes=[pltpu.VMEM((HALO_C + tm, wide), F32)] * 2,
        compiler_params=_cp("arbitrary", "arbitrary"),
    )(p, p, dqkv, dqkv, xc, xc, ccw, dp)


ANY = BS(memory_space=pl.ANY)


def _my_pos():
    return lax.axis_index("x"), lax.axis_index("y"), lax.axis_index("c")


def _dev_index(dev):
    return 4 * dev[0] + 2 * dev[1] + dev[2]


def _all_gather(shards, after=None):
    nk = len(shards)

    tail = [] if after is None else [after]

    def body(*refs):
        ins, outs = refs[:nk], refs[nk + len(tail):2 * nk + len(tail)]
        send, recv, loc = refs[2 * nk + len(tail):]
        x, y, c = _my_pos()
        me, sib = (x, y, c), (x, y, 1 - c)
        chips = [(1 - x, y), (x, 1 - y), (1 - x, 1 - y)]

        def rows(t, dev):
            r = ins[t].shape[0]
            return outs[t].at[pl.ds(pl.multiple_of(_dev_index(dev) * r, SUB), r), :]

        def copy(t, k, block, to, src=None):
            return pltpu.make_async_remote_copy(
                src_ref=rows(t, block) if src is None else src, dst_ref=rows(t, block),
                send_sem=send.at[t, k], recv_sem=recv.at[t, k], device_id=to, device_id_type=MESH)

        mine = [pltpu.make_async_copy(ins[t], rows(t, me), loc.at[t]) for t in range(nk)]
        for cp in mine:
            cp.start()
        first = []
        for t in range(nk):
            first.append(copy(t, 0, me, sib, src=ins[t]))
            first += [copy(t, 1 + j, me, (*chip, c), src=ins[t]) for j, chip in enumerate(chips)]
        for cp in first:
            cp.start()
        passed = []
        for j, chip in enumerate(chips):
            for t in range(nk):
                copy(t, 1 + j, (*chip, c), me).wait_recv()
                cp = copy(t, 4 + j, (*chip, c), sib)
                cp.start()
                passed.append(cp)
        for t in range(nk):
            copy(t, 0, sib, me).wait_recv()
            for j, chip in enumerate(chips):
                copy(t, 4 + j, (*chip, 1 - c), me).wait_recv()
        for cp in first + passed:
            cp.wait_send()
        for cp in mine:
            cp.wait()

    return pl.pallas_call(
        body, name="all_gather", in_specs=[ANY] * (nk + len(tail)), out_specs=[ANY] * nk,
        out_shape=[S((N_DEV * a.shape[0], a.shape[1]), a.dtype) for a in shards],
        scratch_shapes=[pltpu.SemaphoreType.DMA((nk, 7)), pltpu.SemaphoreType.DMA((nk, 7)), pltpu.SemaphoreType.DMA((nk,))],
    )(*shards, *tail)


SEM = BS(memory_space=pltpu.SEMAPHORE)
HBM = BS(memory_space=pltpu.HBM)
EFFECT = pltpu.SideEffectType.DATAFLOW_SIDE_EFFECTING


def _peers(x, y, c):
    return [((1 - x) if k & 4 else x, (1 - y) if k & 2 else y, (1 - c) if k & 1 else c) for k in range(1, N_DEV)]


def _exchange_copy(kind, src, land, send, recv, t, k, peer, me, arriving):
    frm = peer if arriving else me
    if kind == "gather":
        r = src.shape[0]
        s_ref = src
        d_ref = land.at[pl.ds(pl.multiple_of(_dev_index(frm) * r, SUB), r), :]
    else:
        r = src.shape[0] // N_DEV
        s_ref = src.at[pl.ds(pl.multiple_of(_dev_index(peer) * r, SUB), r), :]
        d_ref = land.at[_dev_index(frm)]
    sem = t * (N_DEV - 1) + k
    return pltpu.make_async_remote_copy(src_ref=s_ref, dst_ref=d_ref, send_sem=send.at[sem], recv_sem=recv.at[sem],
                                        device_id=peer, device_id_type=MESH)


def _own_copy(kind, src, land, own, t, me):
    if kind == "gather":
        r = src.shape[0]
        return pltpu.make_async_copy(src, land.at[pl.ds(pl.multiple_of(_dev_index(me) * r, SUB), r), :], own.at[t])
    r = src.shape[0] // N_DEV
    return pltpu.make_async_copy(src.at[pl.ds(pl.multiple_of(_dev_index(me) * r, SUB), r), :], land.at[_dev_index(me)], own.at[t])


def _exchange_start(kind, srcs, after, name):
    nk = len(srcs)
    if kind == "gather":
        lands = [lax.empty((N_DEV * a.shape[0], a.shape[1]), a.dtype) for a in srcs]
    else:
        lands = [lax.empty((N_DEV, a.shape[0] // N_DEV, a.shape[1]), a.dtype) for a in srcs]

    def body(*refs):
        src, land = refs[:nk], refs[nk:2 * nk]
        send, recv, own = refs[2 * nk + 1], refs[2 * nk + 2], refs[2 * nk + 3]
        token = refs[-1]
        x, y, c = _my_pos()
        me = (x, y, c)
        for t in range(nk):
            _own_copy(kind, src[t], land[t], own, t, me).start()
            for k, peer in enumerate(_peers(x, y, c)):
                _exchange_copy(kind, src[t], land[t], send, recv, t, k, peer, me, False).start()
        token[...] = jnp.zeros_like(token)

    hbm = lambda a: pltpu.HBM(a.shape, a.dtype)
    out = pl.pallas_call(
        body, name=name,
        out_shape=(pltpu.SemaphoreType.DMA((nk * (N_DEV - 1),)), pltpu.SemaphoreType.DMA((nk * (N_DEV - 1),)),
                   pltpu.SemaphoreType.DMA((nk,)), *[hbm(a) for a in srcs], *[hbm(a) for a in lands], S((SUB, LANES), F32)),
        in_specs=[HBM] * (2 * nk) + [ANY],
        out_specs=(SEM, SEM, SEM, *[HBM] * (2 * nk), BS(memory_space=pltpu.VMEM)),
        input_output_aliases={i: 3 + i for i in range(2 * nk)},
        compiler_params=pltpu.CompilerParams(has_side_effects=EFFECT),
    )(*[pltpu.with_memory_space_constraint(a, pltpu.HBM) for a in (*srcs, *lands)], after)
    return dict(kind=kind, nk=nk, send=out[0], recv=out[1], own=out[2], srcs=out[3:3 + nk], lands=out[3 + nk:3 + 2 * nk],
                token=out[-1])


def _exchange_wait(ex, after, name):
    kind, nk = ex["kind"], ex["nk"]

    def body(*refs):
        src, land = refs[:nk], refs[nk:2 * nk]
        send, recv, own = refs[2 * nk], refs[2 * nk + 1], refs[2 * nk + 2]
        x, y, c = _my_pos()
        me = (x, y, c)
        for t in range(nk):
            _own_copy(kind, src[t], land[t], own, t, me).wait()
            for k, peer in enumerate(_peers(x, y, c)):
                _exchange_copy(kind, src[t], land[t], send, recv, t, k, peer, me, False).wait_send()
                _exchange_copy(kind, src[t], land[t], send, recv, t, k, peer, me, True).wait_recv()

    hbm = lambda a: pltpu.HBM(a.shape, a.dtype)
    out = pl.pallas_call(
        body, name=name,
        out_shape=(*[hbm(a) for a in ex["srcs"]], *[hbm(a) for a in ex["lands"]]),
        in_specs=[HBM] * (2 * nk) + [SEM, SEM, SEM, ANY], out_specs=tuple([HBM] * (2 * nk)),
        input_output_aliases={i: i for i in range(2 * nk)},
        compiler_params=pltpu.CompilerParams(has_side_effects=EFFECT),
    )(*ex["srcs"], *ex["lands"], ex["send"], ex["recv"], ex["own"], after)
    return list(out[nk:])


BLOCK_BYTES = 4 << 20


def _row_tile(rows, row_bytes, align):
    best = align
    for tr in range(align, rows + 1, align):
        if rows % tr == 0 and tr * row_bytes <= BLOCK_BYTES:
            best = tr
    return best


def _sum8(a):
    _, r, w = a.shape
    tr = _row_tile(r, N_DEV * w * a.dtype.itemsize, 32 // a.dtype.itemsize)

    def body(a_ref, o_ref):
        acc = a_ref[0].astype(F32)
        for d in range(1, N_DEV):
            acc = acc + a_ref[d].astype(F32)
        o_ref[...] = acc

    return pl.pallas_call(
        body, name="sum8", grid=(r // tr,), in_specs=[BS((N_DEV, tr, w), lambda i: (0, i, 0))],
        out_specs=BS((tr, w), lambda i: (i, 0)), out_shape=S((r, w), F32), compiler_params=_cp("arbitrary"),
    )(a)


def _adamw(w, g, m, v):
    r, c = w.shape
    tr = _row_tile(r, c * 4 * 2, SUB)

    def body(w_ref, g_ref, m_ref, v_ref, d_ref, mo_ref, vo_ref):
        d_ref[...], mo_ref[...], vo_ref[...] = _adam_update(w_ref[...], g_ref[...], m_ref[...], v_ref[...])

    blk = BS((tr, c), lambda i: (i, 0))
    return pl.pallas_call(
        body, name="adamw", grid=(r // tr,), in_specs=[blk] * 4, out_specs=[blk] * 3,
        out_shape=[S((r, c), F32)] * 3, compiler_params=_cp("arbitrary"),
    )(w, g, m, v)


def _sum8_t(a, tc=256):
    _, r, w = a.shape

    def body(a_ref, o_ref):
        acc = a_ref[0].astype(F32)
        for d in range(1, N_DEV):
            acc = acc + a_ref[d].astype(F32)
        o_ref[...] = acc.T

    return pl.pallas_call(
        body, name="sum8_t", grid=(w // tc,), in_specs=[BS((N_DEV, r, tc), lambda j: (0, 0, j))],
        out_specs=BS((tc, r), lambda j: (j, 0)), out_shape=S((w, r), F32), compiler_params=_cp("arbitrary"),
    )(a)


def _rows_view(a):
    nl, r, c = a.shape
    assert nl == 2
    return a.transpose(2, 0, 1).reshape(c, nl, r // LANES, LANES).transpose(0, 2, 1, 3).reshape(-1, LANES)


def _rows_view_back(a, shape):
    nl, r, c = shape
    return a.reshape(c, r // LANES, nl, LANES).transpose(0, 2, 1, 3).reshape(c, nl, r).transpose(1, 2, 0)


def _adamw_rows(w, g, m, v, tr=2048):
    n = w.shape[0]

    def body(w_ref, g_ref, m_ref, v_ref, d_ref, mo_ref, vo_ref):
        d_ref[...], mo_ref[...], vo_ref[...] = _adam_update(w_ref[...], g_ref[...], m_ref[...], v_ref[...])

    blk = BS((tr, LANES), lambda i: (i, 0))
    return pl.pallas_call(
        body, name="adamw_rows", grid=(pl.cdiv(n, tr),), in_specs=[blk] * 4, out_specs=[blk] * 3,
        out_shape=[S((n, LANES), F32)] * 3, compiler_params=_cp("arbitrary"),
    )(w, g, m, v)


def _adam_update(w, g, m, v):
    m2 = ADAM_B1 * m + (1.0 - ADAM_B1) * g
    v2 = ADAM_B2 * v + (1.0 - ADAM_B2) * (g * g)
    m_hat = m2 / (1.0 - ADAM_B1 ** ADAM_STEP)
    v_hat = v2 / (1.0 - ADAM_B2 ** ADAM_STEP)
    return -ADAM_LR * (m_hat / (jnp.sqrt(v_hat) + ADAM_EPS) + ADAM_WD * w), m2, v2


def _adamw_layer(w, g, m, v, l, prev):
    nl, r, c = w.shape
    tr = _row_tile(r, c * 4 * 2, SUB)

    def body(w_ref, g_ref, m_ref, v_ref, *refs):
        go_ref, d_ref, mo_ref, vo_ref = refs[-4:]
        gv = g_ref[...]
        go_ref[...] = gv
        d_ref[...], mo_ref[...], vo_ref[...] = _adam_update(w_ref[...], gv, m_ref[...], v_ref[...])

    slot = BS((None, tr, c), lambda i: (l, i, 0))
    keep = [] if prev is None else [ANY] * 4
    return pl.pallas_call(
        body, name="adamw_layer", grid=(r // tr,), in_specs=[slot, BS((tr, c), lambda i: (i, 0)), slot, slot] + keep,
        out_specs=[slot] * 4, out_shape=[S((nl, r, c), F32)] * 4,
        input_output_aliases={} if prev is None else {4 + i: i for i in range(4)},
        compiler_params=_cp("arbitrary"),
    )(w, g, m, v, *(prev or ()))


def _blob(arrays):
    flat = jnp.concatenate([a.reshape(-1) for a in arrays])
    rows = -(-flat.shape[0] // (SUB * LANES)) * SUB
    return jnp.pad(flat, (0, rows * LANES - flat.shape[0])).reshape(rows, LANES)


def _unblob(blob, shapes, lead=()):
    flat = blob.reshape(lead + (-1,))
    out, off = [], 0
    for s in shapes:
        size = math.prod(s)
        out.append(flat[..., off:off + size].reshape(lead + tuple(s)))
        off += size
    return out


def _y_rows(w):
    return jnp.concatenate([w[0:A_W], w[A_W + B_W:], w[A_W:A_W + B_W]], axis=0)


def _y_rows_back(g):
    return jnp.concatenate([g[0:A_W], g[A_W + C_W:], g[A_W:A_W + C_W]], axis=0)


SMALL = ("norm_w", "q_norm_w", "k_norm_w", "sinks", "b_conv_b", "b_ln_w", "b_ln_b", "b_pw_b", "c_a_log", "c_dt_bias",
         "c_onorm_w", "b_conv_w", "c_conv_w")
ORDER = ("norm_w", "w_in", "q_norm_w", "k_norm_w", "sinks", "b_conv_w", "b_conv_b", "b_ln_w", "b_ln_b", "b_pw_w", "b_pw_b",
         "c_conv_w", "c_a_log", "c_dt_bias", "c_onorm_w", "w_out")


def kernel(x, positions, norm_w, w_in, q_norm_w, k_norm_w, sinks, b_conv_w, b_conv_b, b_ln_w, b_ln_b, b_pw_w, b_pw_b, c_conv_w, c_a_log, c_dt_bias, c_onorm_w, w_out, loss_target, m_norm_w, m_w_in, m_q_norm_w, m_k_norm_w, m_sinks, m_b_conv_w, m_b_conv_b, m_b_ln_w, m_b_ln_b, m_b_pw_w, m_b_pw_b, m_c_conv_w, m_c_a_log, m_c_dt_bias, m_c_onorm_w, m_w_out, v_norm_w, v_w_in, v_q_norm_w, v_k_norm_w, v_sinks, v_b_conv_w, v_b_conv_b, v_b_ln_w, v_b_ln_b, v_b_pw_w, v_b_pw_b, v_c_conv_w, v_c_a_log, v_c_dt_bias, v_c_onorm_w, v_w_out):
    W = dict(norm_w=norm_w, w_in=w_in, q_norm_w=q_norm_w, k_norm_w=k_norm_w, sinks=sinks, b_conv_w=b_conv_w, b_conv_b=b_conv_b,
             b_ln_w=b_ln_w, b_ln_b=b_ln_b, b_pw_w=b_pw_w, b_pw_b=b_pw_b, c_conv_w=c_conv_w, c_a_log=c_a_log,
             c_dt_bias=c_dt_bias, c_onorm_w=c_onorm_w, w_out=w_out)
    M = dict(norm_w=m_norm_w, w_in=m_w_in, q_norm_w=m_q_norm_w, k_norm_w=m_k_norm_w, sinks=m_sinks, b_conv_w=m_b_conv_w,
             b_conv_b=m_b_conv_b, b_ln_w=m_b_ln_w, b_ln_b=m_b_ln_b, b_pw_w=m_b_pw_w, b_pw_b=m_b_pw_b, c_conv_w=m_c_conv_w,
             c_a_log=m_c_a_log, c_dt_bias=m_c_dt_bias, c_onorm_w=m_c_onorm_w, w_out=m_w_out)
    V = dict(norm_w=v_norm_w, w_in=v_w_in, q_norm_w=v_q_norm_w, k_norm_w=v_k_norm_w, sinks=v_sinks, b_conv_w=v_b_conv_w,
             b_conv_b=v_b_conv_b, b_ln_w=v_b_ln_w, b_ln_b=v_b_ln_b, b_pw_w=v_b_pw_w, b_pw_b=v_b_pw_b, c_conv_w=v_c_conv_w,
             c_a_log=v_c_a_log, c_dt_bias=v_c_dt_bias, c_onorm_w=v_c_onorm_w, w_out=v_w_out)
    nseq, t, d = x.shape
    n = nseq * t
    tr = min(256, t)
    tmm = min(512, n)
    tmw = min(1024, n)
    tkk = min(2048, n)
    me = _dev_index(_my_pos())
    xs = [x.reshape(n, d)]
    tgt = loss_target.reshape(n, d)
    tabs = _rope_tables(positions.reshape(n))

    win_p = _pack_cols(w_in).astype(BF16)
    wout_b = w_out.astype(BF16)
    sharded_small = (b_pw_w, b_conv_w, c_conv_w)
    g_win0, g_small = _all_gather([win_p[0], _blob(sharded_small)])
    win = [g_win0]
    later = _exchange_start("gather", [win_p[1], wout_b[0], wout_b[1]], g_small, "gather_start")
    pw_all, cw_all, ccw_all = _unblob(g_small, [a.shape for a in sharded_small], lead=(N_DEV,))
    pw_all = pw_all.transpose(1, 0, 2, 3).reshape(DEPTH, B_W, B_W).astype(BF16)
    cw_all = jnp.pad(cw_all.transpose(1, 2, 0, 3).reshape(DEPTH, B_K, B_W), ((0, 0), (0, HALO_B - B_K), (0, 0)))
    ccw_all = jnp.pad(ccw_all.transpose(1, 2, 0, 3).reshape(DEPTH, C_K, 3 * C_W), ((0, 0), (0, SUB - C_K), (0, 0)))
    qw_all, kw_all = jnp.tile(q_norm_w, (1, 2)), jnp.tile(k_norm_w, (1, 2))
    lanes6 = lambda a: jnp.zeros((DEPTH, LANES), F32).at[:, GG:GG + C_HEADS].set(a)
    alog_all, dtb_all = lanes6(c_a_log), lanes6(c_dt_bias)

    def layer_params(l):
        row = lambda a: a[l][None]
        return dict(
            nw=row(norm_w), qw=row(qw_all), kw=row(kw_all), sinks=sinks[l], cw=cw_all[l], cb=row(b_conv_b), lnw=row(b_ln_w),
            lnb=row(b_ln_b), pw=pw_all[l], pwb=row(b_pw_b), ccw=ccw_all[l], alog=row(alog_all), dtb=row(dtb_all),
            onw=row(c_onorm_w))

    saved = []
    for l in range(DEPTH):
        q = layer_params(l)
        nw = q["nw"] + later["token"][0:1, 0:1] if l == 0 else q["nw"]
        p, h = _inproj(xs[l], nw, win[l], tm=tmw)
        y, o_a, lse = _attn_fwd(p, tabs, q["qw"], q["kw"], q["sinks"], nseq)
        gates = _gdn_gates_fwd(p, q["alog"], q["dtb"], tm=tr)
        xc, qkv = _gdn_pre_fwd(p, q["ccw"], nseq, tm=tr)
        y, o_c, u, w, tinv, ss = _gdn_chunk_fwd(qkv, gates, p, y, q["onw"], nseq)
        y, hc = _conf_fwd(p, y, q["cw"], q["cb"], q["lnw"], q["lnb"], q["pw"], q["pwb"], nseq, tm=tr)
        saved.append(dict(q=q, p=p, h=h, y=y, o_a=o_a, lse=lse, gates=gates, xc=xc, qkv=qkv, o_c=o_c, u=u, w=w, tinv=tinv,
                          ss=ss, hc=hc))
        if l == 0:
            g_win1, g_wout0, g_wout1 = _exchange_wait(later, y, "gather_wait")
            win.append(g_win1)
            wout = [_y_rows(g_wout0), _y_rows(g_wout1)]
        if l + 1 < DEPTH:
            xs.append(_outproj(xs[l], y, wout[l], tm=tmw, tn=512))
        else:
            dxn, lsum = _outproj_loss(xs[l], y, wout[l], tgt, tm=tmw, tn=512)
    loss = lax.psum(jnp.sum(lsum) * (0.5 / d), ("x", "y", "c"))

    sent, smalls = [None] * DEPTH, [None] * DEPTH
    for l in reversed(range(DEPTH)):
        s = saved[l]
        q, p = s["q"], s["p"]
        dy = _matmul(dxn, wout[l], "nt", F32, tmw, 1024, d, "outproj_bwd_dy")
        dwout = _y_rows_back(_matmul(s["y"], dxn, "tn", BF16, 1024, 1024, tkk, "outproj_bwd_dw"))
        dp, dkv, dqw, dkw, dsk = _attn_bwd(p, dy, s["o_a"], s["lse"], tabs, q["qw"], q["kw"], q["sinks"], nseq)
        dp, dqkv, dgate, donw = _gdn_chunk_bwd(s["qkv"], s["gates"], p, dy, dp, q["onw"], s["o_c"], s["u"], s["w"],
                                               s["tinv"], s["ss"], nseq)
        dp, dccw = _gdn_pre_bwd(p, dqkv, s["xc"], dp, q["ccw"], nseq, tm=tr)
        early = P_K // 768
        dwin_a = _matmul(s["h"], dp, "tn", BF16, 1024, 768, tkk, "inproj_bwd_dw_a", b_cols=(0, early))
        sent_a = _exchange_start("scatter", [dwin_a, dwout], donw, "scatter_start_a%d" % l)
        dp = _put_cols(dp, dkv, P_K, sent_a["token"], tm=tmm)
        dp, dal, ddb = _gdn_gates_bwd(dgate, p, q["alog"], q["dtb"], dp, tm=tr)
        dp, dhc, dpw, dpwb, dlnw, dlnb, dcb = _conf_bwd1(p, dy, dp, s["hc"], q["lnw"], q["lnb"], q["pw"], q["pwb"], tm=tr)
        dp, dcw = _conf_bwd2(p, dhc, dp, q["cw"], nseq, tm=tr)
        dwin_b = _matmul(s["h"], dp, "tn", BF16, 1024, 768, tkk, "inproj_bwd_dw_b", b_cols=(early, P_W // 768 - early))
        sent_b = _exchange_start("scatter", [dwin_b, dpw], dpwb, "scatter_start_b%d" % l)
        sent[l] = (sent_a, sent_b)
        dxn, dnw = _inproj_bwd_dx(dp, win[l], xs[l], q["nw"] + sent_b["token"][0:1, 0:1], dxn, tm=tmm)
        halves = lambda a: a.sum(0)[:A_DH] + a.sum(0)[A_DH:]
        smalls[l] = dict(
            norm_w=dnw.sum(0), q_norm_w=halves(dqw), k_norm_w=halves(dkw), sinks=dsk.sum(0)[:A_HEADS], b_conv_b=dcb.sum(0),
            b_ln_w=dlnw.sum(0), b_ln_b=dlnb.sum(0), b_pw_b=dpwb.sum(0), c_a_log=dal.sum(0)[GG:GG + C_HEADS],
            c_dt_bias=ddb.sum(0)[GG:GG + C_HEADS], c_onorm_w=donw.sum(0),
            b_conv_w=dcw.reshape(B_K, SUB, B_W).sum(1), c_conv_w=dccw.reshape(C_K, SUB, 3 * C_W).sum(1))
    grad_x = dxn.reshape(nseq, t, d)

    G, delta, new_m, new_v = {}, {}, {}, {}
    big = ("w_in", "w_out", "b_pw_w")
    stacks = {k: None for k in big}
    after = dxn
    g_t = [None] * DEPTH
    for l in reversed(range(DEPTH)):
        r_win_a, r_wout = _exchange_wait(sent[l][0], after, "scatter_wait_a%d" % l)
        r_win_b, r_pw = _exchange_wait(sent[l][1], r_wout, "scatter_wait_b%d" % l)
        g_t[l] = jnp.concatenate([_sum8_t(r_win_a), _sum8_t(r_win_b)], axis=0).reshape(P_W, -1, LANES)
        for k, r in (("w_out", r_wout), ("b_pw_w", r_pw)):
            stacks[k] = _adamw_layer(W[k], _sum8(r), M[k], V[k], l, stacks[k])
        after = stacks["w_out"][1]
    g_in = jnp.stack(g_t, axis=2).reshape(-1, LANES)
    g_in = _unpack_cols(g_in, axis=0, each=g_in.shape[0] // P_W)
    rows = _adamw_rows(_rows_view(w_in), g_in, _rows_view(m_w_in), _rows_view(v_w_in))
    stacks["w_in"] = [_rows_view_back(a, w_in.shape) for a in (g_in, *rows)]
    for k in big:
        G[k], delta[k], new_m[k], new_v[k] = stacks[k]
    part = _blob([jnp.stack([smalls[l][k] for l in range(DEPTH)]) for k in SMALL])
    (tot,) = _all_gather([part], after=rows[0])
    tot = _sum8(tot.reshape(N_DEV, part.shape[0], LANES))
    full_shapes = [(DEPTH,) + smalls[0][k].shape for k in SMALL]
    for k, g in zip(SMALL, _unblob(tot, full_shapes)):
        G[k] = g
    G["b_conv_w"] = lax.dynamic_slice_in_dim(G["b_conv_w"], me * (B_W // N_DEV), B_W // N_DEV, axis=2)
    G["c_conv_w"] = lax.dynamic_slice_in_dim(G["c_conv_w"], me * (3 * C_W // N_DEV), 3 * C_W // N_DEV, axis=2)
    dl, mo, vo = _adamw(*[_blob([src[k] for k in SMALL]) for src in (W, G, M, V)])
    shapes = [W[k].shape for k in SMALL]
    for k, a, b, c in zip(SMALL, _unblob(dl, shapes), _unblob(mo, shapes), _unblob(vo, shapes)):
        delta[k], new_m[k], new_v[k] = a, b, c
    return (loss, grad_x, *[G[k] for k in ORDER], *[delta[k] for k in ORDER], *[new_m[k] for k in ORDER],
            *[new_v[k] for k in ORDER])
```

```python
import math

import jax
import jax.numpy as jnp
from jax import lax
from jax.experimental import pallas as pl
from jax.experimental.pallas import tpu as pltpu

F32 = jnp.float32
BF16 = jnp.bfloat16
HI = lax.Precision.HIGHEST
MESH = pl.DeviceIdType.MESH
S = jax.ShapeDtypeStruct
BS = pl.BlockSpec

N_DEV = 8
DEPTH = 2
D_MODEL = 2048
A_HEADS, A_KV, A_DH, A_W, A_KVW = 12, 4, 64, 768, 256
ROT = 16
THETA = 500000.0
ABLK = 128
B_W, B_K = 512, 31
C_HEADS, C_DH, C_W, C_K, CHUNK = 6, 128, 768, 4, 64
EPS = 1e-6
IN_COLS = 6668
P_Q, P_ZA, P_ZC, P_QKV, P_K, P_V, P_UB, P_ZB, P_BA, P_W = 0, 768, 1536, 2304, 4608, 4864, 5120, 6144, 6656, 6912
Y_A, Y_C, Y_B = 0, 768, 1536
LANES = 128
SUB = 8

ADAM_LR, ADAM_B1, ADAM_B2, ADAM_EPS, ADAM_WD, ADAM_STEP = 0.001, 0.9, 0.999, 1e-08, 0.01, 10


def _cp(*sem, vmem=None):
    kw = {}
    if sem:
        kw["dimension_semantics"] = sem
    if vmem:
        kw["vmem_limit_bytes"] = vmem
    return pltpu.CompilerParams(**kw)


def _pack_cols(w):
    z = jnp.zeros(w.shape[:-1] + (P_W - IN_COLS,), w.dtype)
    return jnp.concatenate([w[..., 0:768], w[..., 1280:2048], w[..., 5900:6668], w[..., 3584:5888],
                            w[..., 768:1024], w[..., 1024:1280], w[..., 2048:3072], w[..., 3072:3584],
                            w[..., 5888:5900], z], axis=-1)


def _unpack_cols(g, axis=-1, each=1):
    parts = ((P_Q, 768), (P_K, 256), (P_V, 256), (P_ZA, 768), (P_UB, 1024), (P_ZB, 512), (P_QKV, 2304), (P_BA, 12), (P_ZC, 768))
    return jnp.concatenate([lax.slice_in_dim(g, each * o, each * (o + n), axis=axis) for o, n in parts], axis=axis)


def _sigmoid(x):
    return 0.5 * jnp.tanh(0.5 * x) + 0.5


def _dsilu(x, sg):
    return sg * (1.0 + x * (1.0 - sg))


def _fold8(x):
    r, c = x.shape
    return x.reshape(r // SUB, SUB, c).sum(axis=0)


def _dot(a, b, prec=None):
    return jnp.dot(a, b, preferred_element_type=F32, precision=prec)


def _dot_nt(a, b, prec=None):
    return lax.dot_general(a, b, (((1,), (1,)), ((), ())), preferred_element_type=F32, precision=prec)


def _dot_tn(a, b, prec=None):
    return lax.dot_general(a, b, (((0,), (0,)), ((), ())), preferred_element_type=F32, precision=prec)


def _lane(shape):
    return lax.broadcasted_iota(jnp.int32, shape, 1)


def _subl(shape):
    return lax.broadcasted_iota(jnp.int32, shape, 0)


def _col(x, j):
    return jnp.sum(jnp.where(_lane(x.shape) == j, x, 0.0), axis=-1, keepdims=True)


def _inproj(x, nw, w, tm=512, tn=768):
    n, d = x.shape
    pw = w.shape[1]

    def body(x_ref, nw_ref, w_ref, p_ref, h_ref):
        @pl.when(pl.program_id(1) == 0)
        def _():
            xv = x_ref[...]
            r = lax.rsqrt(jnp.mean(xv * xv, axis=-1, keepdims=True) + EPS)
            h_ref[...] = (xv * r * nw_ref[...]).astype(BF16)

        p_ref[...] = _dot(h_ref[...], w_ref[...])

    return pl.pallas_call(
        body, name="inproj", grid=(n // tm, pw // tn),
        in_specs=[BS((tm, d), lambda i, j: (i, 0)), BS((1, d), lambda i, j: (0, 0)), BS((d, tn), lambda i, j: (0, j))],
        out_specs=[BS((tm, tn), lambda i, j: (i, j)), BS((tm, d), lambda i, j: (i, 0))],
        out_shape=[S((n, pw), F32), S((n, d), BF16)],
        compiler_params=_cp("arbitrary", "arbitrary"),
    )(x, nw, w)


def _outproj(x, y, w, tm=512, tn=1024):
    n, d = x.shape
    k = y.shape[1]

    def body(x_ref, y_ref, w_ref, o_ref):
        o_ref[...] = x_ref[...] + _dot(y_ref[...], w_ref[...])

    return pl.pallas_call(
        body, name="outproj", grid=(n // tm, d // tn),
        in_specs=[BS((tm, tn), lambda i, j: (i, j)), BS((tm, k), lambda i, j: (i, 0)), BS((k, tn), lambda i, j: (0, j))],
        out_specs=BS((tm, tn), lambda i, j: (i, j)),
        out_shape=S((n, d), F32),
        compiler_params=_cp("arbitrary", "arbitrary"),
    )(x, y, w)


def _outproj_loss(x, y, w, tgt, tm=512, tn=1024):
    n, d = x.shape
    k = y.shape[1]

    def body(x_ref, y_ref, w_ref, t_ref, g_ref, gb_ref, l_ref):
        @pl.when((pl.program_id(0) == 0) & (pl.program_id(1) == 0))
        def _():
            l_ref[...] = jnp.zeros_like(l_ref)

        diff = x_ref[...] + _dot(y_ref[...], w_ref[...]) - t_ref[...]
        g = diff * (1.0 / d)
        g_ref[...] = g
        gb_ref[...] = g.astype(BF16)
        f = _fold8(diff * diff)
        acc = f[:, 0:LANES]
        for c in range(1, tn // LANES):
            acc = acc + f[:, c * LANES:(c + 1) * LANES]
        l_ref[...] += acc

    return pl.pallas_call(
        body, name="outproj_loss", grid=(n // tm, d // tn),
        in_specs=[BS((tm, tn), lambda i, j: (i, j)), BS((tm, k), lambda i, j: (i, 0)), BS((k, tn), lambda i, j: (0, j)),
                  BS((tm, tn), lambda i, j: (i, j))],
        out_specs=[BS((tm, tn), lambda i, j: (i, j)), BS((tm, tn), lambda i, j: (i, j)), BS((SUB, LANES), lambda i, j: (0, 0))],
        out_shape=[S((n, d), F32), S((n, d), BF16), S((SUB, LANES), F32)],
        compiler_params=_cp("arbitrary", "arbitrary"),
    )(x, y, w, tgt)


def _matmul(a, b, mode, out_dtype, tm, tn, tk, name, b_cols=None):
    if mode == "nn":
        (m, kk), nn = a.shape, b.shape[1]
        a_spec, b_spec = BS((tm, tk), lambda i, j, k: (i, k)), BS((tk, tn), lambda i, j, k: (k, j))
        dot = _dot
    elif mode == "nt":
        (m, kk), nn = a.shape, b.shape[0]
        a_spec, b_spec = BS((tm, tk), lambda i, j, k: (i, k)), BS((tn, tk), lambda i, j, k: (j, k))
        dot = _dot_nt
    else:
        j0, nj = b_cols or (0, b.shape[1] // tn)
        (kk, m), nn = a.shape, nj * tn
        a_spec, b_spec = BS((tk, tm), lambda i, j, k: (k, i)), BS((tk, tn), lambda i, j, k: (k, j0 + j))
        dot = _dot_tn
    nk = kk // tk

    def body(a_ref, b_ref, o_ref, acc_ref):
        kid = pl.program_id(2)

        @pl.when(kid == 0)
        def _():
            acc_ref[...] = jnp.zeros_like(acc_ref)

        acc_ref[...] += dot(a_ref[...].astype(BF16), b_ref[...].astype(BF16))

        @pl.when(kid == nk - 1)
        def _():
            o_ref[...] = acc_ref[...].astype(out_dtype)

    return pl.pallas_call(
        body, name=name, grid=(m // tm, nn // tn, nk),
        in_specs=[a_spec, b_spec], out_specs=BS((tm, tn), lambda i, j, k: (i, j)),
        out_shape=S((m, nn), out_dtype), scratch_shapes=[pltpu.VMEM((tm, tn), F32)],
        compiler_params=_cp("arbitrary", "arbitrary", "arbitrary"),
    )(a, b)


SLAB = 16


def _inproj_bwd_dx(dp, w, x, nw, dres, tm=512, tk=768):
    n, d = x.shape
    nk = dp.shape[1] // tk

    def body(dp_ref, w_ref, x_ref, nw_ref, dr_ref, dx_ref, dxb_ref, dnw_ref, acc_ref):
        kid = pl.program_id(1)

        @pl.when((pl.program_id(0) == 0) & (kid == 0))
        def _():
            dnw_ref[...] = jnp.zeros_like(dnw_ref)

        @pl.when(kid == 0)
        def _():
            acc_ref[...] = jnp.zeros_like(acc_ref)

        acc_ref[...] += _dot_nt(dp_ref[...], w_ref[...])

        @pl.when(kid == nk - 1)
        def _():
            def slab(i, carry):
                rows = pl.ds(pl.multiple_of(i * SLAB, SLAB), SLAB)
                dh, xv = acc_ref[rows, :], x_ref[rows, :]
                r = lax.rsqrt(jnp.mean(xv * xv, axis=-1, keepdims=True) + EPS)
                dnw_ref[...] += _fold8(dh * xv * r)
                g = dh * nw_ref[...]
                mm = jnp.mean(g * xv, axis=-1, keepdims=True)
                dx = dr_ref[rows, :] + r * g - xv * (r * r * r * mm)
                dx_ref[rows, :] = dx
                dxb_ref[rows, :] = dx.astype(BF16)
                return carry

            lax.fori_loop(0, tm // SLAB, slab, 0)

    return pl.pallas_call(
        body, name="inproj_bwd_dx", grid=(n // tm, nk),
        in_specs=[BS((tm, tk), lambda i, k: (i, k)), BS((d, tk), lambda i, k: (0, k)), BS((tm, d), lambda i, k: (i, 0)),
                  BS((1, d), lambda i, k: (0, 0)), BS((tm, d), lambda i, k: (i, 0))],
        out_specs=[BS((tm, d), lambda i, k: (i, 0)), BS((tm, d), lambda i, k: (i, 0)), BS((SUB, d), lambda i, k: (0, 0))],
        out_shape=[S((n, d), F32), S((n, d), BF16), S((SUB, d), F32)],
        scratch_shapes=[pltpu.VMEM((tm, d), F32)],
        compiler_params=_cp("arbitrary", "arbitrary"),
    )(dp, w, x, nw, dres)


def _rope_tables(pos):
    half = ROT // 2
    inv = THETA ** (-jnp.arange(0, ROT, 2, dtype=F32) / ROT)
    ang = pos.astype(F32)[:, None] * inv
    cos, sin = jnp.cos(ang), jnp.sin(ang)
    n = pos.shape[0]
    one = jnp.ones((n, A_DH - ROT), F32)
    zero = jnp.zeros((n, A_DH - ROT), F32)
    zh = jnp.zeros((n, half), F32)
    c = jnp.concatenate([cos, cos, one], axis=1)
    s1 = jnp.concatenate([-sin, zh, zero], axis=1)
    s2 = jnp.concatenate([zh, sin, zero], axis=1)
    return tuple(jnp.concatenate([t, t], axis=1) for t in (c, s1, s2))


def _half_stat(t):
    lo = _lane(t.shape) < A_DH
    s_lo = jnp.sum(jnp.where(lo, t, 0.0), axis=-1, keepdims=True)
    s_hi = jnp.sum(jnp.where(lo, 0.0, t), axis=-1, keepdims=True)
    return jnp.where(lo, s_lo, s_hi)


def _normrope(x, w, c, s1, s2):
    r = lax.rsqrt(_half_stat(x * x) * (1.0 / A_DH) + EPS)
    xn = x * r * w
    return xn * c + pltpu.roll(xn, LANES - ROT // 2, 1) * s1 + pltpu.roll(xn, ROT // 2, 1) * s2, r


def _normrope_bwd(dy, x, r, w, c, s1, s2):
    dxn = dy * c + pltpu.roll(dy * s1, ROT // 2, 1) + pltpu.roll(dy * s2, LANES - ROT // 2, 1)
    g = dxn * w
    mm = _half_stat(g * x) * (1.0 / A_DH)
    return r * g - x * (r * r * r * mm), dxn * x * r


def _attn_mask(first):
    qi = _subl((ABLK, 2 * ABLK))
    kj = _lane((ABLK, 2 * ABLK))
    dist = qi + ABLK - kj
    return (dist >= 0) & (dist < ABLK) & (jnp.logical_not(first) | (kj >= ABLK))


def _keep_half(x, b):
    lo = _lane(x.shape) < A_DH
    return jnp.where(lo if b == 0 else jnp.logical_not(lo), x, jnp.zeros_like(x))


def _head_operand(x, j):
    a, b = j % 2, (j // 3) % 2
    return _keep_half(x if a == b else pltpu.roll(x, A_DH, 1), b)


def _head_result(x, j):
    a, b = j % 2, (j // 3) % 2
    return _keep_half(x if a == b else pltpu.roll(x, A_DH, 1), a)


def _attn_fwd(p, tabs, qw, kw, sinks, nseq):
    n = p.shape[0]
    nb = n // nseq // ABLK
    cur = lambda b, i: (b * nb + i, 0)
    prv = lambda b, i: (b * nb + jnp.maximum(i - 1, 0), 0)
    colblk = lambda f, w, off: (lambda b, i: (f(b, i)[0], off // w))

    def body(q_ref, za_ref, kc_ref, vc_ref, kp_ref, vp_ref, c_ref, s1_ref, s2_ref, cp_ref, s1p_ref, s2p_ref,
             qw_ref, kw_ref, sink_ref, y_ref, o_ref, lse_ref):
        first = pl.program_id(1) == 0
        tc = (c_ref[...], s1_ref[...], s2_ref[...])
        tp = (cp_ref[...], s1p_ref[...], s2p_ref[...])
        q, kc, kp = q_ref[...], kc_ref[...], kp_ref[...]
        qn = [_normrope(q[:, LANES * b:LANES * (b + 1)], qw_ref[...], *tc)[0] for b in range(A_W // LANES)]
        k2, v2 = [], []
        for b in range(A_KVW // LANES):
            sl = slice(LANES * b, LANES * (b + 1))
            k2.append(jnp.concatenate([_normrope(kp[:, sl], kw_ref[...], *tp)[0],
                                       _normrope(kc[:, sl], kw_ref[...], *tc)[0]], axis=0).astype(BF16))
            v2.append(jnp.concatenate([vp_ref[:, sl], vc_ref[:, sl]], axis=0).astype(BF16))
        valid = _attn_mask(first)
        heads = range(A_HEADS)
        qm = [_head_operand(qn[j // 2], j).astype(BF16) for j in heads]
        s = [jnp.where(valid, _dot_nt(qm[j], k2[j // 6]) * (A_DH ** -0.5), -jnp.inf) for j in heads]
        m = [jnp.maximum(jnp.max(s[j], axis=-1, keepdims=True), sink_ref[j]) for j in heads]
        e = [jnp.exp(s[j] - m[j]) for j in heads]
        den = [jnp.sum(e[j], axis=-1, keepdims=True) + jnp.exp(sink_ref[j] - m[j]) for j in heads]
        outs = [_head_result(_dot((e[j] * (1.0 / den[j])).astype(BF16), v2[j // 6]), j) for j in heads]
        lse = jnp.zeros((ABLK, LANES), F32)
        for j in heads:
            lse = jnp.where(_lane(lse.shape) == j, m[j] + jnp.log(den[j]), lse)
        o = jnp.concatenate([outs[2 * b] + outs[2 * b + 1] for b in range(A_W // LANES)], axis=1)
        za = za_ref[...]
        o_ref[...] = o
        lse_ref[...] = lse
        y_ref[...] = (o * (za * _sigmoid(za))).astype(BF16)

    tab_specs = [BS((ABLK, LANES), cur)] * 3 + [BS((ABLK, LANES), prv)] * 3
    return pl.pallas_call(
        body, name="attn_fwd", grid=(nseq, nb),
        in_specs=[BS((ABLK, A_W), colblk(cur, A_W, P_Q)), BS((ABLK, A_W), colblk(cur, A_W, P_ZA)),
                  BS((ABLK, A_KVW), colblk(cur, A_KVW, P_K)), BS((ABLK, A_KVW), colblk(cur, A_KVW, P_V)),
                  BS((ABLK, A_KVW), colblk(prv, A_KVW, P_K)), BS((ABLK, A_KVW), colblk(prv, A_KVW, P_V))]
        + tab_specs + [BS((1, LANES), lambda b, i: (0, 0))] * 2 + [BS(memory_space=pltpu.SMEM)],
        out_specs=[BS((ABLK, A_W), colblk(cur, A_W, Y_A)), BS((ABLK, A_W), cur), BS((ABLK, LANES), cur)],
        out_shape=[S((n, D_MODEL), BF16), S((n, A_W), F32), S((n, LANES), F32)],
        compiler_params=_cp("arbitrary", "arbitrary"),
    )(p, p, p, p, p, p, *tabs, *tabs, qw, kw, sinks)


def _attn_bwd(p, dy, o, lse, tabs, qw, kw, sinks, nseq):
    n = p.shape[0]
    nb = n // nseq // ABLK
    cur = lambda b, i: (b * nb + jnp.minimum(i, nb - 1), 0)
    prv = lambda b, i: (b * nb + jnp.maximum(i - 1, 0), 0)
    colblk = lambda f, w, off: (lambda b, i: (f(b, i)[0], off // w))

    def body(q_ref, za_ref, kc_ref, vc_ref, kp_ref, vp_ref, dy_ref, o_ref, lse_ref,
             c_ref, s1_ref, s2_ref, cp_ref, s1p_ref, s2p_ref, qw_ref, kw_ref, sink_ref,
             dqza_ref, dkv_ref, dqw_ref, dkw_ref, dsk_ref, tk_ref, tv_ref, ck_ref, cv_ref):
        i = pl.program_id(1)
        first = i == 0
        tc = (c_ref[...], s1_ref[...], s2_ref[...])
        tp = (cp_ref[...], s1p_ref[...], s2p_ref[...])
        nkb = A_KVW // LANES

        @pl.when((pl.program_id(0) == 0) & first)
        def _():
            dqw_ref[...] = jnp.zeros_like(dqw_ref)
            dkw_ref[...] = jnp.zeros_like(dkw_ref)
            dsk_ref[...] = jnp.zeros_like(dsk_ref)

        @pl.when(i < nb)
        def _():
            q, kc, kp = q_ref[...], kc_ref[...], kp_ref[...]
            qn, rq = [], []
            for b in range(A_W // LANES):
                a, r = _normrope(q[:, LANES * b:LANES * (b + 1)], qw_ref[...], *tc)
                qn.append(a)
                rq.append(r)
            k2, v2 = [], []
            for b in range(nkb):
                sl = slice(LANES * b, LANES * (b + 1))
                k2.append(jnp.concatenate([_normrope(kp[:, sl], kw_ref[...], *tp)[0],
                                           _normrope(kc[:, sl], kw_ref[...], *tc)[0]], axis=0).astype(BF16))
                v2.append(jnp.concatenate([vp_ref[:, sl], vc_ref[:, sl]], axis=0).astype(BF16))
            valid = _attn_mask(first)
            za, dy, o, lse = za_ref[...], dy_ref[...], o_ref[...], lse_ref[...]
            sg = _sigmoid(za)
            do = dy * za * sg
            dqza_ref[:, A_W:2 * A_W] = (dy * o * _dsilu(za, sg)).astype(BF16)
            heads = range(A_HEADS)
            blk = lambda x, b: x[:, LANES * b:LANES * (b + 1)]
            qm = [_head_operand(qn[j // 2], j).astype(BF16) for j in heads]
            lj = [_col(lse, j) for j in heads]
            pr = [jnp.exp(jnp.where(valid, _dot_nt(qm[j], k2[j // 6]) * (A_DH ** -0.5), -jnp.inf) - lj[j]) for j in heads]
            dom = [_head_operand(blk(do, j // 2), j).astype(BF16) for j in heads]
            delta = [jnp.sum(_keep_half(blk(do, j // 2) * blk(o, j // 2), j % 2), axis=-1, keepdims=True) for j in heads]
            ds = [(pr[j] * (_dot_nt(dom[j], v2[j // 6]) - delta[j]) * (A_DH ** -0.5)).astype(BF16) for j in heads]
            dqh = [_head_result(_dot(ds[j], k2[j // 6]), j) for j in heads]
            dkh = [_dot_tn(ds[j], qm[j]) for j in heads]
            dvh = [_dot_tn(pr[j].astype(BF16), dom[j]) for j in heads]
            per_blk = A_HEADS // nkb
            dks = [sum(dkh[per_blk * g + 1:per_blk * (g + 1)], dkh[per_blk * g]) for g in range(nkb)]
            dvs = [sum(dvh[per_blk * g + 1:per_blk * (g + 1)], dvh[per_blk * g]) for g in range(nkb)]
            dsk = jnp.zeros((ABLK, LANES), F32)
            for j in heads:
                dsk = dsk + jnp.where(_lane(dsk.shape) == j, -jnp.exp(sink_ref[j] - lj[j]) * delta[j], 0.0)
            dsk_ref[...] += _fold8(dsk)
            dqn = jnp.concatenate([dqh[2 * b] + dqh[2 * b + 1] for b in range(A_W // LANES)], axis=1)
            dqw = jnp.zeros((SUB, LANES), F32)
            dqo = []
            for b in range(A_W // LANES):
                sl = slice(LANES * b, LANES * (b + 1))
                dx, wt = _normrope_bwd(dqn[:, sl], q[:, sl], rq[b], qw_ref[...], *tc)
                dqo.append(dx)
                dqw = dqw + _fold8(wt)
            dqw_ref[...] += dqw
            dqza_ref[:, 0:A_W] = jnp.concatenate(dqo, axis=1).astype(BF16)
            tk_ref[...] = jnp.concatenate(dks, axis=1)
            tv_ref[...] = jnp.concatenate(dvs, axis=1)

        @pl.when(i == nb)
        def _():
            tk_ref[...] = jnp.zeros_like(tk_ref)
            tv_ref[...] = jnp.zeros_like(tv_ref)

        @pl.when(i > 0)
        def _():
            kp = kp_ref[...]
            dkn = ck_ref[...] + tk_ref[0:ABLK, :]
            dkw = jnp.zeros((SUB, LANES), F32)
            dko = []
            for b in range(nkb):
                sl = slice(LANES * b, LANES * (b + 1))
                r = _normrope(kp[:, sl], kw_ref[...], *tp)[1]
                dx, wt = _normrope_bwd(dkn[:, sl], kp[:, sl], r, kw_ref[...], *tp)
                dko.append(dx)
                dkw = dkw + _fold8(wt)
            dkw_ref[...] += dkw
            dkv_ref[:, 0:A_KVW] = jnp.concatenate(dko, axis=1).astype(BF16)
            dkv_ref[:, A_KVW:2 * A_KVW] = (cv_ref[...] + tv_ref[0:ABLK, :]).astype(BF16)

        ck_ref[...] = tk_ref[ABLK:2 * ABLK, :]
        cv_ref[...] = tv_ref[ABLK:2 * ABLK, :]

    tab_specs = [BS((ABLK, LANES), cur)] * 3 + [BS((ABLK, LANES), prv)] * 3
    acc = BS((SUB, LANES), lambda b, i: (0, 0))
    return pl.pallas_call(
        body, name="attn_bwd", grid=(nseq, nb + 1),
        in_specs=[BS((ABLK, A_W), colblk(cur, A_W, P_Q)), BS((ABLK, A_W), colblk(cur, A_W, P_ZA)),
                  BS((ABLK, A_KVW), colblk(cur, A_KVW, P_K)), BS((ABLK, A_KVW), colblk(cur, A_KVW, P_V)),
                  BS((ABLK, A_KVW), colblk(prv, A_KVW, P_K)), BS((ABLK, A_KVW), colblk(prv, A_KVW, P_V)),
                  BS((ABLK, A_W), colblk(cur, A_W, Y_A)), BS((ABLK, A_W), cur), BS((ABLK, LANES), cur)]
        + tab_specs + [BS((1, LANES), lambda b, i: (0, 0))] * 2 + [BS(memory_space=pltpu.SMEM)],
        out_specs=[BS((ABLK, 2 * A_W), cur), BS((ABLK, 2 * A_KVW), prv), acc, acc, acc],
        out_shape=[S((n, P_W), BF16), S((n, 2 * A_KVW), BF16)] + [S((SUB, LANES), F32)] * 3,
        scratch_shapes=[pltpu.VMEM((2 * ABLK, A_KVW), F32)] * 2 + [pltpu.VMEM((ABLK, A_KVW), F32)] * 2,
        compiler_params=_cp("arbitrary", "arbitrary"),
    )(p, p, p, p, p, p, dy, o, lse, *tabs, *tabs, qw, kw, sinks)


def _put_cols(dst, src, col_off, after, tm=512):
    n, w = src.shape

    def body(s_ref, d_in_ref, after_ref, d_ref):
        d_ref[...] = s_ref[...]

    return pl.pallas_call(
        body, name="put_cols", grid=(n // tm,),
        in_specs=[BS((tm, w), lambda i: (i, 0)), BS(memory_space=pl.ANY), BS(memory_space=pl.ANY)],
        out_specs=BS((tm, w), lambda i: (i, col_off // w)),
        out_shape=S(dst.shape, dst.dtype), input_output_aliases={1: 0},
        compiler_params=_cp("arbitrary"),
    )(src, dst, after)


HALO_B = 32


def _layernorm(hc, lnw, lnb):
    mu = jnp.mean(hc, axis=-1, keepdims=True)
    xc = hc - mu
    rstd = lax.rsqrt(jnp.mean(xc * xc, axis=-1, keepdims=True) + EPS)
    xhat = xc * rstd
    return xhat, rstd, xhat * lnw + lnb


def _shifted_copies(buf_ref, sh_ref):
    rows = sh_ref.shape[1]
    for b in range(1, SUB):
        sh_ref[b - 1] = buf_ref[pl.ds(b, rows), :]


def _rows_from(buf_ref, sh_ref, off, rows, cols=slice(None)):
    a, b = divmod(off, SUB)
    if b == 0:
        return buf_ref[pl.ds(SUB * a, rows), cols]
    return sh_ref[b - 1, pl.ds(SUB * a, rows), cols]


def _conf_fwd(p, y, cw, cb, lnw, lnb, pw, pwb, nseq, tm=256):
    n = p.shape[0]
    t = n // nseq
    nt = t // tm
    row = lambda b, i: b * nt + i
    halo = lambda b, i: jnp.maximum((b * t + i * tm) // HALO_B - 1, 0)
    vec = BS((1, B_W), lambda b, i: (0, 0))

    def body(ub_ref, uh_ref, zb_ref, cw_ref, cb_ref, lnw_ref, lnb_ref, pw_ref, pwb_ref, y_in_ref, y_ref, hc_ref, buf_ref, sh_ref):
        ub, uh = ub_ref[...], uh_ref[...]
        hh = uh[:, :B_W] * _sigmoid(uh[:, B_W:])
        buf_ref[0:HALO_B, :] = jnp.where(pl.program_id(1) > 0, hh, 0.0)
        buf_ref[HALO_B:, :] = ub[:, :B_W] * _sigmoid(ub[:, B_W:])
        _shifted_copies(buf_ref, sh_ref)
        hc = jnp.zeros((tm, B_W), F32) + cb_ref[...]
        for k in range(B_K):
            hc = hc + cw_ref[k:k + 1, :] * _rows_from(buf_ref, sh_ref, HALO_B - B_K + 1 + k, tm)
        hc_ref[...] = hc
        ln = _layernorm(hc, lnw_ref[...], lnb_ref[...])[2]
        sw = ln * _sigmoid(ln)
        ob = _dot(sw.astype(BF16), pw_ref[...]) + pwb_ref[...]
        zb = zb_ref[...]
        y_ref[...] = (ob * (zb * _sigmoid(zb))).astype(BF16)

    return pl.pallas_call(
        body, name="conf_fwd", grid=(nseq, nt),
        in_specs=[BS((tm, 2 * B_W), lambda b, i: (row(b, i), P_UB // (2 * B_W))),
                  BS((HALO_B, 2 * B_W), lambda b, i: (halo(b, i), P_UB // (2 * B_W))),
                  BS((tm, B_W), lambda b, i: (row(b, i), P_ZB // B_W)),
                  BS((HALO_B, B_W), lambda b, i: (0, 0)), vec, vec, vec, BS((B_W, B_W), lambda b, i: (0, 0)), vec,
                  BS(memory_space=pl.ANY)],
        out_specs=[BS((tm, B_W), lambda b, i: (row(b, i), Y_B // B_W)), BS((tm, B_W), lambda b, i: (row(b, i), 0))],
        out_shape=[S(y.shape, y.dtype), S((n, B_W), F32)], input_output_aliases={9: 0},
        scratch_shapes=[pltpu.VMEM((HALO_B + tm, B_W), F32), pltpu.VMEM((SUB - 1, HALO_B + tm - SUB, B_W), F32)],
        compiler_params=_cp("arbitrary", "arbitrary"),
    )(p, p, p, cw, cb, lnw, lnb, pw, pwb, y)


def _conf_bwd1(p, dy, dp, hc, lnw, lnb, pw, pwb, tm=256):
    n = p.shape[0]
    vec = BS((1, B_W), lambda i: (0, 0))
    acc = BS((SUB, B_W), lambda i: (0, 0))

    def body(dy_ref, zb_ref, hc_ref, lnw_ref, lnb_ref, pw_ref, pwb_ref, dp_in_ref,
             dzb_ref, dhc_ref, dpw_ref, dpwb_ref, dlnw_ref, dlnb_ref, dcb_ref):
        @pl.when(pl.program_id(0) == 0)
        def _():
            for r in (dpw_ref, dpwb_ref, dlnw_ref, dlnb_ref, dcb_ref):
                r[...] = jnp.zeros_like(r)

        xhat, rstd, ln = _layernorm(hc_ref[...], lnw_ref[...], lnb_ref[...])
        sgl = _sigmoid(ln)
        sw = (ln * sgl).astype(BF16)
        ob = _dot(sw, pw_ref[...]) + pwb_ref[...]
        dy, zb = dy_ref[...], zb_ref[...]
        sgz = _sigmoid(zb)
        dzb_ref[...] = (dy * ob * _dsilu(zb, sgz)).astype(BF16)
        dob = dy * zb * sgz
        dobb = dob.astype(BF16)
        dpwb_ref[...] += _fold8(dob)
        dpw_ref[...] += _dot_tn(sw, dobb)
        dln = _dot_nt(dobb, pw_ref[...]) * _dsilu(ln, sgl)
        dlnw_ref[...] += _fold8(dln * xhat)
        dlnb_ref[...] += _fold8(dln)
        dxh = dln * lnw_ref[...]
        dhc = rstd * (dxh - jnp.mean(dxh, axis=-1, keepdims=True) - xhat * jnp.mean(dxh * xhat, axis=-1, keepdims=True))
        dcb_ref[...] += _fold8(dhc)
        dhc_ref[...] = dhc

    return pl.pallas_call(
        body, name="conf_bwd1", grid=(n // tm,),
        in_specs=[BS((tm, B_W), lambda i: (i, Y_B // B_W)), BS((tm, B_W), lambda i: (i, P_ZB // B_W)),
                  BS((tm, B_W), lambda i: (i, 0)), vec, vec, BS((B_W, B_W), lambda i: (0, 0)), vec,
                  BS(memory_space=pl.ANY)],
        out_specs=[BS((tm, B_W), lambda i: (i, P_ZB // B_W)), BS((tm, B_W), lambda i: (i, 0)),
                   BS((B_W, B_W), lambda i: (0, 0)), acc, acc, acc, acc],
        out_shape=[S(dp.shape, dp.dtype), S((n, B_W), F32), S((B_W, B_W), F32)] + [S((SUB, B_W), F32)] * 4,
        input_output_aliases={7: 0},
        compiler_params=_cp("arbitrary"),
    )(dy, p, hc, lnw, lnb, pw, pwb, dp)


def _conf_bwd2(p, dhc, dp, cw, nseq, tm=256):
    n = p.shape[0]
    t = n // nseq
    nt = t // tm
    row = lambda b, i: b * nt + i
    prev = lambda b, i: jnp.maximum((b * t + i * tm) // HALO_B - 1, 0)
    nxt = lambda b, i: jnp.minimum((b * t + (i + 1) * tm) // HALO_B, n // HALO_B - 1)

    def body(ub_ref, uh_ref, dh_ref, dn_ref, cw_ref, dp_in_ref, dub_ref, dcw_ref, buf_ref, dbuf_ref, sh_ref, dsh_ref):
        i = pl.program_id(1)

        @pl.when((pl.program_id(0) == 0) & (i == 0))
        def _():
            dcw_ref[...] = jnp.zeros_like(dcw_ref)

        uh = uh_ref[...]
        buf_ref[0:HALO_B, :] = jnp.where(i > 0, uh[:, :B_W] * _sigmoid(uh[:, B_W:]), 0.0)
        buf_ref[HALO_B:, :] = ub_ref[:, :B_W] * _sigmoid(ub_ref[:, B_W:])
        dbuf_ref[0:tm, :] = dh_ref[...]
        dbuf_ref[tm:, :] = jnp.where(i < nt - 1, dn_ref[...], 0.0)
        _shifted_copies(buf_ref, sh_ref)
        _shifted_copies(dbuf_ref, dsh_ref)
        for c in range(B_W // LANES):
            cs, gs = slice(LANES * c, LANES * (c + 1)), slice(B_W + LANES * c, B_W + LANES * (c + 1))
            for r0 in range(0, tm, LANES):
                dhc = dh_ref[r0:r0 + LANES, cs]
                dhg = jnp.zeros((LANES, LANES), F32)
                for k in range(B_K):
                    dhg = dhg + cw_ref[k:k + 1, cs] * _rows_from(dbuf_ref, dsh_ref, r0 + B_K - 1 - k, LANES, cs)
                    dcw_ref[SUB * k:SUB * (k + 1), cs] += _fold8(
                        dhc * _rows_from(buf_ref, sh_ref, r0 + HALO_B - B_K + 1 + k, LANES, cs))
                a, sg = ub_ref[r0:r0 + LANES, cs], _sigmoid(ub_ref[r0:r0 + LANES, gs])
                dub_ref[r0:r0 + LANES, cs] = (dhg * sg).astype(BF16)
                dub_ref[r0:r0 + LANES, gs] = (dhg * a * sg * (1.0 - sg)).astype(BF16)

    return pl.pallas_call(
        body, name="conf_bwd2", grid=(nseq, nt),
        in_specs=[BS((tm, 2 * B_W), lambda b, i: (row(b, i), P_UB // (2 * B_W))),
                  BS((HALO_B, 2 * B_W), lambda b, i: (prev(b, i), P_UB // (2 * B_W))),
                  BS((tm, B_W), lambda b, i: (row(b, i), 0)), BS((HALO_B, B_W), lambda b, i: (nxt(b, i), 0)),
                  BS((HALO_B, B_W), lambda b, i: (0, 0)), BS(memory_space=pl.ANY)],
        out_specs=[BS((tm, 2 * B_W), lambda b, i: (row(b, i), P_UB // (2 * B_W))),
                   BS((SUB * B_K, B_W), lambda b, i: (0, 0))],
        out_shape=[S(dp.shape, dp.dtype), S((SUB * B_K, B_W), F32)], input_output_aliases={5: 0},
        scratch_shapes=[pltpu.VMEM((HALO_B + tm, B_W), F32)] * 2 + [pltpu.VMEM((SUB - 1, HALO_B + tm - SUB, B_W), F32)] * 2,
        compiler_params=_cp("arbitrary", "arbitrary"),
    )(p, p, dhc, dhc, cw, dp)


HALO_C = 8
QS = C_DH ** -0.5
NCB = 3 * C_HEADS
CB0 = P_QKV // LANES
ZC0 = P_ZC // LANES
GB, GG = 0, C_HEADS


def _softplus(z):
    return jnp.maximum(z, 0.0) + jnp.log(1.0 + jnp.exp(-jnp.abs(z)))


def _gdn_gates_fwd(p, alog_l, dtb_l, tm=256):
    n = p.shape[0]

    def body(ba_ref, al_ref, db_ref, o_ref):
        blk = ba_ref[...]
        lane = _lane(blk.shape)
        g = jnp.where((lane >= GG) & (lane < GG + C_HEADS), -jnp.exp(al_ref[...]) * _softplus(blk + db_ref[...]), 0.0)
        tri = (_subl((CHUNK, CHUNK)) >= _lane((CHUNK, CHUNK))).astype(F32)
        gc = jnp.concatenate([_dot(tri, g[CHUNK * c:CHUNK * (c + 1)], HI) for c in range(tm // CHUNK)], axis=0)
        o_ref[...] = jnp.where(lane < GG, _sigmoid(blk), gc)

    return pl.pallas_call(
        body, name="gdn_gates_fwd", grid=(n // tm,),
        in_specs=[BS((tm, LANES), lambda i: (i, P_BA // LANES)), BS((1, LANES), lambda i: (0, 0)), BS((1, LANES), lambda i: (0, 0))],
        out_specs=BS((tm, LANES), lambda i: (i, 0)), out_shape=S((n, LANES), F32),
        compiler_params=_cp("arbitrary"),
    )(p, alog_l, dtb_l)


def _gdn_pre_fwd(p, ccw, nseq, tm=256):
    n = p.shape[0]
    t = n // nseq
    nt = t // tm
    row = lambda b, i: b * nt + i
    halo = lambda b, i: jnp.maximum((b * t + i * tm) // HALO_C - 1, 0)

    def body(x_ref, xh_ref, w_ref, xc_ref, o_ref, buf_ref):
        buf_ref[0:HALO_C, :] = jnp.where(pl.program_id(1) > 0, xh_ref[...], 0.0)
        buf_ref[HALO_C:, :] = x_ref[...]
        for c in range(NCB):
            cs = slice(LANES * c, LANES * (c + 1))
            xc = jnp.zeros((tm, LANES), F32)
            for k in range(C_K):
                xc = xc + w_ref[k:k + 1, cs] * buf_ref[pl.ds(HALO_C - C_K + 1 + k, tm), cs]
            xc_ref[:, cs] = xc
            act = xc * _sigmoid(xc)
            if c < 2 * C_HEADS:
                act = act * (lax.rsqrt(jnp.sum(act * act, axis=-1, keepdims=True) + EPS) * (QS if c < C_HEADS else 1.0))
            o_ref[:, cs] = act

    wide = 3 * C_W
    return pl.pallas_call(
        body, name="gdn_pre_fwd", grid=(nseq, nt),
        in_specs=[BS((tm, wide), lambda b, i: (row(b, i), P_QKV // wide)), BS((HALO_C, wide), lambda b, i: (halo(b, i), P_QKV // wide)),
                  BS((SUB, wide), lambda b, i: (0, 0))],
        out_specs=[BS((tm, wide), lambda b, i: (row(b, i), 0))] * 2,
        out_shape=[S((n, wide), F32)] * 2,
        scratch_shapes=[pltpu.VMEM((HALO_C + tm, wide), F32)],
        compiler_params=_cp("arbitrary", "arbitrary"),
    )(p, p, ccw)


def _chunk_common(q, k, gt, gtt, h):
    beta = _col(gt, GB + h)
    gc = _col(gt, GG + h)
    gcr = gtt[GG + h:GG + h + 1, :]
    ii, jj = _subl((CHUNK, CHUNK)), _lane((CHUNK, CHUNK))
    incl, strict = ii >= jj, ii > jj
    dec = jnp.exp(jnp.where(incl, gc - gcr, -jnp.inf))
    kb = k * beta
    kbf = k.astype(BF16)
    a = jnp.where(strict, _dot_nt(kb.astype(BF16), kbf) * dec, 0.0)
    mq = jnp.where(incl, _dot_nt(q.astype(BF16), kbf) * dec, 0.0)
    glast = jnp.sum(jnp.where(_subl(gc.shape) == CHUNK - 1, gc, 0.0), axis=0, keepdims=True)
    return beta, gc, incl, strict, dec, kb, a, mq, glast


def _split(x):
    hi = x.astype(BF16)
    return hi, (x - hi.astype(F32)).astype(BF16)


def _dot3(dot, a, b):
    (ah, al), (bh, bl) = a, b
    return dot(ah, bh) + (dot(ah, bl) + dot(al, bh))


def _unit_lower_inverses(mats):
    eye = (_subl(mats[0].shape) == _lane(mats[0].shape)).astype(F32)
    ms = [-a for a in mats]
    invs = [eye + m for m in ms]
    parts = [_split(m) for m in ms]
    for _ in range(5):
        ms = [_dot3(_dot, s, s) for s in parts]
        parts = [_split(m) for m in ms]
        invs = [inv + _dot3(_dot, _split(inv), s) for inv, s in zip(invs, parts)]
    return invs


def _gdn_chunk_fwd(qkv, gates, p, y, onw, nseq, tt=512):
    n = qkv.shape[0]
    t = n // nseq
    tt = min(tt, t)
    nt = t // tt
    nch = tt // CHUNK

    def body(q_ref, k_ref, v_ref, g_ref, zc_ref, onw_ref, y_in_ref, y_ref, o_ref, u_ref, w_ref, t_ref, ss_ref, s_scr):
        @pl.when(pl.program_id(1) == 0)
        def _():
            s_scr[...] = jnp.zeros_like(s_scr)

        def step(c, carry):
            rows = pl.ds(pl.multiple_of(c * CHUNK, CHUNK), CHUNK)
            gt = g_ref[rows, :]
            gtt = gt.T
            heads = range(C_HEADS)
            hs = [slice(C_DH * h, C_DH * (h + 1)) for h in heads]
            q, k, v = ([r[rows, hs[h]] for h in heads] for r in (q_ref, k_ref, v_ref))
            cm = [_chunk_common(q[h], k[h], gt, gtt, h) for h in heads]
            beta, gc, kb, mq, glast = ([m[i] for m in cm] for i in (0, 1, 5, 7, 8))
            tinv = _unit_lower_inverses([m[6] for m in cm])
            egc = [jnp.exp(g) for g in gc]
            sol = [_dot3(_dot, _split(tinv[h]), _split(jnp.concatenate([v[h] * beta[h], kb[h] * egc[h]], axis=1))) for h in heads]
            sv = [s_scr[h] for h in heads]
            sb = [s.astype(BF16) for s in sv]
            vnb = [(sol[h][:, :C_DH] - _dot(sol[h][:, C_DH:].astype(BF16), sb[h])).astype(BF16) for h in heads]
            o = [_dot((q[h] * egc[h]).astype(BF16), sb[h]) + _dot(mq[h].astype(BF16), vnb[h]) for h in heads]
            for h in heads:
                ss_ref[h, c] = sv[h]
                s_scr[h] = sv[h] * jnp.exp(glast[h]) + _dot_tn((k[h] * jnp.exp(glast[h] - gc[h])).astype(BF16), vnb[h])
            for h in heads:
                o_ref[rows, hs[h]] = o[h]
                u_ref[rows, hs[h]] = sol[h][:, :C_DH]
                w_ref[rows, hs[h]] = sol[h][:, C_DH:]
                t_ref[rows, hs[h]] = jnp.concatenate([tinv[h], jnp.zeros_like(tinv[h])], axis=1)
                zc = zc_ref[rows, hs[h]]
                r = lax.rsqrt(jnp.mean(o[h] * o[h], axis=-1, keepdims=True) + EPS)
                y_ref[rows, hs[h]] = (o[h] * r * onw_ref[...] * (zc * _sigmoid(zc))).astype(BF16)
            return carry

        lax.fori_loop(0, nch, step, 0, unroll=4 if nch % 4 == 0 else 1)

    row = lambda b, i: b * nt + i
    wb = lambda col: BS((tt, C_W), lambda b, i: (row(b, i), col))
    return pl.pallas_call(
        body, name="gdn_chunk_fwd", grid=(nseq, nt),
        in_specs=[wb(0), wb(1), wb(2), BS((tt, LANES), lambda b, i: (row(b, i), 0)), wb(P_ZC // C_W),
                  BS((1, LANES), lambda b, i: (0, 0)), BS(memory_space=pl.ANY)],
        out_specs=[wb(Y_C // C_W), wb(0), wb(0), wb(0), wb(0),
                   BS((None, C_HEADS, nch, C_DH, C_DH), lambda b, i: (b, 0, i, 0, 0))],
        out_shape=[S(y.shape, y.dtype)] + [S((n, C_W), F32)] * 4 + [S((nseq, C_HEADS, t // CHUNK, C_DH, C_DH), F32)],
        input_output_aliases={6: 0},
        scratch_shapes=[pltpu.VMEM((C_HEADS, C_DH, C_DH), F32)],
        compiler_params=_cp("arbitrary", "arbitrary"),
    )(qkv, qkv, qkv, gates, p, onw, y)


def _gdn_chunk_bwd(qkv, gates, p, dy, dp, onw, o, u, w, tinv, ss, nseq, tt=256):
    n = qkv.shape[0]
    t = n // nseq
    tt = min(tt, t)
    nt = t // tt
    nch = tt // CHUNK

    def body(q_ref, k_ref, v_ref, g_ref, zc_ref, onw_ref, o_ref, dy_ref, u_ref, w_ref, t_ref, ss_ref, dp_in_ref,
             dzc_ref, dqkv_ref, dg_ref, donw_ref, ds_scr):
        @pl.when(pl.program_id(1) == 0)
        def _():
            ds_scr[...] = jnp.zeros_like(ds_scr)

        @pl.when((pl.program_id(0) == 0) & (pl.program_id(1) == 0))
        def _():
            donw_ref[...] = jnp.zeros_like(donw_ref)

        def rsum(x):
            return jnp.sum(x, axis=-1, keepdims=True)

        def step(ci, carry):
            c = nch - 1 - ci
            rows = pl.ds(pl.multiple_of(c * CHUNK, CHUNK), CHUNK)
            gt = g_ref[rows, :]
            gtt = gt.T
            live = [head(c, rows, gt, gtt, h) for h in range(C_HEADS)]
            while live:
                live = [g for g in live if next(g, False)]
            return carry

        def head(c, rows, gt, gtt, h):
            hs = slice(C_DH * h, C_DH * (h + 1))
            q, k, v = q_ref[rows, hs], k_ref[rows, hs], v_ref[rows, hs]
            zc, o, dy, u, w = zc_ref[rows, hs], o_ref[rows, hs], dy_ref[rows, hs], u_ref[rows, hs], w_ref[rows, hs]
            tm_ = t_ref[rows, hs][:, 0:CHUNK]
            sv, dsv = ss_ref[h, c], ds_scr[h]
            sb, dsb = sv.astype(BF16), dsv.astype(BF16)
            sg = _sigmoid(zc)
            r = lax.rsqrt(jnp.mean(o * o, axis=-1, keepdims=True) + EPS)
            on = o * r
            ow = onw_ref[...]
            dzc_ref[rows, hs] = (dy * on * ow * _dsilu(zc, sg)).astype(BF16)
            t1 = dy * zc * sg
            donw_ref[...] += _fold8(t1 * on)
            don = t1 * ow
            do = r * (don - on * jnp.mean(don * on, axis=-1, keepdims=True))
            dob = do.astype(BF16)
            yield True
            beta, gc, incl, strict, dec, kb, a, mq, glast = _chunk_common(q, k, gt, gtt, h)
            egc = jnp.exp(gc)
            gl = jnp.exp(glast)
            ekd = jnp.exp(glast - gc)
            wb = w.astype(BF16)
            vnb = (u - _dot(wb, sb)).astype(BF16)
            qg = q * egc
            yield True
            dvn = _dot_tn(mq.astype(BF16), dob) + _dot((k * ekd).astype(BF16), dsb)
            dvnb = dvn.astype(BF16)
            dqg = _dot_nt(dob, sb)
            yield True
            dmq = jnp.where(incl, _dot_nt(dob, vnb), 0.0)
            dkd = _dot_nt(vnb, dsb)
            dgl = jnp.sum(rsum(dsv * sv), axis=0, keepdims=True)
            dw = -_dot_nt(dvnb, sb)
            yield True
            ds_scr[h] = gl * dsv + _dot_tn(qg.astype(BF16), dob) - _dot_tn(wb, dvnb)
            db = _dot3(_dot_tn, _split(tm_), _split(jnp.concatenate([dvn, dw], axis=1)))
            dbv, dbk = db[:, :C_DH], db[:, C_DH:]
            yield True
            da = -jnp.where(strict, _dot3(_dot_nt, _split(dbv), _split(u)) + _dot3(_dot_nt, _split(dbk), _split(w)), 0.0)
            yield True
            e = da * a + dmq * mq
            dgc = rsum(e) - rsum(e.T)
            dgb, dhb, kbf = (da * dec).astype(BF16), (dmq * dec).astype(BF16), k.astype(BF16)
            dkb = _dot(dgb, kbf)
            tk = rsum(dbk * k)
            rk = rsum(dkd * k) * ekd
            dq = _dot(dhb, kbf) + egc * dqg
            dk = _dot_tn(dgb, kb.astype(BF16)) + _dot_tn(dhb, q.astype(BF16)) + beta * (egc * dbk + dkb) + ekd * dkd
            dbeta = rsum(dbv * v) + tk * egc + rsum(dkb * k)
            dgc = dgc + tk * beta * egc + egc * rsum(dqg * q) - rk
            dglast = jnp.sum(rk, axis=0, keepdims=True) + dgl * gl
            dgc = dgc + jnp.where(_subl(dgc.shape) == CHUNK - 1, dglast, 0.0)
            dqkv_ref[0, rows, hs] = dq
            dqkv_ref[1, rows, hs] = dk
            dqkv_ref[2, rows, hs] = beta * dbv
            lane = _lane((CHUNK, LANES))
            dg_ref[h, rows, :] = jnp.where(lane == 0, dbeta, jnp.where(lane == 1, dgc, 0.0))

        lax.fori_loop(0, nch, step, 0)

    row = lambda b, i: b * nt + nt - 1 - i
    wb = lambda col: BS((tt, C_W), lambda b, i: (row(b, i), col))
    return pl.pallas_call(
        body, name="gdn_chunk_bwd", grid=(nseq, nt),
        in_specs=[wb(0), wb(1), wb(2), BS((tt, LANES), lambda b, i: (row(b, i), 0)), wb(P_ZC // C_W),
                  BS((1, LANES), lambda b, i: (0, 0)), wb(0), wb(Y_C // C_W), wb(0), wb(0), wb(0),
                  BS((None, C_HEADS, nch, C_DH, C_DH), lambda b, i: (b, 0, nt - 1 - i, 0, 0)), BS(memory_space=pl.ANY)],
        out_specs=[wb(P_ZC // C_W), BS((3, tt, C_W), lambda b, i: (0, row(b, i), 0)),
                   BS((C_HEADS, tt, LANES), lambda b, i: (0, row(b, i), 0)), BS((SUB, LANES), lambda b, i: (0, 0))],
        out_shape=[S(dp.shape, dp.dtype), S((3, n, C_W), F32), S((C_HEADS, n, LANES), F32), S((SUB, LANES), F32)],
        input_output_aliases={12: 0},
        scratch_shapes=[pltpu.VMEM((C_HEADS, C_DH, C_DH), F32)],
        compiler_params=_cp("arbitrary", "arbitrary"),
    )(qkv, qkv, qkv, gates, p, onw, o, dy, u, w, tinv, ss, dp)


def _gdn_gates_bwd(dgate, p, alog_l, dtb_l, dp, tm=256):
    n = p.shape[0]
    acc = BS((SUB, LANES), lambda i: (0, 0))

    def body(dg_ref, ba_ref, al_ref, db_ref, dp_in_ref, dba_ref, dal_ref, ddb_ref):
        @pl.when(pl.program_id(0) == 0)
        def _():
            dal_ref[...] = jnp.zeros_like(dal_ref)
            ddb_ref[...] = jnp.zeros_like(ddb_ref)

        blk = ba_ref[...]
        lane = _lane(blk.shape)
        dbeta = jnp.zeros_like(blk)
        dgc = jnp.zeros_like(blk)
        for h in range(C_HEADS):
            dbeta = dbeta + jnp.where(lane == GB + h, _col(dg_ref[h], 0), 0.0)
            dgc = dgc + jnp.where(lane == GG + h, _col(dg_ref[h], 1), 0.0)
        tri = (_subl((CHUNK, CHUNK)) <= _lane((CHUNK, CHUNK))).astype(F32)
        dg = jnp.concatenate([_dot(tri, dgc[CHUNK * c:CHUNK * (c + 1)], HI) for c in range(tm // CHUNK)], axis=0)
        beta = _sigmoid(blk)
        z = blk + db_ref[...]
        ea = jnp.exp(al_ref[...])
        isg = (lane >= GG) & (lane < GG + C_HEADS)
        dz = jnp.where(isg, -dg * ea * _sigmoid(z), 0.0)
        dal_ref[...] += _fold8(jnp.where(isg, -dg * ea * _softplus(z), 0.0))
        ddb_ref[...] += _fold8(dz)
        out = jnp.where(lane < GG, dbeta * beta * (1.0 - beta), dz)
        dba_ref[...] = jnp.concatenate([out, jnp.zeros_like(out)], axis=1).astype(BF16)

    return pl.pallas_call(
        body, name="gdn_gates_bwd", grid=(n // tm,),
        in_specs=[BS((C_HEADS, tm, LANES), lambda i: (0, i, 0)), BS((tm, LANES), lambda i: (i, P_BA // LANES)),
                  BS((1, LANES), lambda i: (0, 0)), BS((1, LANES), lambda i: (0, 0)), BS(memory_space=pl.ANY)],
        out_specs=[BS((tm, 2 * LANES), lambda i: (i, P_BA // (2 * LANES))), acc, acc],
        out_shape=[S(dp.shape, dp.dtype), S((SUB, LANES), F32), S((SUB, LANES), F32)],
        input_output_aliases={4: 0},
        compiler_params=_cp("arbitrary"),
    )(dgate, p, alog_l, dtb_l, dp)


def _gdn_pre_bwd(p, dqkv, xc, dp, ccw, nseq, tm=256):
    n = p.shape[0]
    t = n // nseq
    nt = t // tm
    wide = 3 * C_W
    row = lambda b, i: b * nt + i
    prev = lambda b, i: jnp.maximum((b * t + i * tm) // HALO_C - 1, 0)
    nxt = lambda b, i: jnp.minimum((b * t + (i + 1) * tm) // HALO_C, n // HALO_C - 1)

    def d_conv_out(d, xc, part):
        sg = _sigmoid(xc)
        act = xc * sg
        if part < 2:
            cs = QS if part == 0 else 1.0
            rn = lax.rsqrt(jnp.sum(act * act, axis=-1, keepdims=True) + EPS)
            d = cs * rn * d - act * (cs * rn * rn * rn * jnp.sum(d * act, axis=-1, keepdims=True))
        return d * _dsilu(xc, sg)

    def body(x_ref, xh_ref, d_ref, dn_ref, xc_ref, xn_ref, w_ref, dp_in_ref, dx_ref, dw_ref, buf_ref, dbuf_ref):
        i = pl.program_id(1)

        @pl.when((pl.program_id(0) == 0) & (i == 0))
        def _():
            dw_ref[...] = jnp.zeros_like(dw_ref)

        buf_ref[0:HALO_C, :] = jnp.where(i > 0, xh_ref[...], 0.0)
        buf_ref[HALO_C:, :] = x_ref[...]
        for c in range(NCB):
            cs = slice(LANES * c, LANES * (c + 1))
            part, hd = divmod(c, C_HEADS)
            hs = slice(LANES * hd, LANES * (hd + 1))
            d = d_conv_out(d_ref[part, :, hs], xc_ref[:, cs], part)
            dbuf_ref[0:tm, cs] = d
            dbuf_ref[tm:, cs] = jnp.where(i < nt - 1, d_conv_out(dn_ref[part, :, hs], xn_ref[:, cs], part), 0.0)
            dx = jnp.zeros((tm, LANES), F32)
            for k in range(C_K):
                dx = dx + w_ref[k:k + 1, cs] * dbuf_ref[pl.ds(C_K - 1 - k, tm), cs]
                dw_ref[SUB * k:SUB * (k + 1), cs] += _fold8(d * buf_ref[pl.ds(HALO_C - C_K + 1 + k, tm), cs])
            dx_ref[:, cs] = dx.astype(BF16)

    return pl.pallas_call(
        body, name="gdn_pre_bwd", grid=(nseq, nt),
        in_specs=[BS((tm, wide), lambda b, i: (row(b, i), P_QKV // wide)), BS((HALO_C, wide), lambda b, i: (prev(b, i), P_QKV // wide)),
                  BS((3, tm, C_W), lambda b, i: (0, row(b, i), 0)), BS((3, HALO_C, C_W), lambda b, i: (0, nxt(b, i), 0)),
                  BS((tm, wide), lambda b, i: (row(b, i), 0)), BS((HALO_C, wide), lambda b, i: (nxt(b, i), 0)),
                  BS((SUB, wide), lambda b, i: (0, 0)), BS(memory_space=pl.ANY)],
        out_specs=[BS((tm, wide), lambda b, i: (row(b, i), P_QKV // wide)), BS((SUB * C_K, wide), lambda b, i: (0, 0))],
        out_shape=[S(dp.shape, dp.dtype), S((SUB * C_K, wide), F32)], input_output_aliases={7: 0},
        scratch_shapes=[pltpu.VMEM((HALO_C + tm, wide), F32)] * 2,
        compiler_params=_cp("arbitrary", "arbitrary"),
    )(p, p, dqkv, dqkv, xc, xc, ccw, dp)


ANY = BS(memory_space=pl.ANY)


def _my_pos():
    return lax.axis_index("x"), lax.axis_index("y"), lax.axis_index("c")


def _dev_index(dev):
    return 4 * dev[0] + 2 * dev[1] + dev[2]


def _all_gather(shards, after=None):
    nk = len(shards)

    tail = [] if after is None else [after]

    def body(*refs):
        ins, outs = refs[:nk], refs[nk + len(tail):2 * nk + len(tail)]
        send, recv, loc = refs[2 * nk + len(tail):]
        x, y, c = _my_pos()
        me, sib = (x, y, c), (x, y, 1 - c)
        chips = [(1 - x, y), (x, 1 - y), (1 - x, 1 - y)]

        def rows(t, dev):
            r = ins[t].shape[0]
            return outs[t].at[pl.ds(pl.multiple_of(_dev_index(dev) * r, SUB), r), :]

        def copy(t, k, block, to, src=None):
            return pltpu.make_async_remote_copy(
                src_ref=rows(t, block) if src is None else src, dst_ref=rows(t, block),
                send_sem=send.at[t, k], recv_sem=recv.at[t, k], device_id=to, device_id_type=MESH)

        mine = [pltpu.make_async_copy(ins[t], rows(t, me), loc.at[t]) for t in range(nk)]
        for cp in mine:
            cp.start()
        first = []
        for t in range(nk):
            first.append(copy(t, 0, me, sib, src=ins[t]))
            first += [copy(t, 1 + j, me, (*chip, c), src=ins[t]) for j, chip in enumerate(chips)]
        for cp in first:
            cp.start()
        passed = []
        for j, chip in enumerate(chips):
            for t in range(nk):
                copy(t, 1 + j, (*chip, c), me).wait_recv()
                cp = copy(t, 4 + j, (*chip, c), sib)
                cp.start()
                passed.append(cp)
        for t in range(nk):
            copy(t, 0, sib, me).wait_recv()
            for j, chip in enumerate(chips):
                copy(t, 4 + j, (*chip, 1 - c), me).wait_recv()
        for cp in first + passed:
            cp.wait_send()
        for cp in mine:
            cp.wait()

    return pl.pallas_call(
        body, name="all_gather", in_specs=[ANY] * (nk + len(tail)), out_specs=[ANY] * nk,
        out_shape=[S((N_DEV * a.shape[0], a.shape[1]), a.dtype) for a in shards],
        scratch_shapes=[pltpu.SemaphoreType.DMA((nk, 7)), pltpu.SemaphoreType.DMA((nk, 7)), pltpu.SemaphoreType.DMA((nk,))],
    )(*shards, *tail)


SEM = BS(memory_space=pltpu.SEMAPHORE)
HBM = BS(memory_space=pltpu.HBM)
EFFECT = pltpu.SideEffectType.DATAFLOW_SIDE_EFFECTING


def _peers(x, y, c):
    return [((1 - x) if k & 4 else x, (1 - y) if k & 2 else y, (1 - c) if k & 1 else c) for k in range(1, N_DEV)]


def _exchange_copy(kind, src, land, send, recv, t, k, peer, me, arriving):
    frm = peer if arriving else me
    if kind == "gather":
        r = src.shape[0]
        s_ref = src
        d_ref = land.at[pl.ds(pl.multiple_of(_dev_index(frm) * r, SUB), r), :]
    else:
        r = src.shape[0] // N_DEV
        s_ref = src.at[pl.ds(pl.multiple_of(_dev_index(peer) * r, SUB), r), :]
        d_ref = land.at[_dev_index(frm)]
    sem = t * (N_DEV - 1) + k
    return pltpu.make_async_remote_copy(src_ref=s_ref, dst_ref=d_ref, send_sem=send.at[sem], recv_sem=recv.at[sem],
                                        device_id=peer, device_id_type=MESH)


def _own_copy(kind, src, land, own, t, me):
    if kind == "gather":
        r = src.shape[0]
        return pltpu.make_async_copy(src, land.at[pl.ds(pl.multiple_of(_dev_index(me) * r, SUB), r), :], own.at[t])
    r = src.shape[0] // N_DEV
    return pltpu.make_async_copy(src.at[pl.ds(pl.multiple_of(_dev_index(me) * r, SUB), r), :], land.at[_dev_index(me)], own.at[t])


def _exchange_start(kind, srcs, after, name):
    nk = len(srcs)
    if kind == "gather":
        lands = [lax.empty((N_DEV * a.shape[0], a.shape[1]), a.dtype) for a in srcs]
    else:
        lands = [lax.empty((N_DEV, a.shape[0] // N_DEV, a.shape[1]), a.dtype) for a in srcs]

    def body(*refs):
        src, land = refs[:nk], refs[nk:2 * nk]
        send, recv, own = refs[2 * nk + 1], refs[2 * nk + 2], refs[2 * nk + 3]
        token = refs[-1]
        x, y, c = _my_pos()
        me = (x, y, c)
        for t in range(nk):
            _own_copy(kind, src[t], land[t], own, t, me).start()
            for k, peer in enumerate(_peers(x, y, c)):
                _exchange_copy(kind, src[t], land[t], send, recv, t, k, peer, me, False).start()
        token[...] = jnp.zeros_like(token)

    hbm = lambda a: pltpu.HBM(a.shape, a.dtype)
    out = pl.pallas_call(
        body, name=name,
        out_shape=(pltpu.SemaphoreType.DMA((nk * (N_DEV - 1),)), pltpu.SemaphoreType.DMA((nk * (N_DEV - 1),)),
                   pltpu.SemaphoreType.DMA((nk,)), *[hbm(a) for a in srcs], *[hbm(a) for a in lands], S((SUB, LANES), F32)),
        in_specs=[HBM] * (2 * nk) + [ANY],
        out_specs=(SEM, SEM, SEM, *[HBM] * (2 * nk), BS(memory_space=pltpu.VMEM)),
        input_output_aliases={i: 3 + i for i in range(2 * nk)},
        compiler_params=pltpu.CompilerParams(has_side_effects=EFFECT),
    )(*[pltpu.with_memory_space_constraint(a, pltpu.HBM) for a in (*srcs, *lands)], after)
    return dict(kind=kind, nk=nk, send=out[0], recv=out[1], own=out[2], srcs=out[3:3 + nk], lands=out[3 + nk:3 + 2 * nk],
                token=out[-1])


def _exchange_wait(ex, after, name):
    kind, nk = ex["kind"], ex["nk"]

    def body(*refs):
        src, land = refs[:nk], refs[nk:2 * nk]
        send, recv, own = refs[2 * nk], refs[2 * nk + 1], refs[2 * nk + 2]
        x, y, c = _my_pos()
        me = (x, y, c)
        for t in range(nk):
            _own_copy(kind, src[t], land[t], own, t, me).wait()
            for k, peer in enumerate(_peers(x, y, c)):
                _exchange_copy(kind, src[t], land[t], send, recv, t, k, peer, me, False).wait_send()
                _exchange_copy(kind, src[t], land[t], send, recv, t, k, peer, me, True).wait_recv()

    hbm = lambda a: pltpu.HBM(a.shape, a.dtype)
    out = pl.pallas_call(
        body, name=name,
        out_shape=(*[hbm(a) for a in ex["srcs"]], *[hbm(a) for a in ex["lands"]]),
        in_specs=[HBM] * (2 * nk) + [SEM, SEM, SEM, ANY], out_specs=tuple([HBM] * (2 * nk)),
        input_output_aliases={i: i for i in range(2 * nk)},
        compiler_params=pltpu.CompilerParams(has_side_effects=EFFECT),
    )(*ex["srcs"], *ex["lands"], ex["send"], ex["recv"], ex["own"], after)
    return list(out[nk:])


BLOCK_BYTES = 4 << 20


def _row_tile(rows, row_bytes, align):
    best = align
    for tr in range(align, rows + 1, align):
        if rows % tr == 0 and tr * row_bytes <= BLOCK_BYTES:
            best = tr
    return best


def _sum8(a):
    _, r, w = a.shape
    tr = _row_tile(r, N_DEV * w * a.dtype.itemsize, 32 // a.dtype.itemsize)

    def body(a_ref, o_ref):
        acc = a_ref[0].astype(F32)
        for d in range(1, N_DEV):
            acc = acc + a_ref[d].astype(F32)
        o_ref[...] = acc

    return pl.pallas_call(
        body, name="sum8", grid=(r // tr,), in_specs=[BS((N_DEV, tr, w), lambda i: (0, i, 0))],
        out_specs=BS((tr, w), lambda i: (i, 0)), out_shape=S((r, w), F32), compiler_params=_cp("arbitrary"),
    )(a)


def _adamw(w, g, m, v):
    r, c = w.shape
    tr = _row_tile(r, c * 4 * 2, SUB)

    def body(w_ref, g_ref, m_ref, v_ref, d_ref, mo_ref, vo_ref):
        d_ref[...], mo_ref[...], vo_ref[...] = _adam_update(w_ref[...], g_ref[...], m_ref[...], v_ref[...])

    blk = BS((tr, c), lambda i: (i, 0))
    return pl.pallas_call(
        body, name="adamw", grid=(r // tr,), in_specs=[blk] * 4, out_specs=[blk] * 3,
        out_shape=[S((r, c), F32)] * 3, compiler_params=_cp("arbitrary"),
    )(w, g, m, v)


def _sum8_t(a, tc=256):
    _, r, w = a.shape

    def body(a_ref, o_ref):
        acc = a_ref[0].astype(F32)
        for d in range(1, N_DEV):
            acc = acc + a_ref[d].astype(F32)
        o_ref[...] = acc.T

    return pl.pallas_call(
        body, name="sum8_t", grid=(w // tc,), in_specs=[BS((N_DEV, r, tc), lambda j: (0, 0, j))],
        out_specs=BS((tc, r), lambda j: (j, 0)), out_shape=S((w, r), F32), compiler_params=_cp("arbitrary"),
    )(a)


def _rows_view(a):
    nl, r, c = a.shape
    assert nl == 2
    return a.transpose(2, 0, 1).reshape(c, nl, r // LANES, LANES).transpose(0, 2, 1, 3).reshape(-1, LANES)


def _rows_view_back(a, shape):
    nl, r, c = shape
    return a.reshape(c, r // LANES, nl, LANES).transpose(0, 2, 1, 3).reshape(c, nl, r).transpose(1, 2, 0)


def _adamw_rows(w, g, m, v, tr=2048):
    n = w.shape[0]

    def body(w_ref, g_ref, m_ref, v_ref, d_ref, mo_ref, vo_ref):
        d_ref[...], mo_ref[...], vo_ref[...] = _adam_update(w_ref[...], g_ref[...], m_ref[...], v_ref[...])

    blk = BS((tr, LANES), lambda i: (i, 0))
    return pl.pallas_call(
        body, name="adamw_rows", grid=(pl.cdiv(n, tr),), in_specs=[blk] * 4, out_specs=[blk] * 3,
        out_shape=[S((n, LANES), F32)] * 3, compiler_params=_cp("arbitrary"),
    )(w, g, m, v)


def _adam_update(w, g, m, v):
    m2 = ADAM_B1 * m + (1.0 - ADAM_B1) * g
    v2 = ADAM_B2 * v + (1.0 - ADAM_B2) * (g * g)
    m_hat = m2 / (1.0 - ADAM_B1 ** ADAM_STEP)
    v_hat = v2 / (1.0 - ADAM_B2 ** ADAM_STEP)
    return -ADAM_LR * (m_hat / (jnp.sqrt(v_hat) + ADAM_EPS) + ADAM_WD * w), m2, v2


def _adamw_layer(w, g, m, v, l, prev):
    nl, r, c = w.shape
    tr = _row_tile(r, c * 4 * 2, SUB)

    def body(w_ref, g_ref, m_ref, v_ref, *refs):
        go_ref, d_ref, mo_ref, vo_ref = refs[-4:]
        gv = g_ref[...]
        go_ref[...] = gv
        d_ref[...], mo_ref[...], vo_ref[...] = _adam_update(w_ref[...], gv, m_ref[...], v_ref[...])

    slot = BS((None, tr, c), lambda i: (l, i, 0))
    keep = [] if prev is None else [ANY] * 4
    return pl.pallas_call(
        body, name="adamw_layer", grid=(r // tr,), in_specs=[slot, BS((tr, c), lambda i: (i, 0)), slot, slot] + keep,
        out_specs=[slot] * 4, out_shape=[S((nl, r, c), F32)] * 4,
        input_output_aliases={} if prev is None else {4 + i: i for i in range(4)},
        compiler_params=_cp("arbitrary"),
    )(w, g, m, v, *(prev or ()))


def _blob(arrays):
    flat = jnp.concatenate([a.reshape(-1) for a in arrays])
    rows = -(-flat.shape[0] // (SUB * LANES)) * SUB
    return jnp.pad(flat, (0, rows * LANES - flat.shape[0])).reshape(rows, LANES)


def _unblob(blob, shapes, lead=()):
    flat = blob.reshape(lead + (-1,))
    out, off = [], 0
    for s in shapes:
        size = math.prod(s)
        out.append(flat[..., off:off + size].reshape(lead + tuple(s)))
        off += size
    return out


def _y_rows(w):
    return jnp.concatenate([w[0:A_W], w[A_W + B_W:], w[A_W:A_W + B_W]], axis=0)


def _y_rows_back(g):
    return jnp.concatenate([g[0:A_W], g[A_W + C_W:], g[A_W:A_W + C_W]], axis=0)


SMALL = ("norm_w", "q_norm_w", "k_norm_w", "sinks", "b_conv_b", "b_ln_w", "b_ln_b", "b_pw_b", "c_a_log", "c_dt_bias",
         "c_onorm_w", "b_conv_w", "c_conv_w")
ORDER = ("norm_w", "w_in", "q_norm_w", "k_norm_w", "sinks", "b_conv_w", "b_conv_b", "b_ln_w", "b_ln_b", "b_pw_w", "b_pw_b",
         "c_conv_w", "c_a_log", "c_dt_bias", "c_onorm_w", "w_out")


def kernel(x, positions, norm_w, w_in, q_norm_w, k_norm_w, sinks, b_conv_w, b_conv_b, b_ln_w, b_ln_b, b_pw_w, b_pw_b, c_conv_w, c_a_log, c_dt_bias, c_onorm_w, w_out, loss_target, m_norm_w, m_w_in, m_q_norm_w, m_k_norm_w, m_sinks, m_b_conv_w, m_b_conv_b, m_b_ln_w, m_b_ln_b, m_b_pw_w, m_b_pw_b, m_c_conv_w, m_c_a_log, m_c_dt_bias, m_c_onorm_w, m_w_out, v_norm_w, v_w_in, v_q_norm_w, v_k_norm_w, v_sinks, v_b_conv_w, v_b_conv_b, v_b_ln_w, v_b_ln_b, v_b_pw_w, v_b_pw_b, v_c_conv_w, v_c_a_log, v_c_dt_bias, v_c_onorm_w, v_w_out):
    W = dict(norm_w=norm_w, w_in=w_in, q_norm_w=q_norm_w, k_norm_w=k_norm_w, sinks=sinks, b_conv_w=b_conv_w, b_conv_b=b_conv_b,
             b_ln_w=b_ln_w, b_ln_b=b_ln_b, b_pw_w=b_pw_w, b_pw_b=b_pw_b, c_conv_w=c_conv_w, c_a_log=c_a_log,
             c_dt_bias=c_dt_bias, c_onorm_w=c_onorm_w, w_out=w_out)
    M = dict(norm_w=m_norm_w, w_in=m_w_in, q_norm_w=m_q_norm_w, k_norm_w=m_k_norm_w, sinks=m_sinks, b_conv_w=m_b_conv_w,
             b_conv_b=m_b_conv_b, b_ln_w=m_b_ln_w, b_ln_b=m_b_ln_b, b_pw_w=m_b_pw_w, b_pw_b=m_b_pw_b, c_conv_w=m_c_conv_w,
             c_a_log=m_c_a_log, c_dt_bias=m_c_dt_bias, c_onorm_w=m_c_onorm_w, w_out=m_w_out)
    V = dict(norm_w=v_norm_w, w_in=v_w_in, q_norm_w=v_q_norm_w, k_norm_w=v_k_norm_w, sinks=v_sinks, b_conv_w=v_b_conv_w,
             b_conv_b=v_b_conv_b, b_ln_w=v_b_ln_w, b_ln_b=v_b_ln_b, b_pw_w=v_b_pw_w, b_pw_b=v_b_pw_b, c_conv_w=v_c_conv_w,
             c_a_log=v_c_a_log, c_dt_bias=v_c_dt_bias, c_onorm_w=v_c_onorm_w, w_out=v_w_out)
    nseq, t, d = x.shape
    n = nseq * t
    tr = min(256, t)
    tmm = min(512, n)
    tmw = min(1024, n)
    tkk = min(2048, n)
    me = _dev_index(_my_pos())
    xs = [x.reshape(n, d)]
    tgt = loss_target.reshape(n, d)
    tabs = _rope_tables(positions.reshape(n))

    win_p = _pack_cols(w_in).astype(BF16)
    wout_b = w_out.astype(BF16)
    sharded_small = (b_pw_w, b_conv_w, c_conv_w)
    g_win0, g_small = _all_gather([win_p[0], _blob(sharded_small)])
    win = [g_win0]
    later = _exchange_start("gather", [win_p[1], wout_b[0], wout_b[1]], g_small, "gather_start")
    pw_all, cw_all, ccw_all = _unblob(g_small, [a.shape for a in sharded_small], lead=(N_DEV,))
    pw_all = pw_all.transpose(1, 0, 2, 3).reshape(DEPTH, B_W, B_W).astype(BF16)
    cw_all = jnp.pad(cw_all.transpose(1, 2, 0, 3).reshape(DEPTH, B_K, B_W), ((0, 0), (0, HALO_B - B_K), (0, 0)))
    ccw_all = jnp.pad(ccw_all.transpose(1, 2, 0, 3).reshape(DEPTH, C_K, 3 * C_W), ((0, 0), (0, SUB - C_K), (0, 0)))
    qw_all, kw_all = jnp.tile(q_norm_w, (1, 2)), jnp.tile(k_norm_w, (1, 2))
    lanes6 = lambda a: jnp.zeros((DEPTH, LANES), F32).at[:, GG:GG + C_HEADS].set(a)
    alog_all, dtb_all = lanes6(c_a_log), lanes6(c_dt_bias)

    def layer_params(l):
        row = lambda a: a[l][None]
        return dict(
            nw=row(norm_w), qw=row(qw_all), kw=row(kw_all), sinks=sinks[l], cw=cw_all[l], cb=row(b_conv_b), lnw=row(b_ln_w),
            lnb=row(b_ln_b), pw=pw_all[l], pwb=row(b_pw_b), ccw=ccw_all[l], alog=row(alog_all), dtb=row(dtb_all),
            onw=row(c_onorm_w))

    saved = []
    for l in range(DEPTH):
        q = layer_params(l)
        nw = q["nw"] + later["token"][0:1, 0:1] if l == 0 else q["nw"]
        p, h = _inproj(xs[l], nw, win[l], tm=tmw)
        y, o_a, lse = _attn_fwd(p, tabs, q["qw"], q["kw"], q["sinks"], nseq)
        gates = _gdn_gates_fwd(p, q["alog"], q["dtb"], tm=tr)
        xc, qkv = _gdn_pre_fwd(p, q["ccw"], nseq, tm=tr)
        y, o_c, u, w, tinv, ss = _gdn_chunk_fwd(qkv, gates, p, y, q["onw"], nseq)
        y, hc = _conf_fwd(p, y, q["cw"], q["cb"], q["lnw"], q["lnb"], q["pw"], q["pwb"], nseq, tm=tr)
        saved.append(dict(q=q, p=p, h=h, y=y, o_a=o_a, lse=lse, gates=gates, xc=xc, qkv=qkv, o_c=o_c, u=u, w=w, tinv=tinv,
                          ss=ss, hc=hc))
        if l == 0:
            g_win1, g_wout0, g_wout1 = _exchange_wait(later, y, "gather_wait")
            win.append(g_win1)
            wout = [_y_rows(g_wout0), _y_rows(g_wout1)]
        if l + 1 < DEPTH:
            xs.append(_outproj(xs[l], y, wout[l], tm=tmw, tn=512))
        else:
            dxn, dxn_b, lsum = _outproj_loss(xs[l], y, wout[l], tgt, tm=tmw, tn=512)
    loss = lax.psum(jnp.sum(lsum) * (0.5 / d), ("x", "y", "c"))

    sent, smalls = [None] * DEPTH, [None] * DEPTH
    for l in reversed(range(DEPTH)):
        s = saved[l]
        q, p = s["q"], s["p"]
        dy = _matmul(dxn_b, wout[l], "nt", F32, tmw, 512, d, "outproj_bwd_dy")
        dwout = _y_rows_back(_matmul(s["y"], dxn_b, "tn", BF16, 1024, 1024, tkk, "outproj_bwd_dw"))
        dp, dkv, dqw, dkw, dsk = _attn_bwd(p, dy, s["o_a"], s["lse"], tabs, q["qw"], q["kw"], q["sinks"], nseq)
        dp, dqkv, dgate, donw = _gdn_chunk_bwd(s["qkv"], s["gates"], p, dy, dp, q["onw"], s["o_c"], s["u"], s["w"],
                                               s["tinv"], s["ss"], nseq)
        dp, dccw = _gdn_pre_bwd(p, dqkv, s["xc"], dp, q["ccw"], nseq, tm=tr)
        early = P_K // 768
        dwin_a = _matmul(s["h"], dp, "tn", BF16, 1024, 768, tkk, "inproj_bwd_dw_a", b_cols=(0, early))
        sent_a = _exchange_start("scatter", [dwin_a, dwout], donw, "scatter_start_a%d" % l)
        dp = _put_cols(dp, dkv, P_K, sent_a["token"], tm=tmm)
        dp, dal, ddb = _gdn_gates_bwd(dgate, p, q["alog"], q["dtb"], dp, tm=tr)
        dp, dhc, dpw, dpwb, dlnw, dlnb, dcb = _conf_bwd1(p, dy, dp, s["hc"], q["lnw"], q["lnb"], q["pw"], q["pwb"], tm=tr)
        dp, dcw = _conf_bwd2(p, dhc, dp, q["cw"], nseq, tm=tr)
        dwin_b = _matmul(s["h"], dp, "tn", BF16, 1024, 768, tkk, "inproj_bwd_dw_b", b_cols=(early, P_W // 768 - early))
        sent_b = _exchange_start("scatter", [dwin_b, dpw], dpwb, "scatter_start_b%d" % l)
        sent[l] = (sent_a, sent_b)
        dxn, dxn_b, dnw = _inproj_bwd_dx(dp, win[l], xs[l], q["nw"] + sent_b["token"][0:1, 0:1], dxn, tm=tmm)
        halves = lambda a: a.sum(0)[:A_DH] + a.sum(0)[A_DH:]
        smalls[l] = dict(
            norm_w=dnw.sum(0), q_norm_w=halves(dqw), k_norm_w=halves(dkw), sinks=dsk.sum(0)[:A_HEADS], b_conv_b=dcb.sum(0),
            b_ln_w=dlnw.sum(0), b_ln_b=dlnb.sum(0), b_pw_b=dpwb.sum(0), c_a_log=dal.sum(0)[GG:GG + C_HEADS],
            c_dt_bias=ddb.sum(0)[GG:GG + C_HEADS], c_onorm_w=donw.sum(0),
            b_conv_w=dcw.reshape(B_K, SUB, B_W).sum(1), c_conv_w=dccw.reshape(C_K, SUB, 3 * C_W).sum(1))
    grad_x = dxn.reshape(nseq, t, d)

    G, delta, new_m, new_v = {}, {}, {}, {}
    big = ("w_in", "w_out", "b_pw_w")
    stacks = {k: None for k in big}
    after = dxn
    g_t = [None] * DEPTH
    for l in reversed(range(DEPTH)):
        r_win_a, r_wout = _exchange_wait(sent[l][0], after, "scatter_wait_a%d" % l)
        r_win_b, r_pw = _exchange_wait(sent[l][1], r_wout, "scatter_wait_b%d" % l)
        g_t[l] = jnp.concatenate([_sum8_t(r_win_a), _sum8_t(r_win_b)], axis=0).reshape(P_W, -1, LANES)
        for k, r in (("w_out", r_wout), ("b_pw_w", r_pw)):
            stacks[k] = _adamw_layer(W[k], _sum8(r), M[k], V[k], l, stacks[k])
        after = stacks["w_out"][1]
    g_in = jnp.stack(g_t, axis=2).reshape(-1, LANES)
    g_in = _unpack_cols(g_in, axis=0, each=g_in.shape[0] // P_W)
    rows = _adamw_rows(_rows_view(w_in), g_in, _rows_view(m_w_in), _rows_view(v_w_in))
    stacks["w_in"] = [_rows_view_back(a, w_in.shape) for a in (g_in, *rows)]
    for k in big:
        G[k], delta[k], new_m[k], new_v[k] = stacks[k]
    part = _blob([jnp.stack([smalls[l][k] for l in range(DEPTH)]) for k in SMALL])
    (tot,) = _all_gather([part], after=rows[0])
    tot = _sum8(tot.reshape(N_DEV, part.shape[0], LANES))
    full_shapes = [(DEPTH,) + smalls[0][k].shape for k in SMALL]
    for k, g in zip(SMALL, _unblob(tot, full_shapes)):
        G[k] = g
    G["b_conv_w"] = lax.dynamic_slice_in_dim(G["b_conv_w"], me * (B_W // N_DEV), B_W // N_DEV, axis=2)
    G["c_conv_w"] = lax.dynamic_slice_in_dim(G["c_conv_w"], me * (3 * C_W // N_DEV), 3 * C_W // N_DEV, axis=2)
    dl, mo, vo = _adamw(*[_blob([src[k] for k in SMALL]) for src in (W, G, M, V)])
    shapes = [W[k].shape for k in SMALL]
    for k, a, b, c in zip(SMALL, _unblob(dl, shapes), _unblob(mo, shapes), _unblob(vo, shapes)):
        delta[k], new_m[k], new_v[k] = a, b, c
    return (loss, grad_x, *[G[k] for k in ORDER], *[delta[k] for k in ORDER], *[new_m[k] for k in ORDER],
            *[new_v[k] for k in ORDER])
```
